```python
import jax, jax.numpy as jnp
from jax import lax
import numpy as np

D_MODEL = 1024
BATCH = 8
SEQ = 8192
DEPTH = 2

CHUNK = 64
NORM_EPS = 1e-6
L2_EPS = 1e-6
CONV_WIDTH = D_MODEL // 2
CONV_K = 3
HGRN_HEADS = 4
HGRN_DK = 128
HGRN_DV = 128
HGRN_WIDTH = HGRN_HEADS * HGRN_DK
GDN_QK_HEADS = 4
GDN_V_HEADS = 8
GDN_DK = 128
GDN_DV = 128
GDN_CONV_K = 4
GDN_QK_WIDTH = GDN_QK_HEADS * GDN_DK
GDN_V_WIDTH = GDN_V_HEADS * GDN_DV
N_BRANCHES = 3
IN_SPLITS = (CONV_WIDTH, CONV_WIDTH, CONV_WIDTH, CONV_WIDTH,
             HGRN_WIDTH, HGRN_WIDTH, HGRN_HEADS * HGRN_DV, HGRN_HEADS * HGRN_DV,
             GDN_QK_WIDTH, GDN_QK_WIDTH, GDN_V_WIDTH, GDN_V_HEADS, GDN_V_HEADS, GDN_V_WIDTH,
             N_BRANCHES * D_MODEL)
IN_COLS = sum(IN_SPLITS)
MIN_F = 1e-30

kernel_name = "hybrid_conv_hgrn2_gdn_gated_merge"


def rmsnorm(x, w):
    xf = x.astype(jnp.float32)
    y = xf * lax.rsqrt(jnp.mean(xf * xf, axis=-1, keepdims=True) + NORM_EPS)
    return (y * w.astype(jnp.float32)).astype(x.dtype)


def head_rmsnorm(o, w):
    return o * lax.rsqrt(jnp.mean(o * o, axis=-1, keepdims=True) + NORM_EPS) * w.astype(jnp.float32)


def l2norm(x):
    return x * lax.rsqrt(jnp.sum(x * x, axis=-1, keepdims=True) + L2_EPS)


def masked_exp(diff, mask):
    return jnp.where(mask, jnp.exp(jnp.where(mask, diff, 0.0)), 0.0)


def causal_dwconv(x, w):
    k = w.shape[0]
    return lax.conv_general_dilated(
        x, w[:, None, :].astype(x.dtype), window_strides=(1,), padding=((k - 1, 0),),
        dimension_numbers=('NWC', 'WIO', 'NWC'), feature_group_count=x.shape[-1])


def to_chunks(x, heads):
    b, s, hd = x.shape
    return x.reshape(b, s // CHUNK, CHUNK, heads, hd // heads).transpose(0, 3, 1, 2, 4)


def scalar_chunks(x):
    b, s, h = x.shape
    return x.reshape(b, s // CHUNK, CHUNK, h).transpose(0, 3, 1, 2)


def from_chunks(o):
    b, h, n, l, d = o.shape
    return o.transpose(0, 2, 3, 1, 4).reshape(b, n * l, h, d)


def hgrn2_chunked(q, k, v, log_f):
    g = jnp.cumsum(log_f, axis=-2)
    g_last = g[..., -1:, :]
    q_inter = q * jnp.exp(g)
    k_state = k * jnp.exp(g_last - g)
    causal = jnp.tril(jnp.ones((CHUNK, CHUNK), dtype=bool))[:, :, None]
    mv = lambda a: jnp.moveaxis(a, 2, 0)

    def step(state, inp):
        qc, kc, vc, gc, qi, ks, gl = inp
        diff = gc[..., :, None, :] - gc[..., None, :, :]
        decay = masked_exp(diff, causal)
        scores = jnp.einsum('bhtk,bhtsk,bhsk->bhts', qc, decay, kc)
        o = jnp.einsum('bhts,bhsv->bhtv', scores, vc) + jnp.einsum('bhtk,bhkv->bhtv', qi, state)
        state = jnp.exp(gl)[..., 0, :, None] * state + jnp.einsum('bhsk,bhsv->bhkv', ks, vc)
        return state, o

    b, h = q.shape[0], q.shape[1]
    s0 = jnp.zeros((b, h, q.shape[-1], v.shape[-1]), jnp.float32)
    _, o = lax.scan(step, s0, (mv(q), mv(k), mv(v), mv(g), mv(q_inter), mv(k_state), mv(g_last)))
    return jnp.moveaxis(o, 0, 2)


def gated_delta_chunked(q, k, v, log_a, beta):
    g = jnp.cumsum(log_a, axis=-1)
    g_last = g[..., -1]
    diff = g[..., :, None] - g[..., None, :]
    strict = jnp.tril(jnp.ones((CHUNK, CHUNK), dtype=bool), -1)
    causal = jnp.tril(jnp.ones((CHUNK, CHUNK), dtype=bool))
    decay_strict = masked_exp(diff, strict)
    decay_causal = masked_exp(diff, causal)
    kb = k * beta[..., None]
    m = jnp.einsum('bhntk,bhnsk->bhnts', k, kb) * decay_strict
    eye = jnp.eye(CHUNK, dtype=jnp.float32)
    rhs = jnp.concatenate([v, k * jnp.exp(g)[..., None]], axis=-1)
    sol = lax.linalg.triangular_solve(eye + m, rhs, left_side=True, lower=True, unit_diagonal=True)
    u, w = sol[..., :v.shape[-1]], sol[..., v.shape[-1]:]
    a_qk = jnp.einsum('bhntk,bhnsk->bhnts', q, kb) * decay_causal
    q_inter = q * jnp.exp(g)[..., None]
    k_state = kb * jnp.exp(g_last[..., None] - g)[..., None]
    mv = lambda a: jnp.moveaxis(a, 2, 0)

    def step(state, inp):
        uc, wc, aqk, qi, ks, gl = inp
        e = uc - jnp.einsum('bhlk,bhkv->bhlv', wc, state)
        o = jnp.einsum('bhlk,bhkv->bhlv', qi, state) + jnp.einsum('bhts,bhsv->bhtv', aqk, e)
        state = jnp.exp(gl)[..., None, None] * state + jnp.einsum('bhlk,bhlv->bhkv', ks, e)
        return state, o

    b, h = q.shape[0], q.shape[1]
    s0 = jnp.zeros((b, h, q.shape[-1], v.shape[-1]), jnp.float32)
    _, o = lax.scan(step, s0, (mv(u), mv(w), mv(a_qk), mv(q_inter), mv(k_state), mv(g_last)))
    return jnp.moveaxis(o, 0, 2)


def _fwd_setup_inputs(seed: int = 0) -> dict:
    key = jax.random.key(seed)
    ks = jax.random.split(key, 16)
    f32 = jnp.float32
    nrm = lambda k, shape, scale: jax.random.normal(k, shape, f32) * scale
    dt = jnp.exp(jax.random.uniform(ks[7], (DEPTH, GDN_V_HEADS), f32, np.log(1e-3), np.log(1e-1)))
    return {
        "x": nrm(ks[0], (BATCH, SEQ, D_MODEL), 1.0),
        "norm_w": 1.0 + nrm(ks[1], (DEPTH, D_MODEL), 0.02),
        "w_in": nrm(ks[2], (DEPTH, D_MODEL, IN_COLS), D_MODEL ** -0.5),
        "b_gate": nrm(ks[3], (DEPTH, N_BRANCHES * D_MODEL), 0.1),
        "conv_a": nrm(ks[4], (DEPTH, CONV_K, CONV_WIDTH), CONV_K ** -0.5),
        "conv_c": nrm(ks[5], (DEPTH, GDN_CONV_K, 2 * GDN_QK_WIDTH + GDN_V_WIDTH), GDN_CONV_K ** -0.5),
        "a_log": jnp.log(jax.random.uniform(ks[6], (DEPTH, GDN_V_HEADS), f32, 1.0, 16.0)),
        "dt_bias": dt + jnp.log(-jnp.expm1(-dt)),
        "lower_bounds": nrm(ks[8], (DEPTH, HGRN_WIDTH), 0.5),
        "hgrn_norm_w": 1.0 + nrm(ks[9], (DEPTH, HGRN_DV), 0.02),
        "gdn_norm_w": 1.0 + nrm(ks[10], (DEPTH, GDN_DV), 0.02),
        "w_out_a": nrm(ks[11], (DEPTH, CONV_WIDTH, D_MODEL), CONV_WIDTH ** -0.5),
        "w_out_b": nrm(ks[12], (DEPTH, HGRN_HEADS * HGRN_DV, D_MODEL), (HGRN_HEADS * HGRN_DV) ** -0.5),
        "w_out_c": nrm(ks[13], (DEPTH, GDN_V_WIDTH, D_MODEL), GDN_V_WIDTH ** -0.5),
        "w_o": nrm(ks[14], (DEPTH, D_MODEL, D_MODEL), D_MODEL ** -0.5),
        "final_norm_w": 1.0 + nrm(ks[15], (D_MODEL,), 0.02),
    }


def _fwd_reference(x, norm_w, w_in, b_gate, conv_a, conv_c, a_log, dt_bias, lower_bounds,
              hgrn_norm_w, gdn_norm_w, w_out_a, w_out_b, w_out_c, w_o, final_norm_w):
    f32 = jnp.float32
    bsz, seq = x.shape[0], x.shape[1]
    split_idx = tuple(int(s) for s in np.cumsum(IN_SPLITS)[:-1])
    p = jax.nn.softmax(lower_bounds.astype(f32), axis=0)
    lbs = jnp.cumsum(p, axis=0) - p[0]

    for l in range(DEPTH):
        h = rmsnorm(x, norm_w[l])
        proj = jnp.einsum('bsd,dc->bsc', h, w_in[l])
        (a_b, a_c, a_x, a_z, b_q, b_f, b_i, b_z,
         c_q, c_k, c_v, c_beta, c_a, c_z, g_all) = jnp.split(proj, split_idx, axis=-1)
        gates = jax.nn.sigmoid(g_all + b_gate[l]).reshape(bsz, seq, N_BRANCHES, D_MODEL)

        y_a = a_b * causal_dwconv(a_c * a_x, conv_a[l]) * jax.nn.silu(a_z)
        y_a = jnp.einsum('bsc,cd->bsd', y_a, w_out_a[l])

        lb = lbs[l].reshape(1, HGRN_HEADS, 1, 1, HGRN_DK)
        qb = to_chunks(jax.nn.silu(b_q.astype(f32)) * HGRN_DK ** -0.5, HGRN_HEADS)
        fz = to_chunks(b_f.astype(f32), HGRN_HEADS)
        f_gate = lb + (1.0 - lb) * jax.nn.sigmoid(fz)
        log_f = jnp.log(jnp.maximum(f_gate, MIN_F))
        kb = 1.0 - f_gate
        vb = to_chunks(b_i.astype(f32), HGRN_HEADS)
        o_b = from_chunks(hgrn2_chunked(qb, kb, vb, log_f))
        y_b = head_rmsnorm(o_b, hgrn_norm_w[l]).reshape(bsz, seq, -1) * jax.nn.silu(b_z.astype(f32))
        y_b = jnp.einsum('bsc,cd->bsd', y_b.astype(x.dtype), w_out_b[l])

        qkv = jax.nn.silu(causal_dwconv(jnp.concatenate([c_q, c_k, c_v], axis=-1), conv_c[l])).astype(f32)
        qc, kc, vc = jnp.split(qkv, (GDN_QK_WIDTH, 2 * GDN_QK_WIDTH), axis=-1)
        rep = GDN_V_HEADS // GDN_QK_HEADS
        qc = jnp.repeat(l2norm(qc.reshape(bsz, seq, GDN_QK_HEADS, GDN_DK)), rep, axis=2) * GDN_DK ** -0.5
        kc = jnp.repeat(l2norm(kc.reshape(bsz, seq, GDN_QK_HEADS, GDN_DK)), rep, axis=2)
        beta = jax.nn.sigmoid(c_beta.astype(f32))
        log_a = -jnp.exp(a_log[l].astype(f32)) * jax.nn.softplus(c_a.astype(f32) + dt_bias[l].astype(f32))
        o_c = gated_delta_chunked(
            to_chunks(qc.reshape(bsz, seq, -1), GDN_V_HEADS), to_chunks(kc.reshape(bsz, seq, -1), GDN_V_HEADS),
            to_chunks(vc, GDN_V_HEADS), scalar_chunks(log_a), scalar_chunks(beta))
        y_c = head_rmsnorm(from_chunks(o_c), gdn_norm_w[l]).reshape(bsz, seq, -1) * jax.nn.silu(c_z.astype(f32))
        y_c = jnp.einsum('bsc,cd->bsd', y_c.astype(x.dtype), w_out_c[l])

        merged = gates[:, :, 0] * y_a + gates[:, :, 1] * y_b + gates[:, :, 2] * y_c
        x = x + jnp.einsum('bsd,de->bse', merged, w_o[l])

    return rmsnorm(x, final_norm_w)


import jax as _jax
import jax.numpy as _jnp

TWIN_FORMAT = 'train_step'
FWD_PARAMS = ['x', 'norm_w', 'w_in', 'b_gate', 'conv_a', 'conv_c', 'a_log', 'dt_bias', 'lower_bounds', 'hgrn_norm_w', 'gdn_norm_w', 'w_out_a', 'w_out_b', 'w_out_c', 'w_o', 'final_norm_w']
TWIN_WEIGHTS = ['norm_w', 'w_in', 'b_gate', 'conv_a', 'conv_c', 'a_log', 'dt_bias', 'lower_bounds', 'hgrn_norm_w', 'gdn_norm_w', 'w_out_a', 'w_out_b', 'w_out_c', 'w_o', 'final_norm_w']
TWIN_DIFF_INPUT = 'x'
TWIN_INPUTS = ['x', 'norm_w', 'w_in', 'b_gate', 'conv_a', 'conv_c', 'a_log', 'dt_bias', 'lower_bounds', 'hgrn_norm_w', 'gdn_norm_w', 'w_out_a', 'w_out_b', 'w_out_c', 'w_o', 'final_norm_w', 'loss_target', 'm_norm_w', 'm_w_in', 'm_b_gate', 'm_conv_a', 'm_conv_c', 'm_a_log', 'm_dt_bias', 'm_lower_bounds', 'm_hgrn_norm_w', 'm_gdn_norm_w', 'm_w_out_a', 'm_w_out_b', 'm_w_out_c', 'm_w_o', 'm_final_norm_w', 'v_norm_w', 'v_w_in', 'v_b_gate', 'v_conv_a', 'v_conv_c', 'v_a_log', 'v_dt_bias', 'v_lower_bounds', 'v_hgrn_norm_w', 'v_gdn_norm_w', 'v_w_out_a', 'v_w_out_b', 'v_w_out_c', 'v_w_o', 'v_final_norm_w']
TWIN_OUTPUTS = ['loss', 'grad_x', 'grad_norm_w', 'grad_w_in', 'grad_b_gate', 'grad_conv_a', 'grad_conv_c', 'grad_a_log', 'grad_dt_bias', 'grad_lower_bounds', 'grad_hgrn_norm_w', 'grad_gdn_norm_w', 'grad_w_out_a', 'grad_w_out_b', 'grad_w_out_c', 'grad_w_o', 'grad_final_norm_w', 'delta_norm_w', 'delta_w_in', 'delta_b_gate', 'delta_conv_a', 'delta_conv_c', 'delta_a_log', 'delta_dt_bias', 'delta_lower_bounds', 'delta_hgrn_norm_w', 'delta_gdn_norm_w', 'delta_w_out_a', 'delta_w_out_b', 'delta_w_out_c', 'delta_w_o', 'delta_final_norm_w', 'new_m_norm_w', 'new_m_w_in', 'new_m_b_gate', 'new_m_conv_a', 'new_m_conv_c', 'new_m_a_log', 'new_m_dt_bias', 'new_m_lower_bounds', 'new_m_hgrn_norm_w', 'new_m_gdn_norm_w', 'new_m_w_out_a', 'new_m_w_out_b', 'new_m_w_out_c', 'new_m_w_o', 'new_m_final_norm_w', 'new_v_norm_w', 'new_v_w_in', 'new_v_b_gate', 'new_v_conv_a', 'new_v_conv_c', 'new_v_a_log', 'new_v_dt_bias', 'new_v_lower_bounds', 'new_v_hgrn_norm_w', 'new_v_gdn_norm_w', 'new_v_w_out_a', 'new_v_w_out_b', 'new_v_w_out_c', 'new_v_w_o', 'new_v_final_norm_w']
TWIN_LEAF_KINDS = {'loss': 'loss', 'grad_x': 'grad_x', 'grad_norm_w': 'grad_w', 'grad_w_in': 'grad_w', 'grad_b_gate': 'grad_w', 'grad_conv_a': 'grad_w', 'grad_conv_c': 'grad_w', 'grad_a_log': 'grad_w', 'grad_dt_bias': 'grad_w', 'grad_lower_bounds': 'grad_w', 'grad_hgrn_norm_w': 'grad_w', 'grad_gdn_norm_w': 'grad_w', 'grad_w_out_a': 'grad_w', 'grad_w_out_b': 'grad_w', 'grad_w_out_c': 'grad_w', 'grad_w_o': 'grad_w', 'grad_final_norm_w': 'grad_w', 'delta_norm_w': 'delta_w', 'delta_w_in': 'delta_w', 'delta_b_gate': 'delta_w', 'delta_conv_a': 'delta_w', 'delta_conv_c': 'delta_w', 'delta_a_log': 'delta_w', 'delta_dt_bias': 'delta_w', 'delta_lower_bounds': 'delta_w', 'delta_hgrn_norm_w': 'delta_w', 'delta_gdn_norm_w': 'delta_w', 'delta_w_out_a': 'delta_w', 'delta_w_out_b': 'delta_w', 'delta_w_out_c': 'delta_w', 'delta_w_o': 'delta_w', 'delta_final_norm_w': 'delta_w', 'new_m_norm_w': 'new_m', 'new_m_w_in': 'new_m', 'new_m_b_gate': 'new_m', 'new_m_conv_a': 'new_m', 'new_m_conv_c': 'new_m', 'new_m_a_log': 'new_m', 'new_m_dt_bias': 'new_m', 'new_m_lower_bounds': 'new_m', 'new_m_hgrn_norm_w': 'new_m', 'new_m_gdn_norm_w': 'new_m', 'new_m_w_out_a': 'new_m', 'new_m_w_out_b': 'new_m', 'new_m_w_out_c': 'new_m', 'new_m_w_o': 'new_m', 'new_m_final_norm_w': 'new_m', 'new_v_norm_w': 'new_v', 'new_v_w_in': 'new_v', 'new_v_b_gate': 'new_v', 'new_v_conv_a': 'new_v', 'new_v_conv_c': 'new_v', 'new_v_a_log': 'new_v', 'new_v_dt_bias': 'new_v', 'new_v_lower_bounds': 'new_v', 'new_v_hgrn_norm_w': 'new_v', 'new_v_gdn_norm_w': 'new_v', 'new_v_w_out_a': 'new_v', 'new_v_w_out_b': 'new_v', 'new_v_w_out_c': 'new_v', 'new_v_w_o': 'new_v', 'new_v_final_norm_w': 'new_v'}


def _forward(args):
    return _fwd_reference(*[args[k] for k in FWD_PARAMS])


def _output_shape():
    def fwd():
        inp = _fwd_setup_inputs(0)
        return _fwd_reference(*[inp[k] for k in FWD_PARAMS])
    out = _jax.eval_shape(fwd)
    return out.shape, out.dtype

N_MICROBATCH = 1
ADAM_LR = 0.001
ADAM_B1 = 0.9
ADAM_B2 = 0.999
ADAM_EPS = 1e-08
ADAM_WD = 0.01
ADAM_STEP = 10
PER_EXAMPLE_BATCH_AXIS = {'x': 0, 'loss_target': 0}
SHARED_INPUTS = []
_WEIGHT_DTYPES = {'norm_w': _jnp.float32, 'w_in': _jnp.float32, 'b_gate': _jnp.float32, 'conv_a': _jnp.float32, 'conv_c': _jnp.float32, 'a_log': _jnp.float32, 'dt_bias': _jnp.float32, 'lower_bounds': _jnp.float32, 'hgrn_norm_w': _jnp.float32, 'gdn_norm_w': _jnp.float32, 'w_out_a': _jnp.float32, 'w_out_b': _jnp.float32, 'w_out_c': _jnp.float32, 'w_o': _jnp.float32, 'final_norm_w': _jnp.float32}
MOMENT_SCALE = {'norm_w': 2.469837e-01, 'w_in': 7.265338e-02, 'b_gate': 2.890276e-02, 'conv_a': 1.107782e-01, 'conv_c': 7.005755e-02, 'a_log': 2.525346e-01, 'dt_bias': 2.450193e-01, 'lower_bounds': 8.306158e-03, 'hgrn_norm_w': 2.328576e-01, 'gdn_norm_w': 2.139783e-01, 'w_out_a': 7.443412e-02, 'w_out_b': 7.512135e-02, 'w_out_c': 7.470992e-02, 'w_o': 1.297881e-01, 'final_norm_w': 6.397428e+01}


def _to_microbatches(a, axis):
    t = _jnp.moveaxis(a, axis, 0)
    t = t.reshape((N_MICROBATCH, t.shape[0] // N_MICROBATCH) + t.shape[1:])
    return _jnp.moveaxis(t, 1, axis + 1)


def setup_inputs(seed: int = 0) -> dict:
    inp = _fwd_setup_inputs(seed)
    key = _jax.random.fold_in(_jax.random.key(seed), 7919)
    shape, _ = _output_shape()
    out = dict(inp)
    out["loss_target"] = _jax.random.normal(_jax.random.fold_in(key, 0), shape, _jnp.float32)
    for i, name in enumerate(TWIN_WEIGHTS):
        w = inp[name].astype(_jnp.float32)
        if MOMENT_SCALE is None:
            s = _jnp.sqrt(_jnp.mean(_jnp.square(w)) + 1e-30)
        else:
            s = MOMENT_SCALE[name]
        km, kv = _jax.random.split(_jax.random.fold_in(key, i + 1))
        out[name] = w
        out["m_" + name] = s * _jax.random.normal(km, w.shape, _jnp.float32)
        out["v_" + name] = (s * s) * _jax.random.uniform(kv, w.shape, _jnp.float32, 0.5, 1.5)
    if N_MICROBATCH > 1:
        for name, axis in PER_EXAMPLE_BATCH_AXIS.items():
            out[name] = _to_microbatches(out[name], axis)
    return {'x': out['x'], 'norm_w': out['norm_w'], 'w_in': out['w_in'], 'b_gate': out['b_gate'], 'conv_a': out['conv_a'], 'conv_c': out['conv_c'], 'a_log': out['a_log'], 'dt_bias': out['dt_bias'], 'lower_bounds': out['lower_bounds'], 'hgrn_norm_w': out['hgrn_norm_w'], 'gdn_norm_w': out['gdn_norm_w'], 'w_out_a': out['w_out_a'], 'w_out_b': out['w_out_b'], 'w_out_c': out['w_out_c'], 'w_o': out['w_o'], 'final_norm_w': out['final_norm_w'], 'loss_target': out['loss_target'], 'm_norm_w': out['m_norm_w'], 'm_w_in': out['m_w_in'], 'm_b_gate': out['m_b_gate'], 'm_conv_a': out['m_conv_a'], 'm_conv_c': out['m_conv_c'], 'm_a_log': out['m_a_log'], 'm_dt_bias': out['m_dt_bias'], 'm_lower_bounds': out['m_lower_bounds'], 'm_hgrn_norm_w': out['m_hgrn_norm_w'], 'm_gdn_norm_w': out['m_gdn_norm_w'], 'm_w_out_a': out['m_w_out_a'], 'm_w_out_b': out['m_w_out_b'], 'm_w_out_c': out['m_w_out_c'], 'm_w_o': out['m_w_o'], 'm_final_norm_w': out['m_final_norm_w'], 'v_norm_w': out['v_norm_w'], 'v_w_in': out['v_w_in'], 'v_b_gate': out['v_b_gate'], 'v_conv_a': out['v_conv_a'], 'v_conv_c': out['v_conv_c'], 'v_a_log': out['v_a_log'], 'v_dt_bias': out['v_dt_bias'], 'v_lower_bounds': out['v_lower_bounds'], 'v_hgrn_norm_w': out['v_hgrn_norm_w'], 'v_gdn_norm_w': out['v_gdn_norm_w'], 'v_w_out_a': out['v_w_out_a'], 'v_w_out_b': out['v_w_out_b'], 'v_w_out_c': out['v_w_out_c'], 'v_w_o': out['v_w_o'], 'v_final_norm_w': out['v_final_norm_w']}


def _loss(weights, diff, rest, loss_target):
    with _jax.named_scope("forward"):
        args = {**rest, TWIN_DIFF_INPUT: diff, **{k: w.astype(_WEIGHT_DTYPES[k]) for k, w in weights.items()}}
        y = _forward(args)
    with _jax.named_scope("loss_head"):
        err = _jnp.square(y.astype(_jnp.float32) - loss_target)
        return 0.5 * _jnp.sum(_jnp.mean(err, axis=-1)) if err.ndim else 0.5 * err


def _adamw(w, g, m, v):
    m = ADAM_B1 * m + (1.0 - ADAM_B1) * g
    v = ADAM_B2 * v + (1.0 - ADAM_B2) * _jnp.square(g)
    m_hat = m / (1.0 - ADAM_B1 ** ADAM_STEP)
    v_hat = v / (1.0 - ADAM_B2 ** ADAM_STEP)
    delta = -ADAM_LR * (m_hat / (_jnp.sqrt(v_hat) + ADAM_EPS) + ADAM_WD * w)
    return delta, m, v


def reference(x, norm_w, w_in, b_gate, conv_a, conv_c, a_log, dt_bias, lower_bounds, hgrn_norm_w, gdn_norm_w, w_out_a, w_out_b, w_out_c, w_o, final_norm_w, loss_target, m_norm_w, m_w_in, m_b_gate, m_conv_a, m_conv_c, m_a_log, m_dt_bias, m_lower_bounds, m_hgrn_norm_w, m_gdn_norm_w, m_w_out_a, m_w_out_b, m_w_out_c, m_w_o, m_final_norm_w, v_norm_w, v_w_in, v_b_gate, v_conv_a, v_conv_c, v_a_log, v_dt_bias, v_lower_bounds, v_hgrn_norm_w, v_gdn_norm_w, v_w_out_a, v_w_out_b, v_w_out_c, v_w_o, v_final_norm_w):
    given = dict(x=x, norm_w=norm_w, w_in=w_in, b_gate=b_gate, conv_a=conv_a, conv_c=conv_c, a_log=a_log, dt_bias=dt_bias, lower_bounds=lower_bounds, hgrn_norm_w=hgrn_norm_w, gdn_norm_w=gdn_norm_w, w_out_a=w_out_a, w_out_b=w_out_b, w_out_c=w_out_c, w_o=w_o, final_norm_w=final_norm_w, loss_target=loss_target, m_norm_w=m_norm_w, m_w_in=m_w_in, m_b_gate=m_b_gate, m_conv_a=m_conv_a, m_conv_c=m_conv_c, m_a_log=m_a_log, m_dt_bias=m_dt_bias, m_lower_bounds=m_lower_bounds, m_hgrn_norm_w=m_hgrn_norm_w, m_gdn_norm_w=m_gdn_norm_w, m_w_out_a=m_w_out_a, m_w_out_b=m_w_out_b, m_w_out_c=m_w_out_c, m_w_o=m_w_o, m_final_norm_w=m_final_norm_w, v_norm_w=v_norm_w, v_w_in=v_w_in, v_b_gate=v_b_gate, v_conv_a=v_conv_a, v_conv_c=v_conv_c, v_a_log=v_a_log, v_dt_bias=v_dt_bias, v_lower_bounds=v_lower_bounds, v_hgrn_norm_w=v_hgrn_norm_w, v_gdn_norm_w=v_gdn_norm_w, v_w_out_a=v_w_out_a, v_w_out_b=v_w_out_b, v_w_out_c=v_w_out_c, v_w_o=v_w_o, v_final_norm_w=v_final_norm_w)
    weights = {n: given[n] for n in TWIN_WEIGHTS}
    shared = {n: given[n] for n in SHARED_INPUTS}
    per_example = {n: given[n] for n in ['x']}
    grad_fn = _jax.value_and_grad(_loss, argnums=(0, 1))

    def one_microbatch(ex, loss_target):
        ex = dict(ex)
        diff = ex.pop(TWIN_DIFF_INPUT)
        return grad_fn(weights, diff, {**shared, **ex}, loss_target)

    if N_MICROBATCH == 1:
        loss, (grad_w, grad_x) = one_microbatch(per_example, given["loss_target"])
    else:
        def body(carry, xs):
            loss_sum, grad_sum = carry
            l_k, (gw_k, gx_k) = one_microbatch(xs[0], xs[1])
            with _jax.named_scope("update"):
                return (loss_sum + l_k, _jax.tree.map(_jnp.add, grad_sum, gw_k)), gx_k

        init = (_jnp.zeros((), _jnp.float32), _jax.tree.map(_jnp.zeros_like, weights))
        (loss, grad_w), grad_x = _jax.lax.scan(body, init, (per_example, given["loss_target"]))
    with _jax.named_scope("update"):
        delta_w, new_m, new_v = {}, {}, {}
        for n in TWIN_WEIGHTS:
            delta_w[n], new_m[n], new_v[n] = _adamw(weights[n], grad_w[n], given["m_" + n], given["v_" + n])
    return (loss, grad_x, *[grad_w[n] for n in TWIN_WEIGHTS], *[delta_w[n] for n in TWIN_WEIGHTS],
            *[new_m[n] for n in TWIN_WEIGHTS], *[new_v[n] for n in TWIN_WEIGHTS])
```

```python
import functools

import jax
import jax.numpy as jnp
from jax import lax
from jax.experimental import pallas as pl
from jax.experimental.pallas import tpu as pltpu

f32 = jnp.float32
bf16 = jnp.bfloat16

D = 1024
L = 64
SUB = 16
NORM_EPS = 1e-6
L2_EPS = 1e-6
MIN_F = 1e-30
HK = 128
QK_SCALE = HK ** -0.5
N_GDN = 8
N_HGRN = 4

OFF_A, OFF_B, OFF_CQKV, OFF_G, OFF_CZ, OFF_S = 0, 2048, 4096, 6144, 9216, 10240
NP = 10368
N_ORIG = 10256

ADAM_LR, ADAM_B1, ADAM_B2, ADAM_EPS, ADAM_WD, ADAM_STEP = 0.001, 0.9, 0.999, 1e-08, 0.01, 10

VMEM_LIMIT = 56 * 1024 * 1024


def _cp(sem):
    return pltpu.CompilerParams(dimension_semantics=sem, vmem_limit_bytes=VMEM_LIMIT)


def _sigmoid(x):
    return jax.nn.sigmoid(x)


def _silu(x):
    return x * _sigmoid(x)


def _dsilu(x):
    s = _sigmoid(x)
    return s * (1.0 + x * (1.0 - s))


def _softplus(x):
    u = jnp.exp(-jnp.abs(x))
    w = 1.0 + u
    l1p = jnp.where(w == 1.0, u, jnp.log(w) * (u / (w - 1.0)))
    return jnp.maximum(x, 0.0) + l1p


def _dot(a, b):
    return jnp.dot(a, b, preferred_element_type=f32)


def _dot_nt(a, b):
    return lax.dot_general(a, b, (((1,), (1,)), ((), ())), preferred_element_type=f32)


def _dot_tn(a, b):
    return lax.dot_general(a, b, (((0,), (0,)), ((), ())), preferred_element_type=f32)


def _bdot(a, b):
    return lax.dot_general(a, b, (((2,), (1,)), ((0,), (0,))), preferred_element_type=f32)


def _bdot_nt(a, b):
    return lax.dot_general(a, b, (((2,), (2,)), ((0,), (0,))), preferred_element_type=f32)


def _bdot_tn(a, b):
    return lax.dot_general(a, b, (((1,), (1,)), ((0,), (0,))), preferred_element_type=f32)


def _bdot_hi(a, b):
    return lax.dot_general(a, b, (((2,), (1,)), ((0,), (0,))), preferred_element_type=f32,
                           precision=lax.Precision.HIGHEST)


def _b(x):
    return x.astype(bf16)


def _chunk_cumsum(x, rows_in_chunk):
    n = x.shape[0]
    for s in (1, 2, 4, 8, 16, 32):
        x = x + jnp.where(rows_in_chunk >= s, pltpu.roll(x, s, axis=0), 0.0)
    return x


def _chunk_rev_cumsum(x, rows_in_chunk):
    n = x.shape[0]
    for s in (1, 2, 4, 8, 16, 32):
        x = x + jnp.where(rows_in_chunk + s < L, pltpu.roll(x, n - s, axis=0), 0.0)
    return x


def _shift_down(x, s):
    return pltpu.roll(x, s, axis=0) if s else x


def _shift_up(x, s):
    return pltpu.roll(x, x.shape[0] - s, axis=0) if s else x


def inproj_fwd(x, nw, w):
    T = x.shape[0]
    tT, tn = min(512, T), 1152

    def body(x_ref, nw_ref, w_ref, p_ref, h_ref, hs):
        @pl.when(pl.program_id(1) == 0)
        def _():
            xv = x_ref[...]
            r = lax.rsqrt(jnp.mean(xv * xv, axis=-1, keepdims=True) + NORM_EPS)
            hv = _b(xv * r * nw_ref[...])
            hs[...] = hv
            h_ref[...] = hv
        p_ref[...] = _dot(hs[...], w_ref[...])

    return pl.pallas_call(
        body, name="inproj_fwd", grid=(T // tT, NP // tn),
        in_specs=[pl.BlockSpec((tT, D), lambda i, j: (i, 0)), pl.BlockSpec((1, D), lambda i, j: (0, 0)),
                  pl.BlockSpec((D, tn), lambda i, j: (0, j))],
        out_specs=[pl.BlockSpec((tT, tn), lambda i, j: (i, j)), pl.BlockSpec((tT, D), lambda i, j: (i, 0))],
        out_shape=[jax.ShapeDtypeStruct((T, NP), f32), jax.ShapeDtypeStruct((T, D), bf16)],
        scratch_shapes=[pltpu.VMEM((tT, D), bf16)],
        compiler_params=_cp(("parallel", "arbitrary")),
    )(x, nw, w)


def matmul_tn(a, b, name):
    T, K = a.shape
    N = b.shape[1]
    tT = min(512, T)
    tn = 1152 if N % 1152 == 0 else min(N, 1024)

    def body(a_ref, b_ref, o_ref):
        @pl.when(pl.program_id(1) == 0)
        def _():
            o_ref[...] = jnp.zeros_like(o_ref)
        o_ref[...] += _dot_tn(_b(a_ref[...]), _b(b_ref[...]))

    return pl.pallas_call(
        body, name=name, grid=(N // tn, T // tT),
        in_specs=[pl.BlockSpec((tT, K), lambda j, t: (t, 0)), pl.BlockSpec((tT, tn), lambda j, t: (t, j))],
        out_specs=pl.BlockSpec((K, tn), lambda j, t: (0, j)),
        out_shape=jax.ShapeDtypeStruct((K, N), f32),
        compiler_params=_cp(("parallel", "arbitrary")),
    )(a, b)


def inproj_bwd(dp, w, x, nw, dres):
    T = x.shape[0]
    tT, tk = min(512, T), 1152
    nk = NP // tk

    def body(dp_ref, w_ref, x_ref, nw_ref, dres_ref, dx_ref, dnw_ref, acc):
        i, k = pl.program_id(0), pl.program_id(1)

        @pl.when((i == 0) & (k == 0))
        def _():
            dnw_ref[...] = jnp.zeros_like(dnw_ref)

        @pl.when(k == 0)
        def _():
            acc[...] = jnp.zeros_like(acc)
        acc[...] += _dot_nt(dp_ref[...], w_ref[...])

        @pl.when(k == nk - 1)
        def _():
            xv = x_ref[...]
            r = lax.rsqrt(jnp.mean(xv * xv, axis=-1, keepdims=True) + NORM_EPS)
            xh = xv * r
            dy = acc[...]
            dyw = dy * nw_ref[...]
            dx_ref[...] = r * (dyw - xh * jnp.mean(dyw * xh, axis=-1, keepdims=True)) + dres_ref[...]
            dnw_ref[...] += jnp.sum(dy * xh, axis=0, keepdims=True)

    return pl.pallas_call(
        body, name="inproj_bwd", grid=(T // tT, nk),
        in_specs=[pl.BlockSpec((tT, tk), lambda i, k: (i, k)), pl.BlockSpec((D, tk), lambda i, k: (0, k)),
                  pl.BlockSpec((tT, D), lambda i, k: (i, 0)), pl.BlockSpec((1, D), lambda i, k: (0, 0)),
                  pl.BlockSpec((tT, D), lambda i, k: (i, 0))],
        out_specs=[pl.BlockSpec((tT, D), lambda i, k: (i, 0)), pl.BlockSpec((1, D), lambda i, k: (0, 0))],
        out_shape=[jax.ShapeDtypeStruct((T, D), f32), jax.ShapeDtypeStruct((1, D), f32)],
        scratch_shapes=[pltpu.VMEM((tT, D), f32)],
        compiler_params=_cp(("arbitrary", "arbitrary")),
    )(dp, w, x, nw, dres)


def loss_head(x, fw, tgt):
    T = x.shape[0]
    tT = min(512, T)

    def body(x_ref, fw_ref, t_ref, loss_ref, dx_ref, dfw_ref):
        @pl.when(pl.program_id(0) == 0)
        def _():
            loss_ref[...] = jnp.zeros_like(loss_ref)
            dfw_ref[...] = jnp.zeros_like(dfw_ref)
        xv = x_ref[...]
        r = lax.rsqrt(jnp.mean(xv * xv, axis=-1, keepdims=True) + NORM_EPS)
        xh = xv * r
        err = xh * fw_ref[...] - t_ref[...]
        part = 0.5 * jnp.sum(jnp.mean(err * err, axis=-1, keepdims=True), axis=0, keepdims=True)
        loss_ref[...] += jnp.broadcast_to(part, loss_ref.shape)
        dy = err * (1.0 / D)
        dyw = dy * fw_ref[...]
        dx_ref[...] = r * (dyw - xh * jnp.mean(dyw * xh, axis=-1, keepdims=True))
        dfw_ref[...] += jnp.sum(dy * xh, axis=0, keepdims=True)

    return pl.pallas_call(
        body, name="loss_head", grid=(T // tT,),
        in_specs=[pl.BlockSpec((tT, D), lambda i: (i, 0)), pl.BlockSpec((1, D), lambda i: (0, 0)),
                  pl.BlockSpec((tT, D), lambda i: (i, 0))],
        out_specs=[pl.BlockSpec((1, 128), lambda i: (0, 0)), pl.BlockSpec((tT, D), lambda i: (i, 0)),
                   pl.BlockSpec((1, D), lambda i: (0, 0))],
        out_shape=[jax.ShapeDtypeStruct((1, 128), f32), jax.ShapeDtypeStruct((T, D), f32),
                   jax.ShapeDtypeStruct((1, D), f32)],
        compiler_params=_cp(("arbitrary",)),
    )(x, fw, tgt)


def _halo_specs(tT, T, width, colblk):
    nb8 = T // 8
    per = tT // 8
    prev = pl.BlockSpec((8, width), lambda i: (jnp.maximum(i * per - 1, 0), colblk))
    nxt = pl.BlockSpec((8, width), lambda i: (jnp.minimum((i + 1) * per, nb8 - 1), colblk))
    return prev, nxt


def mixa_fwd(p, cw):
    T = p.shape[0]
    tT = min(512, T)
    prev_spec, _ = _halo_specs(tT, T, 2048, OFF_A // 2048)

    def body(p_ref, pp_ref, cw_ref, y_ref):
        pv = p_ref[...]
        u = pv[:, 512:1024] * pv[:, 1024:1536]
        pp = pp_ref[...]
        up = jnp.where(pl.program_id(0) == 0, 0.0, pp[:, 512:1024] * pp[:, 1024:1536])
        ue = jnp.concatenate([up, u], axis=0)
        cv = cw_ref[0:1, :] * _shift_down(ue, 2) + cw_ref[1:2, :] * _shift_down(ue, 1) + cw_ref[2:3, :] * ue
        y_ref[...] = _b(pv[:, 0:512] * cv[8:] * _silu(pv[:, 1536:2048]))

    return pl.pallas_call(
        body, name="mixa_fwd", grid=(T // tT,),
        in_specs=[pl.BlockSpec((tT, 2048), lambda i: (i, OFF_A // 2048)), prev_spec,
                  pl.BlockSpec((3, 512), lambda i: (0, 0))],
        out_specs=pl.BlockSpec((tT, 512), lambda i: (i, 0)),
        out_shape=jax.ShapeDtypeStruct((T, 512), bf16),
        compiler_params=_cp(("parallel",)),
    )(p, p, cw)


def mixa_bwd(p, cw, dy):
    T = p.shape[0]
    tT = min(512, T)
    nt = T // tT
    prev_spec, next_spec = _halo_specs(tT, T, 2048, OFF_A // 2048)
    _, dnext_spec = _halo_specs(tT, T, 512, 0)

    def body(p_ref, pp_ref, pn_ref, cw_ref, dy_ref, dyn_ref, dp_ref, dcw_ref):
        i = pl.program_id(0)

        @pl.when(i == 0)
        def _():
            dcw_ref[...] = jnp.zeros_like(dcw_ref)
        pv, pp, pn = p_ref[...], pp_ref[...], pn_ref[...]
        pe = jnp.concatenate([pp, pv, pn], axis=0)
        rows = lax.broadcasted_iota(jnp.int32, (tT + 16, 1), 0)
        ab, ac, ax, az = pe[:, 0:512], pe[:, 512:1024], pe[:, 1024:1536], pe[:, 1536:2048]
        u = jnp.where((rows < 8) & (i == 0), 0.0, ac * ax)
        u1, u2 = _shift_down(u, 1), _shift_down(u, 2)
        w0, w1, w2 = cw_ref[0:1, :], cw_ref[1:2, :], cw_ref[2:3, :]
        cv = w0 * u2 + w1 * u1 + w2 * u
        dye = jnp.concatenate([jnp.zeros((8, 512), f32), dy_ref[...], dyn_ref[...]], axis=0)
        dye = jnp.where((rows >= tT + 8) & (i == nt - 1), 0.0, dye)
        sz = _silu(az)
        dcv = dye * ab * sz
        du = w2 * dcv + w1 * _shift_up(dcv, 1) + w0 * _shift_up(dcv, 2)
        inner = (rows >= 8) & (rows < tT + 8)
        dcv_in = jnp.where(inner, dcv, 0.0)
        dcw_ref[0:1, :] += jnp.sum(dcv_in * u2, axis=0, keepdims=True)
        dcw_ref[1:2, :] += jnp.sum(dcv_in * u1, axis=0, keepdims=True)
        dcw_ref[2:3, :] += jnp.sum(dcv_in * u, axis=0, keepdims=True)
        sl = slice(8, tT + 8)
        dp_ref[:, 0:512] = _b((dye * cv * sz)[sl])
        dp_ref[:, 512:1024] = _b((du * ax)[sl])
        dp_ref[:, 1024:1536] = _b((du * ac)[sl])
        dp_ref[:, 1536:2048] = _b((dye * ab * cv * _dsilu(az))[sl])

    return pl.pallas_call(
        body, name="mixa_bwd", grid=(nt,),
        in_specs=[pl.BlockSpec((tT, 2048), lambda i: (i, OFF_A // 2048)), prev_spec, next_spec,
                  pl.BlockSpec((3, 512), lambda i: (0, 0)),
                  pl.BlockSpec((tT, 512), lambda i: (i, 0)), dnext_spec],
        out_specs=[pl.BlockSpec((tT, 2048), lambda i: (i, 0)), pl.BlockSpec((8, 512), lambda i: (0, 0))],
        out_shape=[jax.ShapeDtypeStruct((T, 2048), bf16), jax.ShapeDtypeStruct((8, 512), f32)],
        compiler_params=_cp(("arbitrary",)),
    )(p, p, p, cw, dy, dy)


def _l2n_fwd(y):
    return y * lax.rsqrt(jnp.sum(y * y, axis=-1, keepdims=True) + L2_EPS)


def mixc_pre_fwd(p, cw, alog_l, dtb_l):
    T = p.shape[0]
    tT = min(512, T)
    prev_spec, _ = _halo_specs(tT, T, 2048, OFF_CQKV // 2048)

    def body(p_ref, pp_ref, ps_ref, cw_ref, al_ref, dt_ref, q_ref, k_ref, v_ref, sm_ref):
        pp = jnp.where(pl.program_id(0) == 0, 0.0, pp_ref[...])
        xe = jnp.concatenate([pp, p_ref[...]], axis=0)
        cv = (cw_ref[0:1, :] * _shift_down(xe, 3) + cw_ref[1:2, :] * _shift_down(xe, 2)
              + cw_ref[2:3, :] * _shift_down(xe, 1) + cw_ref[3:4, :] * xe)[8:]
        y = _silu(cv)
        for hh in range(4):
            sl = slice(hh * HK, (hh + 1) * HK)
            q_ref[:, sl] = _l2n_fwd(y[:, sl]) * QK_SCALE
            k_ref[:, sl] = _l2n_fwd(y[:, 512 + hh * HK:512 + (hh + 1) * HK])
        v_ref[...] = y[:, 1024:2048]
        ps = ps_ref[...]
        lane = lax.broadcasted_iota(jnp.int32, ps.shape, 1)
        la = -jnp.exp(al_ref[...]) * _softplus(ps + dt_ref[...])
        rin = lax.broadcasted_iota(jnp.int32, ps.shape, 0) % L
        g = _chunk_cumsum(la, rin)
        sm_ref[...] = jnp.where(lane < 8, _sigmoid(ps), jnp.where(lane < 16, g, 0.0))

    return pl.pallas_call(
        body, name="mixc_pre_fwd", grid=(T // tT,),
        in_specs=[pl.BlockSpec((tT, 2048), lambda i: (i, OFF_CQKV // 2048)), prev_spec,
                  pl.BlockSpec((tT, 128), lambda i: (i, OFF_S // 128)),
                  pl.BlockSpec((4, 2048), lambda i: (0, 0)),
                  pl.BlockSpec((1, 128), lambda i: (0, 0)), pl.BlockSpec((1, 128), lambda i: (0, 0))],
        out_specs=[pl.BlockSpec((tT, 512), lambda i: (i, 0)), pl.BlockSpec((tT, 512), lambda i: (i, 0)),
                   pl.BlockSpec((tT, 1024), lambda i: (i, 0)), pl.BlockSpec((tT, 128), lambda i: (i, 0))],
        out_shape=[jax.ShapeDtypeStruct((T, 512), f32), jax.ShapeDtypeStruct((T, 512), f32),
                   jax.ShapeDtypeStruct((T, 1024), f32), jax.ShapeDtypeStruct((T, 128), f32)],
        compiler_params=_cp(("parallel",)),
    )(p, p, p, cw, alog_l, dtb_l)


def mixc_pre_bwd(p, cw, alog_l, dtb_l, dq8, dk8, dv, dsm8):
    T = p.shape[0]
    tT = min(256, T)
    nt = T // tT
    prev_spec, next_spec = _halo_specs(tT, T, 2048, OFF_CQKV // 2048)
    _, n1024 = _halo_specs(tT, T, 1024, 0)

    def body(p_ref, pp_ref, pn_ref, ps_ref, cw_ref, al_ref, dt_ref, dq_ref, dqn_ref, dk_ref, dkn_ref,
             dv_ref, dvn_ref, dsm_ref, dp_ref, dps_ref, dcw_ref, dsml_ref):
        i = pl.program_id(0)

        @pl.when(i == 0)
        def _():
            dcw_ref[...] = jnp.zeros_like(dcw_ref)
            dsml_ref[...] = jnp.zeros_like(dsml_ref)
        rows = lax.broadcasted_iota(jnp.int32, (tT + 16, 1), 0)
        pp = jnp.where(i == 0, 0.0, pp_ref[...])
        xe = jnp.concatenate([pp, p_ref[...], pn_ref[...]], axis=0)
        xs = [_shift_down(xe, 3), _shift_down(xe, 2), _shift_down(xe, 1), xe]
        cv = cw_ref[0:1, :] * xs[0] + cw_ref[1:2, :] * xs[1] + cw_ref[2:3, :] * xs[2] + cw_ref[3:4, :] * xs[3]
        y = _silu(cv)
        last = (rows >= tT + 8) & (i == nt - 1)
        z8q = jnp.zeros((8, 1024), f32)

        def ext(cur_ref, nxt_ref):
            return jnp.where(last, 0.0, jnp.concatenate([z8q, cur_ref[...], nxt_ref[...]], axis=0))
        dq8e, dk8e, dve = ext(dq_ref, dqn_ref), ext(dk_ref, dkn_ref), ext(dv_ref, dvn_ref)
        dys = []
        for (d8, base, scale) in ((dq8e, 0, QK_SCALE), (dk8e, 512, 1.0)):
            for hh in range(4):
                dn = (d8[:, (2 * hh) * HK:(2 * hh + 1) * HK] + d8[:, (2 * hh + 1) * HK:(2 * hh + 2) * HK]) * scale
                yh = y[:, base + hh * HK:base + (hh + 1) * HK]
                r = lax.rsqrt(jnp.sum(yh * yh, axis=-1, keepdims=True) + L2_EPS)
                nh = yh * r
                dys.append(r * (dn - nh * jnp.sum(dn * nh, axis=-1, keepdims=True)))
        dyy = jnp.concatenate(dys + [dve], axis=1)
        dcv = dyy * _dsilu(cv)
        dx = (cw_ref[3:4, :] * dcv + cw_ref[2:3, :] * _shift_up(dcv, 1) + cw_ref[1:2, :] * _shift_up(dcv, 2)
              + cw_ref[0:1, :] * _shift_up(dcv, 3))
        dp_ref[...] = _b(dx[8:tT + 8])
        inner = (rows >= 8) & (rows < tT + 8)
        dcv_in = jnp.where(inner, dcv, 0.0)
        for j in range(4):
            dcw_ref[j:j + 1, :] += jnp.sum(dcv_in * xs[j], axis=0, keepdims=True)
        ps = ps_ref[...]
        lane = lax.broadcasted_iota(jnp.int32, ps.shape, 1)
        dsm = dsm_ref[:, 0:128]
        for hh in range(1, N_GDN):
            dsm = dsm + dsm_ref[:, hh * 128:(hh + 1) * 128]
        beta = _sigmoid(ps)
        xa = ps + dt_ref[...]
        nea = -jnp.exp(al_ref[...])
        dpa = dsm * nea * _sigmoid(xa)
        dps_ref[...] = _b(jnp.where(lane < 8, dsm * beta * (1.0 - beta), jnp.where(lane < 16, dpa, 0.0)))
        amask = (lane >= 8) & (lane < 16)
        dsml_ref[0:1, :] += jnp.sum(jnp.where(amask, dsm * nea * _softplus(xa), 0.0), axis=0, keepdims=True)
        dsml_ref[1:2, :] += jnp.sum(jnp.where(amask, dpa, 0.0), axis=0, keepdims=True)

    cur1024 = pl.BlockSpec((tT, 1024), lambda i: (i, 0))
    return pl.pallas_call(
        body, name="mixc_pre_bwd", grid=(nt,),
        in_specs=[pl.BlockSpec((tT, 2048), lambda i: (i, OFF_CQKV // 2048)), prev_spec, next_spec,
                  pl.BlockSpec((tT, 128), lambda i: (i, OFF_S // 128)),
                  pl.BlockSpec((4, 2048), lambda i: (0, 0)),
                  pl.BlockSpec((1, 128), lambda i: (0, 0)), pl.BlockSpec((1, 128), lambda i: (0, 0)),
                  cur1024, n1024, cur1024, n1024, cur1024, n1024, cur1024],
        out_specs=[pl.BlockSpec((tT, 2048), lambda i: (i, 0)), pl.BlockSpec((tT, 128), lambda i: (i, 0)),
                   pl.BlockSpec((8, 2048), lambda i: (0, 0)), pl.BlockSpec((8, 128), lambda i: (0, 0))],
        out_shape=[jax.ShapeDtypeStruct((T, 2048), bf16), jax.ShapeDtypeStruct((T, 128), bf16),
                   jax.ShapeDtypeStruct((8, 2048), f32), jax.ShapeDtypeStruct((8, 128), f32)],
        compiler_params=_cp(("arbitrary",)),
    )(p, p, p, p, cw, alog_l, dtb_l, dq8, dq8, dk8, dk8, dv, dv, dsm8)


def _tri_inverse(m):
    r = lax.broadcasted_iota(jnp.int32, (L, L), 0)
    c = lax.broadcasted_iota(jnp.int32, (L, L), 1)
    eye = (r == c).astype(f32)[None]
    same = lambda w: ((r // w) == (c // w))[None]
    md = jnp.where(same(8), m, 0.0)
    m2 = _bdot_hi(md, md)
    m4 = _bdot_hi(m2, m2)
    t = _bdot_hi(_bdot_hi(eye - md, eye + m2), eye + m4)
    for w in (16, 32, 64):
        mo = jnp.where(same(w) & jnp.logical_not(same(w // 2)), m, 0.0)
        t = t - _bdot_hi(_bdot_hi(t, mo), t)
    return t


def _col_to_row(col, eye):
    return jnp.sum(eye * col, axis=1, keepdims=True)


def _row_to_col(row, eye):
    return jnp.sum(eye * row, axis=2, keepdims=True)


def _gdn_chunk_terms(q, k, v, beta, g, t_inv=None):
    r = lax.broadcasted_iota(jnp.int32, (L, L), 0)
    c = lax.broadcasted_iota(jnp.int32, (L, L), 1)
    eye = (r == c).astype(f32)[None]
    causal, strict = (c <= r)[None], (c < r)[None]
    diff = g - _col_to_row(g, eye)
    dec = jnp.exp(jnp.where(causal, diff, 0.0))
    dc = jnp.where(causal, dec, 0.0)
    ds = jnp.where(strict, dec, 0.0)
    eg = jnp.exp(g)
    gl = g[:, L - 1:L, :]
    egl = jnp.exp(gl - g)
    kb = k * beta
    kk = _bdot_nt(_b(k), _b(kb))
    qk = _bdot_nt(_b(q), _b(kb))
    m = kk * ds
    aqk = qk * dc
    if t_inv is None:
        t_inv = _tri_inverse(m)
    tb = _b(t_inv)
    keg = k * eg
    u = _bdot(tb, _b(v))
    w = _bdot(tb, _b(keg))
    return dict(eye=eye, causal=causal, strict=strict, dc=dc, ds=ds, eg=eg, gl=gl, egl=egl, kb=kb, kk=kk, qk=qk,
                m=m, aqk=aqk, t=t_inv, u=u, w=w, qi=q * eg, ks=kb * egl, keg=keg)


def gdn_fwd(qn, kn, vv, sm):
    T = qn.shape[0]
    tB = min(256, T)
    nc = tB // L
    N = T // L

    def body(q_ref, k_ref, v_ref, sm_ref, o_ref, st_ref, ti_ref, s_scr):
        h = pl.program_id(0)

        @pl.when(pl.program_id(1) == 0)
        def _():
            s_scr[...] = jnp.zeros_like(s_scr)
        smv = sm_ref[...]
        lane = lax.broadcasted_iota(jnp.int32, smv.shape, 1)
        beta = jnp.sum(jnp.where(lane == h, smv, 0.0), axis=1, keepdims=True).reshape(nc, L, 1)
        g = jnp.sum(jnp.where(lane == 8 + h, smv, 0.0), axis=1, keepdims=True).reshape(nc, L, 1)
        q = q_ref[...].reshape(nc, L, HK)
        k = k_ref[...].reshape(nc, L, HK)
        v = v_ref[...].reshape(nc, L, HK)
        tm = _gdn_chunk_terms(q, k, v, beta, g)
        ti_ref[...] = tm["t"]
        s = s_scr[...]
        for ci in range(nc):
            st_ref[ci] = s
            sb = _b(s)
            e = tm["u"][ci] - _dot(_b(tm["w"][ci]), sb)
            eb = _b(e)
            o_ref[ci * L:(ci + 1) * L, :] = _dot(_b(tm["qi"][ci]), sb) + _dot(_b(tm["aqk"][ci]), eb)
            s = jnp.exp(tm["gl"][ci]) * s + _dot_tn(_b(tm["ks"][ci]), eb)
        s_scr[...] = s

    return pl.pallas_call(
        body, name="gdn_fwd", grid=(N_GDN, T // tB),
        in_specs=[pl.BlockSpec((tB, HK), lambda h, n: (n, h // 2)), pl.BlockSpec((tB, HK), lambda h, n: (n, h // 2)),
                  pl.BlockSpec((tB, HK), lambda h, n: (n, h)), pl.BlockSpec((tB, 128), lambda h, n: (n, 0))],
        out_specs=[pl.BlockSpec((tB, HK), lambda h, n: (n, h)),
                   pl.BlockSpec((None, nc, HK, HK), lambda h, n: (h, n, 0, 0)),
                   pl.BlockSpec((None, nc, L, L), lambda h, n: (h, n, 0, 0))],
        out_shape=[jax.ShapeDtypeStruct((T, N_GDN * HK), f32), jax.ShapeDtypeStruct((N_GDN, N, HK, HK), f32),
                   jax.ShapeDtypeStruct((N_GDN, N, L, L), f32)],
        scratch_shapes=[pltpu.VMEM((HK, HK), f32)],
        compiler_params=_cp(("parallel", "arbitrary")),
    )(qn, kn, vv, sm)


def gdn_bwd(qn, kn, vv, sm, st, ti, do):
    T = qn.shape[0]
    tB = min(256, T)
    nc = tB // L
    nb = T // tB

    def body(q_ref, k_ref, v_ref, sm_ref, st_ref, ti_ref, do_ref, dq_ref, dk_ref, dv_ref, dsm_ref, ds_scr):
        h = pl.program_id(0)

        @pl.when(pl.program_id(1) == 0)
        def _():
            ds_scr[...] = jnp.zeros_like(ds_scr)
        smv = sm_ref[...]
        lane = lax.broadcasted_iota(jnp.int32, smv.shape, 1)
        beta = jnp.sum(jnp.where(lane == h, smv, 0.0), axis=1, keepdims=True).reshape(nc, L, 1)
        g = jnp.sum(jnp.where(lane == 8 + h, smv, 0.0), axis=1, keepdims=True).reshape(nc, L, 1)
        q = q_ref[...].reshape(nc, L, HK)
        k = k_ref[...].reshape(nc, L, HK)
        v = v_ref[...].reshape(nc, L, HK)
        do = do_ref[...].reshape(nc, L, HK)
        s = st_ref[...]
        tm = _gdn_chunk_terms(q, k, v, beta, g, t_inv=ti_ref[...])
        eye, dc, ds_, eg, egl = tm["eye"], tm["dc"], tm["ds"], tm["eg"], tm["egl"]
        kb, u, w, qi, ks, tinv = tm["kb"], tm["u"], tm["w"], tm["qi"], tm["ks"], tm["t"]
        sb, dob = _b(s), _b(do)
        e = u - _bdot(_b(w), sb)
        eb = _b(e)
        egl_last = jnp.exp(tm["gl"])
        de0 = _bdot_tn(_b(tm["aqk"]), dob)
        ds0 = _bdot_tn(_b(qi), dob)
        dsn = ds_scr[...]
        des, dsns = [None] * nc, [None] * nc
        for ci in reversed(range(nc)):
            dsns[ci] = dsn
            de = de0[ci] + _dot(_b(ks[ci]), _b(dsn))
            des[ci] = de
            dsn = ds0[ci] + egl_last[ci] * dsn - _dot_tn(_b(w[ci]), _b(de))
        ds_scr[...] = dsn
        de = jnp.stack(des, axis=0)
        dsp = jnp.stack(dsns, axis=0)
        deb, dspb = _b(de), _b(dsp)
        dks = _bdot_nt(eb, dspb)
        dqi = _bdot_nt(dob, sb)
        daqk = jnp.where(tm["causal"], _bdot_nt(dob, eb), 0.0)
        dw = -_bdot_nt(deb, sb)
        tb = _b(tinv)
        dvv = _bdot_tn(tb, deb)
        dkg = _bdot_tn(tb, _b(dw))
        dm = -jnp.where(tm["strict"], _bdot_nt(_b(dvv), _b(u)) + _bdot_nt(_b(dkg), _b(w)), 0.0)
        x = _b(dm * ds_)
        y = _b(daqk * dc)
        kbb, kbf, qbf = _b(kb), _b(k), _b(q)
        dk = _bdot(x, kbb) + dkg * eg
        dkb = _bdot_tn(x, kbf) + _bdot_tn(y, qbf) + dks * egl
        dq = _bdot(y, kbb) + dqi * eg
        dk = dk + dkb * beta
        dbeta = jnp.sum(dkb * k, axis=-1, keepdims=True)
        z = dm * tm["m"] + daqk * tm["aqk"]
        dg = (jnp.sum(dqi * qi - dks * ks + dkg * tm["keg"], axis=-1, keepdims=True)
              + jnp.sum(z, axis=-1, keepdims=True) - _row_to_col(jnp.sum(z, axis=1, keepdims=True), eye))
        dgl = (egl_last * jnp.sum(jnp.sum(s * dsp, axis=2, keepdims=True), axis=1, keepdims=True)
               + jnp.sum(jnp.sum(dks * ks, axis=2, keepdims=True), axis=1, keepdims=True))
        rowi = lax.broadcasted_iota(jnp.int32, (nc, L, 1), 1)
        dg = dg + jnp.where(rowi == L - 1, dgl, 0.0)
        dg2 = dg.reshape(tB, 1)
        rin = lax.broadcasted_iota(jnp.int32, (tB, 1), 0) % L
        dla = _chunk_rev_cumsum(jnp.broadcast_to(dg2, (tB, 128)), jnp.broadcast_to(rin, (tB, 128)))
        dq_ref[...] = dq.reshape(tB, HK)
        dk_ref[...] = dk.reshape(tB, HK)
        dv_ref[...] = dvv.reshape(tB, HK)
        dsm_ref[...] = jnp.where(lane == h, dbeta.reshape(tB, 1), jnp.where(lane == 8 + h, dla, 0.0))

    rev = lambda n: nb - 1 - n
    return pl.pallas_call(
        body, name="gdn_bwd", grid=(N_GDN, nb),
        in_specs=[pl.BlockSpec((tB, HK), lambda h, n: (rev(n), h // 2)),
                  pl.BlockSpec((tB, HK), lambda h, n: (rev(n), h // 2)),
                  pl.BlockSpec((tB, HK), lambda h, n: (rev(n), h)), pl.BlockSpec((tB, 128), lambda h, n: (rev(n), 0)),
                  pl.BlockSpec((None, nc, HK, HK), lambda h, n: (h, rev(n), 0, 0)),
                  pl.BlockSpec((None, nc, L, L), lambda h, n: (h, rev(n), 0, 0)),
                  pl.BlockSpec((tB, HK), lambda h, n: (rev(n), h))],
        out_specs=[pl.BlockSpec((tB, HK), lambda h, n: (rev(n), h))] * 4,
        out_shape=[jax.ShapeDtypeStruct((T, N_GDN * HK), f32)] * 4,
        scratch_shapes=[pltpu.VMEM((HK, HK), f32)],
        compiler_params=_cp(("parallel", "arbitrary")),
    )(qn, kn, vv, sm, st, ti, do)


def _hgrn_prep(bq, bf_, bi, lb):
    tB = bq.shape[0]
    q = _silu(bq) * QK_SCALE
    sg = _sigmoid(bf_)
    f = lb + (1.0 - lb) * sg
    logf = jnp.log(jnp.maximum(f, MIN_F))
    rin = lax.broadcasted_iota(jnp.int32, (tB, HK), 0) % L
    g = _chunk_cumsum(logf, rin)
    return q, sg, f, 1.0 - f, bi, g, rin


def _hgrn_intra(q, kk, v, g, do=None):
    n = q.shape[0]
    nsub = L // SUB
    bwd = do is not None
    o_rows = [None] * nsub
    if bwd:
        dq_rows = [None] * nsub
        dkk_acc = jnp.zeros_like(kk)
        dv_acc = jnp.zeros_like(v)
    for i in range(1, nsub):
        lo, hi, w = i * SUB, (i + 1) * SUB, i * SUB
        ref = g[:, lo - 1:lo, :]
        eq = jnp.exp(g[:, lo:hi, :] - ref)
        ek = jnp.exp(ref - g[:, :w, :])
        qs = _b(q[:, lo:hi, :] * eq)
        ks = _b(kk[:, :w, :] * ek)
        p = _bdot_nt(qs, ks)
        o_rows[i] = _bdot(_b(p), _b(v[:, :w, :]))
        if bwd:
            dob = _b(do[:, lo:hi, :])
            dp = _b(_bdot_nt(dob, _b(v[:, :w, :])))
            dq_rows[i] = _bdot(dp, ks) * eq
            pad = jnp.zeros((n, L - w, HK), f32)
            dkk_acc = dkk_acc + jnp.concatenate([_bdot_tn(dp, qs) * ek, pad], axis=1)
            dv_acc = dv_acc + jnp.concatenate([_bdot_tn(_b(p), dob), pad], axis=1)
    m = n * nsub
    q4, k4, v4, g4 = (a.reshape(m, SUB, HK) for a in (q, kk, v, g))
    r = lax.broadcasted_iota(jnp.int32, (m, SUB, HK), 1)
    od = jnp.zeros((m, SUB, HK), f32)
    if bwd:
        do4 = do.reshape(m, SUB, HK)
        dqd = jnp.zeros((m, SUB, HK), f32)
        dkd = jnp.zeros((m, SUB, HK), f32)
        dvd = jnp.zeros((m, SUB, HK), f32)
    for j in range(SUB):
        gj, kj, vj = g4[:, j:j + 1, :], k4[:, j:j + 1, :], v4[:, j:j + 1, :]
        ok = r >= j
        e = jnp.where(ok, jnp.exp(jnp.where(ok, g4 - gj, 0.0)), 0.0)
        xq = q4 * e
        pj = jnp.sum(xq * kj, axis=-1, keepdims=True)
        od = od + pj * vj
        if bwd:
            dpj = jnp.sum(do4 * vj, axis=-1, keepdims=True)
            dqd = dqd + dpj * kj * e
            dkd = dkd + jnp.where(r == j, jnp.sum(dpj * xq, axis=1, keepdims=True), 0.0)
            dvd = dvd + jnp.where(r == j, jnp.sum(pj * do4, axis=1, keepdims=True), 0.0)
    od = od.reshape(n, L, HK)
    o = jnp.concatenate([od[:, :SUB, :]] + [od[:, i * SUB:(i + 1) * SUB, :] + o_rows[i] for i in range(1, nsub)], axis=1)
    if not bwd:
        return o
    dqd = dqd.reshape(n, L, HK)
    dq = jnp.concatenate([dqd[:, :SUB, :]] + [dqd[:, i * SUB:(i + 1) * SUB, :] + dq_rows[i] for i in range(1, nsub)], axis=1)
    return o, dq, dkk_acc + dkd.reshape(n, L, HK), dv_acc + dvd.reshape(n, L, HK)


def hgrn_fwd(p, lbs):
    T = p.shape[0]
    tB = min(256, T)
    nc = tB // L
    N = T // L
    cq, cf, ci_ = OFF_B // HK, (OFF_B + 512) // HK, (OFF_B + 1024) // HK

    def body(bq_ref, bf_ref, bi_ref, lb_ref, o_ref, st_ref, s_scr):
        @pl.when(pl.program_id(1) == 0)
        def _():
            s_scr[...] = jnp.zeros_like(s_scr)
        q, sg, f, kk, v, g, rin = _hgrn_prep(bq_ref[...], bf_ref[...], bi_ref[...], lb_ref[...])
        q3, k3, v3, g3 = (a.reshape(nc, L, HK) for a in (q, kk, v, g))
        o = _hgrn_intra(q3, k3, v3, g3)
        gl = g3[:, L - 1:L, :]
        qt = _b(q3 * jnp.exp(g3))
        kt = _b(k3 * jnp.exp(gl - g3))
        vb = _b(v3)
        st = s_scr[...]
        for c in range(nc):
            st_ref[c] = st
            o_ref[c * L:(c + 1) * L, :] = o[c] + _dot_nt(qt[c], _b(st))
            st = st * jnp.exp(gl[c]) + _dot_tn(vb[c], kt[c])
        s_scr[...] = st

    return pl.pallas_call(
        body, name="hgrn_fwd", grid=(N_HGRN, T // tB),
        in_specs=[pl.BlockSpec((tB, HK), lambda h, n: (n, cq + h)), pl.BlockSpec((tB, HK), lambda h, n: (n, cf + h)),
                  pl.BlockSpec((tB, HK), lambda h, n: (n, ci_ + h)), pl.BlockSpec((1, HK), lambda h, n: (0, h))],
        out_specs=[pl.BlockSpec((tB, HK), lambda h, n: (n, h)),
                   pl.BlockSpec((None, nc, HK, HK), lambda h, n: (h, n, 0, 0))],
        out_shape=[jax.ShapeDtypeStruct((T, N_HGRN * HK), f32), jax.ShapeDtypeStruct((N_HGRN, N, HK, HK), f32)],
        scratch_shapes=[pltpu.VMEM((HK, HK), f32)],
        compiler_params=_cp(("parallel", "arbitrary")),
    )(p, p, p, lbs)


def hgrn_bwd(p, lbs, st, do):
    T = p.shape[0]
    tB = min(256, T)
    nc = tB // L
    nb = T // tB
    cq, cf, ci_ = OFF_B // HK, (OFF_B + 512) // HK, (OFF_B + 1024) // HK

    def body(bq_ref, bf_ref, bi_ref, lb_ref, st_ref, do_ref, dbq_ref, dbf_ref, dbi_ref, dlb_ref, ds_scr):
        @pl.when(pl.program_id(1) == 0)
        def _():
            ds_scr[...] = jnp.zeros_like(ds_scr)
            dlb_ref[...] = jnp.zeros_like(dlb_ref)
        lb = lb_ref[...]
        bq = bq_ref[...]
        q, sg, f, kk, v, g, rin = _hgrn_prep(bq, bf_ref[...], bi_ref[...], lb)
        q3, k3, v3, g3 = (a.reshape(nc, L, HK) for a in (q, kk, v, g))
        do3 = do_ref[...].reshape(nc, L, HK)
        dob = _b(do3)
        gl = g3[:, L - 1:L, :]
        egl = jnp.exp(gl)
        eg, egr = jnp.exp(g3), jnp.exp(gl - g3)
        qt, kt = q3 * eg, k3 * egr
        s = st_ref[...]
        ds0 = _bdot_tn(dob, _b(qt))
        dsn = ds_scr[...]
        dsns = [None] * nc
        for c in reversed(range(nc)):
            dsns[c] = dsn
            dsn = ds0[c] + dsn * egl[c]
        ds_scr[...] = dsn
        dsp = jnp.stack(dsns, axis=0)
        dspb = _b(dsp)
        dqt = _bdot(dob, _b(s))
        dkt = _bdot(_b(v3), dspb)
        dv_state = _bdot_nt(_b(kt), dspb)
        dgl = egl * jnp.sum(s * dsp, axis=1, keepdims=True) + jnp.sum(dkt * kt, axis=1, keepdims=True)
        _, dq_i, dkk_i, dv_i = _hgrn_intra(q3, k3, v3, g3, do=do3)
        dq = dq_i + dqt * eg
        dkk = dkk_i + dkt * egr
        dv = dv_i + dv_state
        rowi = lax.broadcasted_iota(jnp.int32, (nc, L, HK), 1)
        dg = q3 * dq - k3 * dkk + jnp.where(rowi == L - 1, dgl, 0.0)
        dlogf = _chunk_rev_cumsum(dg.reshape(tB, HK), rin)
        dkk2 = dkk.reshape(tB, HK)
        df = jnp.where(f > MIN_F, dlogf / f, 0.0) - dkk2
        dlb_ref[...] += jnp.sum(df * (1.0 - sg), axis=0, keepdims=True)
        dbf_ref[...] = _b(df * (1.0 - lb) * sg * (1.0 - sg))
        dbq_ref[...] = _b(dq.reshape(tB, HK) * QK_SCALE * _dsilu(bq))
        dbi_ref[...] = _b(dv.reshape(tB, HK))

    rev = lambda n: nb - 1 - n
    return pl.pallas_call(
        body, name="hgrn_bwd", grid=(N_HGRN, nb),
        in_specs=[pl.BlockSpec((tB, HK), lambda h, n: (rev(n), cq + h)),
                  pl.BlockSpec((tB, HK), lambda h, n: (rev(n), cf + h)),
                  pl.BlockSpec((tB, HK), lambda h, n: (rev(n), ci_ + h)), pl.BlockSpec((1, HK), lambda h, n: (0, h)),
                  pl.BlockSpec((None, nc, HK, HK), lambda h, n: (h, rev(n), 0, 0)),
                  pl.BlockSpec((tB, HK), lambda h, n: (rev(n), h))],
        out_specs=[pl.BlockSpec((tB, HK), lambda h, n: (rev(n), h))] * 3 + [pl.BlockSpec((1, HK), lambda h, n: (0, h))],
        out_shape=[jax.ShapeDtypeStruct((T, N_HGRN * HK), bf16)] * 3 + [jax.ShapeDtypeStruct((1, N_HGRN * HK), f32)],
        scratch_shapes=[pltpu.VMEM((HK, HK), f32)],
        compiler_params=_cp(("parallel", "arbitrary")),
    )(p, p, p, lbs, st, do)


def _headnorm_fwd(o, z, w, nheads):
    outs = []
    for hh in range(nheads):
        sl = slice(hh * HK, (hh + 1) * HK)
        oh = o[:, sl]
        r = lax.rsqrt(jnp.mean(oh * oh, axis=-1, keepdims=True) + NORM_EPS)
        outs.append(oh * r * w * _silu(z[:, sl]))
    return jnp.concatenate(outs, axis=1)


def _headnorm_bwd(o, z, w, dy, nheads):
    dos, dzs = [], []
    dw = jnp.zeros((1, HK), f32)
    for hh in range(nheads):
        sl = slice(hh * HK, (hh + 1) * HK)
        oh, zh, dyh = o[:, sl], z[:, sl], dy[:, sl]
        r = lax.rsqrt(jnp.mean(oh * oh, axis=-1, keepdims=True) + NORM_EPS)
        on = oh * r
        sz = _silu(zh)
        dn = dyh * sz * w
        dos.append(r * (dn - on * jnp.mean(dn * on, axis=-1, keepdims=True)))
        dzs.append(dyh * on * w * _dsilu(zh))
        dw = dw + jnp.sum(dyh * sz * on, axis=0, keepdims=True)
    return jnp.concatenate(dos, axis=1), jnp.concatenate(dzs, axis=1), dw


def _merge_specs(tT):
    row = lambda w, cb=0: pl.BlockSpec((tT, w), lambda i, cb=cb: (i, cb))
    full = lambda r, c: pl.BlockSpec((r, c), lambda i: (0, 0))
    return row, full


def merge_fwd(x, p, ya, ob, oc, hw, gw, bg, woa, wob, woc, wo):
    T = x.shape[0]
    tT = min(256, T)
    row, full = _merge_specs(tT)

    def body(x_ref, bz_ref, cz_ref, g_ref, ya_ref, ob_ref, oc_ref, hw_ref, gw_ref, bg_ref,
             woa_ref, wob_ref, woc_ref, wo_ref, out_ref):
        yb = _b(_headnorm_fwd(ob_ref[...], bz_ref[...], hw_ref[...], N_HGRN))
        yc = _b(_headnorm_fwd(oc_ref[...], cz_ref[...], gw_ref[...], N_GDN))
        gates = _sigmoid(g_ref[...] + bg_ref[...])
        merged = (gates[:, 0:D] * _dot(ya_ref[...], woa_ref[...]) + gates[:, D:2 * D] * _dot(yb, wob_ref[...])
                  + gates[:, 2 * D:3 * D] * _dot(yc, woc_ref[...]))
        out_ref[...] = x_ref[...] + _dot(_b(merged), wo_ref[...])

    return pl.pallas_call(
        body, name="merge_fwd", grid=(T // tT,),
        in_specs=[row(D), row(512, (OFF_B + 1536) // 512), row(1024, OFF_CZ // 1024), row(3072, OFF_G // 3072),
                  row(512), row(512), row(1024), full(1, HK), full(1, HK), full(1, 3 * D),
                  full(512, D), full(512, D), full(D, D), full(D, D)],
        out_specs=row(D),
        out_shape=jax.ShapeDtypeStruct((T, D), f32),
        compiler_params=_cp(("parallel",)),
    )(x, p, p, p, ya, ob, oc, hw, gw, bg, woa, wob, woc, wo)


def merge_bwd(dxo, p, ya, ob, oc, hw, gw, bg, woa, wob, woc, wo):
    T = dxo.shape[0]
    tT = min(256, T)
    row, full = _merge_specs(tT)

    def body(dx_ref, bz_ref, cz_ref, g_ref, ya_ref, ob_ref, oc_ref, hw_ref, gw_ref, bg_ref,
             woa_ref, wob_ref, woc_ref, wo_ref,
             dya_ref, dob_ref, doc_ref, dbz_ref, dcz_ref, dg_ref, mg_ref, dy3_ref, yb_ref, yc_ref,
             dbg_ref, dhw_ref, dgw_ref):
        @pl.when(pl.program_id(0) == 0)
        def _():
            dbg_ref[...] = jnp.zeros_like(dbg_ref)
            dhw_ref[...] = jnp.zeros_like(dhw_ref)
            dgw_ref[...] = jnp.zeros_like(dgw_ref)
        ob, oc, bz, cz = ob_ref[...], oc_ref[...], bz_ref[...], cz_ref[...]
        hw_, gw_ = hw_ref[...], gw_ref[...]
        yb = _b(_headnorm_fwd(ob, bz, hw_, N_HGRN))
        yc = _b(_headnorm_fwd(oc, cz, gw_, N_GDN))
        yb_ref[...] = yb
        yc_ref[...] = yc
        gates = _sigmoid(g_ref[...] + bg_ref[...])
        ys = (_dot(ya_ref[...], woa_ref[...]), _dot(yb, wob_ref[...]), _dot(yc, woc_ref[...]))
        dmerged = _dot_nt(_b(dx_ref[...]), wo_ref[...])
        merged = jnp.zeros_like(dmerged)
        dys = []
        for i in range(3):
            gi = gates[:, i * D:(i + 1) * D]
            merged = merged + gi * ys[i]
            dyi = _b(dmerged * gi)
            dys.append(dyi)
            dy3_ref[:, i * D:(i + 1) * D] = dyi
            dgp = dmerged * ys[i] * gi * (1.0 - gi)
            dg_ref[:, i * D:(i + 1) * D] = _b(dgp)
            dbg_ref[:, i * D:(i + 1) * D] += jnp.sum(dgp, axis=0, keepdims=True)
        mg_ref[...] = _b(merged)
        dya_ref[...] = _dot_nt(dys[0], woa_ref[...])
        dob, dbz, dhw = _headnorm_bwd(ob, bz, hw_, _dot_nt(dys[1], wob_ref[...]), N_HGRN)
        doc, dcz, dgw = _headnorm_bwd(oc, cz, gw_, _dot_nt(dys[2], woc_ref[...]), N_GDN)
        dob_ref[...] = dob
        doc_ref[...] = doc
        dbz_ref[...] = _b(dbz)
        dcz_ref[...] = _b(dcz)
        dhw_ref[...] += dhw
        dgw_ref[...] += dgw

    sd = jax.ShapeDtypeStruct
    return pl.pallas_call(
        body, name="merge_bwd", grid=(T // tT,),
        in_specs=[row(D), row(512, (OFF_B + 1536) // 512), row(1024, OFF_CZ // 1024), row(3072, OFF_G // 3072),
                  row(512), row(512), row(1024), full(1, HK), full(1, HK), full(1, 3 * D),
                  full(512, D), full(512, D), full(D, D), full(D, D)],
        out_specs=[row(512), row(512), row(1024), row(512), row(1024), row(3072), row(D), row(3 * D), row(512),
                   row(1024), full(1, 3 * D), full(1, HK), full(1, HK)],
        out_shape=[sd((T, 512), f32), sd((T, 512), f32), sd((T, 1024), f32), sd((T, 512), bf16), sd((T, 1024), bf16),
                   sd((T, 3072), bf16), sd((T, D), bf16), sd((T, 3 * D), bf16), sd((T, 512), bf16),
                   sd((T, 1024), bf16), sd((1, 3 * D), f32), sd((1, HK), f32), sd((1, HK), f32)],
        compiler_params=_cp(("arbitrary",)),
    )(dxo, p, p, p, ya, ob, oc, hw, gw, bg, woa, wob, woc, wo)


def layer_fwd(x, w):
    p, h = inproj_fwd(x, w["norm_w"], w["w_in"])
    ya = mixa_fwd(p, w["conv_a"])
    qn, kn, vv, sm = mixc_pre_fwd(p, w["conv_c"], w["alog_l"], w["dtb_l"])
    oc, st_c, ti = gdn_fwd(qn, kn, vv, sm)
    ob, st_b = hgrn_fwd(p, w["lbs"])
    xo = merge_fwd(x, p, ya, ob, oc, w["hgrn_norm_w"], w["gdn_norm_w"], w["b_gate"],
                   w["w_out_a"], w["w_out_b"], w["w_out_c"], w["w_o"])
    saved = dict(x=x, p=p, h=h, ya=ya, qn=qn, kn=kn, vv=vv, sm=sm, oc=oc, st_c=st_c, ti=ti, ob=ob, st_b=st_b)
    return xo, saved


def layer_bwd(dxo, w, s):
    p = s["p"]
    (dya, dob, doc, dbz, dcz, dg, merged, dy3, yb, yc, dbg, dhw, dgw) = merge_bwd(
        dxo, p, s["ya"], s["ob"], s["oc"], w["hgrn_norm_w"], w["gdn_norm_w"], w["b_gate"],
        w["w_out_a"], w["w_out_b"], w["w_out_c"], w["w_o"])
    g = {}
    g["w_o"] = matmul_tn(merged, dxo, "dw_o")
    g["w_out_a"] = matmul_tn(s["ya"], dy3[:, 0:D], "dw_out_a")
    g["w_out_b"] = matmul_tn(yb, dy3[:, D:2 * D], "dw_out_b")
    g["w_out_c"] = matmul_tn(yc, dy3[:, 2 * D:3 * D], "dw_out_c")
    dbq, dbf, dbi, dlbs = hgrn_bwd(p, w["lbs"], s["st_b"], dob)
    dq8, dk8, dvv, dsm8 = gdn_bwd(s["qn"], s["kn"], s["vv"], s["sm"], s["st_c"], s["ti"], doc)
    dpc, dps, dcc, dsmall = mixc_pre_bwd(p, w["conv_c"], w["alog_l"], w["dtb_l"], dq8, dk8, dvv, dsm8)
    dpa, dca = mixa_bwd(p, w["conv_a"], dya)
    dp = jnp.concatenate([dpa, dbq, dbf, dbi, dbz, dpc, dg, dcz, dps], axis=1)
    g["w_in"] = matmul_tn(s["h"], dp, "dw_in")
    dx, dnw = inproj_bwd(dp, w["w_in"], s["x"], w["norm_w"], dxo)
    g.update(norm_w=dnw, b_gate=dbg, hgrn_norm_w=dhw, gdn_norm_w=dgw, lbs=dlbs, conv_a=dca[0:3], conv_c=dcc[0:4],
             a_log=dsmall[0:1, 8:16], dt_bias=dsmall[1:2, 8:16])
    return dx, g


def lbs_fwd(lb):
    def body(lb_ref, o_ref):
        l0, l1 = lb_ref[0:1, :], lb_ref[1:2, :]
        mx = jnp.maximum(l0, l1)
        e0, e1 = jnp.exp(l0 - mx), jnp.exp(l1 - mx)
        o_ref[0:1, :] = jnp.zeros_like(l0)
        o_ref[1:2, :] = e1 / (e0 + e1)
    return pl.pallas_call(body, name="lbs_fwd", out_shape=jax.ShapeDtypeStruct(lb.shape, f32))(lb)


def _adam_math(w, g, m, v):
    mn = ADAM_B1 * m + (1.0 - ADAM_B1) * g
    vn = ADAM_B2 * v + (1.0 - ADAM_B2) * (g * g)
    mh = mn / (1.0 - ADAM_B1 ** ADAM_STEP)
    vh = vn / (1.0 - ADAM_B2 ** ADAM_STEP)
    return -ADAM_LR * (mh / (jnp.sqrt(vh) + ADAM_EPS) + ADAM_WD * w), mn, vn


def adam(w, g, m, v, name):
    R, C = w.shape
    tr = 256 if R % 256 == 0 else R

    def body(w_ref, g_ref, m_ref, v_ref, d_ref, mo_ref, vo_ref):
        d, mn, vn = _adam_math(w_ref[...], g_ref[...], m_ref[...], v_ref[...])
        d_ref[...] = d
        mo_ref[...] = mn
        vo_ref[...] = vn

    spec = pl.BlockSpec((tr, C), lambda i: (i, 0))
    return pl.pallas_call(
        body, name=name, grid=(R // tr,), in_specs=[spec] * 4, out_specs=[spec] * 3,
        out_shape=[jax.ShapeDtypeStruct((R, C), f32)] * 3, compiler_params=_cp(("parallel",)),
    )(w, g, m, v)


SMALL_ROWS = 96
LB_ROW = 64


def small_update(parts, wp, mp, vp):
    def body(p_ref, w_ref, m_ref, v_ref, g_ref, d_ref, mo_ref, vo_ref):
        gs = p_ref[0]
        for i in range(1, 8):
            gs = gs + p_ref[i]
        w = w_ref[...]
        l0, l1 = w[LB_ROW:LB_ROW + 8], w[LB_ROW + 8:LB_ROW + 16]
        mx = jnp.maximum(l0, l1)
        e0, e1 = jnp.exp(l0 - mx), jnp.exp(l1 - mx)
        p0, p1 = e0 / (e0 + e1), e1 / (e0 + e1)
        s = p1 * gs[LB_ROW + 8:LB_ROW + 16]
        g = jnp.concatenate([gs[0:LB_ROW], -p0 * s, p1 * gs[LB_ROW + 8:LB_ROW + 16] - p1 * s, gs[LB_ROW + 16:]], axis=0)
        d, mn, vn = _adam_math(w, g, m_ref[...], v_ref[...])
        g_ref[...] = g
        d_ref[...] = d
        mo_ref[...] = mn
        vo_ref[...] = vn
    return pl.pallas_call(body, name="small_update",
                          out_shape=[jax.ShapeDtypeStruct((SMALL_ROWS, 128), f32)] * 4)(parts, wp, mp, vp)


def partial_sum(own, recv):
    R = own.shape[0]
    tr = _row_tile(R)

    def body(o_ref, r_ref, out_ref):
        out_ref[...] = ((o_ref[...] + r_ref[0].astype(f32)) + r_ref[1].astype(f32)) + r_ref[2].astype(f32)

    return pl.pallas_call(
        body, name="partial_sum", grid=(R // tr,),
        in_specs=[pl.BlockSpec((tr, 128), lambda i: (i, 0)), pl.BlockSpec((3, tr, 128), lambda i: (0, i, 0))],
        out_specs=pl.BlockSpec((tr, 128), lambda i: (i, 0)),
        out_shape=jax.ShapeDtypeStruct((R, 128), f32), compiler_params=_cp(("parallel",)),
    )(own, recv)


def pair_sum(a, b):
    R = a.shape[0]
    tr = _row_tile(R)

    def body(a_ref, b_ref, out_ref):
        out_ref[...] = a_ref[...] + b_ref[...]

    spec = pl.BlockSpec((tr, 128), lambda i: (i, 0))
    return pl.pallas_call(body, name="pair_sum", grid=(R // tr,), in_specs=[spec, spec], out_specs=spec,
                          out_shape=jax.ShapeDtypeStruct((R, 128), f32), compiler_params=_cp(("parallel",)))(a, b)


def _row_tile(R):
    for t in (4096, 2048, 1024, 512, 256, 128, 64, 32, 16, 8):
        if R % t == 0:
            return t
    return R


MESH = pl.DeviceIdType.MESH
_HBM = pl.BlockSpec(memory_space=pltpu.HBM)


def _place():
    return lax.axis_index("x"), lax.axis_index("y"), lax.axis_index("c")


def weight_gather(xs):
    R = xs.shape[0]
    rh = R // 2

    def body(x_ref, out_ref, send_sems, recv_sems, local_sem):
        x, y, c = _place()
        me, sibling = (x, y, c), (x, y, 1 - c)
        chips = [(1 - x, y), (x, 1 - y), (1 - x, 1 - y)]
        mine_src = x_ref.at[pl.ds(pl.multiple_of(c * rh, 16), rh)]

        def blk(px, py, pc):
            return out_ref.at[4 * px + 2 * py + pc]

        def copy(k, block, to, src=None):
            return pltpu.make_async_remote_copy(
                src_ref=blk(*block) if src is None else src, dst_ref=blk(*block),
                send_sem=send_sems.at[k], recv_sem=recv_sems.at[k], device_id=to, device_id_type=MESH)

        mine = pltpu.make_async_copy(mine_src, blk(*me), local_sem)
        mine.start()
        first = [copy(0, me, sibling, src=mine_src)]
        first += [copy(1 + j, me, (*chip, c), src=mine_src) for j, chip in enumerate(chips)]
        for cp in first:
            cp.start()
        passed = [copy(4 + j, (*chip, c), sibling) for j, chip in enumerate(chips)]
        for j, chip in enumerate(chips):
            copy(1 + j, (*chip, c), me).wait_recv()
            passed[j].start()
        copy(0, sibling, me).wait_recv()
        for j, chip in enumerate(chips):
            copy(4 + j, (*chip, 1 - c), me).wait_recv()
        for cp in first + passed:
            cp.wait_send()
        mine.wait()

    return pl.pallas_call(
        body, name="weight_gather", in_specs=[_HBM], out_specs=_HBM,
        out_shape=jax.ShapeDtypeStruct((8, rh, 128), xs.dtype),
        scratch_shapes=[pltpu.SemaphoreType.DMA((7,)), pltpu.SemaphoreType.DMA((7,)), pltpu.SemaphoreType.DMA],
    )(xs)


def grad_exchange(gb, small):
    R = gb.shape[1]
    S = small.shape[0]

    def body(gb_ref, sm_ref, recv_ref, smalls_ref, send_sems, recv_sems, local_sem):
        x, y, c = _place()
        chips = [(1 - x, y), (x, 1 - y), (1 - x, 1 - y)]
        my_slot = smalls_ref.at[4 * x + 2 * y + c]
        mine = pltpu.make_async_copy(sm_ref, my_slot, local_sem)
        mine.start()
        copies = []
        for j, (px, py) in enumerate(chips):
            copies.append(pltpu.make_async_remote_copy(
                src_ref=gb_ref.at[2 * px + py], dst_ref=recv_ref.at[j], send_sem=send_sems.at[j],
                recv_sem=recv_sems.at[j], device_id=(px, py, c), device_id_type=MESH))
        for mask in range(1, 8):
            fx, fy, fc = (mask >> 2) & 1, (mask >> 1) & 1, mask & 1
            peer = ((1 - x) if fx else x, (1 - y) if fy else y, (1 - c) if fc else c)
            copies.append(pltpu.make_async_remote_copy(
                src_ref=sm_ref, dst_ref=my_slot, send_sem=send_sems.at[2 + mask], recv_sem=recv_sems.at[2 + mask],
                device_id=peer, device_id_type=MESH))
        for cp in copies:
            cp.start()
        for cp in copies:
            cp.wait_recv()
        for cp in copies:
            cp.wait_send()
        mine.wait()

    return pl.pallas_call(
        body, name="grad_exchange", in_specs=[_HBM, _HBM], out_specs=[_HBM, _HBM],
        out_shape=[jax.ShapeDtypeStruct((3, R, 128), gb.dtype), jax.ShapeDtypeStruct((8, S, 128), f32)],
        scratch_shapes=[pltpu.SemaphoreType.DMA((10,)), pltpu.SemaphoreType.DMA((10,)), pltpu.SemaphoreType.DMA],
    )(gb, small)


def sibling_swap(h):
    def body(h_ref, out_ref, send_sem, recv_sem):
        x, y, c = _place()
        cp = pltpu.make_async_remote_copy(src_ref=h_ref, dst_ref=out_ref, send_sem=send_sem, recv_sem=recv_sem,
                                          device_id=(x, y, 1 - c), device_id_type=MESH)
        cp.start()
        cp.wait()

    return pl.pallas_call(
        body, name="sibling_swap", in_specs=[_HBM], out_specs=_HBM,
        out_shape=jax.ShapeDtypeStruct(h.shape, h.dtype),
        scratch_shapes=[pltpu.SemaphoreType.DMA, pltpu.SemaphoreType.DMA],
    )(h)


N_CHIPS = 4
SHARD_COLS = N_ORIG // N_CHIPS


def _rows128(flat, row_multiple):
    n = flat.shape[-1]
    unit = 128 * row_multiple
    total = -(-n // unit) * unit
    pad = [(0, 0)] * (flat.ndim - 1) + [(0, total - n)]
    return jnp.pad(flat, pad).reshape(flat.shape[:-1] + (total // 128, 128))


def _to_padded_cols(w):
    z = jnp.zeros(w.shape[:-1] + (NP - N_ORIG,), w.dtype)
    return jnp.concatenate([w[..., 0:6144], w[..., 7184:10256], w[..., 6160:7184], w[..., 6144:6160], z], axis=-1)


def _from_padded_cols(g):
    return jnp.concatenate([g[..., 0:6144], g[..., 10240:10256], g[..., 9216:10240], g[..., 6144:9216]], axis=-1)


def _pack_weight_shard(w_in, w_out_a, w_out_b, w_out_c, w_o, conv_a, conv_c):
    parts = [_b(a).reshape(-1) for a in (w_in, w_out_a, w_out_b, w_out_c, w_o)]
    conv = jnp.concatenate([conv_a.reshape(-1), conv_c.reshape(-1)])
    parts.append(lax.bitcast_convert_type(conv, bf16).reshape(-1))
    return _rows128(jnp.concatenate(parts), 32)


def _unpack_weights(gathered):
    flat = gathered.reshape(N_CHIPS, -1)
    sizes = [2 * D * SHARD_COLS, 2 * 512 * 256, 2 * 512 * 256, 2 * 256 * D, 2 * 256 * D, 2 * (2 * 3 * 128 + 2 * 4 * 512)]
    offs = [0]
    for s in sizes:
        offs.append(offs[-1] + s)
    seg = [flat[:, offs[i]:offs[i + 1]] for i in range(len(sizes))]
    cols = lambda a, shp: a.reshape((N_CHIPS,) + shp).transpose(1, 2, 0, 3).reshape(shp[0], shp[1], N_CHIPS * shp[2])
    rows = lambda a, shp: a.reshape((N_CHIPS,) + shp).transpose(1, 0, 2, 3).reshape(shp[0], N_CHIPS * shp[1], shp[2])
    conv = lax.bitcast_convert_type(seg[5].reshape(N_CHIPS, -1, 2), f32)
    return dict(
        w_in=_to_padded_cols(cols(seg[0], (2, D, SHARD_COLS))),
        w_out_a=cols(seg[1], (2, 512, 256)), w_out_b=cols(seg[2], (2, 512, 256)),
        w_out_c=rows(seg[3], (2, 256, D)), w_o=rows(seg[4], (2, 256, D)),
        conv_a=cols(conv[:, :768], (2, 3, 128)), conv_c=cols(conv[:, 768:], (2, 4, 512)))


_BIG = (("w_in", "cols", (2, D, SHARD_COLS)), ("w_out_a", "cols", (2, 512, 256)), ("w_out_b", "cols", (2, 512, 256)),
        ("w_out_c", "rows", (2, 256, D)), ("w_o", "rows", (2, 256, D)), ("conv_a", "cols", (2, 3, 128)),
        ("conv_c", "cols", (2, 4, 512)))


def _pack_grads(g):
    parts = []
    for name, kind, shp in _BIG:
        a = g[name]
        if kind == "cols":
            a = a.reshape(shp[0], shp[1], N_CHIPS, shp[2]).transpose(2, 0, 1, 3)
        else:
            a = a.reshape(shp[0], N_CHIPS, shp[1], shp[2]).transpose(1, 0, 2, 3)
        parts.append(a.reshape(N_CHIPS, -1))
    return _rows128(jnp.concatenate(parts, axis=1), 32)


def _unpack_shard(pack):
    flat = pack.reshape(-1)
    out, off = {}, 0
    for name, _, shp in _BIG:
        n = shp[0] * shp[1] * shp[2]
        out[name] = flat[off:off + n].reshape(shp)
        off += n
    return out


_SMALL = (("norm_w", 0, 2 * D), ("b_gate", 16, 6 * D), ("hgrn_norm_w", 80, 2 * HK), ("gdn_norm_w", 82, 2 * HK),
          ("a_log", 84, 16), ("dt_bias", 85, 16), ("final_norm_w", 88, D))
LOSS_ROW = 86


def _pack_small(v, loss_row=None):
    out = jnp.zeros((SMALL_ROWS, 128), f32)
    for name, row, n in _SMALL:
        a = _rows128(v[name].reshape(-1), 1)
        out = lax.dynamic_update_slice(out, a, (row, 0))
    lb = jnp.pad(v["lower_bounds"].reshape(2, 4, 128), ((0, 0), (0, 4), (0, 0))).reshape(16, 128)
    out = lax.dynamic_update_slice(out, lb, (LB_ROW, 0))
    if loss_row is not None:
        out = lax.dynamic_update_slice(out, loss_row, (LOSS_ROW, 0))
    return out


def _unpack_small(p, shapes):
    out = {}
    for name, row, n in _SMALL:
        nrows = -(-n // 128)
        out[name] = p[row:row + nrows].reshape(-1)[:n].reshape(shapes[name])
    out["lower_bounds"] = p[LB_ROW:LB_ROW + 16].reshape(2, 8, 128)[:, :4].reshape(2, 512)
    return out


def _lane_vec(a8):
    return jnp.pad(a8.reshape(1, 8), ((0, 0), (8, 112)))


WEIGHT_NAMES = ("norm_w", "w_in", "b_gate", "conv_a", "conv_c", "a_log", "dt_bias", "lower_bounds", "hgrn_norm_w",
                "gdn_norm_w", "w_out_a", "w_out_b", "w_out_c", "w_o", "final_norm_w")


def kernel(x, norm_w, w_in, b_gate, conv_a, conv_c, a_log, dt_bias, lower_bounds, hgrn_norm_w, gdn_norm_w, w_out_a, w_out_b, w_out_c, w_o, final_norm_w, loss_target, m_norm_w, m_w_in, m_b_gate, m_conv_a, m_conv_c, m_a_log, m_dt_bias, m_lower_bounds, m_hgrn_norm_w, m_gdn_norm_w, m_w_out_a, m_w_out_b, m_w_out_c, m_w_o, m_final_norm_w, v_norm_w, v_w_in, v_b_gate, v_conv_a, v_conv_c, v_a_log, v_dt_bias, v_lower_bounds, v_hgrn_norm_w, v_gdn_norm_w, v_w_out_a, v_w_out_b, v_w_out_c, v_w_o, v_final_norm_w):
    wts = dict(norm_w=norm_w, w_in=w_in, b_gate=b_gate, conv_a=conv_a, conv_c=conv_c, a_log=a_log, dt_bias=dt_bias,
               lower_bounds=lower_bounds, hgrn_norm_w=hgrn_norm_w, gdn_norm_w=gdn_norm_w, w_out_a=w_out_a,
               w_out_b=w_out_b, w_out_c=w_out_c, w_o=w_o, final_norm_w=final_norm_w)
    mom = dict(norm_w=m_norm_w, w_in=m_w_in, b_gate=m_b_gate, conv_a=m_conv_a, conv_c=m_conv_c, a_log=m_a_log,
               dt_bias=m_dt_bias, lower_bounds=m_lower_bounds, hgrn_norm_w=m_hgrn_norm_w, gdn_norm_w=m_gdn_norm_w,
               w_out_a=m_w_out_a, w_out_b=m_w_out_b, w_out_c=m_w_out_c, w_o=m_w_o, final_norm_w=m_final_norm_w)
    var = dict(norm_w=v_norm_w, w_in=v_w_in, b_gate=v_b_gate, conv_a=v_conv_a, conv_c=v_conv_c, a_log=v_a_log,
               dt_bias=v_dt_bias, lower_bounds=v_lower_bounds, hgrn_norm_w=v_hgrn_norm_w, gdn_norm_w=v_gdn_norm_w,
               w_out_a=v_w_out_a, w_out_b=v_w_out_b, w_out_c=v_w_out_c, w_o=v_w_o, final_norm_w=v_final_norm_w)
    chip = 2 * lax.axis_index("x") + lax.axis_index("y")

    full = _unpack_weights(weight_gather(
        _pack_weight_shard(w_in, w_out_a, w_out_b, w_out_c, w_o, conv_a, conv_c)))
    lbs = lbs_fwd(lower_bounds)
    layers = []
    for l in range(2):
        layers.append(dict(
            norm_w=norm_w[l:l + 1], w_in=full["w_in"][l], b_gate=b_gate[l:l + 1], conv_a=full["conv_a"][l],
            conv_c=full["conv_c"][l], alog_l=_lane_vec(a_log[l]), dtb_l=_lane_vec(dt_bias[l]), lbs=lbs[l:l + 1],
            hgrn_norm_w=hgrn_norm_w[l:l + 1], gdn_norm_w=gdn_norm_w[l:l + 1], w_out_a=full["w_out_a"][l],
            w_out_b=full["w_out_b"][l], w_out_c=full["w_out_c"][l], w_o=full["w_o"][l]))

    xs, saved = x[0], []
    for l in range(2):
        xs, s = layer_fwd(xs, layers[l])
        saved.append(s)
    loss_row, dx, dfw = loss_head(xs, final_norm_w.reshape(1, D), loss_target[0])
    lg = [None, None]
    for l in (1, 0):
        dx, lg[l] = layer_bwd(dx, layers[l], saved[l])
    grad_x = dx[None]

    stack = lambda n: jnp.stack([lg[0][n], lg[1][n]], axis=0)
    gfull = {n: stack(n) for n in ("w_out_a", "w_out_b", "w_out_c", "w_o", "conv_a", "conv_c")}
    gfull["w_in"] = _from_padded_cols(stack("w_in"))
    gpack = _pack_grads(gfull)
    gsmall = {n: stack(n) for n in ("norm_w", "b_gate", "hgrn_norm_w", "gdn_norm_w", "a_log", "dt_bias")}
    gsmall["lower_bounds"] = stack("lbs")
    gsmall["final_norm_w"] = dfw
    small = _pack_small(gsmall, loss_row=loss_row)
    recv, smalls = grad_exchange(_b(gpack), small)
    own = lax.dynamic_index_in_dim(gpack, chip, axis=0, keepdims=False)
    half = partial_sum(own, recv)
    gshard = _unpack_shard(pair_sum(half, sibling_swap(half)))

    out_g, out_d, out_m, out_v = {}, {}, {}, {}
    for name, _, shp in _BIG:
        two_d = lambda a: a.reshape(-1, a.shape[-1])
        d, mn, vn = adam(two_d(wts[name]), two_d(gshard[name]), two_d(mom[name]), two_d(var[name]), "adam_" + name)
        out_g[name] = gshard[name]
        out_d[name], out_m[name], out_v[name] = (a.reshape(wts[name].shape) for a in (d, mn, vn))
    sg, sd, smn, svn = small_update(smalls, _pack_small(wts), _pack_small(mom), _pack_small(var))
    shapes = {n: wts[n].shape for n in WEIGHT_NAMES}
    for dst, src in ((out_g, sg), (out_d, sd), (out_m, smn), (out_v, svn)):
        dst.update(_unpack_small(src, shapes))
    loss = sg[LOSS_ROW, 0]
    return (loss, grad_x, *[out_g[n] for n in WEIGHT_NAMES], *[out_d[n] for n in WEIGHT_NAMES],
            *[out_m[n] for n in WEIGHT_NAMES], *[out_v[n] for n in WEIGHT_NAMES])
```

```python
import functools

import jax
import jax.numpy as jnp
from jax import lax
from jax.experimental import pallas as pl
from jax.experimental.pallas import tpu as pltpu

f32 = jnp.float32
bf16 = jnp.bfloat16

D = 1024
L = 64
SUB = 16
NORM_EPS = 1e-6
L2_EPS = 1e-6
MIN_F = 1e-30
HK = 128
QK_SCALE = HK ** -0.5
N_GDN = 8
N_HGRN = 4

OFF_A, OFF_B, OFF_CQKV, OFF_G, OFF_CZ, OFF_S = 0, 2048, 4096, 6144, 9216, 10240
NP = 10368
N_ORIG = 10256

ADAM_LR, ADAM_B1, ADAM_B2, ADAM_EPS, ADAM_WD, ADAM_STEP = 0.001, 0.9, 0.999, 1e-08, 0.01, 10

VMEM_LIMIT = 56 * 1024 * 1024


def _cp(sem):
    return pltpu.CompilerParams(dimension_semantics=sem, vmem_limit_bytes=VMEM_LIMIT)


def _sigmoid(x):
    return jax.nn.sigmoid(x)


def _silu(x):
    return x * _sigmoid(x)


def _dsilu(x):
    s = _sigmoid(x)
    return s * (1.0 + x * (1.0 - s))


def _softplus(x):
    u = jnp.exp(-jnp.abs(x))
    w = 1.0 + u
    l1p = jnp.where(w == 1.0, u, jnp.log(w) * (u / (w - 1.0)))
    return jnp.maximum(x, 0.0) + l1p


def _dot(a, b):
    return jnp.dot(a, b, preferred_element_type=f32)


def _dot_nt(a, b):
    return lax.dot_general(a, b, (((1,), (1,)), ((), ())), preferred_element_type=f32)


def _dot_tn(a, b):
    return lax.dot_general(a, b, (((0,), (0,)), ((), ())), preferred_element_type=f32)


def _bdot(a, b):
    return lax.dot_general(a, b, (((2,), (1,)), ((0,), (0,))), preferred_element_type=f32)


def _bdot_nt(a, b):
    return lax.dot_general(a, b, (((2,), (2,)), ((0,), (0,))), preferred_element_type=f32)


def _bdot_tn(a, b):
    return lax.dot_general(a, b, (((1,), (1,)), ((0,), (0,))), preferred_element_type=f32)


def _bdot_hi(a, b):
    return lax.dot_general(a, b, (((2,), (1,)), ((0,), (0,))), preferred_element_type=f32,
                           precision=lax.Precision.HIGHEST)


def _b(x):
    return x.astype(bf16)


def _chunk_cumsum(x, rows_in_chunk):
    n = x.shape[0]
    for s in (1, 2, 4, 8, 16, 32):
        x = x + jnp.where(rows_in_chunk >= s, pltpu.roll(x, s, axis=0), 0.0)
    return x


def _chunk_rev_cumsum(x, rows_in_chunk):
    n = x.shape[0]
    for s in (1, 2, 4, 8, 16, 32):
        x = x + jnp.where(rows_in_chunk + s < L, pltpu.roll(x, n - s, axis=0), 0.0)
    return x


def _shift_down(x, s):
    return pltpu.roll(x, s, axis=0) if s else x


def _shift_up(x, s):
    return pltpu.roll(x, x.shape[0] - s, axis=0) if s else x


def inproj_fwd(x, nw, w, l):
    T = x.shape[0]
    tT, tn = min(512, T), 1152

    def body(x_ref, nw_ref, w_ref, p_ref, h_ref, hs):
        @pl.when(pl.program_id(1) == 0)
        def _():
            xv = x_ref[...]
            r = lax.rsqrt(jnp.mean(xv * xv, axis=-1, keepdims=True) + NORM_EPS)
            hv = _b(xv * r * nw_ref[...])
            hs[...] = hv
            h_ref[...] = hv
        p_ref[...] = _dot(hs[...], w_ref[...])

    return pl.pallas_call(
        body, name="inproj_fwd", grid=(T // tT, NP // tn),
        in_specs=[pl.BlockSpec((tT, D), lambda i, j: (i, 0)), pl.BlockSpec((1, D), lambda i, j: (0, 0)),
                  pl.BlockSpec((None, D, tn), lambda i, j: (l, 0, j))],
        out_specs=[pl.BlockSpec((tT, tn), lambda i, j: (i, j)), pl.BlockSpec((tT, D), lambda i, j: (i, 0))],
        out_shape=[jax.ShapeDtypeStruct((T, NP), f32), jax.ShapeDtypeStruct((T, D), bf16)],
        scratch_shapes=[pltpu.VMEM((tT, D), bf16)],
        compiler_params=_cp(("parallel", "arbitrary")),
    )(x, nw, w)


def matmul_tn(a, b, name, n=None, b_col0=0, with_bf16=False):
    T, K = a.shape
    N = b.shape[1] if n is None else n
    tT = min(512, T)
    tn = 1152 if N % 1152 == 0 else min(N, 1024)
    nt = T // tT
    cb0 = b_col0 // tn

    def body(a_ref, b_ref, o_ref, *ob_ref):
        @pl.when(pl.program_id(1) == 0)
        def _():
            o_ref[...] = jnp.zeros_like(o_ref)
        o_ref[...] += _dot_tn(_b(a_ref[...]), _b(b_ref[...]))
        if with_bf16:
            @pl.when(pl.program_id(1) == nt - 1)
            def _():
                ob_ref[0][...] = _b(o_ref[...])

    ospec = pl.BlockSpec((K, tn), lambda j, t: (0, j))
    return pl.pallas_call(
        body, name=name, grid=(N // tn, nt),
        in_specs=[pl.BlockSpec((tT, K), lambda j, t: (t, 0)), pl.BlockSpec((tT, tn), lambda j, t: (t, cb0 + j))],
        out_specs=[ospec, ospec] if with_bf16 else ospec,
        out_shape=([jax.ShapeDtypeStruct((K, N), f32), jax.ShapeDtypeStruct((K, N), bf16)] if with_bf16
                   else jax.ShapeDtypeStruct((K, N), f32)),
        compiler_params=_cp(("parallel", "arbitrary")),
    )(a, b)


def inproj_bwd(dp, w, l, x, nw, dres):
    T = x.shape[0]
    tT, tk = min(512, T), 1152
    nk = NP // tk

    def body(dp_ref, w_ref, x_ref, nw_ref, dres_ref, dx_ref, dnw_ref, acc):
        i, k = pl.program_id(0), pl.program_id(1)

        @pl.when((i == 0) & (k == 0))
        def _():
            dnw_ref[...] = jnp.zeros_like(dnw_ref)

        @pl.when(k == 0)
        def _():
            acc[...] = jnp.zeros_like(acc)
        acc[...] += _dot_nt(dp_ref[...], w_ref[...])

        @pl.when(k == nk - 1)
        def _():
            xv = x_ref[...]
            r = lax.rsqrt(jnp.mean(xv * xv, axis=-1, keepdims=True) + NORM_EPS)
            xh = xv * r
            dy = acc[...]
            dyw = dy * nw_ref[...]
            dx_ref[...] = r * (dyw - xh * jnp.mean(dyw * xh, axis=-1, keepdims=True)) + dres_ref[...]
            dnw_ref[...] += jnp.sum(dy * xh, axis=0, keepdims=True)

    return pl.pallas_call(
        body, name="inproj_bwd", grid=(T // tT, nk),
        in_specs=[pl.BlockSpec((tT, tk), lambda i, k: (i, k)), pl.BlockSpec((None, D, tk), lambda i, k: (l, 0, k)),
                  pl.BlockSpec((tT, D), lambda i, k: (i, 0)), pl.BlockSpec((1, D), lambda i, k: (0, 0)),
                  pl.BlockSpec((tT, D), lambda i, k: (i, 0))],
        out_specs=[pl.BlockSpec((tT, D), lambda i, k: (i, 0)), pl.BlockSpec((1, D), lambda i, k: (0, 0))],
        out_shape=[jax.ShapeDtypeStruct((T, D), f32), jax.ShapeDtypeStruct((1, D), f32)],
        scratch_shapes=[pltpu.VMEM((tT, D), f32)],
        compiler_params=_cp(("arbitrary", "arbitrary")),
    )(dp, w, x, nw, dres)


def loss_head(x, fw, tgt):
    T = x.shape[0]
    tT = min(512, T)

    def body(x_ref, fw_ref, t_ref, loss_ref, dx_ref, dfw_ref):
        @pl.when(pl.program_id(0) == 0)
        def _():
            loss_ref[...] = jnp.zeros_like(loss_ref)
            dfw_ref[...] = jnp.zeros_like(dfw_ref)
        xv = x_ref[...]
        r = lax.rsqrt(jnp.mean(xv * xv, axis=-1, keepdims=True) + NORM_EPS)
        xh = xv * r
        err = xh * fw_ref[...] - t_ref[...]
        part = 0.5 * jnp.sum(jnp.mean(err * err, axis=-1, keepdims=True), axis=0, keepdims=True)
        loss_ref[...] += jnp.broadcast_to(part, loss_ref.shape)
        dy = err * (1.0 / D)
        dyw = dy * fw_ref[...]
        dx_ref[...] = r * (dyw - xh * jnp.mean(dyw * xh, axis=-1, keepdims=True))
        dfw_ref[...] += jnp.sum(dy * xh, axis=0, keepdims=True)

    return pl.pallas_call(
        body, name="loss_head", grid=(T // tT,),
        in_specs=[pl.BlockSpec((tT, D), lambda i: (i, 0)), pl.BlockSpec((1, D), lambda i: (0, 0)),
                  pl.BlockSpec((tT, D), lambda i: (i, 0))],
        out_specs=[pl.BlockSpec((1, 128), lambda i: (0, 0)), pl.BlockSpec((tT, D), lambda i: (i, 0)),
                   pl.BlockSpec((1, D), lambda i: (0, 0))],
        out_shape=[jax.ShapeDtypeStruct((1, 128), f32), jax.ShapeDtypeStruct((T, D), f32),
                   jax.ShapeDtypeStruct((1, D), f32)],
        compiler_params=_cp(("arbitrary",)),
    )(x, fw, tgt)


def _halo_specs(tT, T, width, colblk):
    nb8 = T // 8
    per = tT // 8
    prev = pl.BlockSpec((8, width), lambda i: (jnp.maximum(i * per - 1, 0), colblk))
    nxt = pl.BlockSpec((8, width), lambda i: (jnp.minimum((i + 1) * per, nb8 - 1), colblk))
    return prev, nxt


def mixa_fwd(p, cw):
    T = p.shape[0]
    tT = min(512, T)
    prev_spec, _ = _halo_specs(tT, T, 2048, OFF_A // 2048)

    def body(p_ref, pp_ref, cw_ref, y_ref):
        pv = p_ref[...]
        u = pv[:, 512:1024] * pv[:, 1024:1536]
        pp = pp_ref[...]
        up = jnp.where(pl.program_id(0) == 0, 0.0, pp[:, 512:1024] * pp[:, 1024:1536])
        ue = jnp.concatenate([up, u], axis=0)
        cv = cw_ref[0:1, :] * _shift_down(ue, 2) + cw_ref[1:2, :] * _shift_down(ue, 1) + cw_ref[2:3, :] * ue
        y_ref[...] = _b(pv[:, 0:512] * cv[8:] * _silu(pv[:, 1536:2048]))

    return pl.pallas_call(
        body, name="mixa_fwd", grid=(T // tT,),
        in_specs=[pl.BlockSpec((tT, 2048), lambda i: (i, OFF_A // 2048)), prev_spec,
                  pl.BlockSpec((3, 512), lambda i: (0, 0))],
        out_specs=pl.BlockSpec((tT, 512), lambda i: (i, 0)),
        out_shape=jax.ShapeDtypeStruct((T, 512), bf16),
        compiler_params=_cp(("parallel",)),
    )(p, p, cw)


def mixa_bwd(p, cw, dy):
    T = p.shape[0]
    tT = min(512, T)
    nt = T // tT
    prev_spec, next_spec = _halo_specs(tT, T, 2048, OFF_A // 2048)
    _, dnext_spec = _halo_specs(tT, T, 512, 0)

    def body(p_ref, pp_ref, pn_ref, cw_ref, dy_ref, dyn_ref, dp_ref, dcw_ref):
        i = pl.program_id(0)

        @pl.when(i == 0)
        def _():
            dcw_ref[...] = jnp.zeros_like(dcw_ref)
        pv, pp, pn = p_ref[...], pp_ref[...], pn_ref[...]
        pe = jnp.concatenate([pp, pv, pn], axis=0)
        rows = lax.broadcasted_iota(jnp.int32, (tT + 16, 1), 0)
        ab, ac, ax, az = pe[:, 0:512], pe[:, 512:1024], pe[:, 1024:1536], pe[:, 1536:2048]
        u = jnp.where((rows < 8) & (i == 0), 0.0, ac * ax)
        u1, u2 = _shift_down(u, 1), _shift_down(u, 2)
        w0, w1, w2 = cw_ref[0:1, :], cw_ref[1:2, :], cw_ref[2:3, :]
        cv = w0 * u2 + w1 * u1 + w2 * u
        dye = jnp.concatenate([jnp.zeros((8, 512), f32), dy_ref[...], dyn_ref[...]], axis=0)
        dye = jnp.where((rows >= tT + 8) & (i == nt - 1), 0.0, dye)
        sz = _silu(az)
        dcv = dye * ab * sz
        du = w2 * dcv + w1 * _shift_up(dcv, 1) + w0 * _shift_up(dcv, 2)
        inner = (rows >= 8) & (rows < tT + 8)
        dcv_in = jnp.where(inner, dcv, 0.0)
        dcw_ref[0:1, :] += jnp.sum(dcv_in * u2, axis=0, keepdims=True)
        dcw_ref[1:2, :] += jnp.sum(dcv_in * u1, axis=0, keepdims=True)
        dcw_ref[2:3, :] += jnp.sum(dcv_in * u, axis=0, keepdims=True)
        sl = slice(8, tT + 8)
        dp_ref[:, 0:512] = _b((dye * cv * sz)[sl])
        dp_ref[:, 512:1024] = _b((du * ax)[sl])
        dp_ref[:, 1024:1536] = _b((du * ac)[sl])
        dp_ref[:, 1536:2048] = _b((dye * ab * cv * _dsilu(az))[sl])

    return pl.pallas_call(
        body, name="mixa_bwd", grid=(nt,),
        in_specs=[pl.BlockSpec((tT, 2048), lambda i: (i, OFF_A // 2048)), prev_spec, next_spec,
                  pl.BlockSpec((3, 512), lambda i: (0, 0)),
                  pl.BlockSpec((tT, 512), lambda i: (i, 0)), dnext_spec],
        out_specs=[pl.BlockSpec((tT, 2048), lambda i: (i, 0)), pl.BlockSpec((8, 512), lambda i: (0, 0))],
        out_shape=[jax.ShapeDtypeStruct((T, 2048), bf16), jax.ShapeDtypeStruct((8, 512), f32)],
        compiler_params=_cp(("arbitrary",)),
    )(p, p, p, cw, dy, dy)


def _l2n_fwd(y):
    return y * lax.rsqrt(jnp.sum(y * y, axis=-1, keepdims=True) + L2_EPS)


def mixc_pre_fwd(p, cw, alog_l, dtb_l):
    T = p.shape[0]
    tT = min(512, T)
    prev_spec, _ = _halo_specs(tT, T, 2048, OFF_CQKV // 2048)

    def body(p_ref, pp_ref, ps_ref, cw_ref, al_ref, dt_ref, q_ref, k_ref, v_ref, sm_ref):
        pp = jnp.where(pl.program_id(0) == 0, 0.0, pp_ref[...])
        xe = jnp.concatenate([pp, p_ref[...]], axis=0)
        cv = (cw_ref[0:1, :] * _shift_down(xe, 3) + cw_ref[1:2, :] * _shift_down(xe, 2)
              + cw_ref[2:3, :] * _shift_down(xe, 1) + cw_ref[3:4, :] * xe)[8:]
        y = _silu(cv)
        for hh in range(4):
            sl = slice(hh * HK, (hh + 1) * HK)
            q_ref[:, sl] = _l2n_fwd(y[:, sl]) * QK_SCALE
            k_ref[:, sl] = _l2n_fwd(y[:, 512 + hh * HK:512 + (hh + 1) * HK])
        v_ref[...] = y[:, 1024:2048]
        ps = ps_ref[...]
        lane = lax.broadcasted_iota(jnp.int32, ps.shape, 1)
        la = -jnp.exp(al_ref[...]) * _softplus(ps + dt_ref[...])
        rin = lax.broadcasted_iota(jnp.int32, ps.shape, 0) % L
        g = _chunk_cumsum(la, rin)
        sm_ref[...] = jnp.where(lane < 8, _sigmoid(ps), jnp.where(lane < 16, g, 0.0))

    return pl.pallas_call(
        body, name="mixc_pre_fwd", grid=(T // tT,),
        in_specs=[pl.BlockSpec((tT, 2048), lambda i: (i, OFF_CQKV // 2048)), prev_spec,
                  pl.BlockSpec((tT, 128), lambda i: (i, OFF_S // 128)),
                  pl.BlockSpec((4, 2048), lambda i: (0, 0)),
                  pl.BlockSpec((1, 128), lambda i: (0, 0)), pl.BlockSpec((1, 128), lambda i: (0, 0))],
        out_specs=[pl.BlockSpec((tT, 512), lambda i: (i, 0)), pl.BlockSpec((tT, 512), lambda i: (i, 0)),
                   pl.BlockSpec((tT, 1024), lambda i: (i, 0)), pl.BlockSpec((tT, 128), lambda i: (i, 0))],
        out_shape=[jax.ShapeDtypeStruct((T, 512), f32), jax.ShapeDtypeStruct((T, 512), f32),
                   jax.ShapeDtypeStruct((T, 1024), f32), jax.ShapeDtypeStruct((T, 128), f32)],
        compiler_params=_cp(("parallel",)),
    )(p, p, p, cw, alog_l, dtb_l)


def mixc_pre_bwd(p, cw, alog_l, dtb_l, dq8, dk8, dv, dsm8):
    T = p.shape[0]
    tT = min(256, T)
    nt = T // tT
    prev_spec, next_spec = _halo_specs(tT, T, 2048, OFF_CQKV // 2048)
    _, n1024 = _halo_specs(tT, T, 1024, 0)

    def body(p_ref, pp_ref, pn_ref, ps_ref, cw_ref, al_ref, dt_ref, dq_ref, dqn_ref, dk_ref, dkn_ref,
             dv_ref, dvn_ref, dsm_ref, dp_ref, dps_ref, dcw_ref, dsml_ref):
        i = pl.program_id(0)

        @pl.when(i == 0)
        def _():
            dcw_ref[...] = jnp.zeros_like(dcw_ref)
            dsml_ref[...] = jnp.zeros_like(dsml_ref)
        rows = lax.broadcasted_iota(jnp.int32, (tT + 16, 1), 0)
        pp = jnp.where(i == 0, 0.0, pp_ref[...])
        xe = jnp.concatenate([pp, p_ref[...], pn_ref[...]], axis=0)
        xs = [_shift_down(xe, 3), _shift_down(xe, 2), _shift_down(xe, 1), xe]
        cv = cw_ref[0:1, :] * xs[0] + cw_ref[1:2, :] * xs[1] + cw_ref[2:3, :] * xs[2] + cw_ref[3:4, :] * xs[3]
        y = _silu(cv)
        last = (rows >= tT + 8) & (i == nt - 1)
        z8q = jnp.zeros((8, 1024), f32)

        def ext(cur_ref, nxt_ref):
            return jnp.where(last, 0.0, jnp.concatenate([z8q, cur_ref[...], nxt_ref[...]], axis=0))
        dq8e, dk8e, dve = ext(dq_ref, dqn_ref), ext(dk_ref, dkn_ref), ext(dv_ref, dvn_ref)
        dys = []
        for (d8, base, scale) in ((dq8e, 0, QK_SCALE), (dk8e, 512, 1.0)):
            for hh in range(4):
                dn = (d8[:, (2 * hh) * HK:(2 * hh + 1) * HK] + d8[:, (2 * hh + 1) * HK:(2 * hh + 2) * HK]) * scale
                yh = y[:, base + hh * HK:base + (hh + 1) * HK]
                r = lax.rsqrt(jnp.sum(yh * yh, axis=-1, keepdims=True) + L2_EPS)
                nh = yh * r
                dys.append(r * (dn - nh * jnp.sum(dn * nh, axis=-1, keepdims=True)))
        dyy = jnp.concatenate(dys + [dve], axis=1)
        dcv = dyy * _dsilu(cv)
        dx = (cw_ref[3:4, :] * dcv + cw_ref[2:3, :] * _shift_up(dcv, 1) + cw_ref[1:2, :] * _shift_up(dcv, 2)
              + cw_ref[0:1, :] * _shift_up(dcv, 3))
        dp_ref[...] = _b(dx[8:tT + 8])
        inner = (rows >= 8) & (rows < tT + 8)
        dcv_in = jnp.where(inner, dcv, 0.0)
        for j in range(4):
            dcw_ref[j:j + 1, :] += jnp.sum(dcv_in * xs[j], axis=0, keepdims=True)
        ps = ps_ref[...]
        lane = lax.broadcasted_iota(jnp.int32, ps.shape, 1)
        dsm = dsm_ref[:, 0:128]
        for hh in range(1, N_GDN):
            dsm = dsm + dsm_ref[:, hh * 128:(hh + 1) * 128]
        beta = _sigmoid(ps)
        xa = ps + dt_ref[...]
        nea = -jnp.exp(al_ref[...])
        dpa = dsm * nea * _sigmoid(xa)
        dps_ref[...] = _b(jnp.where(lane < 8, dsm * beta * (1.0 - beta), jnp.where(lane < 16, dpa, 0.0)))
        amask = (lane >= 8) & (lane < 16)
        dsml_ref[0:1, :] += jnp.sum(jnp.where(amask, dsm * nea * _softplus(xa), 0.0), axis=0, keepdims=True)
        dsml_ref[1:2, :] += jnp.sum(jnp.where(amask, dpa, 0.0), axis=0, keepdims=True)

    cur1024 = pl.BlockSpec((tT, 1024), lambda i: (i, 0))
    return pl.pallas_call(
        body, name="mixc_pre_bwd", grid=(nt,),
        in_specs=[pl.BlockSpec((tT, 2048), lambda i: (i, OFF_CQKV // 2048)), prev_spec, next_spec,
                  pl.BlockSpec((tT, 128), lambda i: (i, OFF_S // 128)),
                  pl.BlockSpec((4, 2048), lambda i: (0, 0)),
                  pl.BlockSpec((1, 128), lambda i: (0, 0)), pl.BlockSpec((1, 128), lambda i: (0, 0)),
                  cur1024, n1024, cur1024, n1024, cur1024, n1024, cur1024],
        out_specs=[pl.BlockSpec((tT, 2048), lambda i: (i, 0)), pl.BlockSpec((tT, 128), lambda i: (i, 0)),
                   pl.BlockSpec((8, 2048), lambda i: (0, 0)), pl.BlockSpec((8, 128), lambda i: (0, 0))],
        out_shape=[jax.ShapeDtypeStruct((T, 2048), bf16), jax.ShapeDtypeStruct((T, 128), bf16),
                   jax.ShapeDtypeStruct((8, 2048), f32), jax.ShapeDtypeStruct((8, 128), f32)],
        compiler_params=_cp(("arbitrary",)),
    )(p, p, p, p, cw, alog_l, dtb_l, dq8, dq8, dk8, dk8, dv, dv, dsm8)


def _tri_inverse(m):
    r = lax.broadcasted_iota(jnp.int32, (L, L), 0)
    c = lax.broadcasted_iota(jnp.int32, (L, L), 1)
    eye = (r == c).astype(f32)[None]
    same = lambda w: ((r // w) == (c // w))[None]
    md = jnp.where(same(8), m, 0.0)
    m2 = _bdot_hi(md, md)
    m4 = _bdot_hi(m2, m2)
    t = _bdot_hi(_bdot_hi(eye - md, eye + m2), eye + m4)
    for w in (16, 32, 64):
        mo = jnp.where(same(w) & jnp.logical_not(same(w // 2)), m, 0.0)
        t = t - _bdot_hi(_bdot_hi(t, mo), t)
    return t


def _col_to_row(col, eye):
    return jnp.sum(eye * col, axis=1, keepdims=True)


def _row_to_col(row, eye):
    return jnp.sum(eye * row, axis=2, keepdims=True)


def _gdn_chunk_terms(q, k, v, beta, g, t_inv=None):
    r = lax.broadcasted_iota(jnp.int32, (L, L), 0)
    c = lax.broadcasted_iota(jnp.int32, (L, L), 1)
    eye = (r == c).astype(f32)[None]
    causal, strict = (c <= r)[None], (c < r)[None]
    diff = g - _col_to_row(g, eye)
    dec = jnp.exp(jnp.where(causal, diff, 0.0))
    dc = jnp.where(causal, dec, 0.0)
    ds = jnp.where(strict, dec, 0.0)
    eg = jnp.exp(g)
    gl = g[:, L - 1:L, :]
    egl = jnp.exp(gl - g)
    kb = k * beta
    kk = _bdot_nt(_b(k), _b(kb))
    qk = _bdot_nt(_b(q), _b(kb))
    m = kk * ds
    aqk = qk * dc
    if t_inv is None:
        t_inv = _tri_inverse(m)
    tb = _b(t_inv)
    keg = k * eg
    u = _bdot(tb, _b(v))
    w = _bdot(tb, _b(keg))
    return dict(eye=eye, causal=causal, strict=strict, dc=dc, ds=ds, eg=eg, gl=gl, egl=egl, kb=kb, kk=kk, qk=qk,
                m=m, aqk=aqk, t=t_inv, u=u, w=w, qi=q * eg, ks=kb * egl, keg=keg)


def gdn_fwd(qn, kn, vv, sm):
    T = qn.shape[0]
    tB = min(256, T)
    nc = tB // L
    N = T // L

    def body(q_ref, k_ref, v_ref, sm_ref, o_ref, st_ref, ti_ref, s_scr):
        h = pl.program_id(0)

        @pl.when(pl.program_id(1) == 0)
        def _():
            s_scr[...] = jnp.zeros_like(s_scr)
        smv = sm_ref[...]
        lane = lax.broadcasted_iota(jnp.int32, smv.shape, 1)
        beta = jnp.sum(jnp.where(lane == h, smv, 0.0), axis=1, keepdims=True).reshape(nc, L, 1)
        g = jnp.sum(jnp.where(lane == 8 + h, smv, 0.0), axis=1, keepdims=True).reshape(nc, L, 1)
        q = q_ref[...].reshape(nc, L, HK)
        k = k_ref[...].reshape(nc, L, HK)
        v = v_ref[...].reshape(nc, L, HK)
        tm = _gdn_chunk_terms(q, k, v, beta, g)
        ti_ref[...] = tm["t"]
        s = s_scr[...]
        for ci in range(nc):
            st_ref[ci] = s
            sb = _b(s)
            e = tm["u"][ci] - _dot(_b(tm["w"][ci]), sb)
            eb = _b(e)
            o_ref[ci * L:(ci + 1) * L, :] = _dot(_b(tm["qi"][ci]), sb) + _dot(_b(tm["aqk"][ci]), eb)
            s = jnp.exp(tm["gl"][ci]) * s + _dot_tn(_b(tm["ks"][ci]), eb)
        s_scr[...] = s

    return pl.pallas_call(
        body, name="gdn_fwd", grid=(N_GDN, T // tB),
        in_specs=[pl.BlockSpec((tB, HK), lambda h, n: (n, h // 2)), pl.BlockSpec((tB, HK), lambda h, n: (n, h // 2)),
                  pl.BlockSpec((tB, HK), lambda h, n: (n, h)), pl.BlockSpec((tB, 128), lambda h, n: (n, 0))],
        out_specs=[pl.BlockSpec((tB, HK), lambda h, n: (n, h)),
                   pl.BlockSpec((None, nc, HK, HK), lambda h, n: (h, n, 0, 0)),
                   pl.BlockSpec((None, nc, L, L), lambda h, n: (h, n, 0, 0))],
        out_shape=[jax.ShapeDtypeStruct((T, N_GDN * HK), f32), jax.ShapeDtypeStruct((N_GDN, N, HK, HK), f32),
                   jax.ShapeDtypeStruct((N_GDN, N, L, L), f32)],
        scratch_shapes=[pltpu.VMEM((HK, HK), f32)],
        compiler_params=_cp(("parallel", "arbitrary")),
    )(qn, kn, vv, sm)


def gdn_bwd(qn, kn, vv, sm, st, ti, do):
    T = qn.shape[0]
    tB = min(256, T)
    nc = tB // L
    nb = T // tB

    def body(q_ref, k_ref, v_ref, sm_ref, st_ref, ti_ref, do_ref, dq_ref, dk_ref, dv_ref, dsm_ref, ds_scr):
        h = pl.program_id(0)

        @pl.when(pl.program_id(1) == 0)
        def _():
            ds_scr[...] = jnp.zeros_like(ds_scr)
        smv = sm_ref[...]
        lane = lax.broadcasted_iota(jnp.int32, smv.shape, 1)
        beta = jnp.sum(jnp.where(lane == h, smv, 0.0), axis=1, keepdims=True).reshape(nc, L, 1)
        g = jnp.sum(jnp.where(lane == 8 + h, smv, 0.0), axis=1, keepdims=True).reshape(nc, L, 1)
        q = q_ref[...].reshape(nc, L, HK)
        k = k_ref[...].reshape(nc, L, HK)
        v = v_ref[...].reshape(nc, L, HK)
        do = do_ref[...].reshape(nc, L, HK)
        s = st_ref[...]
        tm = _gdn_chunk_terms(q, k, v, beta, g, t_inv=ti_ref[...])
        eye, dc, ds_, eg, egl = tm["eye"], tm["dc"], tm["ds"], tm["eg"], tm["egl"]
        kb, u, w, qi, ks, tinv = tm["kb"], tm["u"], tm["w"], tm["qi"], tm["ks"], tm["t"]
        sb, dob = _b(s), _b(do)
        e = u - _bdot(_b(w), sb)
        eb = _b(e)
        egl_last = jnp.exp(tm["gl"])
        de0 = _bdot_tn(_b(tm["aqk"]), dob)
        ds0 = _bdot_tn(_b(qi), dob)
        dsn = ds_scr[...]
        des, dsns = [None] * nc, [None] * nc
        for ci in reversed(range(nc)):
            dsns[ci] = dsn
            de = de0[ci] + _dot(_b(ks[ci]), _b(dsn))
            des[ci] = de
            dsn = ds0[ci] + egl_last[ci] * dsn - _dot_tn(_b(w[ci]), _b(de))
        ds_scr[...] = dsn
        de = jnp.stack(des, axis=0)
        dsp = jnp.stack(dsns, axis=0)
        deb, dspb = _b(de), _b(dsp)
        dks = _bdot_nt(eb, dspb)
        dqi = _bdot_nt(dob, sb)
        daqk = jnp.where(tm["causal"], _bdot_nt(dob, eb), 0.0)
        dw = -_bdot_nt(deb, sb)
        tb = _b(tinv)
        dvv = _bdot_tn(tb, deb)
        dkg = _bdot_tn(tb, _b(dw))
        dm = -jnp.where(tm["strict"], _bdot_nt(_b(dvv), _b(u)) + _bdot_nt(_b(dkg), _b(w)), 0.0)
        x = _b(dm * ds_)
        y = _b(daqk * dc)
        kbb, kbf, qbf = _b(kb), _b(k), _b(q)
        dk = _bdot(x, kbb) + dkg * eg
        dkb = _bdot_tn(x, kbf) + _bdot_tn(y, qbf) + dks * egl
        dq = _bdot(y, kbb) + dqi * eg
        dk = dk + dkb * beta
        dbeta = jnp.sum(dkb * k, axis=-1, keepdims=True)
        z = dm * tm["m"] + daqk * tm["aqk"]
        dg = (jnp.sum(dqi * qi - dks * ks + dkg * tm["keg"], axis=-1, keepdims=True)
              + jnp.sum(z, axis=-1, keepdims=True) - _row_to_col(jnp.sum(z, axis=1, keepdims=True), eye))
        dgl = (egl_last * jnp.sum(jnp.sum(s * dsp, axis=2, keepdims=True), axis=1, keepdims=True)
               + jnp.sum(jnp.sum(dks * ks, axis=2, keepdims=True), axis=1, keepdims=True))
        rowi = lax.broadcasted_iota(jnp.int32, (nc, L, 1), 1)
        dg = dg + jnp.where(rowi == L - 1, dgl, 0.0)
        dg2 = dg.reshape(tB, 1)
        rin = lax.broadcasted_iota(jnp.int32, (tB, 1), 0) % L
        dla = _chunk_rev_cumsum(jnp.broadcast_to(dg2, (tB, 128)), jnp.broadcast_to(rin, (tB, 128)))
        dq_ref[...] = dq.reshape(tB, HK)
        dk_ref[...] = dk.reshape(tB, HK)
        dv_ref[...] = dvv.reshape(tB, HK)
        dsm_ref[...] = jnp.where(lane == h, dbeta.reshape(tB, 1), jnp.where(lane == 8 + h, dla, 0.0))

    rev = lambda n: nb - 1 - n
    return pl.pallas_call(
        body, name="gdn_bwd", grid=(N_GDN, nb),
        in_specs=[pl.BlockSpec((tB, HK), lambda h, n: (rev(n), h // 2)),
                  pl.BlockSpec((tB, HK), lambda h, n: (rev(n), h // 2)),
                  pl.BlockSpec((tB, HK), lambda h, n: (rev(n), h)), pl.BlockSpec((tB, 128), lambda h, n: (rev(n), 0)),
                  pl.BlockSpec((None, nc, HK, HK), lambda h, n: (h, rev(n), 0, 0)),
                  pl.BlockSpec((None, nc, L, L), lambda h, n: (h, rev(n), 0, 0)),
                  pl.BlockSpec((tB, HK), lambda h, n: (rev(n), h))],
        out_specs=[pl.BlockSpec((tB, HK), lambda h, n: (rev(n), h))] * 4,
        out_shape=[jax.ShapeDtypeStruct((T, N_GDN * HK), f32)] * 4,
        scratch_shapes=[pltpu.VMEM((HK, HK), f32)],
        compiler_params=_cp(("parallel", "arbitrary")),
    )(qn, kn, vv, sm, st, ti, do)


def _hgrn_prep(bq, bf_, bi, lb):
    tB = bq.shape[0]
    q = _silu(bq) * QK_SCALE
    sg = _sigmoid(bf_)
    f = lb + (1.0 - lb) * sg
    logf = jnp.log(jnp.maximum(f, MIN_F))
    rin = lax.broadcasted_iota(jnp.int32, (tB, HK), 0) % L
    g = _chunk_cumsum(logf, rin)
    return q, sg, f, 1.0 - f, bi, g, rin


def _hgrn_intra(q, kk, v, g, do=None):
    n = q.shape[0]
    nsub = L // SUB
    bwd = do is not None
    o_rows = [None] * nsub
    if bwd:
        dq_rows = [None] * nsub
        dkk_acc = jnp.zeros_like(kk)
        dv_acc = jnp.zeros_like(v)
    for i in range(1, nsub):
        lo, hi, w = i * SUB, (i + 1) * SUB, i * SUB
        ref = g[:, lo - 1:lo, :]
        eq = jnp.exp(g[:, lo:hi, :] - ref)
        ek = jnp.exp(ref - g[:, :w, :])
        qs = _b(q[:, lo:hi, :] * eq)
        ks = _b(kk[:, :w, :] * ek)
        p = _bdot_nt(qs, ks)
        o_rows[i] = _bdot(_b(p), _b(v[:, :w, :]))
        if bwd:
            dob = _b(do[:, lo:hi, :])
            dp = _b(_bdot_nt(dob, _b(v[:, :w, :])))
            dq_rows[i] = _bdot(dp, ks) * eq
            pad = jnp.zeros((n, L - w, HK), f32)
            dkk_acc = dkk_acc + jnp.concatenate([_bdot_tn(dp, qs) * ek, pad], axis=1)
            dv_acc = dv_acc + jnp.concatenate([_bdot_tn(_b(p), dob), pad], axis=1)
    m = n * nsub
    q4, k4, v4, g4 = (a.reshape(m, SUB, HK) for a in (q, kk, v, g))
    r = lax.broadcasted_iota(jnp.int32, (m, SUB, HK), 1)
    od = jnp.zeros((m, SUB, HK), f32)
    if bwd:
        do4 = do.reshape(m, SUB, HK)
        dqd = jnp.zeros((m, SUB, HK), f32)
        dkd = jnp.zeros((m, SUB, HK), f32)
        dvd = jnp.zeros((m, SUB, HK), f32)
    for j in range(SUB):
        gj, kj, vj = g4[:, j:j + 1, :], k4[:, j:j + 1, :], v4[:, j:j + 1, :]
        ok = r >= j
        e = jnp.where(ok, jnp.exp(jnp.where(ok, g4 - gj, 0.0)), 0.0)
        xq = q4 * e
        pj = jnp.sum(xq * kj, axis=-1, keepdims=True)
        od = od + pj * vj
        if bwd:
            dpj = jnp.sum(do4 * vj, axis=-1, keepdims=True)
            dqd = dqd + dpj * kj * e
            dkd = dkd + jnp.where(r == j, jnp.sum(dpj * xq, axis=1, keepdims=True), 0.0)
            dvd = dvd + jnp.where(r == j, jnp.sum(pj * do4, axis=1, keepdims=True), 0.0)
    od = od.reshape(n, L, HK)
    o = jnp.concatenate([od[:, :SUB, :]] + [od[:, i * SUB:(i + 1) * SUB, :] + o_rows[i] for i in range(1, nsub)], axis=1)
    if not bwd:
        return o
    dqd = dqd.reshape(n, L, HK)
    dq = jnp.concatenate([dqd[:, :SUB, :]] + [dqd[:, i * SUB:(i + 1) * SUB, :] + dq_rows[i] for i in range(1, nsub)], axis=1)
    return o, dq, dkk_acc + dkd.reshape(n, L, HK), dv_acc + dvd.reshape(n, L, HK)


def hgrn_fwd(p, lbs):
    T = p.shape[0]
    tB = min(256, T)
    nc = tB // L
    N = T // L
    cq, cf, ci_ = OFF_B // HK, (OFF_B + 512) // HK, (OFF_B + 1024) // HK

    def body(bq_ref, bf_ref, bi_ref, lb_ref, o_ref, st_ref, s_scr):
        @pl.when(pl.program_id(1) == 0)
        def _():
            s_scr[...] = jnp.zeros_like(s_scr)
        q, sg, f, kk, v, g, rin = _hgrn_prep(bq_ref[...], bf_ref[...], bi_ref[...], lb_ref[...])
        q3, k3, v3, g3 = (a.reshape(nc, L, HK) for a in (q, kk, v, g))
        o = _hgrn_intra(q3, k3, v3, g3)
        gl = g3[:, L - 1:L, :]
        qt = _b(q3 * jnp.exp(g3))
        kt = _b(k3 * jnp.exp(gl - g3))
        vb = _b(v3)
        st = s_scr[...]
        for c in range(nc):
            st_ref[c] = st
            o_ref[c * L:(c + 1) * L, :] = o[c] + _dot_nt(qt[c], _b(st))
            st = st * jnp.exp(gl[c]) + _dot_tn(vb[c], kt[c])
        s_scr[...] = st

    return pl.pallas_call(
        body, name="hgrn_fwd", grid=(N_HGRN, T // tB),
        in_specs=[pl.BlockSpec((tB, HK), lambda h, n: (n, cq + h)), pl.BlockSpec((tB, HK), lambda h, n: (n, cf + h)),
                  pl.BlockSpec((tB, HK), lambda h, n: (n, ci_ + h)), pl.BlockSpec((1, HK), lambda h, n: (0, h))],
        out_specs=[pl.BlockSpec((tB, HK), lambda h, n: (n, h)),
                   pl.BlockSpec((None, nc, HK, HK), lambda h, n: (h, n, 0, 0))],
        out_shape=[jax.ShapeDtypeStruct((T, N_HGRN * HK), f32), jax.ShapeDtypeStruct((N_HGRN, N, HK, HK), f32)],
        scratch_shapes=[pltpu.VMEM((HK, HK), f32)],
        compiler_params=_cp(("parallel", "arbitrary")),
    )(p, p, p, lbs)


def hgrn_bwd(p, lbs, st, do):
    T = p.shape[0]
    tB = min(256, T)
    nc = tB // L
    nb = T // tB
    cq, cf, ci_ = OFF_B // HK, (OFF_B + 512) // HK, (OFF_B + 1024) // HK

    def body(bq_ref, bf_ref, bi_ref, lb_ref, st_ref, do_ref, dbq_ref, dbf_ref, dbi_ref, dlb_ref, ds_scr):
        @pl.when(pl.program_id(1) == 0)
        def _():
            ds_scr[...] = jnp.zeros_like(ds_scr)
            dlb_ref[...] = jnp.zeros_like(dlb_ref)
        lb = lb_ref[...]
        bq = bq_ref[...]
        q, sg, f, kk, v, g, rin = _hgrn_prep(bq, bf_ref[...], bi_ref[...], lb)
        q3, k3, v3, g3 = (a.reshape(nc, L, HK) for a in (q, kk, v, g))
        do3 = do_ref[...].reshape(nc, L, HK)
        dob = _b(do3)
        gl = g3[:, L - 1:L, :]
        egl = jnp.exp(gl)
        eg, egr = jnp.exp(g3), jnp.exp(gl - g3)
        qt, kt = q3 * eg, k3 * egr
        s = st_ref[...]
        ds0 = _bdot_tn(dob, _b(qt))
        dsn = ds_scr[...]
        dsns = [None] * nc
        for c in reversed(range(nc)):
            dsns[c] = dsn
            dsn = ds0[c] + dsn * egl[c]
        ds_scr[...] = dsn
        dsp = jnp.stack(dsns, axis=0)
        dspb = _b(dsp)
        dqt = _bdot(dob, _b(s))
        dkt = _bdot(_b(v3), dspb)
        dv_state = _bdot_nt(_b(kt), dspb)
        dgl = egl * jnp.sum(s * dsp, axis=1, keepdims=True) + jnp.sum(dkt * kt, axis=1, keepdims=True)
        _, dq_i, dkk_i, dv_i = _hgrn_intra(q3, k3, v3, g3, do=do3)
        dq = dq_i + dqt * eg
        dkk = dkk_i + dkt * egr
        dv = dv_i + dv_state
        rowi = lax.broadcasted_iota(jnp.int32, (nc, L, HK), 1)
        dg = q3 * dq - k3 * dkk + jnp.where(rowi == L - 1, dgl, 0.0)
        dlogf = _chunk_rev_cumsum(dg.reshape(tB, HK), rin)
        dkk2 = dkk.reshape(tB, HK)
        df = jnp.where(f > MIN_F, dlogf / f, 0.0) - dkk2
        dlb_ref[...] += jnp.sum(df * (1.0 - sg), axis=0, keepdims=True)
        dbf_ref[...] = _b(df * (1.0 - lb) * sg * (1.0 - sg))
        dbq_ref[...] = _b(dq.reshape(tB, HK) * QK_SCALE * _dsilu(bq))
        dbi_ref[...] = _b(dv.reshape(tB, HK))

    rev = lambda n: nb - 1 - n
    return pl.pallas_call(
        body, name="hgrn_bwd", grid=(N_HGRN, nb),
        in_specs=[pl.BlockSpec((tB, HK), lambda h, n: (rev(n), cq + h)),
                  pl.BlockSpec((tB, HK), lambda h, n: (rev(n), cf + h)),
                  pl.BlockSpec((tB, HK), lambda h, n: (rev(n), ci_ + h)), pl.BlockSpec((1, HK), lambda h, n: (0, h)),
                  pl.BlockSpec((None, nc, HK, HK), lambda h, n: (h, rev(n), 0, 0)),
                  pl.BlockSpec((tB, HK), lambda h, n: (rev(n), h))],
        out_specs=[pl.BlockSpec((tB, HK), lambda h, n: (rev(n), h))] * 3 + [pl.BlockSpec((1, HK), lambda h, n: (0, h))],
        out_shape=[jax.ShapeDtypeStruct((T, N_HGRN * HK), bf16)] * 3 + [jax.ShapeDtypeStruct((1, N_HGRN * HK), f32)],
        scratch_shapes=[pltpu.VMEM((HK, HK), f32)],
        compiler_params=_cp(("parallel", "arbitrary")),
    )(p, p, p, lbs, st, do)


def _headnorm_fwd(o, z, w, nheads):
    outs = []
    for hh in range(nheads):
        sl = slice(hh * HK, (hh + 1) * HK)
        oh = o[:, sl]
        r = lax.rsqrt(jnp.mean(oh * oh, axis=-1, keepdims=True) + NORM_EPS)
        outs.append(oh * r * w * _silu(z[:, sl]))
    return jnp.concatenate(outs, axis=1)


def _headnorm_bwd(o, z, w, dy, nheads):
    dos, dzs = [], []
    dw = jnp.zeros((1, HK), f32)
    for hh in range(nheads):
        sl = slice(hh * HK, (hh + 1) * HK)
        oh, zh, dyh = o[:, sl], z[:, sl], dy[:, sl]
        r = lax.rsqrt(jnp.mean(oh * oh, axis=-1, keepdims=True) + NORM_EPS)
        on = oh * r
        sz = _silu(zh)
        dn = dyh * sz * w
        dos.append(r * (dn - on * jnp.mean(dn * on, axis=-1, keepdims=True)))
        dzs.append(dyh * on * w * _dsilu(zh))
        dw = dw + jnp.sum(dyh * sz * on, axis=0, keepdims=True)
    return jnp.concatenate(dos, axis=1), jnp.concatenate(dzs, axis=1), dw


def _merge_specs(tT, l):
    row = lambda w, cb=0: pl.BlockSpec((tT, w), lambda i, cb=cb: (i, cb))
    full = lambda r, c: pl.BlockSpec((r, c), lambda i: (0, 0))
    layer = lambda r, c: pl.BlockSpec((None, r, c), lambda i: (l, 0, 0))
    return row, full, layer


def merge_fwd(x, p, ya, ob, oc, hw, gw, bg, woa, wob, woc, wo, l):
    T = x.shape[0]
    tT = min(256, T)
    row, full, layer = _merge_specs(tT, l)

    def body(x_ref, bz_ref, cz_ref, g_ref, ya_ref, ob_ref, oc_ref, hw_ref, gw_ref, bg_ref,
             woa_ref, wob_ref, woc_ref, wo_ref, out_ref):
        yb = _b(_headnorm_fwd(ob_ref[...], bz_ref[...], hw_ref[...], N_HGRN))
        yc = _b(_headnorm_fwd(oc_ref[...], cz_ref[...], gw_ref[...], N_GDN))
        gates = _sigmoid(g_ref[...] + bg_ref[...])
        merged = (gates[:, 0:D] * _dot(ya_ref[...], woa_ref[...]) + gates[:, D:2 * D] * _dot(yb, wob_ref[...])
                  + gates[:, 2 * D:3 * D] * _dot(yc, woc_ref[...]))
        out_ref[...] = x_ref[...] + _dot(_b(merged), wo_ref[...])

    return pl.pallas_call(
        body, name="merge_fwd", grid=(T // tT,),
        in_specs=[row(D), row(512, (OFF_B + 1536) // 512), row(1024, OFF_CZ // 1024), row(3072, OFF_G // 3072),
                  row(512), row(512), row(1024), full(1, HK), full(1, HK), full(1, 3 * D),
                  layer(512, D), layer(512, D), layer(D, D), layer(D, D)],
        out_specs=row(D),
        out_shape=jax.ShapeDtypeStruct((T, D), f32),
        compiler_params=_cp(("parallel",)),
    )(x, p, p, p, ya, ob, oc, hw, gw, bg, woa, wob, woc, wo)


def merge_bwd(dxo, p, ya, ob, oc, hw, gw, bg, woa, wob, woc, wo, l):
    T = dxo.shape[0]
    tT = min(256, T)
    row, full, layer = _merge_specs(tT, l)

    def body(dx_ref, bz_ref, cz_ref, g_ref, ya_ref, ob_ref, oc_ref, hw_ref, gw_ref, bg_ref,
             woa_ref, wob_ref, woc_ref, wo_ref,
             dya_ref, dob_ref, doc_ref, dbz_ref, dcz_ref, dg_ref, mg_ref, dy3_ref, yb_ref, yc_ref,
             dbg_ref, dhw_ref, dgw_ref):
        @pl.when(pl.program_id(0) == 0)
        def _():
            dbg_ref[...] = jnp.zeros_like(dbg_ref)
            dhw_ref[...] = jnp.zeros_like(dhw_ref)
            dgw_ref[...] = jnp.zeros_like(dgw_ref)
        ob, oc, bz, cz = ob_ref[...], oc_ref[...], bz_ref[...], cz_ref[...]
        hw_, gw_ = hw_ref[...], gw_ref[...]
        yb = _b(_headnorm_fwd(ob, bz, hw_, N_HGRN))
        yc = _b(_headnorm_fwd(oc, cz, gw_, N_GDN))
        yb_ref[...] = yb
        yc_ref[...] = yc
        gates = _sigmoid(g_ref[...] + bg_ref[...])
        ys = (_dot(ya_ref[...], woa_ref[...]), _dot(yb, wob_ref[...]), _dot(yc, woc_ref[...]))
        dmerged = _dot_nt(_b(dx_ref[...]), wo_ref[...])
        merged = jnp.zeros_like(dmerged)
        dys = []
        for i in range(3):
            gi = gates[:, i * D:(i + 1) * D]
            merged = merged + gi * ys[i]
            dyi = _b(dmerged * gi)
            dys.append(dyi)
            dy3_ref[:, i * D:(i + 1) * D] = dyi
            dgp = dmerged * ys[i] * gi * (1.0 - gi)
            dg_ref[:, i * D:(i + 1) * D] = _b(dgp)
            dbg_ref[:, i * D:(i + 1) * D] += jnp.sum(dgp, axis=0, keepdims=True)
        mg_ref[...] = _b(merged)
        dya_ref[...] = _dot_nt(dys[0], woa_ref[...])
        dob, dbz, dhw = _headnorm_bwd(ob, bz, hw_, _dot_nt(dys[1], wob_ref[...]), N_HGRN)
        doc, dcz, dgw = _headnorm_bwd(oc, cz, gw_, _dot_nt(dys[2], woc_ref[...]), N_GDN)
        dob_ref[...] = dob
        doc_ref[...] = doc
        dbz_ref[...] = _b(dbz)
        dcz_ref[...] = _b(dcz)
        dhw_ref[...] += dhw
        dgw_ref[...] += dgw

    sd = jax.ShapeDtypeStruct
    return pl.pallas_call(
        body, name="merge_bwd", grid=(T // tT,),
        in_specs=[row(D), row(512, (OFF_B + 1536) // 512), row(1024, OFF_CZ // 1024), row(3072, OFF_G // 3072),
                  row(512), row(512), row(1024), full(1, HK), full(1, HK), full(1, 3 * D),
                  layer(512, D), layer(512, D), layer(D, D), layer(D, D)],
        out_specs=[row(512), row(512), row(1024), row(512), row(1024), row(3072), row(D), row(3 * D), row(512),
                   row(1024), full(1, 3 * D), full(1, HK), full(1, HK)],
        out_shape=[sd((T, 512), f32), sd((T, 512), f32), sd((T, 1024), f32), sd((T, 512), bf16), sd((T, 1024), bf16),
                   sd((T, 3072), bf16), sd((T, D), bf16), sd((T, 3 * D), bf16), sd((T, 512), bf16),
                   sd((T, 1024), bf16), sd((1, 3 * D), f32), sd((1, HK), f32), sd((1, HK), f32)],
        compiler_params=_cp(("arbitrary",)),
    )(dxo, p, p, p, ya, ob, oc, hw, gw, bg, woa, wob, woc, wo)


def layer_fwd(x, w):
    l = w["l"]
    p, h = inproj_fwd(x, w["norm_w"], w["w_in"], l)
    ya = mixa_fwd(p, w["conv_a"])
    qn, kn, vv, sm = mixc_pre_fwd(p, w["conv_c"], w["alog_l"], w["dtb_l"])
    oc, st_c, ti = gdn_fwd(qn, kn, vv, sm)
    ob, st_b = hgrn_fwd(p, w["lbs"])
    xo = merge_fwd(x, p, ya, ob, oc, w["hgrn_norm_w"], w["gdn_norm_w"], w["b_gate"],
                   w["w_out_a"], w["w_out_b"], w["w_out_c"], w["w_o"], l)
    saved = dict(x=x, p=p, h=h, ya=ya, qn=qn, kn=kn, vv=vv, sm=sm, oc=oc, st_c=st_c, ti=ti, ob=ob, st_b=st_b)
    return xo, saved


def layer_bwd(dxo, w, s):
    p, l = s["p"], w["l"]
    (dya, dob, doc, dbz, dcz, dg, merged, dy3, yb, yc, dbg, dhw, dgw) = merge_bwd(
        dxo, p, s["ya"], s["ob"], s["oc"], w["hgrn_norm_w"], w["gdn_norm_w"], w["b_gate"],
        w["w_out_a"], w["w_out_b"], w["w_out_c"], w["w_o"], l)
    g = {}
    g["w_o"] = matmul_tn(merged, dxo, "dw_o", with_bf16=True)
    g["w_out_a"] = matmul_tn(s["ya"], dy3, "dw_out_a", n=D, b_col0=0, with_bf16=True)
    g["w_out_b"] = matmul_tn(yb, dy3, "dw_out_b", n=D, b_col0=D, with_bf16=True)
    g["w_out_c"] = matmul_tn(yc, dy3, "dw_out_c", n=D, b_col0=2 * D, with_bf16=True)
    dbq, dbf, dbi, dlbs = hgrn_bwd(p, w["lbs"], s["st_b"], dob)
    dq8, dk8, dvv, dsm8 = gdn_bwd(s["qn"], s["kn"], s["vv"], s["sm"], s["st_c"], s["ti"], doc)
    dpc, dps, dcc, dsmall = mixc_pre_bwd(p, w["conv_c"], w["alog_l"], w["dtb_l"], dq8, dk8, dvv, dsm8)
    dpa, dca = mixa_bwd(p, w["conv_a"], dya)
    dp = jnp.concatenate([dpa, dbq, dbf, dbi, dbz, dpc, dg, dcz, dps], axis=1)
    g["w_in"] = matmul_tn(s["h"], dp, "dw_in")
    dx, dnw = inproj_bwd(dp, w["w_in"], l, s["x"], w["norm_w"], dxo)
    g.update(norm_w=dnw, b_gate=dbg, hgrn_norm_w=dhw, gdn_norm_w=dgw, lbs=dlbs, conv_a=dca[0:3], conv_c=dcc[0:4],
             a_log=dsmall[0:1, 8:16], dt_bias=dsmall[1:2, 8:16])
    return dx, g


def lbs_fwd(lb):
    def body(lb_ref, o_ref):
        l0, l1 = lb_ref[0:1, :], lb_ref[1:2, :]
        mx = jnp.maximum(l0, l1)
        e0, e1 = jnp.exp(l0 - mx), jnp.exp(l1 - mx)
        o_ref[0:1, :] = jnp.zeros_like(l0)
        o_ref[1:2, :] = e1 / (e0 + e1)
    return pl.pallas_call(body, name="lbs_fwd", out_shape=jax.ShapeDtypeStruct(lb.shape, f32))(lb)


def _adam_math(w, g, m, v):
    mn = ADAM_B1 * m + (1.0 - ADAM_B1) * g
    vn = ADAM_B2 * v + (1.0 - ADAM_B2) * (g * g)
    mh = mn / (1.0 - ADAM_B1 ** ADAM_STEP)
    vh = vn / (1.0 - ADAM_B2 ** ADAM_STEP)
    return -ADAM_LR * (mh / (jnp.sqrt(vh) + ADAM_EPS) + ADAM_WD * w), mn, vn


def adam(w, g, m, v, name):
    R, C = w.shape
    tr = 256 if R % 256 == 0 else R

    def body(w_ref, g_ref, m_ref, v_ref, d_ref, mo_ref, vo_ref):
        d, mn, vn = _adam_math(w_ref[...], g_ref[...], m_ref[...], v_ref[...])
        d_ref[...] = d
        mo_ref[...] = mn
        vo_ref[...] = vn

    spec = pl.BlockSpec((tr, C), lambda i: (i, 0))
    return pl.pallas_call(
        body, name=name, grid=(R // tr,), in_specs=[spec] * 4, out_specs=[spec] * 3,
        out_shape=[jax.ShapeDtypeStruct((R, C), f32)] * 3, compiler_params=_cp(("parallel",)),
    )(w, g, m, v)


def adam_pair(h, hs, w, m, v, name):
    _, R, C = w.shape
    cp = h.shape[2]
    tr = 256 if R % 256 == 0 else R

    def body(h_ref, hs_ref, w_ref, m_ref, v_ref, g_ref, d_ref, mo_ref, vo_ref):
        g = (h_ref[...] + hs_ref[...])[:, :C]
        d, mn, vn = _adam_math(w_ref[...], g, m_ref[...], v_ref[...])
        g_ref[...] = g
        d_ref[...] = d
        mo_ref[...] = mn
        vo_ref[...] = vn

    hspec = pl.BlockSpec((None, tr, cp), lambda l, i: (l, i, 0))
    spec = pl.BlockSpec((None, tr, C), lambda l, i: (l, i, 0))
    return pl.pallas_call(
        body, name=name, grid=(2, R // tr), in_specs=[hspec, hspec, spec, spec, spec], out_specs=[spec] * 4,
        out_shape=[jax.ShapeDtypeStruct(w.shape, f32)] * 4, compiler_params=_cp(("parallel", "parallel")),
    )(h, hs, w, m, v)


_SMALL = (("norm_w", 2 * D), ("b_gate", 6 * D), ("lower_bounds", None), ("hgrn_norm_w", 2 * HK),
          ("gdn_norm_w", 2 * HK), ("a_log", 16), ("dt_bias", 16), ("final_norm_w", D), ("loss", None))
_CONV = (("conv_a", 2 * 3 * 512), ("conv_c", 2 * 4 * 2048))


def _small_rows(n):
    return 16 if n is None else -(-n // 1024) * 8


LB_ROW = sum(_small_rows(n) for _, n in _SMALL[:2])
ADAM_ROWS = sum(_small_rows(n) for _, n in _SMALL)
SMALL_ROWS = ADAM_ROWS + sum(_small_rows(n) for _, n in _CONV)


def small_update(parts, wp, mp, vp):
    def body(p_ref, w_ref, m_ref, v_ref, g_ref, d_ref, mo_ref, vo_ref):
        gs = p_ref[0]
        for i in range(1, 8):
            gs = gs + p_ref[i]
        w = w_ref[...]
        l0, l1 = w[LB_ROW:LB_ROW + 8], w[LB_ROW + 8:LB_ROW + 16]
        mx = jnp.maximum(l0, l1)
        e0, e1 = jnp.exp(l0 - mx), jnp.exp(l1 - mx)
        p0, p1 = e0 / (e0 + e1), e1 / (e0 + e1)
        dl1 = gs[LB_ROW + 8:LB_ROW + 16]
        s = p1 * dl1
        g = jnp.concatenate([gs[0:LB_ROW], -p0 * s, p1 * dl1 - p1 * s, gs[LB_ROW + 16:ADAM_ROWS]], axis=0)
        d, mn, vn = _adam_math(w, g, m_ref[...], v_ref[...])
        g_ref[0:ADAM_ROWS, :] = g
        g_ref[ADAM_ROWS:, :] = gs[ADAM_ROWS:]
        d_ref[...] = d
        mo_ref[...] = mn
        vo_ref[...] = vn
    sd = jax.ShapeDtypeStruct
    return pl.pallas_call(body, name="small_update",
                          out_shape=[sd((SMALL_ROWS, 128), f32)] + [sd((ADAM_ROWS, 128), f32)] * 3)(parts, wp, mp, vp)


def _chip_spec(block, index_of):
    return pl.BlockSpec(block, lambda l, i, chip: index_of(l, i, chip[0]))


def partial_sum_shards(own, recv, chip, name):
    _, _, R, C = own.shape
    tr = 256 if R % 256 == 0 else R

    def body(chip_ref, o_ref, r_ref, out_ref):
        out_ref[...] = ((o_ref[...] + r_ref[0].astype(f32)) + r_ref[1].astype(f32)) + r_ref[2].astype(f32)

    return pl.pallas_call(
        body, name=name,
        grid_spec=pltpu.PrefetchScalarGridSpec(
            num_scalar_prefetch=1, grid=(2, R // tr),
            in_specs=[_chip_spec((None, None, tr, C), lambda l, i, c: (c, l, i, 0)),
                      pl.BlockSpec((3, None, tr, C), lambda l, i, chip: (0, l, i, 0))],
            out_specs=pl.BlockSpec((None, tr, C), lambda l, i, chip: (l, i, 0))),
        out_shape=jax.ShapeDtypeStruct((2, R, C), f32), compiler_params=_cp(("arbitrary", "arbitrary")),
    )(chip, own, recv)


def partial_sum_window(own0, own1, recv, chip, axis, name):
    _, _, r, c = recv.shape

    def body(chip_ref, o0_ref, o1_ref, r_ref, out_ref):
        l = pl.program_id(0)
        rest = (r_ref[0].astype(f32), r_ref[1].astype(f32), r_ref[2].astype(f32))

        @pl.when(l == 0)
        def _():
            out_ref[...] = ((o0_ref[...] + rest[0]) + rest[1]) + rest[2]

        @pl.when(l == 1)
        def _():
            out_ref[...] = ((o1_ref[...] + rest[0]) + rest[1]) + rest[2]

    win = _chip_spec((r, c), (lambda l, i, ch: (0, ch)) if axis == 1 else (lambda l, i, ch: (ch, 0)))
    return pl.pallas_call(
        body, name=name,
        grid_spec=pltpu.PrefetchScalarGridSpec(
            num_scalar_prefetch=1, grid=(2, 1),
            in_specs=[win, win, pl.BlockSpec((3, None, r, c), lambda l, i, chip: (0, l, 0, 0))],
            out_specs=pl.BlockSpec((None, r, c), lambda l, i, chip: (l, 0, 0))),
        out_shape=jax.ShapeDtypeStruct((2, r, c), f32), compiler_params=_cp(("arbitrary", "arbitrary")),
    )(chip, own0, own1, recv)


MESH = pl.DeviceIdType.MESH
_HBM = pl.BlockSpec(memory_space=pltpu.HBM)


def _place():
    return lax.axis_index("x"), lax.axis_index("y"), lax.axis_index("c")


def weight_gather(arrs):
    n = len(arrs)

    def body(*refs):
        x_refs, out_refs = refs[:n], refs[n:2 * n]
        send_sems, recv_sems, local_sems = refs[2 * n:]
        x, y, c = _place()
        me, sibling = (x, y, c), (x, y, 1 - c)
        chips = [(1 - x, y), (x, 1 - y), (1 - x, 1 - y)]

        def copy(a, k, block, to, own_src=False):
            px, py, pc = block
            dst = out_refs[a].at[2 * px + py, pc]
            return pltpu.make_async_remote_copy(
                src_ref=x_refs[a].at[c] if own_src else dst, dst_ref=dst,
                send_sem=send_sems.at[7 * a + k], recv_sem=recv_sems.at[7 * a + k], device_id=to, device_id_type=MESH)

        mine = [pltpu.make_async_copy(x_refs[a].at[c], out_refs[a].at[2 * x + y, c], local_sems.at[a])
                for a in range(n)]
        for cp in mine:
            cp.start()
        first = []
        for a in range(n):
            first.append(copy(a, 0, me, sibling, own_src=True))
            first += [copy(a, 1 + j, me, (*chip, c), own_src=True) for j, chip in enumerate(chips)]
        for cp in first:
            cp.start()
        passed = []
        for j, chip in enumerate(chips):
            for a in range(n):
                copy(a, 1 + j, (*chip, c), me).wait_recv()
                fwd = copy(a, 4 + j, (*chip, c), sibling)
                fwd.start()
                passed.append(fwd)
        for a in range(n):
            copy(a, 0, sibling, me).wait_recv()
            for j, chip in enumerate(chips):
                copy(a, 4 + j, (*chip, 1 - c), me).wait_recv()
        for cp in first + passed:
            cp.wait_send()
        for cp in mine:
            cp.wait()

    return pl.pallas_call(
        body, name="weight_gather", in_specs=[_HBM] * n, out_specs=[_HBM] * n,
        out_shape=[jax.ShapeDtypeStruct((N_CHIPS,) + a.shape, a.dtype) for a in arrs],
        scratch_shapes=[pltpu.SemaphoreType.DMA((7 * n,)), pltpu.SemaphoreType.DMA((7 * n,)),
                        pltpu.SemaphoreType.DMA((n,))],
    )(*arrs)


SHARD_W = 256


def grad_exchange(gb_win, col_mats, row_mats, small):
    mats = [(m, 1) for m in col_mats] + [(m, 0) for m in row_mats]
    nm = len(mats)
    S = small.shape[0]
    n_copies = 3 + 6 * nm + 7

    def win_shape(m, axis):
        r, c = m[0].shape
        return (r, SHARD_W) if axis == 1 else (SHARD_W, c)

    def body(*refs):
        gb_ref = refs[0]
        mat_refs = refs[1:1 + 2 * nm]
        sm_ref = refs[1 + 2 * nm]
        outs = refs[2 + 2 * nm:4 + 3 * nm]
        recv_win, recv_mats, smalls_ref = outs[0], outs[1:1 + nm], outs[1 + nm]
        send_sems, recv_sems, local_sem = refs[4 + 3 * nm:]
        x, y, c = _place()
        chips = [(1 - x, y), (x, 1 - y), (1 - x, 1 - y)]
        my_slot = smalls_ref.at[4 * x + 2 * y + c]
        mine = pltpu.make_async_copy(sm_ref, my_slot, local_sem)
        mine.start()
        copies = []

        def add(src, dst, to):
            k = len(copies)
            copies.append(pltpu.make_async_remote_copy(src_ref=src, dst_ref=dst, send_sem=send_sems.at[k],
                                                       recv_sem=recv_sems.at[k], device_id=to, device_id_type=MESH))

        for j, (px, py) in enumerate(chips):
            q = 2 * px + py
            add(gb_ref.at[q], recv_win.at[j], (px, py, c))
            lo = pl.multiple_of(q * SHARD_W, SHARD_W)
            for mi, (_, axis) in enumerate(mats):
                for l in range(2):
                    full = mat_refs[2 * mi + l]
                    src = full.at[:, pl.ds(lo, SHARD_W)] if axis == 1 else full.at[pl.ds(lo, SHARD_W), :]
                    add(src, recv_mats[mi].at[j, l], (px, py, c))
        for mask in range(1, 8):
            fx, fy, fc = (mask >> 2) & 1, (mask >> 1) & 1, mask & 1
            peer = ((1 - x) if fx else x, (1 - y) if fy else y, (1 - c) if fc else c)
            add(sm_ref, my_slot, peer)
        for cp in copies:
            cp.start()
        for cp in copies:
            cp.wait_recv()
        for cp in copies:
            cp.wait_send()
        mine.wait()

    sd = jax.ShapeDtypeStruct
    out_shape = ([sd((3,) + gb_win.shape[1:], gb_win.dtype)]
                 + [sd((3, 2) + win_shape(m, axis), m[0].dtype) for m, axis in mats] + [sd((8, S, 128), f32)])
    flat = [a for m, _ in mats for a in m]
    return pl.pallas_call(
        body, name="grad_exchange", in_specs=[_HBM] * (2 + 2 * nm), out_specs=[_HBM] * (2 + nm), out_shape=out_shape,
        scratch_shapes=[pltpu.SemaphoreType.DMA((n_copies,)), pltpu.SemaphoreType.DMA((n_copies,)),
                        pltpu.SemaphoreType.DMA],
    )(gb_win, *flat, small)


def sibling_swap(hs):
    n = len(hs)

    def body(*refs):
        h_refs, out_refs, send_sems, recv_sems = refs[:n], refs[n:2 * n], refs[2 * n], refs[2 * n + 1]
        x, y, c = _place()
        copies = [pltpu.make_async_remote_copy(src_ref=h_refs[a], dst_ref=out_refs[a], send_sem=send_sems.at[a],
                                               recv_sem=recv_sems.at[a], device_id=(x, y, 1 - c), device_id_type=MESH)
                  for a in range(n)]
        for cp in copies:
            cp.start()
        for cp in copies:
            cp.wait()

    return pl.pallas_call(
        body, name="sibling_swap", in_specs=[_HBM] * n, out_specs=[_HBM] * n,
        out_shape=[jax.ShapeDtypeStruct(h.shape, h.dtype) for h in hs],
        scratch_shapes=[pltpu.SemaphoreType.DMA((n,)), pltpu.SemaphoreType.DMA((n,))],
    )(*hs)


N_CHIPS = 4
SHARD_COLS = N_ORIG // N_CHIPS


SHARD_PAD = 2688
_COL_SEGMENTS = ((0, 6144, 0), (6144, 6160, OFF_S), (6160, 7184, OFF_CZ), (7184, N_ORIG, OFF_G))


def _shard_pieces():
    pieces = []
    for lo, hi, dst in _COL_SEGMENTS:
        for p in range(N_CHIPS):
            a, b = max(lo, p * SHARD_COLS), min(hi, (p + 1) * SHARD_COLS)
            if a < b:
                pieces.append((p, a - p * SHARD_COLS, dst + a - lo, b - a))
    return pieces


def win_cast_pad(w):
    tr = 256

    def body(x_ref, o_ref):
        o_ref[:, :SHARD_COLS] = _b(x_ref[...])
        o_ref[:, SHARD_COLS:] = jnp.zeros((tr, SHARD_PAD - SHARD_COLS), bf16)

    return pl.pallas_call(
        body, name="win_cast_pad", grid=(2, D // tr),
        in_specs=[pl.BlockSpec((None, tr, SHARD_COLS), lambda l, i: (l, i, 0))],
        out_specs=pl.BlockSpec((None, tr, SHARD_PAD), lambda l, i: (l, i, 0)),
        out_shape=jax.ShapeDtypeStruct((2, D, SHARD_PAD), bf16), compiler_params=_cp(("parallel", "parallel")),
    )(w)


def win_to_padded(w4):
    tr = 256
    pieces = _shard_pieces()

    def body(a_ref, o_ref):
        o_ref[:, N_ORIG:] = jnp.zeros((tr, NP - N_ORIG), bf16)
        for p, j0, c0, n in pieces:
            o_ref[:, c0:c0 + n] = a_ref[p, :, j0:j0 + n]

    return pl.pallas_call(
        body, name="win_to_padded", grid=(2, D // tr),
        in_specs=[pl.BlockSpec((N_CHIPS, None, tr, SHARD_PAD), lambda l, i: (0, l, i, 0))],
        out_specs=pl.BlockSpec((None, tr, NP), lambda l, i: (l, i, 0)),
        out_shape=jax.ShapeDtypeStruct((2, D, NP), bf16), compiler_params=_cp(("parallel", "parallel")),
    )(w4)


def win_from_padded(dw0, dw1):
    tr = 64
    nt = D // tr
    pieces = _shard_pieces()

    def body(d0_ref, d1_ref, of_ref, ob_ref):
        def put(src):
            for p in range(N_CHIPS):
                of_ref[p, :, SHARD_COLS:] = jnp.zeros((tr, SHARD_PAD - SHARD_COLS), f32)
                ob_ref[p, :, SHARD_COLS:] = jnp.zeros((tr, SHARD_PAD - SHARD_COLS), bf16)
            for p, j0, c0, n in pieces:
                v = src[:, c0:c0 + n]
                of_ref[p, :, j0:j0 + n] = v
                ob_ref[p, :, j0:j0 + n] = _b(v)

        @pl.when(pl.program_id(0) == 0)
        def _():
            put(d0_ref)

        @pl.when(pl.program_id(0) == 1)
        def _():
            put(d1_ref)

    out_spec = pl.BlockSpec((N_CHIPS, None, tr, SHARD_PAD), lambda l, i: (0, l, i, 0))
    return pl.pallas_call(
        body, name="win_from_padded", grid=(2, nt),
        in_specs=[pl.BlockSpec((tr, NP), lambda l, i: (jnp.where(l == 0, i, nt - 1), 0)),
                  pl.BlockSpec((tr, NP), lambda l, i: (jnp.where(l == 1, i, 0), 0))],
        out_specs=[out_spec, out_spec],
        out_shape=[jax.ShapeDtypeStruct((N_CHIPS, 2, D, SHARD_PAD), f32),
                   jax.ShapeDtypeStruct((N_CHIPS, 2, D, SHARD_PAD), bf16)],
        compiler_params=_cp(("arbitrary", "arbitrary")),
    )(dw0, dw1)


def _rows128(a):
    flat = a.reshape(-1)
    total = -(-flat.shape[0] // 1024) * 1024
    return jnp.pad(flat, (0, total - flat.shape[0])).reshape(total // 128, 128)


def _lb_rows(lb):
    return jnp.pad(lb.reshape(2, 4, 128), ((0, 0), (0, 4), (0, 0))).reshape(16, 128)


def _pack_small(v, with_conv):
    rows = []
    for name, n in _SMALL + (_CONV if with_conv else ()):
        if name == "lower_bounds":
            rows.append(_lb_rows(v[name]))
        elif name == "loss":
            rows.append(jnp.broadcast_to(v[name], (16, 128)) if name in v else jnp.zeros((16, 128), f32))
        else:
            rows.append(_rows128(v[name]))
    return jnp.concatenate(rows, axis=0)


def _unpack_small(p, shapes, with_conv):
    out, row = {}, 0
    for name, n in _SMALL + (_CONV if with_conv else ()):
        nrows = _small_rows(n)
        blk = p[row:row + nrows]
        if name == "lower_bounds":
            out[name] = blk.reshape(2, 8, 128)[:, :4].reshape(2, 512)
        elif name == "loss":
            out[name] = blk[0, 0]
        else:
            out[name] = blk.reshape(-1)[:n].reshape(shapes[name])
        row += nrows
    return out


def _lane_vec(a8):
    return jnp.pad(a8.reshape(1, 8), ((0, 0), (8, 112)))


WEIGHT_NAMES = ("norm_w", "w_in", "b_gate", "conv_a", "conv_c", "a_log", "dt_bias", "lower_bounds", "hgrn_norm_w",
                "gdn_norm_w", "w_out_a", "w_out_b", "w_out_c", "w_o", "final_norm_w")


def kernel(x, norm_w, w_in, b_gate, conv_a, conv_c, a_log, dt_bias, lower_bounds, hgrn_norm_w, gdn_norm_w, w_out_a, w_out_b, w_out_c, w_o, final_norm_w, loss_target, m_norm_w, m_w_in, m_b_gate, m_conv_a, m_conv_c, m_a_log, m_dt_bias, m_lower_bounds, m_hgrn_norm_w, m_gdn_norm_w, m_w_out_a, m_w_out_b, m_w_out_c, m_w_o, m_final_norm_w, v_norm_w, v_w_in, v_b_gate, v_conv_a, v_conv_c, v_a_log, v_dt_bias, v_lower_bounds, v_hgrn_norm_w, v_gdn_norm_w, v_w_out_a, v_w_out_b, v_w_out_c, v_w_o, v_final_norm_w):
    wts = dict(norm_w=norm_w, w_in=w_in, b_gate=b_gate, conv_a=conv_a, conv_c=conv_c, a_log=a_log, dt_bias=dt_bias,
               lower_bounds=lower_bounds, hgrn_norm_w=hgrn_norm_w, gdn_norm_w=gdn_norm_w, w_out_a=w_out_a,
               w_out_b=w_out_b, w_out_c=w_out_c, w_o=w_o, final_norm_w=final_norm_w)
    mom = dict(norm_w=m_norm_w, w_in=m_w_in, b_gate=m_b_gate, conv_a=m_conv_a, conv_c=m_conv_c, a_log=m_a_log,
               dt_bias=m_dt_bias, lower_bounds=m_lower_bounds, hgrn_norm_w=m_hgrn_norm_w, gdn_norm_w=m_gdn_norm_w,
               w_out_a=m_w_out_a, w_out_b=m_w_out_b, w_out_c=m_w_out_c, w_o=m_w_o, final_norm_w=m_final_norm_w)
    var = dict(norm_w=v_norm_w, w_in=v_w_in, b_gate=v_b_gate, conv_a=v_conv_a, conv_c=v_conv_c, a_log=v_a_log,
               dt_bias=v_dt_bias, lower_bounds=v_lower_bounds, hgrn_norm_w=v_hgrn_norm_w, gdn_norm_w=v_gdn_norm_w,
               w_out_a=v_w_out_a, w_out_b=v_w_out_b, w_out_c=v_w_out_c, w_o=v_w_o, final_norm_w=v_final_norm_w)
    chip = 2 * lax.axis_index("x") + lax.axis_index("y")
    chip1 = chip.reshape(1).astype(jnp.int32)

    win4, woa4, wob4, woc4, wo4, ca4, cc4 = weight_gather(
        [win_cast_pad(w_in), _b(w_out_a), _b(w_out_b), _b(w_out_c), _b(w_o), conv_a, conv_c])
    by_cols = lambda a: a.transpose(1, 2, 0, 3).reshape(a.shape[1], a.shape[2], N_CHIPS * a.shape[3])
    by_rows = lambda a: a.transpose(1, 0, 2, 3).reshape(a.shape[1], N_CHIPS * a.shape[2], a.shape[3])
    full = dict(w_in=win_to_padded(win4), w_out_a=by_cols(woa4), w_out_b=by_cols(wob4), w_out_c=by_rows(woc4),
                w_o=by_rows(wo4), conv_a=by_cols(ca4), conv_c=by_cols(cc4))
    lbs = lbs_fwd(lower_bounds)
    layers = []
    for l in range(2):
        layers.append(dict(
            l=l, norm_w=norm_w[l:l + 1], w_in=full["w_in"], b_gate=b_gate[l:l + 1], conv_a=full["conv_a"][l],
            conv_c=full["conv_c"][l], alog_l=_lane_vec(a_log[l]), dtb_l=_lane_vec(dt_bias[l]), lbs=lbs[l:l + 1],
            hgrn_norm_w=hgrn_norm_w[l:l + 1], gdn_norm_w=gdn_norm_w[l:l + 1], w_out_a=full["w_out_a"],
            w_out_b=full["w_out_b"], w_out_c=full["w_out_c"], w_o=full["w_o"]))

    xs, saved = x[0], []
    for l in range(2):
        xs, s = layer_fwd(xs, layers[l])
        saved.append(s)
    loss_row, dx, dfw = loss_head(xs, final_norm_w.reshape(1, D), loss_target[0])
    lg = [None, None]
    for l in (1, 0):
        dx, lg[l] = layer_bwd(dx, layers[l], saved[l])
    grad_x = dx[None]

    stack = lambda n: jnp.stack([lg[0][n], lg[1][n]], axis=0)
    gsmall = {n: stack(n) for n in ("norm_w", "b_gate", "hgrn_norm_w", "gdn_norm_w", "a_log", "dt_bias", "conv_a",
                                    "conv_c")}
    gsmall.update(lower_bounds=stack("lbs"), final_norm_w=dfw, loss=loss_row)
    gf_win, gb_win = win_from_padded(lg[0]["w_in"], lg[1]["w_in"])
    col_names, row_names = ("w_out_a", "w_out_b"), ("w_out_c", "w_o")
    pair = lambda n, k: (lg[0][n][k], lg[1][n][k])
    recv = grad_exchange(gb_win, [pair(n, 1) for n in col_names], [pair(n, 1) for n in row_names],
                         _pack_small(gsmall, True))
    smalls = recv[-1]
    halves = [partial_sum_shards(gf_win, recv[0], chip1, "psum_w_in")]
    for i, n in enumerate(col_names + row_names):
        halves.append(partial_sum_window(*pair(n, 0), recv[1 + i], chip1, 1 if n in col_names else 0, "psum_" + n))
    others = sibling_swap(halves)

    out_g, out_d, out_m, out_v = {}, {}, {}, {}
    for n, h, hs in zip(("w_in",) + col_names + row_names, halves, others):
        out_g[n], out_d[n], out_m[n], out_v[n] = adam_pair(h, hs, wts[n], mom[n], var[n], "adam_" + n)
    small_names = [n for n, _ in _SMALL if n != "loss"]
    pack = lambda v: _pack_small({n: v[n] for n in small_names}, False)
    sg, sd, smn, svn = small_update(smalls, pack(wts), pack(mom), pack(var))
    shapes = {n: wts[n].shape for n in small_names}
    shapes.update(conv_a=(2, 3, 512), conv_c=(2, 4, 2048))
    for dst, src, conv in ((out_g, sg, True), (out_d, sd, False), (out_m, smn, False), (out_v, svn, False)):
        dst.update(_unpack_small(src, shapes, conv))
    loss = out_g.pop("loss")
    for n in ("conv_a", "conv_c"):
        width = wts[n].shape[2]
        g = lax.dynamic_slice_in_dim(out_g[n], chip * width, width, axis=2)
        two_d = lambda a: a.reshape(-1, width)
        d, mn, vn = adam(two_d(wts[n]), two_d(g), two_d(mom[n]), two_d(var[n]), "adam_" + n)
        out_g[n] = g
        out_d[n], out_m[n], out_v[n] = (a.reshape(wts[n].shape) for a in (d, mn, vn))
    return (loss, grad_x, *[out_g[n] for n in WEIGHT_NAMES], *[out_d[n] for n in WEIGHT_NAMES],
            *[out_m[n] for n in WEIGHT_NAMES], *[out_v[n] for n in WEIGHT_NAMES])
```

```python
import functools

import jax
import jax.numpy as jnp
from jax import lax
from jax.experimental import pallas as pl
from jax.experimental.pallas import tpu as pltpu

f32 = jnp.float32
bf16 = jnp.bfloat16

D = 1024
L = 64
SUB = 16
NORM_EPS = 1e-6
L2_EPS = 1e-6
MIN_F = 1e-30
HK = 128
QK_SCALE = HK ** -0.5
N_GDN = 8
GDN_BLOCK = 1024
N_HGRN = 4

OFF_A, OFF_B, OFF_CQKV, OFF_G, OFF_CZ, OFF_S = 0, 2048, 4096, 6144, 9216, 10240
NP = 10368
N_ORIG = 10256

ADAM_LR, ADAM_B1, ADAM_B2, ADAM_EPS, ADAM_WD, ADAM_STEP = 0.001, 0.9, 0.999, 1e-08, 0.01, 10

VMEM_LIMIT = 56 * 1024 * 1024


def _cp(sem):
    return pltpu.CompilerParams(dimension_semantics=sem, vmem_limit_bytes=VMEM_LIMIT)


def _sigmoid(x):
    return jax.nn.sigmoid(x)


def _silu(x):
    return x * _sigmoid(x)


def _dsilu(x):
    s = _sigmoid(x)
    return s * (1.0 + x * (1.0 - s))


def _softplus(x):
    u = jnp.exp(-jnp.abs(x))
    w = 1.0 + u
    l1p = jnp.where(w == 1.0, u, jnp.log(w) * (u / (w - 1.0)))
    return jnp.maximum(x, 0.0) + l1p


def _dot(a, b):
    return jnp.dot(a, b, preferred_element_type=f32)


def _dot_nt(a, b):
    return lax.dot_general(a, b, (((1,), (1,)), ((), ())), preferred_element_type=f32)


def _dot_tn(a, b):
    return lax.dot_general(a, b, (((0,), (0,)), ((), ())), preferred_element_type=f32)


def _bdot(a, b):
    return lax.dot_general(a, b, (((2,), (1,)), ((0,), (0,))), preferred_element_type=f32)


def _bdot_nt(a, b):
    return lax.dot_general(a, b, (((2,), (2,)), ((0,), (0,))), preferred_element_type=f32)


def _bdot_tn(a, b):
    return lax.dot_general(a, b, (((1,), (1,)), ((0,), (0,))), preferred_element_type=f32)


def _bdot_split(a, b):
    ah, bh = _b(a), _b(b)
    al, bl = _b(a - ah.astype(f32)), _b(b - bh.astype(f32))
    return _bdot(ah, bh) + (_bdot(ah, bl) + _bdot(al, bh))


def _b(x):
    return x.astype(bf16)


def _chunk_cumsum(x, rows_in_chunk):
    n = x.shape[0]
    for s in (1, 2, 4, 8, 16, 32):
        x = x + jnp.where(rows_in_chunk >= s, pltpu.roll(x, s, axis=0), 0.0)
    return x


def _chunk_rev_cumsum(x, rows_in_chunk):
    n = x.shape[0]
    for s in (1, 2, 4, 8, 16, 32):
        x = x + jnp.where(rows_in_chunk + s < L, pltpu.roll(x, n - s, axis=0), 0.0)
    return x


def _shift_down(x, s):
    return pltpu.roll(x, s, axis=0) if s else x


def _shift_up(x, s):
    return pltpu.roll(x, x.shape[0] - s, axis=0) if s else x


def inproj_fwd(x, nw, w, l):
    T = x.shape[0]
    tT, tn = min(512, T), 1152

    def body(x_ref, nw_ref, w_ref, p_ref, h_ref, hs):
        @pl.when(pl.program_id(1) == 0)
        def _():
            xv = x_ref[...]
            r = lax.rsqrt(jnp.mean(xv * xv, axis=-1, keepdims=True) + NORM_EPS)
            hv = _b(xv * r * nw_ref[...])
            hs[...] = hv
            h_ref[...] = hv
        p_ref[...] = _dot(hs[...], w_ref[...])

    return pl.pallas_call(
        body, name="inproj_fwd", grid=(T // tT, NP // tn),
        in_specs=[pl.BlockSpec((tT, D), lambda i, j: (i, 0)), pl.BlockSpec((1, D), lambda i, j: (0, 0)),
                  pl.BlockSpec((None, D, tn), lambda i, j: (l, 0, j))],
        out_specs=[pl.BlockSpec((tT, tn), lambda i, j: (i, j)), pl.BlockSpec((tT, D), lambda i, j: (i, 0))],
        out_shape=[jax.ShapeDtypeStruct((T, NP), f32), jax.ShapeDtypeStruct((T, D), bf16)],
        scratch_shapes=[pltpu.VMEM((tT, D), bf16)],
        compiler_params=_cp(("parallel", "arbitrary")),
    )(x, nw, w)


def matmul_tn(a, b, name, n=None, b_col0=0, with_bf16=False):
    T, K = a.shape
    N = b.shape[1] if n is None else n
    tT = min(2048, T)
    tn = 1152 if N % 1152 == 0 else min(N, 1024)
    nt = T // tT
    cb0 = b_col0 // tn

    def body(a_ref, b_ref, o_ref, *ob_ref):
        @pl.when(pl.program_id(1) == 0)
        def _():
            o_ref[...] = jnp.zeros_like(o_ref)
        o_ref[...] += _dot_tn(_b(a_ref[...]), _b(b_ref[...]))
        if with_bf16:
            @pl.when(pl.program_id(1) == nt - 1)
            def _():
                ob_ref[0][...] = _b(o_ref[...])

    ospec = pl.BlockSpec((K, tn), lambda j, t: (0, j))
    return pl.pallas_call(
        body, name=name, grid=(N // tn, nt),
        in_specs=[pl.BlockSpec((tT, K), lambda j, t: (t, 0)), pl.BlockSpec((tT, tn), lambda j, t: (t, cb0 + j))],
        out_specs=[ospec, ospec] if with_bf16 else ospec,
        out_shape=([jax.ShapeDtypeStruct((K, N), f32), jax.ShapeDtypeStruct((K, N), bf16)] if with_bf16
                   else jax.ShapeDtypeStruct((K, N), f32)),
        compiler_params=_cp(("parallel", "arbitrary")),
    )(a, b)


def inproj_bwd(dp, w, l, x, nw, dres):
    T = x.shape[0]
    tT, tk = min(512, T), 1152
    nk = NP // tk

    def body(dp_ref, w_ref, x_ref, nw_ref, dres_ref, dx_ref, dnw_ref, acc):
        i, k = pl.program_id(0), pl.program_id(1)

        @pl.when((i == 0) & (k == 0))
        def _():
            dnw_ref[...] = jnp.zeros_like(dnw_ref)

        @pl.when(k == 0)
        def _():
            acc[...] = jnp.zeros_like(acc)
        acc[...] += _dot_nt(dp_ref[...], w_ref[...])

        @pl.when(k == nk - 1)
        def _():
            xv = x_ref[...]
            r = lax.rsqrt(jnp.mean(xv * xv, axis=-1, keepdims=True) + NORM_EPS)
            xh = xv * r
            dy = acc[...]
            dyw = dy * nw_ref[...]
            dx_ref[...] = r * (dyw - xh * jnp.mean(dyw * xh, axis=-1, keepdims=True)) + dres_ref[...]
            dnw_ref[...] += jnp.sum(dy * xh, axis=0, keepdims=True)

    return pl.pallas_call(
        body, name="inproj_bwd", grid=(T // tT, nk),
        in_specs=[pl.BlockSpec((tT, tk), lambda i, k: (i, k)), pl.BlockSpec((None, D, tk), lambda i, k: (l, 0, k)),
                  pl.BlockSpec((tT, D), lambda i, k: (i, 0)), pl.BlockSpec((1, D), lambda i, k: (0, 0)),
                  pl.BlockSpec((tT, D), lambda i, k: (i, 0))],
        out_specs=[pl.BlockSpec((tT, D), lambda i, k: (i, 0)), pl.BlockSpec((1, D), lambda i, k: (0, 0))],
        out_shape=[jax.ShapeDtypeStruct((T, D), f32), jax.ShapeDtypeStruct((1, D), f32)],
        scratch_shapes=[pltpu.VMEM((tT, D), f32)],
        compiler_params=_cp(("arbitrary", "arbitrary")),
    )(dp, w, x, nw, dres)


def loss_head(x, fw, tgt):
    T = x.shape[0]
    tT = min(512, T)

    def body(x_ref, fw_ref, t_ref, loss_ref, dx_ref, dfw_ref):
        @pl.when(pl.program_id(0) == 0)
        def _():
            loss_ref[...] = jnp.zeros_like(loss_ref)
            dfw_ref[...] = jnp.zeros_like(dfw_ref)
        xv = x_ref[...]
        r = lax.rsqrt(jnp.mean(xv * xv, axis=-1, keepdims=True) + NORM_EPS)
        xh = xv * r
        err = xh * fw_ref[...] - t_ref[...]
        part = 0.5 * jnp.sum(jnp.mean(err * err, axis=-1, keepdims=True), axis=0, keepdims=True)
        loss_ref[...] += jnp.broadcast_to(part, loss_ref.shape)
        dy = err * (1.0 / D)
        dyw = dy * fw_ref[...]
        dx_ref[...] = r * (dyw - xh * jnp.mean(dyw * xh, axis=-1, keepdims=True))
        dfw_ref[...] += jnp.sum(dy * xh, axis=0, keepdims=True)

    return pl.pallas_call(
        body, name="loss_head", grid=(T // tT,),
        in_specs=[pl.BlockSpec((tT, D), lambda i: (i, 0)), pl.BlockSpec((1, D), lambda i: (0, 0)),
                  pl.BlockSpec((tT, D), lambda i: (i, 0))],
        out_specs=[pl.BlockSpec((1, 128), lambda i: (0, 0)), pl.BlockSpec((tT, D), lambda i: (i, 0)),
                   pl.BlockSpec((1, D), lambda i: (0, 0))],
        out_shape=[jax.ShapeDtypeStruct((1, 128), f32), jax.ShapeDtypeStruct((T, D), f32),
                   jax.ShapeDtypeStruct((1, D), f32)],
        compiler_params=_cp(("arbitrary",)),
    )(x, fw, tgt)


def _halo_specs(tT, T, width, colblk):
    nb8 = T // 8
    per = tT // 8
    prev = pl.BlockSpec((8, width), lambda i: (jnp.maximum(i * per - 1, 0), colblk))
    nxt = pl.BlockSpec((8, width), lambda i: (jnp.minimum((i + 1) * per, nb8 - 1), colblk))
    return prev, nxt


def mixa_fwd(p, cw):
    T = p.shape[0]
    tT = min(512, T)
    prev_spec, _ = _halo_specs(tT, T, 2048, OFF_A // 2048)

    def body(p_ref, pp_ref, cw_ref, y_ref):
        pv = p_ref[...]
        u = pv[:, 512:1024] * pv[:, 1024:1536]
        pp = pp_ref[...]
        up = jnp.where(pl.program_id(0) == 0, 0.0, pp[:, 512:1024] * pp[:, 1024:1536])
        ue = jnp.concatenate([up, u], axis=0)
        cv = cw_ref[0:1, :] * _shift_down(ue, 2) + cw_ref[1:2, :] * _shift_down(ue, 1) + cw_ref[2:3, :] * ue
        y_ref[...] = _b(pv[:, 0:512] * cv[8:] * _silu(pv[:, 1536:2048]))

    return pl.pallas_call(
        body, name="mixa_fwd", grid=(T // tT,),
        in_specs=[pl.BlockSpec((tT, 2048), lambda i: (i, OFF_A // 2048)), prev_spec,
                  pl.BlockSpec((3, 512), lambda i: (0, 0))],
        out_specs=pl.BlockSpec((tT, 512), lambda i: (i, 0)),
        out_shape=jax.ShapeDtypeStruct((T, 512), bf16),
        compiler_params=_cp(("parallel",)),
    )(p, p, cw)


def mixa_bwd(p, cw, dy):
    T = p.shape[0]
    tT = min(512, T)
    nt = T // tT
    prev_spec, next_spec = _halo_specs(tT, T, 2048, OFF_A // 2048)
    _, dnext_spec = _halo_specs(tT, T, 512, 0)

    def body(p_ref, pp_ref, pn_ref, cw_ref, dy_ref, dyn_ref, dp_ref, dcw_ref):
        i = pl.program_id(0)

        @pl.when(i == 0)
        def _():
            dcw_ref[...] = jnp.zeros_like(dcw_ref)
        pv, pp, pn = p_ref[...], pp_ref[...], pn_ref[...]
        pe = jnp.concatenate([pp, pv, pn], axis=0)
        rows = lax.broadcasted_iota(jnp.int32, (tT + 16, 1), 0)
        ab, ac, ax, az = pe[:, 0:512], pe[:, 512:1024], pe[:, 1024:1536], pe[:, 1536:2048]
        u = jnp.where((rows < 8) & (i == 0), 0.0, ac * ax)
        u1, u2 = _shift_down(u, 1), _shift_down(u, 2)
        w0, w1, w2 = cw_ref[0:1, :], cw_ref[1:2, :], cw_ref[2:3, :]
        cv = w0 * u2 + w1 * u1 + w2 * u
        dye = jnp.concatenate([jnp.zeros((8, 512), f32), dy_ref[...], dyn_ref[...]], axis=0)
        dye = jnp.where((rows >= tT + 8) & (i == nt - 1), 0.0, dye)
        sz = _silu(az)
        dcv = dye * ab * sz
        du = w2 * dcv + w1 * _shift_up(dcv, 1) + w0 * _shift_up(dcv, 2)
        inner = (rows >= 8) & (rows < tT + 8)
        dcv_in = jnp.where(inner, dcv, 0.0)
        dcw_ref[0:1, :] += jnp.sum(dcv_in * u2, axis=0, keepdims=True)
        dcw_ref[1:2, :] += jnp.sum(dcv_in * u1, axis=0, keepdims=True)
        dcw_ref[2:3, :] += jnp.sum(dcv_in * u, axis=0, keepdims=True)
        sl = slice(8, tT + 8)
        dp_ref[:, 0:512] = _b((dye * cv * sz)[sl])
        dp_ref[:, 512:1024] = _b((du * ax)[sl])
        dp_ref[:, 1024:1536] = _b((du * ac)[sl])
        dp_ref[:, 1536:2048] = _b((dye * ab * cv * _dsilu(az))[sl])

    return pl.pallas_call(
        body, name="mixa_bwd", grid=(nt,),
        in_specs=[pl.BlockSpec((tT, 2048), lambda i: (i, OFF_A // 2048)), prev_spec, next_spec,
                  pl.BlockSpec((3, 512), lambda i: (0, 0)),
                  pl.BlockSpec((tT, 512), lambda i: (i, 0)), dnext_spec],
        out_specs=[pl.BlockSpec((tT, 2048), lambda i: (i, 0)), pl.BlockSpec((8, 512), lambda i: (0, 0))],
        out_shape=[jax.ShapeDtypeStruct((T, 2048), bf16), jax.ShapeDtypeStruct((8, 512), f32)],
        compiler_params=_cp(("arbitrary",)),
    )(p, p, p, cw, dy, dy)


def _l2n_fwd(y):
    return y * lax.rsqrt(jnp.sum(y * y, axis=-1, keepdims=True) + L2_EPS)


def mixc_pre_fwd(p, cw, alog_l, dtb_l):
    T = p.shape[0]
    tT = min(512, T)
    prev_spec, _ = _halo_specs(tT, T, 2048, OFF_CQKV // 2048)

    def body(p_ref, pp_ref, ps_ref, cw_ref, al_ref, dt_ref, q_ref, k_ref, v_ref, sm_ref):
        pp = jnp.where(pl.program_id(0) == 0, 0.0, pp_ref[...])
        xe = jnp.concatenate([pp, p_ref[...]], axis=0)
        cv = (cw_ref[0:1, :] * _shift_down(xe, 3) + cw_ref[1:2, :] * _shift_down(xe, 2)
              + cw_ref[2:3, :] * _shift_down(xe, 1) + cw_ref[3:4, :] * xe)[8:]
        y = _silu(cv)
        for hh in range(4):
            sl = slice(hh * HK, (hh + 1) * HK)
            q_ref[:, sl] = _l2n_fwd(y[:, sl]) * QK_SCALE
            k_ref[:, sl] = _l2n_fwd(y[:, 512 + hh * HK:512 + (hh + 1) * HK])
        v_ref[...] = y[:, 1024:2048]
        ps = ps_ref[...]
        lane = lax.broadcasted_iota(jnp.int32, ps.shape, 1)
        la = -jnp.exp(al_ref[...]) * _softplus(ps + dt_ref[...])
        rin = lax.broadcasted_iota(jnp.int32, ps.shape, 0) % L
        g = _chunk_cumsum(la, rin)
        sm_ref[...] = jnp.where(lane < 8, _sigmoid(ps), jnp.where(lane < 16, g, 0.0))

    return pl.pallas_call(
        body, name="mixc_pre_fwd", grid=(T // tT,),
        in_specs=[pl.BlockSpec((tT, 2048), lambda i: (i, OFF_CQKV // 2048)), prev_spec,
                  pl.BlockSpec((tT, 128), lambda i: (i, OFF_S // 128)),
                  pl.BlockSpec((4, 2048), lambda i: (0, 0)),
                  pl.BlockSpec((1, 128), lambda i: (0, 0)), pl.BlockSpec((1, 128), lambda i: (0, 0))],
        out_specs=[pl.BlockSpec((tT, 512), lambda i: (i, 0)), pl.BlockSpec((tT, 512), lambda i: (i, 0)),
                   pl.BlockSpec((tT, 1024), lambda i: (i, 0)), pl.BlockSpec((tT, 128), lambda i: (i, 0))],
        out_shape=[jax.ShapeDtypeStruct((T, 512), f32), jax.ShapeDtypeStruct((T, 512), f32),
                   jax.ShapeDtypeStruct((T, 1024), f32), jax.ShapeDtypeStruct((T, 128), f32)],
        compiler_params=_cp(("parallel",)),
    )(p, p, p, cw, alog_l, dtb_l)


def mixc_pre_bwd(p, cw, alog_l, dtb_l, dq8, dk8, dv, dsm8):
    T = p.shape[0]
    tT = min(256, T)
    nt = T // tT
    prev_spec, next_spec = _halo_specs(tT, T, 2048, OFF_CQKV // 2048)
    _, n1024 = _halo_specs(tT, T, 1024, 0)

    def body(p_ref, pp_ref, pn_ref, ps_ref, cw_ref, al_ref, dt_ref, dq_ref, dqn_ref, dk_ref, dkn_ref,
             dv_ref, dvn_ref, dsm_ref, dp_ref, dps_ref, dcw_ref, dsml_ref):
        i = pl.program_id(0)

        @pl.when(i == 0)
        def _():
            dcw_ref[...] = jnp.zeros_like(dcw_ref)
            dsml_ref[...] = jnp.zeros_like(dsml_ref)
        rows = lax.broadcasted_iota(jnp.int32, (tT + 16, 1), 0)
        pp = jnp.where(i == 0, 0.0, pp_ref[...])
        xe = jnp.concatenate([pp, p_ref[...], pn_ref[...]], axis=0)
        xs = [_shift_down(xe, 3), _shift_down(xe, 2), _shift_down(xe, 1), xe]
        cv = cw_ref[0:1, :] * xs[0] + cw_ref[1:2, :] * xs[1] + cw_ref[2:3, :] * xs[2] + cw_ref[3:4, :] * xs[3]
        y = _silu(cv)
        last = (rows >= tT + 8) & (i == nt - 1)
        z8q = jnp.zeros((8, 1024), f32)

        def ext(cur_ref, nxt_ref):
            return jnp.where(last, 0.0, jnp.concatenate([z8q, cur_ref[...], nxt_ref[...]], axis=0))
        dq8e, dk8e, dve = ext(dq_ref, dqn_ref), ext(dk_ref, dkn_ref), ext(dv_ref, dvn_ref)
        dys = []
        for (d8, base, scale) in ((dq8e, 0, QK_SCALE), (dk8e, 512, 1.0)):
            for hh in range(4):
                dn = (d8[:, (2 * hh) * HK:(2 * hh + 1) * HK] + d8[:, (2 * hh + 1) * HK:(2 * hh + 2) * HK]) * scale
                yh = y[:, base + hh * HK:base + (hh + 1) * HK]
                r = lax.rsqrt(jnp.sum(yh * yh, axis=-1, keepdims=True) + L2_EPS)
                nh = yh * r
                dys.append(r * (dn - nh * jnp.sum(dn * nh, axis=-1, keepdims=True)))
        dyy = jnp.concatenate(dys + [dve], axis=1)
        dcv = dyy * _dsilu(cv)
        dx = (cw_ref[3:4, :] * dcv + cw_ref[2:3, :] * _shift_up(dcv, 1) + cw_ref[1:2, :] * _shift_up(dcv, 2)
              + cw_ref[0:1, :] * _shift_up(dcv, 3))
        dp_ref[...] = _b(dx[8:tT + 8])
        inner = (rows >= 8) & (rows < tT + 8)
        dcv_in = jnp.where(inner, dcv, 0.0)
        for j in range(4):
            dcw_ref[j:j + 1, :] += jnp.sum(dcv_in * xs[j], axis=0, keepdims=True)
        ps = ps_ref[...]
        lane = lax.broadcasted_iota(jnp.int32, ps.shape, 1)
        dsm = dsm_ref[:, 0:128]
        for hh in range(1, N_GDN):
            dsm = dsm + dsm_ref[:, hh * 128:(hh + 1) * 128]
        beta = _sigmoid(ps)
        xa = ps + dt_ref[...]
        nea = -jnp.exp(al_ref[...])
        dpa = dsm * nea * _sigmoid(xa)
        dps_ref[...] = _b(jnp.where(lane < 8, dsm * beta * (1.0 - beta), jnp.where(lane < 16, dpa, 0.0)))
        amask = (lane >= 8) & (lane < 16)
        dsml_ref[0:1, :] += jnp.sum(jnp.where(amask, dsm * nea * _softplus(xa), 0.0), axis=0, keepdims=True)
        dsml_ref[1:2, :] += jnp.sum(jnp.where(amask, dpa, 0.0), axis=0, keepdims=True)

    cur1024 = pl.BlockSpec((tT, 1024), lambda i: (i, 0))
    return pl.pallas_call(
        body, name="mixc_pre_bwd", grid=(nt,),
        in_specs=[pl.BlockSpec((tT, 2048), lambda i: (i, OFF_CQKV // 2048)), prev_spec, next_spec,
                  pl.BlockSpec((tT, 128), lambda i: (i, OFF_S // 128)),
                  pl.BlockSpec((4, 2048), lambda i: (0, 0)),
                  pl.BlockSpec((1, 128), lambda i: (0, 0)), pl.BlockSpec((1, 128), lambda i: (0, 0)),
                  cur1024, n1024, cur1024, n1024, cur1024, n1024, cur1024],
        out_specs=[pl.BlockSpec((tT, 2048), lambda i: (i, 0)), pl.BlockSpec((tT, 128), lambda i: (i, 0)),
                   pl.BlockSpec((8, 2048), lambda i: (0, 0)), pl.BlockSpec((8, 128), lambda i: (0, 0))],
        out_shape=[jax.ShapeDtypeStruct((T, 2048), bf16), jax.ShapeDtypeStruct((T, 128), bf16),
                   jax.ShapeDtypeStruct((8, 2048), f32), jax.ShapeDtypeStruct((8, 128), f32)],
        compiler_params=_cp(("arbitrary",)),
    )(p, p, p, p, cw, alog_l, dtb_l, dq8, dq8, dk8, dk8, dv, dv, dsm8)


def _tri_inverse(m):
    r = lax.broadcasted_iota(jnp.int32, (L, L), 0)
    c = lax.broadcasted_iota(jnp.int32, (L, L), 1)
    eye = (r == c).astype(f32)[None]
    same = lambda w: ((r // w) == (c // w))[None]
    md = jnp.where(same(8), m, 0.0)
    m2 = _bdot_split(md, md)
    m4 = _bdot_split(m2, m2)
    t = _bdot_split(_bdot_split(eye - md, eye + m2), eye + m4)
    for w in (16, 32, 64):
        mo = jnp.where(same(w) & jnp.logical_not(same(w // 2)), m, 0.0)
        t = t - _bdot_split(_bdot_split(t, mo), t)
    return t


def _col_to_row(col, eye):
    return jnp.sum(eye * col, axis=1, keepdims=True)


def _row_to_col(row, eye):
    return jnp.sum(eye * row, axis=2, keepdims=True)


def _gdn_chunk_terms(q, k, v, beta, g, t_inv=None):
    r = lax.broadcasted_iota(jnp.int32, (L, L), 0)
    c = lax.broadcasted_iota(jnp.int32, (L, L), 1)
    eye = (r == c).astype(f32)[None]
    causal, strict = (c <= r)[None], (c < r)[None]
    diff = g - _col_to_row(g, eye)
    dec = jnp.exp(jnp.where(causal, diff, 0.0))
    dc = jnp.where(causal, dec, 0.0)
    ds = jnp.where(strict, dec, 0.0)
    eg = jnp.exp(g)
    gl = g[:, L - 1:L, :]
    egl = jnp.exp(gl - g)
    kb = k * beta
    kk = _bdot_nt(_b(k), _b(kb))
    qk = _bdot_nt(_b(q), _b(kb))
    m = kk * ds
    aqk = qk * dc
    if t_inv is None:
        t_inv = _tri_inverse(m)
    tb = _b(t_inv)
    keg = k * eg
    u = _bdot(tb, _b(v))
    w = _bdot(tb, _b(keg))
    ks = kb * egl
    ksw = _bdot_tn(_b(ks), _b(w))
    return dict(eye=eye, causal=causal, strict=strict, dc=dc, ds=ds, eg=eg, gl=gl, egl=egl, kb=kb, kk=kk, qk=qk,
                m=m, aqk=aqk, t=t_inv, u=u, w=w, qi=q * eg, ks=ks, keg=keg, ksw=ksw)


def gdn_fwd(qn, kn, vv, sm):
    T = qn.shape[0]
    tB = min(GDN_BLOCK, T)
    nc = tB // L
    N = T // L

    def body(q_ref, k_ref, v_ref, sm_ref, o_ref, st_ref, ti_ref, s_scr):
        h = pl.program_id(0)

        @pl.when(pl.program_id(1) == 0)
        def _():
            s_scr[...] = jnp.zeros_like(s_scr)
        smv = sm_ref[...]
        lane = lax.broadcasted_iota(jnp.int32, smv.shape, 1)
        beta = jnp.sum(jnp.where(lane == h, smv, 0.0), axis=1, keepdims=True).reshape(nc, L, 1)
        g = jnp.sum(jnp.where(lane == 8 + h, smv, 0.0), axis=1, keepdims=True).reshape(nc, L, 1)
        q = q_ref[...].reshape(nc, L, HK)
        k = k_ref[...].reshape(nc, L, HK)
        v = v_ref[...].reshape(nc, L, HK)
        tm = _gdn_chunk_terms(q, k, v, beta, g)
        ti_ref[...] = tm["t"]
        ksu = _bdot_tn(_b(tm["ks"]), _b(tm["u"]))
        kswb = _b(tm["ksw"])
        egl_last = jnp.exp(tm["gl"])
        s = s_scr[...]
        states = [None] * nc
        for ci in range(nc):
            states[ci] = s
            s = egl_last[ci] * s + (ksu[ci] - _dot(kswb[ci], _b(s)))
        s_scr[...] = s
        sall = jnp.stack(states, axis=0)
        st_ref[...] = sall
        sb = _b(sall)
        e = tm["u"] - _bdot(_b(tm["w"]), sb)
        o = _bdot(_b(tm["qi"]), sb) + _bdot(_b(tm["aqk"]), _b(e))
        o_ref[...] = o.reshape(tB, HK)

    return pl.pallas_call(
        body, name="gdn_fwd", grid=(N_GDN, T // tB),
        in_specs=[pl.BlockSpec((tB, HK), lambda h, n: (n, h // 2)), pl.BlockSpec((tB, HK), lambda h, n: (n, h // 2)),
                  pl.BlockSpec((tB, HK), lambda h, n: (n, h)), pl.BlockSpec((tB, 128), lambda h, n: (n, 0))],
        out_specs=[pl.BlockSpec((tB, HK), lambda h, n: (n, h)),
                   pl.BlockSpec((None, nc, HK, HK), lambda h, n: (h, n, 0, 0)),
                   pl.BlockSpec((None, nc, L, L), lambda h, n: (h, n, 0, 0))],
        out_shape=[jax.ShapeDtypeStruct((T, N_GDN * HK), f32), jax.ShapeDtypeStruct((N_GDN, N, HK, HK), f32),
                   jax.ShapeDtypeStruct((N_GDN, N, L, L), f32)],
        scratch_shapes=[pltpu.VMEM((HK, HK), f32)],
        compiler_params=_cp(("parallel", "arbitrary")),
    )(qn, kn, vv, sm)


def gdn_bwd(qn, kn, vv, sm, st, ti, do):
    T = qn.shape[0]
    tB = min(GDN_BLOCK, T)
    nc = tB // L
    nb = T // tB

    def body(q_ref, k_ref, v_ref, sm_ref, st_ref, ti_ref, do_ref, dq_ref, dk_ref, dv_ref, dsm_ref, ds_scr):
        h = pl.program_id(0)

        @pl.when(pl.program_id(1) == 0)
        def _():
            ds_scr[...] = jnp.zeros_like(ds_scr)
        smv = sm_ref[...]
        lane = lax.broadcasted_iota(jnp.int32, smv.shape, 1)
        beta = jnp.sum(jnp.where(lane == h, smv, 0.0), axis=1, keepdims=True).reshape(nc, L, 1)
        g = jnp.sum(jnp.where(lane == 8 + h, smv, 0.0), axis=1, keepdims=True).reshape(nc, L, 1)
        q = q_ref[...].reshape(nc, L, HK)
        k = k_ref[...].reshape(nc, L, HK)
        v = v_ref[...].reshape(nc, L, HK)
        do = do_ref[...].reshape(nc, L, HK)
        s = st_ref[...]
        tm = _gdn_chunk_terms(q, k, v, beta, g, t_inv=ti_ref[...])
        eye, dc, ds_, eg, egl = tm["eye"], tm["dc"], tm["ds"], tm["eg"], tm["egl"]
        kb, u, w, qi, ks, tinv = tm["kb"], tm["u"], tm["w"], tm["qi"], tm["ks"], tm["t"]
        sb, dob = _b(s), _b(do)
        e = u - _bdot(_b(w), sb)
        eb = _b(e)
        egl_last = jnp.exp(tm["gl"])
        de0 = _bdot_tn(_b(tm["aqk"]), dob)
        ds0 = _bdot_tn(_b(qi), dob) - _bdot_tn(_b(w), _b(de0))
        kswb = _b(tm["ksw"])
        dsn = ds_scr[...]
        dsns = [None] * nc
        for ci in reversed(range(nc)):
            dsns[ci] = dsn
            dsn = ds0[ci] + (egl_last[ci] * dsn - _dot_tn(kswb[ci], _b(dsn)))
        ds_scr[...] = dsn
        dsp = jnp.stack(dsns, axis=0)
        dspb = _b(dsp)
        de = de0 + _bdot(_b(ks), dspb)
        deb = _b(de)
        dks = _bdot_nt(eb, dspb)
        dqi = _bdot_nt(dob, sb)
        daqk = jnp.where(tm["causal"], _bdot_nt(dob, eb), 0.0)
        dw = -_bdot_nt(deb, sb)
        tb = _b(tinv)
        dvv = _bdot_tn(tb, deb)
        dkg = _bdot_tn(tb, _b(dw))
        dm = -jnp.where(tm["strict"], _bdot_nt(_b(dvv), _b(u)) + _bdot_nt(_b(dkg), _b(w)), 0.0)
        x = _b(dm * ds_)
        y = _b(daqk * dc)
        kbb, kbf, qbf = _b(kb), _b(k), _b(q)
        dk = _bdot(x, kbb) + dkg * eg
        dkb = _bdot_tn(x, kbf) + _bdot_tn(y, qbf) + dks * egl
        dq = _bdot(y, kbb) + dqi * eg
        dk = dk + dkb * beta
        dbeta = jnp.sum(dkb * k, axis=-1, keepdims=True)
        z = dm * tm["m"] + daqk * tm["aqk"]
        dg = (jnp.sum(dqi * qi - dks * ks + dkg * tm["keg"], axis=-1, keepdims=True)
              + jnp.sum(z, axis=-1, keepdims=True) - _row_to_col(jnp.sum(z, axis=1, keepdims=True), eye))
        dgl = (egl_last * jnp.sum(jnp.sum(s * dsp, axis=2, keepdims=True), axis=1, keepdims=True)
               + jnp.sum(jnp.sum(dks * ks, axis=2, keepdims=True), axis=1, keepdims=True))
        rowi = lax.broadcasted_iota(jnp.int32, (nc, L, 1), 1)
        dg = dg + jnp.where(rowi == L - 1, dgl, 0.0)
        dg2 = dg.reshape(tB, 1)
        rin = lax.broadcasted_iota(jnp.int32, (tB, 1), 0) % L
        dla = _chunk_rev_cumsum(jnp.broadcast_to(dg2, (tB, 128)), jnp.broadcast_to(rin, (tB, 128)))
        dq_ref[...] = dq.reshape(tB, HK)
        dk_ref[...] = dk.reshape(tB, HK)
        dv_ref[...] = dvv.reshape(tB, HK)
        dsm_ref[...] = jnp.where(lane == h, dbeta.reshape(tB, 1), jnp.where(lane == 8 + h, dla, 0.0))

    rev = lambda n: nb - 1 - n
    return pl.pallas_call(
        body, name="gdn_bwd", grid=(N_GDN, nb),
        in_specs=[pl.BlockSpec((tB, HK), lambda h, n: (rev(n), h // 2)),
                  pl.BlockSpec((tB, HK), lambda h, n: (rev(n), h // 2)),
                  pl.BlockSpec((tB, HK), lambda h, n: (rev(n), h)), pl.BlockSpec((tB, 128), lambda h, n: (rev(n), 0)),
                  pl.BlockSpec((None, nc, HK, HK), lambda h, n: (h, rev(n), 0, 0)),
                  pl.BlockSpec((None, nc, L, L), lambda h, n: (h, rev(n), 0, 0)),
                  pl.BlockSpec((tB, HK), lambda h, n: (rev(n), h))],
        out_specs=[pl.BlockSpec((tB, HK), lambda h, n: (rev(n), h))] * 4,
        out_shape=[jax.ShapeDtypeStruct((T, N_GDN * HK), f32)] * 4,
        scratch_shapes=[pltpu.VMEM((HK, HK), f32)],
        compiler_params=_cp(("parallel", "arbitrary")),
    )(qn, kn, vv, sm, st, ti, do)


def _hgrn_prep(bq, bf_, bi, lb):
    tB = bq.shape[0]
    q = _silu(bq) * QK_SCALE
    sg = _sigmoid(bf_)
    f = lb + (1.0 - lb) * sg
    logf = jnp.log(jnp.maximum(f, MIN_F))
    rin = lax.broadcasted_iota(jnp.int32, (tB, HK), 0) % L
    g = _chunk_cumsum(logf, rin)
    return q, sg, f, 1.0 - f, bi, g, rin


def _hgrn_intra(q, kk, v, g, do=None):
    n = q.shape[0]
    nsub = L // SUB
    bwd = do is not None
    o_rows = [None] * nsub
    if bwd:
        dq_rows = [None] * nsub
        dkk_acc = jnp.zeros_like(kk)
        dv_acc = jnp.zeros_like(v)
    for i in range(1, nsub):
        lo, hi, w = i * SUB, (i + 1) * SUB, i * SUB
        ref = g[:, lo - 1:lo, :]
        eq = jnp.exp(g[:, lo:hi, :] - ref)
        ek = jnp.exp(ref - g[:, :w, :])
        qs = _b(q[:, lo:hi, :] * eq)
        ks = _b(kk[:, :w, :] * ek)
        p = _bdot_nt(qs, ks)
        o_rows[i] = _bdot(_b(p), _b(v[:, :w, :]))
        if bwd:
            dob = _b(do[:, lo:hi, :])
            dp = _b(_bdot_nt(dob, _b(v[:, :w, :])))
            dq_rows[i] = _bdot(dp, ks) * eq
            pad = jnp.zeros((n, L - w, HK), f32)
            dkk_acc = dkk_acc + jnp.concatenate([_bdot_tn(dp, qs) * ek, pad], axis=1)
            dv_acc = dv_acc + jnp.concatenate([_bdot_tn(_b(p), dob), pad], axis=1)
    m = n * nsub
    q4, k4, v4, g4 = (a.reshape(m, SUB, HK) for a in (q, kk, v, g))
    r = lax.broadcasted_iota(jnp.int32, (m, SUB, HK), 1)
    od = jnp.zeros((m, SUB, HK), f32)
    if bwd:
        do4 = do.reshape(m, SUB, HK)
        dqd = jnp.zeros((m, SUB, HK), f32)
        dkd = jnp.zeros((m, SUB, HK), f32)
        dvd = jnp.zeros((m, SUB, HK), f32)
    for j in range(SUB):
        gj, kj, vj = g4[:, j:j + 1, :], k4[:, j:j + 1, :], v4[:, j:j + 1, :]
        ok = r >= j
        e = jnp.where(ok, jnp.exp(jnp.where(ok, g4 - gj, 0.0)), 0.0)
        xq = q4 * e
        pj = jnp.sum(xq * kj, axis=-1, keepdims=True)
        od = od + pj * vj
        if bwd:
            dpj = jnp.sum(do4 * vj, axis=-1, keepdims=True)
            dqd = dqd + dpj * kj * e
            dkd = dkd + jnp.where(r == j, jnp.sum(dpj * xq, axis=1, keepdims=True), 0.0)
            dvd = dvd + jnp.where(r == j, jnp.sum(pj * do4, axis=1, keepdims=True), 0.0)
    od = od.reshape(n, L, HK)
    o = jnp.concatenate([od[:, :SUB, :]] + [od[:, i * SUB:(i + 1) * SUB, :] + o_rows[i] for i in range(1, nsub)], axis=1)
    if not bwd:
        return o
    dqd = dqd.reshape(n, L, HK)
    dq = jnp.concatenate([dqd[:, :SUB, :]] + [dqd[:, i * SUB:(i + 1) * SUB, :] + dq_rows[i] for i in range(1, nsub)], axis=1)
    return o, dq, dkk_acc + dkd.reshape(n, L, HK), dv_acc + dvd.reshape(n, L, HK)


def hgrn_fwd(p, lbs):
    T = p.shape[0]
    tB = min(256, T)
    nc = tB // L
    N = T // L
    cq, cf, ci_ = OFF_B // HK, (OFF_B + 512) // HK, (OFF_B + 1024) // HK

    def body(bq_ref, bf_ref, bi_ref, lb_ref, o_ref, st_ref, s_scr):
        @pl.when(pl.program_id(1) == 0)
        def _():
            s_scr[...] = jnp.zeros_like(s_scr)
        q, sg, f, kk, v, g, rin = _hgrn_prep(bq_ref[...], bf_ref[...], bi_ref[...], lb_ref[...])
        q3, k3, v3, g3 = (a.reshape(nc, L, HK) for a in (q, kk, v, g))
        o = _hgrn_intra(q3, k3, v3, g3)
        gl = g3[:, L - 1:L, :]
        qt = _b(q3 * jnp.exp(g3))
        kt = _b(k3 * jnp.exp(gl - g3))
        vb = _b(v3)
        st = s_scr[...]
        for c in range(nc):
            st_ref[c] = st
            o_ref[c * L:(c + 1) * L, :] = o[c] + _dot_nt(qt[c], _b(st))
            st = st * jnp.exp(gl[c]) + _dot_tn(vb[c], kt[c])
        s_scr[...] = st

    return pl.pallas_call(
        body, name="hgrn_fwd", grid=(N_HGRN, T // tB),
        in_specs=[pl.BlockSpec((tB, HK), lambda h, n: (n, cq + h)), pl.BlockSpec((tB, HK), lambda h, n: (n, cf + h)),
                  pl.BlockSpec((tB, HK), lambda h, n: (n, ci_ + h)), pl.BlockSpec((1, HK), lambda h, n: (0, h))],
        out_specs=[pl.BlockSpec((tB, HK), lambda h, n: (n, h)),
                   pl.BlockSpec((None, nc, HK, HK), lambda h, n: (h, n, 0, 0))],
        out_shape=[jax.ShapeDtypeStruct((T, N_HGRN * HK), f32), jax.ShapeDtypeStruct((N_HGRN, N, HK, HK), f32)],
        scratch_shapes=[pltpu.VMEM((HK, HK), f32)],
        compiler_params=_cp(("parallel", "arbitrary")),
    )(p, p, p, lbs)


def hgrn_bwd(p, lbs, st, do):
    T = p.shape[0]
    tB = min(256, T)
    nc = tB // L
    nb = T // tB
    cq, cf, ci_ = OFF_B // HK, (OFF_B + 512) // HK, (OFF_B + 1024) // HK

    def body(bq_ref, bf_ref, bi_ref, lb_ref, st_ref, do_ref, dbq_ref, dbf_ref, dbi_ref, dlb_ref, ds_scr):
        @pl.when(pl.program_id(1) == 0)
        def _():
            ds_scr[...] = jnp.zeros_like(ds_scr)
            dlb_ref[...] = jnp.zeros_like(dlb_ref)
        lb = lb_ref[...]
        bq = bq_ref[...]
        q, sg, f, kk, v, g, rin = _hgrn_prep(bq, bf_ref[...], bi_ref[...], lb)
        q3, k3, v3, g3 = (a.reshape(nc, L, HK) for a in (q, kk, v, g))
        do3 = do_ref[...].reshape(nc, L, HK)
        dob = _b(do3)
        gl = g3[:, L - 1:L, :]
        egl = jnp.exp(gl)
        eg, egr = jnp.exp(g3), jnp.exp(gl - g3)
        qt, kt = q3 * eg, k3 * egr
        s = st_ref[...]
        ds0 = _bdot_tn(dob, _b(qt))
        dsn = ds_scr[...]
        dsns = [None] * nc
        for c in reversed(range(nc)):
            dsns[c] = dsn
            dsn = ds0[c] + dsn * egl[c]
        ds_scr[...] = dsn
        dsp = jnp.stack(dsns, axis=0)
        dspb = _b(dsp)
        dqt = _bdot(dob, _b(s))
        dkt = _bdot(_b(v3), dspb)
        dv_state = _bdot_nt(_b(kt), dspb)
        dgl = egl * jnp.sum(s * dsp, axis=1, keepdims=True) + jnp.sum(dkt * kt, axis=1, keepdims=True)
        _, dq_i, dkk_i, dv_i = _hgrn_intra(q3, k3, v3, g3, do=do3)
        dq = dq_i + dqt * eg
        dkk = dkk_i + dkt * egr
        dv = dv_i + dv_state
        rowi = lax.broadcasted_iota(jnp.int32, (nc, L, HK), 1)
        dg = q3 * dq - k3 * dkk + jnp.where(rowi == L - 1, dgl, 0.0)
        dlogf = _chunk_rev_cumsum(dg.reshape(tB, HK), rin)
        dkk2 = dkk.reshape(tB, HK)
        df = jnp.where(f > MIN_F, dlogf / f, 0.0) - dkk2
        dlb_ref[...] += jnp.sum(df * (1.0 - sg), axis=0, keepdims=True)
        dbf_ref[...] = _b(df * (1.0 - lb) * sg * (1.0 - sg))
        dbq_ref[...] = _b(dq.reshape(tB, HK) * QK_SCALE * _dsilu(bq))
        dbi_ref[...] = _b(dv.reshape(tB, HK))

    rev = lambda n: nb - 1 - n
    return pl.pallas_call(
        body, name="hgrn_bwd", grid=(N_HGRN, nb),
        in_specs=[pl.BlockSpec((tB, HK), lambda h, n: (rev(n), cq + h)),
                  pl.BlockSpec((tB, HK), lambda h, n: (rev(n), cf + h)),
                  pl.BlockSpec((tB, HK), lambda h, n: (rev(n), ci_ + h)), pl.BlockSpec((1, HK), lambda h, n: (0, h)),
                  pl.BlockSpec((None, nc, HK, HK), lambda h, n: (h, rev(n), 0, 0)),
                  pl.BlockSpec((tB, HK), lambda h, n: (rev(n), h))],
        out_specs=[pl.BlockSpec((tB, HK), lambda h, n: (rev(n), h))] * 3 + [pl.BlockSpec((1, HK), lambda h, n: (0, h))],
        out_shape=[jax.ShapeDtypeStruct((T, N_HGRN * HK), bf16)] * 3 + [jax.ShapeDtypeStruct((1, N_HGRN * HK), f32)],
        scratch_shapes=[pltpu.VMEM((HK, HK), f32)],
        compiler_params=_cp(("parallel", "arbitrary")),
    )(p, p, p, lbs, st, do)


def _headnorm_fwd(o, z, w, nheads):
    outs = []
    for hh in range(nheads):
        sl = slice(hh * HK, (hh + 1) * HK)
        oh = o[:, sl]
        r = lax.rsqrt(jnp.mean(oh * oh, axis=-1, keepdims=True) + NORM_EPS)
        outs.append(oh * r * w * _silu(z[:, sl]))
    return jnp.concatenate(outs, axis=1)


def _headnorm_bwd(o, z, w, dy, nheads):
    dos, dzs = [], []
    dw = jnp.zeros((1, HK), f32)
    for hh in range(nheads):
        sl = slice(hh * HK, (hh + 1) * HK)
        oh, zh, dyh = o[:, sl], z[:, sl], dy[:, sl]
        r = lax.rsqrt(jnp.mean(oh * oh, axis=-1, keepdims=True) + NORM_EPS)
        on = oh * r
        sz = _silu(zh)
        dn = dyh * sz * w
        dos.append(r * (dn - on * jnp.mean(dn * on, axis=-1, keepdims=True)))
        dzs.append(dyh * on * w * _dsilu(zh))
        dw = dw + jnp.sum(dyh * sz * on, axis=0, keepdims=True)
    return jnp.concatenate(dos, axis=1), jnp.concatenate(dzs, axis=1), dw


def _merge_specs(tT, l):
    row = lambda w, cb=0: pl.BlockSpec((tT, w), lambda i, cb=cb: (i, cb))
    full = lambda r, c: pl.BlockSpec((r, c), lambda i: (0, 0))
    layer = lambda r, c: pl.BlockSpec((None, r, c), lambda i: (l, 0, 0))
    return row, full, layer


def merge_fwd(x, p, ya, ob, oc, hw, gw, bg, woa, wob, woc, wo, l):
    T = x.shape[0]
    tT = min(256, T)
    row, full, layer = _merge_specs(tT, l)

    def body(x_ref, bz_ref, cz_ref, g_ref, ya_ref, ob_ref, oc_ref, hw_ref, gw_ref, bg_ref,
             woa_ref, wob_ref, woc_ref, wo_ref, out_ref):
        yb = _b(_headnorm_fwd(ob_ref[...], bz_ref[...], hw_ref[...], N_HGRN))
        yc = _b(_headnorm_fwd(oc_ref[...], cz_ref[...], gw_ref[...], N_GDN))
        gates = _sigmoid(g_ref[...] + bg_ref[...])
        merged = (gates[:, 0:D] * _dot(ya_ref[...], woa_ref[...]) + gates[:, D:2 * D] * _dot(yb, wob_ref[...])
                  + gates[:, 2 * D:3 * D] * _dot(yc, woc_ref[...]))
        out_ref[...] = x_ref[...] + _dot(_b(merged), wo_ref[...])

    return pl.pallas_call(
        body, name="merge_fwd", grid=(T // tT,),
        in_specs=[row(D), row(512, (OFF_B + 1536) // 512), row(1024, OFF_CZ // 1024), row(3072, OFF_G // 3072),
                  row(512), row(512), row(1024), full(1, HK), full(1, HK), full(1, 3 * D),
                  layer(512, D), layer(512, D), layer(D, D), layer(D, D)],
        out_specs=row(D),
        out_shape=jax.ShapeDtypeStruct((T, D), f32),
        compiler_params=_cp(("parallel",)),
    )(x, p, p, p, ya, ob, oc, hw, gw, bg, woa, wob, woc, wo)


def merge_bwd(dxo, p, ya, ob, oc, hw, gw, bg, woa, wob, woc, wo, l):
    T = dxo.shape[0]
    tT = min(256, T)
    row, full, layer = _merge_specs(tT, l)

    def body(dx_ref, bz_ref, cz_ref, g_ref, ya_ref, ob_ref, oc_ref, hw_ref, gw_ref, bg_ref,
             woa_ref, wob_ref, woc_ref, wo_ref,
             dya_ref, dob_ref, doc_ref, dbz_ref, dcz_ref, dg_ref, mg_ref, dy3_ref, yb_ref, yc_ref,
             dbg_ref, dhw_ref, dgw_ref):
        @pl.when(pl.program_id(0) == 0)
        def _():
            dbg_ref[...] = jnp.zeros_like(dbg_ref)
            dhw_ref[...] = jnp.zeros_like(dhw_ref)
            dgw_ref[...] = jnp.zeros_like(dgw_ref)
        ob, oc, bz, cz = ob_ref[...], oc_ref[...], bz_ref[...], cz_ref[...]
        hw_, gw_ = hw_ref[...], gw_ref[...]
        yb = _b(_headnorm_fwd(ob, bz, hw_, N_HGRN))
        yc = _b(_headnorm_fwd(oc, cz, gw_, N_GDN))
        yb_ref[...] = yb
        yc_ref[...] = yc
        gates = _sigmoid(g_ref[...] + bg_ref[...])
        ys = (_dot(ya_ref[...], woa_ref[...]), _dot(yb, wob_ref[...]), _dot(yc, woc_ref[...]))
        dmerged = _dot_nt(_b(dx_ref[...]), wo_ref[...])
        merged = jnp.zeros_like(dmerged)
        dys = []
        for i in range(3):
            gi = gates[:, i * D:(i + 1) * D]
            merged = merged + gi * ys[i]
            dyi = _b(dmerged * gi)
            dys.append(dyi)
            dy3_ref[:, i * D:(i + 1) * D] = dyi
            dgp = dmerged * ys[i] * gi * (1.0 - gi)
            dg_ref[:, i * D:(i + 1) * D] = _b(dgp)
            dbg_ref[:, i * D:(i + 1) * D] += jnp.sum(dgp, axis=0, keepdims=True)
        mg_ref[...] = _b(merged)
        dya_ref[...] = _dot_nt(dys[0], woa_ref[...])
        dob, dbz, dhw = _headnorm_bwd(ob, bz, hw_, _dot_nt(dys[1], wob_ref[...]), N_HGRN)
        doc, dcz, dgw = _headnorm_bwd(oc, cz, gw_, _dot_nt(dys[2], woc_ref[...]), N_GDN)
        dob_ref[...] = dob
        doc_ref[...] = doc
        dbz_ref[...] = _b(dbz)
        dcz_ref[...] = _b(dcz)
        dhw_ref[...] += dhw
        dgw_ref[...] += dgw

    sd = jax.ShapeDtypeStruct
    return pl.pallas_call(
        body, name="merge_bwd", grid=(T // tT,),
        in_specs=[row(D), row(512, (OFF_B + 1536) // 512), row(1024, OFF_CZ // 1024), row(3072, OFF_G // 3072),
                  row(512), row(512), row(1024), full(1, HK), full(1, HK), full(1, 3 * D),
                  layer(512, D), layer(512, D), layer(D, D), layer(D, D)],
        out_specs=[row(512), row(512), row(1024), row(512), row(1024), row(3072), row(D), row(3 * D), row(512),
                   row(1024), full(1, 3 * D), full(1, HK), full(1, HK)],
        out_shape=[sd((T, 512), f32), sd((T, 512), f32), sd((T, 1024), f32), sd((T, 512), bf16), sd((T, 1024), bf16),
                   sd((T, 3072), bf16), sd((T, D), bf16), sd((T, 3 * D), bf16), sd((T, 512), bf16),
                   sd((T, 1024), bf16), sd((1, 3 * D), f32), sd((1, HK), f32), sd((1, HK), f32)],
        compiler_params=_cp(("arbitrary",)),
    )(dxo, p, p, p, ya, ob, oc, hw, gw, bg, woa, wob, woc, wo)


def layer_fwd(x, w):
    l = w["l"]
    p, h = inproj_fwd(x, w["norm_w"], w["w_in"], l)
    ya = mixa_fwd(p, w["conv_a"])
    qn, kn, vv, sm = mixc_pre_fwd(p, w["conv_c"], w["alog_l"], w["dtb_l"])
    oc, st_c, ti = gdn_fwd(qn, kn, vv, sm)
    ob, st_b = hgrn_fwd(p, w["lbs"])
    xo = merge_fwd(x, p, ya, ob, oc, w["hgrn_norm_w"], w["gdn_norm_w"], w["b_gate"],
                   w["w_out_a"], w["w_out_b"], w["w_out_c"], w["w_o"], l)
    saved = dict(x=x, p=p, h=h, ya=ya, qn=qn, kn=kn, vv=vv, sm=sm, oc=oc, st_c=st_c, ti=ti, ob=ob, st_b=st_b)
    return xo, saved


def layer_bwd(dxo, w, s):
    p, l = s["p"], w["l"]
    (dya, dob, doc, dbz, dcz, dg, merged, dy3, yb, yc, dbg, dhw, dgw) = merge_bwd(
        dxo, p, s["ya"], s["ob"], s["oc"], w["hgrn_norm_w"], w["gdn_norm_w"], w["b_gate"],
        w["w_out_a"], w["w_out_b"], w["w_out_c"], w["w_o"], l)
    g = {}
    g["w_o"] = matmul_tn(merged, dxo, "dw_o", with_bf16=True)
    g["w_out_a"] = matmul_tn(s["ya"], dy3, "dw_out_a", n=D, b_col0=0, with_bf16=True)
    g["w_out_b"] = matmul_tn(yb, dy3, "dw_out_b", n=D, b_col0=D, with_bf16=True)
    g["w_out_c"] = matmul_tn(yc, dy3, "dw_out_c", n=D, b_col0=2 * D, with_bf16=True)
    dbq, dbf, dbi, dlbs = hgrn_bwd(p, w["lbs"], s["st_b"], dob)
    dq8, dk8, dvv, dsm8 = gdn_bwd(s["qn"], s["kn"], s["vv"], s["sm"], s["st_c"], s["ti"], doc)
    dpc, dps, dcc, dsmall = mixc_pre_bwd(p, w["conv_c"], w["alog_l"], w["dtb_l"], dq8, dk8, dvv, dsm8)
    dpa, dca = mixa_bwd(p, w["conv_a"], dya)
    dp = jnp.concatenate([dpa, dbq, dbf, dbi, dbz, dpc, dg, dcz, dps], axis=1)
    g["w_in"] = matmul_tn(s["h"], dp, "dw_in")
    dx, dnw = inproj_bwd(dp, w["w_in"], l, s["x"], w["norm_w"], dxo)
    g.update(norm_w=dnw, b_gate=dbg, hgrn_norm_w=dhw, gdn_norm_w=dgw, lbs=dlbs, conv_a=dca[0:3], conv_c=dcc[0:4],
             a_log=dsmall[0:1, 8:16], dt_bias=dsmall[1:2, 8:16])
    return dx, g


def lbs_fwd(lb):
    def body(lb_ref, o_ref):
        l0, l1 = lb_ref[0:1, :], lb_ref[1:2, :]
        mx = jnp.maximum(l0, l1)
        e0, e1 = jnp.exp(l0 - mx), jnp.exp(l1 - mx)
        o_ref[0:1, :] = jnp.zeros_like(l0)
        o_ref[1:2, :] = e1 / (e0 + e1)
    return pl.pallas_call(body, name="lbs_fwd", out_shape=jax.ShapeDtypeStruct(lb.shape, f32))(lb)


def _adam_math(w, g, m, v):
    mn = ADAM_B1 * m + (1.0 - ADAM_B1) * g
    vn = ADAM_B2 * v + (1.0 - ADAM_B2) * (g * g)
    mh = mn / (1.0 - ADAM_B1 ** ADAM_STEP)
    vh = vn / (1.0 - ADAM_B2 ** ADAM_STEP)
    return -ADAM_LR * (mh / (jnp.sqrt(vh) + ADAM_EPS) + ADAM_WD * w), mn, vn


def adam(w, g, m, v, name):
    R, C = w.shape
    tr = 256 if R % 256 == 0 else R

    def body(w_ref, g_ref, m_ref, v_ref, d_ref, mo_ref, vo_ref):
        d, mn, vn = _adam_math(w_ref[...], g_ref[...], m_ref[...], v_ref[...])
        d_ref[...] = d
        mo_ref[...] = mn
        vo_ref[...] = vn

    spec = pl.BlockSpec((tr, C), lambda i: (i, 0))
    return pl.pallas_call(
        body, name=name, grid=(R // tr,), in_specs=[spec] * 4, out_specs=[spec] * 3,
        out_shape=[jax.ShapeDtypeStruct((R, C), f32)] * 3, compiler_params=_cp(("parallel",)),
    )(w, g, m, v)


def adam_pair(h, hs, w, m, v, name):
    _, R, C = w.shape
    cp = h.shape[2]
    tr = 256 if R % 256 == 0 else R

    def body(h_ref, hs_ref, w_ref, m_ref, v_ref, g_ref, d_ref, mo_ref, vo_ref):
        g = (h_ref[...] + hs_ref[...])[:, :C]
        d, mn, vn = _adam_math(w_ref[...], g, m_ref[...], v_ref[...])
        g_ref[...] = g
        d_ref[...] = d
        mo_ref[...] = mn
        vo_ref[...] = vn

    hspec = pl.BlockSpec((None, tr, cp), lambda l, i: (l, i, 0))
    spec = pl.BlockSpec((None, tr, C), lambda l, i: (l, i, 0))
    return pl.pallas_call(
        body, name=name, grid=(2, R // tr), in_specs=[hspec, hspec, spec, spec, spec], out_specs=[spec] * 4,
        out_shape=[jax.ShapeDtypeStruct(w.shape, f32)] * 4, compiler_params=_cp(("parallel", "parallel")),
    )(h, hs, w, m, v)


_SMALL = (("norm_w", 2 * D), ("b_gate", 6 * D), ("lower_bounds", None), ("hgrn_norm_w", 2 * HK),
          ("gdn_norm_w", 2 * HK), ("a_log", 16), ("dt_bias", 16), ("final_norm_w", D), ("loss", None))
_CONV = (("conv_a", 2 * 3 * 512), ("conv_c", 2 * 4 * 2048))


def _small_rows(n):
    return 16 if n is None else -(-n // 1024) * 8


LB_ROW = sum(_small_rows(n) for _, n in _SMALL[:2])
ADAM_ROWS = sum(_small_rows(n) for _, n in _SMALL)
SMALL_ROWS = ADAM_ROWS + sum(_small_rows(n) for _, n in _CONV)


def small_update(parts, wp, mp, vp):
    def body(p_ref, w_ref, m_ref, v_ref, g_ref, d_ref, mo_ref, vo_ref):
        gs = p_ref[0]
        for i in range(1, 8):
            gs = gs + p_ref[i]
        w = w_ref[...]
        l0, l1 = w[LB_ROW:LB_ROW + 8], w[LB_ROW + 8:LB_ROW + 16]
        mx = jnp.maximum(l0, l1)
        e0, e1 = jnp.exp(l0 - mx), jnp.exp(l1 - mx)
        p0, p1 = e0 / (e0 + e1), e1 / (e0 + e1)
        dl1 = gs[LB_ROW + 8:LB_ROW + 16]
        s = p1 * dl1
        g = jnp.concatenate([gs[0:LB_ROW], -p0 * s, p1 * dl1 - p1 * s, gs[LB_ROW + 16:ADAM_ROWS]], axis=0)
        d, mn, vn = _adam_math(w, g, m_ref[...], v_ref[...])
        g_ref[0:ADAM_ROWS, :] = g
        g_ref[ADAM_ROWS:, :] = gs[ADAM_ROWS:]
        d_ref[...] = d
        mo_ref[...] = mn
        vo_ref[...] = vn
    sd = jax.ShapeDtypeStruct
    return pl.pallas_call(body, name="small_update",
                          out_shape=[sd((SMALL_ROWS, 128), f32)] + [sd((ADAM_ROWS, 128), f32)] * 3)(parts, wp, mp, vp)


def _chip_spec(block, index_of):
    return pl.BlockSpec(block, lambda l, i, chip: index_of(l, i, chip[0]))


def partial_sum_shards(own, recv, chip, name):
    _, _, R, C = own.shape
    tr = 256 if R % 256 == 0 else R

    def body(chip_ref, o_ref, r_ref, out_ref):
        out_ref[...] = ((o_ref[...] + r_ref[0].astype(f32)) + r_ref[1].astype(f32)) + r_ref[2].astype(f32)

    return pl.pallas_call(
        body, name=name,
        grid_spec=pltpu.PrefetchScalarGridSpec(
            num_scalar_prefetch=1, grid=(2, R // tr),
            in_specs=[_chip_spec((None, None, tr, C), lambda l, i, c: (c, l, i, 0)),
                      pl.BlockSpec((3, None, tr, C), lambda l, i, chip: (0, l, i, 0))],
            out_specs=pl.BlockSpec((None, tr, C), lambda l, i, chip: (l, i, 0))),
        out_shape=jax.ShapeDtypeStruct((2, R, C), f32), compiler_params=_cp(("arbitrary", "arbitrary")),
    )(chip, own, recv)


def partial_sum_window(own0, own1, recv, chip, axis, name):
    _, _, r, c = recv.shape

    def body(chip_ref, o0_ref, o1_ref, r_ref, out_ref):
        l = pl.program_id(0)
        rest = (r_ref[0].astype(f32), r_ref[1].astype(f32), r_ref[2].astype(f32))

        @pl.when(l == 0)
        def _():
            out_ref[...] = ((o0_ref[...] + rest[0]) + rest[1]) + rest[2]

        @pl.when(l == 1)
        def _():
            out_ref[...] = ((o1_ref[...] + rest[0]) + rest[1]) + rest[2]

    win = _chip_spec((r, c), (lambda l, i, ch: (0, ch)) if axis == 1 else (lambda l, i, ch: (ch, 0)))
    return pl.pallas_call(
        body, name=name,
        grid_spec=pltpu.PrefetchScalarGridSpec(
            num_scalar_prefetch=1, grid=(2, 1),
            in_specs=[win, win, pl.BlockSpec((3, None, r, c), lambda l, i, chip: (0, l, 0, 0))],
            out_specs=pl.BlockSpec((None, r, c), lambda l, i, chip: (l, 0, 0))),
        out_shape=jax.ShapeDtypeStruct((2, r, c), f32), compiler_params=_cp(("arbitrary", "arbitrary")),
    )(chip, own0, own1, recv)


MESH = pl.DeviceIdType.MESH
_HBM = pl.BlockSpec(memory_space=pltpu.HBM)


def _place():
    return lax.axis_index("x"), lax.axis_index("y"), lax.axis_index("c")


def weight_gather(arrs):
    n = len(arrs)

    def body(*refs):
        x_refs, out_refs = refs[:n], refs[n:2 * n]
        send_sems, recv_sems, local_sems = refs[2 * n:]
        x, y, c = _place()
        me, sibling = (x, y, c), (x, y, 1 - c)
        chips = [(1 - x, y), (x, 1 - y), (1 - x, 1 - y)]

        def copy(a, k, block, to, own_src=False):
            px, py, pc = block
            dst = out_refs[a].at[2 * px + py, pc]
            return pltpu.make_async_remote_copy(
                src_ref=x_refs[a].at[c] if own_src else dst, dst_ref=dst,
                send_sem=send_sems.at[7 * a + k], recv_sem=recv_sems.at[7 * a + k], device_id=to, device_id_type=MESH)

        mine = [pltpu.make_async_copy(x_refs[a].at[c], out_refs[a].at[2 * x + y, c], local_sems.at[a])
                for a in range(n)]
        for cp in mine:
            cp.start()
        first = []
        for a in range(n):
            first.append(copy(a, 0, me, sibling, own_src=True))
            first += [copy(a, 1 + j, me, (*chip, c), own_src=True) for j, chip in enumerate(chips)]
        for cp in first:
            cp.start()
        passed = []
        for j, chip in enumerate(chips):
            for a in range(n):
                copy(a, 1 + j, (*chip, c), me).wait_recv()
                fwd = copy(a, 4 + j, (*chip, c), sibling)
                fwd.start()
                passed.append(fwd)
        for a in range(n):
            copy(a, 0, sibling, me).wait_recv()
            for j, chip in enumerate(chips):
                copy(a, 4 + j, (*chip, 1 - c), me).wait_recv()
        for cp in first + passed:
            cp.wait_send()
        for cp in mine:
            cp.wait()

    return pl.pallas_call(
        body, name="weight_gather", in_specs=[_HBM] * n, out_specs=[_HBM] * n,
        out_shape=[jax.ShapeDtypeStruct((N_CHIPS,) + a.shape, a.dtype) for a in arrs],
        scratch_shapes=[pltpu.SemaphoreType.DMA((7 * n,)), pltpu.SemaphoreType.DMA((7 * n,)),
                        pltpu.SemaphoreType.DMA((n,))],
    )(*arrs)


SHARD_W = 256


def grad_exchange(gb_win, col_mats, row_mats, small):
    mats = [(m, 1) for m in col_mats] + [(m, 0) for m in row_mats]
    nm = len(mats)
    S = small.shape[0]
    n_copies = 3 + 6 * nm + 7

    def win_shape(m, axis):
        r, c = m[0].shape
        return (r, SHARD_W) if axis == 1 else (SHARD_W, c)

    def body(*refs):
        gb_ref = refs[0]
        mat_refs = refs[1:1 + 2 * nm]
        sm_ref = refs[1 + 2 * nm]
        outs = refs[2 + 2 * nm:4 + 3 * nm]
        recv_win, recv_mats, smalls_ref = outs[0], outs[1:1 + nm], outs[1 + nm]
        send_sems, recv_sems, local_sem = refs[4 + 3 * nm:]
        x, y, c = _place()
        chips = [(1 - x, y), (x, 1 - y), (1 - x, 1 - y)]
        my_slot = smalls_ref.at[4 * x + 2 * y + c]
        mine = pltpu.make_async_copy(sm_ref, my_slot, local_sem)
        mine.start()
        copies = []

        def add(src, dst, to):
            k = len(copies)
            copies.append(pltpu.make_async_remote_copy(src_ref=src, dst_ref=dst, send_sem=send_sems.at[k],
                                                       recv_sem=recv_sems.at[k], device_id=to, device_id_type=MESH))

        for j, (px, py) in enumerate(chips):
            q = 2 * px + py
            add(gb_ref.at[q], recv_win.at[j], (px, py, c))
            lo = pl.multiple_of(q * SHARD_W, SHARD_W)
            for mi, (_, axis) in enumerate(mats):
                for l in range(2):
                    full = mat_refs[2 * mi + l]
                    src = full.at[:, pl.ds(lo, SHARD_W)] if axis == 1 else full.at[pl.ds(lo, SHARD_W), :]
                    add(src, recv_mats[mi].at[j, l], (px, py, c))
        for mask in range(1, 8):
            fx, fy, fc = (mask >> 2) & 1, (mask >> 1) & 1, mask & 1
            peer = ((1 - x) if fx else x, (1 - y) if fy else y, (1 - c) if fc else c)
            add(sm_ref, my_slot, peer)
        for cp in copies:
            cp.start()
        for cp in copies:
            cp.wait_recv()
        for cp in copies:
            cp.wait_send()
        mine.wait()

    sd = jax.ShapeDtypeStruct
    out_shape = ([sd((3,) + gb_win.shape[1:], gb_win.dtype)]
                 + [sd((3, 2) + win_shape(m, axis), m[0].dtype) for m, axis in mats] + [sd((8, S, 128), f32)])
    flat = [a for m, _ in mats for a in m]
    return pl.pallas_call(
        body, name="grad_exchange", in_specs=[_HBM] * (2 + 2 * nm), out_specs=[_HBM] * (2 + nm), out_shape=out_shape,
        scratch_shapes=[pltpu.SemaphoreType.DMA((n_copies,)), pltpu.SemaphoreType.DMA((n_copies,)),
                        pltpu.SemaphoreType.DMA],
    )(gb_win, *flat, small)


def sibling_swap(hs):
    n = len(hs)

    def body(*refs):
        h_refs, out_refs, send_sems, recv_sems = refs[:n], refs[n:2 * n], refs[2 * n], refs[2 * n + 1]
        x, y, c = _place()
        copies = [pltpu.make_async_remote_copy(src_ref=h_refs[a], dst_ref=out_refs[a], send_sem=send_sems.at[a],
                                               recv_sem=recv_sems.at[a], device_id=(x, y, 1 - c), device_id_type=MESH)
                  for a in range(n)]
        for cp in copies:
            cp.start()
        for cp in copies:
            cp.wait()

    return pl.pallas_call(
        body, name="sibling_swap", in_specs=[_HBM] * n, out_specs=[_HBM] * n,
        out_shape=[jax.ShapeDtypeStruct(h.shape, h.dtype) for h in hs],
        scratch_shapes=[pltpu.SemaphoreType.DMA((n,)), pltpu.SemaphoreType.DMA((n,))],
    )(*hs)


N_CHIPS = 4
SHARD_COLS = N_ORIG // N_CHIPS


SHARD_PAD = 2688
_COL_SEGMENTS = ((0, 6144, 0), (6144, 6160, OFF_S), (6160, 7184, OFF_CZ), (7184, N_ORIG, OFF_G))


def _shard_pieces():
    pieces = []
    for lo, hi, dst in _COL_SEGMENTS:
        for p in range(N_CHIPS):
            a, b = max(lo, p * SHARD_COLS), min(hi, (p + 1) * SHARD_COLS)
            if a < b:
                pieces.append((p, a - p * SHARD_COLS, dst + a - lo, b - a))
    return pieces


def win_cast_pad(w):
    tr = 256

    def body(x_ref, o_ref):
        o_ref[:, :SHARD_COLS] = _b(x_ref[...])
        o_ref[:, SHARD_COLS:] = jnp.zeros((tr, SHARD_PAD - SHARD_COLS), bf16)

    return pl.pallas_call(
        body, name="win_cast_pad", grid=(2, D // tr),
        in_specs=[pl.BlockSpec((None, tr, SHARD_COLS), lambda l, i: (l, i, 0))],
        out_specs=pl.BlockSpec((None, tr, SHARD_PAD), lambda l, i: (l, i, 0)),
        out_shape=jax.ShapeDtypeStruct((2, D, SHARD_PAD), bf16), compiler_params=_cp(("parallel", "parallel")),
    )(w)


def win_to_padded(w4):
    tr = 256
    pieces = _shard_pieces()

    def body(a_ref, o_ref):
        o_ref[:, N_ORIG:] = jnp.zeros((tr, NP - N_ORIG), bf16)
        for p, j0, c0, n in pieces:
            o_ref[:, c0:c0 + n] = a_ref[p, :, j0:j0 + n]

    return pl.pallas_call(
        body, name="win_to_padded", grid=(2, D // tr),
        in_specs=[pl.BlockSpec((N_CHIPS, None, tr, SHARD_PAD), lambda l, i: (0, l, i, 0))],
        out_specs=pl.BlockSpec((None, tr, NP), lambda l, i: (l, i, 0)),
        out_shape=jax.ShapeDtypeStruct((2, D, NP), bf16), compiler_params=_cp(("parallel", "parallel")),
    )(w4)


def win_from_padded(dw0, dw1):
    tr = 64
    nt = D // tr
    pieces = _shard_pieces()

    def body(d0_ref, d1_ref, of_ref, ob_ref):
        def put(src):
            for p in range(N_CHIPS):
                of_ref[p, :, SHARD_COLS:] = jnp.zeros((tr, SHARD_PAD - SHARD_COLS), f32)
                ob_ref[p, :, SHARD_COLS:] = jnp.zeros((tr, SHARD_PAD - SHARD_COLS), bf16)
            for p, j0, c0, n in pieces:
                v = src[:, c0:c0 + n]
                of_ref[p, :, j0:j0 + n] = v
                ob_ref[p, :, j0:j0 + n] = _b(v)

        @pl.when(pl.program_id(0) == 0)
        def _():
            put(d0_ref)

        @pl.when(pl.program_id(0) == 1)
        def _():
            put(d1_ref)

    out_spec = pl.BlockSpec((N_CHIPS, None, tr, SHARD_PAD), lambda l, i: (0, l, i, 0))
    return pl.pallas_call(
        body, name="win_from_padded", grid=(2, nt),
        in_specs=[pl.BlockSpec((tr, NP), lambda l, i: (jnp.where(l == 0, i, nt - 1), 0)),
                  pl.BlockSpec((tr, NP), lambda l, i: (jnp.where(l == 1, i, 0), 0))],
        out_specs=[out_spec, out_spec],
        out_shape=[jax.ShapeDtypeStruct((N_CHIPS, 2, D, SHARD_PAD), f32),
                   jax.ShapeDtypeStruct((N_CHIPS, 2, D, SHARD_PAD), bf16)],
        compiler_params=_cp(("arbitrary", "arbitrary")),
    )(dw0, dw1)


def _rows128(a):
    flat = a.reshape(-1)
    total = -(-flat.shape[0] // 1024) * 1024
    return jnp.pad(flat, (0, total - flat.shape[0])).reshape(total // 128, 128)


def _lb_rows(lb):
    return jnp.pad(lb.reshape(2, 4, 128), ((0, 0), (0, 4), (0, 0))).reshape(16, 128)


def _pack_small(v, with_conv):
    rows = []
    for name, n in _SMALL + (_CONV if with_conv else ()):
        if name == "lower_bounds":
            rows.append(_lb_rows(v[name]))
        elif name == "loss":
            rows.append(jnp.broadcast_to(v[name], (16, 128)) if name in v else jnp.zeros((16, 128), f32))
        else:
            rows.append(_rows128(v[name]))
    return jnp.concatenate(rows, axis=0)


def _unpack_small(p, shapes, with_conv):
    out, row = {}, 0
    for name, n in _SMALL + (_CONV if with_conv else ()):
        nrows = _small_rows(n)
        blk = p[row:row + nrows]
        if name == "lower_bounds":
            out[name] = blk.reshape(2, 8, 128)[:, :4].reshape(2, 512)
        elif name == "loss":
            out[name] = blk[0, 0]
        else:
            out[name] = blk.reshape(-1)[:n].reshape(shapes[name])
        row += nrows
    return out


def _lane_vec(a8):
    return jnp.pad(a8.reshape(1, 8), ((0, 0), (8, 112)))


WEIGHT_NAMES = ("norm_w", "w_in", "b_gate", "conv_a", "conv_c", "a_log", "dt_bias", "lower_bounds", "hgrn_norm_w",
                "gdn_norm_w", "w_out_a", "w_out_b", "w_out_c", "w_o", "final_norm_w")


def kernel(x, norm_w, w_in, b_gate, conv_a, conv_c, a_log, dt_bias, lower_bounds, hgrn_norm_w, gdn_norm_w, w_out_a, w_out_b, w_out_c, w_o, final_norm_w, loss_target, m_norm_w, m_w_in, m_b_gate, m_conv_a, m_conv_c, m_a_log, m_dt_bias, m_lower_bounds, m_hgrn_norm_w, m_gdn_norm_w, m_w_out_a, m_w_out_b, m_w_out_c, m_w_o, m_final_norm_w, v_norm_w, v_w_in, v_b_gate, v_conv_a, v_conv_c, v_a_log, v_dt_bias, v_lower_bounds, v_hgrn_norm_w, v_gdn_norm_w, v_w_out_a, v_w_out_b, v_w_out_c, v_w_o, v_final_norm_w):
    wts = dict(norm_w=norm_w, w_in=w_in, b_gate=b_gate, conv_a=conv_a, conv_c=conv_c, a_log=a_log, dt_bias=dt_bias,
               lower_bounds=lower_bounds, hgrn_norm_w=hgrn_norm_w, gdn_norm_w=gdn_norm_w, w_out_a=w_out_a,
               w_out_b=w_out_b, w_out_c=w_out_c, w_o=w_o, final_norm_w=final_norm_w)
    mom = dict(norm_w=m_norm_w, w_in=m_w_in, b_gate=m_b_gate, conv_a=m_conv_a, conv_c=m_conv_c, a_log=m_a_log,
               dt_bias=m_dt_bias, lower_bounds=m_lower_bounds, hgrn_norm_w=m_hgrn_norm_w, gdn_norm_w=m_gdn_norm_w,
               w_out_a=m_w_out_a, w_out_b=m_w_out_b, w_out_c=m_w_out_c, w_o=m_w_o, final_norm_w=m_final_norm_w)
    var = dict(norm_w=v_norm_w, w_in=v_w_in, b_gate=v_b_gate, conv_a=v_conv_a, conv_c=v_conv_c, a_log=v_a_log,
               dt_bias=v_dt_bias, lower_bounds=v_lower_bounds, hgrn_norm_w=v_hgrn_norm_w, gdn_norm_w=v_gdn_norm_w,
               w_out_a=v_w_out_a, w_out_b=v_w_out_b, w_out_c=v_w_out_c, w_o=v_w_o, final_norm_w=v_final_norm_w)
    chip = 2 * lax.axis_index("x") + lax.axis_index("y")
    chip1 = chip.reshape(1).astype(jnp.int32)

    win4, woa4, wob4, woc4, wo4, ca4, cc4 = weight_gather(
        [win_cast_pad(w_in), _b(w_out_a), _b(w_out_b), _b(w_out_c), _b(w_o), conv_a, conv_c])
    by_cols = lambda a: a.transpose(1, 2, 0, 3).reshape(a.shape[1], a.shape[2], N_CHIPS * a.shape[3])
    by_rows = lambda a: a.transpose(1, 0, 2, 3).reshape(a.shape[1], N_CHIPS * a.shape[2], a.shape[3])
    full = dict(w_in=win_to_padded(win4), w_out_a=by_cols(woa4), w_out_b=by_cols(wob4), w_out_c=by_rows(woc4),
                w_o=by_rows(wo4), conv_a=by_cols(ca4), conv_c=by_cols(cc4))
    lbs = lbs_fwd(lower_bounds)
    layers = []
    for l in range(2):
        layers.append(dict(
            l=l, norm_w=norm_w[l:l + 1], w_in=full["w_in"], b_gate=b_gate[l:l + 1], conv_a=full["conv_a"][l],
            conv_c=full["conv_c"][l], alog_l=_lane_vec(a_log[l]), dtb_l=_lane_vec(dt_bias[l]), lbs=lbs[l:l + 1],
            hgrn_norm_w=hgrn_norm_w[l:l + 1], gdn_norm_w=gdn_norm_w[l:l + 1], w_out_a=full["w_out_a"],
            w_out_b=full["w_out_b"], w_out_c=full["w_out_c"], w_o=full["w_o"]))

    xs, saved = x[0], []
    for l in range(2):
        xs, s = layer_fwd(xs, layers[l])
        saved.append(s)
    loss_row, dx, dfw = loss_head(xs, final_norm_w.reshape(1, D), loss_target[0])
    lg = [None, None]
    for l in (1, 0):
        dx, lg[l] = layer_bwd(dx, layers[l], saved[l])
    grad_x = dx[None]

    stack = lambda n: jnp.stack([lg[0][n], lg[1][n]], axis=0)
    gsmall = {n: stack(n) for n in ("norm_w", "b_gate", "hgrn_norm_w", "gdn_norm_w", "a_log", "dt_bias", "conv_a",
                                    "conv_c")}
    gsmall.update(lower_bounds=stack("lbs"), final_norm_w=dfw, loss=loss_row)
    gf_win, gb_win = win_from_padded(lg[0]["w_in"], lg[1]["w_in"])
    col_names, row_names = ("w_out_a", "w_out_b"), ("w_out_c", "w_o")
    pair = lambda n, k: (lg[0][n][k], lg[1][n][k])
    recv = grad_exchange(gb_win, [pair(n, 1) for n in col_names], [pair(n, 1) for n in row_names],
                         _pack_small(gsmall, True))
    smalls = recv[-1]
    halves = [partial_sum_shards(gf_win, recv[0], chip1, "psum_w_in")]
    for i, n in enumerate(col_names + row_names):
        halves.append(partial_sum_window(*pair(n, 0), recv[1 + i], chip1, 1 if n in col_names else 0, "psum_" + n))
    others = sibling_swap(halves)

    out_g, out_d, out_m, out_v = {}, {}, {}, {}
    for n, h, hs in zip(("w_in",) + col_names + row_names, halves, others):
        out_g[n], out_d[n], out_m[n], out_v[n] = adam_pair(h, hs, wts[n], mom[n], var[n], "adam_" + n)
    small_names = [n for n, _ in _SMALL if n != "loss"]
    pack = lambda v: _pack_small({n: v[n] for n in small_names}, False)
    sg, sd, smn, svn = small_update(smalls, pack(wts), pack(mom), pack(var))
    shapes = {n: wts[n].shape for n in small_names}
    shapes.update(conv_a=(2, 3, 512), conv_c=(2, 4, 2048))
    for dst, src, conv in ((out_g, sg, True), (out_d, sd, False), (out_m, smn, False), (out_v, svn, False)):
        dst.update(_unpack_small(src, shapes, conv))
    loss = out_g.pop("loss")
    for n in ("conv_a", "conv_c"):
        width = wts[n].shape[2]
        g = lax.dynamic_slice_in_dim(out_g[n], chip * width, width, axis=2)
        two_d = lambda a: a.reshape(-1, width)
        d, mn, vn = adam(two_d(wts[n]), two_d(g), two_d(mom[n]), two_d(var[n]), "adam_" + n)
        out_g[n] = g
        out_d[n], out_m[n], out_v[n] = (a.reshape(wts[n].shape) for a in (d, mn, vn))
    return (loss, grad_x, *[out_g[n] for n in WEIGHT_NAMES], *[out_d[n] for n in WEIGHT_NAMES],
            *[out_m[n] for n in WEIGHT_NAMES], *[out_v[n] for n in WEIGHT_NAMES])
```

```python
import functools

import jax
import jax.numpy as jnp
from jax import lax
from jax.experimental import pallas as pl
from jax.experimental.pallas import tpu as pltpu

f32 = jnp.float32
bf16 = jnp.bfloat16

D = 1024
L = 64
SUB = 16
NORM_EPS = 1e-6
L2_EPS = 1e-6
MIN_F = 1e-30
HK = 128
QK_SCALE = HK ** -0.5
N_GDN = 8
GDN_BLOCK = 1024
N_HGRN = 4

OFF_A, OFF_B, OFF_CQKV, OFF_G, OFF_CZ, OFF_S = 0, 2048, 4096, 6144, 9216, 10240
NP = 10368
N_ORIG = 10256

ADAM_LR, ADAM_B1, ADAM_B2, ADAM_EPS, ADAM_WD, ADAM_STEP = 0.001, 0.9, 0.999, 1e-08, 0.01, 10

VMEM_LIMIT = 56 * 1024 * 1024


def _cp(sem):
    return pltpu.CompilerParams(dimension_semantics=sem, vmem_limit_bytes=VMEM_LIMIT)


def _sigmoid(x):
    return jax.nn.sigmoid(x)


def _silu(x):
    return x * _sigmoid(x)


def _dsilu(x):
    s = _sigmoid(x)
    return s * (1.0 + x * (1.0 - s))


def _softplus(x):
    u = jnp.exp(-jnp.abs(x))
    w = 1.0 + u
    l1p = jnp.where(w == 1.0, u, jnp.log(w) * (u / (w - 1.0)))
    return jnp.maximum(x, 0.0) + l1p


def _dot(a, b):
    return jnp.dot(a, b, preferred_element_type=f32)


def _dot_nt(a, b):
    return lax.dot_general(a, b, (((1,), (1,)), ((), ())), preferred_element_type=f32)


def _dot_tn(a, b):
    return lax.dot_general(a, b, (((0,), (0,)), ((), ())), preferred_element_type=f32)


def _bdot(a, b):
    return lax.dot_general(a, b, (((2,), (1,)), ((0,), (0,))), preferred_element_type=f32)


def _bdot_nt(a, b):
    return lax.dot_general(a, b, (((2,), (2,)), ((0,), (0,))), preferred_element_type=f32)


def _bdot_tn(a, b):
    return lax.dot_general(a, b, (((1,), (1,)), ((0,), (0,))), preferred_element_type=f32)


def _bdot_split(a, b):
    ah, bh = _b(a), _b(b)
    al, bl = _b(a - ah.astype(f32)), _b(b - bh.astype(f32))
    return _bdot(ah, bh) + (_bdot(ah, bl) + _bdot(al, bh))


def _b(x):
    return x.astype(bf16)


def _chunk_cumsum(x, rows_in_chunk):
    n = x.shape[0]
    for s in (1, 2, 4, 8, 16, 32):
        x = x + jnp.where(rows_in_chunk >= s, pltpu.roll(x, s, axis=0), 0.0)
    return x


def _chunk_rev_cumsum(x, rows_in_chunk):
    n = x.shape[0]
    for s in (1, 2, 4, 8, 16, 32):
        x = x + jnp.where(rows_in_chunk + s < L, pltpu.roll(x, n - s, axis=0), 0.0)
    return x


def _shift_down(x, s):
    return pltpu.roll(x, s, axis=0) if s else x


def _shift_up(x, s):
    return pltpu.roll(x, x.shape[0] - s, axis=0) if s else x


def inproj_fwd(x, nw, w, l):
    T = x.shape[0]
    tT, tn = min(512, T), 1152

    def body(x_ref, nw_ref, w_ref, p_ref, h_ref, hs):
        @pl.when(pl.program_id(1) == 0)
        def _():
            xv = x_ref[...]
            r = lax.rsqrt(jnp.mean(xv * xv, axis=-1, keepdims=True) + NORM_EPS)
            hv = _b(xv * r * nw_ref[...])
            hs[...] = hv
            h_ref[...] = hv
        p_ref[...] = _dot(hs[...], w_ref[...])

    return pl.pallas_call(
        body, name="inproj_fwd", grid=(T // tT, NP // tn),
        in_specs=[pl.BlockSpec((tT, D), lambda i, j: (i, 0)), pl.BlockSpec((1, D), lambda i, j: (0, 0)),
                  pl.BlockSpec((None, D, tn), lambda i, j: (l, 0, j))],
        out_specs=[pl.BlockSpec((tT, tn), lambda i, j: (i, j)), pl.BlockSpec((tT, D), lambda i, j: (i, 0))],
        out_shape=[jax.ShapeDtypeStruct((T, NP), f32), jax.ShapeDtypeStruct((T, D), bf16)],
        scratch_shapes=[pltpu.VMEM((tT, D), bf16)],
        compiler_params=_cp(("parallel", "arbitrary")),
    )(x, nw, w)


def matmul_tn(a, b, name, n=None, b_col0=0, with_bf16=False):
    T, K = a.shape
    N = b.shape[1] if n is None else n
    tT = min(2048, T)
    tn = 1152 if N % 1152 == 0 else min(N, 1024)
    nt = T // tT
    cb0 = b_col0 // tn

    def body(a_ref, b_ref, o_ref, *ob_ref):
        @pl.when(pl.program_id(1) == 0)
        def _():
            o_ref[...] = jnp.zeros_like(o_ref)
        o_ref[...] += _dot_tn(_b(a_ref[...]), _b(b_ref[...]))
        if with_bf16:
            @pl.when(pl.program_id(1) == nt - 1)
            def _():
                ob_ref[0][...] = _b(o_ref[...])

    ospec = pl.BlockSpec((K, tn), lambda j, t: (0, j))
    return pl.pallas_call(
        body, name=name, grid=(N // tn, nt),
        in_specs=[pl.BlockSpec((tT, K), lambda j, t: (t, 0)), pl.BlockSpec((tT, tn), lambda j, t: (t, cb0 + j))],
        out_specs=[ospec, ospec] if with_bf16 else ospec,
        out_shape=([jax.ShapeDtypeStruct((K, N), f32), jax.ShapeDtypeStruct((K, N), bf16)] if with_bf16
                   else jax.ShapeDtypeStruct((K, N), f32)),
        compiler_params=_cp(("parallel", "arbitrary")),
    )(a, b)


def inproj_bwd(dp, w, l, x, nw, dres):
    T = x.shape[0]
    tT, tk = min(512, T), 1152
    nk = NP // tk

    def body(dp_ref, w_ref, x_ref, nw_ref, dres_ref, dx_ref, dnw_ref, acc):
        i, k = pl.program_id(0), pl.program_id(1)

        @pl.when((i == 0) & (k == 0))
        def _():
            dnw_ref[...] = jnp.zeros_like(dnw_ref)

        @pl.when(k == 0)
        def _():
            acc[...] = jnp.zeros_like(acc)
        acc[...] += _dot_nt(dp_ref[...], w_ref[...])

        @pl.when(k == nk - 1)
        def _():
            xv = x_ref[...]
            r = lax.rsqrt(jnp.mean(xv * xv, axis=-1, keepdims=True) + NORM_EPS)
            xh = xv * r
            dy = acc[...]
            dyw = dy * nw_ref[...]
            dx_ref[...] = r * (dyw - xh * jnp.mean(dyw * xh, axis=-1, keepdims=True)) + dres_ref[...]
            dnw_ref[...] += jnp.sum(dy * xh, axis=0, keepdims=True)

    return pl.pallas_call(
        body, name="inproj_bwd", grid=(T // tT, nk),
        in_specs=[pl.BlockSpec((tT, tk), lambda i, k: (i, k)), pl.BlockSpec((None, D, tk), lambda i, k: (l, 0, k)),
                  pl.BlockSpec((tT, D), lambda i, k: (i, 0)), pl.BlockSpec((1, D), lambda i, k: (0, 0)),
                  pl.BlockSpec((tT, D), lambda i, k: (i, 0))],
        out_specs=[pl.BlockSpec((tT, D), lambda i, k: (i, 0)), pl.BlockSpec((1, D), lambda i, k: (0, 0))],
        out_shape=[jax.ShapeDtypeStruct((T, D), f32), jax.ShapeDtypeStruct((1, D), f32)],
        scratch_shapes=[pltpu.VMEM((tT, D), f32)],
        compiler_params=_cp(("arbitrary", "arbitrary")),
    )(dp, w, x, nw, dres)


def loss_head(x, fw, tgt):
    T = x.shape[0]
    tT = min(512, T)

    def body(x_ref, fw_ref, t_ref, loss_ref, dx_ref, dfw_ref):
        @pl.when(pl.program_id(0) == 0)
        def _():
            loss_ref[...] = jnp.zeros_like(loss_ref)
            dfw_ref[...] = jnp.zeros_like(dfw_ref)
        xv = x_ref[...]
        r = lax.rsqrt(jnp.mean(xv * xv, axis=-1, keepdims=True) + NORM_EPS)
        xh = xv * r
        err = xh * fw_ref[...] - t_ref[...]
        part = 0.5 * jnp.sum(jnp.mean(err * err, axis=-1, keepdims=True), axis=0, keepdims=True)
        loss_ref[...] += jnp.broadcast_to(part, loss_ref.shape)
        dy = err * (1.0 / D)
        dyw = dy * fw_ref[...]
        dx_ref[...] = r * (dyw - xh * jnp.mean(dyw * xh, axis=-1, keepdims=True))
        dfw_ref[...] += jnp.sum(dy * xh, axis=0, keepdims=True)

    return pl.pallas_call(
        body, name="loss_head", grid=(T // tT,),
        in_specs=[pl.BlockSpec((tT, D), lambda i: (i, 0)), pl.BlockSpec((1, D), lambda i: (0, 0)),
                  pl.BlockSpec((tT, D), lambda i: (i, 0))],
        out_specs=[pl.BlockSpec((1, 128), lambda i: (0, 0)), pl.BlockSpec((tT, D), lambda i: (i, 0)),
                   pl.BlockSpec((1, D), lambda i: (0, 0))],
        out_shape=[jax.ShapeDtypeStruct((1, 128), f32), jax.ShapeDtypeStruct((T, D), f32),
                   jax.ShapeDtypeStruct((1, D), f32)],
        compiler_params=_cp(("arbitrary",)),
    )(x, fw, tgt)


def _halo_specs(tT, T, width, colblk):
    nb8 = T // 8
    per = tT // 8
    prev = pl.BlockSpec((8, width), lambda i: (jnp.maximum(i * per - 1, 0), colblk))
    nxt = pl.BlockSpec((8, width), lambda i: (jnp.minimum((i + 1) * per, nb8 - 1), colblk))
    return prev, nxt


def mixa_fwd(p, cw):
    T = p.shape[0]
    tT = min(512, T)
    prev_spec, _ = _halo_specs(tT, T, 2048, OFF_A // 2048)

    def body(p_ref, pp_ref, cw_ref, y_ref):
        pv = p_ref[...]
        u = pv[:, 512:1024] * pv[:, 1024:1536]
        pp = pp_ref[...]
        up = jnp.where(pl.program_id(0) == 0, 0.0, pp[:, 512:1024] * pp[:, 1024:1536])
        ue = jnp.concatenate([up, u], axis=0)
        cv = cw_ref[0:1, :] * _shift_down(ue, 2) + cw_ref[1:2, :] * _shift_down(ue, 1) + cw_ref[2:3, :] * ue
        y_ref[...] = _b(pv[:, 0:512] * cv[8:] * _silu(pv[:, 1536:2048]))

    return pl.pallas_call(
        body, name="mixa_fwd", grid=(T // tT,),
        in_specs=[pl.BlockSpec((tT, 2048), lambda i: (i, OFF_A // 2048)), prev_spec,
                  pl.BlockSpec((3, 512), lambda i: (0, 0))],
        out_specs=pl.BlockSpec((tT, 512), lambda i: (i, 0)),
        out_shape=jax.ShapeDtypeStruct((T, 512), bf16),
        compiler_params=_cp(("parallel",)),
    )(p, p, cw)


def mixa_bwd(p, cw, dy):
    T = p.shape[0]
    tT = min(512, T)
    nt = T // tT
    prev_spec, next_spec = _halo_specs(tT, T, 2048, OFF_A // 2048)
    _, dnext_spec = _halo_specs(tT, T, 512, 0)

    def body(p_ref, pp_ref, pn_ref, cw_ref, dy_ref, dyn_ref, dp_ref, dcw_ref):
        i = pl.program_id(0)

        @pl.when(i == 0)
        def _():
            dcw_ref[...] = jnp.zeros_like(dcw_ref)
        pv, pp, pn = p_ref[...], pp_ref[...], pn_ref[...]
        pe = jnp.concatenate([pp, pv, pn], axis=0)
        rows = lax.broadcasted_iota(jnp.int32, (tT + 16, 1), 0)
        ab, ac, ax, az = pe[:, 0:512], pe[:, 512:1024], pe[:, 1024:1536], pe[:, 1536:2048]
        u = jnp.where((rows < 8) & (i == 0), 0.0, ac * ax)
        u1, u2 = _shift_down(u, 1), _shift_down(u, 2)
        w0, w1, w2 = cw_ref[0:1, :], cw_ref[1:2, :], cw_ref[2:3, :]
        cv = w0 * u2 + w1 * u1 + w2 * u
        dye = jnp.concatenate([jnp.zeros((8, 512), f32), dy_ref[...], dyn_ref[...]], axis=0)
        dye = jnp.where((rows >= tT + 8) & (i == nt - 1), 0.0, dye)
        sz = _silu(az)
        dcv = dye * ab * sz
        du = w2 * dcv + w1 * _shift_up(dcv, 1) + w0 * _shift_up(dcv, 2)
        inner = (rows >= 8) & (rows < tT + 8)
        dcv_in = jnp.where(inner, dcv, 0.0)
        dcw_ref[0:1, :] += jnp.sum(dcv_in * u2, axis=0, keepdims=True)
        dcw_ref[1:2, :] += jnp.sum(dcv_in * u1, axis=0, keepdims=True)
        dcw_ref[2:3, :] += jnp.sum(dcv_in * u, axis=0, keepdims=True)
        sl = slice(8, tT + 8)
        dp_ref[:, 0:512] = _b((dye * cv * sz)[sl])
        dp_ref[:, 512:1024] = _b((du * ax)[sl])
        dp_ref[:, 1024:1536] = _b((du * ac)[sl])
        dp_ref[:, 1536:2048] = _b((dye * ab * cv * _dsilu(az))[sl])

    return pl.pallas_call(
        body, name="mixa_bwd", grid=(nt,),
        in_specs=[pl.BlockSpec((tT, 2048), lambda i: (i, OFF_A // 2048)), prev_spec, next_spec,
                  pl.BlockSpec((3, 512), lambda i: (0, 0)),
                  pl.BlockSpec((tT, 512), lambda i: (i, 0)), dnext_spec],
        out_specs=[pl.BlockSpec((tT, 2048), lambda i: (i, 0)), pl.BlockSpec((8, 512), lambda i: (0, 0))],
        out_shape=[jax.ShapeDtypeStruct((T, 2048), bf16), jax.ShapeDtypeStruct((8, 512), f32)],
        compiler_params=_cp(("arbitrary",)),
    )(p, p, p, cw, dy, dy)


def _l2n_fwd(y):
    return y * lax.rsqrt(jnp.sum(y * y, axis=-1, keepdims=True) + L2_EPS)


def mixc_pre_fwd(p, cw, alog_l, dtb_l):
    T = p.shape[0]
    tT = min(512, T)
    prev_spec, _ = _halo_specs(tT, T, 2048, OFF_CQKV // 2048)

    def body(p_ref, pp_ref, ps_ref, cw_ref, al_ref, dt_ref, q_ref, k_ref, v_ref, sm_ref):
        pp = jnp.where(pl.program_id(0) == 0, 0.0, pp_ref[...])
        xe = jnp.concatenate([pp, p_ref[...]], axis=0)
        cv = (cw_ref[0:1, :] * _shift_down(xe, 3) + cw_ref[1:2, :] * _shift_down(xe, 2)
              + cw_ref[2:3, :] * _shift_down(xe, 1) + cw_ref[3:4, :] * xe)[8:]
        y = _silu(cv)
        for hh in range(4):
            sl = slice(hh * HK, (hh + 1) * HK)
            q_ref[:, sl] = _l2n_fwd(y[:, sl]) * QK_SCALE
            k_ref[:, sl] = _l2n_fwd(y[:, 512 + hh * HK:512 + (hh + 1) * HK])
        v_ref[...] = y[:, 1024:2048]
        ps = ps_ref[...]
        lane = lax.broadcasted_iota(jnp.int32, ps.shape, 1)
        la = -jnp.exp(al_ref[...]) * _softplus(ps + dt_ref[...])
        rin = lax.broadcasted_iota(jnp.int32, ps.shape, 0) % L
        g = _chunk_cumsum(la, rin)
        sm_ref[...] = jnp.where(lane < 8, _sigmoid(ps), jnp.where(lane < 16, g, 0.0))

    return pl.pallas_call(
        body, name="mixc_pre_fwd", grid=(T // tT,),
        in_specs=[pl.BlockSpec((tT, 2048), lambda i: (i, OFF_CQKV // 2048)), prev_spec,
                  pl.BlockSpec((tT, 128), lambda i: (i, OFF_S // 128)),
                  pl.BlockSpec((4, 2048), lambda i: (0, 0)),
                  pl.BlockSpec((1, 128), lambda i: (0, 0)), pl.BlockSpec((1, 128), lambda i: (0, 0))],
        out_specs=[pl.BlockSpec((tT, 512), lambda i: (i, 0)), pl.BlockSpec((tT, 512), lambda i: (i, 0)),
                   pl.BlockSpec((tT, 1024), lambda i: (i, 0)), pl.BlockSpec((tT, 128), lambda i: (i, 0))],
        out_shape=[jax.ShapeDtypeStruct((T, 512), f32), jax.ShapeDtypeStruct((T, 512), f32),
                   jax.ShapeDtypeStruct((T, 1024), f32), jax.ShapeDtypeStruct((T, 128), f32)],
        compiler_params=_cp(("parallel",)),
    )(p, p, p, cw, alog_l, dtb_l)


def mixc_pre_bwd(p, cw, alog_l, dtb_l, dq8, dk8, dv, dsm8):
    T = p.shape[0]
    tT = min(256, T)
    nt = T // tT
    prev_spec, next_spec = _halo_specs(tT, T, 2048, OFF_CQKV // 2048)
    _, n1024 = _halo_specs(tT, T, 1024, 0)

    def body(p_ref, pp_ref, pn_ref, ps_ref, cw_ref, al_ref, dt_ref, dq_ref, dqn_ref, dk_ref, dkn_ref,
             dv_ref, dvn_ref, dsm_ref, dp_ref, dps_ref, dcw_ref, dsml_ref):
        i = pl.program_id(0)

        @pl.when(i == 0)
        def _():
            dcw_ref[...] = jnp.zeros_like(dcw_ref)
            dsml_ref[...] = jnp.zeros_like(dsml_ref)
        rows = lax.broadcasted_iota(jnp.int32, (tT + 16, 1), 0)
        pp = jnp.where(i == 0, 0.0, pp_ref[...])
        xe = jnp.concatenate([pp, p_ref[...], pn_ref[...]], axis=0)
        xs = [_shift_down(xe, 3), _shift_down(xe, 2), _shift_down(xe, 1), xe]
        cv = cw_ref[0:1, :] * xs[0] + cw_ref[1:2, :] * xs[1] + cw_ref[2:3, :] * xs[2] + cw_ref[3:4, :] * xs[3]
        y = _silu(cv)
        last = (rows >= tT + 8) & (i == nt - 1)
        z8q = jnp.zeros((8, 1024), f32)

        def ext(cur_ref, nxt_ref):
            return jnp.where(last, 0.0, jnp.concatenate([z8q, cur_ref[...], nxt_ref[...]], axis=0))
        dq8e, dk8e, dve = ext(dq_ref, dqn_ref), ext(dk_ref, dkn_ref), ext(dv_ref, dvn_ref)
        dys = []
        for (d8, base, scale) in ((dq8e, 0, QK_SCALE), (dk8e, 512, 1.0)):
            for hh in range(4):
                dn = (d8[:, (2 * hh) * HK:(2 * hh + 1) * HK] + d8[:, (2 * hh + 1) * HK:(2 * hh + 2) * HK]) * scale
                yh = y[:, base + hh * HK:base + (hh + 1) * HK]
                r = lax.rsqrt(jnp.sum(yh * yh, axis=-1, keepdims=True) + L2_EPS)
                nh = yh * r
                dys.append(r * (dn - nh * jnp.sum(dn * nh, axis=-1, keepdims=True)))
        dyy = jnp.concatenate(dys + [dve], axis=1)
        dcv = dyy * _dsilu(cv)
        dx = (cw_ref[3:4, :] * dcv + cw_ref[2:3, :] * _shift_up(dcv, 1) + cw_ref[1:2, :] * _shift_up(dcv, 2)
              + cw_ref[0:1, :] * _shift_up(dcv, 3))
        dp_ref[...] = _b(dx[8:tT + 8])
        inner = (rows >= 8) & (rows < tT + 8)
        dcv_in = jnp.where(inner, dcv, 0.0)
        for j in range(4):
            dcw_ref[j:j + 1, :] += jnp.sum(dcv_in * xs[j], axis=0, keepdims=True)
        ps = ps_ref[...]
        lane = lax.broadcasted_iota(jnp.int32, ps.shape, 1)
        dsm = dsm_ref[:, 0:128]
        for hh in range(1, N_GDN):
            dsm = dsm + dsm_ref[:, hh * 128:(hh + 1) * 128]
        beta = _sigmoid(ps)
        xa = ps + dt_ref[...]
        nea = -jnp.exp(al_ref[...])
        dpa = dsm * nea * _sigmoid(xa)
        dps_ref[...] = _b(jnp.where(lane < 8, dsm * beta * (1.0 - beta), jnp.where(lane < 16, dpa, 0.0)))
        amask = (lane >= 8) & (lane < 16)
        dsml_ref[0:1, :] += jnp.sum(jnp.where(amask, dsm * nea * _softplus(xa), 0.0), axis=0, keepdims=True)
        dsml_ref[1:2, :] += jnp.sum(jnp.where(amask, dpa, 0.0), axis=0, keepdims=True)

    cur1024 = pl.BlockSpec((tT, 1024), lambda i: (i, 0))
    return pl.pallas_call(
        body, name="mixc_pre_bwd", grid=(nt,),
        in_specs=[pl.BlockSpec((tT, 2048), lambda i: (i, OFF_CQKV // 2048)), prev_spec, next_spec,
                  pl.BlockSpec((tT, 128), lambda i: (i, OFF_S // 128)),
                  pl.BlockSpec((4, 2048), lambda i: (0, 0)),
                  pl.BlockSpec((1, 128), lambda i: (0, 0)), pl.BlockSpec((1, 128), lambda i: (0, 0)),
                  cur1024, n1024, cur1024, n1024, cur1024, n1024, cur1024],
        out_specs=[pl.BlockSpec((tT, 2048), lambda i: (i, 0)), pl.BlockSpec((tT, 128), lambda i: (i, 0)),
                   pl.BlockSpec((8, 2048), lambda i: (0, 0)), pl.BlockSpec((8, 128), lambda i: (0, 0))],
        out_shape=[jax.ShapeDtypeStruct((T, 2048), bf16), jax.ShapeDtypeStruct((T, 128), bf16),
                   jax.ShapeDtypeStruct((8, 2048), f32), jax.ShapeDtypeStruct((8, 128), f32)],
        compiler_params=_cp(("arbitrary",)),
    )(p, p, p, p, cw, alog_l, dtb_l, dq8, dq8, dk8, dk8, dv, dv, dsm8)


def _tri_inverse(m):
    r = lax.broadcasted_iota(jnp.int32, (L, L), 0)
    c = lax.broadcasted_iota(jnp.int32, (L, L), 1)
    eye = (r == c).astype(f32)[None]
    same = lambda w: ((r // w) == (c // w))[None]
    md = jnp.where(same(8), m, 0.0)
    m2 = _bdot_split(md, md)
    m4 = _bdot_split(m2, m2)
    t = _bdot_split(_bdot_split(eye - md, eye + m2), eye + m4)
    for w in (16, 32, 64):
        mo = jnp.where(same(w) & jnp.logical_not(same(w // 2)), m, 0.0)
        t = t - _bdot_split(_bdot_split(t, mo), t)
    return t


def _col_to_row(col, eye):
    return jnp.sum(eye * col, axis=1, keepdims=True)


def _row_to_col(row, eye):
    return jnp.sum(eye * row, axis=2, keepdims=True)


def _gdn_chunk_terms(q, k, v, beta, g, t_inv=None):
    r = lax.broadcasted_iota(jnp.int32, (L, L), 0)
    c = lax.broadcasted_iota(jnp.int32, (L, L), 1)
    eye = (r == c).astype(f32)[None]
    causal, strict = (c <= r)[None], (c < r)[None]
    diff = g - _col_to_row(g, eye)
    dec = jnp.exp(jnp.where(causal, diff, 0.0))
    dc = jnp.where(causal, dec, 0.0)
    ds = jnp.where(strict, dec, 0.0)
    eg = jnp.exp(g)
    gl = g[:, L - 1:L, :]
    egl = jnp.exp(gl - g)
    kb = k * beta
    kk = _bdot_nt(_b(k), _b(kb))
    qk = _bdot_nt(_b(q), _b(kb))
    m = kk * ds
    aqk = qk * dc
    if t_inv is None:
        t_inv = _tri_inverse(m)
    tb = _b(t_inv)
    keg = k * eg
    u = _bdot(tb, _b(v))
    w = _bdot(tb, _b(keg))
    ks = kb * egl
    ksw = _bdot_tn(_b(ks), _b(w))
    return dict(eye=eye, causal=causal, strict=strict, dc=dc, ds=ds, eg=eg, gl=gl, egl=egl, kb=kb, kk=kk, qk=qk,
                m=m, aqk=aqk, t=t_inv, u=u, w=w, qi=q * eg, ks=ks, keg=keg, ksw=ksw)


def gdn_fwd(qn, kn, vv, sm):
    T = qn.shape[0]
    tB = min(GDN_BLOCK, T)
    nc = tB // L
    N = T // L

    def body(q_ref, k_ref, v_ref, sm_ref, o_ref, st_ref, ti_ref, s_scr):
        h = pl.program_id(0)

        @pl.when(pl.program_id(1) == 0)
        def _():
            s_scr[...] = jnp.zeros_like(s_scr)
        smv = sm_ref[...]
        lane = lax.broadcasted_iota(jnp.int32, smv.shape, 1)
        beta = jnp.sum(jnp.where(lane == h, smv, 0.0), axis=1, keepdims=True).reshape(nc, L, 1)
        g = jnp.sum(jnp.where(lane == 8 + h, smv, 0.0), axis=1, keepdims=True).reshape(nc, L, 1)
        q = q_ref[...].reshape(nc, L, HK)
        k = k_ref[...].reshape(nc, L, HK)
        v = v_ref[...].reshape(nc, L, HK)
        tm = _gdn_chunk_terms(q, k, v, beta, g)
        ti_ref[...] = tm["t"]
        ksu = _bdot_tn(_b(tm["ks"]), _b(tm["u"]))
        kswb = _b(tm["ksw"])
        egl_last = jnp.exp(tm["gl"])
        s = s_scr[...]
        states = [None] * nc
        for ci in range(nc):
            states[ci] = s
            s = egl_last[ci] * s + (ksu[ci] - _dot(kswb[ci], _b(s)))
        s_scr[...] = s
        sall = jnp.stack(states, axis=0)
        st_ref[...] = sall
        sb = _b(sall)
        e = tm["u"] - _bdot(_b(tm["w"]), sb)
        o = _bdot(_b(tm["qi"]), sb) + _bdot(_b(tm["aqk"]), _b(e))
        o_ref[...] = o.reshape(tB, HK)

    return pl.pallas_call(
        body, name="gdn_fwd", grid=(N_GDN, T // tB),
        in_specs=[pl.BlockSpec((tB, HK), lambda h, n: (n, h // 2)), pl.BlockSpec((tB, HK), lambda h, n: (n, h // 2)),
                  pl.BlockSpec((tB, HK), lambda h, n: (n, h)), pl.BlockSpec((tB, 128), lambda h, n: (n, 0))],
        out_specs=[pl.BlockSpec((tB, HK), lambda h, n: (n, h)),
                   pl.BlockSpec((None, nc, HK, HK), lambda h, n: (h, n, 0, 0)),
                   pl.BlockSpec((None, nc, L, L), lambda h, n: (h, n, 0, 0))],
        out_shape=[jax.ShapeDtypeStruct((T, N_GDN * HK), f32), jax.ShapeDtypeStruct((N_GDN, N, HK, HK), f32),
                   jax.ShapeDtypeStruct((N_GDN, N, L, L), f32)],
        scratch_shapes=[pltpu.VMEM((HK, HK), f32)],
        compiler_params=_cp(("parallel", "arbitrary")),
    )(qn, kn, vv, sm)


def gdn_bwd(qn, kn, vv, sm, st, ti, do):
    T = qn.shape[0]
    tB = min(GDN_BLOCK, T)
    nc = tB // L
    nb = T // tB

    def body(q_ref, k_ref, v_ref, sm_ref, st_ref, ti_ref, do_ref, dq_ref, dk_ref, dv_ref, dsm_ref, ds_scr):
        h = pl.program_id(0)

        @pl.when(pl.program_id(1) == 0)
        def _():
            ds_scr[...] = jnp.zeros_like(ds_scr)
        smv = sm_ref[...]
        lane = lax.broadcasted_iota(jnp.int32, smv.shape, 1)
        beta = jnp.sum(jnp.where(lane == h, smv, 0.0), axis=1, keepdims=True).reshape(nc, L, 1)
        g = jnp.sum(jnp.where(lane == 8 + h, smv, 0.0), axis=1, keepdims=True).reshape(nc, L, 1)
        q = q_ref[...].reshape(nc, L, HK)
        k = k_ref[...].reshape(nc, L, HK)
        v = v_ref[...].reshape(nc, L, HK)
        do = do_ref[...].reshape(nc, L, HK)
        s = st_ref[...]
        tm = _gdn_chunk_terms(q, k, v, beta, g, t_inv=ti_ref[...])
        eye, dc, ds_, eg, egl = tm["eye"], tm["dc"], tm["ds"], tm["eg"], tm["egl"]
        kb, u, w, qi, ks, tinv = tm["kb"], tm["u"], tm["w"], tm["qi"], tm["ks"], tm["t"]
        sb, dob = _b(s), _b(do)
        e = u - _bdot(_b(w), sb)
        eb = _b(e)
        egl_last = jnp.exp(tm["gl"])
        de0 = _bdot_tn(_b(tm["aqk"]), dob)
        ds0 = _bdot_tn(_b(qi), dob) - _bdot_tn(_b(w), _b(de0))
        kswb = _b(tm["ksw"])
        dsn = ds_scr[...]
        dsns = [None] * nc
        for ci in reversed(range(nc)):
            dsns[ci] = dsn
            dsn = ds0[ci] + (egl_last[ci] * dsn - _dot_tn(kswb[ci], _b(dsn)))
        ds_scr[...] = dsn
        dsp = jnp.stack(dsns, axis=0)
        dspb = _b(dsp)
        de = de0 + _bdot(_b(ks), dspb)
        deb = _b(de)
        dks = _bdot_nt(eb, dspb)
        dqi = _bdot_nt(dob, sb)
        daqk = jnp.where(tm["causal"], _bdot_nt(dob, eb), 0.0)
        dw = -_bdot_nt(deb, sb)
        tb = _b(tinv)
        dvv = _bdot_tn(tb, deb)
        dkg = _bdot_tn(tb, _b(dw))
        dm = -jnp.where(tm["strict"], _bdot_nt(_b(dvv), _b(u)) + _bdot_nt(_b(dkg), _b(w)), 0.0)
        x = _b(dm * ds_)
        y = _b(daqk * dc)
        kbb, kbf, qbf = _b(kb), _b(k), _b(q)
        dk = _bdot(x, kbb) + dkg * eg
        dkb = _bdot_tn(x, kbf) + _bdot_tn(y, qbf) + dks * egl
        dq = _bdot(y, kbb) + dqi * eg
        dk = dk + dkb * beta
        dbeta = jnp.sum(dkb * k, axis=-1, keepdims=True)
        z = dm * tm["m"] + daqk * tm["aqk"]
        dg = (jnp.sum(dqi * qi - dks * ks + dkg * tm["keg"], axis=-1, keepdims=True)
              + jnp.sum(z, axis=-1, keepdims=True) - _row_to_col(jnp.sum(z, axis=1, keepdims=True), eye))
        dgl = (egl_last * jnp.sum(jnp.sum(s * dsp, axis=2, keepdims=True), axis=1, keepdims=True)
               + jnp.sum(jnp.sum(dks * ks, axis=2, keepdims=True), axis=1, keepdims=True))
        rowi = lax.broadcasted_iota(jnp.int32, (nc, L, 1), 1)
        dg = dg + jnp.where(rowi == L - 1, dgl, 0.0)
        dg2 = dg.reshape(tB, 1)
        rin = lax.broadcasted_iota(jnp.int32, (tB, 1), 0) % L
        dla = _chunk_rev_cumsum(jnp.broadcast_to(dg2, (tB, 128)), jnp.broadcast_to(rin, (tB, 128)))
        dq_ref[...] = dq.reshape(tB, HK)
        dk_ref[...] = dk.reshape(tB, HK)
        dv_ref[...] = dvv.reshape(tB, HK)
        dsm_ref[...] = jnp.where(lane == h, dbeta.reshape(tB, 1), jnp.where(lane == 8 + h, dla, 0.0))

    rev = lambda n: nb - 1 - n
    return pl.pallas_call(
        body, name="gdn_bwd", grid=(N_GDN, nb),
        in_specs=[pl.BlockSpec((tB, HK), lambda h, n: (rev(n), h // 2)),
                  pl.BlockSpec((tB, HK), lambda h, n: (rev(n), h // 2)),
                  pl.BlockSpec((tB, HK), lambda h, n: (rev(n), h)), pl.BlockSpec((tB, 128), lambda h, n: (rev(n), 0)),
                  pl.BlockSpec((None, nc, HK, HK), lambda h, n: (h, rev(n), 0, 0)),
                  pl.BlockSpec((None, nc, L, L), lambda h, n: (h, rev(n), 0, 0)),
                  pl.BlockSpec((tB, HK), lambda h, n: (rev(n), h))],
        out_specs=[pl.BlockSpec((tB, HK), lambda h, n: (rev(n), h))] * 4,
        out_shape=[jax.ShapeDtypeStruct((T, N_GDN * HK), f32)] * 4,
        scratch_shapes=[pltpu.VMEM((HK, HK), f32)],
        compiler_params=_cp(("parallel", "arbitrary")),
    )(qn, kn, vv, sm, st, ti, do)


def _hgrn_prep(bq, bf_, bi, lb):
    tB = bq.shape[0]
    q = _silu(bq) * QK_SCALE
    sg = _sigmoid(bf_)
    f = lb + (1.0 - lb) * sg
    logf = jnp.log(jnp.maximum(f, MIN_F))
    rin = lax.broadcasted_iota(jnp.int32, (tB, HK), 0) % L
    g = _chunk_cumsum(logf, rin)
    return q, sg, f, 1.0 - f, bi, g, rin


def _hgrn_intra(q, kk, v, g, do=None):
    n = q.shape[0]
    nsub = L // SUB
    bwd = do is not None
    o_rows = [None] * nsub
    if bwd:
        dq_rows = [None] * nsub
        dkk_acc = jnp.zeros_like(kk)
        dv_acc = jnp.zeros_like(v)
    for i in range(1, nsub):
        lo, hi, w = i * SUB, (i + 1) * SUB, i * SUB
        ref = g[:, lo - 1:lo, :]
        eq = jnp.exp(g[:, lo:hi, :] - ref)
        ek = jnp.exp(ref - g[:, :w, :])
        qs = _b(q[:, lo:hi, :] * eq)
        ks = _b(kk[:, :w, :] * ek)
        p = _bdot_nt(qs, ks)
        o_rows[i] = _bdot(_b(p), _b(v[:, :w, :]))
        if bwd:
            dob = _b(do[:, lo:hi, :])
            dp = _b(_bdot_nt(dob, _b(v[:, :w, :])))
            dq_rows[i] = _bdot(dp, ks) * eq
            pad = jnp.zeros((n, L - w, HK), f32)
            dkk_acc = dkk_acc + jnp.concatenate([_bdot_tn(dp, qs) * ek, pad], axis=1)
            dv_acc = dv_acc + jnp.concatenate([_bdot_tn(_b(p), dob), pad], axis=1)
    m = n * nsub
    q4, k4, v4, g4 = (a.reshape(m, SUB, HK) for a in (q, kk, v, g))
    r = lax.broadcasted_iota(jnp.int32, (m, SUB, HK), 1)
    od = jnp.zeros((m, SUB, HK), f32)
    if bwd:
        do4 = do.reshape(m, SUB, HK)
        dqd = jnp.zeros((m, SUB, HK), f32)
        dkd = jnp.zeros((m, SUB, HK), f32)
        dvd = jnp.zeros((m, SUB, HK), f32)
    for j in range(SUB):
        gj, kj, vj = g4[:, j:j + 1, :], k4[:, j:j + 1, :], v4[:, j:j + 1, :]
        ok = r >= j
        e = jnp.where(ok, jnp.exp(jnp.where(ok, g4 - gj, 0.0)), 0.0)
        xq = q4 * e
        pj = jnp.sum(xq * kj, axis=-1, keepdims=True)
        od = od + pj * vj
        if bwd:
            dpj = jnp.sum(do4 * vj, axis=-1, keepdims=True)
            dqd = dqd + dpj * kj * e
            dkd = dkd + jnp.where(r == j, jnp.sum(dpj * xq, axis=1, keepdims=True), 0.0)
            dvd = dvd + jnp.where(r == j, jnp.sum(pj * do4, axis=1, keepdims=True), 0.0)
    od = od.reshape(n, L, HK)
    o = jnp.concatenate([od[:, :SUB, :]] + [od[:, i * SUB:(i + 1) * SUB, :] + o_rows[i] for i in range(1, nsub)], axis=1)
    if not bwd:
        return o
    dqd = dqd.reshape(n, L, HK)
    dq = jnp.concatenate([dqd[:, :SUB, :]] + [dqd[:, i * SUB:(i + 1) * SUB, :] + dq_rows[i] for i in range(1, nsub)], axis=1)
    return o, dq, dkk_acc + dkd.reshape(n, L, HK), dv_acc + dvd.reshape(n, L, HK)


def hgrn_fwd(p, lbs):
    T = p.shape[0]
    tB = min(256, T)
    nc = tB // L
    N = T // L
    cq, cf, ci_ = OFF_B // HK, (OFF_B + 512) // HK, (OFF_B + 1024) // HK

    def body(bq_ref, bf_ref, bi_ref, lb_ref, o_ref, st_ref, s_scr):
        @pl.when(pl.program_id(1) == 0)
        def _():
            s_scr[...] = jnp.zeros_like(s_scr)
        q, sg, f, kk, v, g, rin = _hgrn_prep(bq_ref[...], bf_ref[...], bi_ref[...], lb_ref[...])
        q3, k3, v3, g3 = (a.reshape(nc, L, HK) for a in (q, kk, v, g))
        o = _hgrn_intra(q3, k3, v3, g3)
        gl = g3[:, L - 1:L, :]
        qt = _b(q3 * jnp.exp(g3))
        kt = _b(k3 * jnp.exp(gl - g3))
        vb = _b(v3)
        st = s_scr[...]
        for c in range(nc):
            st_ref[c] = st
            o_ref[c * L:(c + 1) * L, :] = o[c] + _dot_nt(qt[c], _b(st))
            st = st * jnp.exp(gl[c]) + _dot_tn(vb[c], kt[c])
        s_scr[...] = st

    return pl.pallas_call(
        body, name="hgrn_fwd", grid=(N_HGRN, T // tB),
        in_specs=[pl.BlockSpec((tB, HK), lambda h, n: (n, cq + h)), pl.BlockSpec((tB, HK), lambda h, n: (n, cf + h)),
                  pl.BlockSpec((tB, HK), lambda h, n: (n, ci_ + h)), pl.BlockSpec((1, HK), lambda h, n: (0, h))],
        out_specs=[pl.BlockSpec((tB, HK), lambda h, n: (n, h)),
                   pl.BlockSpec((None, nc, HK, HK), lambda h, n: (h, n, 0, 0))],
        out_shape=[jax.ShapeDtypeStruct((T, N_HGRN * HK), f32), jax.ShapeDtypeStruct((N_HGRN, N, HK, HK), f32)],
        scratch_shapes=[pltpu.VMEM((HK, HK), f32)],
        compiler_params=_cp(("parallel", "arbitrary")),
    )(p, p, p, lbs)


def hgrn_bwd(p, lbs, st, do):
    T = p.shape[0]
    tB = min(256, T)
    nc = tB // L
    nb = T // tB
    cq, cf, ci_ = OFF_B // HK, (OFF_B + 512) // HK, (OFF_B + 1024) // HK

    def body(bq_ref, bf_ref, bi_ref, lb_ref, st_ref, do_ref, dbq_ref, dbf_ref, dbi_ref, dlb_ref, ds_scr):
        @pl.when(pl.program_id(1) == 0)
        def _():
            ds_scr[...] = jnp.zeros_like(ds_scr)
            dlb_ref[...] = jnp.zeros_like(dlb_ref)
        lb = lb_ref[...]
        bq = bq_ref[...]
        q, sg, f, kk, v, g, rin = _hgrn_prep(bq, bf_ref[...], bi_ref[...], lb)
        q3, k3, v3, g3 = (a.reshape(nc, L, HK) for a in (q, kk, v, g))
        do3 = do_ref[...].reshape(nc, L, HK)
        dob = _b(do3)
        gl = g3[:, L - 1:L, :]
        egl = jnp.exp(gl)
        eg, egr = jnp.exp(g3), jnp.exp(gl - g3)
        qt, kt = q3 * eg, k3 * egr
        s = st_ref[...]
        ds0 = _bdot_tn(dob, _b(qt))
        dsn = ds_scr[...]
        dsns = [None] * nc
        for c in reversed(range(nc)):
            dsns[c] = dsn
            dsn = ds0[c] + dsn * egl[c]
        ds_scr[...] = dsn
        dsp = jnp.stack(dsns, axis=0)
        dspb = _b(dsp)
        dqt = _bdot(dob, _b(s))
        dkt = _bdot(_b(v3), dspb)
        dv_state = _bdot_nt(_b(kt), dspb)
        dgl = egl * jnp.sum(s * dsp, axis=1, keepdims=True) + jnp.sum(dkt * kt, axis=1, keepdims=True)
        _, dq_i, dkk_i, dv_i = _hgrn_intra(q3, k3, v3, g3, do=do3)
        dq = dq_i + dqt * eg
        dkk = dkk_i + dkt * egr
        dv = dv_i + dv_state
        rowi = lax.broadcasted_iota(jnp.int32, (nc, L, HK), 1)
        dg = q3 * dq - k3 * dkk + jnp.where(rowi == L - 1, dgl, 0.0)
        dlogf = _chunk_rev_cumsum(dg.reshape(tB, HK), rin)
        dkk2 = dkk.reshape(tB, HK)
        df = jnp.where(f > MIN_F, dlogf / f, 0.0) - dkk2
        dlb_ref[...] += jnp.sum(df * (1.0 - sg), axis=0, keepdims=True)
        dbf_ref[...] = _b(df * (1.0 - lb) * sg * (1.0 - sg))
        dbq_ref[...] = _b(dq.reshape(tB, HK) * QK_SCALE * _dsilu(bq))
        dbi_ref[...] = _b(dv.reshape(tB, HK))

    rev = lambda n: nb - 1 - n
    return pl.pallas_call(
        body, name="hgrn_bwd", grid=(N_HGRN, nb),
        in_specs=[pl.BlockSpec((tB, HK), lambda h, n: (rev(n), cq + h)),
                  pl.BlockSpec((tB, HK), lambda h, n: (rev(n), cf + h)),
                  pl.BlockSpec((tB, HK), lambda h, n: (rev(n), ci_ + h)), pl.BlockSpec((1, HK), lambda h, n: (0, h)),
                  pl.BlockSpec((None, nc, HK, HK), lambda h, n: (h, rev(n), 0, 0)),
                  pl.BlockSpec((tB, HK), lambda h, n: (rev(n), h))],
        out_specs=[pl.BlockSpec((tB, HK), lambda h, n: (rev(n), h))] * 3 + [pl.BlockSpec((1, HK), lambda h, n: (0, h))],
        out_shape=[jax.ShapeDtypeStruct((T, N_HGRN * HK), bf16)] * 3 + [jax.ShapeDtypeStruct((1, N_HGRN * HK), f32)],
        scratch_shapes=[pltpu.VMEM((HK, HK), f32)],
        compiler_params=_cp(("parallel", "arbitrary")),
    )(p, p, p, lbs, st, do)


def _headnorm_fwd(o, z, w, nheads):
    outs = []
    for hh in range(nheads):
        sl = slice(hh * HK, (hh + 1) * HK)
        oh = o[:, sl]
        r = lax.rsqrt(jnp.mean(oh * oh, axis=-1, keepdims=True) + NORM_EPS)
        outs.append(oh * r * w * _silu(z[:, sl]))
    return jnp.concatenate(outs, axis=1)


def _headnorm_bwd(o, z, w, dy, nheads):
    dos, dzs = [], []
    dw = jnp.zeros((1, HK), f32)
    for hh in range(nheads):
        sl = slice(hh * HK, (hh + 1) * HK)
        oh, zh, dyh = o[:, sl], z[:, sl], dy[:, sl]
        r = lax.rsqrt(jnp.mean(oh * oh, axis=-1, keepdims=True) + NORM_EPS)
        on = oh * r
        sz = _silu(zh)
        dn = dyh * sz * w
        dos.append(r * (dn - on * jnp.mean(dn * on, axis=-1, keepdims=True)))
        dzs.append(dyh * on * w * _dsilu(zh))
        dw = dw + jnp.sum(dyh * sz * on, axis=0, keepdims=True)
    return jnp.concatenate(dos, axis=1), jnp.concatenate(dzs, axis=1), dw


def _merge_specs(tT, l):
    row = lambda w, cb=0: pl.BlockSpec((tT, w), lambda i, cb=cb: (i, cb))
    full = lambda r, c: pl.BlockSpec((r, c), lambda i: (0, 0))
    layer = lambda r, c: pl.BlockSpec((None, r, c), lambda i: (l, 0, 0))
    return row, full, layer


def merge_fwd(x, p, ya, ob, oc, hw, gw, bg, woa, wob, woc, wo, l):
    T = x.shape[0]
    tT = min(256, T)
    row, full, layer = _merge_specs(tT, l)

    def body(x_ref, bz_ref, cz_ref, g_ref, ya_ref, ob_ref, oc_ref, hw_ref, gw_ref, bg_ref,
             woa_ref, wob_ref, woc_ref, wo_ref, out_ref):
        yb = _b(_headnorm_fwd(ob_ref[...], bz_ref[...], hw_ref[...], N_HGRN))
        yc = _b(_headnorm_fwd(oc_ref[...], cz_ref[...], gw_ref[...], N_GDN))
        gates = _sigmoid(g_ref[...] + bg_ref[...])
        merged = (gates[:, 0:D] * _dot(ya_ref[...], woa_ref[...]) + gates[:, D:2 * D] * _dot(yb, wob_ref[...])
                  + gates[:, 2 * D:3 * D] * _dot(yc, woc_ref[...]))
        out_ref[...] = x_ref[...] + _dot(_b(merged), wo_ref[...])

    return pl.pallas_call(
        body, name="merge_fwd", grid=(T // tT,),
        in_specs=[row(D), row(512, (OFF_B + 1536) // 512), row(1024, OFF_CZ // 1024), row(3072, OFF_G // 3072),
                  row(512), row(512), row(1024), full(1, HK), full(1, HK), full(1, 3 * D),
                  layer(512, D), layer(512, D), layer(D, D), layer(D, D)],
        out_specs=row(D),
        out_shape=jax.ShapeDtypeStruct((T, D), f32),
        compiler_params=_cp(("parallel",)),
    )(x, p, p, p, ya, ob, oc, hw, gw, bg, woa, wob, woc, wo)


def merge_bwd(dxo, p, ya, ob, oc, hw, gw, bg, woa, wob, woc, wo, l):
    T = dxo.shape[0]
    tT = min(256, T)
    row, full, layer = _merge_specs(tT, l)

    def body(dx_ref, bz_ref, cz_ref, g_ref, ya_ref, ob_ref, oc_ref, hw_ref, gw_ref, bg_ref,
             woa_ref, wob_ref, woc_ref, wo_ref,
             dya_ref, dob_ref, doc_ref, dbz_ref, dcz_ref, dg_ref, mg_ref, dy3_ref, yb_ref, yc_ref,
             dbg_ref, dhw_ref, dgw_ref):
        @pl.when(pl.program_id(0) == 0)
        def _():
            dbg_ref[...] = jnp.zeros_like(dbg_ref)
            dhw_ref[...] = jnp.zeros_like(dhw_ref)
            dgw_ref[...] = jnp.zeros_like(dgw_ref)
        ob, oc, bz, cz = ob_ref[...], oc_ref[...], bz_ref[...], cz_ref[...]
        hw_, gw_ = hw_ref[...], gw_ref[...]
        yb = _b(_headnorm_fwd(ob, bz, hw_, N_HGRN))
        yc = _b(_headnorm_fwd(oc, cz, gw_, N_GDN))
        yb_ref[...] = yb
        yc_ref[...] = yc
        gates = _sigmoid(g_ref[...] + bg_ref[...])
        ys = (_dot(ya_ref[...], woa_ref[...]), _dot(yb, wob_ref[...]), _dot(yc, woc_ref[...]))
        dmerged = _dot_nt(_b(dx_ref[...]), wo_ref[...])
        merged = jnp.zeros_like(dmerged)
        dys = []
        for i in range(3):
            gi = gates[:, i * D:(i + 1) * D]
            merged = merged + gi * ys[i]
            dyi = _b(dmerged * gi)
            dys.append(dyi)
            dy3_ref[:, i * D:(i + 1) * D] = dyi
            dgp = dmerged * ys[i] * gi * (1.0 - gi)
            dg_ref[:, i * D:(i + 1) * D] = _b(dgp)
            dbg_ref[:, i * D:(i + 1) * D] += jnp.sum(dgp, axis=0, keepdims=True)
        mg_ref[...] = _b(merged)
        dya_ref[...] = _dot_nt(dys[0], woa_ref[...])
        dob, dbz, dhw = _headnorm_bwd(ob, bz, hw_, _dot_nt(dys[1], wob_ref[...]), N_HGRN)
        doc, dcz, dgw = _headnorm_bwd(oc, cz, gw_, _dot_nt(dys[2], woc_ref[...]), N_GDN)
        dob_ref[...] = dob
        doc_ref[...] = doc
        dbz_ref[...] = _b(dbz)
        dcz_ref[...] = _b(dcz)
        dhw_ref[...] += dhw
        dgw_ref[...] += dgw

    sd = jax.ShapeDtypeStruct
    return pl.pallas_call(
        body, name="merge_bwd", grid=(T // tT,),
        in_specs=[row(D), row(512, (OFF_B + 1536) // 512), row(1024, OFF_CZ // 1024), row(3072, OFF_G // 3072),
                  row(512), row(512), row(1024), full(1, HK), full(1, HK), full(1, 3 * D),
                  layer(512, D), layer(512, D), layer(D, D), layer(D, D)],
        out_specs=[row(512), row(512), row(1024), row(512), row(1024), row(3072), row(D), row(3 * D), row(512),
                   row(1024), full(1, 3 * D), full(1, HK), full(1, HK)],
        out_shape=[sd((T, 512), f32), sd((T, 512), f32), sd((T, 1024), f32), sd((T, 512), bf16), sd((T, 1024), bf16),
                   sd((T, 3072), bf16), sd((T, D), bf16), sd((T, 3 * D), bf16), sd((T, 512), bf16),
                   sd((T, 1024), bf16), sd((1, 3 * D), f32), sd((1, HK), f32), sd((1, HK), f32)],
        compiler_params=_cp(("arbitrary",)),
    )(dxo, p, p, p, ya, ob, oc, hw, gw, bg, woa, wob, woc, wo)


def layer_fwd(x, w):
    l = w["l"]
    p, h = inproj_fwd(x, w["norm_w"], w["w_in"], l)
    ya = mixa_fwd(p, w["conv_a"])
    qn, kn, vv, sm = mixc_pre_fwd(p, w["conv_c"], w["alog_l"], w["dtb_l"])
    oc, st_c, ti = gdn_fwd(qn, kn, vv, sm)
    ob, st_b = hgrn_fwd(p, w["lbs"])
    xo = merge_fwd(x, p, ya, ob, oc, w["hgrn_norm_w"], w["gdn_norm_w"], w["b_gate"],
                   w["w_out_a"], w["w_out_b"], w["w_out_c"], w["w_o"], l)
    saved = dict(x=x, p=p, h=h, ya=ya, qn=qn, kn=kn, vv=vv, sm=sm, oc=oc, st_c=st_c, ti=ti, ob=ob, st_b=st_b)
    return xo, saved


OUT_MATS = (("w_out_a", "cols"), ("w_out_b", "cols"), ("w_out_c", "rows"), ("w_o", "rows"))


def layer_bwd(dxo, w, s, chip):
    p, l = s["p"], w["l"]
    (dya, dob, doc, dbz, dcz, dg, merged, dy3, yb, yc, dbg, dhw, dgw) = merge_bwd(
        dxo, p, s["ya"], s["ob"], s["oc"], w["hgrn_norm_w"], w["gdn_norm_w"], w["b_gate"],
        w["w_out_a"], w["w_out_b"], w["w_out_c"], w["w_o"], l)
    full = {"w_o": matmul_tn(merged, dxo, "dw_o", with_bf16=True),
            "w_out_a": matmul_tn(s["ya"], dy3, "dw_out_a", n=D, b_col0=0, with_bf16=True),
            "w_out_b": matmul_tn(yb, dy3, "dw_out_b", n=D, b_col0=D, with_bf16=True),
            "w_out_c": matmul_tn(yc, dy3, "dw_out_c", n=D, b_col0=2 * D, with_bf16=True)}
    out_kinds = [k for _, k in OUT_MATS]
    sent_out, token = exchange_start([full[n][1] for n, _ in OUT_MATS], out_kinds, f"grads_out_start{l}")
    dbq, dbf, dbi, dlbs = hgrn_bwd(p, w["lbs"] + token[0:1, 0:1], s["st_b"], dob)
    dq8, dk8, dvv, dsm8 = gdn_bwd(s["qn"], s["kn"], s["vv"], s["sm"], s["st_c"], s["ti"], doc)
    dpc, dps, dcc, dsmall = mixc_pre_bwd(p, w["conv_c"], w["alog_l"], w["dtb_l"], dq8, dk8, dvv, dsm8)
    dpa, dca = mixa_bwd(p, w["conv_a"], dya)
    dp = jnp.concatenate([dpa, dbq, dbf, dbi, dbz, dpc, dg, dcz, dps], axis=1)
    gf_win, gb_win = win_from_padded(matmul_tn(s["h"], dp, "dw_in"))
    sent_in, token = exchange_start([gb_win], ["slot"], f"grads_in_start{l}")
    dx, dnw = inproj_bwd(dp, w["w_in"], l, s["x"], w["norm_w"] + token[0:1, 0:1], dxo)
    recv_out = exchange_wait(sent_out, out_kinds, dx, f"grads_out_wait{l}")
    recv_in = exchange_wait(sent_in, ["slot"], dx, f"grads_in_wait{l}")
    half = {"w_in": partial_sum(gf_win, "slot", recv_in[0], chip, "psum_w_in")}
    for (n, kind), r in zip(OUT_MATS, recv_out):
        half[n] = partial_sum(full[n][0], kind, r, chip, "psum_" + n)
    small = dict(norm_w=dnw, b_gate=dbg, hgrn_norm_w=dhw, gdn_norm_w=dgw, lbs=dlbs, conv_a=dca[0:3], conv_c=dcc[0:4],
                 a_log=dsmall[0:1, 8:16], dt_bias=dsmall[1:2, 8:16])
    return dx, small, half


def lbs_fwd(lb):
    def body(lb_ref, o_ref):
        l0, l1 = lb_ref[0:1, :], lb_ref[1:2, :]
        mx = jnp.maximum(l0, l1)
        e0, e1 = jnp.exp(l0 - mx), jnp.exp(l1 - mx)
        o_ref[0:1, :] = jnp.zeros_like(l0)
        o_ref[1:2, :] = e1 / (e0 + e1)
    return pl.pallas_call(body, name="lbs_fwd", out_shape=jax.ShapeDtypeStruct(lb.shape, f32))(lb)


def _adam_math(w, g, m, v):
    mn = ADAM_B1 * m + (1.0 - ADAM_B1) * g
    vn = ADAM_B2 * v + (1.0 - ADAM_B2) * (g * g)
    mh = mn / (1.0 - ADAM_B1 ** ADAM_STEP)
    vh = vn / (1.0 - ADAM_B2 ** ADAM_STEP)
    return -ADAM_LR * (mh / (jnp.sqrt(vh) + ADAM_EPS) + ADAM_WD * w), mn, vn


def adam(w, g, m, v, name):
    R, C = w.shape
    tr = 256 if R % 256 == 0 else R

    def body(w_ref, g_ref, m_ref, v_ref, d_ref, mo_ref, vo_ref):
        d, mn, vn = _adam_math(w_ref[...], g_ref[...], m_ref[...], v_ref[...])
        d_ref[...] = d
        mo_ref[...] = mn
        vo_ref[...] = vn

    spec = pl.BlockSpec((tr, C), lambda i: (i, 0))
    return pl.pallas_call(
        body, name=name, grid=(R // tr,), in_specs=[spec] * 4, out_specs=[spec] * 3,
        out_shape=[jax.ShapeDtypeStruct((R, C), f32)] * 3, compiler_params=_cp(("parallel",)),
    )(w, g, m, v)


def adam_pair(h, hs, w, m, v, name):
    _, R, C = w.shape
    cp = h[0].shape[1]
    tr = 128 if R % 128 == 0 else R
    nt = R // tr

    def body(h0_ref, h1_ref, s0_ref, s1_ref, w_ref, m_ref, v_ref, g_ref, d_ref, mo_ref, vo_ref):
        def update(h_ref, s_ref):
            g = (h_ref[...] + s_ref[...])[:, :C]
            d, mn, vn = _adam_math(w_ref[...], g, m_ref[...], v_ref[...])
            g_ref[...] = g
            d_ref[...] = d
            mo_ref[...] = mn
            vo_ref[...] = vn

        @pl.when(pl.program_id(0) == 0)
        def _():
            update(h0_ref, s0_ref)

        @pl.when(pl.program_id(0) == 1)
        def _():
            update(h1_ref, s1_ref)

    h0spec = pl.BlockSpec((tr, cp), lambda l, i: (jnp.where(l == 0, i, nt - 1), 0))
    h1spec = pl.BlockSpec((tr, cp), lambda l, i: (jnp.where(l == 1, i, 0), 0))
    spec = pl.BlockSpec((None, tr, C), lambda l, i: (l, i, 0))
    return pl.pallas_call(
        body, name=name, grid=(2, nt), in_specs=[h0spec, h1spec, h0spec, h1spec, spec, spec, spec],
        out_specs=[spec] * 4, out_shape=[jax.ShapeDtypeStruct(w.shape, f32)] * 4,
        compiler_params=_cp(("arbitrary", "arbitrary")),
    )(h[0], h[1], hs[0], hs[1], w, m, v)


_SMALL = (("norm_w", 2 * D), ("b_gate", 6 * D), ("lower_bounds", None), ("hgrn_norm_w", 2 * HK),
          ("gdn_norm_w", 2 * HK), ("a_log", 16), ("dt_bias", 16), ("final_norm_w", D), ("loss", None))
_CONV = (("conv_a", 2 * 3 * 512), ("conv_c", 2 * 4 * 2048))


def _small_rows(n):
    return 16 if n is None else -(-n // 1024) * 8


LB_ROW = sum(_small_rows(n) for _, n in _SMALL[:2])
ADAM_ROWS = sum(_small_rows(n) for _, n in _SMALL)
SMALL_ROWS = ADAM_ROWS + sum(_small_rows(n) for _, n in _CONV)


def small_update(parts, wp, mp, vp):
    def body(p_ref, w_ref, m_ref, v_ref, g_ref, d_ref, mo_ref, vo_ref):
        gs = p_ref[0]
        for i in range(1, 8):
            gs = gs + p_ref[i]
        w = w_ref[...]
        l0, l1 = w[LB_ROW:LB_ROW + 8], w[LB_ROW + 8:LB_ROW + 16]
        mx = jnp.maximum(l0, l1)
        e0, e1 = jnp.exp(l0 - mx), jnp.exp(l1 - mx)
        p0, p1 = e0 / (e0 + e1), e1 / (e0 + e1)
        dl1 = gs[LB_ROW + 8:LB_ROW + 16]
        s = p1 * dl1
        g = jnp.concatenate([gs[0:LB_ROW], -p0 * s, p1 * dl1 - p1 * s, gs[LB_ROW + 16:ADAM_ROWS]], axis=0)
        d, mn, vn = _adam_math(w, g, m_ref[...], v_ref[...])
        g_ref[0:ADAM_ROWS, :] = g
        g_ref[ADAM_ROWS:, :] = gs[ADAM_ROWS:]
        d_ref[...] = d
        mo_ref[...] = mn
        vo_ref[...] = vn
    sd = jax.ShapeDtypeStruct
    return pl.pallas_call(body, name="small_update",
                          out_shape=[sd((SMALL_ROWS, 128), f32)] + [sd((ADAM_ROWS, 128), f32)] * 3)(parts, wp, mp, vp)


def partial_sum(own, kind, recv, chip, name):
    _, r, c = recv.shape
    tr = 256 if r % 256 == 0 else r

    def body(chip_ref, o_ref, r_ref, out_ref):
        out_ref[...] = ((o_ref[...] + r_ref[0].astype(f32)) + r_ref[1].astype(f32)) + r_ref[2].astype(f32)

    own_spec = {"slot": pl.BlockSpec((None, tr, c), lambda i, chip: (chip[0], i, 0)),
                "cols": pl.BlockSpec((tr, c), lambda i, chip: (i, chip[0])),
                "rows": pl.BlockSpec((tr, c), lambda i, chip: (chip[0] * (r // tr) + i, 0))}[kind]
    return pl.pallas_call(
        body, name=name,
        grid_spec=pltpu.PrefetchScalarGridSpec(
            num_scalar_prefetch=1, grid=(r // tr,),
            in_specs=[own_spec, pl.BlockSpec((3, tr, c), lambda i, chip: (0, i, 0))],
            out_specs=pl.BlockSpec((tr, c), lambda i, chip: (i, 0))),
        out_shape=jax.ShapeDtypeStruct((r, c), f32), compiler_params=_cp(("arbitrary",)),
    )(chip, own, recv)


MESH = pl.DeviceIdType.MESH
_HBM = pl.BlockSpec(memory_space=pltpu.HBM)


def _place():
    return lax.axis_index("x"), lax.axis_index("y"), lax.axis_index("c")


def weight_gather(arrs):
    n = len(arrs)

    def body(*refs):
        x_refs, out_refs = refs[:n], refs[n:2 * n]
        send_sems, recv_sems, local_sems = refs[2 * n:]
        x, y, c = _place()
        me, sibling = (x, y, c), (x, y, 1 - c)
        chips = [(1 - x, y), (x, 1 - y), (1 - x, 1 - y)]

        def copy(a, k, block, to, own_src=False):
            px, py, pc = block
            dst = out_refs[a].at[2 * px + py, pc]
            return pltpu.make_async_remote_copy(
                src_ref=x_refs[a].at[c] if own_src else dst, dst_ref=dst,
                send_sem=send_sems.at[7 * a + k], recv_sem=recv_sems.at[7 * a + k], device_id=to, device_id_type=MESH)

        mine = [pltpu.make_async_copy(x_refs[a].at[c], out_refs[a].at[2 * x + y, c], local_sems.at[a])
                for a in range(n)]
        for cp in mine:
            cp.start()
        first = []
        for a in range(n):
            first.append(copy(a, 0, me, sibling, own_src=True))
            first += [copy(a, 1 + j, me, (*chip, c), own_src=True) for j, chip in enumerate(chips)]
        for cp in first:
            cp.start()
        passed = []
        for j, chip in enumerate(chips):
            for a in range(n):
                copy(a, 1 + j, (*chip, c), me).wait_recv()
                fwd = copy(a, 4 + j, (*chip, c), sibling)
                fwd.start()
                passed.append(fwd)
        for a in range(n):
            copy(a, 0, sibling, me).wait_recv()
            for j, chip in enumerate(chips):
                copy(a, 4 + j, (*chip, 1 - c), me).wait_recv()
        for cp in first + passed:
            cp.wait_send()
        for cp in mine:
            cp.wait()

    return pl.pallas_call(
        body, name="weight_gather", in_specs=[_HBM] * n, out_specs=[_HBM] * n,
        out_shape=[jax.ShapeDtypeStruct((N_CHIPS,) + a.shape, a.dtype) for a in arrs],
        scratch_shapes=[pltpu.SemaphoreType.DMA((7 * n,)), pltpu.SemaphoreType.DMA((7 * n,)),
                        pltpu.SemaphoreType.DMA((n,))],
    )(*arrs)


SHARD_W = 256


_SEM = pl.BlockSpec(memory_space=pltpu.SEMAPHORE)
_EFFECT = pltpu.SideEffectType.DATAFLOW_SIDE_EFFECTING


def _landing_shape(a, kind):
    if kind == "slot":
        return a.shape[1:]
    return (a.shape[0], SHARD_W) if kind == "cols" else (SHARD_W, a.shape[1])


def _shard_copies(src_refs, land_refs, kinds, send_sems, recv_sems):
    x, y, c = _place()
    copies = []
    for a, (src, land, kind) in enumerate(zip(src_refs, land_refs, kinds)):
        for j, (px, py) in enumerate(((1 - x, y), (x, 1 - y), (1 - x, 1 - y))):
            q = 2 * px + py
            lo = pl.multiple_of(q * SHARD_W, SHARD_W)
            part = {"slot": lambda: src.at[q], "cols": lambda: src.at[:, pl.ds(lo, SHARD_W)],
                    "rows": lambda: src.at[pl.ds(lo, SHARD_W), :]}[kind]()
            k = 3 * a + j
            copies.append(pltpu.make_async_remote_copy(
                src_ref=part, dst_ref=land.at[j], send_sem=send_sems.at[k], recv_sem=recv_sems.at[k],
                device_id=(px, py, c), device_id_type=MESH))
    return copies


def exchange_start(srcs, kinds, name):
    n = len(srcs)
    lands = [lax.empty((3,) + _landing_shape(a, k), a.dtype) for a, k in zip(srcs, kinds)]

    def body(*refs):
        src_refs, land_refs, send_sems, recv_sems, token = refs[:n], refs[n:2 * n], refs[2 * n], refs[2 * n + 1], refs[-1]
        for cp in _shard_copies(src_refs, land_refs, kinds, send_sems, recv_sems):
            cp.start()
        token[...] = jnp.zeros_like(token)

    both = list(srcs) + lands
    out = pl.pallas_call(
        body, name=name,
        out_shape=(pltpu.SemaphoreType.DMA((3 * n,)), pltpu.SemaphoreType.DMA((3 * n,)),
                   *[pltpu.HBM(a.shape, a.dtype) for a in both], jax.ShapeDtypeStruct((8, 128), f32)),
        in_specs=[_HBM] * (2 * n), out_specs=(_SEM, _SEM, *[_HBM] * (2 * n), pl.BlockSpec(memory_space=pltpu.VMEM)),
        input_output_aliases={i: 2 + i for i in range(2 * n)},
        compiler_params=pltpu.CompilerParams(has_side_effects=_EFFECT),
    )(*[pltpu.with_memory_space_constraint(a, pltpu.HBM) for a in both])
    return (out[0], out[1], out[2:2 + 2 * n]), out[-1]


def exchange_wait(handle, kinds, after, name):
    send_sems, recv_sems, both = handle
    n = len(kinds)

    def body(*refs):
        src_refs, land_refs, s_sems, r_sems = refs[:n], refs[n:2 * n], refs[2 * n], refs[2 * n + 1]
        for cp in _shard_copies(src_refs, land_refs, kinds, s_sems, r_sems):
            cp.wait_send()
            cp.wait_recv()

    out = pl.pallas_call(
        body, name=name, out_shape=tuple(pltpu.HBM(a.shape, a.dtype) for a in both),
        in_specs=[_HBM] * (2 * n) + [_SEM, _SEM, pl.BlockSpec(memory_space=pl.ANY)], out_specs=tuple([_HBM] * (2 * n)),
        input_output_aliases={i: i for i in range(2 * n)},
        compiler_params=pltpu.CompilerParams(has_side_effects=_EFFECT),
    )(*both, send_sems, recv_sems, after)
    return out[n:]


def final_exchange(hs, small):
    n = len(hs)
    S = small.shape[0]

    def body(*refs):
        h_refs, sm_ref, out_refs, smalls_ref = refs[:n], refs[n], refs[n + 1:2 * n + 1], refs[2 * n + 1]
        send_sems, recv_sems, local_sem = refs[2 * n + 2:]
        x, y, c = _place()
        my_slot = smalls_ref.at[4 * x + 2 * y + c]
        mine = pltpu.make_async_copy(sm_ref, my_slot, local_sem)
        mine.start()
        copies = [pltpu.make_async_remote_copy(src_ref=h_refs[a], dst_ref=out_refs[a], send_sem=send_sems.at[a],
                                               recv_sem=recv_sems.at[a], device_id=(x, y, 1 - c), device_id_type=MESH)
                  for a in range(n)]
        for mask in range(1, 8):
            fx, fy, fc = (mask >> 2) & 1, (mask >> 1) & 1, mask & 1
            peer = ((1 - x) if fx else x, (1 - y) if fy else y, (1 - c) if fc else c)
            copies.append(pltpu.make_async_remote_copy(
                src_ref=sm_ref, dst_ref=my_slot, send_sem=send_sems.at[n - 1 + mask], recv_sem=recv_sems.at[n - 1 + mask],
                device_id=peer, device_id_type=MESH))
        for cp in copies:
            cp.start()
        for cp in copies:
            cp.wait_recv()
        for cp in copies:
            cp.wait_send()
        mine.wait()

    sd = jax.ShapeDtypeStruct
    out = pl.pallas_call(
        body, name="final_exchange", in_specs=[_HBM] * (n + 1), out_specs=[_HBM] * (n + 1),
        out_shape=[sd(h.shape, h.dtype) for h in hs] + [sd((8, S, 128), f32)],
        scratch_shapes=[pltpu.SemaphoreType.DMA((n + 7,)), pltpu.SemaphoreType.DMA((n + 7,)),
                        pltpu.SemaphoreType.DMA],
    )(*hs, small)
    return out[:n], out[n]


N_CHIPS = 4
SHARD_COLS = N_ORIG // N_CHIPS


SHARD_PAD = 2688
_COL_SEGMENTS = ((0, 6144, 0), (6144, 6160, OFF_S), (6160, 7184, OFF_CZ), (7184, N_ORIG, OFF_G))


def _shard_pieces():
    pieces = []
    for lo, hi, dst in _COL_SEGMENTS:
        for p in range(N_CHIPS):
            a, b = max(lo, p * SHARD_COLS), min(hi, (p + 1) * SHARD_COLS)
            if a < b:
                pieces.append((p, a - p * SHARD_COLS, dst + a - lo, b - a))
    return pieces


def win_cast_pad(w):
    tr = 256

    def body(x_ref, o_ref):
        o_ref[:, :SHARD_COLS] = _b(x_ref[...])
        o_ref[:, SHARD_COLS:] = jnp.zeros((tr, SHARD_PAD - SHARD_COLS), bf16)

    return pl.pallas_call(
        body, name="win_cast_pad", grid=(2, D // tr),
        in_specs=[pl.BlockSpec((None, tr, SHARD_COLS), lambda l, i: (l, i, 0))],
        out_specs=pl.BlockSpec((None, tr, SHARD_PAD), lambda l, i: (l, i, 0)),
        out_shape=jax.ShapeDtypeStruct((2, D, SHARD_PAD), bf16), compiler_params=_cp(("parallel", "parallel")),
    )(w)


def win_to_padded(w4):
    tr = 256
    pieces = _shard_pieces()

    def body(a_ref, o_ref):
        o_ref[:, N_ORIG:] = jnp.zeros((tr, NP - N_ORIG), bf16)
        for p, j0, c0, n in pieces:
            o_ref[:, c0:c0 + n] = a_ref[p, :, j0:j0 + n]

    return pl.pallas_call(
        body, name="win_to_padded", grid=(2, D // tr),
        in_specs=[pl.BlockSpec((N_CHIPS, None, tr, SHARD_PAD), lambda l, i: (0, l, i, 0))],
        out_specs=pl.BlockSpec((None, tr, NP), lambda l, i: (l, i, 0)),
        out_shape=jax.ShapeDtypeStruct((2, D, NP), bf16), compiler_params=_cp(("parallel", "parallel")),
    )(w4)


def win_from_padded(dw):
    tr = 128
    pieces = _shard_pieces()

    def body(d_ref, of_ref, ob_ref):
        for p in range(N_CHIPS):
            of_ref[p, :, SHARD_COLS:] = jnp.zeros((tr, SHARD_PAD - SHARD_COLS), f32)
            ob_ref[p, :, SHARD_COLS:] = jnp.zeros((tr, SHARD_PAD - SHARD_COLS), bf16)
        for p, j0, c0, n in pieces:
            v = d_ref[:, c0:c0 + n]
            of_ref[p, :, j0:j0 + n] = v
            ob_ref[p, :, j0:j0 + n] = _b(v)

    out_spec = pl.BlockSpec((N_CHIPS, tr, SHARD_PAD), lambda i: (0, i, 0))
    return pl.pallas_call(
        body, name="win_from_padded", grid=(D // tr,),
        in_specs=[pl.BlockSpec((tr, NP), lambda i: (i, 0))], out_specs=[out_spec, out_spec],
        out_shape=[jax.ShapeDtypeStruct((N_CHIPS, D, SHARD_PAD), f32),
                   jax.ShapeDtypeStruct((N_CHIPS, D, SHARD_PAD), bf16)],
        compiler_params=_cp(("parallel",)),
    )(dw)


def _rows128(a):
    flat = a.reshape(-1)
    total = -(-flat.shape[0] // 1024) * 1024
    return jnp.pad(flat, (0, total - flat.shape[0])).reshape(total // 128, 128)


def _lb_rows(lb):
    return jnp.pad(lb.reshape(2, 4, 128), ((0, 0), (0, 4), (0, 0))).reshape(16, 128)


def _pack_small(v, with_conv):
    rows = []
    for name, n in _SMALL + (_CONV if with_conv else ()):
        if name == "lower_bounds":
            rows.append(_lb_rows(v[name]))
        elif name == "loss":
            rows.append(jnp.broadcast_to(v[name], (16, 128)) if name in v else jnp.zeros((16, 128), f32))
        else:
            rows.append(_rows128(v[name]))
    return jnp.concatenate(rows, axis=0)


def _unpack_small(p, shapes, with_conv):
    out, row = {}, 0
    for name, n in _SMALL + (_CONV if with_conv else ()):
        nrows = _small_rows(n)
        blk = p[row:row + nrows]
        if name == "lower_bounds":
            out[name] = blk.reshape(2, 8, 128)[:, :4].reshape(2, 512)
        elif name == "loss":
            out[name] = blk[0, 0]
        else:
            out[name] = blk.reshape(-1)[:n].reshape(shapes[name])
        row += nrows
    return out


def _lane_vec(a8):
    return jnp.pad(a8.reshape(1, 8), ((0, 0), (8, 112)))


WEIGHT_NAMES = ("norm_w", "w_in", "b_gate", "conv_a", "conv_c", "a_log", "dt_bias", "lower_bounds", "hgrn_norm_w",
                "gdn_norm_w", "w_out_a", "w_out_b", "w_out_c", "w_o", "final_norm_w")


def kernel(x, norm_w, w_in, b_gate, conv_a, conv_c, a_log, dt_bias, lower_bounds, hgrn_norm_w, gdn_norm_w, w_out_a, w_out_b, w_out_c, w_o, final_norm_w, loss_target, m_norm_w, m_w_in, m_b_gate, m_conv_a, m_conv_c, m_a_log, m_dt_bias, m_lower_bounds, m_hgrn_norm_w, m_gdn_norm_w, m_w_out_a, m_w_out_b, m_w_out_c, m_w_o, m_final_norm_w, v_norm_w, v_w_in, v_b_gate, v_conv_a, v_conv_c, v_a_log, v_dt_bias, v_lower_bounds, v_hgrn_norm_w, v_gdn_norm_w, v_w_out_a, v_w_out_b, v_w_out_c, v_w_o, v_final_norm_w):
    wts = dict(norm_w=norm_w, w_in=w_in, b_gate=b_gate, conv_a=conv_a, conv_c=conv_c, a_log=a_log, dt_bias=dt_bias,
               lower_bounds=lower_bounds, hgrn_norm_w=hgrn_norm_w, gdn_norm_w=gdn_norm_w, w_out_a=w_out_a,
               w_out_b=w_out_b, w_out_c=w_out_c, w_o=w_o, final_norm_w=final_norm_w)
    mom = dict(norm_w=m_norm_w, w_in=m_w_in, b_gate=m_b_gate, conv_a=m_conv_a, conv_c=m_conv_c, a_log=m_a_log,
               dt_bias=m_dt_bias, lower_bounds=m_lower_bounds, hgrn_norm_w=m_hgrn_norm_w, gdn_norm_w=m_gdn_norm_w,
               w_out_a=m_w_out_a, w_out_b=m_w_out_b, w_out_c=m_w_out_c, w_o=m_w_o, final_norm_w=m_final_norm_w)
    var = dict(norm_w=v_norm_w, w_in=v_w_in, b_gate=v_b_gate, conv_a=v_conv_a, conv_c=v_conv_c, a_log=v_a_log,
               dt_bias=v_dt_bias, lower_bounds=v_lower_bounds, hgrn_norm_w=v_hgrn_norm_w, gdn_norm_w=v_gdn_norm_w,
               w_out_a=v_w_out_a, w_out_b=v_w_out_b, w_out_c=v_w_out_c, w_o=v_w_o, final_norm_w=v_final_norm_w)
    chip = 2 * lax.axis_index("x") + lax.axis_index("y")
    chip1 = chip.reshape(1).astype(jnp.int32)

    win4, woa4, wob4, woc4, wo4, ca4, cc4 = weight_gather(
        [win_cast_pad(w_in), _b(w_out_a), _b(w_out_b), _b(w_out_c), _b(w_o), conv_a, conv_c])
    by_cols = lambda a: a.transpose(1, 2, 0, 3).reshape(a.shape[1], a.shape[2], N_CHIPS * a.shape[3])
    by_rows = lambda a: a.transpose(1, 0, 2, 3).reshape(a.shape[1], N_CHIPS * a.shape[2], a.shape[3])
    full = dict(w_in=win_to_padded(win4), w_out_a=by_cols(woa4), w_out_b=by_cols(wob4), w_out_c=by_rows(woc4),
                w_o=by_rows(wo4), conv_a=by_cols(ca4), conv_c=by_cols(cc4))
    lbs = lbs_fwd(lower_bounds)
    layers = []
    for l in range(2):
        layers.append(dict(
            l=l, norm_w=norm_w[l:l + 1], w_in=full["w_in"], b_gate=b_gate[l:l + 1], conv_a=full["conv_a"][l],
            conv_c=full["conv_c"][l], alog_l=_lane_vec(a_log[l]), dtb_l=_lane_vec(dt_bias[l]), lbs=lbs[l:l + 1],
            hgrn_norm_w=hgrn_norm_w[l:l + 1], gdn_norm_w=gdn_norm_w[l:l + 1], w_out_a=full["w_out_a"],
            w_out_b=full["w_out_b"], w_out_c=full["w_out_c"], w_o=full["w_o"]))

    xs, saved = x[0], []
    for l in range(2):
        xs, s = layer_fwd(xs, layers[l])
        saved.append(s)
    loss_row, dx, dfw = loss_head(xs, final_norm_w.reshape(1, D), loss_target[0])
    lg, half = [None, None], [None, None]
    for l in (1, 0):
        dx, lg[l], half[l] = layer_bwd(dx, layers[l], saved[l], chip1)
    grad_x = dx[None]

    stack = lambda n: jnp.stack([lg[0][n], lg[1][n]], axis=0)
    gsmall = {n: stack(n) for n in ("norm_w", "b_gate", "hgrn_norm_w", "gdn_norm_w", "a_log", "dt_bias", "conv_a",
                                    "conv_c")}
    gsmall.update(lower_bounds=stack("lbs"), final_norm_w=dfw, loss=loss_row)
    mat_names = ("w_in",) + tuple(n for n, _ in OUT_MATS)
    mine = [half[l][n] for n in mat_names for l in range(2)]
    theirs, smalls = final_exchange(mine, _pack_small(gsmall, True))

    out_g, out_d, out_m, out_v = {}, {}, {}, {}
    for i, n in enumerate(mat_names):
        out_g[n], out_d[n], out_m[n], out_v[n] = adam_pair(mine[2 * i:2 * i + 2], theirs[2 * i:2 * i + 2], wts[n],
                                                           mom[n], var[n], "adam_" + n)
    small_names = [n for n, _ in _SMALL if n != "loss"]
    pack = lambda v: _pack_small({n: v[n] for n in small_names}, False)
    sg, sd, smn, svn = small_update(smalls, pack(wts), pack(mom), pack(var))
    shapes = {n: wts[n].shape for n in small_names}
    shapes.update(conv_a=(2, 3, 512), conv_c=(2, 4, 2048))
    for dst, src, conv in ((out_g, sg, True), (out_d, sd, False), (out_m, smn, False), (out_v, svn, False)):
        dst.update(_unpack_small(src, shapes, conv))
    loss = out_g.pop("loss")
    for n in ("conv_a", "conv_c"):
        width = wts[n].shape[2]
        g = lax.dynamic_slice_in_dim(out_g[n], chip * width, width, axis=2)
        two_d = lambda a: a.reshape(-1, width)
        d, mn, vn = adam(two_d(wts[n]), two_d(g), two_d(mom[n]), two_d(var[n]), "adam_" + n)
        out_g[n] = g
        out_d[n], out_m[n], out_v[n] = (a.reshape(wts[n].shape) for a in (d, mn, vn))
    return (loss, grad_x, *[out_g[n] for n in WEIGHT_NAMES], *[out_d[n] for n in WEIGHT_NAMES],
            *[out_m[n] for n in WEIGHT_NAMES], *[out_v[n] for n in WEIGHT_NAMES])
```

```python
import functools

import jax
import jax.numpy as jnp
from jax import lax
from jax.experimental import pallas as pl
from jax.experimental.pallas import tpu as pltpu

f32 = jnp.float32
bf16 = jnp.bfloat16

D = 1024
L = 64
SUB = 16
NORM_EPS = 1e-6
L2_EPS = 1e-6
MIN_F = 1e-30
HK = 128
QK_SCALE = HK ** -0.5
N_GDN = 8
GDN_BLOCK = 1024
N_HGRN = 4

REG_A = 2304
REG_C = 2304
REG_M = 4608
REG_BH = 384
OFF_A, OFF_C, OFF_M, OFF_B = 0, 2304, 4608, 9216
M_BZ, M_G, M_CZ = 0, 512, 3584
NP = 10752
NP_TILE = 1536
N_ORIG = 10256

ADAM_LR, ADAM_B1, ADAM_B2, ADAM_EPS, ADAM_WD, ADAM_STEP = 0.001, 0.9, 0.999, 1e-08, 0.01, 10

VMEM_LIMIT = 56 * 1024 * 1024


def _cp(sem):
    return pltpu.CompilerParams(dimension_semantics=sem, vmem_limit_bytes=VMEM_LIMIT)


def _sigmoid(x):
    return jax.nn.sigmoid(x)


def _silu(x):
    return x * _sigmoid(x)


def _dsilu(x):
    s = _sigmoid(x)
    return s * (1.0 + x * (1.0 - s))


def _softplus(x):
    u = jnp.exp(-jnp.abs(x))
    w = 1.0 + u
    l1p = jnp.where(w == 1.0, u, jnp.log(w) * (u / (w - 1.0)))
    return jnp.maximum(x, 0.0) + l1p


def _dot(a, b):
    return jnp.dot(a, b, preferred_element_type=f32)


def _dot_nt(a, b):
    return lax.dot_general(a, b, (((1,), (1,)), ((), ())), preferred_element_type=f32)


def _dot_tn(a, b):
    return lax.dot_general(a, b, (((0,), (0,)), ((), ())), preferred_element_type=f32)


def _bdot(a, b):
    return lax.dot_general(a, b, (((2,), (1,)), ((0,), (0,))), preferred_element_type=f32)


def _bdot_nt(a, b):
    return lax.dot_general(a, b, (((2,), (2,)), ((0,), (0,))), preferred_element_type=f32)


def _bdot_tn(a, b):
    return lax.dot_general(a, b, (((1,), (1,)), ((0,), (0,))), preferred_element_type=f32)


def _bdot_split(a, b):
    ah, bh = _b(a), _b(b)
    al, bl = _b(a - ah.astype(f32)), _b(b - bh.astype(f32))
    return _bdot(ah, bh) + (_bdot(ah, bl) + _bdot(al, bh))


def _b(x):
    return x.astype(bf16)


def _chunk_cumsum(x, rows_in_chunk):
    n = x.shape[0]
    for s in (1, 2, 4, 8, 16, 32):
        x = x + jnp.where(rows_in_chunk >= s, pltpu.roll(x, s, axis=0), 0.0)
    return x


def _chunk_rev_cumsum(x, rows_in_chunk):
    n = x.shape[0]
    for s in (1, 2, 4, 8, 16, 32):
        x = x + jnp.where(rows_in_chunk + s < L, pltpu.roll(x, n - s, axis=0), 0.0)
    return x


def _shift_down(x, s):
    return pltpu.roll(x, s, axis=0) if s else x


def _shift_up(x, s):
    return pltpu.roll(x, x.shape[0] - s, axis=0) if s else x


def inproj_fwd(x, nw, w, l):
    T = x.shape[0]
    tT, tn = min(512, T), NP_TILE

    def body(x_ref, nw_ref, w_ref, p_ref, h_ref, hs):
        @pl.when(pl.program_id(1) == 0)
        def _():
            xv = x_ref[...]
            r = lax.rsqrt(jnp.mean(xv * xv, axis=-1, keepdims=True) + NORM_EPS)
            hv = _b(xv * r * nw_ref[...])
            hs[...] = hv
            h_ref[...] = hv
        p_ref[...] = _dot(hs[...], w_ref[...])

    return pl.pallas_call(
        body, name="inproj_fwd", grid=(T // tT, NP // tn),
        in_specs=[pl.BlockSpec((tT, D), lambda i, j: (i, 0)), pl.BlockSpec((1, D), lambda i, j: (0, 0)),
                  pl.BlockSpec((None, D, tn), lambda i, j: (l, 0, j))],
        out_specs=[pl.BlockSpec((tT, tn), lambda i, j: (i, j)), pl.BlockSpec((tT, D), lambda i, j: (i, 0))],
        out_shape=[jax.ShapeDtypeStruct((T, NP), f32), jax.ShapeDtypeStruct((T, D), bf16)],
        scratch_shapes=[pltpu.VMEM((tT, D), bf16)],
        compiler_params=_cp(("parallel", "arbitrary")),
    )(x, nw, w)


def matmul_tn(a, b, name, n=None, b_col0=0, with_bf16=False):
    T, K = a.shape
    N = b.shape[1] if n is None else n
    tT = min(2048, T)
    tn = NP_TILE if N % NP_TILE == 0 else min(N, 1024)
    nt = T // tT
    cb0 = b_col0 // tn

    def body(a_ref, b_ref, o_ref, *ob_ref):
        @pl.when(pl.program_id(1) == 0)
        def _():
            o_ref[...] = jnp.zeros_like(o_ref)
        o_ref[...] += _dot_tn(_b(a_ref[...]), _b(b_ref[...]))
        if with_bf16:
            @pl.when(pl.program_id(1) == nt - 1)
            def _():
                ob_ref[0][...] = _b(o_ref[...])

    ospec = pl.BlockSpec((K, tn), lambda j, t: (0, j))
    return pl.pallas_call(
        body, name=name, grid=(N // tn, nt),
        in_specs=[pl.BlockSpec((tT, K), lambda j, t: (t, 0)), pl.BlockSpec((tT, tn), lambda j, t: (t, cb0 + j))],
        out_specs=[ospec, ospec] if with_bf16 else ospec,
        out_shape=([jax.ShapeDtypeStruct((K, N), f32), jax.ShapeDtypeStruct((K, N), bf16)] if with_bf16
                   else jax.ShapeDtypeStruct((K, N), f32)),
        compiler_params=_cp(("parallel", "arbitrary")),
    )(a, b)


def inproj_bwd(dp, w, l, x, nw, dres):
    T = x.shape[0]
    tT, tk = min(512, T), NP_TILE
    nk = NP // tk

    def body(dp_ref, w_ref, x_ref, nw_ref, dres_ref, dx_ref, dnw_ref, acc):
        i, k = pl.program_id(0), pl.program_id(1)

        @pl.when((i == 0) & (k == 0))
        def _():
            dnw_ref[...] = jnp.zeros_like(dnw_ref)

        @pl.when(k == 0)
        def _():
            acc[...] = jnp.zeros_like(acc)
        acc[...] += _dot_nt(dp_ref[...], w_ref[...])

        @pl.when(k == nk - 1)
        def _():
            xv = x_ref[...]
            r = lax.rsqrt(jnp.mean(xv * xv, axis=-1, keepdims=True) + NORM_EPS)
            xh = xv * r
            dy = acc[...]
            dyw = dy * nw_ref[...]
            dx_ref[...] = r * (dyw - xh * jnp.mean(dyw * xh, axis=-1, keepdims=True)) + dres_ref[...]
            dnw_ref[...] += jnp.sum(dy * xh, axis=0, keepdims=True)

    return pl.pallas_call(
        body, name="inproj_bwd", grid=(T // tT, nk),
        in_specs=[pl.BlockSpec((tT, tk), lambda i, k: (i, k)), pl.BlockSpec((None, D, tk), lambda i, k: (l, 0, k)),
                  pl.BlockSpec((tT, D), lambda i, k: (i, 0)), pl.BlockSpec((1, D), lambda i, k: (0, 0)),
                  pl.BlockSpec((tT, D), lambda i, k: (i, 0))],
        out_specs=[pl.BlockSpec((tT, D), lambda i, k: (i, 0)), pl.BlockSpec((1, D), lambda i, k: (0, 0))],
        out_shape=[jax.ShapeDtypeStruct((T, D), f32), jax.ShapeDtypeStruct((1, D), f32)],
        scratch_shapes=[pltpu.VMEM((tT, D), f32)],
        compiler_params=_cp(("arbitrary", "arbitrary")),
    )(dp, w, x, nw, dres)


def loss_head(x, fw, tgt):
    T = x.shape[0]
    tT = min(512, T)

    def body(x_ref, fw_ref, t_ref, loss_ref, dx_ref, dfw_ref):
        @pl.when(pl.program_id(0) == 0)
        def _():
            loss_ref[...] = jnp.zeros_like(loss_ref)
            dfw_ref[...] = jnp.zeros_like(dfw_ref)
        xv = x_ref[...]
        r = lax.rsqrt(jnp.mean(xv * xv, axis=-1, keepdims=True) + NORM_EPS)
        xh = xv * r
        err = xh * fw_ref[...] - t_ref[...]
        part = 0.5 * jnp.sum(jnp.mean(err * err, axis=-1, keepdims=True), axis=0, keepdims=True)
        loss_ref[...] += jnp.broadcast_to(part, loss_ref.shape)
        dy = err * (1.0 / D)
        dyw = dy * fw_ref[...]
        dx_ref[...] = r * (dyw - xh * jnp.mean(dyw * xh, axis=-1, keepdims=True))
        dfw_ref[...] += jnp.sum(dy * xh, axis=0, keepdims=True)

    return pl.pallas_call(
        body, name="loss_head", grid=(T // tT,),
        in_specs=[pl.BlockSpec((tT, D), lambda i: (i, 0)), pl.BlockSpec((1, D), lambda i: (0, 0)),
                  pl.BlockSpec((tT, D), lambda i: (i, 0))],
        out_specs=[pl.BlockSpec((1, 128), lambda i: (0, 0)), pl.BlockSpec((tT, D), lambda i: (i, 0)),
                   pl.BlockSpec((1, D), lambda i: (0, 0))],
        out_shape=[jax.ShapeDtypeStruct((1, 128), f32), jax.ShapeDtypeStruct((T, D), f32),
                   jax.ShapeDtypeStruct((1, D), f32)],
        compiler_params=_cp(("arbitrary",)),
    )(x, fw, tgt)


def _halo_specs(tT, T, width, colblk):
    nb8 = T // 8
    per = tT // 8
    prev = pl.BlockSpec((8, width), lambda i: (jnp.maximum(i * per - 1, 0), colblk))
    nxt = pl.BlockSpec((8, width), lambda i: (jnp.minimum((i + 1) * per, nb8 - 1), colblk))
    return prev, nxt


def mixa_fwd(p, cw):
    T = p.shape[0]
    tT = min(512, T)
    prev_spec, _ = _halo_specs(tT, T, REG_A, OFF_A // REG_A)

    def body(p_ref, pp_ref, cw_ref, y_ref):
        pv = p_ref[...]
        u = pv[:, 512:1024] * pv[:, 1024:1536]
        pp = pp_ref[...]
        up = jnp.where(pl.program_id(0) == 0, 0.0, pp[:, 512:1024] * pp[:, 1024:1536])
        ue = jnp.concatenate([up, u], axis=0)
        cv = cw_ref[0:1, :] * _shift_down(ue, 2) + cw_ref[1:2, :] * _shift_down(ue, 1) + cw_ref[2:3, :] * ue
        y_ref[...] = _b(pv[:, 0:512] * cv[8:] * _silu(pv[:, 1536:2048]))

    return pl.pallas_call(
        body, name="mixa_fwd", grid=(T // tT,),
        in_specs=[pl.BlockSpec((tT, REG_A), lambda i: (i, OFF_A // REG_A)), prev_spec,
                  pl.BlockSpec((3, 512), lambda i: (0, 0))],
        out_specs=pl.BlockSpec((tT, 512), lambda i: (i, 0)),
        out_shape=jax.ShapeDtypeStruct((T, 512), bf16),
        compiler_params=_cp(("parallel",)),
    )(p, p, cw)


def mixa_bwd(p, cw, dy, dp):
    T = p.shape[0]
    tT = min(512, T)
    nt = T // tT
    prev_spec, next_spec = _halo_specs(tT, T, REG_A, OFF_A // REG_A)
    _, dnext_spec = _halo_specs(tT, T, 512, 0)

    def body(p_ref, pp_ref, pn_ref, cw_ref, dy_ref, dyn_ref, dp_in, dp_ref, dcw_ref):
        i = pl.program_id(0)

        @pl.when(i == 0)
        def _():
            dcw_ref[...] = jnp.zeros_like(dcw_ref)
        pv, pp, pn = p_ref[:, 0:2048], pp_ref[:, 0:2048], pn_ref[:, 0:2048]
        pe = jnp.concatenate([pp, pv, pn], axis=0)
        rows = lax.broadcasted_iota(jnp.int32, (tT + 16, 1), 0)
        ab, ac, ax, az = pe[:, 0:512], pe[:, 512:1024], pe[:, 1024:1536], pe[:, 1536:2048]
        u = jnp.where((rows < 8) & (i == 0), 0.0, ac * ax)
        u1, u2 = _shift_down(u, 1), _shift_down(u, 2)
        w0, w1, w2 = cw_ref[0:1, :], cw_ref[1:2, :], cw_ref[2:3, :]
        cv = w0 * u2 + w1 * u1 + w2 * u
        dye = jnp.concatenate([jnp.zeros((8, 512), f32), dy_ref[...], dyn_ref[...]], axis=0)
        dye = jnp.where((rows >= tT + 8) & (i == nt - 1), 0.0, dye)
        sz = _silu(az)
        dcv = dye * ab * sz
        du = w2 * dcv + w1 * _shift_up(dcv, 1) + w0 * _shift_up(dcv, 2)
        inner = (rows >= 8) & (rows < tT + 8)
        dcv_in = jnp.where(inner, dcv, 0.0)
        dcw_ref[0:1, :] += jnp.sum(dcv_in * u2, axis=0, keepdims=True)
        dcw_ref[1:2, :] += jnp.sum(dcv_in * u1, axis=0, keepdims=True)
        dcw_ref[2:3, :] += jnp.sum(dcv_in * u, axis=0, keepdims=True)
        sl = slice(8, tT + 8)
        dp_ref[:, 0:512] = _b((dye * cv * sz)[sl])
        dp_ref[:, 512:1024] = _b((du * ax)[sl])
        dp_ref[:, 1024:1536] = _b((du * ac)[sl])
        dp_ref[:, 1536:2048] = _b((dye * ab * cv * _dsilu(az))[sl])
        dp_ref[:, 2048:] = jnp.zeros((tT, REG_A - 2048), bf16)

    return pl.pallas_call(
        body, name="mixa_bwd", grid=(nt,),
        in_specs=[pl.BlockSpec((tT, REG_A), lambda i: (i, OFF_A // REG_A)), prev_spec, next_spec,
                  pl.BlockSpec((3, 512), lambda i: (0, 0)),
                  pl.BlockSpec((tT, 512), lambda i: (i, 0)), dnext_spec, pl.BlockSpec(memory_space=pl.ANY)],
        out_specs=[pl.BlockSpec((tT, REG_A), lambda i: (i, OFF_A // REG_A)), pl.BlockSpec((8, 512), lambda i: (0, 0))],
        out_shape=[jax.ShapeDtypeStruct((T, NP), bf16), jax.ShapeDtypeStruct((8, 512), f32)],
        input_output_aliases={6: 0},
        compiler_params=_cp(("arbitrary",)),
    )(p, p, p, cw, dy, dy, dp)


def _l2n_fwd(y):
    return y * lax.rsqrt(jnp.sum(y * y, axis=-1, keepdims=True) + L2_EPS)


def mixc_pre_fwd(p, cw, alog_l, dtb_l):
    T = p.shape[0]
    tT = min(512, T)
    prev_spec, _ = _halo_specs(tT, T, REG_C, OFF_C // REG_C)

    def body(p_ref, pp_ref, cw_ref, al_ref, dt_ref, q_ref, k_ref, v_ref, sm_ref):
        pp = jnp.where(pl.program_id(0) == 0, 0.0, pp_ref[:, 0:2048])
        xe = jnp.concatenate([pp, p_ref[:, 0:2048]], axis=0)
        cv = (cw_ref[0:1, :] * _shift_down(xe, 3) + cw_ref[1:2, :] * _shift_down(xe, 2)
              + cw_ref[2:3, :] * _shift_down(xe, 1) + cw_ref[3:4, :] * xe)[8:]
        y = _silu(cv)
        for hh in range(4):
            sl = slice(hh * HK, (hh + 1) * HK)
            q_ref[:, sl] = _l2n_fwd(y[:, sl]) * QK_SCALE
            k_ref[:, sl] = _l2n_fwd(y[:, 512 + hh * HK:512 + (hh + 1) * HK])
        v_ref[...] = y[:, 1024:2048]
        ps = p_ref[:, 2048:2176]
        lane = lax.broadcasted_iota(jnp.int32, ps.shape, 1)
        la = -jnp.exp(al_ref[...]) * _softplus(ps + dt_ref[...])
        rin = lax.broadcasted_iota(jnp.int32, ps.shape, 0) % L
        g = _chunk_cumsum(la, rin)
        sm_ref[...] = jnp.where(lane < 8, _sigmoid(ps), jnp.where(lane < 16, g, 0.0))

    return pl.pallas_call(
        body, name="mixc_pre_fwd", grid=(T // tT,),
        in_specs=[pl.BlockSpec((tT, REG_C), lambda i: (i, OFF_C // REG_C)), prev_spec,
                  pl.BlockSpec((4, 2048), lambda i: (0, 0)),
                  pl.BlockSpec((1, 128), lambda i: (0, 0)), pl.BlockSpec((1, 128), lambda i: (0, 0))],
        out_specs=[pl.BlockSpec((tT, 512), lambda i: (i, 0)), pl.BlockSpec((tT, 512), lambda i: (i, 0)),
                   pl.BlockSpec((tT, 1024), lambda i: (i, 0)), pl.BlockSpec((tT, 128), lambda i: (i, 0))],
        out_shape=[jax.ShapeDtypeStruct((T, 512), f32), jax.ShapeDtypeStruct((T, 512), f32),
                   jax.ShapeDtypeStruct((T, 1024), f32), jax.ShapeDtypeStruct((T, 128), f32)],
        compiler_params=_cp(("parallel",)),
    )(p, p, cw, alog_l, dtb_l)


def mixc_pre_bwd(p, cw, alog_l, dtb_l, dq8, dk8, dv, dsm8, dp):
    T = p.shape[0]
    tT = min(256, T)
    nt = T // tT
    prev_spec, next_spec = _halo_specs(tT, T, REG_C, OFF_C // REG_C)
    _, n1024 = _halo_specs(tT, T, 1024, 0)

    def body(p_ref, pp_ref, pn_ref, cw_ref, al_ref, dt_ref, dq_ref, dqn_ref, dk_ref, dkn_ref,
             dv_ref, dvn_ref, dsm_ref, dp_in, dp_ref, dcw_ref, dsml_ref):
        i = pl.program_id(0)

        @pl.when(i == 0)
        def _():
            dcw_ref[...] = jnp.zeros_like(dcw_ref)
            dsml_ref[...] = jnp.zeros_like(dsml_ref)
        rows = lax.broadcasted_iota(jnp.int32, (tT + 16, 1), 0)
        pp = jnp.where(i == 0, 0.0, pp_ref[:, 0:2048])
        xe = jnp.concatenate([pp, p_ref[:, 0:2048], pn_ref[:, 0:2048]], axis=0)
        xs = [_shift_down(xe, 3), _shift_down(xe, 2), _shift_down(xe, 1), xe]
        cv = cw_ref[0:1, :] * xs[0] + cw_ref[1:2, :] * xs[1] + cw_ref[2:3, :] * xs[2] + cw_ref[3:4, :] * xs[3]
        y = _silu(cv)
        last = (rows >= tT + 8) & (i == nt - 1)
        z8q = jnp.zeros((8, 1024), f32)

        def ext(cur_ref, nxt_ref):
            return jnp.where(last, 0.0, jnp.concatenate([z8q, cur_ref[...], nxt_ref[...]], axis=0))
        dq8e, dk8e, dve = ext(dq_ref, dqn_ref), ext(dk_ref, dkn_ref), ext(dv_ref, dvn_ref)
        dys = []
        for (d8, base, scale) in ((dq8e, 0, QK_SCALE), (dk8e, 512, 1.0)):
            for hh in range(4):
                dn = (d8[:, (2 * hh) * HK:(2 * hh + 1) * HK] + d8[:, (2 * hh + 1) * HK:(2 * hh + 2) * HK]) * scale
                yh = y[:, base + hh * HK:base + (hh + 1) * HK]
                r = lax.rsqrt(jnp.sum(yh * yh, axis=-1, keepdims=True) + L2_EPS)
                nh = yh * r
                dys.append(r * (dn - nh * jnp.sum(dn * nh, axis=-1, keepdims=True)))
        dyy = jnp.concatenate(dys + [dve], axis=1)
        dcv = dyy * _dsilu(cv)
        dx = (cw_ref[3:4, :] * dcv + cw_ref[2:3, :] * _shift_up(dcv, 1) + cw_ref[1:2, :] * _shift_up(dcv, 2)
              + cw_ref[0:1, :] * _shift_up(dcv, 3))
        dp_ref[:, 0:2048] = _b(dx[8:tT + 8])
        dp_ref[:, 2176:] = jnp.zeros((tT, REG_C - 2176), bf16)
        inner = (rows >= 8) & (rows < tT + 8)
        dcv_in = jnp.where(inner, dcv, 0.0)
        for j in range(4):
            dcw_ref[j:j + 1, :] += jnp.sum(dcv_in * xs[j], axis=0, keepdims=True)
        ps = p_ref[:, 2048:2176]
        lane = lax.broadcasted_iota(jnp.int32, ps.shape, 1)
        dsm = dsm_ref[:, 0:128]
        for hh in range(1, N_GDN):
            dsm = dsm + dsm_ref[:, hh * 128:(hh + 1) * 128]
        beta = _sigmoid(ps)
        xa = ps + dt_ref[...]
        nea = -jnp.exp(al_ref[...])
        dpa = dsm * nea * _sigmoid(xa)
        dp_ref[:, 2048:2176] = _b(jnp.where(lane < 8, dsm * beta * (1.0 - beta), jnp.where(lane < 16, dpa, 0.0)))
        amask = (lane >= 8) & (lane < 16)
        dsml_ref[0:1, :] += jnp.sum(jnp.where(amask, dsm * nea * _softplus(xa), 0.0), axis=0, keepdims=True)
        dsml_ref[1:2, :] += jnp.sum(jnp.where(amask, dpa, 0.0), axis=0, keepdims=True)

    cur1024 = pl.BlockSpec((tT, 1024), lambda i: (i, 0))
    return pl.pallas_call(
        body, name="mixc_pre_bwd", grid=(nt,),
        in_specs=[pl.BlockSpec((tT, REG_C), lambda i: (i, OFF_C // REG_C)), prev_spec, next_spec,
                  pl.BlockSpec((4, 2048), lambda i: (0, 0)),
                  pl.BlockSpec((1, 128), lambda i: (0, 0)), pl.BlockSpec((1, 128), lambda i: (0, 0)),
                  cur1024, n1024, cur1024, n1024, cur1024, n1024, cur1024, pl.BlockSpec(memory_space=pl.ANY)],
        out_specs=[pl.BlockSpec((tT, REG_C), lambda i: (i, OFF_C // REG_C)),
                   pl.BlockSpec((8, 2048), lambda i: (0, 0)), pl.BlockSpec((8, 128), lambda i: (0, 0))],
        out_shape=[jax.ShapeDtypeStruct((T, NP), bf16),
                   jax.ShapeDtypeStruct((8, 2048), f32), jax.ShapeDtypeStruct((8, 128), f32)],
        input_output_aliases={13: 0},
        compiler_params=_cp(("arbitrary",)),
    )(p, p, p, cw, alog_l, dtb_l, dq8, dq8, dk8, dk8, dv, dv, dsm8, dp)


def _tri_inverse(m):
    r = lax.broadcasted_iota(jnp.int32, (L, L), 0)
    c = lax.broadcasted_iota(jnp.int32, (L, L), 1)
    eye = (r == c).astype(f32)[None]
    same = lambda w: ((r // w) == (c // w))[None]
    md = jnp.where(same(8), m, 0.0)
    m2 = _bdot_split(md, md)
    m4 = _bdot_split(m2, m2)
    t = _bdot_split(_bdot_split(eye - md, eye + m2), eye + m4)
    for w in (16, 32, 64):
        mo = jnp.where(same(w) & jnp.logical_not(same(w // 2)), m, 0.0)
        t = t - _bdot_split(_bdot_split(t, mo), t)
    return t


def _col_to_row(col, eye):
    return jnp.sum(eye * col, axis=1, keepdims=True)


def _row_to_col(row, eye):
    return jnp.sum(eye * row, axis=2, keepdims=True)


def _gdn_chunk_terms(q, k, v, beta, g, t_inv=None):
    r = lax.broadcasted_iota(jnp.int32, (L, L), 0)
    c = lax.broadcasted_iota(jnp.int32, (L, L), 1)
    eye = (r == c).astype(f32)[None]
    causal, strict = (c <= r)[None], (c < r)[None]
    diff = g - _col_to_row(g, eye)
    dec = jnp.exp(jnp.where(causal, diff, 0.0))
    dc = jnp.where(causal, dec, 0.0)
    ds = jnp.where(strict, dec, 0.0)
    eg = jnp.exp(g)
    gl = g[:, L - 1:L, :]
    egl = jnp.exp(gl - g)
    kb = k * beta
    kk = _bdot_nt(_b(k), _b(kb))
    qk = _bdot_nt(_b(q), _b(kb))
    m = kk * ds
    aqk = qk * dc
    if t_inv is None:
        t_inv = _tri_inverse(m)
    tb = _b(t_inv)
    keg = k * eg
    u = _bdot(tb, _b(v))
    w = _bdot(tb, _b(keg))
    ks = kb * egl
    ksw = _bdot_tn(_b(ks), _b(w))
    return dict(eye=eye, causal=causal, strict=strict, dc=dc, ds=ds, eg=eg, gl=gl, egl=egl, kb=kb, kk=kk, qk=qk,
                m=m, aqk=aqk, t=t_inv, u=u, w=w, qi=q * eg, ks=ks, keg=keg, ksw=ksw)


def gdn_fwd(qn, kn, vv, sm):
    T = qn.shape[0]
    tB = min(GDN_BLOCK, T)
    nc = tB // L
    N = T // L

    def body(q_ref, k_ref, v_ref, sm_ref, o_ref, st_ref, ti_ref, s_scr):
        h = pl.program_id(0)

        @pl.when(pl.program_id(1) == 0)
        def _():
            s_scr[...] = jnp.zeros_like(s_scr)
        smv = sm_ref[...]
        lane = lax.broadcasted_iota(jnp.int32, smv.shape, 1)
        beta = jnp.sum(jnp.where(lane == h, smv, 0.0), axis=1, keepdims=True).reshape(nc, L, 1)
        g = jnp.sum(jnp.where(lane == 8 + h, smv, 0.0), axis=1, keepdims=True).reshape(nc, L, 1)
        q = q_ref[...].reshape(nc, L, HK)
        k = k_ref[...].reshape(nc, L, HK)
        v = v_ref[...].reshape(nc, L, HK)
        tm = _gdn_chunk_terms(q, k, v, beta, g)
        ti_ref[...] = tm["t"]
        ksu = _bdot_tn(_b(tm["ks"]), _b(tm["u"]))
        kswb = _b(tm["ksw"])
        egl_last = jnp.exp(tm["gl"])
        s = s_scr[...]
        states = [None] * nc
        for ci in range(nc):
            states[ci] = s
            s = egl_last[ci] * s + (ksu[ci] - _dot(kswb[ci], _b(s)))
        s_scr[...] = s
        sall = jnp.stack(states, axis=0)
        st_ref[...] = sall
        sb = _b(sall)
        e = tm["u"] - _bdot(_b(tm["w"]), sb)
        o = _bdot(_b(tm["qi"]), sb) + _bdot(_b(tm["aqk"]), _b(e))
        o_ref[...] = o.reshape(tB, HK)

    return pl.pallas_call(
        body, name="gdn_fwd", grid=(N_GDN, T // tB),
        in_specs=[pl.BlockSpec((tB, HK), lambda h, n: (n, h // 2)), pl.BlockSpec((tB, HK), lambda h, n: (n, h // 2)),
                  pl.BlockSpec((tB, HK), lambda h, n: (n, h)), pl.BlockSpec((tB, 128), lambda h, n: (n, 0))],
        out_specs=[pl.BlockSpec((tB, HK), lambda h, n: (n, h)),
                   pl.BlockSpec((None, nc, HK, HK), lambda h, n: (h, n, 0, 0)),
                   pl.BlockSpec((None, nc, L, L), lambda h, n: (h, n, 0, 0))],
        out_shape=[jax.ShapeDtypeStruct((T, N_GDN * HK), f32), jax.ShapeDtypeStruct((N_GDN, N, HK, HK), f32),
                   jax.ShapeDtypeStruct((N_GDN, N, L, L), f32)],
        scratch_shapes=[pltpu.VMEM((HK, HK), f32)],
        compiler_params=_cp(("parallel", "arbitrary")),
    )(qn, kn, vv, sm)


def gdn_bwd(qn, kn, vv, sm, st, ti, do):
    T = qn.shape[0]
    tB = min(GDN_BLOCK, T)
    nc = tB // L
    nb = T // tB

    def body(q_ref, k_ref, v_ref, sm_ref, st_ref, ti_ref, do_ref, dq_ref, dk_ref, dv_ref, dsm_ref, ds_scr):
        h = pl.program_id(0)

        @pl.when(pl.program_id(1) == 0)
        def _():
            ds_scr[...] = jnp.zeros_like(ds_scr)
        smv = sm_ref[...]
        lane = lax.broadcasted_iota(jnp.int32, smv.shape, 1)
        beta = jnp.sum(jnp.where(lane == h, smv, 0.0), axis=1, keepdims=True).reshape(nc, L, 1)
        g = jnp.sum(jnp.where(lane == 8 + h, smv, 0.0), axis=1, keepdims=True).reshape(nc, L, 1)
        q = q_ref[...].reshape(nc, L, HK)
        k = k_ref[...].reshape(nc, L, HK)
        v = v_ref[...].reshape(nc, L, HK)
        do = do_ref[...].reshape(nc, L, HK)
        s = st_ref[...]
        tm = _gdn_chunk_terms(q, k, v, beta, g, t_inv=ti_ref[...])
        eye, dc, ds_, eg, egl = tm["eye"], tm["dc"], tm["ds"], tm["eg"], tm["egl"]
        kb, u, w, qi, ks, tinv = tm["kb"], tm["u"], tm["w"], tm["qi"], tm["ks"], tm["t"]
        sb, dob = _b(s), _b(do)
        e = u - _bdot(_b(w), sb)
        eb = _b(e)
        egl_last = jnp.exp(tm["gl"])
        de0 = _bdot_tn(_b(tm["aqk"]), dob)
        ds0 = _bdot_tn(_b(qi), dob) - _bdot_tn(_b(w), _b(de0))
        kswb = _b(tm["ksw"])
        dsn = ds_scr[...]
        dsns = [None] * nc
        for ci in reversed(range(nc)):
            dsns[ci] = dsn
            dsn = ds0[ci] + (egl_last[ci] * dsn - _dot_tn(kswb[ci], _b(dsn)))
        ds_scr[...] = dsn
        dsp = jnp.stack(dsns, axis=0)
        dspb = _b(dsp)
        de = de0 + _bdot(_b(ks), dspb)
        deb = _b(de)
        dks = _bdot_nt(eb, dspb)
        dqi = _bdot_nt(dob, sb)
        daqk = jnp.where(tm["causal"], _bdot_nt(dob, eb), 0.0)
        dw = -_bdot_nt(deb, sb)
        tb = _b(tinv)
        dvv = _bdot_tn(tb, deb)
        dkg = _bdot_tn(tb, _b(dw))
        dm = -jnp.where(tm["strict"], _bdot_nt(_b(dvv), _b(u)) + _bdot_nt(_b(dkg), _b(w)), 0.0)
        x = _b(dm * ds_)
        y = _b(daqk * dc)
        kbb, kbf, qbf = _b(kb), _b(k), _b(q)
        dk = _bdot(x, kbb) + dkg * eg
        dkb = _bdot_tn(x, kbf) + _bdot_tn(y, qbf) + dks * egl
        dq = _bdot(y, kbb) + dqi * eg
        dk = dk + dkb * beta
        dbeta = jnp.sum(dkb * k, axis=-1, keepdims=True)
        z = dm * tm["m"] + daqk * tm["aqk"]
        dg = (jnp.sum(dqi * qi - dks * ks + dkg * tm["keg"], axis=-1, keepdims=True)
              + jnp.sum(z, axis=-1, keepdims=True) - _row_to_col(jnp.sum(z, axis=1, keepdims=True), eye))
        dgl = (egl_last * jnp.sum(jnp.sum(s * dsp, axis=2, keepdims=True), axis=1, keepdims=True)
               + jnp.sum(jnp.sum(dks * ks, axis=2, keepdims=True), axis=1, keepdims=True))
        rowi = lax.broadcasted_iota(jnp.int32, (nc, L, 1), 1)
        dg = dg + jnp.where(rowi == L - 1, dgl, 0.0)
        dg2 = dg.reshape(tB, 1)
        rin = lax.broadcasted_iota(jnp.int32, (tB, 1), 0) % L
        dla = _chunk_rev_cumsum(jnp.broadcast_to(dg2, (tB, 128)), jnp.broadcast_to(rin, (tB, 128)))
        dq_ref[...] = dq.reshape(tB, HK)
        dk_ref[...] = dk.reshape(tB, HK)
        dv_ref[...] = dvv.reshape(tB, HK)
        dsm_ref[...] = jnp.where(lane == h, dbeta.reshape(tB, 1), jnp.where(lane == 8 + h, dla, 0.0))

    rev = lambda n: nb - 1 - n
    return pl.pallas_call(
        body, name="gdn_bwd", grid=(N_GDN, nb),
        in_specs=[pl.BlockSpec((tB, HK), lambda h, n: (rev(n), h // 2)),
                  pl.BlockSpec((tB, HK), lambda h, n: (rev(n), h // 2)),
                  pl.BlockSpec((tB, HK), lambda h, n: (rev(n), h)), pl.BlockSpec((tB, 128), lambda h, n: (rev(n), 0)),
                  pl.BlockSpec((None, nc, HK, HK), lambda h, n: (h, rev(n), 0, 0)),
                  pl.BlockSpec((None, nc, L, L), lambda h, n: (h, rev(n), 0, 0)),
                  pl.BlockSpec((tB, HK), lambda h, n: (rev(n), h))],
        out_specs=[pl.BlockSpec((tB, HK), lambda h, n: (rev(n), h))] * 4,
        out_shape=[jax.ShapeDtypeStruct((T, N_GDN * HK), f32)] * 4,
        scratch_shapes=[pltpu.VMEM((HK, HK), f32)],
        compiler_params=_cp(("parallel", "arbitrary")),
    )(qn, kn, vv, sm, st, ti, do)


def _hgrn_prep(bq, bf_, bi, lb):
    tB = bq.shape[0]
    q = _silu(bq) * QK_SCALE
    sg = _sigmoid(bf_)
    f = lb + (1.0 - lb) * sg
    logf = jnp.log(jnp.maximum(f, MIN_F))
    rin = lax.broadcasted_iota(jnp.int32, (tB, HK), 0) % L
    g = _chunk_cumsum(logf, rin)
    return q, sg, f, 1.0 - f, bi, g, rin


def _hgrn_intra(q, kk, v, g, do=None):
    n = q.shape[0]
    nsub = L // SUB
    bwd = do is not None
    o_rows = [None] * nsub
    if bwd:
        dq_rows = [None] * nsub
        dkk_acc = jnp.zeros_like(kk)
        dv_acc = jnp.zeros_like(v)
    for i in range(1, nsub):
        lo, hi, w = i * SUB, (i + 1) * SUB, i * SUB
        ref = g[:, lo - 1:lo, :]
        eq = jnp.exp(g[:, lo:hi, :] - ref)
        ek = jnp.exp(ref - g[:, :w, :])
        qs = _b(q[:, lo:hi, :] * eq)
        ks = _b(kk[:, :w, :] * ek)
        p = _bdot_nt(qs, ks)
        o_rows[i] = _bdot(_b(p), _b(v[:, :w, :]))
        if bwd:
            dob = _b(do[:, lo:hi, :])
            dp = _b(_bdot_nt(dob, _b(v[:, :w, :])))
            dq_rows[i] = _bdot(dp, ks) * eq
            pad = jnp.zeros((n, L - w, HK), f32)
            dkk_acc = dkk_acc + jnp.concatenate([_bdot_tn(dp, qs) * ek, pad], axis=1)
            dv_acc = dv_acc + jnp.concatenate([_bdot_tn(_b(p), dob), pad], axis=1)
    m = n * nsub
    q4, k4, v4, g4 = (a.reshape(m, SUB, HK) for a in (q, kk, v, g))
    r = lax.broadcasted_iota(jnp.int32, (m, SUB, HK), 1)
    od = jnp.zeros((m, SUB, HK), f32)
    if bwd:
        do4 = do.reshape(m, SUB, HK)
        dqd = jnp.zeros((m, SUB, HK), f32)
        dkd = jnp.zeros((m, SUB, HK), f32)
        dvd = jnp.zeros((m, SUB, HK), f32)
    for j in range(SUB):
        gj, kj, vj = g4[:, j:j + 1, :], k4[:, j:j + 1, :], v4[:, j:j + 1, :]
        ok = r >= j
        e = jnp.where(ok, jnp.exp(jnp.where(ok, g4 - gj, 0.0)), 0.0)
        xq = q4 * e
        pj = jnp.sum(xq * kj, axis=-1, keepdims=True)
        od = od + pj * vj
        if bwd:
            dpj = jnp.sum(do4 * vj, axis=-1, keepdims=True)
            dqd = dqd + dpj * kj * e
            dkd = dkd + jnp.where(r == j, jnp.sum(dpj * xq, axis=1, keepdims=True), 0.0)
            dvd = dvd + jnp.where(r == j, jnp.sum(pj * do4, axis=1, keepdims=True), 0.0)
    od = od.reshape(n, L, HK)
    o = jnp.concatenate([od[:, :SUB, :]] + [od[:, i * SUB:(i + 1) * SUB, :] + o_rows[i] for i in range(1, nsub)], axis=1)
    if not bwd:
        return o
    dqd = dqd.reshape(n, L, HK)
    dq = jnp.concatenate([dqd[:, :SUB, :]] + [dqd[:, i * SUB:(i + 1) * SUB, :] + dq_rows[i] for i in range(1, nsub)], axis=1)
    return o, dq, dkk_acc + dkd.reshape(n, L, HK), dv_acc + dvd.reshape(n, L, HK)


def hgrn_fwd(p, lbs):
    T = p.shape[0]
    tB = min(256, T)
    nc = tB // L
    N = T // L

    def body(b_ref, lb_ref, o_ref, st_ref, s_scr):
        @pl.when(pl.program_id(1) == 0)
        def _():
            s_scr[...] = jnp.zeros_like(s_scr)
        q, sg, f, kk, v, g, rin = _hgrn_prep(b_ref[:, 0:HK], b_ref[:, HK:2 * HK], b_ref[:, 2 * HK:3 * HK], lb_ref[...])
        q3, k3, v3, g3 = (a.reshape(nc, L, HK) for a in (q, kk, v, g))
        o = _hgrn_intra(q3, k3, v3, g3)
        gl = g3[:, L - 1:L, :]
        qt = _b(q3 * jnp.exp(g3))
        kt = _b(k3 * jnp.exp(gl - g3))
        vb = _b(v3)
        st = s_scr[...]
        for c in range(nc):
            st_ref[c] = st
            o_ref[c * L:(c + 1) * L, :] = o[c] + _dot_nt(qt[c], _b(st))
            st = st * jnp.exp(gl[c]) + _dot_tn(vb[c], kt[c])
        s_scr[...] = st

    return pl.pallas_call(
        body, name="hgrn_fwd", grid=(N_HGRN, T // tB),
        in_specs=[pl.BlockSpec((tB, REG_BH), lambda h, n: (n, OFF_B // REG_BH + h)),
                  pl.BlockSpec((1, HK), lambda h, n: (0, h))],
        out_specs=[pl.BlockSpec((tB, HK), lambda h, n: (n, h)),
                   pl.BlockSpec((None, nc, HK, HK), lambda h, n: (h, n, 0, 0))],
        out_shape=[jax.ShapeDtypeStruct((T, N_HGRN * HK), f32), jax.ShapeDtypeStruct((N_HGRN, N, HK, HK), f32)],
        scratch_shapes=[pltpu.VMEM((HK, HK), f32)],
        compiler_params=_cp(("parallel", "arbitrary")),
    )(p, lbs)


def hgrn_bwd(p, lbs, st, do, dp):
    T = p.shape[0]
    tB = min(256, T)
    nc = tB // L
    nb = T // tB

    def body(b_ref, lb_ref, st_ref, do_ref, dp_in, dp_ref, dlb_ref, ds_scr):
        @pl.when(pl.program_id(1) == 0)
        def _():
            ds_scr[...] = jnp.zeros_like(ds_scr)
            dlb_ref[...] = jnp.zeros_like(dlb_ref)
        lb = lb_ref[...]
        bq = b_ref[:, 0:HK]
        q, sg, f, kk, v, g, rin = _hgrn_prep(bq, b_ref[:, HK:2 * HK], b_ref[:, 2 * HK:3 * HK], lb)
        q3, k3, v3, g3 = (a.reshape(nc, L, HK) for a in (q, kk, v, g))
        do3 = do_ref[...].reshape(nc, L, HK)
        dob = _b(do3)
        gl = g3[:, L - 1:L, :]
        egl = jnp.exp(gl)
        eg, egr = jnp.exp(g3), jnp.exp(gl - g3)
        qt, kt = q3 * eg, k3 * egr
        s = st_ref[...]
        ds0 = _bdot_tn(dob, _b(qt))
        dsn = ds_scr[...]
        dsns = [None] * nc
        for c in reversed(range(nc)):
            dsns[c] = dsn
            dsn = ds0[c] + dsn * egl[c]
        ds_scr[...] = dsn
        dsp = jnp.stack(dsns, axis=0)
        dspb = _b(dsp)
        dqt = _bdot(dob, _b(s))
        dkt = _bdot(_b(v3), dspb)
        dv_state = _bdot_nt(_b(kt), dspb)
        dgl = egl * jnp.sum(s * dsp, axis=1, keepdims=True) + jnp.sum(dkt * kt, axis=1, keepdims=True)
        _, dq_i, dkk_i, dv_i = _hgrn_intra(q3, k3, v3, g3, do=do3)
        dq = dq_i + dqt * eg
        dkk = dkk_i + dkt * egr
        dv = dv_i + dv_state
        rowi = lax.broadcasted_iota(jnp.int32, (nc, L, HK), 1)
        dg = q3 * dq - k3 * dkk + jnp.where(rowi == L - 1, dgl, 0.0)
        dlogf = _chunk_rev_cumsum(dg.reshape(tB, HK), rin)
        dkk2 = dkk.reshape(tB, HK)
        df = jnp.where(f > MIN_F, dlogf / f, 0.0) - dkk2
        dlb_ref[...] += jnp.sum(df * (1.0 - sg), axis=0, keepdims=True)
        dp_ref[:, 0:HK] = _b(dq.reshape(tB, HK) * QK_SCALE * _dsilu(bq))
        dp_ref[:, HK:2 * HK] = _b(df * (1.0 - lb) * sg * (1.0 - sg))
        dp_ref[:, 2 * HK:3 * HK] = _b(dv.reshape(tB, HK))

    rev = lambda n: nb - 1 - n
    bspec = pl.BlockSpec((tB, REG_BH), lambda h, n: (rev(n), OFF_B // REG_BH + h))
    return pl.pallas_call(
        body, name="hgrn_bwd", grid=(N_HGRN, nb),
        in_specs=[bspec, pl.BlockSpec((1, HK), lambda h, n: (0, h)),
                  pl.BlockSpec((None, nc, HK, HK), lambda h, n: (h, rev(n), 0, 0)),
                  pl.BlockSpec((tB, HK), lambda h, n: (rev(n), h)), pl.BlockSpec(memory_space=pl.ANY)],
        out_specs=[bspec, pl.BlockSpec((1, HK), lambda h, n: (0, h))],
        out_shape=[jax.ShapeDtypeStruct((T, NP), bf16), jax.ShapeDtypeStruct((1, N_HGRN * HK), f32)],
        input_output_aliases={4: 0},
        scratch_shapes=[pltpu.VMEM((HK, HK), f32)],
        compiler_params=_cp(("parallel", "arbitrary")),
    )(p, lbs, st, do, dp)


def _headnorm_fwd(o, z, w, nheads):
    outs = []
    for hh in range(nheads):
        sl = slice(hh * HK, (hh + 1) * HK)
        oh = o[:, sl]
        r = lax.rsqrt(jnp.mean(oh * oh, axis=-1, keepdims=True) + NORM_EPS)
        outs.append(oh * r * w * _silu(z[:, sl]))
    return jnp.concatenate(outs, axis=1)


def _headnorm_bwd(o, z, w, dy, nheads):
    dos, dzs = [], []
    dw = jnp.zeros((1, HK), f32)
    for hh in range(nheads):
        sl = slice(hh * HK, (hh + 1) * HK)
        oh, zh, dyh = o[:, sl], z[:, sl], dy[:, sl]
        r = lax.rsqrt(jnp.mean(oh * oh, axis=-1, keepdims=True) + NORM_EPS)
        on = oh * r
        sz = _silu(zh)
        dn = dyh * sz * w
        dos.append(r * (dn - on * jnp.mean(dn * on, axis=-1, keepdims=True)))
        dzs.append(dyh * on * w * _dsilu(zh))
        dw = dw + jnp.sum(dyh * sz * on, axis=0, keepdims=True)
    return jnp.concatenate(dos, axis=1), jnp.concatenate(dzs, axis=1), dw


def _merge_specs(tT, l):
    row = lambda w, cb=0: pl.BlockSpec((tT, w), lambda i, cb=cb: (i, cb))
    full = lambda r, c: pl.BlockSpec((r, c), lambda i: (0, 0))
    layer = lambda r, c: pl.BlockSpec((None, r, c), lambda i: (l, 0, 0))
    return row, full, layer


def merge_fwd(x, p, ya, ob, oc, hw, gw, bg, woa, wob, woc, wo, l):
    T = x.shape[0]
    tT = min(256, T)
    row, full, layer = _merge_specs(tT, l)

    def body(x_ref, pm_ref, ya_ref, ob_ref, oc_ref, hw_ref, gw_ref, bg_ref,
             woa_ref, wob_ref, woc_ref, wo_ref, out_ref):
        yb = _b(_headnorm_fwd(ob_ref[...], pm_ref[:, M_BZ:M_G], hw_ref[...], N_HGRN))
        yc = _b(_headnorm_fwd(oc_ref[...], pm_ref[:, M_CZ:REG_M], gw_ref[...], N_GDN))
        gates = _sigmoid(pm_ref[:, M_G:M_CZ] + bg_ref[...])
        merged = (gates[:, 0:D] * _dot(ya_ref[...], woa_ref[...]) + gates[:, D:2 * D] * _dot(yb, wob_ref[...])
                  + gates[:, 2 * D:3 * D] * _dot(yc, woc_ref[...]))
        out_ref[...] = x_ref[...] + _dot(_b(merged), wo_ref[...])

    return pl.pallas_call(
        body, name="merge_fwd", grid=(T // tT,),
        in_specs=[row(D), row(REG_M, OFF_M // REG_M),
                  row(512), row(512), row(1024), full(1, HK), full(1, HK), full(1, 3 * D),
                  layer(512, D), layer(512, D), layer(D, D), layer(D, D)],
        out_specs=row(D),
        out_shape=jax.ShapeDtypeStruct((T, D), f32),
        compiler_params=_cp(("parallel",)),
    )(x, p, ya, ob, oc, hw, gw, bg, woa, wob, woc, wo)


def merge_bwd(dxo, p, ya, ob, oc, hw, gw, bg, woa, wob, woc, wo, l):
    T = dxo.shape[0]
    tT = min(256, T)
    row, full, layer = _merge_specs(tT, l)

    def body(dx_ref, pm_ref, ya_ref, ob_ref, oc_ref, hw_ref, gw_ref, bg_ref,
             woa_ref, wob_ref, woc_ref, wo_ref,
             dya_ref, dob_ref, doc_ref, dp_ref, mg_ref, dy3_ref, yb_ref, yc_ref,
             dbg_ref, dhw_ref, dgw_ref):
        @pl.when(pl.program_id(0) == 0)
        def _():
            dbg_ref[...] = jnp.zeros_like(dbg_ref)
            dhw_ref[...] = jnp.zeros_like(dhw_ref)
            dgw_ref[...] = jnp.zeros_like(dgw_ref)
        ob, oc, bz, cz = ob_ref[...], oc_ref[...], pm_ref[:, M_BZ:M_G], pm_ref[:, M_CZ:REG_M]
        hw_, gw_ = hw_ref[...], gw_ref[...]
        yb = _b(_headnorm_fwd(ob, bz, hw_, N_HGRN))
        yc = _b(_headnorm_fwd(oc, cz, gw_, N_GDN))
        yb_ref[...] = yb
        yc_ref[...] = yc
        gates = _sigmoid(pm_ref[:, M_G:M_CZ] + bg_ref[...])
        ys = (_dot(ya_ref[...], woa_ref[...]), _dot(yb, wob_ref[...]), _dot(yc, woc_ref[...]))
        dmerged = _dot_nt(_b(dx_ref[...]), wo_ref[...])
        merged = jnp.zeros_like(dmerged)
        dys = []
        for i in range(3):
            gi = gates[:, i * D:(i + 1) * D]
            merged = merged + gi * ys[i]
            dyi = _b(dmerged * gi)
            dys.append(dyi)
            dy3_ref[:, i * D:(i + 1) * D] = dyi
            dgp = dmerged * ys[i] * gi * (1.0 - gi)
            dp_ref[:, M_G + i * D:M_G + (i + 1) * D] = _b(dgp)
            dbg_ref[:, i * D:(i + 1) * D] += jnp.sum(dgp, axis=0, keepdims=True)
        mg_ref[...] = _b(merged)
        dya_ref[...] = _dot_nt(dys[0], woa_ref[...])
        dob, dbz, dhw = _headnorm_bwd(ob, bz, hw_, _dot_nt(dys[1], wob_ref[...]), N_HGRN)
        doc, dcz, dgw = _headnorm_bwd(oc, cz, gw_, _dot_nt(dys[2], woc_ref[...]), N_GDN)
        dob_ref[...] = dob
        doc_ref[...] = doc
        dp_ref[:, M_BZ:M_G] = _b(dbz)
        dp_ref[:, M_CZ:REG_M] = _b(dcz)
        dhw_ref[...] += dhw
        dgw_ref[...] += dgw

    sd = jax.ShapeDtypeStruct
    return pl.pallas_call(
        body, name="merge_bwd", grid=(T // tT,),
        in_specs=[row(D), row(REG_M, OFF_M // REG_M),
                  row(512), row(512), row(1024), full(1, HK), full(1, HK), full(1, 3 * D),
                  layer(512, D), layer(512, D), layer(D, D), layer(D, D)],
        out_specs=[row(512), row(512), row(1024), row(REG_M, OFF_M // REG_M), row(D), row(3 * D), row(512),
                   row(1024), full(1, 3 * D), full(1, HK), full(1, HK)],
        out_shape=[sd((T, 512), f32), sd((T, 512), f32), sd((T, 1024), f32), sd((T, NP), bf16),
                   sd((T, D), bf16), sd((T, 3 * D), bf16), sd((T, 512), bf16),
                   sd((T, 1024), bf16), sd((1, 3 * D), f32), sd((1, HK), f32), sd((1, HK), f32)],
        compiler_params=_cp(("arbitrary",)),
    )(dxo, p, ya, ob, oc, hw, gw, bg, woa, wob, woc, wo)


def layer_fwd(x, w):
    l = w["l"]
    p, h = inproj_fwd(x, w["norm_w"], w["w_in"], l)
    ya = mixa_fwd(p, w["conv_a"])
    qn, kn, vv, sm = mixc_pre_fwd(p, w["conv_c"], w["alog_l"], w["dtb_l"])
    oc, st_c, ti = gdn_fwd(qn, kn, vv, sm)
    ob, st_b = hgrn_fwd(p, w["lbs"])
    xo = merge_fwd(x, p, ya, ob, oc, w["hgrn_norm_w"], w["gdn_norm_w"], w["b_gate"],
                   w["w_out_a"], w["w_out_b"], w["w_out_c"], w["w_o"], l)
    saved = dict(x=x, p=p, h=h, ya=ya, qn=qn, kn=kn, vv=vv, sm=sm, oc=oc, st_c=st_c, ti=ti, ob=ob, st_b=st_b)
    return xo, saved


OUT_MATS = (("w_out_a", "cols"), ("w_out_b", "cols"), ("w_out_c", "rows"), ("w_o", "rows"))


def layer_bwd(dxo, w, s, chip):
    p, l = s["p"], w["l"]
    (dya, dob, doc, dp, merged, dy3, yb, yc, dbg, dhw, dgw) = merge_bwd(
        dxo, p, s["ya"], s["ob"], s["oc"], w["hgrn_norm_w"], w["gdn_norm_w"], w["b_gate"],
        w["w_out_a"], w["w_out_b"], w["w_out_c"], w["w_o"], l)
    full = {"w_o": matmul_tn(merged, dxo, "dw_o", with_bf16=True),
            "w_out_a": matmul_tn(s["ya"], dy3, "dw_out_a", n=D, b_col0=0, with_bf16=True),
            "w_out_b": matmul_tn(yb, dy3, "dw_out_b", n=D, b_col0=D, with_bf16=True),
            "w_out_c": matmul_tn(yc, dy3, "dw_out_c", n=D, b_col0=2 * D, with_bf16=True)}
    out_kinds = [k for _, k in OUT_MATS]
    sent_out, token = exchange_start([full[n][1] for n, _ in OUT_MATS], out_kinds, f"grads_out_start{l}")
    dp, dlbs = hgrn_bwd(p, w["lbs"] + token[0:1, 0:1], s["st_b"], dob, dp)
    dq8, dk8, dvv, dsm8 = gdn_bwd(s["qn"], s["kn"], s["vv"], s["sm"], s["st_c"], s["ti"], doc)
    dp, dcc, dsmall = mixc_pre_bwd(p, w["conv_c"], w["alog_l"], w["dtb_l"], dq8, dk8, dvv, dsm8, dp)
    dp, dca = mixa_bwd(p, w["conv_a"], dya, dp)
    gf_win, gb_win = win_from_padded(matmul_tn(s["h"], dp, "dw_in"))
    sent_in, token = exchange_start([gb_win], ["slot"], f"grads_in_start{l}")
    dx, dnw = inproj_bwd(dp, w["w_in"], l, s["x"], w["norm_w"] + token[0:1, 0:1], dxo)
    recv_out = exchange_wait(sent_out, out_kinds, dx, f"grads_out_wait{l}")
    recv_in = exchange_wait(sent_in, ["slot"], dx, f"grads_in_wait{l}")
    half = {"w_in": partial_sum(gf_win, "slot", recv_in[0], chip, "psum_w_in")}
    for (n, kind), r in zip(OUT_MATS, recv_out):
        half[n] = partial_sum(full[n][0], kind, r, chip, "psum_" + n)
    small = dict(norm_w=dnw, b_gate=dbg, hgrn_norm_w=dhw, gdn_norm_w=dgw, lbs=dlbs, conv_a=dca[0:3], conv_c=dcc[0:4],
                 a_log=dsmall[0:1, 8:16], dt_bias=dsmall[1:2, 8:16])
    return dx, small, half


def lbs_fwd(lb):
    def body(lb_ref, o_ref):
        l0, l1 = lb_ref[0:1, :], lb_ref[1:2, :]
        mx = jnp.maximum(l0, l1)
        e0, e1 = jnp.exp(l0 - mx), jnp.exp(l1 - mx)
        o_ref[0:1, :] = jnp.zeros_like(l0)
        o_ref[1:2, :] = e1 / (e0 + e1)
    return pl.pallas_call(body, name="lbs_fwd", out_shape=jax.ShapeDtypeStruct(lb.shape, f32))(lb)


def _adam_math(w, g, m, v):
    mn = ADAM_B1 * m + (1.0 - ADAM_B1) * g
    vn = ADAM_B2 * v + (1.0 - ADAM_B2) * (g * g)
    mh = mn / (1.0 - ADAM_B1 ** ADAM_STEP)
    vh = vn / (1.0 - ADAM_B2 ** ADAM_STEP)
    return -ADAM_LR * (mh / (jnp.sqrt(vh) + ADAM_EPS) + ADAM_WD * w), mn, vn


def adam(w, g, m, v, name):
    R, C = w.shape
    tr = 256 if R % 256 == 0 else R

    def body(w_ref, g_ref, m_ref, v_ref, d_ref, mo_ref, vo_ref):
        d, mn, vn = _adam_math(w_ref[...], g_ref[...], m_ref[...], v_ref[...])
        d_ref[...] = d
        mo_ref[...] = mn
        vo_ref[...] = vn

    spec = pl.BlockSpec((tr, C), lambda i: (i, 0))
    return pl.pallas_call(
        body, name=name, grid=(R // tr,), in_specs=[spec] * 4, out_specs=[spec] * 3,
        out_shape=[jax.ShapeDtypeStruct((R, C), f32)] * 3, compiler_params=_cp(("parallel",)),
    )(w, g, m, v)


def adam_pair(h, hs, w, m, v, name):
    _, R, C = w.shape
    cp = h[0].shape[1]
    tr = 128 if R % 128 == 0 else R
    nt = R // tr

    def body(h0_ref, h1_ref, s0_ref, s1_ref, w_ref, m_ref, v_ref, g_ref, d_ref, mo_ref, vo_ref):
        def update(h_ref, s_ref):
            g = (h_ref[...] + s_ref[...])[:, :C]
            d, mn, vn = _adam_math(w_ref[...], g, m_ref[...], v_ref[...])
            g_ref[...] = g
            d_ref[...] = d
            mo_ref[...] = mn
            vo_ref[...] = vn

        @pl.when(pl.program_id(0) == 0)
        def _():
            update(h0_ref, s0_ref)

        @pl.when(pl.program_id(0) == 1)
        def _():
            update(h1_ref, s1_ref)

    h0spec = pl.BlockSpec((tr, cp), lambda l, i: (jnp.where(l == 0, i, nt - 1), 0))
    h1spec = pl.BlockSpec((tr, cp), lambda l, i: (jnp.where(l == 1, i, 0), 0))
    spec = pl.BlockSpec((None, tr, C), lambda l, i: (l, i, 0))
    return pl.pallas_call(
        body, name=name, grid=(2, nt), in_specs=[h0spec, h1spec, h0spec, h1spec, spec, spec, spec],
        out_specs=[spec] * 4, out_shape=[jax.ShapeDtypeStruct(w.shape, f32)] * 4,
        compiler_params=_cp(("arbitrary", "arbitrary")),
    )(h[0], h[1], hs[0], hs[1], w, m, v)


_SMALL = (("norm_w", 2 * D), ("b_gate", 6 * D), ("lower_bounds", None), ("hgrn_norm_w", 2 * HK),
          ("gdn_norm_w", 2 * HK), ("a_log", 16), ("dt_bias", 16), ("final_norm_w", D), ("loss", None))
_CONV = (("conv_a", 2 * 3 * 512), ("conv_c", 2 * 4 * 2048))


def _small_rows(n):
    return 16 if n is None else -(-n // 1024) * 8


LB_ROW = sum(_small_rows(n) for _, n in _SMALL[:2])
ADAM_ROWS = sum(_small_rows(n) for _, n in _SMALL)
SMALL_ROWS = ADAM_ROWS + sum(_small_rows(n) for _, n in _CONV)


def small_update(parts, wp, mp, vp):
    def body(p_ref, w_ref, m_ref, v_ref, g_ref, d_ref, mo_ref, vo_ref):
        gs = p_ref[0]
        for i in range(1, 8):
            gs = gs + p_ref[i]
        w = w_ref[...]
        l0, l1 = w[LB_ROW:LB_ROW + 8], w[LB_ROW + 8:LB_ROW + 16]
        mx = jnp.maximum(l0, l1)
        e0, e1 = jnp.exp(l0 - mx), jnp.exp(l1 - mx)
        p0, p1 = e0 / (e0 + e1), e1 / (e0 + e1)
        dl1 = gs[LB_ROW + 8:LB_ROW + 16]
        s = p1 * dl1
        g = jnp.concatenate([gs[0:LB_ROW], -p0 * s, p1 * dl1 - p1 * s, gs[LB_ROW + 16:ADAM_ROWS]], axis=0)
        d, mn, vn = _adam_math(w, g, m_ref[...], v_ref[...])
        g_ref[0:ADAM_ROWS, :] = g
        g_ref[ADAM_ROWS:, :] = gs[ADAM_ROWS:]
        d_ref[...] = d
        mo_ref[...] = mn
        vo_ref[...] = vn
    sd = jax.ShapeDtypeStruct
    return pl.pallas_call(body, name="small_update",
                          out_shape=[sd((SMALL_ROWS, 128), f32)] + [sd((ADAM_ROWS, 128), f32)] * 3)(parts, wp, mp, vp)


def partial_sum(own, kind, recv, chip, name):
    _, r, c = recv.shape
    tr = 256 if r % 256 == 0 else r

    def body(chip_ref, o_ref, r_ref, out_ref):
        out_ref[...] = ((o_ref[...] + r_ref[0].astype(f32)) + r_ref[1].astype(f32)) + r_ref[2].astype(f32)

    own_spec = {"slot": pl.BlockSpec((None, tr, c), lambda i, chip: (chip[0], i, 0)),
                "cols": pl.BlockSpec((tr, c), lambda i, chip: (i, chip[0])),
                "rows": pl.BlockSpec((tr, c), lambda i, chip: (chip[0] * (r // tr) + i, 0))}[kind]
    return pl.pallas_call(
        body, name=name,
        grid_spec=pltpu.PrefetchScalarGridSpec(
            num_scalar_prefetch=1, grid=(r // tr,),
            in_specs=[own_spec, pl.BlockSpec((3, tr, c), lambda i, chip: (0, i, 0))],
            out_specs=pl.BlockSpec((tr, c), lambda i, chip: (i, 0))),
        out_shape=jax.ShapeDtypeStruct((r, c), f32), compiler_params=_cp(("arbitrary",)),
    )(chip, own, recv)


MESH = pl.DeviceIdType.MESH
_HBM = pl.BlockSpec(memory_space=pltpu.HBM)


def _place():
    return lax.axis_index("x"), lax.axis_index("y"), lax.axis_index("c")


def weight_gather(arrs):
    n = len(arrs)

    def body(*refs):
        x_refs, out_refs = refs[:n], refs[n:2 * n]
        send_sems, recv_sems, local_sems = refs[2 * n:]
        x, y, c = _place()
        me, sibling = (x, y, c), (x, y, 1 - c)
        chips = [(1 - x, y), (x, 1 - y), (1 - x, 1 - y)]

        def copy(a, k, block, to, own_src=False):
            px, py, pc = block
            dst = out_refs[a].at[2 * px + py, pc]
            return pltpu.make_async_remote_copy(
                src_ref=x_refs[a].at[c] if own_src else dst, dst_ref=dst,
                send_sem=send_sems.at[7 * a + k], recv_sem=recv_sems.at[7 * a + k], device_id=to, device_id_type=MESH)

        mine = [pltpu.make_async_copy(x_refs[a].at[c], out_refs[a].at[2 * x + y, c], local_sems.at[a])
                for a in range(n)]
        for cp in mine:
            cp.start()
        first = []
        for a in range(n):
            first.append(copy(a, 0, me, sibling, own_src=True))
            first += [copy(a, 1 + j, me, (*chip, c), own_src=True) for j, chip in enumerate(chips)]
        for cp in first:
            cp.start()
        passed = []
        for j, chip in enumerate(chips):
            for a in range(n):
                copy(a, 1 + j, (*chip, c), me).wait_recv()
                fwd = copy(a, 4 + j, (*chip, c), sibling)
                fwd.start()
                passed.append(fwd)
        for a in range(n):
            copy(a, 0, sibling, me).wait_recv()
            for j, chip in enumerate(chips):
                copy(a, 4 + j, (*chip, 1 - c), me).wait_recv()
        for cp in first + passed:
            cp.wait_send()
        for cp in mine:
            cp.wait()

    return pl.pallas_call(
        body, name="weight_gather", in_specs=[_HBM] * n, out_specs=[_HBM] * n,
        out_shape=[jax.ShapeDtypeStruct((N_CHIPS,) + a.shape, a.dtype) for a in arrs],
        scratch_shapes=[pltpu.SemaphoreType.DMA((7 * n,)), pltpu.SemaphoreType.DMA((7 * n,)),
                        pltpu.SemaphoreType.DMA((n,))],
    )(*arrs)


SHARD_W = 256


_SEM = pl.BlockSpec(memory_space=pltpu.SEMAPHORE)
_EFFECT = pltpu.SideEffectType.DATAFLOW_SIDE_EFFECTING


def _landing_shape(a, kind):
    if kind == "slot":
        return a.shape[1:]
    return (a.shape[0], SHARD_W) if kind == "cols" else (SHARD_W, a.shape[1])


def _shard_copies(src_refs, land_refs, kinds, send_sems, recv_sems):
    x, y, c = _place()
    copies = []
    for a, (src, land, kind) in enumerate(zip(src_refs, land_refs, kinds)):
        for j, (px, py) in enumerate(((1 - x, y), (x, 1 - y), (1 - x, 1 - y))):
            q = 2 * px + py
            lo = pl.multiple_of(q * SHARD_W, SHARD_W)
            part = {"slot": lambda: src.at[q], "cols": lambda: src.at[:, pl.ds(lo, SHARD_W)],
                    "rows": lambda: src.at[pl.ds(lo, SHARD_W), :]}[kind]()
            k = 3 * a + j
            copies.append(pltpu.make_async_remote_copy(
                src_ref=part, dst_ref=land.at[j], send_sem=send_sems.at[k], recv_sem=recv_sems.at[k],
                device_id=(px, py, c), device_id_type=MESH))
    return copies


def exchange_start(srcs, kinds, name):
    n = len(srcs)
    lands = [lax.empty((3,) + _landing_shape(a, k), a.dtype) for a, k in zip(srcs, kinds)]

    def body(*refs):
        src_refs, land_refs, send_sems, recv_sems, token = refs[:n], refs[n:2 * n], refs[2 * n], refs[2 * n + 1], refs[-1]
        for cp in _shard_copies(src_refs, land_refs, kinds, send_sems, recv_sems):
            cp.start()
        token[...] = jnp.zeros_like(token)

    both = list(srcs) + lands
    out = pl.pallas_call(
        body, name=name,
        out_shape=(pltpu.SemaphoreType.DMA((3 * n,)), pltpu.SemaphoreType.DMA((3 * n,)),
                   *[pltpu.HBM(a.shape, a.dtype) for a in both], jax.ShapeDtypeStruct((8, 128), f32)),
        in_specs=[_HBM] * (2 * n), out_specs=(_SEM, _SEM, *[_HBM] * (2 * n), pl.BlockSpec(memory_space=pltpu.VMEM)),
        input_output_aliases={i: 2 + i for i in range(2 * n)},
        compiler_params=pltpu.CompilerParams(has_side_effects=_EFFECT),
    )(*[pltpu.with_memory_space_constraint(a, pltpu.HBM) for a in both])
    return (out[0], out[1], out[2:2 + 2 * n]), out[-1]


def exchange_wait(handle, kinds, after, name):
    send_sems, recv_sems, both = handle
    n = len(kinds)

    def body(*refs):
        src_refs, land_refs, s_sems, r_sems = refs[:n], refs[n:2 * n], refs[2 * n], refs[2 * n + 1]
        for cp in _shard_copies(src_refs, land_refs, kinds, s_sems, r_sems):
            cp.wait_send()
            cp.wait_recv()

    out = pl.pallas_call(
        body, name=name, out_shape=tuple(pltpu.HBM(a.shape, a.dtype) for a in both),
        in_specs=[_HBM] * (2 * n) + [_SEM, _SEM, pl.BlockSpec(memory_space=pl.ANY)], out_specs=tuple([_HBM] * (2 * n)),
        input_output_aliases={i: i for i in range(2 * n)},
        compiler_params=pltpu.CompilerParams(has_side_effects=_EFFECT),
    )(*both, send_sems, recv_sems, after)
    return out[n:]


def final_exchange(hs, small):
    n = len(hs)
    S = small.shape[0]

    def body(*refs):
        h_refs, sm_ref, out_refs, smalls_ref = refs[:n], refs[n], refs[n + 1:2 * n + 1], refs[2 * n + 1]
        send_sems, recv_sems, local_sem = refs[2 * n + 2:]
        x, y, c = _place()
        my_slot = smalls_ref.at[4 * x + 2 * y + c]
        mine = pltpu.make_async_copy(sm_ref, my_slot, local_sem)
        mine.start()
        copies = [pltpu.make_async_remote_copy(src_ref=h_refs[a], dst_ref=out_refs[a], send_sem=send_sems.at[a],
                                               recv_sem=recv_sems.at[a], device_id=(x, y, 1 - c), device_id_type=MESH)
                  for a in range(n)]
        for mask in range(1, 8):
            fx, fy, fc = (mask >> 2) & 1, (mask >> 1) & 1, mask & 1
            peer = ((1 - x) if fx else x, (1 - y) if fy else y, (1 - c) if fc else c)
            copies.append(pltpu.make_async_remote_copy(
                src_ref=sm_ref, dst_ref=my_slot, send_sem=send_sems.at[n - 1 + mask], recv_sem=recv_sems.at[n - 1 + mask],
                device_id=peer, device_id_type=MESH))
        for cp in copies:
            cp.start()
        for cp in copies:
            cp.wait_recv()
        for cp in copies:
            cp.wait_send()
        mine.wait()

    sd = jax.ShapeDtypeStruct
    out = pl.pallas_call(
        body, name="final_exchange", in_specs=[_HBM] * (n + 1), out_specs=[_HBM] * (n + 1),
        out_shape=[sd(h.shape, h.dtype) for h in hs] + [sd((8, S, 128), f32)],
        scratch_shapes=[pltpu.SemaphoreType.DMA((n + 7,)), pltpu.SemaphoreType.DMA((n + 7,)),
                        pltpu.SemaphoreType.DMA],
    )(*hs, small)
    return out[:n], out[n]


N_CHIPS = 4
SHARD_COLS = N_ORIG // N_CHIPS


SHARD_PAD = 2688
_COL_SEGMENTS = (
    ((0, 2048, OFF_A),)
    + tuple((2048 + 512 * j + HK * h, 2048 + 512 * j + HK * (h + 1), OFF_B + REG_BH * h + HK * j)
            for j in range(3) for h in range(N_HGRN))
    + ((3584, 4096, OFF_M + M_BZ), (4096, 6144, OFF_C), (6144, 6160, OFF_C + 2048), (6160, 7184, OFF_M + M_CZ),
       (7184, N_ORIG, OFF_M + M_G)))


def _shard_pieces():
    pieces = []
    for lo, hi, dst in _COL_SEGMENTS:
        for p in range(N_CHIPS):
            a, b = max(lo, p * SHARD_COLS), min(hi, (p + 1) * SHARD_COLS)
            if a < b:
                pieces.append((p, a - p * SHARD_COLS, dst + a - lo, b - a))
    return pieces


def win_cast_pad(w):
    tr = 256

    def body(x_ref, o_ref):
        o_ref[:, :SHARD_COLS] = _b(x_ref[...])
        o_ref[:, SHARD_COLS:] = jnp.zeros((tr, SHARD_PAD - SHARD_COLS), bf16)

    return pl.pallas_call(
        body, name="win_cast_pad", grid=(2, D // tr),
        in_specs=[pl.BlockSpec((None, tr, SHARD_COLS), lambda l, i: (l, i, 0))],
        out_specs=pl.BlockSpec((None, tr, SHARD_PAD), lambda l, i: (l, i, 0)),
        out_shape=jax.ShapeDtypeStruct((2, D, SHARD_PAD), bf16), compiler_params=_cp(("parallel", "parallel")),
    )(w)


def win_to_padded(w4):
    tr = 256
    pieces = _shard_pieces()

    def body(a_ref, o_ref):
        o_ref[...] = jnp.zeros((tr, NP), bf16)
        for p, j0, c0, n in pieces:
            o_ref[:, c0:c0 + n] = a_ref[p, :, j0:j0 + n]

    return pl.pallas_call(
        body, name="win_to_padded", grid=(2, D // tr),
        in_specs=[pl.BlockSpec((N_CHIPS, None, tr, SHARD_PAD), lambda l, i: (0, l, i, 0))],
        out_specs=pl.BlockSpec((None, tr, NP), lambda l, i: (l, i, 0)),
        out_shape=jax.ShapeDtypeStruct((2, D, NP), bf16), compiler_params=_cp(("parallel", "parallel")),
    )(w4)


def win_from_padded(dw):
    tr = 128
    pieces = _shard_pieces()

    def body(d_ref, of_ref, ob_ref):
        for p in range(N_CHIPS):
            of_ref[p, :, SHARD_COLS:] = jnp.zeros((tr, SHARD_PAD - SHARD_COLS), f32)
            ob_ref[p, :, SHARD_COLS:] = jnp.zeros((tr, SHARD_PAD - SHARD_COLS), bf16)
        for p, j0, c0, n in pieces:
            v = d_ref[:, c0:c0 + n]
            of_ref[p, :, j0:j0 + n] = v
            ob_ref[p, :, j0:j0 + n] = _b(v)

    out_spec = pl.BlockSpec((N_CHIPS, tr, SHARD_PAD), lambda i: (0, i, 0))
    return pl.pallas_call(
        body, name="win_from_padded", grid=(D // tr,),
        in_specs=[pl.BlockSpec((tr, NP), lambda i: (i, 0))], out_specs=[out_spec, out_spec],
        out_shape=[jax.ShapeDtypeStruct((N_CHIPS, D, SHARD_PAD), f32),
                   jax.ShapeDtypeStruct((N_CHIPS, D, SHARD_PAD), bf16)],
        compiler_params=_cp(("parallel",)),
    )(dw)


def _rows128(a):
    flat = a.reshape(-1)
    total = -(-flat.shape[0] // 1024) * 1024
    return jnp.pad(flat, (0, total - flat.shape[0])).reshape(total // 128, 128)


def _lb_rows(lb):
    return jnp.pad(lb.reshape(2, 4, 128), ((0, 0), (0, 4), (0, 0))).reshape(16, 128)


def _pack_small(v, with_conv):
    rows = []
    for name, n in _SMALL + (_CONV if with_conv else ()):
        if name == "lower_bounds":
            rows.append(_lb_rows(v[name]))
        elif name == "loss":
            rows.append(jnp.broadcast_to(v[name], (16, 128)) if name in v else jnp.zeros((16, 128), f32))
        else:
            rows.append(_rows128(v[name]))
    return jnp.concatenate(rows, axis=0)


def _unpack_small(p, shapes, with_conv):
    out, row = {}, 0
    for name, n in _SMALL + (_CONV if with_conv else ()):
        nrows = _small_rows(n)
        blk = p[row:row + nrows]
        if name == "lower_bounds":
            out[name] = blk.reshape(2, 8, 128)[:, :4].reshape(2, 512)
        elif name == "loss":
            out[name] = blk[0, 0]
        else:
            out[name] = blk.reshape(-1)[:n].reshape(shapes[name])
        row += nrows
    return out


def _lane_vec(a8):
    return jnp.pad(a8.reshape(1, 8), ((0, 0), (8, 112)))


WEIGHT_NAMES = ("norm_w", "w_in", "b_gate", "conv_a", "conv_c", "a_log", "dt_bias", "lower_bounds", "hgrn_norm_w",
                "gdn_norm_w", "w_out_a", "w_out_b", "w_out_c", "w_o", "final_norm_w")


def kernel(x, norm_w, w_in, b_gate, conv_a, conv_c, a_log, dt_bias, lower_bounds, hgrn_norm_w, gdn_norm_w, w_out_a, w_out_b, w_out_c, w_o, final_norm_w, loss_target, m_norm_w, m_w_in, m_b_gate, m_conv_a, m_conv_c, m_a_log, m_dt_bias, m_lower_bounds, m_hgrn_norm_w, m_gdn_norm_w, m_w_out_a, m_w_out_b, m_w_out_c, m_w_o, m_final_norm_w, v_norm_w, v_w_in, v_b_gate, v_conv_a, v_conv_c, v_a_log, v_dt_bias, v_lower_bounds, v_hgrn_norm_w, v_gdn_norm_w, v_w_out_a, v_w_out_b, v_w_out_c, v_w_o, v_final_norm_w):
    wts = dict(norm_w=norm_w, w_in=w_in, b_gate=b_gate, conv_a=conv_a, conv_c=conv_c, a_log=a_log, dt_bias=dt_bias,
               lower_bounds=lower_bounds, hgrn_norm_w=hgrn_norm_w, gdn_norm_w=gdn_norm_w, w_out_a=w_out_a,
               w_out_b=w_out_b, w_out_c=w_out_c, w_o=w_o, final_norm_w=final_norm_w)
    mom = dict(norm_w=m_norm_w, w_in=m_w_in, b_gate=m_b_gate, conv_a=m_conv_a, conv_c=m_conv_c, a_log=m_a_log,
               dt_bias=m_dt_bias, lower_bounds=m_lower_bounds, hgrn_norm_w=m_hgrn_norm_w, gdn_norm_w=m_gdn_norm_w,
               w_out_a=m_w_out_a, w_out_b=m_w_out_b, w_out_c=m_w_out_c, w_o=m_w_o, final_norm_w=m_final_norm_w)
    var = dict(norm_w=v_norm_w, w_in=v_w_in, b_gate=v_b_gate, conv_a=v_conv_a, conv_c=v_conv_c, a_log=v_a_log,
               dt_bias=v_dt_bias, lower_bounds=v_lower_bounds, hgrn_norm_w=v_hgrn_norm_w, gdn_norm_w=v_gdn_norm_w,
               w_out_a=v_w_out_a, w_out_b=v_w_out_b, w_out_c=v_w_out_c, w_o=v_w_o, final_norm_w=v_final_norm_w)
    chip = 2 * lax.axis_index("x") + lax.axis_index("y")
    chip1 = chip.reshape(1).astype(jnp.int32)

    win4, woa4, wob4, woc4, wo4, ca4, cc4 = weight_gather(
        [win_cast_pad(w_in), _b(w_out_a), _b(w_out_b), _b(w_out_c), _b(w_o), conv_a, conv_c])
    by_cols = lambda a: a.transpose(1, 2, 0, 3).reshape(a.shape[1], a.shape[2], N_CHIPS * a.shape[3])
    by_rows = lambda a: a.transpose(1, 0, 2, 3).reshape(a.shape[1], N_CHIPS * a.shape[2], a.shape[3])
    full = dict(w_in=win_to_padded(win4), w_out_a=by_cols(woa4), w_out_b=by_cols(wob4), w_out_c=by_rows(woc4),
                w_o=by_rows(wo4), conv_a=by_cols(ca4), conv_c=by_cols(cc4))
    lbs = lbs_fwd(lower_bounds)
    layers = []
    for l in range(2):
        layers.append(dict(
            l=l, norm_w=norm_w[l:l + 1], w_in=full["w_in"], b_gate=b_gate[l:l + 1], conv_a=full["conv_a"][l],
            conv_c=full["conv_c"][l], alog_l=_lane_vec(a_log[l]), dtb_l=_lane_vec(dt_bias[l]), lbs=lbs[l:l + 1],
            hgrn_norm_w=hgrn_norm_w[l:l + 1], gdn_norm_w=gdn_norm_w[l:l + 1], w_out_a=full["w_out_a"],
            w_out_b=full["w_out_b"], w_out_c=full["w_out_c"], w_o=full["w_o"]))

    xs, saved = x[0], []
    for l in range(2):
        xs, s = layer_fwd(xs, layers[l])
        saved.append(s)
    loss_row, dx, dfw = loss_head(xs, final_norm_w.reshape(1, D), loss_target[0])
    lg, half = [None, None], [None, None]
    for l in (1, 0):
        dx, lg[l], half[l] = layer_bwd(dx, layers[l], saved[l], chip1)
    grad_x = dx[None]

    stack = lambda n: jnp.stack([lg[0][n], lg[1][n]], axis=0)
    gsmall = {n: stack(n) for n in ("norm_w", "b_gate", "hgrn_norm_w", "gdn_norm_w", "a_log", "dt_bias", "conv_a",
                                    "conv_c")}
    gsmall.update(lower_bounds=stack("lbs"), final_norm_w=dfw, loss=loss_row)
    mat_names = ("w_in",) + tuple(n for n, _ in OUT_MATS)
    mine = [half[l][n] for n in mat_names for l in range(2)]
    theirs, smalls = final_exchange(mine, _pack_small(gsmall, True))

    out_g, out_d, out_m, out_v = {}, {}, {}, {}
    for i, n in enumerate(mat_names):
        out_g[n], out_d[n], out_m[n], out_v[n] = adam_pair(mine[2 * i:2 * i + 2], theirs[2 * i:2 * i + 2], wts[n],
                                                           mom[n], var[n], "adam_" + n)
    small_names = [n for n, _ in _SMALL if n != "loss"]
    pack = lambda v: _pack_small({n: v[n] for n in small_names}, False)
    sg, sd, smn, svn = small_update(smalls, pack(wts), pack(mom), pack(var))
    shapes = {n: wts[n].shape for n in small_names}
    shapes.update(conv_a=(2, 3, 512), conv_c=(2, 4, 2048))
    for dst, src, conv in ((out_g, sg, True), (out_d, sd, False), (out_m, smn, False), (out_v, svn, False)):
        dst.update(_unpack_small(src, shapes, conv))
    loss = out_g.pop("loss")
    for n in ("conv_a", "conv_c"):
        width = wts[n].shape[2]
        g = lax.dynamic_slice_in_dim(out_g[n], chip * width, width, axis=2)
        two_d = lambda a: a.reshape(-1, width)
        d, mn, vn = adam(two_d(wts[n]), two_d(g), two_d(mom[n]), two_d(var[n]), "adam_" + n)
        out_g[n] = g
        out_d[n], out_m[n], out_v[n] = (a.reshape(wts[n].shape) for a in (d, mn, vn))
    return (loss, grad_x, *[out_g[n] for n in WEIGHT_NAMES], *[out_d[n] for n in WEIGHT_NAMES],
            *[out_m[n] for n in WEIGHT_NAMES], *[out_v[n] for n in WEIGHT_NAMES])
```

```python
import functools

import jax
import jax.numpy as jnp
from jax import lax
from jax.experimental import pallas as pl
from jax.experimental.pallas import tpu as pltpu

f32 = jnp.float32
bf16 = jnp.bfloat16

D = 1024
L = 64
SUB = 16
NORM_EPS = 1e-6
L2_EPS = 1e-6
MIN_F = 1e-30
HK = 128
QK_SCALE = HK ** -0.5
N_GDN = 8
GDN_BLOCK = 1024
N_HGRN = 4

REG_A = 2304
REG_C = 2304
REG_M = 4608
REG_BH = 384
OFF_A, OFF_C, OFF_M, OFF_B = 0, 2304, 4608, 9216
M_BZ, M_G, M_CZ = 0, 512, 3584
NP = 10752
NP_TILE = 1536
N_ORIG = 10256

ADAM_LR, ADAM_B1, ADAM_B2, ADAM_EPS, ADAM_WD, ADAM_STEP = 0.001, 0.9, 0.999, 1e-08, 0.01, 10

VMEM_LIMIT = 56 * 1024 * 1024


def _cp(sem):
    return pltpu.CompilerParams(dimension_semantics=sem, vmem_limit_bytes=VMEM_LIMIT)


def _sigmoid(x):
    return jax.nn.sigmoid(x)


def _silu(x):
    return x * _sigmoid(x)


def _dsilu(x):
    s = _sigmoid(x)
    return s * (1.0 + x * (1.0 - s))


def _softplus(x):
    u = jnp.exp(-jnp.abs(x))
    w = 1.0 + u
    l1p = jnp.where(w == 1.0, u, jnp.log(w) * (u / (w - 1.0)))
    return jnp.maximum(x, 0.0) + l1p


def _dot(a, b):
    return jnp.dot(a, b, preferred_element_type=f32)


def _dot_nt(a, b):
    return lax.dot_general(a, b, (((1,), (1,)), ((), ())), preferred_element_type=f32)


def _dot_tn(a, b):
    return lax.dot_general(a, b, (((0,), (0,)), ((), ())), preferred_element_type=f32)


def _bdot(a, b):
    return lax.dot_general(a, b, (((2,), (1,)), ((0,), (0,))), preferred_element_type=f32)


def _bdot_nt(a, b):
    return lax.dot_general(a, b, (((2,), (2,)), ((0,), (0,))), preferred_element_type=f32)


def _bdot_tn(a, b):
    return lax.dot_general(a, b, (((1,), (1,)), ((0,), (0,))), preferred_element_type=f32)


def _bdot_split(a, b):
    ah, bh = _b(a), _b(b)
    al, bl = _b(a - ah.astype(f32)), _b(b - bh.astype(f32))
    return _bdot(ah, bh) + (_bdot(ah, bl) + _bdot(al, bh))


def _b(x):
    return x.astype(bf16)


def _chunk_cumsum(x, rows_in_chunk):
    n = x.shape[0]
    for s in (1, 2, 4, 8, 16, 32):
        x = x + jnp.where(rows_in_chunk >= s, pltpu.roll(x, s, axis=0), 0.0)
    return x


def _chunk_rev_cumsum(x, rows_in_chunk):
    n = x.shape[0]
    for s in (1, 2, 4, 8, 16, 32):
        x = x + jnp.where(rows_in_chunk + s < L, pltpu.roll(x, n - s, axis=0), 0.0)
    return x


def _shift_down(x, s):
    return pltpu.roll(x, s, axis=0) if s else x


def _shift_up(x, s):
    return pltpu.roll(x, x.shape[0] - s, axis=0) if s else x


def inproj_fwd(x, nw, w, l):
    T = x.shape[0]
    tT, tn = min(1024, T), NP_TILE

    def body(x_ref, nw_ref, w_ref, p_ref, h_ref, hs):
        @pl.when(pl.program_id(1) == 0)
        def _():
            xv = x_ref[...]
            r = lax.rsqrt(jnp.mean(xv * xv, axis=-1, keepdims=True) + NORM_EPS)
            hv = _b(xv * r * nw_ref[...])
            hs[...] = hv
            h_ref[...] = hv
        p_ref[...] = _dot(hs[...], w_ref[...])

    return pl.pallas_call(
        body, name="inproj_fwd", grid=(T // tT, NP // tn),
        in_specs=[pl.BlockSpec((tT, D), lambda i, j: (i, 0)), pl.BlockSpec((1, D), lambda i, j: (0, 0)),
                  pl.BlockSpec((None, D, tn), lambda i, j: (l, 0, j))],
        out_specs=[pl.BlockSpec((tT, tn), lambda i, j: (i, j)), pl.BlockSpec((tT, D), lambda i, j: (i, 0))],
        out_shape=[jax.ShapeDtypeStruct((T, NP), f32), jax.ShapeDtypeStruct((T, D), bf16)],
        scratch_shapes=[pltpu.VMEM((tT, D), bf16)],
        compiler_params=_cp(("parallel", "arbitrary")),
    )(x, nw, w)


def matmul_tn(a, b, name, n=None, b_col0=0, with_bf16=False):
    T, K = a.shape
    N = b.shape[1] if n is None else n
    tT = min(2048, T)
    tn = NP_TILE if N % NP_TILE == 0 else min(N, 1024)
    nt = T // tT
    cb0 = b_col0 // tn

    def body(a_ref, b_ref, o_ref, *ob_ref):
        @pl.when(pl.program_id(1) == 0)
        def _():
            o_ref[...] = jnp.zeros_like(o_ref)
        o_ref[...] += _dot_tn(_b(a_ref[...]), _b(b_ref[...]))
        if with_bf16:
            @pl.when(pl.program_id(1) == nt - 1)
            def _():
                ob_ref[0][...] = _b(o_ref[...])

    ospec = pl.BlockSpec((K, tn), lambda j, t: (0, j))
    return pl.pallas_call(
        body, name=name, grid=(N // tn, nt),
        in_specs=[pl.BlockSpec((tT, K), lambda j, t: (t, 0)), pl.BlockSpec((tT, tn), lambda j, t: (t, cb0 + j))],
        out_specs=[ospec, ospec] if with_bf16 else ospec,
        out_shape=([jax.ShapeDtypeStruct((K, N), f32), jax.ShapeDtypeStruct((K, N), bf16)] if with_bf16
                   else jax.ShapeDtypeStruct((K, N), f32)),
        compiler_params=_cp(("parallel", "arbitrary")),
    )(a, b)


def inproj_bwd(dp, w, l, x, nw, dres):
    T = x.shape[0]
    tT, tk = min(512, T), NP_TILE
    nk = NP // tk

    def body(dp_ref, w_ref, x_ref, nw_ref, dres_ref, dx_ref, dnw_ref, acc):
        i, k = pl.program_id(0), pl.program_id(1)

        @pl.when((i == 0) & (k == 0))
        def _():
            dnw_ref[...] = jnp.zeros_like(dnw_ref)

        @pl.when(k == 0)
        def _():
            acc[...] = jnp.zeros_like(acc)
        acc[...] += _dot_nt(dp_ref[...], w_ref[...])

        @pl.when(k == nk - 1)
        def _():
            xv = x_ref[...]
            r = lax.rsqrt(jnp.mean(xv * xv, axis=-1, keepdims=True) + NORM_EPS)
            xh = xv * r
            dy = acc[...]
            dyw = dy * nw_ref[...]
            dx_ref[...] = r * (dyw - xh * jnp.mean(dyw * xh, axis=-1, keepdims=True)) + dres_ref[...]
            dnw_ref[...] += jnp.sum(dy * xh, axis=0, keepdims=True)

    return pl.pallas_call(
        body, name="inproj_bwd", grid=(T // tT, nk),
        in_specs=[pl.BlockSpec((tT, tk), lambda i, k: (i, k)), pl.BlockSpec((None, D, tk), lambda i, k: (l, 0, k)),
                  pl.BlockSpec((tT, D), lambda i, k: (i, 0)), pl.BlockSpec((1, D), lambda i, k: (0, 0)),
                  pl.BlockSpec((tT, D), lambda i, k: (i, 0))],
        out_specs=[pl.BlockSpec((tT, D), lambda i, k: (i, 0)), pl.BlockSpec((1, D), lambda i, k: (0, 0))],
        out_shape=[jax.ShapeDtypeStruct((T, D), f32), jax.ShapeDtypeStruct((1, D), f32)],
        scratch_shapes=[pltpu.VMEM((tT, D), f32)],
        compiler_params=_cp(("arbitrary", "arbitrary")),
    )(dp, w, x, nw, dres)


def loss_head(x, fw, tgt):
    T = x.shape[0]
    tT = min(512, T)

    def body(x_ref, fw_ref, t_ref, loss_ref, dx_ref, dfw_ref):
        @pl.when(pl.program_id(0) == 0)
        def _():
            loss_ref[...] = jnp.zeros_like(loss_ref)
            dfw_ref[...] = jnp.zeros_like(dfw_ref)
        xv = x_ref[...]
        r = lax.rsqrt(jnp.mean(xv * xv, axis=-1, keepdims=True) + NORM_EPS)
        xh = xv * r
        err = xh * fw_ref[...] - t_ref[...]
        part = 0.5 * jnp.sum(jnp.mean(err * err, axis=-1, keepdims=True), axis=0, keepdims=True)
        loss_ref[...] += jnp.broadcast_to(part, loss_ref.shape)
        dy = err * (1.0 / D)
        dyw = dy * fw_ref[...]
        dx_ref[...] = r * (dyw - xh * jnp.mean(dyw * xh, axis=-1, keepdims=True))
        dfw_ref[...] += jnp.sum(dy * xh, axis=0, keepdims=True)

    return pl.pallas_call(
        body, name="loss_head", grid=(T // tT,),
        in_specs=[pl.BlockSpec((tT, D), lambda i: (i, 0)), pl.BlockSpec((1, D), lambda i: (0, 0)),
                  pl.BlockSpec((tT, D), lambda i: (i, 0))],
        out_specs=[pl.BlockSpec((1, 128), lambda i: (0, 0)), pl.BlockSpec((tT, D), lambda i: (i, 0)),
                   pl.BlockSpec((1, D), lambda i: (0, 0))],
        out_shape=[jax.ShapeDtypeStruct((1, 128), f32), jax.ShapeDtypeStruct((T, D), f32),
                   jax.ShapeDtypeStruct((1, D), f32)],
        compiler_params=_cp(("arbitrary",)),
    )(x, fw, tgt)


def _halo_specs(tT, T, width, colblk):
    nb8 = T // 8
    per = tT // 8
    prev = pl.BlockSpec((8, width), lambda i: (jnp.maximum(i * per - 1, 0), colblk))
    nxt = pl.BlockSpec((8, width), lambda i: (jnp.minimum((i + 1) * per, nb8 - 1), colblk))
    return prev, nxt


def mixa_fwd(p, cw):
    T = p.shape[0]
    tT = min(512, T)
    prev_spec, _ = _halo_specs(tT, T, REG_A, OFF_A // REG_A)

    def body(p_ref, pp_ref, cw_ref, y_ref):
        pv = p_ref[...]
        u = pv[:, 512:1024] * pv[:, 1024:1536]
        pp = pp_ref[...]
        up = jnp.where(pl.program_id(0) == 0, 0.0, pp[:, 512:1024] * pp[:, 1024:1536])
        ue = jnp.concatenate([up, u], axis=0)
        cv = cw_ref[0:1, :] * _shift_down(ue, 2) + cw_ref[1:2, :] * _shift_down(ue, 1) + cw_ref[2:3, :] * ue
        y_ref[...] = _b(pv[:, 0:512] * cv[8:] * _silu(pv[:, 1536:2048]))

    return pl.pallas_call(
        body, name="mixa_fwd", grid=(T // tT,),
        in_specs=[pl.BlockSpec((tT, REG_A), lambda i: (i, OFF_A // REG_A)), prev_spec,
                  pl.BlockSpec((3, 512), lambda i: (0, 0))],
        out_specs=pl.BlockSpec((tT, 512), lambda i: (i, 0)),
        out_shape=jax.ShapeDtypeStruct((T, 512), bf16),
        compiler_params=_cp(("parallel",)),
    )(p, p, cw)


def mixa_bwd(p, cw, dy, dp):
    T = p.shape[0]
    tT = min(512, T)
    nt = T // tT
    prev_spec, next_spec = _halo_specs(tT, T, REG_A, OFF_A // REG_A)
    _, dnext_spec = _halo_specs(tT, T, 512, 0)

    def body(p_ref, pp_ref, pn_ref, cw_ref, dy_ref, dyn_ref, dp_in, dp_ref, dcw_ref):
        i = pl.program_id(0)

        @pl.when(i == 0)
        def _():
            dcw_ref[...] = jnp.zeros_like(dcw_ref)
        pv, pp, pn = p_ref[:, 0:2048], pp_ref[:, 0:2048], pn_ref[:, 0:2048]
        pe = jnp.concatenate([pp, pv, pn], axis=0)
        rows = lax.broadcasted_iota(jnp.int32, (tT + 16, 1), 0)
        ab, ac, ax, az = pe[:, 0:512], pe[:, 512:1024], pe[:, 1024:1536], pe[:, 1536:2048]
        u = jnp.where((rows < 8) & (i == 0), 0.0, ac * ax)
        u1, u2 = _shift_down(u, 1), _shift_down(u, 2)
        w0, w1, w2 = cw_ref[0:1, :], cw_ref[1:2, :], cw_ref[2:3, :]
        cv = w0 * u2 + w1 * u1 + w2 * u
        dye = jnp.concatenate([jnp.zeros((8, 512), f32), dy_ref[...], dyn_ref[...]], axis=0)
        dye = jnp.where((rows >= tT + 8) & (i == nt - 1), 0.0, dye)
        sz = _silu(az)
        dcv = dye * ab * sz
        du = w2 * dcv + w1 * _shift_up(dcv, 1) + w0 * _shift_up(dcv, 2)
        inner = (rows >= 8) & (rows < tT + 8)
        dcv_in = jnp.where(inner, dcv, 0.0)
        dcw_ref[0:1, :] += jnp.sum(dcv_in * u2, axis=0, keepdims=True)
        dcw_ref[1:2, :] += jnp.sum(dcv_in * u1, axis=0, keepdims=True)
        dcw_ref[2:3, :] += jnp.sum(dcv_in * u, axis=0, keepdims=True)
        sl = slice(8, tT + 8)
        dp_ref[:, 0:512] = _b((dye * cv * sz)[sl])
        dp_ref[:, 512:1024] = _b((du * ax)[sl])
        dp_ref[:, 1024:1536] = _b((du * ac)[sl])
        dp_ref[:, 1536:2048] = _b((dye * ab * cv * _dsilu(az))[sl])
        dp_ref[:, 2048:] = jnp.zeros((tT, REG_A - 2048), bf16)

    return pl.pallas_call(
        body, name="mixa_bwd", grid=(nt,),
        in_specs=[pl.BlockSpec((tT, REG_A), lambda i: (i, OFF_A // REG_A)), prev_spec, next_spec,
                  pl.BlockSpec((3, 512), lambda i: (0, 0)),
                  pl.BlockSpec((tT, 512), lambda i: (i, 0)), dnext_spec, pl.BlockSpec(memory_space=pl.ANY)],
        out_specs=[pl.BlockSpec((tT, REG_A), lambda i: (i, OFF_A // REG_A)), pl.BlockSpec((8, 512), lambda i: (0, 0))],
        out_shape=[jax.ShapeDtypeStruct((T, NP), bf16), jax.ShapeDtypeStruct((8, 512), f32)],
        input_output_aliases={6: 0},
        compiler_params=_cp(("arbitrary",)),
    )(p, p, p, cw, dy, dy, dp)


def _l2n_fwd(y):
    return y * lax.rsqrt(jnp.sum(y * y, axis=-1, keepdims=True) + L2_EPS)


def mixc_pre_fwd(p, cw, alog_l, dtb_l):
    T = p.shape[0]
    tT = min(512, T)
    prev_spec, _ = _halo_specs(tT, T, REG_C, OFF_C // REG_C)

    def body(p_ref, pp_ref, cw_ref, al_ref, dt_ref, q_ref, k_ref, v_ref, sm_ref):
        pp = jnp.where(pl.program_id(0) == 0, 0.0, pp_ref[:, 0:2048])
        xe = jnp.concatenate([pp, p_ref[:, 0:2048]], axis=0)
        cv = (cw_ref[0:1, :] * _shift_down(xe, 3) + cw_ref[1:2, :] * _shift_down(xe, 2)
              + cw_ref[2:3, :] * _shift_down(xe, 1) + cw_ref[3:4, :] * xe)[8:]
        y = _silu(cv)
        for hh in range(4):
            sl = slice(hh * HK, (hh + 1) * HK)
            q_ref[:, sl] = _l2n_fwd(y[:, sl]) * QK_SCALE
            k_ref[:, sl] = _l2n_fwd(y[:, 512 + hh * HK:512 + (hh + 1) * HK])
        v_ref[...] = y[:, 1024:2048]
        ps = p_ref[:, 2048:2176]
        lane = lax.broadcasted_iota(jnp.int32, ps.shape, 1)
        la = -jnp.exp(al_ref[...]) * _softplus(ps + dt_ref[...])
        rin = lax.broadcasted_iota(jnp.int32, ps.shape, 0) % L
        g = _chunk_cumsum(la, rin)
        sm_ref[...] = jnp.where(lane < 8, _sigmoid(ps), jnp.where(lane < 16, g, 0.0))

    return pl.pallas_call(
        body, name="mixc_pre_fwd", grid=(T // tT,),
        in_specs=[pl.BlockSpec((tT, REG_C), lambda i: (i, OFF_C // REG_C)), prev_spec,
                  pl.BlockSpec((4, 2048), lambda i: (0, 0)),
                  pl.BlockSpec((1, 128), lambda i: (0, 0)), pl.BlockSpec((1, 128), lambda i: (0, 0))],
        out_specs=[pl.BlockSpec((tT, 512), lambda i: (i, 0)), pl.BlockSpec((tT, 512), lambda i: (i, 0)),
                   pl.BlockSpec((tT, 1024), lambda i: (i, 0)), pl.BlockSpec((tT, 128), lambda i: (i, 0))],
        out_shape=[jax.ShapeDtypeStruct((T, 512), f32), jax.ShapeDtypeStruct((T, 512), f32),
                   jax.ShapeDtypeStruct((T, 1024), f32), jax.ShapeDtypeStruct((T, 128), f32)],
        compiler_params=_cp(("parallel",)),
    )(p, p, cw, alog_l, dtb_l)


def mixc_pre_bwd(p, cw, alog_l, dtb_l, dq8, dk8, dv, dsm8, dp):
    T = p.shape[0]
    tT = min(256, T)
    nt = T // tT
    prev_spec, next_spec = _halo_specs(tT, T, REG_C, OFF_C // REG_C)
    _, n1024 = _halo_specs(tT, T, 1024, 0)

    def body(p_ref, pp_ref, pn_ref, cw_ref, al_ref, dt_ref, dq_ref, dqn_ref, dk_ref, dkn_ref,
             dv_ref, dvn_ref, dsm_ref, dp_in, dp_ref, dcw_ref, dsml_ref):
        i = pl.program_id(0)

        @pl.when(i == 0)
        def _():
            dcw_ref[...] = jnp.zeros_like(dcw_ref)
            dsml_ref[...] = jnp.zeros_like(dsml_ref)
        rows = lax.broadcasted_iota(jnp.int32, (tT + 16, 1), 0)
        pp = jnp.where(i == 0, 0.0, pp_ref[:, 0:2048])
        xe = jnp.concatenate([pp, p_ref[:, 0:2048], pn_ref[:, 0:2048]], axis=0)
        xs = [_shift_down(xe, 3), _shift_down(xe, 2), _shift_down(xe, 1), xe]
        cv = cw_ref[0:1, :] * xs[0] + cw_ref[1:2, :] * xs[1] + cw_ref[2:3, :] * xs[2] + cw_ref[3:4, :] * xs[3]
        y = _silu(cv)
        last = (rows >= tT + 8) & (i == nt - 1)
        z8q = jnp.zeros((8, 1024), f32)

        def ext(cur_ref, nxt_ref):
            return jnp.where(last, 0.0, jnp.concatenate([z8q, cur_ref[...], nxt_ref[...]], axis=0))
        dq8e, dk8e, dve = ext(dq_ref, dqn_ref), ext(dk_ref, dkn_ref), ext(dv_ref, dvn_ref)
        dys = []
        for (d8, base, scale) in ((dq8e, 0, QK_SCALE), (dk8e, 512, 1.0)):
            for hh in range(4):
                dn = (d8[:, (2 * hh) * HK:(2 * hh + 1) * HK] + d8[:, (2 * hh + 1) * HK:(2 * hh + 2) * HK]) * scale
                yh = y[:, base + hh * HK:base + (hh + 1) * HK]
                r = lax.rsqrt(jnp.sum(yh * yh, axis=-1, keepdims=True) + L2_EPS)
                nh = yh * r
                dys.append(r * (dn - nh * jnp.sum(dn * nh, axis=-1, keepdims=True)))
        dyy = jnp.concatenate(dys + [dve], axis=1)
        dcv = dyy * _dsilu(cv)
        dx = (cw_ref[3:4, :] * dcv + cw_ref[2:3, :] * _shift_up(dcv, 1) + cw_ref[1:2, :] * _shift_up(dcv, 2)
              + cw_ref[0:1, :] * _shift_up(dcv, 3))
        dp_ref[:, 0:2048] = _b(dx[8:tT + 8])
        dp_ref[:, 2176:] = jnp.zeros((tT, REG_C - 2176), bf16)
        inner = (rows >= 8) & (rows < tT + 8)
        dcv_in = jnp.where(inner, dcv, 0.0)
        for j in range(4):
            dcw_ref[j:j + 1, :] += jnp.sum(dcv_in * xs[j], axis=0, keepdims=True)
        ps = p_ref[:, 2048:2176]
        lane = lax.broadcasted_iota(jnp.int32, ps.shape, 1)
        dsm = dsm_ref[:, 0:128]
        for hh in range(1, N_GDN):
            dsm = dsm + dsm_ref[:, hh * 128:(hh + 1) * 128]
        beta = _sigmoid(ps)
        xa = ps + dt_ref[...]
        nea = -jnp.exp(al_ref[...])
        dpa = dsm * nea * _sigmoid(xa)
        dp_ref[:, 2048:2176] = _b(jnp.where(lane < 8, dsm * beta * (1.0 - beta), jnp.where(lane < 16, dpa, 0.0)))
        amask = (lane >= 8) & (lane < 16)
        dsml_ref[0:1, :] += jnp.sum(jnp.where(amask, dsm * nea * _softplus(xa), 0.0), axis=0, keepdims=True)
        dsml_ref[1:2, :] += jnp.sum(jnp.where(amask, dpa, 0.0), axis=0, keepdims=True)

    cur1024 = pl.BlockSpec((tT, 1024), lambda i: (i, 0))
    return pl.pallas_call(
        body, name="mixc_pre_bwd", grid=(nt,),
        in_specs=[pl.BlockSpec((tT, REG_C), lambda i: (i, OFF_C // REG_C)), prev_spec, next_spec,
                  pl.BlockSpec((4, 2048), lambda i: (0, 0)),
                  pl.BlockSpec((1, 128), lambda i: (0, 0)), pl.BlockSpec((1, 128), lambda i: (0, 0)),
                  cur1024, n1024, cur1024, n1024, cur1024, n1024, cur1024, pl.BlockSpec(memory_space=pl.ANY)],
        out_specs=[pl.BlockSpec((tT, REG_C), lambda i: (i, OFF_C // REG_C)),
                   pl.BlockSpec((8, 2048), lambda i: (0, 0)), pl.BlockSpec((8, 128), lambda i: (0, 0))],
        out_shape=[jax.ShapeDtypeStruct((T, NP), bf16),
                   jax.ShapeDtypeStruct((8, 2048), f32), jax.ShapeDtypeStruct((8, 128), f32)],
        input_output_aliases={13: 0},
        compiler_params=_cp(("arbitrary",)),
    )(p, p, p, cw, alog_l, dtb_l, dq8, dq8, dk8, dk8, dv, dv, dsm8, dp)


def _tri_inverse(m):
    r = lax.broadcasted_iota(jnp.int32, (L, L), 0)
    c = lax.broadcasted_iota(jnp.int32, (L, L), 1)
    eye = (r == c).astype(f32)[None]
    same = lambda w: ((r // w) == (c // w))[None]
    md = jnp.where(same(8), m, 0.0)
    m2 = _bdot_split(md, md)
    m4 = _bdot_split(m2, m2)
    t = _bdot_split(_bdot_split(eye - md, eye + m2), eye + m4)
    for w in (16, 32, 64):
        mo = jnp.where(same(w) & jnp.logical_not(same(w // 2)), m, 0.0)
        t = t - _bdot_split(_bdot_split(t, mo), t)
    return t


def _col_to_row(col, eye):
    return jnp.sum(eye * col, axis=1, keepdims=True)


def _row_to_col(row, eye):
    return jnp.sum(eye * row, axis=2, keepdims=True)


def _gdn_chunk_terms(q, k, v, beta, g, t_inv=None):
    r = lax.broadcasted_iota(jnp.int32, (L, L), 0)
    c = lax.broadcasted_iota(jnp.int32, (L, L), 1)
    eye = (r == c).astype(f32)[None]
    causal, strict = (c <= r)[None], (c < r)[None]
    diff = g - _col_to_row(g, eye)
    dec = jnp.exp(jnp.where(causal, diff, 0.0))
    dc = jnp.where(causal, dec, 0.0)
    ds = jnp.where(strict, dec, 0.0)
    eg = jnp.exp(g)
    gl = g[:, L - 1:L, :]
    egl = jnp.exp(gl - g)
    kb = k * beta
    kk = _bdot_nt(_b(k), _b(kb))
    qk = _bdot_nt(_b(q), _b(kb))
    m = kk * ds
    aqk = qk * dc
    if t_inv is None:
        t_inv = _tri_inverse(m)
    tb = _b(t_inv)
    keg = k * eg
    u = _bdot(tb, _b(v))
    w = _bdot(tb, _b(keg))
    ks = kb * egl
    ksw = _bdot_tn(_b(ks), _b(w))
    return dict(eye=eye, causal=causal, strict=strict, dc=dc, ds=ds, eg=eg, gl=gl, egl=egl, kb=kb, kk=kk, qk=qk,
                m=m, aqk=aqk, t=t_inv, u=u, w=w, qi=q * eg, ks=ks, keg=keg, ksw=ksw)


def gdn_fwd(qn, kn, vv, sm):
    T = qn.shape[0]
    tB = min(GDN_BLOCK, T)
    nc = tB // L
    N = T // L

    def body(q_ref, k_ref, v_ref, sm_ref, o_ref, st_ref, ti_ref, s_scr):
        h = pl.program_id(0)

        @pl.when(pl.program_id(1) == 0)
        def _():
            s_scr[...] = jnp.zeros_like(s_scr)
        smv = sm_ref[...]
        lane = lax.broadcasted_iota(jnp.int32, smv.shape, 1)
        beta = jnp.sum(jnp.where(lane == h, smv, 0.0), axis=1, keepdims=True).reshape(nc, L, 1)
        g = jnp.sum(jnp.where(lane == 8 + h, smv, 0.0), axis=1, keepdims=True).reshape(nc, L, 1)
        q = q_ref[...].reshape(nc, L, HK)
        k = k_ref[...].reshape(nc, L, HK)
        v = v_ref[...].reshape(nc, L, HK)
        tm = _gdn_chunk_terms(q, k, v, beta, g)
        ti_ref[...] = tm["t"]
        ksu = _bdot_tn(_b(tm["ks"]), _b(tm["u"]))
        kswb = _b(tm["ksw"])
        egl_last = jnp.exp(tm["gl"])
        s = s_scr[...]
        states = [None] * nc
        for ci in range(nc):
            states[ci] = s
            s = egl_last[ci] * s + (ksu[ci] - _dot(kswb[ci], _b(s)))
        s_scr[...] = s
        sall = jnp.stack(states, axis=0)
        st_ref[...] = sall
        sb = _b(sall)
        e = tm["u"] - _bdot(_b(tm["w"]), sb)
        o = _bdot(_b(tm["qi"]), sb) + _bdot(_b(tm["aqk"]), _b(e))
        o_ref[...] = o.reshape(tB, HK)

    return pl.pallas_call(
        body, name="gdn_fwd", grid=(N_GDN, T // tB),
        in_specs=[pl.BlockSpec((tB, HK), lambda h, n: (n, h // 2)), pl.BlockSpec((tB, HK), lambda h, n: (n, h // 2)),
                  pl.BlockSpec((tB, HK), lambda h, n: (n, h)), pl.BlockSpec((tB, 128), lambda h, n: (n, 0))],
        out_specs=[pl.BlockSpec((tB, HK), lambda h, n: (n, h)),
                   pl.BlockSpec((None, nc, HK, HK), lambda h, n: (h, n, 0, 0)),
                   pl.BlockSpec((None, nc, L, L), lambda h, n: (h, n, 0, 0))],
        out_shape=[jax.ShapeDtypeStruct((T, N_GDN * HK), f32), jax.ShapeDtypeStruct((N_GDN, N, HK, HK), f32),
                   jax.ShapeDtypeStruct((N_GDN, N, L, L), f32)],
        scratch_shapes=[pltpu.VMEM((HK, HK), f32)],
        compiler_params=_cp(("parallel", "arbitrary")),
    )(qn, kn, vv, sm)


def gdn_bwd(qn, kn, vv, sm, st, ti, do):
    T = qn.shape[0]
    tB = min(GDN_BLOCK, T)
    nc = tB // L
    nb = T // tB

    def body(q_ref, k_ref, v_ref, sm_ref, st_ref, ti_ref, do_ref, dq_ref, dk_ref, dv_ref, dsm_ref, ds_scr):
        h = pl.program_id(0)

        @pl.when(pl.program_id(1) == 0)
        def _():
            ds_scr[...] = jnp.zeros_like(ds_scr)
        smv = sm_ref[...]
        lane = lax.broadcasted_iota(jnp.int32, smv.shape, 1)
        beta = jnp.sum(jnp.where(lane == h, smv, 0.0), axis=1, keepdims=True).reshape(nc, L, 1)
        g = jnp.sum(jnp.where(lane == 8 + h, smv, 0.0), axis=1, keepdims=True).reshape(nc, L, 1)
        q = q_ref[...].reshape(nc, L, HK)
        k = k_ref[...].reshape(nc, L, HK)
        v = v_ref[...].reshape(nc, L, HK)
        do = do_ref[...].reshape(nc, L, HK)
        s = st_ref[...]
        tm = _gdn_chunk_terms(q, k, v, beta, g, t_inv=ti_ref[...])
        eye, dc, ds_, eg, egl = tm["eye"], tm["dc"], tm["ds"], tm["eg"], tm["egl"]
        kb, u, w, qi, ks, tinv = tm["kb"], tm["u"], tm["w"], tm["qi"], tm["ks"], tm["t"]
        sb, dob = _b(s), _b(do)
        e = u - _bdot(_b(w), sb)
        eb = _b(e)
        egl_last = jnp.exp(tm["gl"])
        de0 = _bdot_tn(_b(tm["aqk"]), dob)
        ds0 = _bdot_tn(_b(qi), dob) - _bdot_tn(_b(w), _b(de0))
        kswb = _b(tm["ksw"])
        dsn = ds_scr[...]
        dsns = [None] * nc
        for ci in reversed(range(nc)):
            dsns[ci] = dsn
            dsn = ds0[ci] + (egl_last[ci] * dsn - _dot_tn(kswb[ci], _b(dsn)))
        ds_scr[...] = dsn
        dsp = jnp.stack(dsns, axis=0)
        dspb = _b(dsp)
        de = de0 + _bdot(_b(ks), dspb)
        deb = _b(de)
        dks = _bdot_nt(eb, dspb)
        dqi = _bdot_nt(dob, sb)
        daqk = jnp.where(tm["causal"], _bdot_nt(dob, eb), 0.0)
        dw = -_bdot_nt(deb, sb)
        tb = _b(tinv)
        dvv = _bdot_tn(tb, deb)
        dkg = _bdot_tn(tb, _b(dw))
        dm = -jnp.where(tm["strict"], _bdot_nt(_b(dvv), _b(u)) + _bdot_nt(_b(dkg), _b(w)), 0.0)
        x = _b(dm * ds_)
        y = _b(daqk * dc)
        kbb, kbf, qbf = _b(kb), _b(k), _b(q)
        dk = _bdot(x, kbb) + dkg * eg
        dkb = _bdot_tn(x, kbf) + _bdot_tn(y, qbf) + dks * egl
        dq = _bdot(y, kbb) + dqi * eg
        dk = dk + dkb * beta
        dbeta = jnp.sum(dkb * k, axis=-1, keepdims=True)
        z = dm * tm["m"] + daqk * tm["aqk"]
        dg = (jnp.sum(dqi * qi - dks * ks + dkg * tm["keg"], axis=-1, keepdims=True)
              + jnp.sum(z, axis=-1, keepdims=True) - _row_to_col(jnp.sum(z, axis=1, keepdims=True), eye))
        dgl = (egl_last * jnp.sum(jnp.sum(s * dsp, axis=2, keepdims=True), axis=1, keepdims=True)
               + jnp.sum(jnp.sum(dks * ks, axis=2, keepdims=True), axis=1, keepdims=True))
        rowi = lax.broadcasted_iota(jnp.int32, (nc, L, 1), 1)
        dg = dg + jnp.where(rowi == L - 1, dgl, 0.0)
        dg2 = dg.reshape(tB, 1)
        rin = lax.broadcasted_iota(jnp.int32, (tB, 1), 0) % L
        dla = _chunk_rev_cumsum(jnp.broadcast_to(dg2, (tB, 128)), jnp.broadcast_to(rin, (tB, 128)))
        dq_ref[...] = dq.reshape(tB, HK)
        dk_ref[...] = dk.reshape(tB, HK)
        dv_ref[...] = dvv.reshape(tB, HK)
        dsm_ref[...] = jnp.where(lane == h, dbeta.reshape(tB, 1), jnp.where(lane == 8 + h, dla, 0.0))

    rev = lambda n: nb - 1 - n
    return pl.pallas_call(
        body, name="gdn_bwd", grid=(N_GDN, nb),
        in_specs=[pl.BlockSpec((tB, HK), lambda h, n: (rev(n), h // 2)),
                  pl.BlockSpec((tB, HK), lambda h, n: (rev(n), h // 2)),
                  pl.BlockSpec((tB, HK), lambda h, n: (rev(n), h)), pl.BlockSpec((tB, 128), lambda h, n: (rev(n), 0)),
                  pl.BlockSpec((None, nc, HK, HK), lambda h, n: (h, rev(n), 0, 0)),
                  pl.BlockSpec((None, nc, L, L), lambda h, n: (h, rev(n), 0, 0)),
                  pl.BlockSpec((tB, HK), lambda h, n: (rev(n), h))],
        out_specs=[pl.BlockSpec((tB, HK), lambda h, n: (rev(n), h))] * 4,
        out_shape=[jax.ShapeDtypeStruct((T, N_GDN * HK), f32)] * 4,
        scratch_shapes=[pltpu.VMEM((HK, HK), f32)],
        compiler_params=_cp(("parallel", "arbitrary")),
    )(qn, kn, vv, sm, st, ti, do)


def _hgrn_prep(bq, bf_, bi, lb):
    tB = bq.shape[0]
    q = _silu(bq) * QK_SCALE
    sg = _sigmoid(bf_)
    f = lb + (1.0 - lb) * sg
    logf = jnp.log(jnp.maximum(f, MIN_F))
    rin = lax.broadcasted_iota(jnp.int32, (tB, HK), 0) % L
    g = _chunk_cumsum(logf, rin)
    return q, sg, f, 1.0 - f, bi, g, rin


def _hgrn_intra(q, kk, v, g, do=None):
    n = q.shape[0]
    nsub = L // SUB
    bwd = do is not None
    o_rows = [None] * nsub
    if bwd:
        dq_rows = [None] * nsub
        dkk_acc = jnp.zeros_like(kk)
        dv_acc = jnp.zeros_like(v)
    for i in range(1, nsub):
        lo, hi, w = i * SUB, (i + 1) * SUB, i * SUB
        ref = g[:, lo - 1:lo, :]
        eq = jnp.exp(g[:, lo:hi, :] - ref)
        ek = jnp.exp(ref - g[:, :w, :])
        qs = _b(q[:, lo:hi, :] * eq)
        ks = _b(kk[:, :w, :] * ek)
        p = _bdot_nt(qs, ks)
        o_rows[i] = _bdot(_b(p), _b(v[:, :w, :]))
        if bwd:
            dob = _b(do[:, lo:hi, :])
            dp = _b(_bdot_nt(dob, _b(v[:, :w, :])))
            dq_rows[i] = _bdot(dp, ks) * eq
            pad = jnp.zeros((n, L - w, HK), f32)
            dkk_acc = dkk_acc + jnp.concatenate([_bdot_tn(dp, qs) * ek, pad], axis=1)
            dv_acc = dv_acc + jnp.concatenate([_bdot_tn(_b(p), dob), pad], axis=1)
    m = n * nsub
    q4, k4, v4, g4 = (a.reshape(m, SUB, HK) for a in (q, kk, v, g))
    r = lax.broadcasted_iota(jnp.int32, (m, SUB, HK), 1)
    od = jnp.zeros((m, SUB, HK), f32)
    if bwd:
        do4 = do.reshape(m, SUB, HK)
        dqd = jnp.zeros((m, SUB, HK), f32)
        dkd = jnp.zeros((m, SUB, HK), f32)
        dvd = jnp.zeros((m, SUB, HK), f32)
    for j in range(SUB):
        gj, kj, vj = g4[:, j:j + 1, :], k4[:, j:j + 1, :], v4[:, j:j + 1, :]
        ok = r >= j
        e = jnp.where(ok, jnp.exp(jnp.where(ok, g4 - gj, 0.0)), 0.0)
        xq = q4 * e
        pj = jnp.sum(xq * kj, axis=-1, keepdims=True)
        od = od + pj * vj
        if bwd:
            dpj = jnp.sum(do4 * vj, axis=-1, keepdims=True)
            dqd = dqd + dpj * kj * e
            dkd = dkd + jnp.where(r == j, jnp.sum(dpj * xq, axis=1, keepdims=True), 0.0)
            dvd = dvd + jnp.where(r == j, jnp.sum(pj * do4, axis=1, keepdims=True), 0.0)
    od = od.reshape(n, L, HK)
    o = jnp.concatenate([od[:, :SUB, :]] + [od[:, i * SUB:(i + 1) * SUB, :] + o_rows[i] for i in range(1, nsub)], axis=1)
    if not bwd:
        return o
    dqd = dqd.reshape(n, L, HK)
    dq = jnp.concatenate([dqd[:, :SUB, :]] + [dqd[:, i * SUB:(i + 1) * SUB, :] + dq_rows[i] for i in range(1, nsub)], axis=1)
    return o, dq, dkk_acc + dkd.reshape(n, L, HK), dv_acc + dvd.reshape(n, L, HK)


def hgrn_fwd(p, lbs):
    T = p.shape[0]
    tB = min(256, T)
    nc = tB // L
    N = T // L

    def body(b_ref, lb_ref, o_ref, st_ref, s_scr):
        @pl.when(pl.program_id(1) == 0)
        def _():
            s_scr[...] = jnp.zeros_like(s_scr)
        q, sg, f, kk, v, g, rin = _hgrn_prep(b_ref[:, 0:HK], b_ref[:, HK:2 * HK], b_ref[:, 2 * HK:3 * HK], lb_ref[...])
        q3, k3, v3, g3 = (a.reshape(nc, L, HK) for a in (q, kk, v, g))
        o = _hgrn_intra(q3, k3, v3, g3)
        gl = g3[:, L - 1:L, :]
        qt = _b(q3 * jnp.exp(g3))
        kt = _b(k3 * jnp.exp(gl - g3))
        vb = _b(v3)
        st = s_scr[...]
        for c in range(nc):
            st_ref[c] = st
            o_ref[c * L:(c + 1) * L, :] = o[c] + _dot_nt(qt[c], _b(st))
            st = st * jnp.exp(gl[c]) + _dot_tn(vb[c], kt[c])
        s_scr[...] = st

    return pl.pallas_call(
        body, name="hgrn_fwd", grid=(N_HGRN, T // tB),
        in_specs=[pl.BlockSpec((tB, REG_BH), lambda h, n: (n, OFF_B // REG_BH + h)),
                  pl.BlockSpec((1, HK), lambda h, n: (0, h))],
        out_specs=[pl.BlockSpec((tB, HK), lambda h, n: (n, h)),
                   pl.BlockSpec((None, nc, HK, HK), lambda h, n: (h, n, 0, 0))],
        out_shape=[jax.ShapeDtypeStruct((T, N_HGRN * HK), f32), jax.ShapeDtypeStruct((N_HGRN, N, HK, HK), f32)],
        scratch_shapes=[pltpu.VMEM((HK, HK), f32)],
        compiler_params=_cp(("parallel", "arbitrary")),
    )(p, lbs)


def hgrn_bwd(p, lbs, st, do, dp):
    T = p.shape[0]
    tB = min(256, T)
    nc = tB // L
    nb = T // tB

    def body(b_ref, lb_ref, st_ref, do_ref, dp_in, dp_ref, dlb_ref, ds_scr):
        @pl.when(pl.program_id(1) == 0)
        def _():
            ds_scr[...] = jnp.zeros_like(ds_scr)
            dlb_ref[...] = jnp.zeros_like(dlb_ref)
        lb = lb_ref[...]
        bq = b_ref[:, 0:HK]
        q, sg, f, kk, v, g, rin = _hgrn_prep(bq, b_ref[:, HK:2 * HK], b_ref[:, 2 * HK:3 * HK], lb)
        q3, k3, v3, g3 = (a.reshape(nc, L, HK) for a in (q, kk, v, g))
        do3 = do_ref[...].reshape(nc, L, HK)
        dob = _b(do3)
        gl = g3[:, L - 1:L, :]
        egl = jnp.exp(gl)
        eg, egr = jnp.exp(g3), jnp.exp(gl - g3)
        qt, kt = q3 * eg, k3 * egr
        s = st_ref[...]
        ds0 = _bdot_tn(dob, _b(qt))
        dsn = ds_scr[...]
        dsns = [None] * nc
        for c in reversed(range(nc)):
            dsns[c] = dsn
            dsn = ds0[c] + dsn * egl[c]
        ds_scr[...] = dsn
        dsp = jnp.stack(dsns, axis=0)
        dspb = _b(dsp)
        dqt = _bdot(dob, _b(s))
        dkt = _bdot(_b(v3), dspb)
        dv_state = _bdot_nt(_b(kt), dspb)
        dgl = egl * jnp.sum(s * dsp, axis=1, keepdims=True) + jnp.sum(dkt * kt, axis=1, keepdims=True)
        _, dq_i, dkk_i, dv_i = _hgrn_intra(q3, k3, v3, g3, do=do3)
        dq = dq_i + dqt * eg
        dkk = dkk_i + dkt * egr
        dv = dv_i + dv_state
        rowi = lax.broadcasted_iota(jnp.int32, (nc, L, HK), 1)
        dg = q3 * dq - k3 * dkk + jnp.where(rowi == L - 1, dgl, 0.0)
        dlogf = _chunk_rev_cumsum(dg.reshape(tB, HK), rin)
        dkk2 = dkk.reshape(tB, HK)
        df = jnp.where(f > MIN_F, dlogf / f, 0.0) - dkk2
        dlb_ref[...] += jnp.sum(df * (1.0 - sg), axis=0, keepdims=True)
        dp_ref[:, 0:HK] = _b(dq.reshape(tB, HK) * QK_SCALE * _dsilu(bq))
        dp_ref[:, HK:2 * HK] = _b(df * (1.0 - lb) * sg * (1.0 - sg))
        dp_ref[:, 2 * HK:3 * HK] = _b(dv.reshape(tB, HK))

    rev = lambda n: nb - 1 - n
    bspec = pl.BlockSpec((tB, REG_BH), lambda h, n: (rev(n), OFF_B // REG_BH + h))
    return pl.pallas_call(
        body, name="hgrn_bwd", grid=(N_HGRN, nb),
        in_specs=[bspec, pl.BlockSpec((1, HK), lambda h, n: (0, h)),
                  pl.BlockSpec((None, nc, HK, HK), lambda h, n: (h, rev(n), 0, 0)),
                  pl.BlockSpec((tB, HK), lambda h, n: (rev(n), h)), pl.BlockSpec(memory_space=pl.ANY)],
        out_specs=[bspec, pl.BlockSpec((1, HK), lambda h, n: (0, h))],
        out_shape=[jax.ShapeDtypeStruct((T, NP), bf16), jax.ShapeDtypeStruct((1, N_HGRN * HK), f32)],
        input_output_aliases={4: 0},
        scratch_shapes=[pltpu.VMEM((HK, HK), f32)],
        compiler_params=_cp(("parallel", "arbitrary")),
    )(p, lbs, st, do, dp)


def _headnorm_fwd(o, z, w, nheads):
    outs = []
    for hh in range(nheads):
        sl = slice(hh * HK, (hh + 1) * HK)
        oh = o[:, sl]
        r = lax.rsqrt(jnp.mean(oh * oh, axis=-1, keepdims=True) + NORM_EPS)
        outs.append(oh * r * w * _silu(z[:, sl]))
    return jnp.concatenate(outs, axis=1)


def _headnorm_bwd(o, z, w, dy, nheads):
    dos, dzs = [], []
    dw = jnp.zeros((1, HK), f32)
    for hh in range(nheads):
        sl = slice(hh * HK, (hh + 1) * HK)
        oh, zh, dyh = o[:, sl], z[:, sl], dy[:, sl]
        r = lax.rsqrt(jnp.mean(oh * oh, axis=-1, keepdims=True) + NORM_EPS)
        on = oh * r
        sz = _silu(zh)
        dn = dyh * sz * w
        dos.append(r * (dn - on * jnp.mean(dn * on, axis=-1, keepdims=True)))
        dzs.append(dyh * on * w * _dsilu(zh))
        dw = dw + jnp.sum(dyh * sz * on, axis=0, keepdims=True)
    return jnp.concatenate(dos, axis=1), jnp.concatenate(dzs, axis=1), dw


def _merge_specs(tT, l):
    row = lambda w, cb=0: pl.BlockSpec((tT, w), lambda i, cb=cb: (i, cb))
    full = lambda r, c: pl.BlockSpec((r, c), lambda i: (0, 0))
    layer = lambda r, c: pl.BlockSpec((None, r, c), lambda i: (l, 0, 0))
    return row, full, layer


def merge_fwd(x, p, ya, ob, oc, hw, gw, bg, woa, wob, woc, wo, l):
    T = x.shape[0]
    tT = min(256, T)
    row, full, layer = _merge_specs(tT, l)

    def body(x_ref, pm_ref, ya_ref, ob_ref, oc_ref, hw_ref, gw_ref, bg_ref,
             woa_ref, wob_ref, woc_ref, wo_ref, out_ref):
        yb = _b(_headnorm_fwd(ob_ref[...], pm_ref[:, M_BZ:M_G], hw_ref[...], N_HGRN))
        yc = _b(_headnorm_fwd(oc_ref[...], pm_ref[:, M_CZ:REG_M], gw_ref[...], N_GDN))
        gates = _sigmoid(pm_ref[:, M_G:M_CZ] + bg_ref[...])
        merged = (gates[:, 0:D] * _dot(ya_ref[...], woa_ref[...]) + gates[:, D:2 * D] * _dot(yb, wob_ref[...])
                  + gates[:, 2 * D:3 * D] * _dot(yc, woc_ref[...]))
        out_ref[...] = x_ref[...] + _dot(_b(merged), wo_ref[...])

    return pl.pallas_call(
        body, name="merge_fwd", grid=(T // tT,),
        in_specs=[row(D), row(REG_M, OFF_M // REG_M),
                  row(512), row(512), row(1024), full(1, HK), full(1, HK), full(1, 3 * D),
                  layer(512, D), layer(512, D), layer(D, D), layer(D, D)],
        out_specs=row(D),
        out_shape=jax.ShapeDtypeStruct((T, D), f32),
        compiler_params=_cp(("parallel",)),
    )(x, p, ya, ob, oc, hw, gw, bg, woa, wob, woc, wo)


def merge_bwd(dxo, p, ya, ob, oc, hw, gw, bg, woa, wob, woc, wo, l):
    T = dxo.shape[0]
    tT = min(256, T)
    row, full, layer = _merge_specs(tT, l)

    def body(dx_ref, pm_ref, ya_ref, ob_ref, oc_ref, hw_ref, gw_ref, bg_ref,
             woa_ref, wob_ref, woc_ref, wo_ref,
             dya_ref, dob_ref, doc_ref, dp_ref, mg_ref, dy3_ref, yb_ref, yc_ref,
             dbg_ref, dhw_ref, dgw_ref):
        @pl.when(pl.program_id(0) == 0)
        def _():
            dbg_ref[...] = jnp.zeros_like(dbg_ref)
            dhw_ref[...] = jnp.zeros_like(dhw_ref)
            dgw_ref[...] = jnp.zeros_like(dgw_ref)
        ob, oc, bz, cz = ob_ref[...], oc_ref[...], pm_ref[:, M_BZ:M_G], pm_ref[:, M_CZ:REG_M]
        hw_, gw_ = hw_ref[...], gw_ref[...]
        yb = _b(_headnorm_fwd(ob, bz, hw_, N_HGRN))
        yc = _b(_headnorm_fwd(oc, cz, gw_, N_GDN))
        yb_ref[...] = yb
        yc_ref[...] = yc
        gates = _sigmoid(pm_ref[:, M_G:M_CZ] + bg_ref[...])
        ys = (_dot(ya_ref[...], woa_ref[...]), _dot(yb, wob_ref[...]), _dot(yc, woc_ref[...]))
        dmerged = _dot_nt(_b(dx_ref[...]), wo_ref[...])
        merged = jnp.zeros_like(dmerged)
        dys = []
        for i in range(3):
            gi = gates[:, i * D:(i + 1) * D]
            merged = merged + gi * ys[i]
            dyi = _b(dmerged * gi)
            dys.append(dyi)
            dy3_ref[:, i * D:(i + 1) * D] = dyi
            dgp = dmerged * ys[i] * gi * (1.0 - gi)
            dp_ref[:, M_G + i * D:M_G + (i + 1) * D] = _b(dgp)
            dbg_ref[:, i * D:(i + 1) * D] += jnp.sum(dgp, axis=0, keepdims=True)
        mg_ref[...] = _b(merged)
        dya_ref[...] = _dot_nt(dys[0], woa_ref[...])
        dob, dbz, dhw = _headnorm_bwd(ob, bz, hw_, _dot_nt(dys[1], wob_ref[...]), N_HGRN)
        doc, dcz, dgw = _headnorm_bwd(oc, cz, gw_, _dot_nt(dys[2], woc_ref[...]), N_GDN)
        dob_ref[...] = dob
        doc_ref[...] = doc
        dp_ref[:, M_BZ:M_G] = _b(dbz)
        dp_ref[:, M_CZ:REG_M] = _b(dcz)
        dhw_ref[...] += dhw
        dgw_ref[...] += dgw

    sd = jax.ShapeDtypeStruct
    return pl.pallas_call(
        body, name="merge_bwd", grid=(T // tT,),
        in_specs=[row(D), row(REG_M, OFF_M // REG_M),
                  row(512), row(512), row(1024), full(1, HK), full(1, HK), full(1, 3 * D),
                  layer(512, D), layer(512, D), layer(D, D), layer(D, D)],
        out_specs=[row(512), row(512), row(1024), row(REG_M, OFF_M // REG_M), row(D), row(3 * D), row(512),
                   row(1024), full(1, 3 * D), full(1, HK), full(1, HK)],
        out_shape=[sd((T, 512), f32), sd((T, 512), f32), sd((T, 1024), f32), sd((T, NP), bf16),
                   sd((T, D), bf16), sd((T, 3 * D), bf16), sd((T, 512), bf16),
                   sd((T, 1024), bf16), sd((1, 3 * D), f32), sd((1, HK), f32), sd((1, HK), f32)],
        compiler_params=_cp(("arbitrary",)),
    )(dxo, p, ya, ob, oc, hw, gw, bg, woa, wob, woc, wo)


def layer_fwd(x, w):
    l = w["l"]
    p, h = inproj_fwd(x, w["norm_w"], w["w_in"], l)
    ya = mixa_fwd(p, w["conv_a"])
    qn, kn, vv, sm = mixc_pre_fwd(p, w["conv_c"], w["alog_l"], w["dtb_l"])
    oc, st_c, ti = gdn_fwd(qn, kn, vv, sm)
    ob, st_b = hgrn_fwd(p, w["lbs"])
    xo = merge_fwd(x, p, ya, ob, oc, w["hgrn_norm_w"], w["gdn_norm_w"], w["b_gate"],
                   w["w_out_a"], w["w_out_b"], w["w_out_c"], w["w_o"], l)
    saved = dict(x=x, p=p, h=h, ya=ya, qn=qn, kn=kn, vv=vv, sm=sm, oc=oc, st_c=st_c, ti=ti, ob=ob, st_b=st_b)
    return xo, saved


OUT_MATS = (("w_out_a", "cols"), ("w_out_b", "cols"), ("w_out_c", "rows"), ("w_o", "rows"))


def layer_bwd(dxo, w, s, chip):
    p, l = s["p"], w["l"]
    (dya, dob, doc, dp, merged, dy3, yb, yc, dbg, dhw, dgw) = merge_bwd(
        dxo, p, s["ya"], s["ob"], s["oc"], w["hgrn_norm_w"], w["gdn_norm_w"], w["b_gate"],
        w["w_out_a"], w["w_out_b"], w["w_out_c"], w["w_o"], l)
    full = {"w_o": matmul_tn(merged, dxo, "dw_o", with_bf16=True),
            "w_out_a": matmul_tn(s["ya"], dy3, "dw_out_a", n=D, b_col0=0, with_bf16=True),
            "w_out_b": matmul_tn(yb, dy3, "dw_out_b", n=D, b_col0=D, with_bf16=True),
            "w_out_c": matmul_tn(yc, dy3, "dw_out_c", n=D, b_col0=2 * D, with_bf16=True)}
    out_kinds = [k for _, k in OUT_MATS]
    sent_out, token = exchange_start([full[n][1] for n, _ in OUT_MATS], out_kinds, f"grads_out_start{l}")
    dp, dlbs = hgrn_bwd(p, w["lbs"] + token[0:1, 0:1], s["st_b"], dob, dp)
    dq8, dk8, dvv, dsm8 = gdn_bwd(s["qn"], s["kn"], s["vv"], s["sm"], s["st_c"], s["ti"], doc)
    dp, dcc, dsmall = mixc_pre_bwd(p, w["conv_c"], w["alog_l"], w["dtb_l"], dq8, dk8, dvv, dsm8, dp)
    dp, dca = mixa_bwd(p, w["conv_a"], dya, dp)
    gf_win, gb_win = win_from_padded(matmul_tn(s["h"], dp, "dw_in"))
    sent_in, token = exchange_start([gb_win], ["slot"], f"grads_in_start{l}")
    dx, dnw = inproj_bwd(dp, w["w_in"], l, s["x"], w["norm_w"] + token[0:1, 0:1], dxo)
    recv_out = exchange_wait(sent_out, out_kinds, dx, f"grads_out_wait{l}")
    recv_in = exchange_wait(sent_in, ["slot"], dx, f"grads_in_wait{l}")
    half = {"w_in": partial_sum(gf_win, "slot", recv_in[0], chip, "psum_w_in", transposed=True)}
    for (n, kind), r in zip(OUT_MATS, recv_out):
        half[n] = partial_sum(full[n][0], kind, r, chip, "psum_" + n)
    small = dict(norm_w=dnw, b_gate=dbg, hgrn_norm_w=dhw, gdn_norm_w=dgw, lbs=dlbs, conv_a=dca[0:3], conv_c=dcc[0:4],
                 a_log=dsmall[0:1, 8:16], dt_bias=dsmall[1:2, 8:16])
    return dx, small, half


def lbs_fwd(lb):
    def body(lb_ref, o_ref):
        l0, l1 = lb_ref[0:1, :], lb_ref[1:2, :]
        mx = jnp.maximum(l0, l1)
        e0, e1 = jnp.exp(l0 - mx), jnp.exp(l1 - mx)
        o_ref[0:1, :] = jnp.zeros_like(l0)
        o_ref[1:2, :] = e1 / (e0 + e1)
    return pl.pallas_call(body, name="lbs_fwd", out_shape=jax.ShapeDtypeStruct(lb.shape, f32))(lb)


def _adam_math(w, g, m, v):
    mn = ADAM_B1 * m + (1.0 - ADAM_B1) * g
    vn = ADAM_B2 * v + (1.0 - ADAM_B2) * (g * g)
    mh = mn / (1.0 - ADAM_B1 ** ADAM_STEP)
    vh = vn / (1.0 - ADAM_B2 ** ADAM_STEP)
    return -ADAM_LR * (mh / (jnp.sqrt(vh) + ADAM_EPS) + ADAM_WD * w), mn, vn


def adam(w, g, m, v, name):
    R, C = w.shape
    tr = 256 if R % 256 == 0 else R

    def body(w_ref, g_ref, m_ref, v_ref, d_ref, mo_ref, vo_ref):
        d, mn, vn = _adam_math(w_ref[...], g_ref[...], m_ref[...], v_ref[...])
        d_ref[...] = d
        mo_ref[...] = mn
        vo_ref[...] = vn

    spec = pl.BlockSpec((tr, C), lambda i: (i, 0))
    return pl.pallas_call(
        body, name=name, grid=(R // tr,), in_specs=[spec] * 4, out_specs=[spec] * 3,
        out_shape=[jax.ShapeDtypeStruct((R, C), f32)] * 3, compiler_params=_cp(("parallel",)),
    )(w, g, m, v)


def adam_pair(h, hs, w, m, v, name):
    _, R, C = w.shape
    cp = h[0].shape[1]
    tr = 128 if R % 128 == 0 else R
    nt = R // tr

    def body(h0_ref, h1_ref, s0_ref, s1_ref, w_ref, m_ref, v_ref, g_ref, d_ref, mo_ref, vo_ref):
        def update(h_ref, s_ref):
            g = (h_ref[...] + s_ref[...])[:, :C]
            d, mn, vn = _adam_math(w_ref[...], g, m_ref[...], v_ref[...])
            g_ref[...] = g
            d_ref[...] = d
            mo_ref[...] = mn
            vo_ref[...] = vn

        @pl.when(pl.program_id(0) == 0)
        def _():
            update(h0_ref, s0_ref)

        @pl.when(pl.program_id(0) == 1)
        def _():
            update(h1_ref, s1_ref)

    h0spec = pl.BlockSpec((tr, cp), lambda l, i: (jnp.where(l == 0, i, nt - 1), 0))
    h1spec = pl.BlockSpec((tr, cp), lambda l, i: (jnp.where(l == 1, i, 0), 0))
    spec = pl.BlockSpec((None, tr, C), lambda l, i: (l, i, 0))
    return pl.pallas_call(
        body, name=name, grid=(2, nt), in_specs=[h0spec, h1spec, h0spec, h1spec, spec, spec, spec],
        out_specs=[spec] * 4, out_shape=[jax.ShapeDtypeStruct(w.shape, f32)] * 4,
        compiler_params=_cp(("arbitrary", "arbitrary")),
    )(h[0], h[1], hs[0], hs[1], w, m, v)


_SMALL = (("norm_w", 2 * D), ("b_gate", 6 * D), ("lower_bounds", None), ("hgrn_norm_w", 2 * HK),
          ("gdn_norm_w", 2 * HK), ("a_log", 16), ("dt_bias", 16), ("final_norm_w", D), ("loss", None))
_CONV = (("conv_a", 2 * 3 * 512), ("conv_c", 2 * 4 * 2048))


def _small_rows(n):
    return 16 if n is None else -(-n // 1024) * 8


LB_ROW = sum(_small_rows(n) for _, n in _SMALL[:2])
ADAM_ROWS = sum(_small_rows(n) for _, n in _SMALL)
SMALL_ROWS = ADAM_ROWS + sum(_small_rows(n) for _, n in _CONV)


def small_update(parts, wp, mp, vp):
    def body(p_ref, w_ref, m_ref, v_ref, g_ref, d_ref, mo_ref, vo_ref):
        gs = p_ref[0]
        for i in range(1, 8):
            gs = gs + p_ref[i]
        w = w_ref[...]
        l0, l1 = w[LB_ROW:LB_ROW + 8], w[LB_ROW + 8:LB_ROW + 16]
        mx = jnp.maximum(l0, l1)
        e0, e1 = jnp.exp(l0 - mx), jnp.exp(l1 - mx)
        p0, p1 = e0 / (e0 + e1), e1 / (e0 + e1)
        dl1 = gs[LB_ROW + 8:LB_ROW + 16]
        s = p1 * dl1
        g = jnp.concatenate([gs[0:LB_ROW], -p0 * s, p1 * dl1 - p1 * s, gs[LB_ROW + 16:ADAM_ROWS]], axis=0)
        d, mn, vn = _adam_math(w, g, m_ref[...], v_ref[...])
        g_ref[0:ADAM_ROWS, :] = g
        g_ref[ADAM_ROWS:, :] = gs[ADAM_ROWS:]
        d_ref[...] = d
        mo_ref[...] = mn
        vo_ref[...] = vn
    sd = jax.ShapeDtypeStruct
    return pl.pallas_call(body, name="small_update",
                          out_shape=[sd((SMALL_ROWS, 128), f32)] + [sd((ADAM_ROWS, 128), f32)] * 3)(parts, wp, mp, vp)


def partial_sum(own, kind, recv, chip, name, transposed=False):
    _, r, c = recv.shape
    tr = 256 if r % 256 == 0 else r

    def body(chip_ref, o_ref, r_ref, out_ref):
        s = ((o_ref[...] + r_ref[0].astype(f32)) + r_ref[1].astype(f32)) + r_ref[2].astype(f32)
        out_ref[...] = s.T if transposed else s

    own_spec = {"slot": pl.BlockSpec((None, tr, c), lambda i, chip: (chip[0], i, 0)),
                "cols": pl.BlockSpec((tr, c), lambda i, chip: (i, chip[0])),
                "rows": pl.BlockSpec((tr, c), lambda i, chip: (chip[0] * (r // tr) + i, 0))}[kind]
    out_spec = pl.BlockSpec((c, tr), lambda i, chip: (0, i)) if transposed else pl.BlockSpec((tr, c), lambda i, chip: (i, 0))
    return pl.pallas_call(
        body, name=name,
        grid_spec=pltpu.PrefetchScalarGridSpec(
            num_scalar_prefetch=1, grid=(r // tr,),
            in_specs=[own_spec, pl.BlockSpec((3, tr, c), lambda i, chip: (0, i, 0))], out_specs=out_spec),
        out_shape=jax.ShapeDtypeStruct((c, r) if transposed else (r, c), f32), compiler_params=_cp(("arbitrary",)),
    )(chip, own, recv)


def adam_pair_t(h, hs, wt, mt, vt, name):
    C, _, R = wt.shape
    tc = 128

    def body(h0_ref, h1_ref, s0_ref, s1_ref, w_ref, m_ref, v_ref, g_ref, d_ref, mo_ref, vo_ref):
        g = jnp.stack([h0_ref[...] + s0_ref[...], h1_ref[...] + s1_ref[...]], axis=1)
        d, mn, vn = _adam_math(w_ref[...], g, m_ref[...], v_ref[...])
        g_ref[...] = g
        d_ref[...] = d
        mo_ref[...] = mn
        vo_ref[...] = vn

    hspec = pl.BlockSpec((tc, R), lambda i: (i, 0))
    spec = pl.BlockSpec((tc, 2, R), lambda i: (i, 0, 0))
    return pl.pallas_call(
        body, name=name, grid=(pl.cdiv(C, tc),), in_specs=[hspec] * 4 + [spec] * 3, out_specs=[spec] * 4,
        out_shape=[jax.ShapeDtypeStruct(wt.shape, f32)] * 4, compiler_params=_cp(("parallel",)),
    )(h[0], h[1], hs[0], hs[1], wt, mt, vt)


MESH = pl.DeviceIdType.MESH
_HBM = pl.BlockSpec(memory_space=pltpu.HBM)


def _place():
    return lax.axis_index("x"), lax.axis_index("y"), lax.axis_index("c")


def weight_gather(arrs):
    n = len(arrs)

    def body(*refs):
        x_refs, out_refs = refs[:n], refs[n:2 * n]
        send_sems, recv_sems, local_sems = refs[2 * n:]
        x, y, c = _place()
        me, sibling = (x, y, c), (x, y, 1 - c)
        chips = [(1 - x, y), (x, 1 - y), (1 - x, 1 - y)]

        def copy(a, k, block, to, own_src=False):
            px, py, pc = block
            dst = out_refs[a].at[2 * px + py, pc]
            return pltpu.make_async_remote_copy(
                src_ref=x_refs[a].at[c] if own_src else dst, dst_ref=dst,
                send_sem=send_sems.at[7 * a + k], recv_sem=recv_sems.at[7 * a + k], device_id=to, device_id_type=MESH)

        mine = [pltpu.make_async_copy(x_refs[a].at[c], out_refs[a].at[2 * x + y, c], local_sems.at[a])
                for a in range(n)]
        for cp in mine:
            cp.start()
        first = []
        for a in range(n):
            first.append(copy(a, 0, me, sibling, own_src=True))
            first += [copy(a, 1 + j, me, (*chip, c), own_src=True) for j, chip in enumerate(chips)]
        for cp in first:
            cp.start()
        passed = []
        for j, chip in enumerate(chips):
            for a in range(n):
                copy(a, 1 + j, (*chip, c), me).wait_recv()
                fwd = copy(a, 4 + j, (*chip, c), sibling)
                fwd.start()
                passed.append(fwd)
        for a in range(n):
            copy(a, 0, sibling, me).wait_recv()
            for j, chip in enumerate(chips):
                copy(a, 4 + j, (*chip, 1 - c), me).wait_recv()
        for cp in first + passed:
            cp.wait_send()
        for cp in mine:
            cp.wait()

    return pl.pallas_call(
        body, name="weight_gather", in_specs=[_HBM] * n, out_specs=[_HBM] * n,
        out_shape=[jax.ShapeDtypeStruct((N_CHIPS,) + a.shape, a.dtype) for a in arrs],
        scratch_shapes=[pltpu.SemaphoreType.DMA((7 * n,)), pltpu.SemaphoreType.DMA((7 * n,)),
                        pltpu.SemaphoreType.DMA((n,))],
    )(*arrs)


SHARD_W = 256


_SEM = pl.BlockSpec(memory_space=pltpu.SEMAPHORE)
_EFFECT = pltpu.SideEffectType.DATAFLOW_SIDE_EFFECTING


def _landing_shape(a, kind):
    if kind == "slot":
        return a.shape[1:]
    return (a.shape[0], SHARD_W) if kind == "cols" else (SHARD_W, a.shape[1])


def _shard_copies(src_refs, land_refs, kinds, send_sems, recv_sems):
    x, y, c = _place()
    copies = []
    for a, (src, land, kind) in enumerate(zip(src_refs, land_refs, kinds)):
        for j, (px, py) in enumerate(((1 - x, y), (x, 1 - y), (1 - x, 1 - y))):
            q = 2 * px + py
            lo = pl.multiple_of(q * SHARD_W, SHARD_W)
            part = {"slot": lambda: src.at[q], "cols": lambda: src.at[:, pl.ds(lo, SHARD_W)],
                    "rows": lambda: src.at[pl.ds(lo, SHARD_W), :]}[kind]()
            k = 3 * a + j
            copies.append(pltpu.make_async_remote_copy(
                src_ref=part, dst_ref=land.at[j], send_sem=send_sems.at[k], recv_sem=recv_sems.at[k],
                device_id=(px, py, c), device_id_type=MESH))
    return copies


def exchange_start(srcs, kinds, name):
    n = len(srcs)
    lands = [lax.empty((3,) + _landing_shape(a, k), a.dtype) for a, k in zip(srcs, kinds)]

    def body(*refs):
        src_refs, land_refs, send_sems, recv_sems, token = refs[:n], refs[n:2 * n], refs[2 * n], refs[2 * n + 1], refs[-1]
        for cp in _shard_copies(src_refs, land_refs, kinds, send_sems, recv_sems):
            cp.start()
        token[...] = jnp.zeros_like(token)

    both = list(srcs) + lands
    out = pl.pallas_call(
        body, name=name,
        out_shape=(pltpu.SemaphoreType.DMA((3 * n,)), pltpu.SemaphoreType.DMA((3 * n,)),
                   *[pltpu.HBM(a.shape, a.dtype) for a in both], jax.ShapeDtypeStruct((8, 128), f32)),
        in_specs=[_HBM] * (2 * n), out_specs=(_SEM, _SEM, *[_HBM] * (2 * n), pl.BlockSpec(memory_space=pltpu.VMEM)),
        input_output_aliases={i: 2 + i for i in range(2 * n)},
        compiler_params=pltpu.CompilerParams(has_side_effects=_EFFECT),
    )(*[pltpu.with_memory_space_constraint(a, pltpu.HBM) for a in both])
    return (out[0], out[1], out[2:2 + 2 * n]), out[-1]


def exchange_wait(handle, kinds, after, name):
    send_sems, recv_sems, both = handle
    n = len(kinds)

    def body(*refs):
        src_refs, land_refs, s_sems, r_sems = refs[:n], refs[n:2 * n], refs[2 * n], refs[2 * n + 1]
        for cp in _shard_copies(src_refs, land_refs, kinds, s_sems, r_sems):
            cp.wait_send()
            cp.wait_recv()

    out = pl.pallas_call(
        body, name=name, out_shape=tuple(pltpu.HBM(a.shape, a.dtype) for a in both),
        in_specs=[_HBM] * (2 * n) + [_SEM, _SEM, pl.BlockSpec(memory_space=pl.ANY)], out_specs=tuple([_HBM] * (2 * n)),
        input_output_aliases={i: i for i in range(2 * n)},
        compiler_params=pltpu.CompilerParams(has_side_effects=_EFFECT),
    )(*both, send_sems, recv_sems, after)
    return out[n:]


def final_exchange(hs, small):
    n = len(hs)
    S = small.shape[0]

    def body(*refs):
        h_refs, sm_ref, out_refs, smalls_ref = refs[:n], refs[n], refs[n + 1:2 * n + 1], refs[2 * n + 1]
        send_sems, recv_sems, local_sem = refs[2 * n + 2:]
        x, y, c = _place()
        my_slot = smalls_ref.at[4 * x + 2 * y + c]
        mine = pltpu.make_async_copy(sm_ref, my_slot, local_sem)
        mine.start()
        copies = [pltpu.make_async_remote_copy(src_ref=h_refs[a], dst_ref=out_refs[a], send_sem=send_sems.at[a],
                                               recv_sem=recv_sems.at[a], device_id=(x, y, 1 - c), device_id_type=MESH)
                  for a in range(n)]
        for mask in range(1, 8):
            fx, fy, fc = (mask >> 2) & 1, (mask >> 1) & 1, mask & 1
            peer = ((1 - x) if fx else x, (1 - y) if fy else y, (1 - c) if fc else c)
            copies.append(pltpu.make_async_remote_copy(
                src_ref=sm_ref, dst_ref=my_slot, send_sem=send_sems.at[n - 1 + mask], recv_sem=recv_sems.at[n - 1 + mask],
                device_id=peer, device_id_type=MESH))
        for cp in copies:
            cp.start()
        for cp in copies:
            cp.wait_recv()
        for cp in copies:
            cp.wait_send()
        mine.wait()

    sd = jax.ShapeDtypeStruct
    out = pl.pallas_call(
        body, name="final_exchange", in_specs=[_HBM] * (n + 1), out_specs=[_HBM] * (n + 1),
        out_shape=[sd(h.shape, h.dtype) for h in hs] + [sd((8, S, 128), f32)],
        scratch_shapes=[pltpu.SemaphoreType.DMA((n + 7,)), pltpu.SemaphoreType.DMA((n + 7,)),
                        pltpu.SemaphoreType.DMA],
    )(*hs, small)
    return out[:n], out[n]


N_CHIPS = 4
SHARD_COLS = N_ORIG // N_CHIPS


SHARD_PAD = 2688
_COL_SEGMENTS = (
    ((0, 2048, OFF_A),)
    + tuple((2048 + 512 * j + HK * h, 2048 + 512 * j + HK * (h + 1), OFF_B + REG_BH * h + HK * j)
            for j in range(3) for h in range(N_HGRN))
    + ((3584, 4096, OFF_M + M_BZ), (4096, 6144, OFF_C), (6144, 6160, OFF_C + 2048), (6160, 7184, OFF_M + M_CZ),
       (7184, N_ORIG, OFF_M + M_G)))


def _shard_pieces():
    pieces = []
    for lo, hi, dst in _COL_SEGMENTS:
        for p in range(N_CHIPS):
            a, b = max(lo, p * SHARD_COLS), min(hi, (p + 1) * SHARD_COLS)
            if a < b:
                pieces.append((p, a - p * SHARD_COLS, dst + a - lo, b - a))
    return pieces


def win_cast_pad(wt):
    tc = 128

    def body(x_ref, o_ref):
        col = pl.program_id(0) * tc + lax.broadcasted_iota(jnp.int32, (tc, 1), 0)
        for l in range(2):
            o_ref[l] = _b(jnp.where(col < SHARD_COLS, x_ref[:, l, :], 0.0).T)

    return pl.pallas_call(
        body, name="win_cast_pad", grid=(SHARD_PAD // tc,),
        in_specs=[pl.BlockSpec((tc, 2, D), lambda i: (i, 0, 0))],
        out_specs=pl.BlockSpec((2, D, tc), lambda i: (0, 0, i)),
        out_shape=jax.ShapeDtypeStruct((2, D, SHARD_PAD), bf16), compiler_params=_cp(("parallel",)),
    )(wt)


def win_to_padded(w4):
    tr = 256
    pieces = _shard_pieces()

    def body(a_ref, o_ref):
        o_ref[...] = jnp.zeros((tr, NP), bf16)
        for p, j0, c0, n in pieces:
            o_ref[:, c0:c0 + n] = a_ref[p, :, j0:j0 + n]

    return pl.pallas_call(
        body, name="win_to_padded", grid=(2, D // tr),
        in_specs=[pl.BlockSpec((N_CHIPS, None, tr, SHARD_PAD), lambda l, i: (0, l, i, 0))],
        out_specs=pl.BlockSpec((None, tr, NP), lambda l, i: (l, i, 0)),
        out_shape=jax.ShapeDtypeStruct((2, D, NP), bf16), compiler_params=_cp(("parallel", "parallel")),
    )(w4)


def win_from_padded(dw):
    tr = 128
    pieces = _shard_pieces()

    def body(d_ref, of_ref, ob_ref):
        for p in range(N_CHIPS):
            of_ref[p, :, SHARD_COLS:] = jnp.zeros((tr, SHARD_PAD - SHARD_COLS), f32)
            ob_ref[p, :, SHARD_COLS:] = jnp.zeros((tr, SHARD_PAD - SHARD_COLS), bf16)
        for p, j0, c0, n in pieces:
            v = d_ref[:, c0:c0 + n]
            of_ref[p, :, j0:j0 + n] = v
            ob_ref[p, :, j0:j0 + n] = _b(v)

    out_spec = pl.BlockSpec((N_CHIPS, tr, SHARD_PAD), lambda i: (0, i, 0))
    return pl.pallas_call(
        body, name="win_from_padded", grid=(D // tr,),
        in_specs=[pl.BlockSpec((tr, NP), lambda i: (i, 0))], out_specs=[out_spec, out_spec],
        out_shape=[jax.ShapeDtypeStruct((N_CHIPS, D, SHARD_PAD), f32),
                   jax.ShapeDtypeStruct((N_CHIPS, D, SHARD_PAD), bf16)],
        compiler_params=_cp(("parallel",)),
    )(dw)


def _rows128(a):
    flat = a.reshape(-1)
    total = -(-flat.shape[0] // 1024) * 1024
    return jnp.pad(flat, (0, total - flat.shape[0])).reshape(total // 128, 128)


def _lb_rows(lb):
    return jnp.pad(lb.reshape(2, 4, 128), ((0, 0), (0, 4), (0, 0))).reshape(16, 128)


def _pack_small(v, with_conv):
    rows = []
    for name, n in _SMALL + (_CONV if with_conv else ()):
        if name == "lower_bounds":
            rows.append(_lb_rows(v[name]))
        elif name == "loss":
            rows.append(jnp.broadcast_to(v[name], (16, 128)) if name in v else jnp.zeros((16, 128), f32))
        else:
            rows.append(_rows128(v[name]))
    return jnp.concatenate(rows, axis=0)


def _unpack_small(p, shapes, with_conv):
    out, row = {}, 0
    for name, n in _SMALL + (_CONV if with_conv else ()):
        nrows = _small_rows(n)
        blk = p[row:row + nrows]
        if name == "lower_bounds":
            out[name] = blk.reshape(2, 8, 128)[:, :4].reshape(2, 512)
        elif name == "loss":
            out[name] = blk[0, 0]
        else:
            out[name] = blk.reshape(-1)[:n].reshape(shapes[name])
        row += nrows
    return out


def _lane_vec(a8):
    return jnp.pad(a8.reshape(1, 8), ((0, 0), (8, 112)))


WEIGHT_NAMES = ("norm_w", "w_in", "b_gate", "conv_a", "conv_c", "a_log", "dt_bias", "lower_bounds", "hgrn_norm_w",
                "gdn_norm_w", "w_out_a", "w_out_b", "w_out_c", "w_o", "final_norm_w")


def kernel(x, norm_w, w_in, b_gate, conv_a, conv_c, a_log, dt_bias, lower_bounds, hgrn_norm_w, gdn_norm_w, w_out_a, w_out_b, w_out_c, w_o, final_norm_w, loss_target, m_norm_w, m_w_in, m_b_gate, m_conv_a, m_conv_c, m_a_log, m_dt_bias, m_lower_bounds, m_hgrn_norm_w, m_gdn_norm_w, m_w_out_a, m_w_out_b, m_w_out_c, m_w_o, m_final_norm_w, v_norm_w, v_w_in, v_b_gate, v_conv_a, v_conv_c, v_a_log, v_dt_bias, v_lower_bounds, v_hgrn_norm_w, v_gdn_norm_w, v_w_out_a, v_w_out_b, v_w_out_c, v_w_o, v_final_norm_w):
    wts = dict(norm_w=norm_w, w_in=w_in, b_gate=b_gate, conv_a=conv_a, conv_c=conv_c, a_log=a_log, dt_bias=dt_bias,
               lower_bounds=lower_bounds, hgrn_norm_w=hgrn_norm_w, gdn_norm_w=gdn_norm_w, w_out_a=w_out_a,
               w_out_b=w_out_b, w_out_c=w_out_c, w_o=w_o, final_norm_w=final_norm_w)
    mom = dict(norm_w=m_norm_w, w_in=m_w_in, b_gate=m_b_gate, conv_a=m_conv_a, conv_c=m_conv_c, a_log=m_a_log,
               dt_bias=m_dt_bias, lower_bounds=m_lower_bounds, hgrn_norm_w=m_hgrn_norm_w, gdn_norm_w=m_gdn_norm_w,
               w_out_a=m_w_out_a, w_out_b=m_w_out_b, w_out_c=m_w_out_c, w_o=m_w_o, final_norm_w=m_final_norm_w)
    var = dict(norm_w=v_norm_w, w_in=v_w_in, b_gate=v_b_gate, conv_a=v_conv_a, conv_c=v_conv_c, a_log=v_a_log,
               dt_bias=v_dt_bias, lower_bounds=v_lower_bounds, hgrn_norm_w=v_hgrn_norm_w, gdn_norm_w=v_gdn_norm_w,
               w_out_a=v_w_out_a, w_out_b=v_w_out_b, w_out_c=v_w_out_c, w_o=v_w_o, final_norm_w=v_final_norm_w)
    chip = 2 * lax.axis_index("x") + lax.axis_index("y")
    chip1 = chip.reshape(1).astype(jnp.int32)

    win4, woa4, wob4, woc4, wo4, ca4, cc4 = weight_gather(
        [win_cast_pad(jnp.transpose(w_in, (2, 0, 1))), _b(w_out_a), _b(w_out_b), _b(w_out_c), _b(w_o), conv_a, conv_c])
    by_cols = lambda a: a.transpose(1, 2, 0, 3).reshape(a.shape[1], a.shape[2], N_CHIPS * a.shape[3])
    by_rows = lambda a: a.transpose(1, 0, 2, 3).reshape(a.shape[1], N_CHIPS * a.shape[2], a.shape[3])
    full = dict(w_in=win_to_padded(win4), w_out_a=by_cols(woa4), w_out_b=by_cols(wob4), w_out_c=by_rows(woc4),
                w_o=by_rows(wo4), conv_a=by_cols(ca4), conv_c=by_cols(cc4))
    lbs = lbs_fwd(lower_bounds)
    layers = []
    for l in range(2):
        layers.append(dict(
            l=l, norm_w=norm_w[l:l + 1], w_in=full["w_in"], b_gate=b_gate[l:l + 1], conv_a=full["conv_a"][l],
            conv_c=full["conv_c"][l], alog_l=_lane_vec(a_log[l]), dtb_l=_lane_vec(dt_bias[l]), lbs=lbs[l:l + 1],
            hgrn_norm_w=hgrn_norm_w[l:l + 1], gdn_norm_w=gdn_norm_w[l:l + 1], w_out_a=full["w_out_a"],
            w_out_b=full["w_out_b"], w_out_c=full["w_out_c"], w_o=full["w_o"]))

    xs, saved = x[0], []
    for l in range(2):
        xs, s = layer_fwd(xs, layers[l])
        saved.append(s)
    loss_row, dx, dfw = loss_head(xs, final_norm_w.reshape(1, D), loss_target[0])
    lg, half = [None, None], [None, None]
    for l in (1, 0):
        dx, lg[l], half[l] = layer_bwd(dx, layers[l], saved[l], chip1)
    grad_x = dx[None]

    stack = lambda n: jnp.stack([lg[0][n], lg[1][n]], axis=0)
    gsmall = {n: stack(n) for n in ("norm_w", "b_gate", "hgrn_norm_w", "gdn_norm_w", "a_log", "dt_bias", "conv_a",
                                    "conv_c")}
    gsmall.update(lower_bounds=stack("lbs"), final_norm_w=dfw, loss=loss_row)
    mat_names = ("w_in",) + tuple(n for n, _ in OUT_MATS)
    mine = [half[l][n] for n in mat_names for l in range(2)]
    theirs, smalls = final_exchange(mine, _pack_small(gsmall, True))

    out_g, out_d, out_m, out_v = {}, {}, {}, {}
    for i, n in enumerate(mat_names):
        h, hs = mine[2 * i:2 * i + 2], theirs[2 * i:2 * i + 2]
        if n == "w_in":
            fwd, back = (lambda a: jnp.transpose(a, (2, 0, 1))), (lambda a: jnp.transpose(a, (1, 2, 0)))
            res = adam_pair_t(h, hs, fwd(wts[n]), fwd(mom[n]), fwd(var[n]), "adam_" + n)
            out_g[n], out_d[n], out_m[n], out_v[n] = (back(a) for a in res)
        else:
            out_g[n], out_d[n], out_m[n], out_v[n] = adam_pair(h, hs, wts[n], mom[n], var[n], "adam_" + n)
    small_names = [n for n, _ in _SMALL if n != "loss"]
    pack = lambda v: _pack_small({n: v[n] for n in small_names}, False)
    sg, sd, smn, svn = small_update(smalls, pack(wts), pack(mom), pack(var))
    shapes = {n: wts[n].shape for n in small_names}
    shapes.update(conv_a=(2, 3, 512), conv_c=(2, 4, 2048))
    for dst, src, conv in ((out_g, sg, True), (out_d, sd, False), (out_m, smn, False), (out_v, svn, False)):
        dst.update(_unpack_small(src, shapes, conv))
    loss = out_g.pop("loss")
    for n in ("conv_a", "conv_c"):
        width = wts[n].shape[2]
        g = lax.dynamic_slice_in_dim(out_g[n], chip * width, width, axis=2)
        two_d = lambda a: a.reshape(-1, width)
        d, mn, vn = adam(two_d(wts[n]), two_d(g), two_d(mom[n]), two_d(var[n]), "adam_" + n)
        out_g[n] = g
        out_d[n], out_m[n], out_v[n] = (a.reshape(wts[n].shape) for a in (d, mn, vn))
    return (loss, grad_x, *[out_g[n] for n in WEIGHT_NAMES], *[out_d[n] for n in WEIGHT_NAMES],
            *[out_m[n] for n in WEIGHT_NAMES], *[out_v[n] for n in WEIGHT_NAMES])
```

```python
import functools

import jax
import jax.numpy as jnp
from jax import lax
from jax.experimental import pallas as pl
from jax.experimental.pallas import tpu as pltpu

f32 = jnp.float32
bf16 = jnp.bfloat16

D = 1024
L = 64
SUB = 16
NORM_EPS = 1e-6
L2_EPS = 1e-6
MIN_F = 1e-30
HK = 128
QK_SCALE = HK ** -0.5
N_GDN = 8
GDN_BLOCK = 1024
N_HGRN = 4

REG_A = 2304
REG_C = 2304
REG_M = 4608
REG_BH = 384
OFF_A, OFF_C, OFF_M, OFF_B = 0, 2304, 4608, 9216
M_BZ, M_G, M_CZ = 0, 512, 3584
NP = 10752
NP_TILE = 1536
N_ORIG = 10256

ADAM_LR, ADAM_B1, ADAM_B2, ADAM_EPS, ADAM_WD, ADAM_STEP = 0.001, 0.9, 0.999, 1e-08, 0.01, 10

VMEM_LIMIT = 56 * 1024 * 1024


def _cp(sem):
    return pltpu.CompilerParams(dimension_semantics=sem, vmem_limit_bytes=VMEM_LIMIT)


def _sigmoid(x):
    return jax.nn.sigmoid(x)


def _silu(x):
    return x * _sigmoid(x)


def _dsilu(x):
    s = _sigmoid(x)
    return s * (1.0 + x * (1.0 - s))


def _softplus(x):
    u = jnp.exp(-jnp.abs(x))
    w = 1.0 + u
    l1p = jnp.where(w == 1.0, u, jnp.log(w) * (u / (w - 1.0)))
    return jnp.maximum(x, 0.0) + l1p


def _dot(a, b):
    return jnp.dot(a, b, preferred_element_type=f32)


def _dot_nt(a, b):
    return lax.dot_general(a, b, (((1,), (1,)), ((), ())), preferred_element_type=f32)


def _dot_tn(a, b):
    return lax.dot_general(a, b, (((0,), (0,)), ((), ())), preferred_element_type=f32)


def _bdot(a, b):
    return lax.dot_general(a, b, (((2,), (1,)), ((0,), (0,))), preferred_element_type=f32)


def _bdot_nt(a, b):
    return lax.dot_general(a, b, (((2,), (2,)), ((0,), (0,))), preferred_element_type=f32)


def _bdot_tn(a, b):
    return lax.dot_general(a, b, (((1,), (1,)), ((0,), (0,))), preferred_element_type=f32)


def _bdot_split(a, b):
    ah, bh = _b(a), _b(b)
    al, bl = _b(a - ah.astype(f32)), _b(b - bh.astype(f32))
    return _bdot(ah, bh) + (_bdot(ah, bl) + _bdot(al, bh))


def _b(x):
    return x.astype(bf16)


def _chunk_cumsum(x, rows_in_chunk):
    n = x.shape[0]
    for s in (1, 2, 4, 8, 16, 32):
        x = x + jnp.where(rows_in_chunk >= s, pltpu.roll(x, s, axis=0), 0.0)
    return x


def _chunk_rev_cumsum(x, rows_in_chunk):
    n = x.shape[0]
    for s in (1, 2, 4, 8, 16, 32):
        x = x + jnp.where(rows_in_chunk + s < L, pltpu.roll(x, n - s, axis=0), 0.0)
    return x


def _shift_down(x, s):
    return pltpu.roll(x, s, axis=0) if s else x


def _shift_up(x, s):
    return pltpu.roll(x, x.shape[0] - s, axis=0) if s else x


def inproj_fwd(x, nw, w):
    T = x.shape[0]
    tT, tn = min(2048, T), NP_TILE // 2

    def body(x_ref, nw_ref, w_ref, p_ref, h_ref, hs):
        @pl.when(pl.program_id(1) == 0)
        def _():
            xv = x_ref[...]
            r = lax.rsqrt(jnp.mean(xv * xv, axis=-1, keepdims=True) + NORM_EPS)
            hv = _b(xv * r * nw_ref[...])
            hs[...] = hv
            h_ref[...] = hv
        p_ref[...] = _dot(hs[...], w_ref[...])

    return pl.pallas_call(
        body, name="inproj_fwd", grid=(T // tT, NP // tn),
        in_specs=[pl.BlockSpec((tT, D), lambda i, j: (i, 0)), pl.BlockSpec((1, D), lambda i, j: (0, 0)),
                  pl.BlockSpec((D, tn), lambda i, j: (0, j))],
        out_specs=[pl.BlockSpec((tT, tn), lambda i, j: (i, j)), pl.BlockSpec((tT, D), lambda i, j: (i, 0))],
        out_shape=[jax.ShapeDtypeStruct((T, NP), f32), jax.ShapeDtypeStruct((T, D), bf16)],
        scratch_shapes=[pltpu.VMEM((tT, D), bf16)],
        compiler_params=_cp(("parallel", "arbitrary")),
    )(x, nw, w)


def matmul_tn(a, b, name, n=None, b_col0=0, with_bf16=False):
    T, K = a.shape
    N = b.shape[1] if n is None else n
    tT = min(2048, T)
    tn = NP_TILE if N % NP_TILE == 0 else min(N, 1024)
    nt = T // tT
    cb0 = b_col0 // tn

    def body(a_ref, b_ref, o_ref, *ob_ref):
        @pl.when(pl.program_id(1) == 0)
        def _():
            o_ref[...] = jnp.zeros_like(o_ref)
        o_ref[...] += _dot_tn(_b(a_ref[...]), _b(b_ref[...]))
        if with_bf16:
            @pl.when(pl.program_id(1) == nt - 1)
            def _():
                ob_ref[0][...] = _b(o_ref[...])

    ospec = pl.BlockSpec((K, tn), lambda j, t: (0, j))
    return pl.pallas_call(
        body, name=name, grid=(N // tn, nt),
        in_specs=[pl.BlockSpec((tT, K), lambda j, t: (t, 0)), pl.BlockSpec((tT, tn), lambda j, t: (t, cb0 + j))],
        out_specs=[ospec, ospec] if with_bf16 else ospec,
        out_shape=([jax.ShapeDtypeStruct((K, N), f32), jax.ShapeDtypeStruct((K, N), bf16)] if with_bf16
                   else jax.ShapeDtypeStruct((K, N), f32)),
        compiler_params=_cp(("parallel", "arbitrary")),
    )(a, b)


def inproj_bwd(dp, w, x, nw, dres):
    T = x.shape[0]
    tT, tk = min(1024, T), NP_TILE
    nk = NP // tk

    def body(dp_ref, w_ref, x_ref, nw_ref, dres_ref, dx_ref, dnw_ref, acc):
        i, k = pl.program_id(0), pl.program_id(1)

        @pl.when((i == 0) & (k == 0))
        def _():
            dnw_ref[...] = jnp.zeros_like(dnw_ref)

        @pl.when(k == 0)
        def _():
            acc[...] = jnp.zeros_like(acc)
        acc[...] += _dot_nt(dp_ref[...], w_ref[...])

        @pl.when(k == nk - 1)
        def _():
            xv = x_ref[...]
            r = lax.rsqrt(jnp.mean(xv * xv, axis=-1, keepdims=True) + NORM_EPS)
            xh = xv * r
            dy = acc[...]
            dyw = dy * nw_ref[...]
            dx_ref[...] = r * (dyw - xh * jnp.mean(dyw * xh, axis=-1, keepdims=True)) + dres_ref[...]
            dnw_ref[...] += jnp.sum(dy * xh, axis=0, keepdims=True)

    return pl.pallas_call(
        body, name="inproj_bwd", grid=(T // tT, nk),
        in_specs=[pl.BlockSpec((tT, tk), lambda i, k: (i, k)), pl.BlockSpec((D, tk), lambda i, k: (0, k)),
                  pl.BlockSpec((tT, D), lambda i, k: (i, 0)), pl.BlockSpec((1, D), lambda i, k: (0, 0)),
                  pl.BlockSpec((tT, D), lambda i, k: (i, 0))],
        out_specs=[pl.BlockSpec((tT, D), lambda i, k: (i, 0)), pl.BlockSpec((1, D), lambda i, k: (0, 0))],
        out_shape=[jax.ShapeDtypeStruct((T, D), f32), jax.ShapeDtypeStruct((1, D), f32)],
        scratch_shapes=[pltpu.VMEM((tT, D), f32)],
        compiler_params=_cp(("arbitrary", "arbitrary")),
    )(dp, w, x, nw, dres)


def loss_head(x, fw, tgt):
    T = x.shape[0]
    tT = min(512, T)

    def body(x_ref, fw_ref, t_ref, loss_ref, dx_ref, dfw_ref):
        @pl.when(pl.program_id(0) == 0)
        def _():
            loss_ref[...] = jnp.zeros_like(loss_ref)
            dfw_ref[...] = jnp.zeros_like(dfw_ref)
        xv = x_ref[...]
        r = lax.rsqrt(jnp.mean(xv * xv, axis=-1, keepdims=True) + NORM_EPS)
        xh = xv * r
        err = xh * fw_ref[...] - t_ref[...]
        part = 0.5 * jnp.sum(jnp.mean(err * err, axis=-1, keepdims=True), axis=0, keepdims=True)
        loss_ref[...] += jnp.broadcast_to(part, loss_ref.shape)
        dy = err * (1.0 / D)
        dyw = dy * fw_ref[...]
        dx_ref[...] = r * (dyw - xh * jnp.mean(dyw * xh, axis=-1, keepdims=True))
        dfw_ref[...] += jnp.sum(dy * xh, axis=0, keepdims=True)

    return pl.pallas_call(
        body, name="loss_head", grid=(T // tT,),
        in_specs=[pl.BlockSpec((tT, D), lambda i: (i, 0)), pl.BlockSpec((1, D), lambda i: (0, 0)),
                  pl.BlockSpec((tT, D), lambda i: (i, 0))],
        out_specs=[pl.BlockSpec((1, 128), lambda i: (0, 0)), pl.BlockSpec((tT, D), lambda i: (i, 0)),
                   pl.BlockSpec((1, D), lambda i: (0, 0))],
        out_shape=[jax.ShapeDtypeStruct((1, 128), f32), jax.ShapeDtypeStruct((T, D), f32),
                   jax.ShapeDtypeStruct((1, D), f32)],
        compiler_params=_cp(("arbitrary",)),
    )(x, fw, tgt)


def _halo_specs(tT, T, width, colblk):
    nb8 = T // 8
    per = tT // 8
    prev = pl.BlockSpec((8, width), lambda i: (jnp.maximum(i * per - 1, 0), colblk))
    nxt = pl.BlockSpec((8, width), lambda i: (jnp.minimum((i + 1) * per, nb8 - 1), colblk))
    return prev, nxt


def mixa_fwd(p, cw):
    T = p.shape[0]
    tT = min(512, T)
    prev_spec, _ = _halo_specs(tT, T, REG_A, OFF_A // REG_A)

    def body(p_ref, pp_ref, cw_ref, y_ref):
        pv = p_ref[...]
        u = pv[:, 512:1024] * pv[:, 1024:1536]
        pp = pp_ref[...]
        up = jnp.where(pl.program_id(0) == 0, 0.0, pp[:, 512:1024] * pp[:, 1024:1536])
        ue = jnp.concatenate([up, u], axis=0)
        cv = cw_ref[0:1, :] * _shift_down(ue, 2) + cw_ref[1:2, :] * _shift_down(ue, 1) + cw_ref[2:3, :] * ue
        y_ref[...] = _b(pv[:, 0:512] * cv[8:] * _silu(pv[:, 1536:2048]))

    return pl.pallas_call(
        body, name="mixa_fwd", grid=(T // tT,),
        in_specs=[pl.BlockSpec((tT, REG_A), lambda i: (i, OFF_A // REG_A)), prev_spec,
                  pl.BlockSpec((3, 512), lambda i: (0, 0))],
        out_specs=pl.BlockSpec((tT, 512), lambda i: (i, 0)),
        out_shape=jax.ShapeDtypeStruct((T, 512), bf16),
        compiler_params=_cp(("parallel",)),
    )(p, p, cw)


def mixa_bwd(p, cw, dy, dp):
    T = p.shape[0]
    tT = min(512, T)
    nt = T // tT
    prev_spec, next_spec = _halo_specs(tT, T, REG_A, OFF_A // REG_A)
    _, dnext_spec = _halo_specs(tT, T, 512, 0)

    def body(p_ref, pp_ref, pn_ref, cw_ref, dy_ref, dyn_ref, dp_in, dp_ref, dcw_ref):
        i = pl.program_id(0)

        @pl.when(i == 0)
        def _():
            dcw_ref[...] = jnp.zeros_like(dcw_ref)
        pv, pp, pn = p_ref[:, 0:2048], pp_ref[:, 0:2048], pn_ref[:, 0:2048]
        pe = jnp.concatenate([pp, pv, pn], axis=0)
        rows = lax.broadcasted_iota(jnp.int32, (tT + 16, 1), 0)
        ab, ac, ax, az = pe[:, 0:512], pe[:, 512:1024], pe[:, 1024:1536], pe[:, 1536:2048]
        u = jnp.where((rows < 8) & (i == 0), 0.0, ac * ax)
        u1, u2 = _shift_down(u, 1), _shift_down(u, 2)
        w0, w1, w2 = cw_ref[0:1, :], cw_ref[1:2, :], cw_ref[2:3, :]
        cv = w0 * u2 + w1 * u1 + w2 * u
        dye = jnp.concatenate([jnp.zeros((8, 512), f32), dy_ref[...], dyn_ref[...]], axis=0)
        dye = jnp.where((rows >= tT + 8) & (i == nt - 1), 0.0, dye)
        sz = _silu(az)
        dcv = dye * ab * sz
        du = w2 * dcv + w1 * _shift_up(dcv, 1) + w0 * _shift_up(dcv, 2)
        inner = (rows >= 8) & (rows < tT + 8)
        dcv_in = jnp.where(inner, dcv, 0.0)
        dcw_ref[0:1, :] += jnp.sum(dcv_in * u2, axis=0, keepdims=True)
        dcw_ref[1:2, :] += jnp.sum(dcv_in * u1, axis=0, keepdims=True)
        dcw_ref[2:3, :] += jnp.sum(dcv_in * u, axis=0, keepdims=True)
        sl = slice(8, tT + 8)
        dp_ref[:, 0:512] = _b((dye * cv * sz)[sl])
        dp_ref[:, 512:1024] = _b((du * ax)[sl])
        dp_ref[:, 1024:1536] = _b((du * ac)[sl])
        dp_ref[:, 1536:2048] = _b((dye * ab * cv * _dsilu(az))[sl])
        dp_ref[:, 2048:] = jnp.zeros((tT, REG_A - 2048), bf16)

    return pl.pallas_call(
        body, name="mixa_bwd", grid=(nt,),
        in_specs=[pl.BlockSpec((tT, REG_A), lambda i: (i, OFF_A // REG_A)), prev_spec, next_spec,
                  pl.BlockSpec((3, 512), lambda i: (0, 0)),
                  pl.BlockSpec((tT, 512), lambda i: (i, 0)), dnext_spec, pl.BlockSpec(memory_space=pl.ANY)],
        out_specs=[pl.BlockSpec((tT, REG_A), lambda i: (i, OFF_A // REG_A)), pl.BlockSpec((8, 512), lambda i: (0, 0))],
        out_shape=[jax.ShapeDtypeStruct((T, NP), bf16), jax.ShapeDtypeStruct((8, 512), f32)],
        input_output_aliases={6: 0},
        compiler_params=_cp(("arbitrary",)),
    )(p, p, p, cw, dy, dy, dp)


def _l2n_fwd(y):
    return y * lax.rsqrt(jnp.sum(y * y, axis=-1, keepdims=True) + L2_EPS)


def mixc_pre_fwd(p, cw, alog_l, dtb_l):
    T = p.shape[0]
    tT = min(512, T)
    prev_spec, _ = _halo_specs(tT, T, REG_C, OFF_C // REG_C)

    def body(p_ref, pp_ref, cw_ref, al_ref, dt_ref, q_ref, k_ref, v_ref, sm_ref):
        pp = jnp.where(pl.program_id(0) == 0, 0.0, pp_ref[:, 0:2048])
        xe = jnp.concatenate([pp, p_ref[:, 0:2048]], axis=0)
        cv = (cw_ref[0:1, :] * _shift_down(xe, 3) + cw_ref[1:2, :] * _shift_down(xe, 2)
              + cw_ref[2:3, :] * _shift_down(xe, 1) + cw_ref[3:4, :] * xe)[8:]
        y = _silu(cv)
        for hh in range(4):
            sl = slice(hh * HK, (hh + 1) * HK)
            q_ref[:, sl] = _l2n_fwd(y[:, sl]) * QK_SCALE
            k_ref[:, sl] = _l2n_fwd(y[:, 512 + hh * HK:512 + (hh + 1) * HK])
        v_ref[...] = y[:, 1024:2048]
        ps = p_ref[:, 2048:2176]
        lane = lax.broadcasted_iota(jnp.int32, ps.shape, 1)
        la = -jnp.exp(al_ref[...]) * _softplus(ps + dt_ref[...])
        rin = lax.broadcasted_iota(jnp.int32, ps.shape, 0) % L
        g = _chunk_cumsum(la, rin)
        sm_ref[...] = jnp.where(lane < 8, _sigmoid(ps), jnp.where(lane < 16, g, 0.0))

    return pl.pallas_call(
        body, name="mixc_pre_fwd", grid=(T // tT,),
        in_specs=[pl.BlockSpec((tT, REG_C), lambda i: (i, OFF_C // REG_C)), prev_spec,
                  pl.BlockSpec((4, 2048), lambda i: (0, 0)),
                  pl.BlockSpec((1, 128), lambda i: (0, 0)), pl.BlockSpec((1, 128), lambda i: (0, 0))],
        out_specs=[pl.BlockSpec((tT, 512), lambda i: (i, 0)), pl.BlockSpec((tT, 512), lambda i: (i, 0)),
                   pl.BlockSpec((tT, 1024), lambda i: (i, 0)), pl.BlockSpec((tT, 128), lambda i: (i, 0))],
        out_shape=[jax.ShapeDtypeStruct((T, 512), f32), jax.ShapeDtypeStruct((T, 512), f32),
                   jax.ShapeDtypeStruct((T, 1024), f32), jax.ShapeDtypeStruct((T, 128), f32)],
        compiler_params=_cp(("parallel",)),
    )(p, p, cw, alog_l, dtb_l)


def mixc_pre_bwd(p, cw, alog_l, dtb_l, dq8, dk8, dv, dsm8, dp):
    T = p.shape[0]
    tT = min(256, T)
    nt = T // tT
    prev_spec, next_spec = _halo_specs(tT, T, REG_C, OFF_C // REG_C)
    _, n1024 = _halo_specs(tT, T, 1024, 0)

    def body(p_ref, pp_ref, pn_ref, cw_ref, al_ref, dt_ref, dq_ref, dqn_ref, dk_ref, dkn_ref,
             dv_ref, dvn_ref, dsm_ref, dp_in, dp_ref, dcw_ref, dsml_ref):
        i = pl.program_id(0)

        @pl.when(i == 0)
        def _():
            dcw_ref[...] = jnp.zeros_like(dcw_ref)
            dsml_ref[...] = jnp.zeros_like(dsml_ref)
        rows = lax.broadcasted_iota(jnp.int32, (tT + 16, 1), 0)
        pp = jnp.where(i == 0, 0.0, pp_ref[:, 0:2048])
        xe = jnp.concatenate([pp, p_ref[:, 0:2048], pn_ref[:, 0:2048]], axis=0)
        xs = [_shift_down(xe, 3), _shift_down(xe, 2), _shift_down(xe, 1), xe]
        cv = cw_ref[0:1, :] * xs[0] + cw_ref[1:2, :] * xs[1] + cw_ref[2:3, :] * xs[2] + cw_ref[3:4, :] * xs[3]
        y = _silu(cv)
        last = (rows >= tT + 8) & (i == nt - 1)
        z8q = jnp.zeros((8, 1024), f32)

        def ext(cur_ref, nxt_ref):
            return jnp.where(last, 0.0, jnp.concatenate([z8q, cur_ref[...], nxt_ref[...]], axis=0))
        dq8e, dk8e, dve = ext(dq_ref, dqn_ref), ext(dk_ref, dkn_ref), ext(dv_ref, dvn_ref)
        dys = []
        for (d8, base, scale) in ((dq8e, 0, QK_SCALE), (dk8e, 512, 1.0)):
            for hh in range(4):
                dn = (d8[:, (2 * hh) * HK:(2 * hh + 1) * HK] + d8[:, (2 * hh + 1) * HK:(2 * hh + 2) * HK]) * scale
                yh = y[:, base + hh * HK:base + (hh + 1) * HK]
                r = lax.rsqrt(jnp.sum(yh * yh, axis=-1, keepdims=True) + L2_EPS)
                nh = yh * r
                dys.append(r * (dn - nh * jnp.sum(dn * nh, axis=-1, keepdims=True)))
        dyy = jnp.concatenate(dys + [dve], axis=1)
        dcv = dyy * _dsilu(cv)
        dx = (cw_ref[3:4, :] * dcv + cw_ref[2:3, :] * _shift_up(dcv, 1) + cw_ref[1:2, :] * _shift_up(dcv, 2)
              + cw_ref[0:1, :] * _shift_up(dcv, 3))
        dp_ref[:, 0:2048] = _b(dx[8:tT + 8])
        dp_ref[:, 2176:] = jnp.zeros((tT, REG_C - 2176), bf16)
        inner = (rows >= 8) & (rows < tT + 8)
        dcv_in = jnp.where(inner, dcv, 0.0)
        for j in range(4):
            dcw_ref[j:j + 1, :] += jnp.sum(dcv_in * xs[j], axis=0, keepdims=True)
        ps = p_ref[:, 2048:2176]
        lane = lax.broadcasted_iota(jnp.int32, ps.shape, 1)
        dsm = dsm_ref[:, 0:128]
        for hh in range(1, N_GDN):
            dsm = dsm + dsm_ref[:, hh * 128:(hh + 1) * 128]
        beta = _sigmoid(ps)
        xa = ps + dt_ref[...]
        nea = -jnp.exp(al_ref[...])
        dpa = dsm * nea * _sigmoid(xa)
        dp_ref[:, 2048:2176] = _b(jnp.where(lane < 8, dsm * beta * (1.0 - beta), jnp.where(lane < 16, dpa, 0.0)))
        amask = (lane >= 8) & (lane < 16)
        dsml_ref[0:1, :] += jnp.sum(jnp.where(amask, dsm * nea * _softplus(xa), 0.0), axis=0, keepdims=True)
        dsml_ref[1:2, :] += jnp.sum(jnp.where(amask, dpa, 0.0), axis=0, keepdims=True)

    cur1024 = pl.BlockSpec((tT, 1024), lambda i: (i, 0))
    return pl.pallas_call(
        body, name="mixc_pre_bwd", grid=(nt,),
        in_specs=[pl.BlockSpec((tT, REG_C), lambda i: (i, OFF_C // REG_C)), prev_spec, next_spec,
                  pl.BlockSpec((4, 2048), lambda i: (0, 0)),
                  pl.BlockSpec((1, 128), lambda i: (0, 0)), pl.BlockSpec((1, 128), lambda i: (0, 0)),
                  cur1024, n1024, cur1024, n1024, cur1024, n1024, cur1024, pl.BlockSpec(memory_space=pl.ANY)],
        out_specs=[pl.BlockSpec((tT, REG_C), lambda i: (i, OFF_C // REG_C)),
                   pl.BlockSpec((8, 2048), lambda i: (0, 0)), pl.BlockSpec((8, 128), lambda i: (0, 0))],
        out_shape=[jax.ShapeDtypeStruct((T, NP), bf16),
                   jax.ShapeDtypeStruct((8, 2048), f32), jax.ShapeDtypeStruct((8, 128), f32)],
        input_output_aliases={13: 0},
        compiler_params=_cp(("arbitrary",)),
    )(p, p, p, cw, alog_l, dtb_l, dq8, dq8, dk8, dk8, dv, dv, dsm8, dp)


def _tri_inverse(m):
    r = lax.broadcasted_iota(jnp.int32, (L, L), 0)
    c = lax.broadcasted_iota(jnp.int32, (L, L), 1)
    eye = (r == c).astype(f32)[None]
    same = lambda w: ((r // w) == (c // w))[None]
    md = jnp.where(same(8), m, 0.0)
    m2 = _bdot_split(md, md)
    m4 = _bdot_split(m2, m2)
    t = _bdot_split(_bdot_split(eye - md, eye + m2), eye + m4)
    for w in (16, 32, 64):
        mo = jnp.where(same(w) & jnp.logical_not(same(w // 2)), m, 0.0)
        t = t - _bdot_split(_bdot_split(t, mo), t)
    return t


def _col_to_row(col, eye):
    return jnp.sum(eye * col, axis=1, keepdims=True)


def _row_to_col(row, eye):
    return jnp.sum(eye * row, axis=2, keepdims=True)


def _gdn_chunk_terms(q, k, v, beta, g, t_inv=None):
    r = lax.broadcasted_iota(jnp.int32, (L, L), 0)
    c = lax.broadcasted_iota(jnp.int32, (L, L), 1)
    eye = (r == c).astype(f32)[None]
    causal, strict = (c <= r)[None], (c < r)[None]
    diff = g - _col_to_row(g, eye)
    dec = jnp.exp(jnp.where(causal, diff, 0.0))
    dc = jnp.where(causal, dec, 0.0)
    ds = jnp.where(strict, dec, 0.0)
    eg = jnp.exp(g)
    gl = g[:, L - 1:L, :]
    egl = jnp.exp(gl - g)
    kb = k * beta
    kk = _bdot_nt(_b(k), _b(kb))
    qk = _bdot_nt(_b(q), _b(kb))
    m = kk * ds
    aqk = qk * dc
    if t_inv is None:
        t_inv = _tri_inverse(m)
    tb = _b(t_inv)
    keg = k * eg
    u = _bdot(tb, _b(v))
    w = _bdot(tb, _b(keg))
    ks = kb * egl
    ksw = _bdot_tn(_b(ks), _b(w))
    return dict(eye=eye, causal=causal, strict=strict, dc=dc, ds=ds, eg=eg, gl=gl, egl=egl, kb=kb, kk=kk, qk=qk,
                m=m, aqk=aqk, t=t_inv, u=u, w=w, qi=q * eg, ks=ks, keg=keg, ksw=ksw)


def gdn_fwd(qn, kn, vv, sm):
    T = qn.shape[0]
    tB = min(GDN_BLOCK, T)
    nc = tB // L
    N = T // L

    def body(q_ref, k_ref, v_ref, sm_ref, o_ref, st_ref, ti_ref, s_scr):
        h = pl.program_id(0)

        @pl.when(pl.program_id(1) == 0)
        def _():
            s_scr[...] = jnp.zeros_like(s_scr)
        smv = sm_ref[...]
        lane = lax.broadcasted_iota(jnp.int32, smv.shape, 1)
        beta = jnp.sum(jnp.where(lane == h, smv, 0.0), axis=1, keepdims=True).reshape(nc, L, 1)
        g = jnp.sum(jnp.where(lane == 8 + h, smv, 0.0), axis=1, keepdims=True).reshape(nc, L, 1)
        q = q_ref[...].reshape(nc, L, HK)
        k = k_ref[...].reshape(nc, L, HK)
        v = v_ref[...].reshape(nc, L, HK)
        tm = _gdn_chunk_terms(q, k, v, beta, g)
        ti_ref[...] = tm["t"]
        ksu = _bdot_tn(_b(tm["ks"]), _b(tm["u"]))
        kswb = _b(tm["ksw"])
        egl_last = jnp.exp(tm["gl"])
        s = s_scr[...]
        states = [None] * nc
        for ci in range(nc):
            states[ci] = s
            s = egl_last[ci] * s + (ksu[ci] - _dot(kswb[ci], _b(s)))
        s_scr[...] = s
        sall = jnp.stack(states, axis=0)
        st_ref[...] = sall
        sb = _b(sall)
        e = tm["u"] - _bdot(_b(tm["w"]), sb)
        o = _bdot(_b(tm["qi"]), sb) + _bdot(_b(tm["aqk"]), _b(e))
        o_ref[...] = o.reshape(tB, HK)

    return pl.pallas_call(
        body, name="gdn_fwd", grid=(N_GDN, T // tB),
        in_specs=[pl.BlockSpec((tB, HK), lambda h, n: (n, h // 2)), pl.BlockSpec((tB, HK), lambda h, n: (n, h // 2)),
                  pl.BlockSpec((tB, HK), lambda h, n: (n, h)), pl.BlockSpec((tB, 128), lambda h, n: (n, 0))],
        out_specs=[pl.BlockSpec((tB, HK), lambda h, n: (n, h)),
                   pl.BlockSpec((None, nc, HK, HK), lambda h, n: (h, n, 0, 0)),
                   pl.BlockSpec((None, nc, L, L), lambda h, n: (h, n, 0, 0))],
        out_shape=[jax.ShapeDtypeStruct((T, N_GDN * HK), f32), jax.ShapeDtypeStruct((N_GDN, N, HK, HK), f32),
                   jax.ShapeDtypeStruct((N_GDN, N, L, L), f32)],
        scratch_shapes=[pltpu.VMEM((HK, HK), f32)],
        compiler_params=_cp(("parallel", "arbitrary")),
    )(qn, kn, vv, sm)


def gdn_bwd(qn, kn, vv, sm, st, ti, do):
    T = qn.shape[0]
    tB = min(GDN_BLOCK, T)
    nc = tB // L
    nb = T // tB

    def body(q_ref, k_ref, v_ref, sm_ref, st_ref, ti_ref, do_ref, dq_ref, dk_ref, dv_ref, dsm_ref, ds_scr):
        h = pl.program_id(0)

        @pl.when(pl.program_id(1) == 0)
        def _():
            ds_scr[...] = jnp.zeros_like(ds_scr)
        smv = sm_ref[...]
        lane = lax.broadcasted_iota(jnp.int32, smv.shape, 1)
        beta = jnp.sum(jnp.where(lane == h, smv, 0.0), axis=1, keepdims=True).reshape(nc, L, 1)
        g = jnp.sum(jnp.where(lane == 8 + h, smv, 0.0), axis=1, keepdims=True).reshape(nc, L, 1)
        q = q_ref[...].reshape(nc, L, HK)
        k = k_ref[...].reshape(nc, L, HK)
        v = v_ref[...].reshape(nc, L, HK)
        do = do_ref[...].reshape(nc, L, HK)
        s = st_ref[...]
        tm = _gdn_chunk_terms(q, k, v, beta, g, t_inv=ti_ref[...])
        eye, dc, ds_, eg, egl = tm["eye"], tm["dc"], tm["ds"], tm["eg"], tm["egl"]
        kb, u, w, qi, ks, tinv = tm["kb"], tm["u"], tm["w"], tm["qi"], tm["ks"], tm["t"]
        sb, dob = _b(s), _b(do)
        e = u - _bdot(_b(w), sb)
        eb = _b(e)
        egl_last = jnp.exp(tm["gl"])
        de0 = _bdot_tn(_b(tm["aqk"]), dob)
        ds0 = _bdot_tn(_b(qi), dob) - _bdot_tn(_b(w), _b(de0))
        kswb = _b(tm["ksw"])
        dsn = ds_scr[...]
        dsns = [None] * nc
        for ci in reversed(range(nc)):
            dsns[ci] = dsn
            dsn = ds0[ci] + (egl_last[ci] * dsn - _dot_tn(kswb[ci], _b(dsn)))
        ds_scr[...] = dsn
        dsp = jnp.stack(dsns, axis=0)
        dspb = _b(dsp)
        de = de0 + _bdot(_b(ks), dspb)
        deb = _b(de)
        dks = _bdot_nt(eb, dspb)
        dqi = _bdot_nt(dob, sb)
        daqk = jnp.where(tm["causal"], _bdot_nt(dob, eb), 0.0)
        dw = -_bdot_nt(deb, sb)
        tb = _b(tinv)
        dvv = _bdot_tn(tb, deb)
        dkg = _bdot_tn(tb, _b(dw))
        dm = -jnp.where(tm["strict"], _bdot_nt(_b(dvv), _b(u)) + _bdot_nt(_b(dkg), _b(w)), 0.0)
        x = _b(dm * ds_)
        y = _b(daqk * dc)
        kbb, kbf, qbf = _b(kb), _b(k), _b(q)
        dk = _bdot(x, kbb) + dkg * eg
        dkb = _bdot_tn(x, kbf) + _bdot_tn(y, qbf) + dks * egl
        dq = _bdot(y, kbb) + dqi * eg
        dk = dk + dkb * beta
        dbeta = jnp.sum(dkb * k, axis=-1, keepdims=True)
        z = dm * tm["m"] + daqk * tm["aqk"]
        dg = (jnp.sum(dqi * qi - dks * ks + dkg * tm["keg"], axis=-1, keepdims=True)
              + jnp.sum(z, axis=-1, keepdims=True) - _row_to_col(jnp.sum(z, axis=1, keepdims=True), eye))
        dgl = (egl_last * jnp.sum(jnp.sum(s * dsp, axis=2, keepdims=True), axis=1, keepdims=True)
               + jnp.sum(jnp.sum(dks * ks, axis=2, keepdims=True), axis=1, keepdims=True))
        rowi = lax.broadcasted_iota(jnp.int32, (nc, L, 1), 1)
        dg = dg + jnp.where(rowi == L - 1, dgl, 0.0)
        dg2 = dg.reshape(tB, 1)
        rin = lax.broadcasted_iota(jnp.int32, (tB, 1), 0) % L
        dla = _chunk_rev_cumsum(jnp.broadcast_to(dg2, (tB, 128)), jnp.broadcast_to(rin, (tB, 128)))
        dq_ref[...] = dq.reshape(tB, HK)
        dk_ref[...] = dk.reshape(tB, HK)
        dv_ref[...] = dvv.reshape(tB, HK)
        dsm_ref[...] = jnp.where(lane == h, dbeta.reshape(tB, 1), jnp.where(lane == 8 + h, dla, 0.0))

    rev = lambda n: nb - 1 - n
    return pl.pallas_call(
        body, name="gdn_bwd", grid=(N_GDN, nb),
        in_specs=[pl.BlockSpec((tB, HK), lambda h, n: (rev(n), h // 2)),
                  pl.BlockSpec((tB, HK), lambda h, n: (rev(n), h // 2)),
                  pl.BlockSpec((tB, HK), lambda h, n: (rev(n), h)), pl.BlockSpec((tB, 128), lambda h, n: (rev(n), 0)),
                  pl.BlockSpec((None, nc, HK, HK), lambda h, n: (h, rev(n), 0, 0)),
                  pl.BlockSpec((None, nc, L, L), lambda h, n: (h, rev(n), 0, 0)),
                  pl.BlockSpec((tB, HK), lambda h, n: (rev(n), h))],
        out_specs=[pl.BlockSpec((tB, HK), lambda h, n: (rev(n), h))] * 4,
        out_shape=[jax.ShapeDtypeStruct((T, N_GDN * HK), f32)] * 4,
        scratch_shapes=[pltpu.VMEM((HK, HK), f32)],
        compiler_params=_cp(("parallel", "arbitrary")),
    )(qn, kn, vv, sm, st, ti, do)


def _hgrn_prep(bq, bf_, bi, lb):
    tB = bq.shape[0]
    q = _silu(bq) * QK_SCALE
    sg = _sigmoid(bf_)
    f = lb + (1.0 - lb) * sg
    logf = jnp.log(jnp.maximum(f, MIN_F))
    rin = lax.broadcasted_iota(jnp.int32, (tB, HK), 0) % L
    g = _chunk_cumsum(logf, rin)
    return q, sg, f, 1.0 - f, bi, g, rin


def _hgrn_intra(q, kk, v, g, do=None):
    n = q.shape[0]
    nsub = L // SUB
    bwd = do is not None
    o_rows = [None] * nsub
    if bwd:
        dq_rows = [None] * nsub
        dkk_acc = jnp.zeros_like(kk)
        dv_acc = jnp.zeros_like(v)
    for i in range(1, nsub):
        lo, hi, w = i * SUB, (i + 1) * SUB, i * SUB
        ref = g[:, lo - 1:lo, :]
        eq = jnp.exp(g[:, lo:hi, :] - ref)
        ek = jnp.exp(ref - g[:, :w, :])
        qs = _b(q[:, lo:hi, :] * eq)
        ks = _b(kk[:, :w, :] * ek)
        p = _bdot_nt(qs, ks)
        o_rows[i] = _bdot(_b(p), _b(v[:, :w, :]))
        if bwd:
            dob = _b(do[:, lo:hi, :])
            dp = _b(_bdot_nt(dob, _b(v[:, :w, :])))
            dq_rows[i] = _bdot(dp, ks) * eq
            pad = jnp.zeros((n, L - w, HK), f32)
            dkk_acc = dkk_acc + jnp.concatenate([_bdot_tn(dp, qs) * ek, pad], axis=1)
            dv_acc = dv_acc + jnp.concatenate([_bdot_tn(_b(p), dob), pad], axis=1)
    m = n * nsub
    q4, k4, v4, g4 = (a.reshape(m, SUB, HK) for a in (q, kk, v, g))
    r = lax.broadcasted_iota(jnp.int32, (m, SUB, HK), 1)
    od = jnp.zeros((m, SUB, HK), f32)
    if bwd:
        do4 = do.reshape(m, SUB, HK)
        dqd = jnp.zeros((m, SUB, HK), f32)
        dkd = jnp.zeros((m, SUB, HK), f32)
        dvd = jnp.zeros((m, SUB, HK), f32)
    for j in range(SUB):
        gj, kj, vj = g4[:, j:j + 1, :], k4[:, j:j + 1, :], v4[:, j:j + 1, :]
        ok = r >= j
        e = jnp.where(ok, jnp.exp(jnp.where(ok, g4 - gj, 0.0)), 0.0)
        xq = q4 * e
        pj = jnp.sum(xq * kj, axis=-1, keepdims=True)
        od = od + pj * vj
        if bwd:
            dpj = jnp.sum(do4 * vj, axis=-1, keepdims=True)
            dqd = dqd + dpj * kj * e
            dkd = dkd + jnp.where(r == j, jnp.sum(dpj * xq, axis=1, keepdims=True), 0.0)
            dvd = dvd + jnp.where(r == j, jnp.sum(pj * do4, axis=1, keepdims=True), 0.0)
    od = od.reshape(n, L, HK)
    o = jnp.concatenate([od[:, :SUB, :]] + [od[:, i * SUB:(i + 1) * SUB, :] + o_rows[i] for i in range(1, nsub)], axis=1)
    if not bwd:
        return o
    dqd = dqd.reshape(n, L, HK)
    dq = jnp.concatenate([dqd[:, :SUB, :]] + [dqd[:, i * SUB:(i + 1) * SUB, :] + dq_rows[i] for i in range(1, nsub)], axis=1)
    return o, dq, dkk_acc + dkd.reshape(n, L, HK), dv_acc + dvd.reshape(n, L, HK)


def hgrn_fwd(p, lbs):
    T = p.shape[0]
    tB = min(256, T)
    nc = tB // L
    N = T // L

    def body(b_ref, lb_ref, o_ref, st_ref, s_scr):
        @pl.when(pl.program_id(1) == 0)
        def _():
            s_scr[...] = jnp.zeros_like(s_scr)
        q, sg, f, kk, v, g, rin = _hgrn_prep(b_ref[:, 0:HK], b_ref[:, HK:2 * HK], b_ref[:, 2 * HK:3 * HK], lb_ref[...])
        q3, k3, v3, g3 = (a.reshape(nc, L, HK) for a in (q, kk, v, g))
        o = _hgrn_intra(q3, k3, v3, g3)
        gl = g3[:, L - 1:L, :]
        qt = _b(q3 * jnp.exp(g3))
        kt = _b(k3 * jnp.exp(gl - g3))
        vb = _b(v3)
        st = s_scr[...]
        for c in range(nc):
            st_ref[c] = st
            o_ref[c * L:(c + 1) * L, :] = o[c] + _dot_nt(qt[c], _b(st))
            st = st * jnp.exp(gl[c]) + _dot_tn(vb[c], kt[c])
        s_scr[...] = st

    return pl.pallas_call(
        body, name="hgrn_fwd", grid=(N_HGRN, T // tB),
        in_specs=[pl.BlockSpec((tB, REG_BH), lambda h, n: (n, OFF_B // REG_BH + h)),
                  pl.BlockSpec((1, HK), lambda h, n: (0, h))],
        out_specs=[pl.BlockSpec((tB, HK), lambda h, n: (n, h)),
                   pl.BlockSpec((None, nc, HK, HK), lambda h, n: (h, n, 0, 0))],
        out_shape=[jax.ShapeDtypeStruct((T, N_HGRN * HK), f32), jax.ShapeDtypeStruct((N_HGRN, N, HK, HK), f32)],
        scratch_shapes=[pltpu.VMEM((HK, HK), f32)],
        compiler_params=_cp(("parallel", "arbitrary")),
    )(p, lbs)


def hgrn_bwd(p, lbs, st, do, dp):
    T = p.shape[0]
    tB = min(256, T)
    nc = tB // L
    nb = T // tB

    def body(b_ref, lb_ref, st_ref, do_ref, dp_in, dp_ref, dlb_ref, ds_scr):
        @pl.when(pl.program_id(1) == 0)
        def _():
            ds_scr[...] = jnp.zeros_like(ds_scr)
            dlb_ref[...] = jnp.zeros_like(dlb_ref)
        lb = lb_ref[...]
        bq = b_ref[:, 0:HK]
        q, sg, f, kk, v, g, rin = _hgrn_prep(bq, b_ref[:, HK:2 * HK], b_ref[:, 2 * HK:3 * HK], lb)
        q3, k3, v3, g3 = (a.reshape(nc, L, HK) for a in (q, kk, v, g))
        do3 = do_ref[...].reshape(nc, L, HK)
        dob = _b(do3)
        gl = g3[:, L - 1:L, :]
        egl = jnp.exp(gl)
        eg, egr = jnp.exp(g3), jnp.exp(gl - g3)
        qt, kt = q3 * eg, k3 * egr
        s = st_ref[...]
        ds0 = _bdot_tn(dob, _b(qt))
        dsn = ds_scr[...]
        dsns = [None] * nc
        for c in reversed(range(nc)):
            dsns[c] = dsn
            dsn = ds0[c] + dsn * egl[c]
        ds_scr[...] = dsn
        dsp = jnp.stack(dsns, axis=0)
        dspb = _b(dsp)
        dqt = _bdot(dob, _b(s))
        dkt = _bdot(_b(v3), dspb)
        dv_state = _bdot_nt(_b(kt), dspb)
        dgl = egl * jnp.sum(s * dsp, axis=1, keepdims=True) + jnp.sum(dkt * kt, axis=1, keepdims=True)
        _, dq_i, dkk_i, dv_i = _hgrn_intra(q3, k3, v3, g3, do=do3)
        dq = dq_i + dqt * eg
        dkk = dkk_i + dkt * egr
        dv = dv_i + dv_state
        rowi = lax.broadcasted_iota(jnp.int32, (nc, L, HK), 1)
        dg = q3 * dq - k3 * dkk + jnp.where(rowi == L - 1, dgl, 0.0)
        dlogf = _chunk_rev_cumsum(dg.reshape(tB, HK), rin)
        dkk2 = dkk.reshape(tB, HK)
        df = jnp.where(f > MIN_F, dlogf / f, 0.0) - dkk2
        dlb_ref[...] += jnp.sum(df * (1.0 - sg), axis=0, keepdims=True)
        dp_ref[:, 0:HK] = _b(dq.reshape(tB, HK) * QK_SCALE * _dsilu(bq))
        dp_ref[:, HK:2 * HK] = _b(df * (1.0 - lb) * sg * (1.0 - sg))
        dp_ref[:, 2 * HK:3 * HK] = _b(dv.reshape(tB, HK))

    rev = lambda n: nb - 1 - n
    bspec = pl.BlockSpec((tB, REG_BH), lambda h, n: (rev(n), OFF_B // REG_BH + h))
    return pl.pallas_call(
        body, name="hgrn_bwd", grid=(N_HGRN, nb),
        in_specs=[bspec, pl.BlockSpec((1, HK), lambda h, n: (0, h)),
                  pl.BlockSpec((None, nc, HK, HK), lambda h, n: (h, rev(n), 0, 0)),
                  pl.BlockSpec((tB, HK), lambda h, n: (rev(n), h)), pl.BlockSpec(memory_space=pl.ANY)],
        out_specs=[bspec, pl.BlockSpec((1, HK), lambda h, n: (0, h))],
        out_shape=[jax.ShapeDtypeStruct((T, NP), bf16), jax.ShapeDtypeStruct((1, N_HGRN * HK), f32)],
        input_output_aliases={4: 0},
        scratch_shapes=[pltpu.VMEM((HK, HK), f32)],
        compiler_params=_cp(("parallel", "arbitrary")),
    )(p, lbs, st, do, dp)


def _headnorm_fwd(o, z, w, nheads):
    outs = []
    for hh in range(nheads):
        sl = slice(hh * HK, (hh + 1) * HK)
        oh = o[:, sl]
        r = lax.rsqrt(jnp.mean(oh * oh, axis=-1, keepdims=True) + NORM_EPS)
        outs.append(oh * r * w * _silu(z[:, sl]))
    return jnp.concatenate(outs, axis=1)


def _headnorm_bwd(o, z, w, dy, nheads):
    dos, dzs = [], []
    dw = jnp.zeros((1, HK), f32)
    for hh in range(nheads):
        sl = slice(hh * HK, (hh + 1) * HK)
        oh, zh, dyh = o[:, sl], z[:, sl], dy[:, sl]
        r = lax.rsqrt(jnp.mean(oh * oh, axis=-1, keepdims=True) + NORM_EPS)
        on = oh * r
        sz = _silu(zh)
        dn = dyh * sz * w
        dos.append(r * (dn - on * jnp.mean(dn * on, axis=-1, keepdims=True)))
        dzs.append(dyh * on * w * _dsilu(zh))
        dw = dw + jnp.sum(dyh * sz * on, axis=0, keepdims=True)
    return jnp.concatenate(dos, axis=1), jnp.concatenate(dzs, axis=1), dw


def _merge_specs(tT, l):
    row = lambda w, cb=0: pl.BlockSpec((tT, w), lambda i, cb=cb: (i, cb))
    full = lambda r, c: pl.BlockSpec((r, c), lambda i: (0, 0))
    layer = lambda r, c: pl.BlockSpec((None, r, c), lambda i: (l, 0, 0))
    return row, full, layer


def merge_fwd(x, p, ya, ob, oc, hw, gw, bg, woa, wob, woc, wo, l):
    T = x.shape[0]
    tT = min(256, T)
    row, full, layer = _merge_specs(tT, l)

    def body(x_ref, pm_ref, ya_ref, ob_ref, oc_ref, hw_ref, gw_ref, bg_ref,
             woa_ref, wob_ref, woc_ref, wo_ref, out_ref):
        yb = _b(_headnorm_fwd(ob_ref[...], pm_ref[:, M_BZ:M_G], hw_ref[...], N_HGRN))
        yc = _b(_headnorm_fwd(oc_ref[...], pm_ref[:, M_CZ:REG_M], gw_ref[...], N_GDN))
        gates = _sigmoid(pm_ref[:, M_G:M_CZ] + bg_ref[...])
        merged = (gates[:, 0:D] * _dot(ya_ref[...], woa_ref[...]) + gates[:, D:2 * D] * _dot(yb, wob_ref[...])
                  + gates[:, 2 * D:3 * D] * _dot(yc, woc_ref[...]))
        out_ref[...] = x_ref[...] + _dot(_b(merged), wo_ref[...])

    return pl.pallas_call(
        body, name="merge_fwd", grid=(T // tT,),
        in_specs=[row(D), row(REG_M, OFF_M // REG_M),
                  row(512), row(512), row(1024), full(1, HK), full(1, HK), full(1, 3 * D),
                  layer(512, D), layer(512, D), layer(D, D), layer(D, D)],
        out_specs=row(D),
        out_shape=jax.ShapeDtypeStruct((T, D), f32),
        compiler_params=_cp(("parallel",)),
    )(x, p, ya, ob, oc, hw, gw, bg, woa, wob, woc, wo)


def merge_bwd(dxo, p, ya, ob, oc, hw, gw, bg, woa, wob, woc, wo, l):
    T = dxo.shape[0]
    tT = min(256, T)
    row, full, layer = _merge_specs(tT, l)

    def body(dx_ref, pm_ref, ya_ref, ob_ref, oc_ref, hw_ref, gw_ref, bg_ref,
             woa_ref, wob_ref, woc_ref, wo_ref,
             dya_ref, dob_ref, doc_ref, dp_ref, mg_ref, dy3_ref, yb_ref, yc_ref,
             dbg_ref, dhw_ref, dgw_ref):
        @pl.when(pl.program_id(0) == 0)
        def _():
            dbg_ref[...] = jnp.zeros_like(dbg_ref)
            dhw_ref[...] = jnp.zeros_like(dhw_ref)
            dgw_ref[...] = jnp.zeros_like(dgw_ref)
        ob, oc, bz, cz = ob_ref[...], oc_ref[...], pm_ref[:, M_BZ:M_G], pm_ref[:, M_CZ:REG_M]
        hw_, gw_ = hw_ref[...], gw_ref[...]
        yb = _b(_headnorm_fwd(ob, bz, hw_, N_HGRN))
        yc = _b(_headnorm_fwd(oc, cz, gw_, N_GDN))
        yb_ref[...] = yb
        yc_ref[...] = yc
        gates = _sigmoid(pm_ref[:, M_G:M_CZ] + bg_ref[...])
        ys = (_dot(ya_ref[...], woa_ref[...]), _dot(yb, wob_ref[...]), _dot(yc, woc_ref[...]))
        dmerged = _dot_nt(_b(dx_ref[...]), wo_ref[...])
        merged = jnp.zeros_like(dmerged)
        dys = []
        for i in range(3):
            gi = gates[:, i * D:(i + 1) * D]
            merged = merged + gi * ys[i]
            dyi = _b(dmerged * gi)
            dys.append(dyi)
            dy3_ref[:, i * D:(i + 1) * D] = dyi
            dgp = dmerged * ys[i] * gi * (1.0 - gi)
            dp_ref[:, M_G + i * D:M_G + (i + 1) * D] = _b(dgp)
            dbg_ref[:, i * D:(i + 1) * D] += jnp.sum(dgp, axis=0, keepdims=True)
        mg_ref[...] = _b(merged)
        dya_ref[...] = _dot_nt(dys[0], woa_ref[...])
        dob, dbz, dhw = _headnorm_bwd(ob, bz, hw_, _dot_nt(dys[1], wob_ref[...]), N_HGRN)
        doc, dcz, dgw = _headnorm_bwd(oc, cz, gw_, _dot_nt(dys[2], woc_ref[...]), N_GDN)
        dob_ref[...] = dob
        doc_ref[...] = doc
        dp_ref[:, M_BZ:M_G] = _b(dbz)
        dp_ref[:, M_CZ:REG_M] = _b(dcz)
        dhw_ref[...] += dhw
        dgw_ref[...] += dgw

    sd = jax.ShapeDtypeStruct
    return pl.pallas_call(
        body, name="merge_bwd", grid=(T // tT,),
        in_specs=[row(D), row(REG_M, OFF_M // REG_M),
                  row(512), row(512), row(1024), full(1, HK), full(1, HK), full(1, 3 * D),
                  layer(512, D), layer(512, D), layer(D, D), layer(D, D)],
        out_specs=[row(512), row(512), row(1024), row(REG_M, OFF_M // REG_M), row(D), row(3 * D), row(512),
                   row(1024), full(1, 3 * D), full(1, HK), full(1, HK)],
        out_shape=[sd((T, 512), f32), sd((T, 512), f32), sd((T, 1024), f32), sd((T, NP), bf16),
                   sd((T, D), bf16), sd((T, 3 * D), bf16), sd((T, 512), bf16),
                   sd((T, 1024), bf16), sd((1, 3 * D), f32), sd((1, HK), f32), sd((1, HK), f32)],
        compiler_params=_cp(("arbitrary",)),
    )(dxo, p, ya, ob, oc, hw, gw, bg, woa, wob, woc, wo)


def layer_fwd(x, w):
    l = w["l"]
    p, h = inproj_fwd(x, w["norm_w"], w["w_in"])
    ya = mixa_fwd(p, w["conv_a"])
    qn, kn, vv, sm = mixc_pre_fwd(p, w["conv_c"], w["alog_l"], w["dtb_l"])
    oc, st_c, ti = gdn_fwd(qn, kn, vv, sm)
    ob, st_b = hgrn_fwd(p, w["lbs"])
    if "late" in w:
        w.update(w.pop("late")(ob))
    xo = merge_fwd(x, p, ya, ob, oc, w["hgrn_norm_w"], w["gdn_norm_w"], w["b_gate"],
                   w["w_out_a"], w["w_out_b"], w["w_out_c"], w["w_o"], l)
    saved = dict(x=x, p=p, h=h, ya=ya, qn=qn, kn=kn, vv=vv, sm=sm, oc=oc, st_c=st_c, ti=ti, ob=ob, st_b=st_b)
    return xo, saved


OUT_MATS = (("w_out_a", "cols"), ("w_out_b", "cols"), ("w_out_c", "rows"), ("w_o", "rows"))


def layer_bwd(dxo, w, s, chip):
    p, l = s["p"], w["l"]
    (dya, dob, doc, dp, merged, dy3, yb, yc, dbg, dhw, dgw) = merge_bwd(
        dxo, p, s["ya"], s["ob"], s["oc"], w["hgrn_norm_w"], w["gdn_norm_w"], w["b_gate"],
        w["w_out_a"], w["w_out_b"], w["w_out_c"], w["w_o"], l)
    full = {"w_o": matmul_tn(merged, dxo, "dw_o", with_bf16=True),
            "w_out_a": matmul_tn(s["ya"], dy3, "dw_out_a", n=D, b_col0=0, with_bf16=True),
            "w_out_b": matmul_tn(yb, dy3, "dw_out_b", n=D, b_col0=D, with_bf16=True),
            "w_out_c": matmul_tn(yc, dy3, "dw_out_c", n=D, b_col0=2 * D, with_bf16=True)}
    out_kinds = [k for _, k in OUT_MATS]
    sent_out, token = exchange_start([full[n][1] for n, _ in OUT_MATS], out_kinds, f"grads_out_start{l}")
    dp, dlbs = hgrn_bwd(p, w["lbs"] + token[0:1, 0:1], s["st_b"], dob, dp)
    dq8, dk8, dvv, dsm8 = gdn_bwd(s["qn"], s["kn"], s["vv"], s["sm"], s["st_c"], s["ti"], doc)
    dp, dcc, dsmall = mixc_pre_bwd(p, w["conv_c"], w["alog_l"], w["dtb_l"], dq8, dk8, dvv, dsm8, dp)
    dp, dca = mixa_bwd(p, w["conv_a"], dya, dp)
    gf_win, gb_win = win_from_padded(matmul_tn(s["h"], dp, "dw_in"))
    sent_in, token = exchange_start([gb_win], ["slot"], f"grads_in_start{l}")
    dx, dnw = inproj_bwd(dp, w["w_in"], s["x"], w["norm_w"] + token[0:1, 0:1], dxo)
    recv_out = exchange_wait(sent_out, out_kinds, dx, f"grads_out_wait{l}")
    recv_in = exchange_wait(sent_in, ["slot"], dx, f"grads_in_wait{l}")
    half = {"w_in": partial_sum(gf_win, "slot", recv_in[0], chip, "psum_w_in", transposed=True)}
    for (n, kind), r in zip(OUT_MATS, recv_out):
        half[n] = partial_sum(full[n][0], kind, r, chip, "psum_" + n)
    small = dict(norm_w=dnw, b_gate=dbg, hgrn_norm_w=dhw, gdn_norm_w=dgw, lbs=dlbs, conv_a=dca[0:3], conv_c=dcc[0:4],
                 a_log=dsmall[0:1, 8:16], dt_bias=dsmall[1:2, 8:16])
    return dx, small, half


def lbs_fwd(lb):
    def body(lb_ref, o_ref):
        l0, l1 = lb_ref[0:1, :], lb_ref[1:2, :]
        mx = jnp.maximum(l0, l1)
        e0, e1 = jnp.exp(l0 - mx), jnp.exp(l1 - mx)
        o_ref[0:1, :] = jnp.zeros_like(l0)
        o_ref[1:2, :] = e1 / (e0 + e1)
    return pl.pallas_call(body, name="lbs_fwd", out_shape=jax.ShapeDtypeStruct(lb.shape, f32))(lb)


def _adam_math(w, g, m, v):
    mn = ADAM_B1 * m + (1.0 - ADAM_B1) * g
    vn = ADAM_B2 * v + (1.0 - ADAM_B2) * (g * g)
    mh = mn / (1.0 - ADAM_B1 ** ADAM_STEP)
    vh = vn / (1.0 - ADAM_B2 ** ADAM_STEP)
    return -ADAM_LR * (mh / (jnp.sqrt(vh) + ADAM_EPS) + ADAM_WD * w), mn, vn


def adam(w, g, m, v, name):
    R, C = w.shape
    tr = 256 if R % 256 == 0 else R

    def body(w_ref, g_ref, m_ref, v_ref, d_ref, mo_ref, vo_ref):
        d, mn, vn = _adam_math(w_ref[...], g_ref[...], m_ref[...], v_ref[...])
        d_ref[...] = d
        mo_ref[...] = mn
        vo_ref[...] = vn

    spec = pl.BlockSpec((tr, C), lambda i: (i, 0))
    return pl.pallas_call(
        body, name=name, grid=(R // tr,), in_specs=[spec] * 4, out_specs=[spec] * 3,
        out_shape=[jax.ShapeDtypeStruct((R, C), f32)] * 3, compiler_params=_cp(("parallel",)),
    )(w, g, m, v)


def adam_pair(h, hs, w, m, v, name):
    _, R, C = w.shape
    cp = h[0].shape[1]
    tr = 128 if R % 128 == 0 else R
    nt = R // tr

    def body(h0_ref, h1_ref, s0_ref, s1_ref, w_ref, m_ref, v_ref, g_ref, d_ref, mo_ref, vo_ref):
        def update(h_ref, s_ref):
            g = (h_ref[...] + s_ref[...])[:, :C]
            d, mn, vn = _adam_math(w_ref[...], g, m_ref[...], v_ref[...])
            g_ref[...] = g
            d_ref[...] = d
            mo_ref[...] = mn
            vo_ref[...] = vn

        @pl.when(pl.program_id(0) == 0)
        def _():
            update(h0_ref, s0_ref)

        @pl.when(pl.program_id(0) == 1)
        def _():
            update(h1_ref, s1_ref)

    h0spec = pl.BlockSpec((tr, cp), lambda l, i: (jnp.where(l == 0, i, nt - 1), 0))
    h1spec = pl.BlockSpec((tr, cp), lambda l, i: (jnp.where(l == 1, i, 0), 0))
    spec = pl.BlockSpec((None, tr, C), lambda l, i: (l, i, 0))
    return pl.pallas_call(
        body, name=name, grid=(2, nt), in_specs=[h0spec, h1spec, h0spec, h1spec, spec, spec, spec],
        out_specs=[spec] * 4, out_shape=[jax.ShapeDtypeStruct(w.shape, f32)] * 4,
        compiler_params=_cp(("arbitrary", "arbitrary")),
    )(h[0], h[1], hs[0], hs[1], w, m, v)


_SMALL = (("norm_w", 2 * D), ("b_gate", 6 * D), ("lower_bounds", None), ("hgrn_norm_w", 2 * HK),
          ("gdn_norm_w", 2 * HK), ("a_log", 16), ("dt_bias", 16), ("final_norm_w", D), ("loss", None))
_CONV = (("conv_a", 2 * 3 * 512), ("conv_c", 2 * 4 * 2048))


def _small_rows(n):
    return 16 if n is None else -(-n // 1024) * 8


LB_ROW = sum(_small_rows(n) for _, n in _SMALL[:2])
ADAM_ROWS = sum(_small_rows(n) for _, n in _SMALL)
SMALL_ROWS = ADAM_ROWS + sum(_small_rows(n) for _, n in _CONV)


def small_update(parts, wp, mp, vp):
    def body(p_ref, w_ref, m_ref, v_ref, g_ref, d_ref, mo_ref, vo_ref):
        gs = p_ref[0]
        for i in range(1, 8):
            gs = gs + p_ref[i]
        w = w_ref[...]
        l0, l1 = w[LB_ROW:LB_ROW + 8], w[LB_ROW + 8:LB_ROW + 16]
        mx = jnp.maximum(l0, l1)
        e0, e1 = jnp.exp(l0 - mx), jnp.exp(l1 - mx)
        p0, p1 = e0 / (e0 + e1), e1 / (e0 + e1)
        dl1 = gs[LB_ROW + 8:LB_ROW + 16]
        s = p1 * dl1
        g = jnp.concatenate([gs[0:LB_ROW], -p0 * s, p1 * dl1 - p1 * s, gs[LB_ROW + 16:ADAM_ROWS]], axis=0)
        d, mn, vn = _adam_math(w, g, m_ref[...], v_ref[...])
        g_ref[0:ADAM_ROWS, :] = g
        g_ref[ADAM_ROWS:, :] = gs[ADAM_ROWS:]
        d_ref[...] = d
        mo_ref[...] = mn
        vo_ref[...] = vn
    sd = jax.ShapeDtypeStruct
    return pl.pallas_call(body, name="small_update",
                          out_shape=[sd((SMALL_ROWS, 128), f32)] + [sd((ADAM_ROWS, 128), f32)] * 3)(parts, wp, mp, vp)


def partial_sum(own, kind, recv, chip, name, transposed=False):
    _, r, c = recv.shape
    tr = 256 if r % 256 == 0 else r

    def body(chip_ref, o_ref, r_ref, out_ref):
        s = ((o_ref[...] + r_ref[0].astype(f32)) + r_ref[1].astype(f32)) + r_ref[2].astype(f32)
        out_ref[...] = s.T if transposed else s

    own_spec = {"slot": pl.BlockSpec((None, tr, c), lambda i, chip: (chip[0], i, 0)),
                "cols": pl.BlockSpec((tr, c), lambda i, chip: (i, chip[0])),
                "rows": pl.BlockSpec((tr, c), lambda i, chip: (chip[0] * (r // tr) + i, 0))}[kind]
    out_spec = pl.BlockSpec((c, tr), lambda i, chip: (0, i)) if transposed else pl.BlockSpec((tr, c), lambda i, chip: (i, 0))
    return pl.pallas_call(
        body, name=name,
        grid_spec=pltpu.PrefetchScalarGridSpec(
            num_scalar_prefetch=1, grid=(r // tr,),
            in_specs=[own_spec, pl.BlockSpec((3, tr, c), lambda i, chip: (0, i, 0))], out_specs=out_spec),
        out_shape=jax.ShapeDtypeStruct((c, r) if transposed else (r, c), f32), compiler_params=_cp(("arbitrary",)),
    )(chip, own, recv)


def adam_pair_t(h, hs, wt, mt, vt, name):
    C, _, R = wt.shape
    tc = 128

    def body(h0_ref, h1_ref, s0_ref, s1_ref, w_ref, m_ref, v_ref, g_ref, d_ref, mo_ref, vo_ref):
        g = jnp.stack([h0_ref[...] + s0_ref[...], h1_ref[...] + s1_ref[...]], axis=1)
        d, mn, vn = _adam_math(w_ref[...], g, m_ref[...], v_ref[...])
        g_ref[...] = g
        d_ref[...] = d
        mo_ref[...] = mn
        vo_ref[...] = vn

    hspec = pl.BlockSpec((tc, R), lambda i: (i, 0))
    spec = pl.BlockSpec((tc, 2, R), lambda i: (i, 0, 0))
    return pl.pallas_call(
        body, name=name, grid=(pl.cdiv(C, tc),), in_specs=[hspec] * 4 + [spec] * 3, out_specs=[spec] * 4,
        out_shape=[jax.ShapeDtypeStruct(wt.shape, f32)] * 4, compiler_params=_cp(("parallel",)),
    )(h[0], h[1], hs[0], hs[1], wt, mt, vt)


MESH = pl.DeviceIdType.MESH
_HBM = pl.BlockSpec(memory_space=pltpu.HBM)


def _place():
    return lax.axis_index("x"), lax.axis_index("y"), lax.axis_index("c")


def weight_gather(arrs):
    n = len(arrs)

    def body(*refs):
        x_refs, out_refs = refs[:n], refs[n:2 * n]
        send_sems, recv_sems, local_sems = refs[2 * n:]
        x, y, c = _place()
        me, sibling = (x, y, c), (x, y, 1 - c)
        chips = [(1 - x, y), (x, 1 - y), (1 - x, 1 - y)]

        def copy(a, k, block, to, own_src=False):
            px, py, pc = block
            dst = out_refs[a].at[2 * px + py, pc]
            return pltpu.make_async_remote_copy(
                src_ref=x_refs[a].at[c] if own_src else dst, dst_ref=dst,
                send_sem=send_sems.at[7 * a + k], recv_sem=recv_sems.at[7 * a + k], device_id=to, device_id_type=MESH)

        mine = [pltpu.make_async_copy(x_refs[a].at[c], out_refs[a].at[2 * x + y, c], local_sems.at[a])
                for a in range(n)]
        for cp in mine:
            cp.start()
        first = []
        for a in range(n):
            first.append(copy(a, 0, me, sibling, own_src=True))
            first += [copy(a, 1 + j, me, (*chip, c), own_src=True) for j, chip in enumerate(chips)]
        for cp in first:
            cp.start()
        passed = []
        for j, chip in enumerate(chips):
            for a in range(n):
                copy(a, 1 + j, (*chip, c), me).wait_recv()
                fwd = copy(a, 4 + j, (*chip, c), sibling)
                fwd.start()
                passed.append(fwd)
        for a in range(n):
            copy(a, 0, sibling, me).wait_recv()
            for j, chip in enumerate(chips):
                copy(a, 4 + j, (*chip, 1 - c), me).wait_recv()
        for cp in first + passed:
            cp.wait_send()
        for cp in mine:
            cp.wait()

    return pl.pallas_call(
        body, name="weight_gather", in_specs=[_HBM] * n, out_specs=[_HBM] * n,
        out_shape=[jax.ShapeDtypeStruct((N_CHIPS,) + a.shape, a.dtype) for a in arrs],
        scratch_shapes=[pltpu.SemaphoreType.DMA((7 * n,)), pltpu.SemaphoreType.DMA((7 * n,)),
                        pltpu.SemaphoreType.DMA((n,))],
    )(*arrs)


SHARD_W = 256


_SEM = pl.BlockSpec(memory_space=pltpu.SEMAPHORE)
_EFFECT = pltpu.SideEffectType.DATAFLOW_SIDE_EFFECTING


def _landing_shape(a, kind):
    if kind == "all":
        return (N_CHIPS,) + a.shape
    if kind == "slot":
        return (3,) + a.shape[1:]
    return (3,) + ((a.shape[0], SHARD_W) if kind == "cols" else (SHARD_W, a.shape[1]))


def _shard_copies(src_refs, land_refs, kinds, send_sems, recv_sems):
    x, y, c = _place()
    copies = []
    for a, (src, land, kind) in enumerate(zip(src_refs, land_refs, kinds)):
        for j, (px, py) in enumerate(((1 - x, y), (x, 1 - y), (1 - x, 1 - y))):
            q = 2 * px + py
            lo = pl.multiple_of(q * SHARD_W, SHARD_W)
            part = {"slot": lambda: src.at[q], "cols": lambda: src.at[:, pl.ds(lo, SHARD_W)],
                    "rows": lambda: src.at[pl.ds(lo, SHARD_W), :], "all": lambda: src}[kind]()
            k = 3 * a + j
            copies.append(pltpu.make_async_remote_copy(
                src_ref=part, dst_ref=land.at[2 * x + y] if kind == "all" else land.at[j],
                send_sem=send_sems.at[k], recv_sem=recv_sems.at[k], device_id=(px, py, c), device_id_type=MESH))
    return copies


def exchange_start(srcs, kinds, name, after=None):
    n = len(srcs)
    lands = [lax.empty(_landing_shape(a, k), a.dtype) for a, k in zip(srcs, kinds)]
    extra = [] if after is None else [after]

    def body(*refs):
        src_refs, land_refs, token = refs[:n], refs[n:2 * n], refs[-1]
        send_sems, recv_sems = refs[2 * n + len(extra)], refs[2 * n + len(extra) + 1]
        for cp in _shard_copies(src_refs, land_refs, kinds, send_sems, recv_sems):
            cp.start()
        token[...] = jnp.zeros_like(token)

    both = list(srcs) + lands
    out = pl.pallas_call(
        body, name=name,
        out_shape=(pltpu.SemaphoreType.DMA((3 * n,)), pltpu.SemaphoreType.DMA((3 * n,)),
                   *[pltpu.HBM(a.shape, a.dtype) for a in both], jax.ShapeDtypeStruct((8, 128), f32)),
        in_specs=[_HBM] * (2 * n) + [pl.BlockSpec(memory_space=pl.ANY)] * len(extra),
        out_specs=(_SEM, _SEM, *[_HBM] * (2 * n), pl.BlockSpec(memory_space=pltpu.VMEM)),
        input_output_aliases={i: 2 + i for i in range(2 * n)},
        compiler_params=pltpu.CompilerParams(has_side_effects=_EFFECT),
    )(*[pltpu.with_memory_space_constraint(a, pltpu.HBM) for a in both], *extra)
    return (out[0], out[1], out[2:2 + 2 * n]), out[-1]


def exchange_wait(handle, kinds, after, name):
    send_sems, recv_sems, both = handle
    n = len(kinds)

    def body(*refs):
        src_refs, land_refs, s_sems, r_sems = refs[:n], refs[n:2 * n], refs[2 * n], refs[2 * n + 1]
        for cp in _shard_copies(src_refs, land_refs, kinds, s_sems, r_sems):
            cp.wait_send()
            cp.wait_recv()

    out = pl.pallas_call(
        body, name=name, out_shape=tuple(pltpu.HBM(a.shape, a.dtype) for a in both),
        in_specs=[_HBM] * (2 * n) + [_SEM, _SEM, pl.BlockSpec(memory_space=pl.ANY)], out_specs=tuple([_HBM] * (2 * n)),
        input_output_aliases={i: i for i in range(2 * n)},
        compiler_params=pltpu.CompilerParams(has_side_effects=_EFFECT),
    )(*both, send_sems, recv_sems, after)
    return out[n:]


def final_exchange(hs, small):
    n = len(hs)
    S = small.shape[0]

    def body(*refs):
        h_refs, sm_ref, out_refs, smalls_ref = refs[:n], refs[n], refs[n + 1:2 * n + 1], refs[2 * n + 1]
        send_sems, recv_sems, local_sem = refs[2 * n + 2:]
        x, y, c = _place()
        my_slot = smalls_ref.at[4 * x + 2 * y + c]
        mine = pltpu.make_async_copy(sm_ref, my_slot, local_sem)
        mine.start()
        copies = [pltpu.make_async_remote_copy(src_ref=h_refs[a], dst_ref=out_refs[a], send_sem=send_sems.at[a],
                                               recv_sem=recv_sems.at[a], device_id=(x, y, 1 - c), device_id_type=MESH)
                  for a in range(n)]
        for mask in range(1, 8):
            fx, fy, fc = (mask >> 2) & 1, (mask >> 1) & 1, mask & 1
            peer = ((1 - x) if fx else x, (1 - y) if fy else y, (1 - c) if fc else c)
            copies.append(pltpu.make_async_remote_copy(
                src_ref=sm_ref, dst_ref=my_slot, send_sem=send_sems.at[n - 1 + mask], recv_sem=recv_sems.at[n - 1 + mask],
                device_id=peer, device_id_type=MESH))
        for cp in copies:
            cp.start()
        for cp in copies:
            cp.wait_recv()
        for cp in copies:
            cp.wait_send()
        mine.wait()

    sd = jax.ShapeDtypeStruct
    out = pl.pallas_call(
        body, name="final_exchange", in_specs=[_HBM] * (n + 1), out_specs=[_HBM] * (n + 1),
        out_shape=[sd(h.shape, h.dtype) for h in hs] + [sd((8, S, 128), f32)],
        scratch_shapes=[pltpu.SemaphoreType.DMA((n + 7,)), pltpu.SemaphoreType.DMA((n + 7,)),
                        pltpu.SemaphoreType.DMA],
    )(*hs, small)
    return out[:n], out[n]


N_CHIPS = 4
SHARD_COLS = N_ORIG // N_CHIPS


SHARD_PAD = 2688
_COL_SEGMENTS = (
    ((0, 2048, OFF_A),)
    + tuple((2048 + 512 * j + HK * h, 2048 + 512 * j + HK * (h + 1), OFF_B + REG_BH * h + HK * j)
            for j in range(3) for h in range(N_HGRN))
    + ((3584, 4096, OFF_M + M_BZ), (4096, 6144, OFF_C), (6144, 6160, OFF_C + 2048), (6160, 7184, OFF_M + M_CZ),
       (7184, N_ORIG, OFF_M + M_G)))


def _shard_pieces():
    pieces = []
    for lo, hi, dst in _COL_SEGMENTS:
        for p in range(N_CHIPS):
            a, b = max(lo, p * SHARD_COLS), min(hi, (p + 1) * SHARD_COLS)
            if a < b:
                pieces.append((p, a - p * SHARD_COLS, dst + a - lo, b - a))
    return pieces


def win_cast_pad(wt):
    tc = 128

    def body(x_ref, o0_ref, o1_ref):
        col = pl.program_id(0) * tc + lax.broadcasted_iota(jnp.int32, (tc, 1), 0)
        for l, o_ref in enumerate((o0_ref, o1_ref)):
            o_ref[...] = _b(jnp.where(col < SHARD_COLS, x_ref[:, l, :], 0.0).T)

    spec = pl.BlockSpec((D, tc), lambda i: (0, i))
    return pl.pallas_call(
        body, name="win_cast_pad", grid=(SHARD_PAD // tc,),
        in_specs=[pl.BlockSpec((tc, 2, D), lambda i: (i, 0, 0))], out_specs=[spec, spec],
        out_shape=[jax.ShapeDtypeStruct((D, SHARD_PAD), bf16)] * 2, compiler_params=_cp(("parallel",)),
    )(wt)


def win_to_padded(w4, name):
    tr = 256
    pieces = _shard_pieces()

    def body(a_ref, o_ref):
        o_ref[...] = jnp.zeros((tr, NP), bf16)
        for p, j0, c0, n in pieces:
            o_ref[:, c0:c0 + n] = a_ref[p, :, j0:j0 + n]

    return pl.pallas_call(
        body, name=name, grid=(D // tr,),
        in_specs=[pl.BlockSpec((N_CHIPS, tr, SHARD_PAD), lambda i: (0, i, 0))],
        out_specs=pl.BlockSpec((tr, NP), lambda i: (i, 0)),
        out_shape=jax.ShapeDtypeStruct((D, NP), bf16), compiler_params=_cp(("parallel",)),
    )(w4)


def win_from_padded(dw):
    tr = 128
    pieces = _shard_pieces()

    def body(d_ref, of_ref, ob_ref):
        for p in range(N_CHIPS):
            of_ref[p, :, SHARD_COLS:] = jnp.zeros((tr, SHARD_PAD - SHARD_COLS), f32)
            ob_ref[p, :, SHARD_COLS:] = jnp.zeros((tr, SHARD_PAD - SHARD_COLS), bf16)
        for p, j0, c0, n in pieces:
            v = d_ref[:, c0:c0 + n]
            of_ref[p, :, j0:j0 + n] = v
            ob_ref[p, :, j0:j0 + n] = _b(v)

    out_spec = pl.BlockSpec((N_CHIPS, tr, SHARD_PAD), lambda i: (0, i, 0))
    return pl.pallas_call(
        body, name="win_from_padded", grid=(D // tr,),
        in_specs=[pl.BlockSpec((tr, NP), lambda i: (i, 0))], out_specs=[out_spec, out_spec],
        out_shape=[jax.ShapeDtypeStruct((N_CHIPS, D, SHARD_PAD), f32),
                   jax.ShapeDtypeStruct((N_CHIPS, D, SHARD_PAD), bf16)],
        compiler_params=_cp(("parallel",)),
    )(dw)


def _rows128(a):
    flat = a.reshape(-1)
    total = -(-flat.shape[0] // 1024) * 1024
    return jnp.pad(flat, (0, total - flat.shape[0])).reshape(total // 128, 128)


def _lb_rows(lb):
    return jnp.pad(lb.reshape(2, 4, 128), ((0, 0), (0, 4), (0, 0))).reshape(16, 128)


def _pack_small(v, with_conv):
    rows = []
    for name, n in _SMALL + (_CONV if with_conv else ()):
        if name == "lower_bounds":
            rows.append(_lb_rows(v[name]))
        elif name == "loss":
            rows.append(jnp.broadcast_to(v[name], (16, 128)) if name in v else jnp.zeros((16, 128), f32))
        else:
            rows.append(_rows128(v[name]))
    return jnp.concatenate(rows, axis=0)


def _unpack_small(p, shapes, with_conv):
    out, row = {}, 0
    for name, n in _SMALL + (_CONV if with_conv else ()):
        nrows = _small_rows(n)
        blk = p[row:row + nrows]
        if name == "lower_bounds":
            out[name] = blk.reshape(2, 8, 128)[:, :4].reshape(2, 512)
        elif name == "loss":
            out[name] = blk[0, 0]
        else:
            out[name] = blk.reshape(-1)[:n].reshape(shapes[name])
        row += nrows
    return out


def _lane_vec(a8):
    return jnp.pad(a8.reshape(1, 8), ((0, 0), (8, 112)))


WEIGHT_NAMES = ("norm_w", "w_in", "b_gate", "conv_a", "conv_c", "a_log", "dt_bias", "lower_bounds", "hgrn_norm_w",
                "gdn_norm_w", "w_out_a", "w_out_b", "w_out_c", "w_o", "final_norm_w")


def kernel(x, norm_w, w_in, b_gate, conv_a, conv_c, a_log, dt_bias, lower_bounds, hgrn_norm_w, gdn_norm_w, w_out_a, w_out_b, w_out_c, w_o, final_norm_w, loss_target, m_norm_w, m_w_in, m_b_gate, m_conv_a, m_conv_c, m_a_log, m_dt_bias, m_lower_bounds, m_hgrn_norm_w, m_gdn_norm_w, m_w_out_a, m_w_out_b, m_w_out_c, m_w_o, m_final_norm_w, v_norm_w, v_w_in, v_b_gate, v_conv_a, v_conv_c, v_a_log, v_dt_bias, v_lower_bounds, v_hgrn_norm_w, v_gdn_norm_w, v_w_out_a, v_w_out_b, v_w_out_c, v_w_o, v_final_norm_w):
    wts = dict(norm_w=norm_w, w_in=w_in, b_gate=b_gate, conv_a=conv_a, conv_c=conv_c, a_log=a_log, dt_bias=dt_bias,
               lower_bounds=lower_bounds, hgrn_norm_w=hgrn_norm_w, gdn_norm_w=gdn_norm_w, w_out_a=w_out_a,
               w_out_b=w_out_b, w_out_c=w_out_c, w_o=w_o, final_norm_w=final_norm_w)
    mom = dict(norm_w=m_norm_w, w_in=m_w_in, b_gate=m_b_gate, conv_a=m_conv_a, conv_c=m_conv_c, a_log=m_a_log,
               dt_bias=m_dt_bias, lower_bounds=m_lower_bounds, hgrn_norm_w=m_hgrn_norm_w, gdn_norm_w=m_gdn_norm_w,
               w_out_a=m_w_out_a, w_out_b=m_w_out_b, w_out_c=m_w_out_c, w_o=m_w_o, final_norm_w=m_final_norm_w)
    var = dict(norm_w=v_norm_w, w_in=v_w_in, b_gate=v_b_gate, conv_a=v_conv_a, conv_c=v_conv_c, a_log=v_a_log,
               dt_bias=v_dt_bias, lower_bounds=v_lower_bounds, hgrn_norm_w=v_hgrn_norm_w, gdn_norm_w=v_gdn_norm_w,
               w_out_a=v_w_out_a, w_out_b=v_w_out_b, w_out_c=v_w_out_c, w_o=v_w_o, final_norm_w=v_final_norm_w)
    chip = 2 * lax.axis_index("x") + lax.axis_index("y")
    chip1 = chip.reshape(1).astype(jnp.int32)

    win_l0, win_l1 = win_cast_pad(jnp.transpose(w_in, (2, 0, 1)))
    win4_l0, ca4, cc4 = weight_gather([win_l0.reshape(2, D // 2, SHARD_PAD), conv_a, conv_c])
    by_cols = lambda a: a.transpose(1, 2, 0, 3).reshape(a.shape[1], a.shape[2], N_CHIPS * a.shape[3])
    by_rows = lambda a: a.transpose(1, 0, 2, 3).reshape(a.shape[1], N_CHIPS * a.shape[2], a.shape[3])
    conv_a_full, conv_c_full = by_cols(ca4), by_cols(cc4)
    later = [win_l1, _b(w_out_a), _b(w_out_b), _b(w_out_c), _b(w_o)]
    sent_w, token = exchange_start(later, ["all"] * 5, "weights_start", after=win4_l0)

    def late_weights(after):
        lands = exchange_wait(sent_w, ["all"] * 5, after, "weights_wait")
        l1, woa4, wob4, woc4, wo4 = (lax.dynamic_update_index_in_dim(land, own, chip, 0)
                                     for land, own in zip(lands, later))
        outs = dict(w_out_a=by_cols(woa4), w_out_b=by_cols(wob4), w_out_c=by_rows(woc4), w_o=by_rows(wo4))
        layers[1].update(outs, w_in=win_to_padded(l1, "win_to_padded1"))
        return outs

    lbs = lbs_fwd(lower_bounds)
    layers = []
    for l in range(2):
        layers.append(dict(
            l=l, norm_w=norm_w[l:l + 1], b_gate=b_gate[l:l + 1], conv_a=conv_a_full[l], conv_c=conv_c_full[l],
            alog_l=_lane_vec(a_log[l]), dtb_l=_lane_vec(dt_bias[l]), lbs=lbs[l:l + 1],
            hgrn_norm_w=hgrn_norm_w[l:l + 1], gdn_norm_w=gdn_norm_w[l:l + 1]))
    layers[0].update(w_in=win_to_padded(win4_l0.reshape(N_CHIPS, D, SHARD_PAD), "win_to_padded0"), late=late_weights,
                     norm_w=norm_w[0:1] + token[0:1, 0:1])

    xs, saved = x[0], []
    for l in range(2):
        xs, s = layer_fwd(xs, layers[l])
        saved.append(s)
    loss_row, dx, dfw = loss_head(xs, final_norm_w.reshape(1, D), loss_target[0])
    lg, half = [None, None], [None, None]
    for l in (1, 0):
        dx, lg[l], half[l] = layer_bwd(dx, layers[l], saved[l], chip1)
    grad_x = dx[None]

    stack = lambda n: jnp.stack([lg[0][n], lg[1][n]], axis=0)
    gsmall = {n: stack(n) for n in ("norm_w", "b_gate", "hgrn_norm_w", "gdn_norm_w", "a_log", "dt_bias", "conv_a",
                                    "conv_c")}
    gsmall.update(lower_bounds=stack("lbs"), final_norm_w=dfw, loss=loss_row)
    mat_names = ("w_in",) + tuple(n for n, _ in OUT_MATS)
    mine = [half[l][n] for n in mat_names for l in range(2)]
    theirs, smalls = final_exchange(mine, _pack_small(gsmall, True))

    out_g, out_d, out_m, out_v = {}, {}, {}, {}
    for i, n in enumerate(mat_names):
        h, hs = mine[2 * i:2 * i + 2], theirs[2 * i:2 * i + 2]
        if n == "w_in":
            fwd, back = (lambda a: jnp.transpose(a, (2, 0, 1))), (lambda a: jnp.transpose(a, (1, 2, 0)))
            res = adam_pair_t(h, hs, fwd(wts[n]), fwd(mom[n]), fwd(var[n]), "adam_" + n)
            out_g[n], out_d[n], out_m[n], out_v[n] = (back(a) for a in res)
        else:
            out_g[n], out_d[n], out_m[n], out_v[n] = adam_pair(h, hs, wts[n], mom[n], var[n], "adam_" + n)
    small_names = [n for n, _ in _SMALL if n != "loss"]
    pack = lambda v: _pack_small({n: v[n] for n in small_names}, False)
    sg, sd, smn, svn = small_update(smalls, pack(wts), pack(mom), pack(var))
    shapes = {n: wts[n].shape for n in small_names}
    shapes.update(conv_a=(2, 3, 512), conv_c=(2, 4, 2048))
    for dst, src, conv in ((out_g, sg, True), (out_d, sd, False), (out_m, smn, False), (out_v, svn, False)):
        dst.update(_unpack_small(src, shapes, conv))
    loss = out_g.pop("loss")
    for n in ("conv_a", "conv_c"):
        width = wts[n].shape[2]
        g = lax.dynamic_slice_in_dim(out_g[n], chip * width, width, axis=2)
        two_d = lambda a: a.reshape(-1, width)
        d, mn, vn = adam(two_d(wts[n]), two_d(g), two_d(mom[n]), two_d(var[n]), "adam_" + n)
        out_g[n] = g
        out_d[n], out_m[n], out_v[n] = (a.reshape(wts[n].shape) for a in (d, mn, vn))
    return (loss, grad_x, *[out_g[n] for n in WEIGHT_NAMES], *[out_d[n] for n in WEIGHT_NAMES],
            *[out_m[n] for n in WEIGHT_NAMES], *[out_v[n] for n in WEIGHT_NAMES])
```

```python
import functools

import jax
import jax.numpy as jnp
from jax import lax
from jax.experimental import pallas as pl
from jax.experimental.pallas import tpu as pltpu

f32 = jnp.float32
bf16 = jnp.bfloat16

D = 1024
L = 64
SUB = 16
NORM_EPS = 1e-6
L2_EPS = 1e-6
MIN_F = 1e-30
HK = 128
QK_SCALE = HK ** -0.5
N_GDN = 8
GDN_BLOCK = 1024
N_HGRN = 4

REG_A = 2304
REG_C = 2304
REG_M = 4608
REG_BH = 384
OFF_A, OFF_C, OFF_M, OFF_B = 0, 2304, 4608, 9216
M_BZ, M_G, M_CZ = 0, 512, 3584
NP = 10752
NP_TILE = 1536
N_ORIG = 10256

ADAM_LR, ADAM_B1, ADAM_B2, ADAM_EPS, ADAM_WD, ADAM_STEP = 0.001, 0.9, 0.999, 1e-08, 0.01, 10

VMEM_LIMIT = 56 * 1024 * 1024


def _cp(sem):
    return pltpu.CompilerParams(dimension_semantics=sem, vmem_limit_bytes=VMEM_LIMIT)


def _sigmoid(x):
    return jax.nn.sigmoid(x)


def _silu(x):
    return x * _sigmoid(x)


def _silu2(x):
    s = _sigmoid(x)
    y = x * s
    return y, s + y * (1.0 - s)


def _softplus(x):
    u = jnp.exp(-jnp.abs(x))
    w = 1.0 + u
    l1p = jnp.where(w == 1.0, u, jnp.log(w) * (u / (w - 1.0)))
    return jnp.maximum(x, 0.0) + l1p


def _dot(a, b):
    return jnp.dot(a, b, preferred_element_type=f32)


def _dot_nt(a, b):
    return lax.dot_general(a, b, (((1,), (1,)), ((), ())), preferred_element_type=f32)


def _dot_tn(a, b):
    return lax.dot_general(a, b, (((0,), (0,)), ((), ())), preferred_element_type=f32)


def _bdot(a, b):
    return lax.dot_general(a, b, (((2,), (1,)), ((0,), (0,))), preferred_element_type=f32)


def _bdot_nt(a, b):
    return lax.dot_general(a, b, (((2,), (2,)), ((0,), (0,))), preferred_element_type=f32)


def _bdot_tn(a, b):
    return lax.dot_general(a, b, (((1,), (1,)), ((0,), (0,))), preferred_element_type=f32)


def _bdot_split(a, b):
    ah, bh = _b(a), _b(b)
    al, bl = _b(a - ah.astype(f32)), _b(b - bh.astype(f32))
    return _bdot(ah, bh) + (_bdot(ah, bl) + _bdot(al, bh))


def _b(x):
    return x.astype(bf16)


def _chunk_cumsum(x, rows_in_chunk):
    n = x.shape[0]
    for s in (1, 2, 4, 8, 16, 32):
        x = x + jnp.where(rows_in_chunk >= s, pltpu.roll(x, s, axis=0), 0.0)
    return x


def _chunk_rev_cumsum(x, rows_in_chunk):
    n = x.shape[0]
    for s in (1, 2, 4, 8, 16, 32):
        x = x + jnp.where(rows_in_chunk + s < L, pltpu.roll(x, n - s, axis=0), 0.0)
    return x


def _shift_down(x, s):
    return pltpu.roll(x, s, axis=0) if s else x


def _shift_up(x, s):
    return pltpu.roll(x, x.shape[0] - s, axis=0) if s else x


P_MAIN = REG_A + REG_C + N_HGRN * REG_BH
PB_OFF = REG_A + REG_C
FWD_TILE = NP_TILE // 2


def inproj_fwd(x, nw, w):
    T = x.shape[0]
    tT, tn = min(2048, T), FWD_TILE
    first_b, skip = PB_OFF // tn, (OFF_B - PB_OFF) // tn

    def body(x_ref, nw_ref, w_ref, p_ref, h_ref, hs):
        @pl.when(pl.program_id(1) == 0)
        def _():
            xv = x_ref[...]
            r = lax.rsqrt(jnp.mean(xv * xv, axis=-1, keepdims=True) + NORM_EPS)
            hv = _b(xv * r * nw_ref[...])
            hs[...] = hv
            h_ref[...] = hv
        p_ref[...] = _dot(hs[...], w_ref[...])

    return pl.pallas_call(
        body, name="inproj_fwd", grid=(T // tT, P_MAIN // tn),
        in_specs=[pl.BlockSpec((tT, D), lambda i, j: (i, 0)), pl.BlockSpec((1, D), lambda i, j: (0, 0)),
                  pl.BlockSpec((D, tn), lambda i, j: (0, jnp.where(j < first_b, j, j + skip)))],
        out_specs=[pl.BlockSpec((tT, tn), lambda i, j: (i, j)), pl.BlockSpec((tT, D), lambda i, j: (i, 0))],
        out_shape=[jax.ShapeDtypeStruct((T, P_MAIN), f32), jax.ShapeDtypeStruct((T, D), bf16)],
        scratch_shapes=[pltpu.VMEM((tT, D), bf16)],
        compiler_params=_cp(("parallel", "arbitrary")),
    )(x, nw, w)


def inproj_m(h, w):
    T = h.shape[0]
    tT, tn = min(2048, T), FWD_TILE

    def body(h_ref, w_ref, p_ref):
        p_ref[...] = _b(_dot(h_ref[...], w_ref[...]))

    return pl.pallas_call(
        body, name="inproj_m", grid=(T // tT, REG_M // tn),
        in_specs=[pl.BlockSpec((tT, D), lambda i, j: (i, 0)), pl.BlockSpec((D, tn), lambda i, j: (0, OFF_M // tn + j))],
        out_specs=pl.BlockSpec((tT, tn), lambda i, j: (i, j)),
        out_shape=jax.ShapeDtypeStruct((T, REG_M), bf16),
        compiler_params=_cp(("parallel", "arbitrary")),
    )(h, w)


def matmul_tn(a, b, name, n=None, b_col0=0, with_bf16=False):
    T, K = a.shape
    N = b.shape[1] if n is None else n
    tT = min(2048, T)
    tn = NP_TILE if N % NP_TILE == 0 else min(N, 1024)
    nt = T // tT
    cb0 = b_col0 // tn

    def body(a_ref, b_ref, o_ref, *ob_ref):
        @pl.when(pl.program_id(1) == 0)
        def _():
            o_ref[...] = jnp.zeros_like(o_ref)
        o_ref[...] += _dot_tn(_b(a_ref[...]), _b(b_ref[...]))
        if with_bf16:
            @pl.when(pl.program_id(1) == nt - 1)
            def _():
                ob_ref[0][...] = _b(o_ref[...])

    ospec = pl.BlockSpec((K, tn), lambda j, t: (0, j))
    return pl.pallas_call(
        body, name=name, grid=(N // tn, nt),
        in_specs=[pl.BlockSpec((tT, K), lambda j, t: (t, 0)), pl.BlockSpec((tT, tn), lambda j, t: (t, cb0 + j))],
        out_specs=[ospec, ospec] if with_bf16 else ospec,
        out_shape=([jax.ShapeDtypeStruct((K, N), f32), jax.ShapeDtypeStruct((K, N), bf16)] if with_bf16
                   else jax.ShapeDtypeStruct((K, N), f32)),
        compiler_params=_cp(("parallel", "arbitrary")),
    )(a, b)


def inproj_bwd(dp, w, x, nw, dres):
    T = x.shape[0]
    tT, tk = min(1024, T), NP_TILE
    nk = NP // tk

    def body(dp_ref, w_ref, x_ref, nw_ref, dres_ref, dx_ref, dnw_ref, acc):
        i, k = pl.program_id(0), pl.program_id(1)

        @pl.when((i == 0) & (k == 0))
        def _():
            dnw_ref[...] = jnp.zeros_like(dnw_ref)

        @pl.when(k == 0)
        def _():
            acc[...] = jnp.zeros_like(acc)
        acc[...] += _dot_nt(dp_ref[...], w_ref[...])

        @pl.when(k == nk - 1)
        def _():
            xv = x_ref[...]
            r = lax.rsqrt(jnp.mean(xv * xv, axis=-1, keepdims=True) + NORM_EPS)
            xh = xv * r
            dy = acc[...]
            dyw = dy * nw_ref[...]
            dx_ref[...] = r * (dyw - xh * jnp.mean(dyw * xh, axis=-1, keepdims=True)) + dres_ref[...]
            dnw_ref[...] += jnp.sum(dy * xh, axis=0, keepdims=True)

    return pl.pallas_call(
        body, name="inproj_bwd", grid=(T // tT, nk),
        in_specs=[pl.BlockSpec((tT, tk), lambda i, k: (i, k)), pl.BlockSpec((D, tk), lambda i, k: (0, k)),
                  pl.BlockSpec((tT, D), lambda i, k: (i, 0)), pl.BlockSpec((1, D), lambda i, k: (0, 0)),
                  pl.BlockSpec((tT, D), lambda i, k: (i, 0))],
        out_specs=[pl.BlockSpec((tT, D), lambda i, k: (i, 0)), pl.BlockSpec((1, D), lambda i, k: (0, 0))],
        out_shape=[jax.ShapeDtypeStruct((T, D), f32), jax.ShapeDtypeStruct((1, D), f32)],
        scratch_shapes=[pltpu.VMEM((tT, D), f32)],
        compiler_params=_cp(("arbitrary", "arbitrary")),
    )(dp, w, x, nw, dres)


def loss_head(x, fw, tgt):
    T = x.shape[0]
    tT = min(512, T)

    def body(x_ref, fw_ref, t_ref, loss_ref, dx_ref, dfw_ref):
        @pl.when(pl.program_id(0) == 0)
        def _():
            loss_ref[...] = jnp.zeros_like(loss_ref)
            dfw_ref[...] = jnp.zeros_like(dfw_ref)
        xv = x_ref[...]
        r = lax.rsqrt(jnp.mean(xv * xv, axis=-1, keepdims=True) + NORM_EPS)
        xh = xv * r
        err = xh * fw_ref[...] - t_ref[...]
        part = 0.5 * jnp.sum(jnp.mean(err * err, axis=-1, keepdims=True), axis=0, keepdims=True)
        loss_ref[...] += jnp.broadcast_to(part, loss_ref.shape)
        dy = err * (1.0 / D)
        dyw = dy * fw_ref[...]
        dx_ref[...] = r * (dyw - xh * jnp.mean(dyw * xh, axis=-1, keepdims=True))
        dfw_ref[...] += jnp.sum(dy * xh, axis=0, keepdims=True)

    return pl.pallas_call(
        body, name="loss_head", grid=(T // tT,),
        in_specs=[pl.BlockSpec((tT, D), lambda i: (i, 0)), pl.BlockSpec((1, D), lambda i: (0, 0)),
                  pl.BlockSpec((tT, D), lambda i: (i, 0))],
        out_specs=[pl.BlockSpec((1, 128), lambda i: (0, 0)), pl.BlockSpec((tT, D), lambda i: (i, 0)),
                   pl.BlockSpec((1, D), lambda i: (0, 0))],
        out_shape=[jax.ShapeDtypeStruct((1, 128), f32), jax.ShapeDtypeStruct((T, D), f32),
                   jax.ShapeDtypeStruct((1, D), f32)],
        compiler_params=_cp(("arbitrary",)),
    )(x, fw, tgt)


def _halo_specs(tT, T, width, colblk):
    nb8 = T // 8
    per = tT // 8
    prev = pl.BlockSpec((8, width), lambda i: (jnp.maximum(i * per - 1, 0), colblk))
    nxt = pl.BlockSpec((8, width), lambda i: (jnp.minimum((i + 1) * per, nb8 - 1), colblk))
    return prev, nxt


def mixa_fwd(p, cw):
    T = p.shape[0]
    tT = min(512, T)
    prev_spec, _ = _halo_specs(tT, T, REG_A, OFF_A // REG_A)

    def body(p_ref, pp_ref, cw_ref, y_ref):
        pv = p_ref[...]
        u = pv[:, 512:1024] * pv[:, 1024:1536]
        pp = pp_ref[...]
        up = jnp.where(pl.program_id(0) == 0, 0.0, pp[:, 512:1024] * pp[:, 1024:1536])
        ue = jnp.concatenate([up, u], axis=0)
        cv = cw_ref[0:1, :] * _shift_down(ue, 2) + cw_ref[1:2, :] * _shift_down(ue, 1) + cw_ref[2:3, :] * ue
        y_ref[...] = _b(pv[:, 0:512] * cv[8:] * _silu(pv[:, 1536:2048]))

    return pl.pallas_call(
        body, name="mixa_fwd", grid=(T // tT,),
        in_specs=[pl.BlockSpec((tT, REG_A), lambda i: (i, OFF_A // REG_A)), prev_spec,
                  pl.BlockSpec((3, 512), lambda i: (0, 0))],
        out_specs=pl.BlockSpec((tT, 512), lambda i: (i, 0)),
        out_shape=jax.ShapeDtypeStruct((T, 512), bf16),
        compiler_params=_cp(("parallel",)),
    )(p, p, cw)


def mixa_bwd(p, cw, dy, dp):
    T = p.shape[0]
    tT = min(512, T)
    nt = T // tT
    prev_spec, next_spec = _halo_specs(tT, T, REG_A, OFF_A // REG_A)
    _, dnext_spec = _halo_specs(tT, T, 512, 0)

    def body(p_ref, pp_ref, pn_ref, cw_ref, dy_ref, dyn_ref, dp_in, dp_ref, dcw_ref):
        i = pl.program_id(0)

        @pl.when(i == 0)
        def _():
            dcw_ref[...] = jnp.zeros_like(dcw_ref)
        pv, pp, pn = p_ref[:, 0:2048], pp_ref[:, 0:2048], pn_ref[:, 0:2048]
        pe = jnp.concatenate([pp, pv, pn], axis=0)
        rows = lax.broadcasted_iota(jnp.int32, (tT + 16, 1), 0)
        ab, ac, ax, az = pe[:, 0:512], pe[:, 512:1024], pe[:, 1024:1536], pe[:, 1536:2048]
        u = jnp.where((rows < 8) & (i == 0), 0.0, ac * ax)
        u1, u2 = _shift_down(u, 1), _shift_down(u, 2)
        w0, w1, w2 = cw_ref[0:1, :], cw_ref[1:2, :], cw_ref[2:3, :]
        cv = w0 * u2 + w1 * u1 + w2 * u
        dye = jnp.concatenate([jnp.zeros((8, 512), f32), dy_ref[...], dyn_ref[...]], axis=0)
        dye = jnp.where((rows >= tT + 8) & (i == nt - 1), 0.0, dye)
        sz, dsz = _silu2(az)
        dcv = dye * ab * sz
        du = w2 * dcv + w1 * _shift_up(dcv, 1) + w0 * _shift_up(dcv, 2)
        inner = (rows >= 8) & (rows < tT + 8)
        dcv_in = jnp.where(inner, dcv, 0.0)
        dcw_ref[0:1, :] += jnp.sum(dcv_in * u2, axis=0, keepdims=True)
        dcw_ref[1:2, :] += jnp.sum(dcv_in * u1, axis=0, keepdims=True)
        dcw_ref[2:3, :] += jnp.sum(dcv_in * u, axis=0, keepdims=True)
        sl = slice(8, tT + 8)
        dp_ref[:, 0:512] = _b((dye * cv * sz)[sl])
        dp_ref[:, 512:1024] = _b((du * ax)[sl])
        dp_ref[:, 1024:1536] = _b((du * ac)[sl])
        dp_ref[:, 1536:2048] = _b((dye * ab * cv * dsz)[sl])
        dp_ref[:, 2048:] = jnp.zeros((tT, REG_A - 2048), bf16)

    return pl.pallas_call(
        body, name="mixa_bwd", grid=(nt,),
        in_specs=[pl.BlockSpec((tT, REG_A), lambda i: (i, OFF_A // REG_A)), prev_spec, next_spec,
                  pl.BlockSpec((3, 512), lambda i: (0, 0)),
                  pl.BlockSpec((tT, 512), lambda i: (i, 0)), dnext_spec, pl.BlockSpec(memory_space=pl.ANY)],
        out_specs=[pl.BlockSpec((tT, REG_A), lambda i: (i, OFF_A // REG_A)), pl.BlockSpec((8, 512), lambda i: (0, 0))],
        out_shape=[jax.ShapeDtypeStruct((T, NP), bf16), jax.ShapeDtypeStruct((8, 512), f32)],
        input_output_aliases={6: 0},
        compiler_params=_cp(("arbitrary",)),
    )(p, p, p, cw, dy, dy, dp)


def _l2n_fwd(y):
    return y * lax.rsqrt(jnp.sum(y * y, axis=-1, keepdims=True) + L2_EPS)


def mixc_pre_fwd(p, cw, alog_l, dtb_l):
    T = p.shape[0]
    tT = min(512, T)
    prev_spec, _ = _halo_specs(tT, T, REG_C, OFF_C // REG_C)

    def body(p_ref, pp_ref, cw_ref, al_ref, dt_ref, q_ref, k_ref, v_ref, sm_ref):
        pp = jnp.where(pl.program_id(0) == 0, 0.0, pp_ref[:, 0:2048])
        xe = jnp.concatenate([pp, p_ref[:, 0:2048]], axis=0)
        cv = (cw_ref[0:1, :] * _shift_down(xe, 3) + cw_ref[1:2, :] * _shift_down(xe, 2)
              + cw_ref[2:3, :] * _shift_down(xe, 1) + cw_ref[3:4, :] * xe)[8:]
        y = _silu(cv)
        for hh in range(4):
            sl = slice(hh * HK, (hh + 1) * HK)
            q_ref[:, sl] = _l2n_fwd(y[:, sl]) * QK_SCALE
            k_ref[:, sl] = _l2n_fwd(y[:, 512 + hh * HK:512 + (hh + 1) * HK])
        v_ref[...] = y[:, 1024:2048]
        ps = p_ref[:, 2048:2176]
        lane = lax.broadcasted_iota(jnp.int32, ps.shape, 1)
        la = -jnp.exp(al_ref[...]) * _softplus(ps + dt_ref[...])
        rin = lax.broadcasted_iota(jnp.int32, ps.shape, 0) % L
        g = _chunk_cumsum(la, rin)
        sm_ref[...] = jnp.where(lane < 8, _sigmoid(ps), jnp.where(lane < 16, g, 0.0))

    return pl.pallas_call(
        body, name="mixc_pre_fwd", grid=(T // tT,),
        in_specs=[pl.BlockSpec((tT, REG_C), lambda i: (i, OFF_C // REG_C)), prev_spec,
                  pl.BlockSpec((4, 2048), lambda i: (0, 0)),
                  pl.BlockSpec((1, 128), lambda i: (0, 0)), pl.BlockSpec((1, 128), lambda i: (0, 0))],
        out_specs=[pl.BlockSpec((tT, 512), lambda i: (i, 0)), pl.BlockSpec((tT, 512), lambda i: (i, 0)),
                   pl.BlockSpec((tT, 1024), lambda i: (i, 0)), pl.BlockSpec((tT, 128), lambda i: (i, 0))],
        out_shape=[jax.ShapeDtypeStruct((T, 512), f32), jax.ShapeDtypeStruct((T, 512), f32),
                   jax.ShapeDtypeStruct((T, 1024), f32), jax.ShapeDtypeStruct((T, 128), f32)],
        compiler_params=_cp(("parallel",)),
    )(p, p, cw, alog_l, dtb_l)


def mixc_pre_bwd(p, cw, alog_l, dtb_l, dq8, dk8, dv, dsm8, dp):
    T = p.shape[0]
    tT = min(256, T)
    nt = T // tT
    prev_spec, next_spec = _halo_specs(tT, T, REG_C, OFF_C // REG_C)
    _, n1024 = _halo_specs(tT, T, 1024, 0)

    def body(p_ref, pp_ref, pn_ref, cw_ref, al_ref, dt_ref, dq_ref, dqn_ref, dk_ref, dkn_ref,
             dv_ref, dvn_ref, dsm_ref, dp_in, dp_ref, dcw_ref, dsml_ref):
        i = pl.program_id(0)

        @pl.when(i == 0)
        def _():
            dcw_ref[...] = jnp.zeros_like(dcw_ref)
            dsml_ref[...] = jnp.zeros_like(dsml_ref)
        rows = lax.broadcasted_iota(jnp.int32, (tT + 16, 1), 0)
        pp = jnp.where(i == 0, 0.0, pp_ref[:, 0:2048])
        xe = jnp.concatenate([pp, p_ref[:, 0:2048], pn_ref[:, 0:2048]], axis=0)
        xs = [_shift_down(xe, 3), _shift_down(xe, 2), _shift_down(xe, 1), xe]
        cv = cw_ref[0:1, :] * xs[0] + cw_ref[1:2, :] * xs[1] + cw_ref[2:3, :] * xs[2] + cw_ref[3:4, :] * xs[3]
        y, dy_dcv = _silu2(cv)
        last = (rows >= tT + 8) & (i == nt - 1)
        z8q = jnp.zeros((8, 1024), f32)

        def ext(cur_ref, nxt_ref):
            return jnp.where(last, 0.0, jnp.concatenate([z8q, cur_ref[...], nxt_ref[...]], axis=0))
        dq8e, dk8e, dve = ext(dq_ref, dqn_ref), ext(dk_ref, dkn_ref), ext(dv_ref, dvn_ref)
        dys = []
        for (d8, base, scale) in ((dq8e, 0, QK_SCALE), (dk8e, 512, 1.0)):
            for hh in range(4):
                dn = (d8[:, (2 * hh) * HK:(2 * hh + 1) * HK] + d8[:, (2 * hh + 1) * HK:(2 * hh + 2) * HK]) * scale
                yh = y[:, base + hh * HK:base + (hh + 1) * HK]
                r = lax.rsqrt(jnp.sum(yh * yh, axis=-1, keepdims=True) + L2_EPS)
                nh = yh * r
                dys.append(r * (dn - nh * jnp.sum(dn * nh, axis=-1, keepdims=True)))
        dyy = jnp.concatenate(dys + [dve], axis=1)
        dcv = dyy * dy_dcv
        dx = (cw_ref[3:4, :] * dcv + cw_ref[2:3, :] * _shift_up(dcv, 1) + cw_ref[1:2, :] * _shift_up(dcv, 2)
              + cw_ref[0:1, :] * _shift_up(dcv, 3))
        dp_ref[:, 0:2048] = _b(dx[8:tT + 8])
        dp_ref[:, 2176:] = jnp.zeros((tT, REG_C - 2176), bf16)
        inner = (rows >= 8) & (rows < tT + 8)
        dcv_in = jnp.where(inner, dcv, 0.0)
        for j in range(4):
            dcw_ref[j:j + 1, :] += jnp.sum(dcv_in * xs[j], axis=0, keepdims=True)
        ps = p_ref[:, 2048:2176]
        lane = lax.broadcasted_iota(jnp.int32, ps.shape, 1)
        dsm = dsm_ref[:, 0:128]
        for hh in range(1, N_GDN):
            dsm = dsm + dsm_ref[:, hh * 128:(hh + 1) * 128]
        beta = _sigmoid(ps)
        xa = ps + dt_ref[...]
        nea = -jnp.exp(al_ref[...])
        dpa = dsm * nea * _sigmoid(xa)
        dp_ref[:, 2048:2176] = _b(jnp.where(lane < 8, dsm * beta * (1.0 - beta), jnp.where(lane < 16, dpa, 0.0)))
        amask = (lane >= 8) & (lane < 16)
        dsml_ref[0:1, :] += jnp.sum(jnp.where(amask, dsm * nea * _softplus(xa), 0.0), axis=0, keepdims=True)
        dsml_ref[1:2, :] += jnp.sum(jnp.where(amask, dpa, 0.0), axis=0, keepdims=True)

    cur1024 = pl.BlockSpec((tT, 1024), lambda i: (i, 0))
    return pl.pallas_call(
        body, name="mixc_pre_bwd", grid=(nt,),
        in_specs=[pl.BlockSpec((tT, REG_C), lambda i: (i, OFF_C // REG_C)), prev_spec, next_spec,
                  pl.BlockSpec((4, 2048), lambda i: (0, 0)),
                  pl.BlockSpec((1, 128), lambda i: (0, 0)), pl.BlockSpec((1, 128), lambda i: (0, 0)),
                  cur1024, n1024, cur1024, n1024, cur1024, n1024, cur1024, pl.BlockSpec(memory_space=pl.ANY)],
        out_specs=[pl.BlockSpec((tT, REG_C), lambda i: (i, OFF_C // REG_C)),
                   pl.BlockSpec((8, 2048), lambda i: (0, 0)), pl.BlockSpec((8, 128), lambda i: (0, 0))],
        out_shape=[jax.ShapeDtypeStruct((T, NP), bf16),
                   jax.ShapeDtypeStruct((8, 2048), f32), jax.ShapeDtypeStruct((8, 128), f32)],
        input_output_aliases={13: 0},
        compiler_params=_cp(("arbitrary",)),
    )(p, p, p, cw, alog_l, dtb_l, dq8, dq8, dk8, dk8, dv, dv, dsm8, dp)


def _tri_inverse(m):
    r = lax.broadcasted_iota(jnp.int32, (L, L), 0)
    c = lax.broadcasted_iota(jnp.int32, (L, L), 1)
    eye = (r == c).astype(f32)[None]
    same = lambda w: ((r // w) == (c // w))[None]
    md = jnp.where(same(8), m, 0.0)
    m2 = _bdot_split(md, md)
    m4 = _bdot_split(m2, m2)
    t = _bdot_split(_bdot_split(eye - md, eye + m2), eye + m4)
    for w in (16, 32, 64):
        mo = jnp.where(same(w) & jnp.logical_not(same(w // 2)), m, 0.0)
        t = t - _bdot_split(_bdot_split(t, mo), t)
    return t


def _col_to_row(col, eye):
    return jnp.sum(eye * col, axis=1, keepdims=True)


def _row_to_col(row, eye):
    return jnp.sum(eye * row, axis=2, keepdims=True)


def _gdn_chunk_terms(q, k, v, beta, g, t_inv=None):
    r = lax.broadcasted_iota(jnp.int32, (L, L), 0)
    c = lax.broadcasted_iota(jnp.int32, (L, L), 1)
    eye = (r == c).astype(f32)[None]
    causal, strict = (c <= r)[None], (c < r)[None]
    diff = g - _col_to_row(g, eye)
    dec = jnp.exp(jnp.where(causal, diff, 0.0))
    dc = jnp.where(causal, dec, 0.0)
    ds = jnp.where(strict, dec, 0.0)
    eg = jnp.exp(g)
    gl = g[:, L - 1:L, :]
    egl = jnp.exp(gl - g)
    kb = k * beta
    kk = _bdot_nt(_b(k), _b(kb))
    qk = _bdot_nt(_b(q), _b(kb))
    m = kk * ds
    aqk = qk * dc
    if t_inv is None:
        t_inv = _tri_inverse(m)
    tb = _b(t_inv)
    keg = k * eg
    u = _bdot(tb, _b(v))
    w = _bdot(tb, _b(keg))
    ks = kb * egl
    ksw = _bdot_tn(_b(ks), _b(w))
    return dict(eye=eye, causal=causal, strict=strict, dc=dc, ds=ds, eg=eg, gl=gl, egl=egl, kb=kb, kk=kk, qk=qk,
                m=m, aqk=aqk, t=t_inv, u=u, w=w, qi=q * eg, ks=ks, keg=keg, ksw=ksw)


def gdn_fwd(qn, kn, vv, sm):
    T = qn.shape[0]
    tB = min(2 * GDN_BLOCK, T)
    nc = tB // L
    N = T // L

    def body(q_ref, k_ref, v_ref, sm_ref, o_ref, st_ref, ti_ref, s_scr):
        h = pl.program_id(0)

        @pl.when(pl.program_id(1) == 0)
        def _():
            s_scr[...] = jnp.zeros_like(s_scr)
        smv = sm_ref[...]
        lane = lax.broadcasted_iota(jnp.int32, smv.shape, 1)
        beta = jnp.sum(jnp.where(lane == h, smv, 0.0), axis=1, keepdims=True).reshape(nc, L, 1)
        g = jnp.sum(jnp.where(lane == 8 + h, smv, 0.0), axis=1, keepdims=True).reshape(nc, L, 1)
        q = q_ref[...].reshape(nc, L, HK)
        k = k_ref[...].reshape(nc, L, HK)
        v = v_ref[...].reshape(nc, L, HK)
        tm = _gdn_chunk_terms(q, k, v, beta, g)
        ti_ref[...] = tm["t"]
        ksu = _bdot_tn(_b(tm["ks"]), _b(tm["u"]))
        kswb = _b(tm["ksw"])
        egl_last = jnp.exp(tm["gl"])
        s = s_scr[...]
        states = [None] * nc
        for ci in range(nc):
            states[ci] = s
            s = egl_last[ci] * s + (ksu[ci] - _dot(kswb[ci], _b(s)))
        s_scr[...] = s
        sall = jnp.stack(states, axis=0)
        st_ref[...] = sall
        sb = _b(sall)
        e = tm["u"] - _bdot(_b(tm["w"]), sb)
        o = _bdot(_b(tm["qi"]), sb) + _bdot(_b(tm["aqk"]), _b(e))
        o_ref[...] = o.reshape(tB, HK)

    return pl.pallas_call(
        body, name="gdn_fwd", grid=(N_GDN, T // tB),
        in_specs=[pl.BlockSpec((tB, HK), lambda h, n: (n, h // 2)), pl.BlockSpec((tB, HK), lambda h, n: (n, h // 2)),
                  pl.BlockSpec((tB, HK), lambda h, n: (n, h)), pl.BlockSpec((tB, 128), lambda h, n: (n, 0))],
        out_specs=[pl.BlockSpec((tB, HK), lambda h, n: (n, h)),
                   pl.BlockSpec((None, nc, HK, HK), lambda h, n: (h, n, 0, 0)),
                   pl.BlockSpec((None, nc, L, L), lambda h, n: (h, n, 0, 0))],
        out_shape=[jax.ShapeDtypeStruct((T, N_GDN * HK), f32), jax.ShapeDtypeStruct((N_GDN, N, HK, HK), f32),
                   jax.ShapeDtypeStruct((N_GDN, N, L, L), f32)],
        scratch_shapes=[pltpu.VMEM((HK, HK), f32)],
        compiler_params=_cp(("parallel", "arbitrary")),
    )(qn, kn, vv, sm)


def gdn_bwd(qn, kn, vv, sm, st, ti, do):
    T = qn.shape[0]
    tB = min(GDN_BLOCK, T)
    nc = tB // L
    nb = T // tB

    def body(q_ref, k_ref, v_ref, sm_ref, st_ref, ti_ref, do_ref, dq_ref, dk_ref, dv_ref, dsm_ref, ds_scr):
        h = pl.program_id(0)

        @pl.when(pl.program_id(1) == 0)
        def _():
            ds_scr[...] = jnp.zeros_like(ds_scr)
        smv = sm_ref[...]
        lane = lax.broadcasted_iota(jnp.int32, smv.shape, 1)
        beta = jnp.sum(jnp.where(lane == h, smv, 0.0), axis=1, keepdims=True).reshape(nc, L, 1)
        g = jnp.sum(jnp.where(lane == 8 + h, smv, 0.0), axis=1, keepdims=True).reshape(nc, L, 1)
        q = q_ref[...].reshape(nc, L, HK)
        k = k_ref[...].reshape(nc, L, HK)
        v = v_ref[...].reshape(nc, L, HK)
        do = do_ref[...].reshape(nc, L, HK)
        s = st_ref[...]
        tm = _gdn_chunk_terms(q, k, v, beta, g, t_inv=ti_ref[...])
        eye, dc, ds_, eg, egl = tm["eye"], tm["dc"], tm["ds"], tm["eg"], tm["egl"]
        kb, u, w, qi, ks, tinv = tm["kb"], tm["u"], tm["w"], tm["qi"], tm["ks"], tm["t"]
        sb, dob = _b(s), _b(do)
        e = u - _bdot(_b(w), sb)
        eb = _b(e)
        egl_last = jnp.exp(tm["gl"])
        de0 = _bdot_tn(_b(tm["aqk"]), dob)
        ds0 = _bdot_tn(_b(qi), dob) - _bdot_tn(_b(w), _b(de0))
        kswb = _b(tm["ksw"])
        dsn = ds_scr[...]
        dsns = [None] * nc
        for ci in reversed(range(nc)):
            dsns[ci] = dsn
            dsn = ds0[ci] + (egl_last[ci] * dsn - _dot_tn(kswb[ci], _b(dsn)))
        ds_scr[...] = dsn
        dsp = jnp.stack(dsns, axis=0)
        dspb = _b(dsp)
        de = de0 + _bdot(_b(ks), dspb)
        deb = _b(de)
        dks = _bdot_nt(eb, dspb)
        dqi = _bdot_nt(dob, sb)
        daqk = jnp.where(tm["causal"], _bdot_nt(dob, eb), 0.0)
        dw = -_bdot_nt(deb, sb)
        tb = _b(tinv)
        dvv = _bdot_tn(tb, deb)
        dkg = _bdot_tn(tb, _b(dw))
        dm = -jnp.where(tm["strict"], _bdot_nt(_b(dvv), _b(u)) + _bdot_nt(_b(dkg), _b(w)), 0.0)
        x = _b(dm * ds_)
        y = _b(daqk * dc)
        kbb, kbf, qbf = _b(kb), _b(k), _b(q)
        dk = _bdot(x, kbb) + dkg * eg
        dkb = _bdot_tn(x, kbf) + _bdot_tn(y, qbf) + dks * egl
        dq = _bdot(y, kbb) + dqi * eg
        dk = dk + dkb * beta
        dbeta = jnp.sum(dkb * k, axis=-1, keepdims=True)
        z = dm * tm["m"] + daqk * tm["aqk"]
        dg = (jnp.sum(dqi * qi - dks * ks + dkg * tm["keg"], axis=-1, keepdims=True)
              + jnp.sum(z, axis=-1, keepdims=True) - _row_to_col(jnp.sum(z, axis=1, keepdims=True), eye))
        dgl = (egl_last * jnp.sum(jnp.sum(s * dsp, axis=2, keepdims=True), axis=1, keepdims=True)
               + jnp.sum(jnp.sum(dks * ks, axis=2, keepdims=True), axis=1, keepdims=True))
        rowi = lax.broadcasted_iota(jnp.int32, (nc, L, 1), 1)
        dg = dg + jnp.where(rowi == L - 1, dgl, 0.0)
        dg2 = dg.reshape(tB, 1)
        rin = lax.broadcasted_iota(jnp.int32, (tB, 1), 0) % L
        dla = _chunk_rev_cumsum(jnp.broadcast_to(dg2, (tB, 128)), jnp.broadcast_to(rin, (tB, 128)))
        dq_ref[...] = dq.reshape(tB, HK)
        dk_ref[...] = dk.reshape(tB, HK)
        dv_ref[...] = dvv.reshape(tB, HK)
        dsm_ref[...] = jnp.where(lane == h, dbeta.reshape(tB, 1), jnp.where(lane == 8 + h, dla, 0.0))

    rev = lambda n: nb - 1 - n
    return pl.pallas_call(
        body, name="gdn_bwd", grid=(N_GDN, nb),
        in_specs=[pl.BlockSpec((tB, HK), lambda h, n: (rev(n), h // 2)),
                  pl.BlockSpec((tB, HK), lambda h, n: (rev(n), h // 2)),
                  pl.BlockSpec((tB, HK), lambda h, n: (rev(n), h)), pl.BlockSpec((tB, 128), lambda h, n: (rev(n), 0)),
                  pl.BlockSpec((None, nc, HK, HK), lambda h, n: (h, rev(n), 0, 0)),
                  pl.BlockSpec((None, nc, L, L), lambda h, n: (h, rev(n), 0, 0)),
                  pl.BlockSpec((tB, HK), lambda h, n: (rev(n), h))],
        out_specs=[pl.BlockSpec((tB, HK), lambda h, n: (rev(n), h))] * 4,
        out_shape=[jax.ShapeDtypeStruct((T, N_GDN * HK), f32)] * 4,
        scratch_shapes=[pltpu.VMEM((HK, HK), f32)],
        compiler_params=_cp(("parallel", "arbitrary")),
    )(qn, kn, vv, sm, st, ti, do)


def _hgrn_prep(bq, bf_, bi, lb):
    tB = bq.shape[0]
    sq, dsq = _silu2(bq)
    sg = _sigmoid(bf_)
    f = lb + (1.0 - lb) * sg
    logf = jnp.log(jnp.maximum(f, MIN_F))
    rin = lax.broadcasted_iota(jnp.int32, (tB, HK), 0) % L
    g = _chunk_cumsum(logf, rin)
    return sq * QK_SCALE, sg, f, 1.0 - f, bi, g, rin, dsq * QK_SCALE


def _hgrn_intra(q, kk, v, g, do=None):
    n = q.shape[0]
    nsub = L // SUB
    bwd = do is not None
    o_rows = [None] * nsub
    if bwd:
        dq_rows = [None] * nsub
        dkk_acc = jnp.zeros_like(kk)
        dv_acc = jnp.zeros_like(v)
    for i in range(1, nsub):
        lo, hi, w = i * SUB, (i + 1) * SUB, i * SUB
        ref = g[:, lo - 1:lo, :]
        eq = jnp.exp(g[:, lo:hi, :] - ref)
        ek = jnp.exp(ref - g[:, :w, :])
        qs = _b(q[:, lo:hi, :] * eq)
        ks = _b(kk[:, :w, :] * ek)
        p = _bdot_nt(qs, ks)
        o_rows[i] = _bdot(_b(p), _b(v[:, :w, :]))
        if bwd:
            dob = _b(do[:, lo:hi, :])
            dp = _b(_bdot_nt(dob, _b(v[:, :w, :])))
            dq_rows[i] = _bdot(dp, ks) * eq
            pad = jnp.zeros((n, L - w, HK), f32)
            dkk_acc = dkk_acc + jnp.concatenate([_bdot_tn(dp, qs) * ek, pad], axis=1)
            dv_acc = dv_acc + jnp.concatenate([_bdot_tn(_b(p), dob), pad], axis=1)
    m = n * nsub
    q4, k4, v4, g4 = (a.reshape(m, SUB, HK) for a in (q, kk, v, g))
    r = lax.broadcasted_iota(jnp.int32, (m, SUB, HK), 1)
    od = jnp.zeros((m, SUB, HK), f32)
    if bwd:
        do4 = do.reshape(m, SUB, HK)
        dqd = jnp.zeros((m, SUB, HK), f32)
        dkd = jnp.zeros((m, SUB, HK), f32)
        dvd = jnp.zeros((m, SUB, HK), f32)
    for j in range(SUB):
        gj, kj, vj = g4[:, j:j + 1, :], k4[:, j:j + 1, :], v4[:, j:j + 1, :]
        ok = r >= j
        e = jnp.where(ok, jnp.exp(g4 - gj), 0.0)
        xq = q4 * e
        pj = jnp.sum(xq * kj, axis=-1, keepdims=True)
        od = od + pj * vj
        if bwd:
            dpj = jnp.sum(do4 * vj, axis=-1, keepdims=True)
            dqd = dqd + dpj * kj * e
            dkd = dkd + jnp.where(r == j, jnp.sum(dpj * xq, axis=1, keepdims=True), 0.0)
            dvd = dvd + jnp.where(r == j, jnp.sum(pj * do4, axis=1, keepdims=True), 0.0)
    od = od.reshape(n, L, HK)
    o = jnp.concatenate([od[:, :SUB, :]] + [od[:, i * SUB:(i + 1) * SUB, :] + o_rows[i] for i in range(1, nsub)], axis=1)
    if not bwd:
        return o
    dqd = dqd.reshape(n, L, HK)
    dq = jnp.concatenate([dqd[:, :SUB, :]] + [dqd[:, i * SUB:(i + 1) * SUB, :] + dq_rows[i] for i in range(1, nsub)], axis=1)
    return o, dq, dkk_acc + dkd.reshape(n, L, HK), dv_acc + dvd.reshape(n, L, HK)


def hgrn_fwd(p, lbs):
    T = p.shape[0]
    tB = min(1024, T)
    nc = tB // L
    N = T // L

    def body(b_ref, lb_ref, o_ref, st_ref, s_scr):
        @pl.when(pl.program_id(1) == 0)
        def _():
            s_scr[...] = jnp.zeros_like(s_scr)
        q, sg, f, kk, v, g, rin, _ = _hgrn_prep(b_ref[:, 0:HK], b_ref[:, HK:2 * HK], b_ref[:, 2 * HK:3 * HK], lb_ref[...])
        q3, k3, v3, g3 = (a.reshape(nc, L, HK) for a in (q, kk, v, g))
        o = _hgrn_intra(q3, k3, v3, g3)
        gl = g3[:, L - 1:L, :]
        qt = _b(q3 * jnp.exp(g3))
        kt = _b(k3 * jnp.exp(gl - g3))
        vb = _b(v3)
        st = s_scr[...]
        for c in range(nc):
            st_ref[c] = st
            o_ref[c * L:(c + 1) * L, :] = o[c] + _dot_nt(qt[c], _b(st))
            st = st * jnp.exp(gl[c]) + _dot_tn(vb[c], kt[c])
        s_scr[...] = st

    return pl.pallas_call(
        body, name="hgrn_fwd", grid=(N_HGRN, T // tB),
        in_specs=[pl.BlockSpec((tB, REG_BH), lambda h, n: (n, PB_OFF // REG_BH + h)),
                  pl.BlockSpec((1, HK), lambda h, n: (0, h))],
        out_specs=[pl.BlockSpec((tB, HK), lambda h, n: (n, h)),
                   pl.BlockSpec((None, nc, HK, HK), lambda h, n: (h, n, 0, 0))],
        out_shape=[jax.ShapeDtypeStruct((T, N_HGRN * HK), f32), jax.ShapeDtypeStruct((N_HGRN, N, HK, HK), f32)],
        scratch_shapes=[pltpu.VMEM((HK, HK), f32)],
        compiler_params=_cp(("parallel", "arbitrary")),
    )(p, lbs)


def hgrn_bwd(p, lbs, st, do, dp):
    T = p.shape[0]
    tB = min(256, T)
    nc = tB // L
    nb = T // tB

    def body(b_ref, lb_ref, st_ref, do_ref, dp_in, dp_ref, dlb_ref, ds_scr):
        @pl.when(pl.program_id(1) == 0)
        def _():
            ds_scr[...] = jnp.zeros_like(ds_scr)
            dlb_ref[...] = jnp.zeros_like(dlb_ref)
        lb = lb_ref[...]
        bq = b_ref[:, 0:HK]
        q, sg, f, kk, v, g, rin, dq_dbq = _hgrn_prep(bq, b_ref[:, HK:2 * HK], b_ref[:, 2 * HK:3 * HK], lb)
        q3, k3, v3, g3 = (a.reshape(nc, L, HK) for a in (q, kk, v, g))
        do3 = do_ref[...].reshape(nc, L, HK)
        dob = _b(do3)
        gl = g3[:, L - 1:L, :]
        egl = jnp.exp(gl)
        eg, egr = jnp.exp(g3), jnp.exp(gl - g3)
        qt, kt = q3 * eg, k3 * egr
        s = st_ref[...]
        ds0 = _bdot_tn(dob, _b(qt))
        dsn = ds_scr[...]
        dsns = [None] * nc
        for c in reversed(range(nc)):
            dsns[c] = dsn
            dsn = ds0[c] + dsn * egl[c]
        ds_scr[...] = dsn
        dsp = jnp.stack(dsns, axis=0)
        dspb = _b(dsp)
        dqt = _bdot(dob, _b(s))
        dkt = _bdot(_b(v3), dspb)
        dv_state = _bdot_nt(_b(kt), dspb)
        dgl = egl * jnp.sum(s * dsp, axis=1, keepdims=True) + jnp.sum(dkt * kt, axis=1, keepdims=True)
        _, dq_i, dkk_i, dv_i = _hgrn_intra(q3, k3, v3, g3, do=do3)
        dq = dq_i + dqt * eg
        dkk = dkk_i + dkt * egr
        dv = dv_i + dv_state
        rowi = lax.broadcasted_iota(jnp.int32, (nc, L, HK), 1)
        dg = q3 * dq - k3 * dkk + jnp.where(rowi == L - 1, dgl, 0.0)
        dlogf = _chunk_rev_cumsum(dg.reshape(tB, HK), rin)
        dkk2 = dkk.reshape(tB, HK)
        df = jnp.where(f > MIN_F, dlogf / f, 0.0) - dkk2
        dlb_ref[...] += jnp.sum(df * (1.0 - sg), axis=0, keepdims=True)
        dp_ref[:, 0:HK] = _b(dq.reshape(tB, HK) * dq_dbq)
        dp_ref[:, HK:2 * HK] = _b(df * (1.0 - lb) * sg * (1.0 - sg))
        dp_ref[:, 2 * HK:3 * HK] = _b(dv.reshape(tB, HK))

    rev = lambda n: nb - 1 - n
    return pl.pallas_call(
        body, name="hgrn_bwd", grid=(N_HGRN, nb),
        in_specs=[pl.BlockSpec((tB, REG_BH), lambda h, n: (rev(n), PB_OFF // REG_BH + h)),
                  pl.BlockSpec((1, HK), lambda h, n: (0, h)),
                  pl.BlockSpec((None, nc, HK, HK), lambda h, n: (h, rev(n), 0, 0)),
                  pl.BlockSpec((tB, HK), lambda h, n: (rev(n), h)), pl.BlockSpec(memory_space=pl.ANY)],
        out_specs=[pl.BlockSpec((tB, REG_BH), lambda h, n: (rev(n), OFF_B // REG_BH + h)),
                   pl.BlockSpec((1, HK), lambda h, n: (0, h))],
        out_shape=[jax.ShapeDtypeStruct((T, NP), bf16), jax.ShapeDtypeStruct((1, N_HGRN * HK), f32)],
        input_output_aliases={4: 0},
        scratch_shapes=[pltpu.VMEM((HK, HK), f32)],
        compiler_params=_cp(("parallel", "arbitrary")),
    )(p, lbs, st, do, dp)


def _headnorm_fwd(o, z, w, nheads):
    outs, parts = [], []
    for hh in range(nheads):
        sl = slice(hh * HK, (hh + 1) * HK)
        oh = o[:, sl]
        r = lax.rsqrt(jnp.mean(oh * oh, axis=-1, keepdims=True) + NORM_EPS)
        on = oh * r
        sz, dsz = _silu2(z[:, sl])
        outs.append(on * w * sz)
        parts.append((r, on, sz, dsz))
    return jnp.concatenate(outs, axis=1), parts


def _headnorm_bwd(parts, w, dy):
    dos, dzs = [], []
    dw = jnp.zeros((1, HK), f32)
    for hh, (r, on, sz, dsz) in enumerate(parts):
        dyh = dy[:, hh * HK:(hh + 1) * HK]
        dn = dyh * sz * w
        dos.append(r * (dn - on * jnp.mean(dn * on, axis=-1, keepdims=True)))
        dzs.append(dyh * on * w * dsz)
        dw = dw + jnp.sum(dyh * sz * on, axis=0, keepdims=True)
    return jnp.concatenate(dos, axis=1), jnp.concatenate(dzs, axis=1), dw


def _merge_specs(tT, l):
    row = lambda w, cb=0: pl.BlockSpec((tT, w), lambda i, cb=cb: (i, cb))
    full = lambda r, c: pl.BlockSpec((r, c), lambda i: (0, 0))
    layer = lambda r, c: pl.BlockSpec((None, r, c), lambda i: (l, 0, 0))
    return row, full, layer


def merge_fwd(x, p, ya, ob, oc, hw, gw, bg, woa, wob, woc, wo, l):
    T = x.shape[0]
    tT = min(256, T)
    row, full, layer = _merge_specs(tT, l)

    def body(x_ref, pm_ref, ya_ref, ob_ref, oc_ref, hw_ref, gw_ref, bg_ref,
             woa_ref, wob_ref, woc_ref, wo_ref, out_ref):
        yb = _b(_headnorm_fwd(ob_ref[...], pm_ref[:, M_BZ:M_G].astype(f32), hw_ref[...], N_HGRN)[0])
        yc = _b(_headnorm_fwd(oc_ref[...], pm_ref[:, M_CZ:REG_M].astype(f32), gw_ref[...], N_GDN)[0])
        gates = _sigmoid(pm_ref[:, M_G:M_CZ].astype(f32) + bg_ref[...])
        merged = (gates[:, 0:D] * _dot(ya_ref[...], woa_ref[...]) + gates[:, D:2 * D] * _dot(yb, wob_ref[...])
                  + gates[:, 2 * D:3 * D] * _dot(yc, woc_ref[...]))
        out_ref[...] = x_ref[...] + _dot(_b(merged), wo_ref[...])

    return pl.pallas_call(
        body, name="merge_fwd", grid=(T // tT,),
        in_specs=[row(D), row(REG_M),
                  row(512), row(512), row(1024), full(1, HK), full(1, HK), full(1, 3 * D),
                  layer(512, D), layer(512, D), layer(D, D), layer(D, D)],
        out_specs=row(D),
        out_shape=jax.ShapeDtypeStruct((T, D), f32),
        compiler_params=_cp(("parallel",)),
    )(x, p, ya, ob, oc, hw, gw, bg, woa, wob, woc, wo)


def merge_bwd(dxo, p, ya, ob, oc, hw, gw, bg, woa, wob, woc, wo, l):
    T = dxo.shape[0]
    tT = min(256, T)
    row, full, layer = _merge_specs(tT, l)

    def body(dx_ref, pm_ref, ya_ref, ob_ref, oc_ref, hw_ref, gw_ref, bg_ref,
             woa_ref, wob_ref, woc_ref, wo_ref,
             dya_ref, dob_ref, doc_ref, dp_ref, mg_ref, dy3_ref, yb_ref, yc_ref,
             dbg_ref, dhw_ref, dgw_ref):
        @pl.when(pl.program_id(0) == 0)
        def _():
            dbg_ref[...] = jnp.zeros_like(dbg_ref)
            dhw_ref[...] = jnp.zeros_like(dhw_ref)
            dgw_ref[...] = jnp.zeros_like(dgw_ref)
        ob, oc, bz, cz = ob_ref[...], oc_ref[...], pm_ref[:, M_BZ:M_G].astype(f32), pm_ref[:, M_CZ:REG_M].astype(f32)
        hw_, gw_ = hw_ref[...], gw_ref[...]
        yb, parts_b = _headnorm_fwd(ob, bz, hw_, N_HGRN)
        yc, parts_c = _headnorm_fwd(oc, cz, gw_, N_GDN)
        yb, yc = _b(yb), _b(yc)
        yb_ref[...] = yb
        yc_ref[...] = yc
        gates = _sigmoid(pm_ref[:, M_G:M_CZ].astype(f32) + bg_ref[...])
        ys = (_dot(ya_ref[...], woa_ref[...]), _dot(yb, wob_ref[...]), _dot(yc, woc_ref[...]))
        dmerged = _dot_nt(_b(dx_ref[...]), wo_ref[...])
        merged = jnp.zeros_like(dmerged)
        dys = []
        for i in range(3):
            gi = gates[:, i * D:(i + 1) * D]
            merged = merged + gi * ys[i]
            dyi = _b(dmerged * gi)
            dys.append(dyi)
            dy3_ref[:, i * D:(i + 1) * D] = dyi
            dgp = dmerged * ys[i] * gi * (1.0 - gi)
            dp_ref[:, M_G + i * D:M_G + (i + 1) * D] = _b(dgp)
            dbg_ref[:, i * D:(i + 1) * D] += jnp.sum(dgp, axis=0, keepdims=True)
        mg_ref[...] = _b(merged)
        dya_ref[...] = _dot_nt(dys[0], woa_ref[...])
        dob, dbz, dhw = _headnorm_bwd(parts_b, hw_, _dot_nt(dys[1], wob_ref[...]))
        doc, dcz, dgw = _headnorm_bwd(parts_c, gw_, _dot_nt(dys[2], woc_ref[...]))
        dob_ref[...] = dob
        doc_ref[...] = doc
        dp_ref[:, M_BZ:M_G] = _b(dbz)
        dp_ref[:, M_CZ:REG_M] = _b(dcz)
        dhw_ref[...] += dhw
        dgw_ref[...] += dgw

    sd = jax.ShapeDtypeStruct
    return pl.pallas_call(
        body, name="merge_bwd", grid=(T // tT,),
        in_specs=[row(D), row(REG_M),
                  row(512), row(512), row(1024), full(1, HK), full(1, HK), full(1, 3 * D),
                  layer(512, D), layer(512, D), layer(D, D), layer(D, D)],
        out_specs=[row(512), row(512), row(1024), row(REG_M, OFF_M // REG_M), row(D), row(3 * D), row(512),
                   row(1024), full(1, 3 * D), full(1, HK), full(1, HK)],
        out_shape=[sd((T, 512), f32), sd((T, 512), f32), sd((T, 1024), f32), sd((T, NP), bf16),
                   sd((T, D), bf16), sd((T, 3 * D), bf16), sd((T, 512), bf16),
                   sd((T, 1024), bf16), sd((1, 3 * D), f32), sd((1, HK), f32), sd((1, HK), f32)],
        compiler_params=_cp(("arbitrary",)),
    )(dxo, p, ya, ob, oc, hw, gw, bg, woa, wob, woc, wo)


def layer_fwd(x, w):
    l = w["l"]
    p, h = inproj_fwd(x, w["norm_w"], w["w_in"])
    pm = inproj_m(h, w["w_in"])
    ya = mixa_fwd(p, w["conv_a"])
    qn, kn, vv, sm = mixc_pre_fwd(p, w["conv_c"], w["alog_l"], w["dtb_l"])
    oc, st_c, ti = gdn_fwd(qn, kn, vv, sm)
    ob, st_b = hgrn_fwd(p, w["lbs"])
    if "late" in w:
        w.update(w.pop("late")(ob))
    xo = merge_fwd(x, pm, ya, ob, oc, w["hgrn_norm_w"], w["gdn_norm_w"], w["b_gate"],
                   w["w_out_a"], w["w_out_b"], w["w_out_c"], w["w_o"], l)
    saved = dict(x=x, p=p, pm=pm, h=h, ya=ya, qn=qn, kn=kn, vv=vv, sm=sm, oc=oc, st_c=st_c, ti=ti, ob=ob, st_b=st_b)
    return xo, saved


OUT_MATS = (("w_out_a", "cols"), ("w_out_b", "cols"), ("w_out_c", "rows"), ("w_o", "rows"))


def layer_bwd(dxo, w, s, chip):
    p, l = s["p"], w["l"]
    (dya, dob, doc, dp, merged, dy3, yb, yc, dbg, dhw, dgw) = merge_bwd(
        dxo, s["pm"], s["ya"], s["ob"], s["oc"], w["hgrn_norm_w"], w["gdn_norm_w"], w["b_gate"],
        w["w_out_a"], w["w_out_b"], w["w_out_c"], w["w_o"], l)
    full = {"w_o": matmul_tn(merged, dxo, "dw_o", with_bf16=True),
            "w_out_a": matmul_tn(s["ya"], dy3, "dw_out_a", n=D, b_col0=0, with_bf16=True),
            "w_out_b": matmul_tn(yb, dy3, "dw_out_b", n=D, b_col0=D, with_bf16=True),
            "w_out_c": matmul_tn(yc, dy3, "dw_out_c", n=D, b_col0=2 * D, with_bf16=True)}
    out_kinds = [k for _, k in OUT_MATS]
    sent_out, token = exchange_start([full[n][1] for n, _ in OUT_MATS], out_kinds, f"grads_out_start{l}")
    dp, dlbs = hgrn_bwd(p, w["lbs"] + token[0:1, 0:1], s["st_b"], dob, dp)
    dq8, dk8, dvv, dsm8 = gdn_bwd(s["qn"], s["kn"], s["vv"], s["sm"], s["st_c"], s["ti"], doc)
    dp, dcc, dsmall = mixc_pre_bwd(p, w["conv_c"], w["alog_l"], w["dtb_l"], dq8, dk8, dvv, dsm8, dp)
    dp, dca = mixa_bwd(p, w["conv_a"], dya, dp)
    gf_win, gb_win = win_from_padded(matmul_tn(s["h"], dp, "dw_in"))
    sent_in, token = exchange_start([gb_win], ["slot"], f"grads_in_start{l}")
    dx, dnw = inproj_bwd(dp, w["w_in"], s["x"], w["norm_w"] + token[0:1, 0:1], dxo)
    recv_out = exchange_wait(sent_out, out_kinds, dx, f"grads_out_wait{l}")
    recv_in = exchange_wait(sent_in, ["slot"], dx, f"grads_in_wait{l}")
    half = {"w_in": partial_sum(gf_win, "slot", recv_in[0], chip, "psum_w_in", transposed=True)}
    for (n, kind), r in zip(OUT_MATS, recv_out):
        half[n] = partial_sum(full[n][0], kind, r, chip, "psum_" + n)
    small = dict(norm_w=dnw, b_gate=dbg, hgrn_norm_w=dhw, gdn_norm_w=dgw, lbs=dlbs, conv_a=dca[0:3], conv_c=dcc[0:4],
                 a_log=dsmall[0:1, 8:16], dt_bias=dsmall[1:2, 8:16])
    return dx, small, half


def lbs_fwd(lb):
    def body(lb_ref, o_ref):
        l0, l1 = lb_ref[0:1, :], lb_ref[1:2, :]
        mx = jnp.maximum(l0, l1)
        e0, e1 = jnp.exp(l0 - mx), jnp.exp(l1 - mx)
        o_ref[0:1, :] = jnp.zeros_like(l0)
        o_ref[1:2, :] = e1 / (e0 + e1)
    return pl.pallas_call(body, name="lbs_fwd", out_shape=jax.ShapeDtypeStruct(lb.shape, f32))(lb)


def _adam_math(w, g, m, v):
    mn = ADAM_B1 * m + (1.0 - ADAM_B1) * g
    vn = ADAM_B2 * v + (1.0 - ADAM_B2) * (g * g)
    mh = mn / (1.0 - ADAM_B1 ** ADAM_STEP)
    vh = vn / (1.0 - ADAM_B2 ** ADAM_STEP)
    return -ADAM_LR * (mh / (jnp.sqrt(vh) + ADAM_EPS) + ADAM_WD * w), mn, vn


def adam(w, g, m, v, name):
    R, C = w.shape
    tr = 256 if R % 256 == 0 else R

    def body(w_ref, g_ref, m_ref, v_ref, d_ref, mo_ref, vo_ref):
        d, mn, vn = _adam_math(w_ref[...], g_ref[...], m_ref[...], v_ref[...])
        d_ref[...] = d
        mo_ref[...] = mn
        vo_ref[...] = vn

    spec = pl.BlockSpec((tr, C), lambda i: (i, 0))
    return pl.pallas_call(
        body, name=name, grid=(R // tr,), in_specs=[spec] * 4, out_specs=[spec] * 3,
        out_shape=[jax.ShapeDtypeStruct((R, C), f32)] * 3, compiler_params=_cp(("parallel",)),
    )(w, g, m, v)


def adam_pair(h, hs, w, m, v, name):
    _, R, C = w.shape
    cp = h[0].shape[1]
    tr = 128 if R % 128 == 0 else R
    nt = R // tr

    def body(h0_ref, h1_ref, s0_ref, s1_ref, w_ref, m_ref, v_ref, g_ref, d_ref, mo_ref, vo_ref):
        def update(h_ref, s_ref):
            g = (h_ref[...] + s_ref[...])[:, :C]
            d, mn, vn = _adam_math(w_ref[...], g, m_ref[...], v_ref[...])
            g_ref[...] = g
            d_ref[...] = d
            mo_ref[...] = mn
            vo_ref[...] = vn

        @pl.when(pl.program_id(0) == 0)
        def _():
            update(h0_ref, s0_ref)

        @pl.when(pl.program_id(0) == 1)
        def _():
            update(h1_ref, s1_ref)

    h0spec = pl.BlockSpec((tr, cp), lambda l, i: (jnp.where(l == 0, i, nt - 1), 0))
    h1spec = pl.BlockSpec((tr, cp), lambda l, i: (jnp.where(l == 1, i, 0), 0))
    spec = pl.BlockSpec((None, tr, C), lambda l, i: (l, i, 0))
    return pl.pallas_call(
        body, name=name, grid=(2, nt), in_specs=[h0spec, h1spec, h0spec, h1spec, spec, spec, spec],
        out_specs=[spec] * 4, out_shape=[jax.ShapeDtypeStruct(w.shape, f32)] * 4,
        compiler_params=_cp(("arbitrary", "arbitrary")),
    )(h[0], h[1], hs[0], hs[1], w, m, v)


_SMALL = (("norm_w", 2 * D), ("b_gate", 6 * D), ("lower_bounds", None), ("hgrn_norm_w", 2 * HK),
          ("gdn_norm_w", 2 * HK), ("a_log", 16), ("dt_bias", 16), ("final_norm_w", D), ("loss", None))
_CONV = (("conv_a", 2 * 3 * 512), ("conv_c", 2 * 4 * 2048))


def _small_rows(n):
    return 16 if n is None else -(-n // 1024) * 8


LB_ROW = sum(_small_rows(n) for _, n in _SMALL[:2])
ADAM_ROWS = sum(_small_rows(n) for _, n in _SMALL)
SMALL_ROWS = ADAM_ROWS + sum(_small_rows(n) for _, n in _CONV)


def small_update(parts, wp, mp, vp):
    def body(p_ref, w_ref, m_ref, v_ref, g_ref, d_ref, mo_ref, vo_ref):
        gs = p_ref[0]
        for i in range(1, 8):
            gs = gs + p_ref[i]
        w = w_ref[...]
        l0, l1 = w[LB_ROW:LB_ROW + 8], w[LB_ROW + 8:LB_ROW + 16]
        mx = jnp.maximum(l0, l1)
        e0, e1 = jnp.exp(l0 - mx), jnp.exp(l1 - mx)
        p0, p1 = e0 / (e0 + e1), e1 / (e0 + e1)
        dl1 = gs[LB_ROW + 8:LB_ROW + 16]
        s = p1 * dl1
        g = jnp.concatenate([gs[0:LB_ROW], -p0 * s, p1 * dl1 - p1 * s, gs[LB_ROW + 16:ADAM_ROWS]], axis=0)
        d, mn, vn = _adam_math(w, g, m_ref[...], v_ref[...])
        g_ref[0:ADAM_ROWS, :] = g
        g_ref[ADAM_ROWS:, :] = gs[ADAM_ROWS:]
        d_ref[...] = d
        mo_ref[...] = mn
        vo_ref[...] = vn
    sd = jax.ShapeDtypeStruct
    return pl.pallas_call(body, name="small_update",
                          out_shape=[sd((SMALL_ROWS, 128), f32)] + [sd((ADAM_ROWS, 128), f32)] * 3)(parts, wp, mp, vp)


def partial_sum(own, kind, recv, chip, name, transposed=False):
    _, r, c = recv.shape
    tr = 256 if r % 256 == 0 else r

    def body(chip_ref, o_ref, r_ref, out_ref):
        s = ((o_ref[...] + r_ref[0].astype(f32)) + r_ref[1].astype(f32)) + r_ref[2].astype(f32)
        out_ref[...] = s.T if transposed else s

    own_spec = {"slot": pl.BlockSpec((None, tr, c), lambda i, chip: (chip[0], i, 0)),
                "cols": pl.BlockSpec((tr, c), lambda i, chip: (i, chip[0])),
                "rows": pl.BlockSpec((tr, c), lambda i, chip: (chip[0] * (r // tr) + i, 0))}[kind]
    out_spec = pl.BlockSpec((c, tr), lambda i, chip: (0, i)) if transposed else pl.BlockSpec((tr, c), lambda i, chip: (i, 0))
    return pl.pallas_call(
        body, name=name,
        grid_spec=pltpu.PrefetchScalarGridSpec(
            num_scalar_prefetch=1, grid=(r // tr,),
            in_specs=[own_spec, pl.BlockSpec((3, tr, c), lambda i, chip: (0, i, 0))], out_specs=out_spec),
        out_shape=jax.ShapeDtypeStruct((c, r) if transposed else (r, c), f32), compiler_params=_cp(("arbitrary",)),
    )(chip, own, recv)


def adam_pair_t(h, hs, wt, mt, vt, name):
    C, _, R = wt.shape
    tc = 128

    def body(h0_ref, h1_ref, s0_ref, s1_ref, w_ref, m_ref, v_ref, g_ref, d_ref, mo_ref, vo_ref):
        g = jnp.stack([h0_ref[...] + s0_ref[...], h1_ref[...] + s1_ref[...]], axis=1)
        d, mn, vn = _adam_math(w_ref[...], g, m_ref[...], v_ref[...])
        g_ref[...] = g
        d_ref[...] = d
        mo_ref[...] = mn
        vo_ref[...] = vn

    hspec = pl.BlockSpec((tc, R), lambda i: (i, 0))
    spec = pl.BlockSpec((tc, 2, R), lambda i: (i, 0, 0))
    return pl.pallas_call(
        body, name=name, grid=(pl.cdiv(C, tc),), in_specs=[hspec] * 4 + [spec] * 3, out_specs=[spec] * 4,
        out_shape=[jax.ShapeDtypeStruct(wt.shape, f32)] * 4, compiler_params=_cp(("parallel",)),
    )(h[0], h[1], hs[0], hs[1], wt, mt, vt)


MESH = pl.DeviceIdType.MESH
_HBM = pl.BlockSpec(memory_space=pltpu.HBM)


def _place():
    return lax.axis_index("x"), lax.axis_index("y"), lax.axis_index("c")


def weight_gather(arrs):
    n = len(arrs)

    def body(*refs):
        x_refs, out_refs = refs[:n], refs[n:2 * n]
        send_sems, recv_sems, local_sems = refs[2 * n:]
        x, y, c = _place()
        me, sibling = (x, y, c), (x, y, 1 - c)
        chips = [(1 - x, y), (x, 1 - y), (1 - x, 1 - y)]

        def copy(a, k, block, to, own_src=False):
            px, py, pc = block
            dst = out_refs[a].at[2 * px + py, pc]
            return pltpu.make_async_remote_copy(
                src_ref=x_refs[a].at[c] if own_src else dst, dst_ref=dst,
                send_sem=send_sems.at[7 * a + k], recv_sem=recv_sems.at[7 * a + k], device_id=to, device_id_type=MESH)

        mine = [pltpu.make_async_copy(x_refs[a].at[c], out_refs[a].at[2 * x + y, c], local_sems.at[a])
                for a in range(n)]
        for cp in mine:
            cp.start()
        first = []
        for a in range(n):
            first.append(copy(a, 0, me, sibling, own_src=True))
            first += [copy(a, 1 + j, me, (*chip, c), own_src=True) for j, chip in enumerate(chips)]
        for cp in first:
            cp.start()
        passed = []
        for j, chip in enumerate(chips):
            for a in range(n):
                copy(a, 1 + j, (*chip, c), me).wait_recv()
                fwd = copy(a, 4 + j, (*chip, c), sibling)
                fwd.start()
                passed.append(fwd)
        for a in range(n):
            copy(a, 0, sibling, me).wait_recv()
            for j, chip in enumerate(chips):
                copy(a, 4 + j, (*chip, 1 - c), me).wait_recv()
        for cp in first + passed:
            cp.wait_send()
        for cp in mine:
            cp.wait()

    return pl.pallas_call(
        body, name="weight_gather", in_specs=[_HBM] * n, out_specs=[_HBM] * n,
        out_shape=[jax.ShapeDtypeStruct((N_CHIPS,) + a.shape, a.dtype) for a in arrs],
        scratch_shapes=[pltpu.SemaphoreType.DMA((7 * n,)), pltpu.SemaphoreType.DMA((7 * n,)),
                        pltpu.SemaphoreType.DMA((n,))],
    )(*arrs)


SHARD_W = 256


_SEM = pl.BlockSpec(memory_space=pltpu.SEMAPHORE)
_EFFECT = pltpu.SideEffectType.DATAFLOW_SIDE_EFFECTING


def _landing_shape(a, kind):
    if kind == "all":
        return (N_CHIPS,) + a.shape
    if kind == "slot":
        return (3,) + a.shape[1:]
    return (3,) + ((a.shape[0], SHARD_W) if kind == "cols" else (SHARD_W, a.shape[1]))


def _shard_copies(src_refs, land_refs, kinds, send_sems, recv_sems):
    x, y, c = _place()
    copies = []
    for a, (src, land, kind) in enumerate(zip(src_refs, land_refs, kinds)):
        for j, (px, py) in enumerate(((1 - x, y), (x, 1 - y), (1 - x, 1 - y))):
            q = 2 * px + py
            lo = pl.multiple_of(q * SHARD_W, SHARD_W)
            part = {"slot": lambda: src.at[q], "cols": lambda: src.at[:, pl.ds(lo, SHARD_W)],
                    "rows": lambda: src.at[pl.ds(lo, SHARD_W), :], "all": lambda: src}[kind]()
            k = 3 * a + j
            copies.append(pltpu.make_async_remote_copy(
                src_ref=part, dst_ref=land.at[2 * x + y] if kind == "all" else land.at[j],
                send_sem=send_sems.at[k], recv_sem=recv_sems.at[k], device_id=(px, py, c), device_id_type=MESH))
    return copies


def exchange_start(srcs, kinds, name, after=None):
    n = len(srcs)
    lands = [lax.empty(_landing_shape(a, k), a.dtype) for a, k in zip(srcs, kinds)]
    extra = [] if after is None else [after]

    def body(*refs):
        src_refs, land_refs, token = refs[:n], refs[n:2 * n], refs[-1]
        send_sems, recv_sems = refs[2 * n + len(extra)], refs[2 * n + len(extra) + 1]
        for cp in _shard_copies(src_refs, land_refs, kinds, send_sems, recv_sems):
            cp.start()
        token[...] = jnp.zeros_like(token)

    both = list(srcs) + lands
    out = pl.pallas_call(
        body, name=name,
        out_shape=(pltpu.SemaphoreType.DMA((3 * n,)), pltpu.SemaphoreType.DMA((3 * n,)),
                   *[pltpu.HBM(a.shape, a.dtype) for a in both], jax.ShapeDtypeStruct((8, 128), f32)),
        in_specs=[_HBM] * (2 * n) + [pl.BlockSpec(memory_space=pl.ANY)] * len(extra),
        out_specs=(_SEM, _SEM, *[_HBM] * (2 * n), pl.BlockSpec(memory_space=pltpu.VMEM)),
        input_output_aliases={i: 2 + i for i in range(2 * n)},
        compiler_params=pltpu.CompilerParams(has_side_effects=_EFFECT),
    )(*[pltpu.with_memory_space_constraint(a, pltpu.HBM) for a in both], *extra)
    return (out[0], out[1], out[2:2 + 2 * n]), out[-1]


def exchange_wait(handle, kinds, after, name):
    send_sems, recv_sems, both = handle
    n = len(kinds)

    def body(*refs):
        src_refs, land_refs, s_sems, r_sems = refs[:n], refs[n:2 * n], refs[2 * n], refs[2 * n + 1]
        for cp in _shard_copies(src_refs, land_refs, kinds, s_sems, r_sems):
            cp.wait_send()
            cp.wait_recv()

    out = pl.pallas_call(
        body, name=name, out_shape=tuple(pltpu.HBM(a.shape, a.dtype) for a in both),
        in_specs=[_HBM] * (2 * n) + [_SEM, _SEM, pl.BlockSpec(memory_space=pl.ANY)], out_specs=tuple([_HBM] * (2 * n)),
        input_output_aliases={i: i for i in range(2 * n)},
        compiler_params=pltpu.CompilerParams(has_side_effects=_EFFECT),
    )(*both, send_sems, recv_sems, after)
    return out[n:]


def final_exchange(hs, small):
    n = len(hs)
    S = small.shape[0]

    def body(*refs):
        h_refs, sm_ref, out_refs, smalls_ref = refs[:n], refs[n], refs[n + 1:2 * n + 1], refs[2 * n + 1]
        send_sems, recv_sems, local_sem = refs[2 * n + 2:]
        x, y, c = _place()
        my_slot = smalls_ref.at[4 * x + 2 * y + c]
        mine = pltpu.make_async_copy(sm_ref, my_slot, local_sem)
        mine.start()
        copies = [pltpu.make_async_remote_copy(src_ref=h_refs[a], dst_ref=out_refs[a], send_sem=send_sems.at[a],
                                               recv_sem=recv_sems.at[a], device_id=(x, y, 1 - c), device_id_type=MESH)
                  for a in range(n)]
        for mask in range(1, 8):
            fx, fy, fc = (mask >> 2) & 1, (mask >> 1) & 1, mask & 1
            peer = ((1 - x) if fx else x, (1 - y) if fy else y, (1 - c) if fc else c)
            copies.append(pltpu.make_async_remote_copy(
                src_ref=sm_ref, dst_ref=my_slot, send_sem=send_sems.at[n - 1 + mask], recv_sem=recv_sems.at[n - 1 + mask],
                device_id=peer, device_id_type=MESH))
        for cp in copies:
            cp.start()
        for cp in copies:
            cp.wait_recv()
        for cp in copies:
            cp.wait_send()
        mine.wait()

    sd = jax.ShapeDtypeStruct
    out = pl.pallas_call(
        body, name="final_exchange", in_specs=[_HBM] * (n + 1), out_specs=[_HBM] * (n + 1),
        out_shape=[sd(h.shape, h.dtype) for h in hs] + [sd((8, S, 128), f32)],
        scratch_shapes=[pltpu.SemaphoreType.DMA((n + 7,)), pltpu.SemaphoreType.DMA((n + 7,)),
                        pltpu.SemaphoreType.DMA],
    )(*hs, small)
    return out[:n], out[n]


N_CHIPS = 4
SHARD_COLS = N_ORIG // N_CHIPS


SHARD_PAD = 2688
_COL_SEGMENTS = (
    ((0, 2048, OFF_A),)
    + tuple((2048 + 512 * j + HK * h, 2048 + 512 * j + HK * (h + 1), OFF_B + REG_BH * h + HK * j)
            for j in range(3) for h in range(N_HGRN))
    + ((3584, 4096, OFF_M + M_BZ), (4096, 6144, OFF_C), (6144, 6160, OFF_C + 2048), (6160, 7184, OFF_M + M_CZ),
       (7184, N_ORIG, OFF_M + M_G)))


def _shard_pieces():
    pieces = []
    for lo, hi, dst in _COL_SEGMENTS:
        for p in range(N_CHIPS):
            a, b = max(lo, p * SHARD_COLS), min(hi, (p + 1) * SHARD_COLS)
            if a < b:
                pieces.append((p, a - p * SHARD_COLS, dst + a - lo, b - a))
    return pieces


def win_cast_pad(wt):
    tc = 128

    def body(x_ref, o0_ref, o1_ref):
        col = pl.program_id(0) * tc + lax.broadcasted_iota(jnp.int32, (tc, 1), 0)
        for l, o_ref in enumerate((o0_ref, o1_ref)):
            o_ref[...] = _b(jnp.where(col < SHARD_COLS, x_ref[:, l, :], 0.0).T)

    spec = pl.BlockSpec((D, tc), lambda i: (0, i))
    return pl.pallas_call(
        body, name="win_cast_pad", grid=(SHARD_PAD // tc,),
        in_specs=[pl.BlockSpec((tc, 2, D), lambda i: (i, 0, 0))], out_specs=[spec, spec],
        out_shape=[jax.ShapeDtypeStruct((D, SHARD_PAD), bf16)] * 2, compiler_params=_cp(("parallel",)),
    )(wt)


def win_to_padded(w4, name):
    tr = 256
    pieces = _shard_pieces()

    def body(a_ref, o_ref):
        o_ref[...] = jnp.zeros((tr, NP), bf16)
        for p, j0, c0, n in pieces:
            o_ref[:, c0:c0 + n] = a_ref[p, :, j0:j0 + n]

    return pl.pallas_call(
        body, name=name, grid=(D // tr,),
        in_specs=[pl.BlockSpec((N_CHIPS, tr, SHARD_PAD), lambda i: (0, i, 0))],
        out_specs=pl.BlockSpec((tr, NP), lambda i: (i, 0)),
        out_shape=jax.ShapeDtypeStruct((D, NP), bf16), compiler_params=_cp(("parallel",)),
    )(w4)


def win_from_padded(dw):
    tr = 128
    pieces = _shard_pieces()

    def body(d_ref, of_ref, ob_ref):
        for p in range(N_CHIPS):
            of_ref[p, :, SHARD_COLS:] = jnp.zeros((tr, SHARD_PAD - SHARD_COLS), f32)
            ob_ref[p, :, SHARD_COLS:] = jnp.zeros((tr, SHARD_PAD - SHARD_COLS), bf16)
        for p, j0, c0, n in pieces:
            v = d_ref[:, c0:c0 + n]
            of_ref[p, :, j0:j0 + n] = v
            ob_ref[p, :, j0:j0 + n] = _b(v)

    out_spec = pl.BlockSpec((N_CHIPS, tr, SHARD_PAD), lambda i: (0, i, 0))
    return pl.pallas_call(
        body, name="win_from_padded", grid=(D // tr,),
        in_specs=[pl.BlockSpec((tr, NP), lambda i: (i, 0))], out_specs=[out_spec, out_spec],
        out_shape=[jax.ShapeDtypeStruct((N_CHIPS, D, SHARD_PAD), f32),
                   jax.ShapeDtypeStruct((N_CHIPS, D, SHARD_PAD), bf16)],
        compiler_params=_cp(("parallel",)),
    )(dw)


def _rows128(a):
    flat = a.reshape(-1)
    total = -(-flat.shape[0] // 1024) * 1024
    return jnp.pad(flat, (0, total - flat.shape[0])).reshape(total // 128, 128)


def _lb_rows(lb):
    return jnp.pad(lb.reshape(2, 4, 128), ((0, 0), (0, 4), (0, 0))).reshape(16, 128)


def _pack_small(v, with_conv):
    rows = []
    for name, n in _SMALL + (_CONV if with_conv else ()):
        if name == "lower_bounds":
            rows.append(_lb_rows(v[name]))
        elif name == "loss":
            rows.append(jnp.broadcast_to(v[name], (16, 128)) if name in v else jnp.zeros((16, 128), f32))
        else:
            rows.append(_rows128(v[name]))
    return jnp.concatenate(rows, axis=0)


def _unpack_small(p, shapes, with_conv):
    out, row = {}, 0
    for name, n in _SMALL + (_CONV if with_conv else ()):
        nrows = _small_rows(n)
        blk = p[row:row + nrows]
        if name == "lower_bounds":
            out[name] = blk.reshape(2, 8, 128)[:, :4].reshape(2, 512)
        elif name == "loss":
            out[name] = blk[0, 0]
        else:
            out[name] = blk.reshape(-1)[:n].reshape(shapes[name])
        row += nrows
    return out


def _lane_vec(a8):
    return jnp.pad(a8.reshape(1, 8), ((0, 0), (8, 112)))


WEIGHT_NAMES = ("norm_w", "w_in", "b_gate", "conv_a", "conv_c", "a_log", "dt_bias", "lower_bounds", "hgrn_norm_w",
                "gdn_norm_w", "w_out_a", "w_out_b", "w_out_c", "w_o", "final_norm_w")


def kernel(x, norm_w, w_in, b_gate, conv_a, conv_c, a_log, dt_bias, lower_bounds, hgrn_norm_w, gdn_norm_w, w_out_a, w_out_b, w_out_c, w_o, final_norm_w, loss_target, m_norm_w, m_w_in, m_b_gate, m_conv_a, m_conv_c, m_a_log, m_dt_bias, m_lower_bounds, m_hgrn_norm_w, m_gdn_norm_w, m_w_out_a, m_w_out_b, m_w_out_c, m_w_o, m_final_norm_w, v_norm_w, v_w_in, v_b_gate, v_conv_a, v_conv_c, v_a_log, v_dt_bias, v_lower_bounds, v_hgrn_norm_w, v_gdn_norm_w, v_w_out_a, v_w_out_b, v_w_out_c, v_w_o, v_final_norm_w):
    wts = dict(norm_w=norm_w, w_in=w_in, b_gate=b_gate, conv_a=conv_a, conv_c=conv_c, a_log=a_log, dt_bias=dt_bias,
               lower_bounds=lower_bounds, hgrn_norm_w=hgrn_norm_w, gdn_norm_w=gdn_norm_w, w_out_a=w_out_a,
               w_out_b=w_out_b, w_out_c=w_out_c, w_o=w_o, final_norm_w=final_norm_w)
    mom = dict(norm_w=m_norm_w, w_in=m_w_in, b_gate=m_b_gate, conv_a=m_conv_a, conv_c=m_conv_c, a_log=m_a_log,
               dt_bias=m_dt_bias, lower_bounds=m_lower_bounds, hgrn_norm_w=m_hgrn_norm_w, gdn_norm_w=m_gdn_norm_w,
               w_out_a=m_w_out_a, w_out_b=m_w_out_b, w_out_c=m_w_out_c, w_o=m_w_o, final_norm_w=m_final_norm_w)
    var = dict(norm_w=v_norm_w, w_in=v_w_in, b_gate=v_b_gate, conv_a=v_conv_a, conv_c=v_conv_c, a_log=v_a_log,
               dt_bias=v_dt_bias, lower_bounds=v_lower_bounds, hgrn_norm_w=v_hgrn_norm_w, gdn_norm_w=v_gdn_norm_w,
               w_out_a=v_w_out_a, w_out_b=v_w_out_b, w_out_c=v_w_out_c, w_o=v_w_o, final_norm_w=v_final_norm_w)
    chip = 2 * lax.axis_index("x") + lax.axis_index("y")
    chip1 = chip.reshape(1).astype(jnp.int32)

    win_l0, win_l1 = win_cast_pad(jnp.transpose(w_in, (2, 0, 1)))
    win4_l0, ca4, cc4 = weight_gather([win_l0.reshape(2, D // 2, SHARD_PAD), conv_a, conv_c])
    by_cols = lambda a: a.transpose(1, 2, 0, 3).reshape(a.shape[1], a.shape[2], N_CHIPS * a.shape[3])
    by_rows = lambda a: a.transpose(1, 0, 2, 3).reshape(a.shape[1], N_CHIPS * a.shape[2], a.shape[3])
    conv_a_full, conv_c_full = by_cols(ca4), by_cols(cc4)
    later = [win_l1, _b(w_out_a), _b(w_out_b), _b(w_out_c), _b(w_o)]
    sent_w, token = exchange_start(later, ["all"] * 5, "weights_start", after=win4_l0)

    def late_weights(after):
        lands = exchange_wait(sent_w, ["all"] * 5, after, "weights_wait")
        l1, woa4, wob4, woc4, wo4 = (lax.dynamic_update_index_in_dim(land, own, chip, 0)
                                     for land, own in zip(lands, later))
        outs = dict(w_out_a=by_cols(woa4), w_out_b=by_cols(wob4), w_out_c=by_rows(woc4), w_o=by_rows(wo4))
        layers[1].update(outs, w_in=win_to_padded(l1, "win_to_padded1"))
        return outs

    lbs = lbs_fwd(lower_bounds)
    layers = []
    for l in range(2):
        layers.append(dict(
            l=l, norm_w=norm_w[l:l + 1], b_gate=b_gate[l:l + 1], conv_a=conv_a_full[l], conv_c=conv_c_full[l],
            alog_l=_lane_vec(a_log[l]), dtb_l=_lane_vec(dt_bias[l]), lbs=lbs[l:l + 1],
            hgrn_norm_w=hgrn_norm_w[l:l + 1], gdn_norm_w=gdn_norm_w[l:l + 1]))
    layers[0].update(w_in=win_to_padded(win4_l0.reshape(N_CHIPS, D, SHARD_PAD), "win_to_padded0"), late=late_weights,
                     norm_w=norm_w[0:1] + token[0:1, 0:1])

    xs, saved = x[0], []
    for l in range(2):
        xs, s = layer_fwd(xs, layers[l])
        saved.append(s)
    loss_row, dx, dfw = loss_head(xs, final_norm_w.reshape(1, D), loss_target[0])
    lg, half = [None, None], [None, None]
    for l in (1, 0):
        dx, lg[l], half[l] = layer_bwd(dx, layers[l], saved[l], chip1)
    grad_x = dx[None]

    stack = lambda n: jnp.stack([lg[0][n], lg[1][n]], axis=0)
    gsmall = {n: stack(n) for n in ("norm_w", "b_gate", "hgrn_norm_w", "gdn_norm_w", "a_log", "dt_bias", "conv_a",
                                    "conv_c")}
    gsmall.update(lower_bounds=stack("lbs"), final_norm_w=dfw, loss=loss_row)
    mat_names = ("w_in",) + tuple(n for n, _ in OUT_MATS)
    mine = [half[l][n] for n in mat_names for l in range(2)]
    theirs, smalls = final_exchange(mine, _pack_small(gsmall, True))

    out_g, out_d, out_m, out_v = {}, {}, {}, {}
    for i, n in enumerate(mat_names):
        h, hs = mine[2 * i:2 * i + 2], theirs[2 * i:2 * i + 2]
        if n == "w_in":
            fwd, back = (lambda a: jnp.transpose(a, (2, 0, 1))), (lambda a: jnp.transpose(a, (1, 2, 0)))
            res = adam_pair_t(h, hs, fwd(wts[n]), fwd(mom[n]), fwd(var[n]), "adam_" + n)
            out_g[n], out_d[n], out_m[n], out_v[n] = (back(a) for a in res)
        else:
            out_g[n], out_d[n], out_m[n], out_v[n] = adam_pair(h, hs, wts[n], mom[n], var[n], "adam_" + n)
    small_names = [n for n, _ in _SMALL if n != "loss"]
    pack = lambda v: _pack_small({n: v[n] for n in small_names}, False)
    sg, sd, smn, svn = small_update(smalls, pack(wts), pack(mom), pack(var))
    shapes = {n: wts[n].shape for n in small_names}
    shapes.update(conv_a=(2, 3, 512), conv_c=(2, 4, 2048))
    for dst, src, conv in ((out_g, sg, True), (out_d, sd, False), (out_m, smn, False), (out_v, svn, False)):
        dst.update(_unpack_small(src, shapes, conv))
    loss = out_g.pop("loss")
    for n in ("conv_a", "conv_c"):
        width = wts[n].shape[2]
        g = lax.dynamic_slice_in_dim(out_g[n], chip * width, width, axis=2)
        two_d = lambda a: a.reshape(-1, width)
        d, mn, vn = adam(two_d(wts[n]), two_d(g), two_d(mom[n]), two_d(var[n]), "adam_" + n)
        out_g[n] = g
        out_d[n], out_m[n], out_v[n] = (a.reshape(wts[n].shape) for a in (d, mn, vn))
    return (loss, grad_x, *[out_g[n] for n in WEIGHT_NAMES], *[out_d[n] for n in WEIGHT_NAMES],
            *[out_m[n] for n in WEIGHT_NAMES], *[out_v[n] for n in WEIGHT_NAMES])
```

```python
import functools

import jax
import jax.numpy as jnp
from jax import lax
from jax.experimental import pallas as pl
from jax.experimental.pallas import tpu as pltpu

f32 = jnp.float32
bf16 = jnp.bfloat16

D = 1024
L = 64
SUB = 16
NORM_EPS = 1e-6
L2_EPS = 1e-6
MIN_F = 1e-30
HK = 128
QK_SCALE = HK ** -0.5
N_GDN = 8
GDN_BLOCK = 1024
N_HGRN = 4

REG_A = 2304
REG_C = 2304
REG_M = 4608
REG_BH = 384
OFF_A, OFF_C, OFF_M, OFF_B = 0, 2304, 4608, 9216
M_BZ, M_G, M_CZ = 0, 512, 3584
NP = 10752
NP_TILE = 1536
N_ORIG = 10256

ADAM_LR, ADAM_B1, ADAM_B2, ADAM_EPS, ADAM_WD, ADAM_STEP = 0.001, 0.9, 0.999, 1e-08, 0.01, 10

VMEM_LIMIT = 56 * 1024 * 1024


def _cp(sem):
    return pltpu.CompilerParams(dimension_semantics=sem, vmem_limit_bytes=VMEM_LIMIT)


def _sigmoid(x):
    return jax.nn.sigmoid(x)


def _silu(x):
    return x * _sigmoid(x)


def _silu2(x):
    s = _sigmoid(x)
    y = x * s
    return y, s + y * (1.0 - s)


def _softplus(x):
    u = jnp.exp(-jnp.abs(x))
    w = 1.0 + u
    l1p = jnp.where(w == 1.0, u, jnp.log(w) * (u / (w - 1.0)))
    return jnp.maximum(x, 0.0) + l1p


def _dot(a, b):
    return jnp.dot(a, b, preferred_element_type=f32)


def _dot_nt(a, b):
    return lax.dot_general(a, b, (((1,), (1,)), ((), ())), preferred_element_type=f32)


def _dot_tn(a, b):
    return lax.dot_general(a, b, (((0,), (0,)), ((), ())), preferred_element_type=f32)


def _bdot(a, b):
    return lax.dot_general(a, b, (((2,), (1,)), ((0,), (0,))), preferred_element_type=f32)


def _bdot_nt(a, b):
    return lax.dot_general(a, b, (((2,), (2,)), ((0,), (0,))), preferred_element_type=f32)


def _bdot_tn(a, b):
    return lax.dot_general(a, b, (((1,), (1,)), ((0,), (0,))), preferred_element_type=f32)


def _bdot_split(a, b):
    ah, bh = _b(a), _b(b)
    al, bl = _b(a - ah.astype(f32)), _b(b - bh.astype(f32))
    return _bdot(ah, bh) + (_bdot(ah, bl) + _bdot(al, bh))


def _b(x):
    return x.astype(bf16)


def _chunk_cumsum(x, rows_in_chunk):
    n = x.shape[0]
    for s in (1, 2, 4, 8, 16, 32):
        x = x + jnp.where(rows_in_chunk >= s, pltpu.roll(x, s, axis=0), 0.0)
    return x


def _chunk_rev_cumsum(x, rows_in_chunk):
    n = x.shape[0]
    for s in (1, 2, 4, 8, 16, 32):
        x = x + jnp.where(rows_in_chunk + s < L, pltpu.roll(x, n - s, axis=0), 0.0)
    return x


def _shift_down(x, s):
    return pltpu.roll(x, s, axis=0) if s else x


def _shift_up(x, s):
    return pltpu.roll(x, x.shape[0] - s, axis=0) if s else x


def inproj_fwd(x, nw, w):
    T = x.shape[0]
    tT, tn = min(2048, T), NP_TILE // 2

    def body(x_ref, nw_ref, w_ref, p_ref, h_ref, hs):
        @pl.when(pl.program_id(1) == 0)
        def _():
            xv = x_ref[...]
            r = lax.rsqrt(jnp.mean(xv * xv, axis=-1, keepdims=True) + NORM_EPS)
            hv = _b(xv * r * nw_ref[...])
            hs[...] = hv
            h_ref[...] = hv
        p_ref[...] = _dot(hs[...], w_ref[...])

    return pl.pallas_call(
        body, name="inproj_fwd", grid=(T // tT, NP // tn),
        in_specs=[pl.BlockSpec((tT, D), lambda i, j: (i, 0)), pl.BlockSpec((1, D), lambda i, j: (0, 0)),
                  pl.BlockSpec((D, tn), lambda i, j: (0, j))],
        out_specs=[pl.BlockSpec((tT, tn), lambda i, j: (i, j)), pl.BlockSpec((tT, D), lambda i, j: (i, 0))],
        out_shape=[jax.ShapeDtypeStruct((T, NP), f32), jax.ShapeDtypeStruct((T, D), bf16)],
        scratch_shapes=[pltpu.VMEM((tT, D), bf16)],
        compiler_params=_cp(("parallel", "arbitrary")),
    )(x, nw, w)


def matmul_tn(a, b, name, n=None, b_col0=0, with_bf16=False):
    T, K = a.shape
    N = b.shape[1] if n is None else n
    tT = min(2048, T)
    tn = NP_TILE if N % NP_TILE == 0 else min(N, 1024)
    nt = T // tT
    cb0 = b_col0 // tn

    def body(a_ref, b_ref, o_ref, *ob_ref):
        @pl.when(pl.program_id(1) == 0)
        def _():
            o_ref[...] = jnp.zeros_like(o_ref)
        o_ref[...] += _dot_tn(_b(a_ref[...]), _b(b_ref[...]))
        if with_bf16:
            @pl.when(pl.program_id(1) == nt - 1)
            def _():
                ob_ref[0][...] = _b(o_ref[...])

    ospec = pl.BlockSpec((K, tn), lambda j, t: (0, j))
    return pl.pallas_call(
        body, name=name, grid=(N // tn, nt),
        in_specs=[pl.BlockSpec((tT, K), lambda j, t: (t, 0)), pl.BlockSpec((tT, tn), lambda j, t: (t, cb0 + j))],
        out_specs=[ospec, ospec] if with_bf16 else ospec,
        out_shape=([jax.ShapeDtypeStruct((K, N), f32), jax.ShapeDtypeStruct((K, N), bf16)] if with_bf16
                   else jax.ShapeDtypeStruct((K, N), f32)),
        compiler_params=_cp(("parallel", "arbitrary")),
    )(a, b)


def inproj_bwd(dp, w, x, nw, dres):
    T = x.shape[0]
    tT, tk = min(1024, T), NP_TILE
    nk = NP // tk

    def body(dp_ref, w_ref, x_ref, nw_ref, dres_ref, dx_ref, dnw_ref, acc):
        i, k = pl.program_id(0), pl.program_id(1)

        @pl.when((i == 0) & (k == 0))
        def _():
            dnw_ref[...] = jnp.zeros_like(dnw_ref)

        @pl.when(k == 0)
        def _():
            acc[...] = jnp.zeros_like(acc)
        acc[...] += _dot_nt(dp_ref[...], w_ref[...])

        @pl.when(k == nk - 1)
        def _():
            xv = x_ref[...]
            r = lax.rsqrt(jnp.mean(xv * xv, axis=-1, keepdims=True) + NORM_EPS)
            xh = xv * r
            dy = acc[...]
            dyw = dy * nw_ref[...]
            dx_ref[...] = r * (dyw - xh * jnp.mean(dyw * xh, axis=-1, keepdims=True)) + dres_ref[...]
            dnw_ref[...] += jnp.sum(dy * xh, axis=0, keepdims=True)

    return pl.pallas_call(
        body, name="inproj_bwd", grid=(T // tT, nk),
        in_specs=[pl.BlockSpec((tT, tk), lambda i, k: (i, k)), pl.BlockSpec((D, tk), lambda i, k: (0, k)),
                  pl.BlockSpec((tT, D), lambda i, k: (i, 0)), pl.BlockSpec((1, D), lambda i, k: (0, 0)),
                  pl.BlockSpec((tT, D), lambda i, k: (i, 0))],
        out_specs=[pl.BlockSpec((tT, D), lambda i, k: (i, 0)), pl.BlockSpec((1, D), lambda i, k: (0, 0))],
        out_shape=[jax.ShapeDtypeStruct((T, D), f32), jax.ShapeDtypeStruct((1, D), f32)],
        scratch_shapes=[pltpu.VMEM((tT, D), f32)],
        compiler_params=_cp(("arbitrary", "arbitrary")),
    )(dp, w, x, nw, dres)


def loss_head(x, fw, tgt):
    T = x.shape[0]
    tT = min(512, T)

    def body(x_ref, fw_ref, t_ref, loss_ref, dx_ref, dfw_ref):
        @pl.when(pl.program_id(0) == 0)
        def _():
            loss_ref[...] = jnp.zeros_like(loss_ref)
            dfw_ref[...] = jnp.zeros_like(dfw_ref)
        xv = x_ref[...]
        r = lax.rsqrt(jnp.mean(xv * xv, axis=-1, keepdims=True) + NORM_EPS)
        xh = xv * r
        err = xh * fw_ref[...] - t_ref[...]
        part = 0.5 * jnp.sum(jnp.mean(err * err, axis=-1, keepdims=True), axis=0, keepdims=True)
        loss_ref[...] += jnp.broadcast_to(part, loss_ref.shape)
        dy = err * (1.0 / D)
        dyw = dy * fw_ref[...]
        dx_ref[...] = r * (dyw - xh * jnp.mean(dyw * xh, axis=-1, keepdims=True))
        dfw_ref[...] += jnp.sum(dy * xh, axis=0, keepdims=True)

    return pl.pallas_call(
        body, name="loss_head", grid=(T // tT,),
        in_specs=[pl.BlockSpec((tT, D), lambda i: (i, 0)), pl.BlockSpec((1, D), lambda i: (0, 0)),
                  pl.BlockSpec((tT, D), lambda i: (i, 0))],
        out_specs=[pl.BlockSpec((1, 128), lambda i: (0, 0)), pl.BlockSpec((tT, D), lambda i: (i, 0)),
                   pl.BlockSpec((1, D), lambda i: (0, 0))],
        out_shape=[jax.ShapeDtypeStruct((1, 128), f32), jax.ShapeDtypeStruct((T, D), f32),
                   jax.ShapeDtypeStruct((1, D), f32)],
        compiler_params=_cp(("arbitrary",)),
    )(x, fw, tgt)


def _halo_specs(tT, T, width, colblk):
    nb8 = T // 8
    per = tT // 8
    prev = pl.BlockSpec((8, width), lambda i: (jnp.maximum(i * per - 1, 0), colblk))
    nxt = pl.BlockSpec((8, width), lambda i: (jnp.minimum((i + 1) * per, nb8 - 1), colblk))
    return prev, nxt


def mixa_fwd(p, cw):
    T = p.shape[0]
    tT = min(512, T)
    prev_spec, _ = _halo_specs(tT, T, REG_A, OFF_A // REG_A)

    def body(p_ref, pp_ref, cw_ref, y_ref):
        pv = p_ref[...]
        u = pv[:, 512:1024] * pv[:, 1024:1536]
        pp = pp_ref[...]
        up = jnp.where(pl.program_id(0) == 0, 0.0, pp[:, 512:1024] * pp[:, 1024:1536])
        ue = jnp.concatenate([up, u], axis=0)
        cv = cw_ref[0:1, :] * _shift_down(ue, 2) + cw_ref[1:2, :] * _shift_down(ue, 1) + cw_ref[2:3, :] * ue
        y_ref[...] = _b(pv[:, 0:512] * cv[8:] * _silu(pv[:, 1536:2048]))

    return pl.pallas_call(
        body, name="mixa_fwd", grid=(T // tT,),
        in_specs=[pl.BlockSpec((tT, REG_A), lambda i: (i, OFF_A // REG_A)), prev_spec,
                  pl.BlockSpec((3, 512), lambda i: (0, 0))],
        out_specs=pl.BlockSpec((tT, 512), lambda i: (i, 0)),
        out_shape=jax.ShapeDtypeStruct((T, 512), bf16),
        compiler_params=_cp(("parallel",)),
    )(p, p, cw)


def mixa_bwd(p, cw, dy, dp):
    T = p.shape[0]
    tT = min(512, T)
    nt = T // tT
    prev_spec, next_spec = _halo_specs(tT, T, REG_A, OFF_A // REG_A)
    _, dnext_spec = _halo_specs(tT, T, 512, 0)

    def body(p_ref, pp_ref, pn_ref, cw_ref, dy_ref, dyn_ref, dp_in, dp_ref, dcw_ref):
        i = pl.program_id(0)

        @pl.when(i == 0)
        def _():
            dcw_ref[...] = jnp.zeros_like(dcw_ref)
        pv, pp, pn = p_ref[:, 0:2048], pp_ref[:, 0:2048], pn_ref[:, 0:2048]
        pe = jnp.concatenate([pp, pv, pn], axis=0)
        rows = lax.broadcasted_iota(jnp.int32, (tT + 16, 1), 0)
        ab, ac, ax, az = pe[:, 0:512], pe[:, 512:1024], pe[:, 1024:1536], pe[:, 1536:2048]
        u = jnp.where((rows < 8) & (i == 0), 0.0, ac * ax)
        u1, u2 = _shift_down(u, 1), _shift_down(u, 2)
        w0, w1, w2 = cw_ref[0:1, :], cw_ref[1:2, :], cw_ref[2:3, :]
        cv = w0 * u2 + w1 * u1 + w2 * u
        dye = jnp.concatenate([jnp.zeros((8, 512), f32), dy_ref[...], dyn_ref[...]], axis=0)
        dye = jnp.where((rows >= tT + 8) & (i == nt - 1), 0.0, dye)
        sz, dsz = _silu2(az)
        dcv = dye * ab * sz
        du = w2 * dcv + w1 * _shift_up(dcv, 1) + w0 * _shift_up(dcv, 2)
        inner = (rows >= 8) & (rows < tT + 8)
        dcv_in = jnp.where(inner, dcv, 0.0)
        dcw_ref[0:1, :] += jnp.sum(dcv_in * u2, axis=0, keepdims=True)
        dcw_ref[1:2, :] += jnp.sum(dcv_in * u1, axis=0, keepdims=True)
        dcw_ref[2:3, :] += jnp.sum(dcv_in * u, axis=0, keepdims=True)
        sl = slice(8, tT + 8)
        dp_ref[:, 0:512] = _b((dye * cv * sz)[sl])
        dp_ref[:, 512:1024] = _b((du * ax)[sl])
        dp_ref[:, 1024:1536] = _b((du * ac)[sl])
        dp_ref[:, 1536:2048] = _b((dye * ab * cv * dsz)[sl])
        dp_ref[:, 2048:] = jnp.zeros((tT, REG_A - 2048), bf16)

    return pl.pallas_call(
        body, name="mixa_bwd", grid=(nt,),
        in_specs=[pl.BlockSpec((tT, REG_A), lambda i: (i, OFF_A // REG_A)), prev_spec, next_spec,
                  pl.BlockSpec((3, 512), lambda i: (0, 0)),
                  pl.BlockSpec((tT, 512), lambda i: (i, 0)), dnext_spec, pl.BlockSpec(memory_space=pl.ANY)],
        out_specs=[pl.BlockSpec((tT, REG_A), lambda i: (i, OFF_A // REG_A)), pl.BlockSpec((8, 512), lambda i: (0, 0))],
        out_shape=[jax.ShapeDtypeStruct((T, NP), bf16), jax.ShapeDtypeStruct((8, 512), f32)],
        input_output_aliases={6: 0},
        compiler_params=_cp(("arbitrary",)),
    )(p, p, p, cw, dy, dy, dp)


def _l2n_fwd(y):
    return y * lax.rsqrt(jnp.sum(y * y, axis=-1, keepdims=True) + L2_EPS)


def mixc_pre_fwd(p, cw, alog_l, dtb_l):
    T = p.shape[0]
    tT = min(512, T)
    prev_spec, _ = _halo_specs(tT, T, REG_C, OFF_C // REG_C)

    def body(p_ref, pp_ref, cw_ref, al_ref, dt_ref, q_ref, k_ref, v_ref, sm_ref):
        pp = jnp.where(pl.program_id(0) == 0, 0.0, pp_ref[:, 0:2048])
        xe = jnp.concatenate([pp, p_ref[:, 0:2048]], axis=0)
        cv = (cw_ref[0:1, :] * _shift_down(xe, 3) + cw_ref[1:2, :] * _shift_down(xe, 2)
              + cw_ref[2:3, :] * _shift_down(xe, 1) + cw_ref[3:4, :] * xe)[8:]
        y = _silu(cv)
        for hh in range(4):
            sl = slice(hh * HK, (hh + 1) * HK)
            q_ref[:, sl] = _l2n_fwd(y[:, sl]) * QK_SCALE
            k_ref[:, sl] = _l2n_fwd(y[:, 512 + hh * HK:512 + (hh + 1) * HK])
        v_ref[...] = y[:, 1024:2048]
        ps = p_ref[:, 2048:2176]
        lane = lax.broadcasted_iota(jnp.int32, ps.shape, 1)
        la = -jnp.exp(al_ref[...]) * _softplus(ps + dt_ref[...])
        rin = lax.broadcasted_iota(jnp.int32, ps.shape, 0) % L
        g = _chunk_cumsum(la, rin)
        sm_ref[...] = jnp.where(lane < 8, _sigmoid(ps), jnp.where(lane < 16, g, 0.0))

    return pl.pallas_call(
        body, name="mixc_pre_fwd", grid=(T // tT,),
        in_specs=[pl.BlockSpec((tT, REG_C), lambda i: (i, OFF_C // REG_C)), prev_spec,
                  pl.BlockSpec((4, 2048), lambda i: (0, 0)),
                  pl.BlockSpec((1, 128), lambda i: (0, 0)), pl.BlockSpec((1, 128), lambda i: (0, 0))],
        out_specs=[pl.BlockSpec((tT, 512), lambda i: (i, 0)), pl.BlockSpec((tT, 512), lambda i: (i, 0)),
                   pl.BlockSpec((tT, 1024), lambda i: (i, 0)), pl.BlockSpec((tT, 128), lambda i: (i, 0))],
        out_shape=[jax.ShapeDtypeStruct((T, 512), f32), jax.ShapeDtypeStruct((T, 512), f32),
                   jax.ShapeDtypeStruct((T, 1024), f32), jax.ShapeDtypeStruct((T, 128), f32)],
        compiler_params=_cp(("parallel",)),
    )(p, p, cw, alog_l, dtb_l)


def mixc_pre_bwd(p, cw, alog_l, dtb_l, dq8, dk8, dv, dsm8, dp):
    T = p.shape[0]
    tT = min(256, T)
    nt = T // tT
    prev_spec, next_spec = _halo_specs(tT, T, REG_C, OFF_C // REG_C)
    _, n1024 = _halo_specs(tT, T, 1024, 0)

    def body(p_ref, pp_ref, pn_ref, cw_ref, al_ref, dt_ref, dq_ref, dqn_ref, dk_ref, dkn_ref,
             dv_ref, dvn_ref, dsm_ref, dp_in, dp_ref, dcw_ref, dsml_ref):
        i = pl.program_id(0)

        @pl.when(i == 0)
        def _():
            dcw_ref[...] = jnp.zeros_like(dcw_ref)
            dsml_ref[...] = jnp.zeros_like(dsml_ref)
        rows = lax.broadcasted_iota(jnp.int32, (tT + 16, 1), 0)
        pp = jnp.where(i == 0, 0.0, pp_ref[:, 0:2048])
        xe = jnp.concatenate([pp, p_ref[:, 0:2048], pn_ref[:, 0:2048]], axis=0)
        xs = [_shift_down(xe, 3), _shift_down(xe, 2), _shift_down(xe, 1), xe]
        cv = cw_ref[0:1, :] * xs[0] + cw_ref[1:2, :] * xs[1] + cw_ref[2:3, :] * xs[2] + cw_ref[3:4, :] * xs[3]
        y, dy_dcv = _silu2(cv)
        last = (rows >= tT + 8) & (i == nt - 1)
        z8q = jnp.zeros((8, 1024), f32)

        def ext(cur_ref, nxt_ref):
            return jnp.where(last, 0.0, jnp.concatenate([z8q, cur_ref[...], nxt_ref[...]], axis=0))
        dq8e, dk8e, dve = ext(dq_ref, dqn_ref), ext(dk_ref, dkn_ref), ext(dv_ref, dvn_ref)
        dys = []
        for (d8, base, scale) in ((dq8e, 0, QK_SCALE), (dk8e, 512, 1.0)):
            for hh in range(4):
                dn = (d8[:, (2 * hh) * HK:(2 * hh + 1) * HK] + d8[:, (2 * hh + 1) * HK:(2 * hh + 2) * HK]) * scale
                yh = y[:, base + hh * HK:base + (hh + 1) * HK]
                r = lax.rsqrt(jnp.sum(yh * yh, axis=-1, keepdims=True) + L2_EPS)
                nh = yh * r
                dys.append(r * (dn - nh * jnp.sum(dn * nh, axis=-1, keepdims=True)))
        dyy = jnp.concatenate(dys + [dve], axis=1)
        dcv = dyy * dy_dcv
        dx = (cw_ref[3:4, :] * dcv + cw_ref[2:3, :] * _shift_up(dcv, 1) + cw_ref[1:2, :] * _shift_up(dcv, 2)
              + cw_ref[0:1, :] * _shift_up(dcv, 3))
        dp_ref[:, 0:2048] = _b(dx[8:tT + 8])
        dp_ref[:, 2176:] = jnp.zeros((tT, REG_C - 2176), bf16)
        inner = (rows >= 8) & (rows < tT + 8)
        dcv_in = jnp.where(inner, dcv, 0.0)
        for j in range(4):
            dcw_ref[j:j + 1, :] += jnp.sum(dcv_in * xs[j], axis=0, keepdims=True)
        ps = p_ref[:, 2048:2176]
        lane = lax.broadcasted_iota(jnp.int32, ps.shape, 1)
        dsm = dsm_ref[:, 0:128]
        for hh in range(1, N_GDN):
            dsm = dsm + dsm_ref[:, hh * 128:(hh + 1) * 128]
        beta = _sigmoid(ps)
        xa = ps + dt_ref[...]
        nea = -jnp.exp(al_ref[...])
        dpa = dsm * nea * _sigmoid(xa)
        dp_ref[:, 2048:2176] = _b(jnp.where(lane < 8, dsm * beta * (1.0 - beta), jnp.where(lane < 16, dpa, 0.0)))
        amask = (lane >= 8) & (lane < 16)
        dsml_ref[0:1, :] += jnp.sum(jnp.where(amask, dsm * nea * _softplus(xa), 0.0), axis=0, keepdims=True)
        dsml_ref[1:2, :] += jnp.sum(jnp.where(amask, dpa, 0.0), axis=0, keepdims=True)

    cur1024 = pl.BlockSpec((tT, 1024), lambda i: (i, 0))
    return pl.pallas_call(
        body, name="mixc_pre_bwd", grid=(nt,),
        in_specs=[pl.BlockSpec((tT, REG_C), lambda i: (i, OFF_C // REG_C)), prev_spec, next_spec,
                  pl.BlockSpec((4, 2048), lambda i: (0, 0)),
                  pl.BlockSpec((1, 128), lambda i: (0, 0)), pl.BlockSpec((1, 128), lambda i: (0, 0)),
                  cur1024, n1024, cur1024, n1024, cur1024, n1024, cur1024, pl.BlockSpec(memory_space=pl.ANY)],
        out_specs=[pl.BlockSpec((tT, REG_C), lambda i: (i, OFF_C // REG_C)),
                   pl.BlockSpec((8, 2048), lambda i: (0, 0)), pl.BlockSpec((8, 128), lambda i: (0, 0))],
        out_shape=[jax.ShapeDtypeStruct((T, NP), bf16),
                   jax.ShapeDtypeStruct((8, 2048), f32), jax.ShapeDtypeStruct((8, 128), f32)],
        input_output_aliases={13: 0},
        compiler_params=_cp(("arbitrary",)),
    )(p, p, p, cw, alog_l, dtb_l, dq8, dq8, dk8, dk8, dv, dv, dsm8, dp)


def _tri_inverse(m):
    r = lax.broadcasted_iota(jnp.int32, (L, L), 0)
    c = lax.broadcasted_iota(jnp.int32, (L, L), 1)
    eye = (r == c).astype(f32)[None]
    same = lambda w: ((r // w) == (c // w))[None]
    md = jnp.where(same(8), m, 0.0)
    m2 = _bdot_split(md, md)
    m4 = _bdot_split(m2, m2)
    t = _bdot_split(_bdot_split(eye - md, eye + m2), eye + m4)
    for w in (16, 32, 64):
        mo = jnp.where(same(w) & jnp.logical_not(same(w // 2)), m, 0.0)
        t = t - _bdot_split(_bdot_split(t, mo), t)
    return t


def _col_to_row(col, eye):
    return jnp.sum(eye * col, axis=1, keepdims=True)


def _row_to_col(row, eye):
    return jnp.sum(eye * row, axis=2, keepdims=True)


def _gdn_chunk_terms(q, k, v, beta, g, t_inv=None):
    r = lax.broadcasted_iota(jnp.int32, (L, L), 0)
    c = lax.broadcasted_iota(jnp.int32, (L, L), 1)
    eye = (r == c).astype(f32)[None]
    causal, strict = (c <= r)[None], (c < r)[None]
    diff = g - _col_to_row(g, eye)
    dec = jnp.exp(jnp.where(causal, diff, 0.0))
    dc = jnp.where(causal, dec, 0.0)
    ds = jnp.where(strict, dec, 0.0)
    eg = jnp.exp(g)
    gl = g[:, L - 1:L, :]
    egl = jnp.exp(gl - g)
    kb = k * beta
    kk = _bdot_nt(_b(k), _b(kb))
    qk = _bdot_nt(_b(q), _b(kb))
    m = kk * ds
    aqk = qk * dc
    if t_inv is None:
        t_inv = _tri_inverse(m)
    tb = _b(t_inv)
    keg = k * eg
    u = _bdot(tb, _b(v))
    w = _bdot(tb, _b(keg))
    ks = kb * egl
    ksw = _bdot_tn(_b(ks), _b(w))
    return dict(eye=eye, causal=causal, strict=strict, dc=dc, ds=ds, eg=eg, gl=gl, egl=egl, kb=kb, kk=kk, qk=qk,
                m=m, aqk=aqk, t=t_inv, u=u, w=w, qi=q * eg, ks=ks, keg=keg, ksw=ksw)


def gdn_fwd(qn, kn, vv, sm):
    T = qn.shape[0]
    tB = min(2 * GDN_BLOCK, T)
    nc = tB // L
    N = T // L

    def body(q_ref, k_ref, v_ref, sm_ref, o_ref, st_ref, ti_ref, s_scr):
        h = pl.program_id(0)

        @pl.when(pl.program_id(1) == 0)
        def _():
            s_scr[...] = jnp.zeros_like(s_scr)
        smv = sm_ref[...]
        lane = lax.broadcasted_iota(jnp.int32, smv.shape, 1)
        beta = jnp.sum(jnp.where(lane == h, smv, 0.0), axis=1, keepdims=True).reshape(nc, L, 1)
        g = jnp.sum(jnp.where(lane == 8 + h, smv, 0.0), axis=1, keepdims=True).reshape(nc, L, 1)
        q = q_ref[...].reshape(nc, L, HK)
        k = k_ref[...].reshape(nc, L, HK)
        v = v_ref[...].reshape(nc, L, HK)
        tm = _gdn_chunk_terms(q, k, v, beta, g)
        ti_ref[...] = tm["t"]
        ksu = _bdot_tn(_b(tm["ks"]), _b(tm["u"]))
        kswb = _b(tm["ksw"])
        egl_last = jnp.exp(tm["gl"])
        s = s_scr[...]
        states = [None] * nc
        for ci in range(nc):
            states[ci] = s
            s = egl_last[ci] * s + (ksu[ci] - _dot(kswb[ci], _b(s)))
        s_scr[...] = s
        sall = jnp.stack(states, axis=0)
        st_ref[...] = sall
        sb = _b(sall)
        e = tm["u"] - _bdot(_b(tm["w"]), sb)
        o = _bdot(_b(tm["qi"]), sb) + _bdot(_b(tm["aqk"]), _b(e))
        o_ref[...] = o.reshape(tB, HK)

    return pl.pallas_call(
        body, name="gdn_fwd", grid=(N_GDN, T // tB),
        in_specs=[pl.BlockSpec((tB, HK), lambda h, n: (n, h // 2)), pl.BlockSpec((tB, HK), lambda h, n: (n, h // 2)),
                  pl.BlockSpec((tB, HK), lambda h, n: (n, h)), pl.BlockSpec((tB, 128), lambda h, n: (n, 0))],
        out_specs=[pl.BlockSpec((tB, HK), lambda h, n: (n, h)),
                   pl.BlockSpec((None, nc, HK, HK), lambda h, n: (h, n, 0, 0)),
                   pl.BlockSpec((None, nc, L, L), lambda h, n: (h, n, 0, 0))],
        out_shape=[jax.ShapeDtypeStruct((T, N_GDN * HK), f32), jax.ShapeDtypeStruct((N_GDN, N, HK, HK), f32),
                   jax.ShapeDtypeStruct((N_GDN, N, L, L), f32)],
        scratch_shapes=[pltpu.VMEM((HK, HK), f32)],
        compiler_params=_cp(("parallel", "arbitrary")),
    )(qn, kn, vv, sm)


def gdn_bwd(qn, kn, vv, sm, st, ti, do):
    T = qn.shape[0]
    tB = min(GDN_BLOCK, T)
    nc = tB // L
    nb = T // tB

    def body(q_ref, k_ref, v_ref, sm_ref, st_ref, ti_ref, do_ref, dq_ref, dk_ref, dv_ref, dsm_ref, ds_scr):
        h = pl.program_id(0)

        @pl.when(pl.program_id(1) == 0)
        def _():
            ds_scr[...] = jnp.zeros_like(ds_scr)
        smv = sm_ref[...]
        lane = lax.broadcasted_iota(jnp.int32, smv.shape, 1)
        beta = jnp.sum(jnp.where(lane == h, smv, 0.0), axis=1, keepdims=True).reshape(nc, L, 1)
        g = jnp.sum(jnp.where(lane == 8 + h, smv, 0.0), axis=1, keepdims=True).reshape(nc, L, 1)
        q = q_ref[...].reshape(nc, L, HK)
        k = k_ref[...].reshape(nc, L, HK)
        v = v_ref[...].reshape(nc, L, HK)
        do = do_ref[...].reshape(nc, L, HK)
        s = st_ref[...]
        tm = _gdn_chunk_terms(q, k, v, beta, g, t_inv=ti_ref[...])
        eye, dc, ds_, eg, egl = tm["eye"], tm["dc"], tm["ds"], tm["eg"], tm["egl"]
        kb, u, w, qi, ks, tinv = tm["kb"], tm["u"], tm["w"], tm["qi"], tm["ks"], tm["t"]
        sb, dob = _b(s), _b(do)
        e = u - _bdot(_b(w), sb)
        eb = _b(e)
        egl_last = jnp.exp(tm["gl"])
        de0 = _bdot_tn(_b(tm["aqk"]), dob)
        ds0 = _bdot_tn(_b(qi), dob) - _bdot_tn(_b(w), _b(de0))
        kswb = _b(tm["ksw"])
        dsn = ds_scr[...]
        dsns = [None] * nc
        for ci in reversed(range(nc)):
            dsns[ci] = dsn
            dsn = ds0[ci] + (egl_last[ci] * dsn - _dot_tn(kswb[ci], _b(dsn)))
        ds_scr[...] = dsn
        dsp = jnp.stack(dsns, axis=0)
        dspb = _b(dsp)
        de = de0 + _bdot(_b(ks), dspb)
        deb = _b(de)
        dks = _bdot_nt(eb, dspb)
        dqi = _bdot_nt(dob, sb)
        daqk = jnp.where(tm["causal"], _bdot_nt(dob, eb), 0.0)
        dw = -_bdot_nt(deb, sb)
        tb = _b(tinv)
        dvv = _bdot_tn(tb, deb)
        dkg = _bdot_tn(tb, _b(dw))
        dm = -jnp.where(tm["strict"], _bdot_nt(_b(dvv), _b(u)) + _bdot_nt(_b(dkg), _b(w)), 0.0)
        x = _b(dm * ds_)
        y = _b(daqk * dc)
        kbb, kbf, qbf = _b(kb), _b(k), _b(q)
        dk = _bdot(x, kbb) + dkg * eg
        dkb = _bdot_tn(x, kbf) + _bdot_tn(y, qbf) + dks * egl
        dq = _bdot(y, kbb) + dqi * eg
        dk = dk + dkb * beta
        dbeta = jnp.sum(dkb * k, axis=-1, keepdims=True)
        z = dm * tm["m"] + daqk * tm["aqk"]
        dg = (jnp.sum(dqi * qi - dks * ks + dkg * tm["keg"], axis=-1, keepdims=True)
              + jnp.sum(z, axis=-1, keepdims=True) - _row_to_col(jnp.sum(z, axis=1, keepdims=True), eye))
        dgl = (egl_last * jnp.sum(jnp.sum(s * dsp, axis=2, keepdims=True), axis=1, keepdims=True)
               + jnp.sum(jnp.sum(dks * ks, axis=2, keepdims=True), axis=1, keepdims=True))
        rowi = lax.broadcasted_iota(jnp.int32, (nc, L, 1), 1)
        dg = dg + jnp.where(rowi == L - 1, dgl, 0.0)
        dg2 = dg.reshape(tB, 1)
        rin = lax.broadcasted_iota(jnp.int32, (tB, 1), 0) % L
        dla = _chunk_rev_cumsum(jnp.broadcast_to(dg2, (tB, 128)), jnp.broadcast_to(rin, (tB, 128)))
        dq_ref[...] = dq.reshape(tB, HK)
        dk_ref[...] = dk.reshape(tB, HK)
        dv_ref[...] = dvv.reshape(tB, HK)
        dsm_ref[...] = jnp.where(lane == h, dbeta.reshape(tB, 1), jnp.where(lane == 8 + h, dla, 0.0))

    rev = lambda n: nb - 1 - n
    return pl.pallas_call(
        body, name="gdn_bwd", grid=(N_GDN, nb),
        in_specs=[pl.BlockSpec((tB, HK), lambda h, n: (rev(n), h // 2)),
                  pl.BlockSpec((tB, HK), lambda h, n: (rev(n), h // 2)),
                  pl.BlockSpec((tB, HK), lambda h, n: (rev(n), h)), pl.BlockSpec((tB, 128), lambda h, n: (rev(n), 0)),
                  pl.BlockSpec((None, nc, HK, HK), lambda h, n: (h, rev(n), 0, 0)),
                  pl.BlockSpec((None, nc, L, L), lambda h, n: (h, rev(n), 0, 0)),
                  pl.BlockSpec((tB, HK), lambda h, n: (rev(n), h))],
        out_specs=[pl.BlockSpec((tB, HK), lambda h, n: (rev(n), h))] * 4,
        out_shape=[jax.ShapeDtypeStruct((T, N_GDN * HK), f32)] * 4,
        scratch_shapes=[pltpu.VMEM((HK, HK), f32)],
        compiler_params=_cp(("parallel", "arbitrary")),
    )(qn, kn, vv, sm, st, ti, do)


def _hgrn_prep(bq, bf_, bi, lb):
    tB = bq.shape[0]
    sq, dsq = _silu2(bq)
    sg = _sigmoid(bf_)
    f = lb + (1.0 - lb) * sg
    logf = jnp.log(jnp.maximum(f, MIN_F))
    rin = lax.broadcasted_iota(jnp.int32, (tB, HK), 0) % L
    g = _chunk_cumsum(logf, rin)
    return sq * QK_SCALE, sg, f, 1.0 - f, bi, g, rin, dsq * QK_SCALE


def _hgrn_intra(q, kk, v, g, do=None):
    n = q.shape[0]
    nsub = L // SUB
    bwd = do is not None
    o_rows = [None] * nsub
    if bwd:
        dq_rows = [None] * nsub
        dkk_acc = jnp.zeros_like(kk)
        dv_acc = jnp.zeros_like(v)
    for i in range(1, nsub):
        lo, hi, w = i * SUB, (i + 1) * SUB, i * SUB
        ref = g[:, lo - 1:lo, :]
        eq = jnp.exp(g[:, lo:hi, :] - ref)
        ek = jnp.exp(ref - g[:, :w, :])
        qs = _b(q[:, lo:hi, :] * eq)
        ks = _b(kk[:, :w, :] * ek)
        p = _bdot_nt(qs, ks)
        o_rows[i] = _bdot(_b(p), _b(v[:, :w, :]))
        if bwd:
            dob = _b(do[:, lo:hi, :])
            dp = _b(_bdot_nt(dob, _b(v[:, :w, :])))
            dq_rows[i] = _bdot(dp, ks) * eq
            pad = jnp.zeros((n, L - w, HK), f32)
            dkk_acc = dkk_acc + jnp.concatenate([_bdot_tn(dp, qs) * ek, pad], axis=1)
            dv_acc = dv_acc + jnp.concatenate([_bdot_tn(_b(p), dob), pad], axis=1)
    m = n * nsub
    q4, k4, v4, g4 = (a.reshape(m, SUB, HK) for a in (q, kk, v, g))
    r = lax.broadcasted_iota(jnp.int32, (m, SUB, HK), 1)
    od = jnp.zeros((m, SUB, HK), f32)
    if bwd:
        do4 = do.reshape(m, SUB, HK)
        dqd = jnp.zeros((m, SUB, HK), f32)
        dkd = jnp.zeros((m, SUB, HK), f32)
        dvd = jnp.zeros((m, SUB, HK), f32)
    for j in range(SUB):
        gj, kj, vj = g4[:, j:j + 1, :], k4[:, j:j + 1, :], v4[:, j:j + 1, :]
        ok = r >= j
        e = jnp.where(ok, jnp.exp(g4 - gj), 0.0)
        xq = q4 * e
        pj = jnp.sum(xq * kj, axis=-1, keepdims=True)
        od = od + pj * vj
        if bwd:
            dpj = jnp.sum(do4 * vj, axis=-1, keepdims=True)
            dqd = dqd + dpj * kj * e
            dkd = dkd + jnp.where(r == j, jnp.sum(dpj * xq, axis=1, keepdims=True), 0.0)
            dvd = dvd + jnp.where(r == j, jnp.sum(pj * do4, axis=1, keepdims=True), 0.0)
    od = od.reshape(n, L, HK)
    o = jnp.concatenate([od[:, :SUB, :]] + [od[:, i * SUB:(i + 1) * SUB, :] + o_rows[i] for i in range(1, nsub)], axis=1)
    if not bwd:
        return o
    dqd = dqd.reshape(n, L, HK)
    dq = jnp.concatenate([dqd[:, :SUB, :]] + [dqd[:, i * SUB:(i + 1) * SUB, :] + dq_rows[i] for i in range(1, nsub)], axis=1)
    return o, dq, dkk_acc + dkd.reshape(n, L, HK), dv_acc + dvd.reshape(n, L, HK)


def hgrn_fwd(p, lbs):
    T = p.shape[0]
    tB = min(1024, T)
    nc = tB // L
    N = T // L

    def body(b_ref, lb_ref, o_ref, st_ref, s_scr):
        @pl.when(pl.program_id(1) == 0)
        def _():
            s_scr[...] = jnp.zeros_like(s_scr)
        q, sg, f, kk, v, g, rin, _ = _hgrn_prep(b_ref[:, 0:HK], b_ref[:, HK:2 * HK], b_ref[:, 2 * HK:3 * HK], lb_ref[...])
        q3, k3, v3, g3 = (a.reshape(nc, L, HK) for a in (q, kk, v, g))
        o = _hgrn_intra(q3, k3, v3, g3)
        gl = g3[:, L - 1:L, :]
        qt = _b(q3 * jnp.exp(g3))
        kt = _b(k3 * jnp.exp(gl - g3))
        vb = _b(v3)
        st = s_scr[...]
        for c in range(nc):
            st_ref[c] = st
            o_ref[c * L:(c + 1) * L, :] = o[c] + _dot_nt(qt[c], _b(st))
            st = st * jnp.exp(gl[c]) + _dot_tn(vb[c], kt[c])
        s_scr[...] = st

    return pl.pallas_call(
        body, name="hgrn_fwd", grid=(N_HGRN, T // tB),
        in_specs=[pl.BlockSpec((tB, REG_BH), lambda h, n: (n, OFF_B // REG_BH + h)),
                  pl.BlockSpec((1, HK), lambda h, n: (0, h))],
        out_specs=[pl.BlockSpec((tB, HK), lambda h, n: (n, h)),
                   pl.BlockSpec((None, nc, HK, HK), lambda h, n: (h, n, 0, 0))],
        out_shape=[jax.ShapeDtypeStruct((T, N_HGRN * HK), f32), jax.ShapeDtypeStruct((N_HGRN, N, HK, HK), f32)],
        scratch_shapes=[pltpu.VMEM((HK, HK), f32)],
        compiler_params=_cp(("parallel", "arbitrary")),
    )(p, lbs)


def hgrn_bwd(p, lbs, st, do, dp):
    T = p.shape[0]
    tB = min(256, T)
    nc = tB // L
    nb = T // tB

    def body(b_ref, lb_ref, st_ref, do_ref, dp_in, dp_ref, dlb_ref, ds_scr):
        @pl.when(pl.program_id(1) == 0)
        def _():
            ds_scr[...] = jnp.zeros_like(ds_scr)
            dlb_ref[...] = jnp.zeros_like(dlb_ref)
        lb = lb_ref[...]
        bq = b_ref[:, 0:HK]
        q, sg, f, kk, v, g, rin, dq_dbq = _hgrn_prep(bq, b_ref[:, HK:2 * HK], b_ref[:, 2 * HK:3 * HK], lb)
        q3, k3, v3, g3 = (a.reshape(nc, L, HK) for a in (q, kk, v, g))
        do3 = do_ref[...].reshape(nc, L, HK)
        dob = _b(do3)
        gl = g3[:, L - 1:L, :]
        egl = jnp.exp(gl)
        eg, egr = jnp.exp(g3), jnp.exp(gl - g3)
        qt, kt = q3 * eg, k3 * egr
        s = st_ref[...]
        ds0 = _bdot_tn(dob, _b(qt))
        dsn = ds_scr[...]
        dsns = [None] * nc
        for c in reversed(range(nc)):
            dsns[c] = dsn
            dsn = ds0[c] + dsn * egl[c]
        ds_scr[...] = dsn
        dsp = jnp.stack(dsns, axis=0)
        dspb = _b(dsp)
        dqt = _bdot(dob, _b(s))
        dkt = _bdot(_b(v3), dspb)
        dv_state = _bdot_nt(_b(kt), dspb)
        dgl = egl * jnp.sum(s * dsp, axis=1, keepdims=True) + jnp.sum(dkt * kt, axis=1, keepdims=True)
        _, dq_i, dkk_i, dv_i = _hgrn_intra(q3, k3, v3, g3, do=do3)
        dq = dq_i + dqt * eg
        dkk = dkk_i + dkt * egr
        dv = dv_i + dv_state
        rowi = lax.broadcasted_iota(jnp.int32, (nc, L, HK), 1)
        dg = q3 * dq - k3 * dkk + jnp.where(rowi == L - 1, dgl, 0.0)
        dlogf = _chunk_rev_cumsum(dg.reshape(tB, HK), rin)
        dkk2 = dkk.reshape(tB, HK)
        df = jnp.where(f > MIN_F, dlogf / f, 0.0) - dkk2
        dlb_ref[...] += jnp.sum(df * (1.0 - sg), axis=0, keepdims=True)
        dp_ref[:, 0:HK] = _b(dq.reshape(tB, HK) * dq_dbq)
        dp_ref[:, HK:2 * HK] = _b(df * (1.0 - lb) * sg * (1.0 - sg))
        dp_ref[:, 2 * HK:3 * HK] = _b(dv.reshape(tB, HK))

    rev = lambda n: nb - 1 - n
    return pl.pallas_call(
        body, name="hgrn_bwd", grid=(N_HGRN, nb),
        in_specs=[pl.BlockSpec((tB, REG_BH), lambda h, n: (rev(n), OFF_B // REG_BH + h)),
                  pl.BlockSpec((1, HK), lambda h, n: (0, h)),
                  pl.BlockSpec((None, nc, HK, HK), lambda h, n: (h, rev(n), 0, 0)),
                  pl.BlockSpec((tB, HK), lambda h, n: (rev(n), h)), pl.BlockSpec(memory_space=pl.ANY)],
        out_specs=[pl.BlockSpec((tB, REG_BH), lambda h, n: (rev(n), OFF_B // REG_BH + h)),
                   pl.BlockSpec((1, HK), lambda h, n: (0, h))],
        out_shape=[jax.ShapeDtypeStruct((T, NP), bf16), jax.ShapeDtypeStruct((1, N_HGRN * HK), f32)],
        input_output_aliases={4: 0},
        scratch_shapes=[pltpu.VMEM((HK, HK), f32)],
        compiler_params=_cp(("parallel", "arbitrary")),
    )(p, lbs, st, do, dp)


def _headnorm_fwd(o, z, w, nheads):
    outs, parts = [], []
    for hh in range(nheads):
        sl = slice(hh * HK, (hh + 1) * HK)
        oh = o[:, sl]
        r = lax.rsqrt(jnp.mean(oh * oh, axis=-1, keepdims=True) + NORM_EPS)
        on = oh * r
        sz, dsz = _silu2(z[:, sl])
        outs.append(on * w * sz)
        parts.append((r, on, sz, dsz))
    return jnp.concatenate(outs, axis=1), parts


def _headnorm_bwd(parts, w, dy):
    dos, dzs = [], []
    dw = jnp.zeros((1, HK), f32)
    for hh, (r, on, sz, dsz) in enumerate(parts):
        dyh = dy[:, hh * HK:(hh + 1) * HK]
        dn = dyh * sz * w
        dos.append(r * (dn - on * jnp.mean(dn * on, axis=-1, keepdims=True)))
        dzs.append(dyh * on * w * dsz)
        dw = dw + jnp.sum(dyh * sz * on, axis=0, keepdims=True)
    return jnp.concatenate(dos, axis=1), jnp.concatenate(dzs, axis=1), dw


def _merge_specs(tT, l):
    row = lambda w, cb=0: pl.BlockSpec((tT, w), lambda i, cb=cb: (i, cb))
    full = lambda r, c: pl.BlockSpec((r, c), lambda i: (0, 0))
    layer = lambda r, c: pl.BlockSpec((None, r, c), lambda i: (l, 0, 0))
    return row, full, layer


def merge_fwd(x, p, ya, ob, oc, hw, gw, bg, woa, wob, woc, wo, l):
    T = x.shape[0]
    tT = min(256, T)
    row, full, layer = _merge_specs(tT, l)

    def body(x_ref, pm_ref, ya_ref, ob_ref, oc_ref, hw_ref, gw_ref, bg_ref,
             woa_ref, wob_ref, woc_ref, wo_ref, out_ref):
        yb = _b(_headnorm_fwd(ob_ref[...], pm_ref[:, M_BZ:M_G], hw_ref[...], N_HGRN)[0])
        yc = _b(_headnorm_fwd(oc_ref[...], pm_ref[:, M_CZ:REG_M], gw_ref[...], N_GDN)[0])
        gates = _sigmoid(pm_ref[:, M_G:M_CZ] + bg_ref[...])
        merged = (gates[:, 0:D] * _dot(ya_ref[...], woa_ref[...]) + gates[:, D:2 * D] * _dot(yb, wob_ref[...])
                  + gates[:, 2 * D:3 * D] * _dot(yc, woc_ref[...]))
        out_ref[...] = x_ref[...] + _dot(_b(merged), wo_ref[...])

    return pl.pallas_call(
        body, name="merge_fwd", grid=(T // tT,),
        in_specs=[row(D), row(REG_M, OFF_M // REG_M),
                  row(512), row(512), row(1024), full(1, HK), full(1, HK), full(1, 3 * D),
                  layer(512, D), layer(512, D), layer(D, D), layer(D, D)],
        out_specs=row(D),
        out_shape=jax.ShapeDtypeStruct((T, D), f32),
        compiler_params=_cp(("parallel",)),
    )(x, p, ya, ob, oc, hw, gw, bg, woa, wob, woc, wo)


def merge_bwd(dxo, p, ya, ob, oc, hw, gw, bg, woa, wob, woc, wo, l):
    T = dxo.shape[0]
    tT = min(256, T)
    row, full, layer = _merge_specs(tT, l)

    def body(dx_ref, pm_ref, ya_ref, ob_ref, oc_ref, hw_ref, gw_ref, bg_ref,
             woa_ref, wob_ref, woc_ref, wo_ref,
             dya_ref, dob_ref, doc_ref, dp_ref, mg_ref, dy3_ref, yb_ref, yc_ref,
             dbg_ref, dhw_ref, dgw_ref):
        @pl.when(pl.program_id(0) == 0)
        def _():
            dbg_ref[...] = jnp.zeros_like(dbg_ref)
            dhw_ref[...] = jnp.zeros_like(dhw_ref)
            dgw_ref[...] = jnp.zeros_like(dgw_ref)
        ob, oc, bz, cz = ob_ref[...], oc_ref[...], pm_ref[:, M_BZ:M_G], pm_ref[:, M_CZ:REG_M]
        hw_, gw_ = hw_ref[...], gw_ref[...]
        yb, parts_b = _headnorm_fwd(ob, bz, hw_, N_HGRN)
        yc, parts_c = _headnorm_fwd(oc, cz, gw_, N_GDN)
        yb, yc = _b(yb), _b(yc)
        yb_ref[...] = yb
        yc_ref[...] = yc
        gates = _sigmoid(pm_ref[:, M_G:M_CZ] + bg_ref[...])
        ys = (_dot(ya_ref[...], woa_ref[...]), _dot(yb, wob_ref[...]), _dot(yc, woc_ref[...]))
        dmerged = _dot_nt(_b(dx_ref[...]), wo_ref[...])
        merged = jnp.zeros_like(dmerged)
        dys = []
        for i in range(3):
            gi = gates[:, i * D:(i + 1) * D]
            merged = merged + gi * ys[i]
            dyi = _b(dmerged * gi)
            dys.append(dyi)
            dy3_ref[:, i * D:(i + 1) * D] = dyi
            dgp = dmerged * ys[i] * gi * (1.0 - gi)
            dp_ref[:, M_G + i * D:M_G + (i + 1) * D] = _b(dgp)
            dbg_ref[:, i * D:(i + 1) * D] += jnp.sum(dgp, axis=0, keepdims=True)
        mg_ref[...] = _b(merged)
        dya_ref[...] = _dot_nt(dys[0], woa_ref[...])
        dob, dbz, dhw = _headnorm_bwd(parts_b, hw_, _dot_nt(dys[1], wob_ref[...]))
        doc, dcz, dgw = _headnorm_bwd(parts_c, gw_, _dot_nt(dys[2], woc_ref[...]))
        dob_ref[...] = dob
        doc_ref[...] = doc
        dp_ref[:, M_BZ:M_G] = _b(dbz)
        dp_ref[:, M_CZ:REG_M] = _b(dcz)
        dhw_ref[...] += dhw
        dgw_ref[...] += dgw

    sd = jax.ShapeDtypeStruct
    return pl.pallas_call(
        body, name="merge_bwd", grid=(T // tT,),
        in_specs=[row(D), row(REG_M, OFF_M // REG_M),
                  row(512), row(512), row(1024), full(1, HK), full(1, HK), full(1, 3 * D),
                  layer(512, D), layer(512, D), layer(D, D), layer(D, D)],
        out_specs=[row(512), row(512), row(1024), row(REG_M, OFF_M // REG_M), row(D), row(3 * D), row(512),
                   row(1024), full(1, 3 * D), full(1, HK), full(1, HK)],
        out_shape=[sd((T, 512), f32), sd((T, 512), f32), sd((T, 1024), f32), sd((T, NP), bf16),
                   sd((T, D), bf16), sd((T, 3 * D), bf16), sd((T, 512), bf16),
                   sd((T, 1024), bf16), sd((1, 3 * D), f32), sd((1, HK), f32), sd((1, HK), f32)],
        compiler_params=_cp(("arbitrary",)),
    )(dxo, p, ya, ob, oc, hw, gw, bg, woa, wob, woc, wo)


def layer_fwd(x, w):
    l = w["l"]
    p, h = inproj_fwd(x, w["norm_w"], w["w_in"])
    ya = mixa_fwd(p, w["conv_a"])
    qn, kn, vv, sm = mixc_pre_fwd(p, w["conv_c"], w["alog_l"], w["dtb_l"])
    oc, st_c, ti = gdn_fwd(qn, kn, vv, sm)
    ob, st_b = hgrn_fwd(p, w["lbs"])
    if "late" in w:
        w.update(w.pop("late")((ya, oc, ob)))
    xo = merge_fwd(x, p, ya, ob, oc, w["hgrn_norm_w"], w["gdn_norm_w"], w["b_gate"],
                   w["w_out_a"], w["w_out_b"], w["w_out_c"], w["w_o"], l)
    saved = dict(x=x, p=p, h=h, ya=ya, qn=qn, kn=kn, vv=vv, sm=sm, oc=oc, st_c=st_c, ti=ti, ob=ob, st_b=st_b)
    return xo, saved


OUT_MATS = (("w_out_a", "cols"), ("w_out_b", "cols"), ("w_out_c", "rows"), ("w_o", "rows"))


def layer_bwd(dxo, w, s, chip):
    p, l = s["p"], w["l"]
    (dya, dob, doc, dp, merged, dy3, yb, yc, dbg, dhw, dgw) = merge_bwd(
        dxo, p, s["ya"], s["ob"], s["oc"], w["hgrn_norm_w"], w["gdn_norm_w"], w["b_gate"],
        w["w_out_a"], w["w_out_b"], w["w_out_c"], w["w_o"], l)
    full = {"w_o": matmul_tn(merged, dxo, "dw_o", with_bf16=True),
            "w_out_a": matmul_tn(s["ya"], dy3, "dw_out_a", n=D, b_col0=0, with_bf16=True),
            "w_out_b": matmul_tn(yb, dy3, "dw_out_b", n=D, b_col0=D, with_bf16=True),
            "w_out_c": matmul_tn(yc, dy3, "dw_out_c", n=D, b_col0=2 * D, with_bf16=True)}
    out_kinds = [k for _, k in OUT_MATS]
    sent_out, token = exchange_start([full[n][1] for n, _ in OUT_MATS], out_kinds, f"grads_out_start{l}")
    dp, dlbs = hgrn_bwd(p, w["lbs"] + token[0:1, 0:1], s["st_b"], dob, dp)
    dq8, dk8, dvv, dsm8 = gdn_bwd(s["qn"], s["kn"], s["vv"], s["sm"], s["st_c"], s["ti"], doc)
    dp, dcc, dsmall = mixc_pre_bwd(p, w["conv_c"], w["alog_l"], w["dtb_l"], dq8, dk8, dvv, dsm8, dp)
    dp, dca = mixa_bwd(p, w["conv_a"], dya, dp)
    gf_win, gb_win = win_from_padded(matmul_tn(s["h"], dp, "dw_in"))
    sent_in, token = exchange_start([gb_win], ["slot"], f"grads_in_start{l}")
    dx, dnw = inproj_bwd(dp, w["w_in"], s["x"], w["norm_w"] + token[0:1, 0:1], dxo)
    recv_out = exchange_wait(sent_out, out_kinds, dx, f"grads_out_wait{l}")
    recv_in = exchange_wait(sent_in, ["slot"], dx, f"grads_in_wait{l}")
    half = {"w_in": partial_sum(gf_win, "slot", recv_in[0], chip, "psum_w_in", transposed=True)}
    for (n, kind), r in zip(OUT_MATS, recv_out):
        half[n] = partial_sum(full[n][0], kind, r, chip, "psum_" + n)
    small = dict(norm_w=dnw, b_gate=dbg, hgrn_norm_w=dhw, gdn_norm_w=dgw, lbs=dlbs, conv_a=dca[0:3], conv_c=dcc[0:4],
                 a_log=dsmall[0:1, 8:16], dt_bias=dsmall[1:2, 8:16])
    return dx, small, half


def lbs_fwd(lb):
    def body(lb_ref, o_ref):
        l0, l1 = lb_ref[0:1, :], lb_ref[1:2, :]
        mx = jnp.maximum(l0, l1)
        e0, e1 = jnp.exp(l0 - mx), jnp.exp(l1 - mx)
        o_ref[0:1, :] = jnp.zeros_like(l0)
        o_ref[1:2, :] = e1 / (e0 + e1)
    return pl.pallas_call(body, name="lbs_fwd", out_shape=jax.ShapeDtypeStruct(lb.shape, f32))(lb)


def _adam_math(w, g, m, v):
    mn = ADAM_B1 * m + (1.0 - ADAM_B1) * g
    vn = ADAM_B2 * v + (1.0 - ADAM_B2) * (g * g)
    mh = mn / (1.0 - ADAM_B1 ** ADAM_STEP)
    vh = vn / (1.0 - ADAM_B2 ** ADAM_STEP)
    return -ADAM_LR * (mh / (jnp.sqrt(vh) + ADAM_EPS) + ADAM_WD * w), mn, vn


def adam(w, g, m, v, name):
    R, C = w.shape
    tr = 256 if R % 256 == 0 else R

    def body(w_ref, g_ref, m_ref, v_ref, d_ref, mo_ref, vo_ref):
        d, mn, vn = _adam_math(w_ref[...], g_ref[...], m_ref[...], v_ref[...])
        d_ref[...] = d
        mo_ref[...] = mn
        vo_ref[...] = vn

    spec = pl.BlockSpec((tr, C), lambda i: (i, 0))
    return pl.pallas_call(
        body, name=name, grid=(R // tr,), in_specs=[spec] * 4, out_specs=[spec] * 3,
        out_shape=[jax.ShapeDtypeStruct((R, C), f32)] * 3, compiler_params=_cp(("parallel",)),
    )(w, g, m, v)


def adam_pair(h, hs, w, m, v, name):
    _, R, C = w.shape
    cp = h[0].shape[1]
    tr = 128 if R % 128 == 0 else R
    nt = R // tr

    def body(h0_ref, h1_ref, s0_ref, s1_ref, w_ref, m_ref, v_ref, g_ref, d_ref, mo_ref, vo_ref):
        def update(h_ref, s_ref):
            g = (h_ref[...] + s_ref[...])[:, :C]
            d, mn, vn = _adam_math(w_ref[...], g, m_ref[...], v_ref[...])
            g_ref[...] = g
            d_ref[...] = d
            mo_ref[...] = mn
            vo_ref[...] = vn

        @pl.when(pl.program_id(0) == 0)
        def _():
            update(h0_ref, s0_ref)

        @pl.when(pl.program_id(0) == 1)
        def _():
            update(h1_ref, s1_ref)

    h0spec = pl.BlockSpec((tr, cp), lambda l, i: (jnp.where(l == 0, i, nt - 1), 0))
    h1spec = pl.BlockSpec((tr, cp), lambda l, i: (jnp.where(l == 1, i, 0), 0))
    spec = pl.BlockSpec((None, tr, C), lambda l, i: (l, i, 0))
    return pl.pallas_call(
        body, name=name, grid=(2, nt), in_specs=[h0spec, h1spec, h0spec, h1spec, spec, spec, spec],
        out_specs=[spec] * 4, out_shape=[jax.ShapeDtypeStruct(w.shape, f32)] * 4,
        compiler_params=_cp(("arbitrary", "arbitrary")),
    )(h[0], h[1], hs[0], hs[1], w, m, v)


_SMALL = (("norm_w", 2 * D), ("b_gate", 6 * D), ("lower_bounds", None), ("hgrn_norm_w", 2 * HK),
          ("gdn_norm_w", 2 * HK), ("a_log", 16), ("dt_bias", 16), ("final_norm_w", D), ("loss", None))
_CONV = (("conv_a", 2 * 3 * 512), ("conv_c", 2 * 4 * 2048))


def _small_rows(n):
    return 16 if n is None else -(-n // 1024) * 8


LB_ROW = sum(_small_rows(n) for _, n in _SMALL[:2])
ADAM_ROWS = sum(_small_rows(n) for _, n in _SMALL)
SMALL_ROWS = ADAM_ROWS + sum(_small_rows(n) for _, n in _CONV)


def small_update(parts, wp, mp, vp):
    def body(p_ref, w_ref, m_ref, v_ref, g_ref, d_ref, mo_ref, vo_ref):
        gs = p_ref[0]
        for i in range(1, 8):
            gs = gs + p_ref[i]
        w = w_ref[...]
        l0, l1 = w[LB_ROW:LB_ROW + 8], w[LB_ROW + 8:LB_ROW + 16]
        mx = jnp.maximum(l0, l1)
        e0, e1 = jnp.exp(l0 - mx), jnp.exp(l1 - mx)
        p0, p1 = e0 / (e0 + e1), e1 / (e0 + e1)
        dl1 = gs[LB_ROW + 8:LB_ROW + 16]
        s = p1 * dl1
        g = jnp.concatenate([gs[0:LB_ROW], -p0 * s, p1 * dl1 - p1 * s, gs[LB_ROW + 16:ADAM_ROWS]], axis=0)
        d, mn, vn = _adam_math(w, g, m_ref[...], v_ref[...])
        g_ref[0:ADAM_ROWS, :] = g
        g_ref[ADAM_ROWS:, :] = gs[ADAM_ROWS:]
        d_ref[...] = d
        mo_ref[...] = mn
        vo_ref[...] = vn
    sd = jax.ShapeDtypeStruct
    return pl.pallas_call(body, name="small_update",
                          out_shape=[sd((SMALL_ROWS, 128), f32)] + [sd((ADAM_ROWS, 128), f32)] * 3)(parts, wp, mp, vp)


def partial_sum(own, kind, recv, chip, name, transposed=False):
    _, r, c = recv.shape
    tr = 256 if r % 256 == 0 else r

    def body(chip_ref, o_ref, r_ref, out_ref):
        s = ((o_ref[...] + r_ref[0].astype(f32)) + r_ref[1].astype(f32)) + r_ref[2].astype(f32)
        out_ref[...] = s.T if transposed else s

    own_spec = {"slot": pl.BlockSpec((None, tr, c), lambda i, chip: (chip[0], i, 0)),
                "cols": pl.BlockSpec((tr, c), lambda i, chip: (i, chip[0])),
                "rows": pl.BlockSpec((tr, c), lambda i, chip: (chip[0] * (r // tr) + i, 0))}[kind]
    out_spec = pl.BlockSpec((c, tr), lambda i, chip: (0, i)) if transposed else pl.BlockSpec((tr, c), lambda i, chip: (i, 0))
    return pl.pallas_call(
        body, name=name,
        grid_spec=pltpu.PrefetchScalarGridSpec(
            num_scalar_prefetch=1, grid=(r // tr,),
            in_specs=[own_spec, pl.BlockSpec((3, tr, c), lambda i, chip: (0, i, 0))], out_specs=out_spec),
        out_shape=jax.ShapeDtypeStruct((c, r) if transposed else (r, c), f32), compiler_params=_cp(("arbitrary",)),
    )(chip, own, recv)


def adam_pair_t(h, hs, wt, mt, vt, name):
    C, _, R = wt.shape
    tc = 128

    def body(h0_ref, h1_ref, s0_ref, s1_ref, w_ref, m_ref, v_ref, g_ref, d_ref, mo_ref, vo_ref):
        g = jnp.stack([h0_ref[...] + s0_ref[...], h1_ref[...] + s1_ref[...]], axis=1)
        d, mn, vn = _adam_math(w_ref[...], g, m_ref[...], v_ref[...])
        g_ref[...] = g
        d_ref[...] = d
        mo_ref[...] = mn
        vo_ref[...] = vn

    hspec = pl.BlockSpec((tc, R), lambda i: (i, 0))
    spec = pl.BlockSpec((tc, 2, R), lambda i: (i, 0, 0))
    return pl.pallas_call(
        body, name=name, grid=(pl.cdiv(C, tc),), in_specs=[hspec] * 4 + [spec] * 3, out_specs=[spec] * 4,
        out_shape=[jax.ShapeDtypeStruct(wt.shape, f32)] * 4, compiler_params=_cp(("parallel",)),
    )(h[0], h[1], hs[0], hs[1], wt, mt, vt)


MESH = pl.DeviceIdType.MESH
_HBM = pl.BlockSpec(memory_space=pltpu.HBM)


def _place():
    return lax.axis_index("x"), lax.axis_index("y"), lax.axis_index("c")


def weight_gather(arrs):
    n = len(arrs)

    def body(*refs):
        x_refs, out_refs = refs[:n], refs[n:2 * n]
        send_sems, recv_sems, local_sems = refs[2 * n:]
        x, y, c = _place()
        me, sibling = (x, y, c), (x, y, 1 - c)
        chips = [(1 - x, y), (x, 1 - y), (1 - x, 1 - y)]

        def copy(a, k, block, to, own_src=False):
            px, py, pc = block
            dst = out_refs[a].at[2 * px + py, pc]
            return pltpu.make_async_remote_copy(
                src_ref=x_refs[a].at[c] if own_src else dst, dst_ref=dst,
                send_sem=send_sems.at[7 * a + k], recv_sem=recv_sems.at[7 * a + k], device_id=to, device_id_type=MESH)

        mine = [pltpu.make_async_copy(x_refs[a].at[c], out_refs[a].at[2 * x + y, c], local_sems.at[a])
                for a in range(n)]
        for cp in mine:
            cp.start()
        first = []
        for a in range(n):
            first.append(copy(a, 0, me, sibling, own_src=True))
            first += [copy(a, 1 + j, me, (*chip, c), own_src=True) for j, chip in enumerate(chips)]
        for cp in first:
            cp.start()
        passed = []
        for j, chip in enumerate(chips):
            for a in range(n):
                copy(a, 1 + j, (*chip, c), me).wait_recv()
                fwd = copy(a, 4 + j, (*chip, c), sibling)
                fwd.start()
                passed.append(fwd)
        for a in range(n):
            copy(a, 0, sibling, me).wait_recv()
            for j, chip in enumerate(chips):
                copy(a, 4 + j, (*chip, 1 - c), me).wait_recv()
        for cp in first + passed:
            cp.wait_send()
        for cp in mine:
            cp.wait()

    return pl.pallas_call(
        body, name="weight_gather", in_specs=[_HBM] * n, out_specs=[_HBM] * n,
        out_shape=[jax.ShapeDtypeStruct((N_CHIPS,) + a.shape, a.dtype) for a in arrs],
        scratch_shapes=[pltpu.SemaphoreType.DMA((7 * n,)), pltpu.SemaphoreType.DMA((7 * n,)),
                        pltpu.SemaphoreType.DMA((n,))],
    )(*arrs)


SHARD_W = 256


_SEM = pl.BlockSpec(memory_space=pltpu.SEMAPHORE)
_EFFECT = pltpu.SideEffectType.DATAFLOW_SIDE_EFFECTING


def _landing_shape(a, kind):
    if kind == "all":
        return (N_CHIPS,) + a.shape
    if kind == "slot":
        return (3,) + a.shape[1:]
    return (3,) + ((a.shape[0], SHARD_W) if kind == "cols" else (SHARD_W, a.shape[1]))


def _shard_copies(src_refs, land_refs, kinds, send_sems, recv_sems):
    x, y, c = _place()
    copies = []
    for a, (src, land, kind) in enumerate(zip(src_refs, land_refs, kinds)):
        for j, (px, py) in enumerate(((1 - x, y), (x, 1 - y), (1 - x, 1 - y))):
            q = 2 * px + py
            lo = pl.multiple_of(q * SHARD_W, SHARD_W)
            part = {"slot": lambda: src.at[q], "cols": lambda: src.at[:, pl.ds(lo, SHARD_W)],
                    "rows": lambda: src.at[pl.ds(lo, SHARD_W), :], "all": lambda: src}[kind]()
            k = 3 * a + j
            copies.append(pltpu.make_async_remote_copy(
                src_ref=part, dst_ref=land.at[2 * x + y] if kind == "all" else land.at[j],
                send_sem=send_sems.at[k], recv_sem=recv_sems.at[k], device_id=(px, py, c), device_id_type=MESH))
    return copies


def exchange_start(srcs, kinds, name, after=None):
    n = len(srcs)
    lands = [lax.empty(_landing_shape(a, k), a.dtype) for a, k in zip(srcs, kinds)]
    extra = [] if after is None else [after]

    def body(*refs):
        src_refs, land_refs, token = refs[:n], refs[n:2 * n], refs[-1]
        send_sems, recv_sems = refs[2 * n + len(extra)], refs[2 * n + len(extra) + 1]
        for cp in _shard_copies(src_refs, land_refs, kinds, send_sems, recv_sems):
            cp.start()
        token[...] = jnp.zeros_like(token)

    both = list(srcs) + lands
    out = pl.pallas_call(
        body, name=name,
        out_shape=(pltpu.SemaphoreType.DMA((3 * n,)), pltpu.SemaphoreType.DMA((3 * n,)),
                   *[pltpu.HBM(a.shape, a.dtype) for a in both], jax.ShapeDtypeStruct((8, 128), f32)),
        in_specs=[_HBM] * (2 * n) + [pl.BlockSpec(memory_space=pl.ANY)] * len(extra),
        out_specs=(_SEM, _SEM, *[_HBM] * (2 * n), pl.BlockSpec(memory_space=pltpu.VMEM)),
        input_output_aliases={i: 2 + i for i in range(2 * n)},
        compiler_params=pltpu.CompilerParams(has_side_effects=_EFFECT),
    )(*[pltpu.with_memory_space_constraint(a, pltpu.HBM) for a in both], *extra)
    return (out[0], out[1], out[2:2 + 2 * n]), out[-1]


def exchange_wait(handle, kinds, after, name):
    send_sems, recv_sems, both = handle
    n = len(kinds)
    after = after if isinstance(after, tuple) else (after,)

    def body(*refs):
        src_refs, land_refs, s_sems, r_sems = refs[:n], refs[n:2 * n], refs[2 * n], refs[2 * n + 1]
        for cp in _shard_copies(src_refs, land_refs, kinds, s_sems, r_sems):
            cp.wait_send()
            cp.wait_recv()

    out = pl.pallas_call(
        body, name=name, out_shape=tuple(pltpu.HBM(a.shape, a.dtype) for a in both),
        in_specs=[_HBM] * (2 * n) + [_SEM, _SEM] + [pl.BlockSpec(memory_space=pl.ANY)] * len(after),
        out_specs=tuple([_HBM] * (2 * n)), input_output_aliases={i: i for i in range(2 * n)},
        compiler_params=pltpu.CompilerParams(has_side_effects=_EFFECT),
    )(*both, send_sems, recv_sems, *after)
    return out[n:]


def final_exchange(hs, small):
    n = len(hs)
    S = small.shape[0]

    def body(*refs):
        h_refs, sm_ref, out_refs, smalls_ref = refs[:n], refs[n], refs[n + 1:2 * n + 1], refs[2 * n + 1]
        send_sems, recv_sems, local_sem = refs[2 * n + 2:]
        x, y, c = _place()
        my_slot = smalls_ref.at[4 * x + 2 * y + c]
        mine = pltpu.make_async_copy(sm_ref, my_slot, local_sem)
        mine.start()
        copies = [pltpu.make_async_remote_copy(src_ref=h_refs[a], dst_ref=out_refs[a], send_sem=send_sems.at[a],
                                               recv_sem=recv_sems.at[a], device_id=(x, y, 1 - c), device_id_type=MESH)
                  for a in range(n)]
        for mask in range(1, 8):
            fx, fy, fc = (mask >> 2) & 1, (mask >> 1) & 1, mask & 1
            peer = ((1 - x) if fx else x, (1 - y) if fy else y, (1 - c) if fc else c)
            copies.append(pltpu.make_async_remote_copy(
                src_ref=sm_ref, dst_ref=my_slot, send_sem=send_sems.at[n - 1 + mask], recv_sem=recv_sems.at[n - 1 + mask],
                device_id=peer, device_id_type=MESH))
        for cp in copies:
            cp.start()
        for cp in copies:
            cp.wait_recv()
        for cp in copies:
            cp.wait_send()
        mine.wait()

    sd = jax.ShapeDtypeStruct
    out = pl.pallas_call(
        body, name="final_exchange", in_specs=[_HBM] * (n + 1), out_specs=[_HBM] * (n + 1),
        out_shape=[sd(h.shape, h.dtype) for h in hs] + [sd((8, S, 128), f32)],
        scratch_shapes=[pltpu.SemaphoreType.DMA((n + 7,)), pltpu.SemaphoreType.DMA((n + 7,)),
                        pltpu.SemaphoreType.DMA],
    )(*hs, small)
    return out[:n], out[n]


N_CHIPS = 4
SHARD_COLS = N_ORIG // N_CHIPS


SHARD_PAD = 2688
_COL_SEGMENTS = (
    ((0, 2048, OFF_A),)
    + tuple((2048 + 512 * j + HK * h, 2048 + 512 * j + HK * (h + 1), OFF_B + REG_BH * h + HK * j)
            for j in range(3) for h in range(N_HGRN))
    + ((3584, 4096, OFF_M + M_BZ), (4096, 6144, OFF_C), (6144, 6160, OFF_C + 2048), (6160, 7184, OFF_M + M_CZ),
       (7184, N_ORIG, OFF_M + M_G)))


def _shard_pieces():
    pieces = []
    for lo, hi, dst in _COL_SEGMENTS:
        for p in range(N_CHIPS):
            a, b = max(lo, p * SHARD_COLS), min(hi, (p + 1) * SHARD_COLS)
            if a < b:
                pieces.append((p, a - p * SHARD_COLS, dst + a - lo, b - a))
    return pieces


def win_cast_pad(wt):
    tc = 128

    def body(x_ref, o0_ref, o1_ref):
        col = pl.program_id(0) * tc + lax.broadcasted_iota(jnp.int32, (tc, 1), 0)
        for l, o_ref in enumerate((o0_ref, o1_ref)):
            o_ref[...] = _b(jnp.where(col < SHARD_COLS, x_ref[:, l, :], 0.0).T)

    spec = pl.BlockSpec((D, tc), lambda i: (0, i))
    return pl.pallas_call(
        body, name="win_cast_pad", grid=(SHARD_PAD // tc,),
        in_specs=[pl.BlockSpec((tc, 2, D), lambda i: (i, 0, 0))], out_specs=[spec, spec],
        out_shape=[jax.ShapeDtypeStruct((D, SHARD_PAD), bf16)] * 2, compiler_params=_cp(("parallel",)),
    )(wt)


def win_to_padded(w4, name):
    tr = 256
    pieces = _shard_pieces()

    def body(a_ref, o_ref):
        o_ref[...] = jnp.zeros((tr, NP), bf16)
        for p, j0, c0, n in pieces:
            o_ref[:, c0:c0 + n] = a_ref[p, :, j0:j0 + n]

    return pl.pallas_call(
        body, name=name, grid=(D // tr,),
        in_specs=[pl.BlockSpec((N_CHIPS, tr, SHARD_PAD), lambda i: (0, i, 0))],
        out_specs=pl.BlockSpec((tr, NP), lambda i: (i, 0)),
        out_shape=jax.ShapeDtypeStruct((D, NP), bf16), compiler_params=_cp(("parallel",)),
    )(w4)


def win_from_padded(dw):
    tr = 128
    pieces = _shard_pieces()

    def body(d_ref, of_ref, ob_ref):
        for p in range(N_CHIPS):
            of_ref[p, :, SHARD_COLS:] = jnp.zeros((tr, SHARD_PAD - SHARD_COLS), f32)
            ob_ref[p, :, SHARD_COLS:] = jnp.zeros((tr, SHARD_PAD - SHARD_COLS), bf16)
        for p, j0, c0, n in pieces:
            v = d_ref[:, c0:c0 + n]
            of_ref[p, :, j0:j0 + n] = v
            ob_ref[p, :, j0:j0 + n] = _b(v)

    out_spec = pl.BlockSpec((N_CHIPS, tr, SHARD_PAD), lambda i: (0, i, 0))
    return pl.pallas_call(
        body, name="win_from_padded", grid=(D // tr,),
        in_specs=[pl.BlockSpec((tr, NP), lambda i: (i, 0))], out_specs=[out_spec, out_spec],
        out_shape=[jax.ShapeDtypeStruct((N_CHIPS, D, SHARD_PAD), f32),
                   jax.ShapeDtypeStruct((N_CHIPS, D, SHARD_PAD), bf16)],
        compiler_params=_cp(("parallel",)),
    )(dw)


def _rows128(a):
    flat = a.reshape(-1)
    total = -(-flat.shape[0] // 1024) * 1024
    return jnp.pad(flat, (0, total - flat.shape[0])).reshape(total // 128, 128)


def _lb_rows(lb):
    return jnp.pad(lb.reshape(2, 4, 128), ((0, 0), (0, 4), (0, 0))).reshape(16, 128)


def _pack_small(v, with_conv):
    rows = []
    for name, n in _SMALL + (_CONV if with_conv else ()):
        if name == "lower_bounds":
            rows.append(_lb_rows(v[name]))
        elif name == "loss":
            rows.append(jnp.broadcast_to(v[name], (16, 128)) if name in v else jnp.zeros((16, 128), f32))
        else:
            rows.append(_rows128(v[name]))
    return jnp.concatenate(rows, axis=0)


def _unpack_small(p, shapes, with_conv):
    out, row = {}, 0
    for name, n in _SMALL + (_CONV if with_conv else ()):
        nrows = _small_rows(n)
        blk = p[row:row + nrows]
        if name == "lower_bounds":
            out[name] = blk.reshape(2, 8, 128)[:, :4].reshape(2, 512)
        elif name == "loss":
            out[name] = blk[0, 0]
        else:
            out[name] = blk.reshape(-1)[:n].reshape(shapes[name])
        row += nrows
    return out


def _lane_vec(a8):
    return jnp.pad(a8.reshape(1, 8), ((0, 0), (8, 112)))


WEIGHT_NAMES = ("norm_w", "w_in", "b_gate", "conv_a", "conv_c", "a_log", "dt_bias", "lower_bounds", "hgrn_norm_w",
                "gdn_norm_w", "w_out_a", "w_out_b", "w_out_c", "w_o", "final_norm_w")


def kernel(x, norm_w, w_in, b_gate, conv_a, conv_c, a_log, dt_bias, lower_bounds, hgrn_norm_w, gdn_norm_w, w_out_a, w_out_b, w_out_c, w_o, final_norm_w, loss_target, m_norm_w, m_w_in, m_b_gate, m_conv_a, m_conv_c, m_a_log, m_dt_bias, m_lower_bounds, m_hgrn_norm_w, m_gdn_norm_w, m_w_out_a, m_w_out_b, m_w_out_c, m_w_o, m_final_norm_w, v_norm_w, v_w_in, v_b_gate, v_conv_a, v_conv_c, v_a_log, v_dt_bias, v_lower_bounds, v_hgrn_norm_w, v_gdn_norm_w, v_w_out_a, v_w_out_b, v_w_out_c, v_w_o, v_final_norm_w):
    wts = dict(norm_w=norm_w, w_in=w_in, b_gate=b_gate, conv_a=conv_a, conv_c=conv_c, a_log=a_log, dt_bias=dt_bias,
               lower_bounds=lower_bounds, hgrn_norm_w=hgrn_norm_w, gdn_norm_w=gdn_norm_w, w_out_a=w_out_a,
               w_out_b=w_out_b, w_out_c=w_out_c, w_o=w_o, final_norm_w=final_norm_w)
    mom = dict(norm_w=m_norm_w, w_in=m_w_in, b_gate=m_b_gate, conv_a=m_conv_a, conv_c=m_conv_c, a_log=m_a_log,
               dt_bias=m_dt_bias, lower_bounds=m_lower_bounds, hgrn_norm_w=m_hgrn_norm_w, gdn_norm_w=m_gdn_norm_w,
               w_out_a=m_w_out_a, w_out_b=m_w_out_b, w_out_c=m_w_out_c, w_o=m_w_o, final_norm_w=m_final_norm_w)
    var = dict(norm_w=v_norm_w, w_in=v_w_in, b_gate=v_b_gate, conv_a=v_conv_a, conv_c=v_conv_c, a_log=v_a_log,
               dt_bias=v_dt_bias, lower_bounds=v_lower_bounds, hgrn_norm_w=v_hgrn_norm_w, gdn_norm_w=v_gdn_norm_w,
               w_out_a=v_w_out_a, w_out_b=v_w_out_b, w_out_c=v_w_out_c, w_o=v_w_o, final_norm_w=v_final_norm_w)
    chip = 2 * lax.axis_index("x") + lax.axis_index("y")
    chip1 = chip.reshape(1).astype(jnp.int32)

    win_l0, win_l1 = win_cast_pad(jnp.transpose(w_in, (2, 0, 1)))
    win4_l0, ca4, cc4 = weight_gather([win_l0.reshape(2, D // 2, SHARD_PAD), conv_a, conv_c])
    by_cols = lambda a: a.transpose(1, 2, 0, 3).reshape(a.shape[1], a.shape[2], N_CHIPS * a.shape[3])
    by_rows = lambda a: a.transpose(1, 0, 2, 3).reshape(a.shape[1], N_CHIPS * a.shape[2], a.shape[3])
    conv_a_full, conv_c_full = by_cols(ca4), by_cols(cc4)
    later = [win_l1, _b(w_out_a), _b(w_out_b), _b(w_out_c), _b(w_o)]
    sent_w, token = exchange_start(later, ["all"] * 5, "weights_start", after=win4_l0)

    def late_weights(after):
        lands = exchange_wait(sent_w, ["all"] * 5, after, "weights_wait")
        l1, woa4, wob4, woc4, wo4 = (lax.dynamic_update_index_in_dim(land, own, chip, 0)
                                     for land, own in zip(lands, later))
        outs = dict(w_out_a=by_cols(woa4), w_out_b=by_cols(wob4), w_out_c=by_rows(woc4), w_o=by_rows(wo4))
        layers[1].update(outs, w_in=win_to_padded(l1, "win_to_padded1"))
        return outs

    lbs = lbs_fwd(lower_bounds)
    layers = []
    for l in range(2):
        layers.append(dict(
            l=l, norm_w=norm_w[l:l + 1], b_gate=b_gate[l:l + 1], conv_a=conv_a_full[l], conv_c=conv_c_full[l],
            alog_l=_lane_vec(a_log[l]), dtb_l=_lane_vec(dt_bias[l]), lbs=lbs[l:l + 1],
            hgrn_norm_w=hgrn_norm_w[l:l + 1], gdn_norm_w=gdn_norm_w[l:l + 1]))
    layers[0].update(w_in=win_to_padded(win4_l0.reshape(N_CHIPS, D, SHARD_PAD), "win_to_padded0"), late=late_weights,
                     norm_w=norm_w[0:1] + token[0:1, 0:1])

    xs, saved = x[0], []
    for l in range(2):
        xs, s = layer_fwd(xs, layers[l])
        saved.append(s)
    loss_row, dx, dfw = loss_head(xs, final_norm_w.reshape(1, D), loss_target[0])
    lg, half = [None, None], [None, None]
    for l in (1, 0):
        dx, lg[l], half[l] = layer_bwd(dx, layers[l], saved[l], chip1)
    grad_x = dx[None]

    stack = lambda n: jnp.stack([lg[0][n], lg[1][n]], axis=0)
    gsmall = {n: stack(n) for n in ("norm_w", "b_gate", "hgrn_norm_w", "gdn_norm_w", "a_log", "dt_bias", "conv_a",
                                    "conv_c")}
    gsmall.update(lower_bounds=stack("lbs"), final_norm_w=dfw, loss=loss_row)
    mat_names = ("w_in",) + tuple(n for n, _ in OUT_MATS)
    mine = [half[l][n] for n in mat_names for l in range(2)]
    theirs, smalls = final_exchange(mine, _pack_small(gsmall, True))

    out_g, out_d, out_m, out_v = {}, {}, {}, {}
    for i, n in enumerate(mat_names):
        h, hs = mine[2 * i:2 * i + 2], theirs[2 * i:2 * i + 2]
        if n == "w_in":
            fwd, back = (lambda a: jnp.transpose(a, (2, 0, 1))), (lambda a: jnp.transpose(a, (1, 2, 0)))
            res = adam_pair_t(h, hs, fwd(wts[n]), fwd(mom[n]), fwd(var[n]), "adam_" + n)
            out_g[n], out_d[n], out_m[n], out_v[n] = (back(a) for a in res)
        else:
            out_g[n], out_d[n], out_m[n], out_v[n] = adam_pair(h, hs, wts[n], mom[n], var[n], "adam_" + n)
    small_names = [n for n, _ in _SMALL if n != "loss"]
    pack = lambda v: _pack_small({n: v[n] for n in small_names}, False)
    sg, sd, smn, svn = small_update(smalls, pack(wts), pack(mom), pack(var))
    shapes = {n: wts[n].shape for n in small_names}
    shapes.update(conv_a=(2, 3, 512), conv_c=(2, 4, 2048))
    for dst, src, conv in ((out_g, sg, True), (out_d, sd, False), (out_m, smn, False), (out_v, svn, False)):
        dst.update(_unpack_small(src, shapes, conv))
    loss = out_g.pop("loss")
    for n in ("conv_a", "conv_c"):
        width = wts[n].shape[2]
        g = lax.dynamic_slice_in_dim(out_g[n], chip * width, width, axis=2)
        two_d = lambda a: a.reshape(-1, width)
        d, mn, vn = adam(two_d(wts[n]), two_d(g), two_d(mom[n]), two_d(var[n]), "adam_" + n)
        out_g[n] = g
        out_d[n], out_m[n], out_v[n] = (a.reshape(wts[n].shape) for a in (d, mn, vn))
    return (loss, grad_x, *[out_g[n] for n in WEIGHT_NAMES], *[out_d[n] for n in WEIGHT_NAMES],
            *[out_m[n] for n in WEIGHT_NAMES], *[out_v[n] for n in WEIGHT_NAMES])
```

```python
import jax
import jax.numpy as jnp
from jax import lax
from jax.experimental import pallas as pl
from jax.experimental.pallas import tpu as pltpu

f32 = jnp.float32
bf16 = jnp.bfloat16

D = 1024
L = 64
SUB = 16
NORM_EPS = 1e-6
L2_EPS = 1e-6
MIN_F = 1e-30
HK = 128
QK_SCALE = HK ** -0.5
N_GDN = 8
GDN_BLOCK = 1024
N_HGRN = 4

REG_A = 2304
REG_C = 2304
REG_M = 4608
REG_BH = 384
OFF_A, OFF_C, OFF_M, OFF_B = 0, 2304, 4608, 9216
M_BZ, M_G, M_CZ = 0, 512, 3584
NP = 10752
NP_TILE = 1536
N_ORIG = 10256

ADAM_LR, ADAM_B1, ADAM_B2, ADAM_EPS, ADAM_WD, ADAM_STEP = 0.001, 0.9, 0.999, 1e-08, 0.01, 10

VMEM_LIMIT = 56 * 1024 * 1024


def _cp(sem):
    return pltpu.CompilerParams(dimension_semantics=sem, vmem_limit_bytes=VMEM_LIMIT)


def _sigmoid(x):
    return jax.nn.sigmoid(x)


def _silu(x):
    return x * _sigmoid(x)


def _silu2(x):
    s = _sigmoid(x)
    y = x * s
    return y, s + y * (1.0 - s)


def _softplus(x):
    u = jnp.exp(-jnp.abs(x))
    w = 1.0 + u
    l1p = jnp.where(w == 1.0, u, jnp.log(w) * (u / (w - 1.0)))
    return jnp.maximum(x, 0.0) + l1p


def _dot(a, b):
    return jnp.dot(a, b, preferred_element_type=f32)


def _dot_nt(a, b):
    return lax.dot_general(a, b, (((1,), (1,)), ((), ())), preferred_element_type=f32)


def _dot_tn(a, b):
    return lax.dot_general(a, b, (((0,), (0,)), ((), ())), preferred_element_type=f32)


def _bdot(a, b):
    return lax.dot_general(a, b, (((2,), (1,)), ((0,), (0,))), preferred_element_type=f32)


def _bdot_nt(a, b):
    return lax.dot_general(a, b, (((2,), (2,)), ((0,), (0,))), preferred_element_type=f32)


def _bdot_tn(a, b):
    return lax.dot_general(a, b, (((1,), (1,)), ((0,), (0,))), preferred_element_type=f32)


def _bdot_split(a, b):
    ah, bh = _b(a), _b(b)
    al, bl = _b(a - ah.astype(f32)), _b(b - bh.astype(f32))
    return _bdot(ah, bh) + (_bdot(ah, bl) + _bdot(al, bh))


def _b(x):
    return x.astype(bf16)


def _chunk_cumsum(x, rows_in_chunk):
    n = x.shape[0]
    for s in (1, 2, 4, 8, 16, 32):
        x = x + jnp.where(rows_in_chunk >= s, pltpu.roll(x, s, axis=0), 0.0)
    return x


def _chunk_rev_cumsum(x, rows_in_chunk):
    n = x.shape[0]
    for s in (1, 2, 4, 8, 16, 32):
        x = x + jnp.where(rows_in_chunk + s < L, pltpu.roll(x, n - s, axis=0), 0.0)
    return x


def _shift_down(x, s):
    return pltpu.roll(x, s, axis=0) if s else x


def _shift_up(x, s):
    return pltpu.roll(x, x.shape[0] - s, axis=0) if s else x


def inproj_fwd(x, nw, w):
    T = x.shape[0]
    tT, tn = min(2048, T), NP_TILE // 2

    def body(x_ref, nw_ref, w_ref, p_ref, h_ref, hs):
        @pl.when(pl.program_id(1) == 0)
        def _():
            xv = x_ref[...]
            r = lax.rsqrt(jnp.mean(xv * xv, axis=-1, keepdims=True) + NORM_EPS)
            hv = _b(xv * r * nw_ref[...])
            hs[...] = hv
            h_ref[...] = hv
        p_ref[...] = _dot(hs[...], w_ref[...])

    return pl.pallas_call(
        body, name="inproj_fwd", grid=(T // tT, NP // tn),
        in_specs=[pl.BlockSpec((tT, D), lambda i, j: (i, 0)), pl.BlockSpec((1, D), lambda i, j: (0, 0)),
                  pl.BlockSpec((D, tn), lambda i, j: (0, j))],
        out_specs=[pl.BlockSpec((tT, tn), lambda i, j: (i, j)), pl.BlockSpec((tT, D), lambda i, j: (i, 0))],
        out_shape=[jax.ShapeDtypeStruct((T, NP), f32), jax.ShapeDtypeStruct((T, D), bf16)],
        scratch_shapes=[pltpu.VMEM((tT, D), bf16)],
        compiler_params=_cp(("parallel", "arbitrary")),
    )(x, nw, w)


def matmul_tn(a, b, name, n=None, b_col0=0, with_bf16=False):
    T, K = a.shape
    N = b.shape[1] if n is None else n
    tT = min(2048, T)
    tn = NP_TILE if N % NP_TILE == 0 else min(N, 1024)
    nt = T // tT
    cb0 = b_col0 // tn

    def body(a_ref, b_ref, o_ref, *ob_ref):
        @pl.when(pl.program_id(1) == 0)
        def _():
            o_ref[...] = jnp.zeros_like(o_ref)
        o_ref[...] += _dot_tn(_b(a_ref[...]), _b(b_ref[...]))
        if with_bf16:
            @pl.when(pl.program_id(1) == nt - 1)
            def _():
                ob_ref[0][...] = _b(o_ref[...])

    ospec = pl.BlockSpec((K, tn), lambda j, t: (0, j))
    return pl.pallas_call(
        body, name=name, grid=(N // tn, nt),
        in_specs=[pl.BlockSpec((tT, K), lambda j, t: (t, 0)), pl.BlockSpec((tT, tn), lambda j, t: (t, cb0 + j))],
        out_specs=[ospec, ospec] if with_bf16 else ospec,
        out_shape=([jax.ShapeDtypeStruct((K, N), f32), jax.ShapeDtypeStruct((K, N), bf16)] if with_bf16
                   else jax.ShapeDtypeStruct((K, N), f32)),
        compiler_params=_cp(("parallel", "arbitrary")),
    )(a, b)


def inproj_bwd(dp, w, x, nw, dres):
    T = x.shape[0]
    tT, tk = min(1024, T), NP // 4
    nk = NP // tk

    def body(dp_ref, w_ref, x_ref, nw_ref, dres_ref, dx_ref, dnw_ref, acc):
        i, k = pl.program_id(0), pl.program_id(1)

        @pl.when((i == 0) & (k == 0))
        def _():
            dnw_ref[...] = jnp.zeros_like(dnw_ref)

        @pl.when(k == 0)
        def _():
            acc[...] = jnp.zeros_like(acc)
        acc[...] += _dot_nt(dp_ref[...], w_ref[...])

        @pl.when(k == nk - 1)
        def _():
            xv = x_ref[...]
            r = lax.rsqrt(jnp.mean(xv * xv, axis=-1, keepdims=True) + NORM_EPS)
            xh = xv * r
            dy = acc[...]
            dyw = dy * nw_ref[...]
            dx_ref[...] = r * (dyw - xh * jnp.mean(dyw * xh, axis=-1, keepdims=True)) + dres_ref[...]
            dnw_ref[...] += jnp.sum(dy * xh, axis=0, keepdims=True)

    return pl.pallas_call(
        body, name="inproj_bwd", grid=(T // tT, nk),
        in_specs=[pl.BlockSpec((tT, tk), lambda i, k: (i, k)), pl.BlockSpec((D, tk), lambda i, k: (0, k)),
                  pl.BlockSpec((tT, D), lambda i, k: (i, 0)), pl.BlockSpec((1, D), lambda i, k: (0, 0)),
                  pl.BlockSpec((tT, D), lambda i, k: (i, 0))],
        out_specs=[pl.BlockSpec((tT, D), lambda i, k: (i, 0)), pl.BlockSpec((1, D), lambda i, k: (0, 0))],
        out_shape=[jax.ShapeDtypeStruct((T, D), f32), jax.ShapeDtypeStruct((1, D), f32)],
        scratch_shapes=[pltpu.VMEM((tT, D), f32)],
        compiler_params=_cp(("arbitrary", "arbitrary")),
    )(dp, w, x, nw, dres)


def loss_head(x, fw, tgt):
    T = x.shape[0]
    tT = min(512, T)

    def body(x_ref, fw_ref, t_ref, loss_ref, dx_ref, dfw_ref):
        @pl.when(pl.program_id(0) == 0)
        def _():
            loss_ref[...] = jnp.zeros_like(loss_ref)
            dfw_ref[...] = jnp.zeros_like(dfw_ref)
        xv = x_ref[...]
        r = lax.rsqrt(jnp.mean(xv * xv, axis=-1, keepdims=True) + NORM_EPS)
        xh = xv * r
        err = xh * fw_ref[...] - t_ref[...]
        part = 0.5 * jnp.sum(jnp.mean(err * err, axis=-1, keepdims=True), axis=0, keepdims=True)
        loss_ref[...] += jnp.broadcast_to(part, loss_ref.shape)
        dy = err * (1.0 / D)
        dyw = dy * fw_ref[...]
        dx_ref[...] = r * (dyw - xh * jnp.mean(dyw * xh, axis=-1, keepdims=True))
        dfw_ref[...] += jnp.sum(dy * xh, axis=0, keepdims=True)

    return pl.pallas_call(
        body, name="loss_head", grid=(T // tT,),
        in_specs=[pl.BlockSpec((tT, D), lambda i: (i, 0)), pl.BlockSpec((1, D), lambda i: (0, 0)),
                  pl.BlockSpec((tT, D), lambda i: (i, 0))],
        out_specs=[pl.BlockSpec((1, 128), lambda i: (0, 0)), pl.BlockSpec((tT, D), lambda i: (i, 0)),
                   pl.BlockSpec((1, D), lambda i: (0, 0))],
        out_shape=[jax.ShapeDtypeStruct((1, 128), f32), jax.ShapeDtypeStruct((T, D), f32),
                   jax.ShapeDtypeStruct((1, D), f32)],
        compiler_params=_cp(("arbitrary",)),
    )(x, fw, tgt)


def _halo_specs(tT, T, width, colblk):
    nb8 = T // 8
    per = tT // 8
    prev = pl.BlockSpec((8, width), lambda i: (jnp.maximum(i * per - 1, 0), colblk))
    nxt = pl.BlockSpec((8, width), lambda i: (jnp.minimum((i + 1) * per, nb8 - 1), colblk))
    return prev, nxt


def mixa_fwd(p, cw):
    T = p.shape[0]
    tT = min(512, T)
    prev_spec, _ = _halo_specs(tT, T, REG_A, OFF_A // REG_A)

    def body(p_ref, pp_ref, cw_ref, y_ref):
        pv = p_ref[...]
        u = pv[:, 512:1024] * pv[:, 1024:1536]
        pp = pp_ref[...]
        up = jnp.where(pl.program_id(0) == 0, 0.0, pp[:, 512:1024] * pp[:, 1024:1536])
        ue = jnp.concatenate([up, u], axis=0)
        cv = cw_ref[0:1, :] * _shift_down(ue, 2) + cw_ref[1:2, :] * _shift_down(ue, 1) + cw_ref[2:3, :] * ue
        y_ref[...] = _b(pv[:, 0:512] * cv[8:] * _silu(pv[:, 1536:2048]))

    return pl.pallas_call(
        body, name="mixa_fwd", grid=(T // tT,),
        in_specs=[pl.BlockSpec((tT, REG_A), lambda i: (i, OFF_A // REG_A)), prev_spec,
                  pl.BlockSpec((3, 512), lambda i: (0, 0))],
        out_specs=pl.BlockSpec((tT, 512), lambda i: (i, 0)),
        out_shape=jax.ShapeDtypeStruct((T, 512), bf16),
        compiler_params=_cp(("parallel",)),
    )(p, p, cw)


def mixa_bwd(p, cw, dy, dp):
    T = p.shape[0]
    tT = min(512, T)
    nt = T // tT
    prev_spec, next_spec = _halo_specs(tT, T, REG_A, OFF_A // REG_A)
    _, dnext_spec = _halo_specs(tT, T, 512, 0)

    def body(p_ref, pp_ref, pn_ref, cw_ref, dy_ref, dyn_ref, dp_in, dp_ref, dcw_ref):
        i = pl.program_id(0)

        @pl.when(i == 0)
        def _():
            dcw_ref[...] = jnp.zeros_like(dcw_ref)
        pv, pp, pn = p_ref[:, 0:2048], pp_ref[:, 0:2048], pn_ref[:, 0:2048]
        pe = jnp.concatenate([pp, pv, pn], axis=0)
        rows = lax.broadcasted_iota(jnp.int32, (tT + 16, 1), 0)
        ab, ac, ax, az = pe[:, 0:512], pe[:, 512:1024], pe[:, 1024:1536], pe[:, 1536:2048]
        u = jnp.where((rows < 8) & (i == 0), 0.0, ac * ax)
        u1, u2 = _shift_down(u, 1), _shift_down(u, 2)
        w0, w1, w2 = cw_ref[0:1, :], cw_ref[1:2, :], cw_ref[2:3, :]
        cv = w0 * u2 + w1 * u1 + w2 * u
        dye = jnp.concatenate([jnp.zeros((8, 512), f32), dy_ref[...], dyn_ref[...]], axis=0)
        dye = jnp.where((rows >= tT + 8) & (i == nt - 1), 0.0, dye)
        sz, dsz = _silu2(az)
        dcv = dye * ab * sz
        du = w2 * dcv + w1 * _shift_up(dcv, 1) + w0 * _shift_up(dcv, 2)
        inner = (rows >= 8) & (rows < tT + 8)
        dcv_in = jnp.where(inner, dcv, 0.0)
        dcw_ref[0:1, :] += jnp.sum(dcv_in * u2, axis=0, keepdims=True)
        dcw_ref[1:2, :] += jnp.sum(dcv_in * u1, axis=0, keepdims=True)
        dcw_ref[2:3, :] += jnp.sum(dcv_in * u, axis=0, keepdims=True)
        sl = slice(8, tT + 8)
        dp_ref[:, 0:512] = _b((dye * cv * sz)[sl])
        dp_ref[:, 512:1024] = _b((du * ax)[sl])
        dp_ref[:, 1024:1536] = _b((du * ac)[sl])
        dp_ref[:, 1536:2048] = _b((dye * ab * cv * dsz)[sl])
        dp_ref[:, 2048:] = jnp.zeros((tT, REG_A - 2048), bf16)

    return pl.pallas_call(
        body, name="mixa_bwd", grid=(nt,),
        in_specs=[pl.BlockSpec((tT, REG_A), lambda i: (i, OFF_A // REG_A)), prev_spec, next_spec,
                  pl.BlockSpec((3, 512), lambda i: (0, 0)),
                  pl.BlockSpec((tT, 512), lambda i: (i, 0)), dnext_spec, pl.BlockSpec(memory_space=pl.ANY)],
        out_specs=[pl.BlockSpec((tT, REG_A), lambda i: (i, OFF_A // REG_A)), pl.BlockSpec((8, 512), lambda i: (0, 0))],
        out_shape=[jax.ShapeDtypeStruct((T, NP), bf16), jax.ShapeDtypeStruct((8, 512), f32)],
        input_output_aliases={6: 0},
        compiler_params=_cp(("arbitrary",)),
    )(p, p, p, cw, dy, dy, dp)


def _l2n_fwd(y):
    return y * lax.rsqrt(jnp.sum(y * y, axis=-1, keepdims=True) + L2_EPS)


def mixc_pre_fwd(p, cw, alog_l, dtb_l):
    T = p.shape[0]
    tT = min(512, T)
    prev_spec, _ = _halo_specs(tT, T, REG_C, OFF_C // REG_C)

    def body(p_ref, pp_ref, cw_ref, al_ref, dt_ref, q_ref, k_ref, v_ref, sm_ref):
        pp = jnp.where(pl.program_id(0) == 0, 0.0, pp_ref[:, 0:2048])
        xe = jnp.concatenate([pp, p_ref[:, 0:2048]], axis=0)
        cv = (cw_ref[0:1, :] * _shift_down(xe, 3) + cw_ref[1:2, :] * _shift_down(xe, 2)
              + cw_ref[2:3, :] * _shift_down(xe, 1) + cw_ref[3:4, :] * xe)[8:]
        y = _silu(cv)
        for hh in range(4):
            sl = slice(hh * HK, (hh + 1) * HK)
            q_ref[:, sl] = _l2n_fwd(y[:, sl]) * QK_SCALE
            k_ref[:, sl] = _l2n_fwd(y[:, 512 + hh * HK:512 + (hh + 1) * HK])
        v_ref[...] = y[:, 1024:2048]
        ps = p_ref[:, 2048:2176]
        lane = lax.broadcasted_iota(jnp.int32, ps.shape, 1)
        la = -jnp.exp(al_ref[...]) * _softplus(ps + dt_ref[...])
        rin = lax.broadcasted_iota(jnp.int32, ps.shape, 0) % L
        g = _chunk_cumsum(la, rin)
        sm_ref[...] = jnp.where(lane < 8, _sigmoid(ps), jnp.where(lane < 16, g, 0.0))

    return pl.pallas_call(
        body, name="mixc_pre_fwd", grid=(T // tT,),
        in_specs=[pl.BlockSpec((tT, REG_C), lambda i: (i, OFF_C // REG_C)), prev_spec,
                  pl.BlockSpec((4, 2048), lambda i: (0, 0)),
                  pl.BlockSpec((1, 128), lambda i: (0, 0)), pl.BlockSpec((1, 128), lambda i: (0, 0))],
        out_specs=[pl.BlockSpec((tT, 512), lambda i: (i, 0)), pl.BlockSpec((tT, 512), lambda i: (i, 0)),
                   pl.BlockSpec((tT, 1024), lambda i: (i, 0)), pl.BlockSpec((tT, 128), lambda i: (i, 0))],
        out_shape=[jax.ShapeDtypeStruct((T, 512), f32), jax.ShapeDtypeStruct((T, 512), f32),
                   jax.ShapeDtypeStruct((T, 1024), f32), jax.ShapeDtypeStruct((T, 128), f32)],
        compiler_params=_cp(("parallel",)),
    )(p, p, cw, alog_l, dtb_l)


def mixc_pre_bwd(p, cw, alog_l, dtb_l, dq8, dk8, dv, dsm8, dp):
    T = p.shape[0]
    tT = min(256, T)
    nt = T // tT
    prev_spec, next_spec = _halo_specs(tT, T, REG_C, OFF_C // REG_C)
    _, n1024 = _halo_specs(tT, T, 1024, 0)

    def body(p_ref, pp_ref, pn_ref, cw_ref, al_ref, dt_ref, dq_ref, dqn_ref, dk_ref, dkn_ref,
             dv_ref, dvn_ref, dsm_ref, dp_in, dp_ref, dcw_ref, dsml_ref):
        i = pl.program_id(0)

        @pl.when(i == 0)
        def _():
            dcw_ref[...] = jnp.zeros_like(dcw_ref)
            dsml_ref[...] = jnp.zeros_like(dsml_ref)
        rows = lax.broadcasted_iota(jnp.int32, (tT + 16, 1), 0)
        pp = jnp.where(i == 0, 0.0, pp_ref[:, 0:2048])
        xe = jnp.concatenate([pp, p_ref[:, 0:2048], pn_ref[:, 0:2048]], axis=0)
        xs = [_shift_down(xe, 3), _shift_down(xe, 2), _shift_down(xe, 1), xe]
        cv = cw_ref[0:1, :] * xs[0] + cw_ref[1:2, :] * xs[1] + cw_ref[2:3, :] * xs[2] + cw_ref[3:4, :] * xs[3]
        y, dy_dcv = _silu2(cv)
        last = (rows >= tT + 8) & (i == nt - 1)
        z8q = jnp.zeros((8, 1024), f32)

        def ext(cur_ref, nxt_ref):
            return jnp.where(last, 0.0, jnp.concatenate([z8q, cur_ref[...], nxt_ref[...]], axis=0))
        dq8e, dk8e, dve = ext(dq_ref, dqn_ref), ext(dk_ref, dkn_ref), ext(dv_ref, dvn_ref)
        dys = []
        for (d8, base, scale) in ((dq8e, 0, QK_SCALE), (dk8e, 512, 1.0)):
            for hh in range(4):
                dn = (d8[:, (2 * hh) * HK:(2 * hh + 1) * HK] + d8[:, (2 * hh + 1) * HK:(2 * hh + 2) * HK]) * scale
                yh = y[:, base + hh * HK:base + (hh + 1) * HK]
                r = lax.rsqrt(jnp.sum(yh * yh, axis=-1, keepdims=True) + L2_EPS)
                nh = yh * r
                dys.append(r * (dn - nh * jnp.sum(dn * nh, axis=-1, keepdims=True)))
        dyy = jnp.concatenate(dys + [dve], axis=1)
        dcv = dyy * dy_dcv
        dx = (cw_ref[3:4, :] * dcv + cw_ref[2:3, :] * _shift_up(dcv, 1) + cw_ref[1:2, :] * _shift_up(dcv, 2)
              + cw_ref[0:1, :] * _shift_up(dcv, 3))
        dp_ref[:, 0:2048] = _b(dx[8:tT + 8])
        dp_ref[:, 2176:] = jnp.zeros((tT, REG_C - 2176), bf16)
        inner = (rows >= 8) & (rows < tT + 8)
        dcv_in = jnp.where(inner, dcv, 0.0)
        for j in range(4):
            dcw_ref[j:j + 1, :] += jnp.sum(dcv_in * xs[j], axis=0, keepdims=True)
        ps = p_ref[:, 2048:2176]
        lane = lax.broadcasted_iota(jnp.int32, ps.shape, 1)
        dsm = dsm_ref[:, 0:128]
        for hh in range(1, N_GDN):
            dsm = dsm + dsm_ref[:, hh * 128:(hh + 1) * 128]
        beta = _sigmoid(ps)
        xa = ps + dt_ref[...]
        nea = -jnp.exp(al_ref[...])
        dpa = dsm * nea * _sigmoid(xa)
        dp_ref[:, 2048:2176] = _b(jnp.where(lane < 8, dsm * beta * (1.0 - beta), jnp.where(lane < 16, dpa, 0.0)))
        amask = (lane >= 8) & (lane < 16)
        dsml_ref[0:1, :] += jnp.sum(jnp.where(amask, dsm * nea * _softplus(xa), 0.0), axis=0, keepdims=True)
        dsml_ref[1:2, :] += jnp.sum(jnp.where(amask, dpa, 0.0), axis=0, keepdims=True)

    cur1024 = pl.BlockSpec((tT, 1024), lambda i: (i, 0))
    return pl.pallas_call(
        body, name="mixc_pre_bwd", grid=(nt,),
        in_specs=[pl.BlockSpec((tT, REG_C), lambda i: (i, OFF_C // REG_C)), prev_spec, next_spec,
                  pl.BlockSpec((4, 2048), lambda i: (0, 0)),
                  pl.BlockSpec((1, 128), lambda i: (0, 0)), pl.BlockSpec((1, 128), lambda i: (0, 0)),
                  cur1024, n1024, cur1024, n1024, cur1024, n1024, cur1024, pl.BlockSpec(memory_space=pl.ANY)],
        out_specs=[pl.BlockSpec((tT, REG_C), lambda i: (i, OFF_C // REG_C)),
                   pl.BlockSpec((8, 2048), lambda i: (0, 0)), pl.BlockSpec((8, 128), lambda i: (0, 0))],
        out_shape=[jax.ShapeDtypeStruct((T, NP), bf16),
                   jax.ShapeDtypeStruct((8, 2048), f32), jax.ShapeDtypeStruct((8, 128), f32)],
        input_output_aliases={13: 0},
        compiler_params=_cp(("arbitrary",)),
    )(p, p, p, cw, alog_l, dtb_l, dq8, dq8, dk8, dk8, dv, dv, dsm8, dp)


def _tri_inverse(m):
    r = lax.broadcasted_iota(jnp.int32, (L, L), 0)
    c = lax.broadcasted_iota(jnp.int32, (L, L), 1)
    eye = (r == c).astype(f32)[None]
    same = lambda w: ((r // w) == (c // w))[None]
    md = jnp.where(same(8), m, 0.0)
    m2 = _bdot_split(md, md)
    m4 = _bdot_split(m2, m2)
    t = _bdot_split(_bdot_split(eye - md, eye + m2), eye + m4)
    for w in (16, 32, 64):
        mo = jnp.where(same(w) & jnp.logical_not(same(w // 2)), m, 0.0)
        t = t - _bdot_split(_bdot_split(t, mo), t)
    return t


def _col_to_row(col, eye):
    return jnp.sum(eye * col, axis=1, keepdims=True)


def _row_to_col(row, eye):
    return jnp.sum(eye * row, axis=2, keepdims=True)


def _gdn_chunk_terms(q, k, v, beta, g, t_inv=None):
    r = lax.broadcasted_iota(jnp.int32, (L, L), 0)
    c = lax.broadcasted_iota(jnp.int32, (L, L), 1)
    eye = (r == c).astype(f32)[None]
    causal, strict = (c <= r)[None], (c < r)[None]
    diff = g - _col_to_row(g, eye)
    dec = jnp.exp(jnp.where(causal, diff, 0.0))
    dc = jnp.where(causal, dec, 0.0)
    ds = jnp.where(strict, dec, 0.0)
    eg = jnp.exp(g)
    gl = g[:, L - 1:L, :]
    egl = jnp.exp(gl - g)
    kb = k * beta
    kk = _bdot_nt(_b(k), _b(kb))
    qk = _bdot_nt(_b(q), _b(kb))
    m = kk * ds
    aqk = qk * dc
    if t_inv is None:
        t_inv = _tri_inverse(m)
    tb = _b(t_inv)
    keg = k * eg
    u = _bdot(tb, _b(v))
    w = _bdot(tb, _b(keg))
    ks = kb * egl
    ksw = _bdot_tn(_b(ks), _b(w))
    return dict(eye=eye, causal=causal, strict=strict, dc=dc, ds=ds, eg=eg, gl=gl, egl=egl, kb=kb, kk=kk, qk=qk,
                m=m, aqk=aqk, t=t_inv, u=u, w=w, qi=q * eg, ks=ks, keg=keg, ksw=ksw)


def gdn_fwd(qn, kn, vv, sm):
    T = qn.shape[0]
    tB = min(2 * GDN_BLOCK, T)
    nc = tB // L
    N = T // L

    def body(q_ref, k_ref, v_ref, sm_ref, o_ref, st_ref, ti_ref, s_scr):
        h = pl.program_id(0)

        @pl.when(pl.program_id(1) == 0)
        def _():
            s_scr[...] = jnp.zeros_like(s_scr)
        smv = sm_ref[...]
        lane = lax.broadcasted_iota(jnp.int32, smv.shape, 1)
        beta = jnp.sum(jnp.where(lane == h, smv, 0.0), axis=1, keepdims=True).reshape(nc, L, 1)
        g = jnp.sum(jnp.where(lane == 8 + h, smv, 0.0), axis=1, keepdims=True).reshape(nc, L, 1)
        q = q_ref[...].reshape(nc, L, HK)
        k = k_ref[...].reshape(nc, L, HK)
        v = v_ref[...].reshape(nc, L, HK)
        tm = _gdn_chunk_terms(q, k, v, beta, g)
        ti_ref[...] = tm["t"]
        ksu = _bdot_tn(_b(tm["ks"]), _b(tm["u"]))
        kswb = _b(tm["ksw"])
        egl_last = jnp.exp(tm["gl"])
        s = s_scr[...]
        states = [None] * nc
        for ci in range(nc):
            states[ci] = s
            s = egl_last[ci] * s + (ksu[ci] - _dot(kswb[ci], _b(s)))
        s_scr[...] = s
        sall = jnp.stack(states, axis=0)
        st_ref[...] = sall
        sb = _b(sall)
        e = tm["u"] - _bdot(_b(tm["w"]), sb)
        o = _bdot(_b(tm["qi"]), sb) + _bdot(_b(tm["aqk"]), _b(e))
        o_ref[...] = o.reshape(tB, HK)

    return pl.pallas_call(
        body, name="gdn_fwd", grid=(N_GDN, T // tB),
        in_specs=[pl.BlockSpec((tB, HK), lambda h, n: (n, h // 2)), pl.BlockSpec((tB, HK), lambda h, n: (n, h // 2)),
                  pl.BlockSpec((tB, HK), lambda h, n: (n, h)), pl.BlockSpec((tB, 128), lambda h, n: (n, 0))],
        out_specs=[pl.BlockSpec((tB, HK), lambda h, n: (n, h)),
                   pl.BlockSpec((None, nc, HK, HK), lambda h, n: (h, n, 0, 0)),
                   pl.BlockSpec((None, nc, L, L), lambda h, n: (h, n, 0, 0))],
        out_shape=[jax.ShapeDtypeStruct((T, N_GDN * HK), f32), jax.ShapeDtypeStruct((N_GDN, N, HK, HK), f32),
                   jax.ShapeDtypeStruct((N_GDN, N, L, L), f32)],
        scratch_shapes=[pltpu.VMEM((HK, HK), f32)],
        compiler_params=_cp(("parallel", "arbitrary")),
    )(qn, kn, vv, sm)


def gdn_bwd(qn, kn, vv, sm, st, ti, do):
    T = qn.shape[0]
    tB = min(GDN_BLOCK, T)
    nc = tB // L
    nb = T // tB

    def body(q_ref, k_ref, v_ref, sm_ref, st_ref, ti_ref, do_ref, dq_ref, dk_ref, dv_ref, dsm_ref, ds_scr):
        h = pl.program_id(0)

        @pl.when(pl.program_id(1) == 0)
        def _():
            ds_scr[...] = jnp.zeros_like(ds_scr)
        smv = sm_ref[...]
        lane = lax.broadcasted_iota(jnp.int32, smv.shape, 1)
        beta = jnp.sum(jnp.where(lane == h, smv, 0.0), axis=1, keepdims=True).reshape(nc, L, 1)
        g = jnp.sum(jnp.where(lane == 8 + h, smv, 0.0), axis=1, keepdims=True).reshape(nc, L, 1)
        q = q_ref[...].reshape(nc, L, HK)
        k = k_ref[...].reshape(nc, L, HK)
        v = v_ref[...].reshape(nc, L, HK)
        do = do_ref[...].reshape(nc, L, HK)
        s = st_ref[...]
        tm = _gdn_chunk_terms(q, k, v, beta, g, t_inv=ti_ref[...])
        eye, dc, ds_, eg, egl = tm["eye"], tm["dc"], tm["ds"], tm["eg"], tm["egl"]
        kb, u, w, qi, ks, tinv = tm["kb"], tm["u"], tm["w"], tm["qi"], tm["ks"], tm["t"]
        sb, dob = _b(s), _b(do)
        e = u - _bdot(_b(w), sb)
        eb = _b(e)
        egl_last = jnp.exp(tm["gl"])
        de0 = _bdot_tn(_b(tm["aqk"]), dob)
        ds0 = _bdot_tn(_b(qi), dob) - _bdot_tn(_b(w), _b(de0))
        kswb = _b(tm["ksw"])
        dsn = ds_scr[...]
        dsns = [None] * nc
        for ci in reversed(range(nc)):
            dsns[ci] = dsn
            dsn = ds0[ci] + (egl_last[ci] * dsn - _dot_tn(kswb[ci], _b(dsn)))
        ds_scr[...] = dsn
        dsp = jnp.stack(dsns, axis=0)
        dspb = _b(dsp)
        de = de0 + _bdot(_b(ks), dspb)
        deb = _b(de)
        dks = _bdot_nt(eb, dspb)
        dqi = _bdot_nt(dob, sb)
        daqk = jnp.where(tm["causal"], _bdot_nt(dob, eb), 0.0)
        dw = -_bdot_nt(deb, sb)
        tb = _b(tinv)
        dvv = _bdot_tn(tb, deb)
        dkg = _bdot_tn(tb, _b(dw))
        dm = -jnp.where(tm["strict"], _bdot_nt(_b(dvv), _b(u)) + _bdot_nt(_b(dkg), _b(w)), 0.0)
        x = _b(dm * ds_)
        y = _b(daqk * dc)
        kbb, kbf, qbf = _b(kb), _b(k), _b(q)
        dk = _bdot(x, kbb) + dkg * eg
        dkb = _bdot_tn(x, kbf) + _bdot_tn(y, qbf) + dks * egl
        dq = _bdot(y, kbb) + dqi * eg
        dk = dk + dkb * beta
        dbeta = jnp.sum(dkb * k, axis=-1, keepdims=True)
        z = dm * tm["m"] + daqk * tm["aqk"]
        dg = (jnp.sum(dqi * qi - dks * ks + dkg * tm["keg"], axis=-1, keepdims=True)
              + jnp.sum(z, axis=-1, keepdims=True) - _row_to_col(jnp.sum(z, axis=1, keepdims=True), eye))
        dgl = (egl_last * jnp.sum(jnp.sum(s * dsp, axis=2, keepdims=True), axis=1, keepdims=True)
               + jnp.sum(jnp.sum(dks * ks, axis=2, keepdims=True), axis=1, keepdims=True))
        rowi = lax.broadcasted_iota(jnp.int32, (nc, L, 1), 1)
        dg = dg + jnp.where(rowi == L - 1, dgl, 0.0)
        dg2 = dg.reshape(tB, 1)
        rin = lax.broadcasted_iota(jnp.int32, (tB, 1), 0) % L
        dla = _chunk_rev_cumsum(jnp.broadcast_to(dg2, (tB, 128)), jnp.broadcast_to(rin, (tB, 128)))
        dq_ref[...] = dq.reshape(tB, HK)
        dk_ref[...] = dk.reshape(tB, HK)
        dv_ref[...] = dvv.reshape(tB, HK)
        dsm_ref[...] = jnp.where(lane == h, dbeta.reshape(tB, 1), jnp.where(lane == 8 + h, dla, 0.0))

    rev = lambda n: nb - 1 - n
    return pl.pallas_call(
        body, name="gdn_bwd", grid=(N_GDN, nb),
        in_specs=[pl.BlockSpec((tB, HK), lambda h, n: (rev(n), h // 2)),
                  pl.BlockSpec((tB, HK), lambda h, n: (rev(n), h // 2)),
                  pl.BlockSpec((tB, HK), lambda h, n: (rev(n), h)), pl.BlockSpec((tB, 128), lambda h, n: (rev(n), 0)),
                  pl.BlockSpec((None, nc, HK, HK), lambda h, n: (h, rev(n), 0, 0)),
                  pl.BlockSpec((None, nc, L, L), lambda h, n: (h, rev(n), 0, 0)),
                  pl.BlockSpec((tB, HK), lambda h, n: (rev(n), h))],
        out_specs=[pl.BlockSpec((tB, HK), lambda h, n: (rev(n), h))] * 4,
        out_shape=[jax.ShapeDtypeStruct((T, N_GDN * HK), f32)] * 4,
        scratch_shapes=[pltpu.VMEM((HK, HK), f32)],
        compiler_params=_cp(("parallel", "arbitrary")),
    )(qn, kn, vv, sm, st, ti, do)


def _hgrn_prep(bq, bf_, bi, lb):
    tB = bq.shape[0]
    sq, dsq = _silu2(bq)
    sg = _sigmoid(bf_)
    f = lb + (1.0 - lb) * sg
    logf = jnp.log(jnp.maximum(f, MIN_F))
    rin = lax.broadcasted_iota(jnp.int32, (tB, HK), 0) % L
    g = _chunk_cumsum(logf, rin)
    return sq * QK_SCALE, sg, f, 1.0 - f, bi, g, rin, dsq * QK_SCALE


def _hgrn_intra(q, kk, v, g, do=None):
    n = q.shape[0]
    nsub = L // SUB
    bwd = do is not None
    o_rows = [None] * nsub
    if bwd:
        dq_rows = [None] * nsub
        dkk_acc = jnp.zeros_like(kk)
        dv_acc = jnp.zeros_like(v)
    for i in range(1, nsub):
        lo, hi, w = i * SUB, (i + 1) * SUB, i * SUB
        ref = g[:, lo - 1:lo, :]
        eq = jnp.exp(g[:, lo:hi, :] - ref)
        ek = jnp.exp(ref - g[:, :w, :])
        qs = _b(q[:, lo:hi, :] * eq)
        ks = _b(kk[:, :w, :] * ek)
        p = _bdot_nt(qs, ks)
        o_rows[i] = _bdot(_b(p), _b(v[:, :w, :]))
        if bwd:
            dob = _b(do[:, lo:hi, :])
            dp = _b(_bdot_nt(dob, _b(v[:, :w, :])))
            dq_rows[i] = _bdot(dp, ks) * eq
            pad = jnp.zeros((n, L - w, HK), f32)
            dkk_acc = dkk_acc + jnp.concatenate([_bdot_tn(dp, qs) * ek, pad], axis=1)
            dv_acc = dv_acc + jnp.concatenate([_bdot_tn(_b(p), dob), pad], axis=1)
    m = n * nsub
    q4, k4, v4, g4 = (a.reshape(m, SUB, HK) for a in (q, kk, v, g))
    r = lax.broadcasted_iota(jnp.int32, (m, SUB, HK), 1)
    od = jnp.zeros((m, SUB, HK), f32)
    if bwd:
        do4 = do.reshape(m, SUB, HK)
        dqd = jnp.zeros((m, SUB, HK), f32)
        dkd = jnp.zeros((m, SUB, HK), f32)
        dvd = jnp.zeros((m, SUB, HK), f32)
    for j in range(SUB):
        gj, kj, vj = g4[:, j:j + 1, :], k4[:, j:j + 1, :], v4[:, j:j + 1, :]
        ok = r >= j
        e = jnp.where(ok, jnp.exp(g4 - gj), 0.0)
        xq = q4 * e
        pj = jnp.sum(xq * kj, axis=-1, keepdims=True)
        od = od + pj * vj
        if bwd:
            dpj = jnp.sum(do4 * vj, axis=-1, keepdims=True)
            dqd = dqd + dpj * kj * e
            dkd = dkd + jnp.where(r == j, jnp.sum(dpj * xq, axis=1, keepdims=True), 0.0)
            dvd = dvd + jnp.where(r == j, jnp.sum(pj * do4, axis=1, keepdims=True), 0.0)
    od = od.reshape(n, L, HK)
    o = jnp.concatenate([od[:, :SUB, :]] + [od[:, i * SUB:(i + 1) * SUB, :] + o_rows[i] for i in range(1, nsub)], axis=1)
    if not bwd:
        return o
    dqd = dqd.reshape(n, L, HK)
    dq = jnp.concatenate([dqd[:, :SUB, :]] + [dqd[:, i * SUB:(i + 1) * SUB, :] + dq_rows[i] for i in range(1, nsub)], axis=1)
    return o, dq, dkk_acc + dkd.reshape(n, L, HK), dv_acc + dvd.reshape(n, L, HK)


def hgrn_fwd(p, lbs):
    T = p.shape[0]
    tB = min(1024, T)
    nc = tB // L
    N = T // L

    def body(b_ref, lb_ref, o_ref, st_ref, s_scr):
        @pl.when(pl.program_id(1) == 0)
        def _():
            s_scr[...] = jnp.zeros_like(s_scr)
        q, sg, f, kk, v, g, rin, _ = _hgrn_prep(b_ref[:, 0:HK], b_ref[:, HK:2 * HK], b_ref[:, 2 * HK:3 * HK], lb_ref[...])
        q3, k3, v3, g3 = (a.reshape(nc, L, HK) for a in (q, kk, v, g))
        o = _hgrn_intra(q3, k3, v3, g3)
        gl = g3[:, L - 1:L, :]
        qt = _b(q3 * jnp.exp(g3))
        kt = _b(k3 * jnp.exp(gl - g3))
        vb = _b(v3)
        st = s_scr[...]
        for c in range(nc):
            st_ref[c] = st
            o_ref[c * L:(c + 1) * L, :] = o[c] + _dot_nt(qt[c], _b(st))
            st = st * jnp.exp(gl[c]) + _dot_tn(vb[c], kt[c])
        s_scr[...] = st

    return pl.pallas_call(
        body, name="hgrn_fwd", grid=(N_HGRN, T // tB),
        in_specs=[pl.BlockSpec((tB, REG_BH), lambda h, n: (n, OFF_B // REG_BH + h)),
                  pl.BlockSpec((1, HK), lambda h, n: (0, h))],
        out_specs=[pl.BlockSpec((tB, HK), lambda h, n: (n, h)),
                   pl.BlockSpec((None, nc, HK, HK), lambda h, n: (h, n, 0, 0))],
        out_shape=[jax.ShapeDtypeStruct((T, N_HGRN * HK), f32), jax.ShapeDtypeStruct((N_HGRN, N, HK, HK), f32)],
        scratch_shapes=[pltpu.VMEM((HK, HK), f32)],
        compiler_params=_cp(("parallel", "arbitrary")),
    )(p, lbs)


def hgrn_bwd(p, lbs, st, do, dp):
    T = p.shape[0]
    tB = min(256, T)
    nc = tB // L
    nb = T // tB

    def body(b_ref, lb_ref, st_ref, do_ref, dp_in, dp_ref, dlb_ref, ds_scr):
        @pl.when(pl.program_id(1) == 0)
        def _():
            ds_scr[...] = jnp.zeros_like(ds_scr)
            dlb_ref[...] = jnp.zeros_like(dlb_ref)
        lb = lb_ref[...]
        bq = b_ref[:, 0:HK]
        q, sg, f, kk, v, g, rin, dq_dbq = _hgrn_prep(bq, b_ref[:, HK:2 * HK], b_ref[:, 2 * HK:3 * HK], lb)
        q3, k3, v3, g3 = (a.reshape(nc, L, HK) for a in (q, kk, v, g))
        do3 = do_ref[...].reshape(nc, L, HK)
        dob = _b(do3)
        gl = g3[:, L - 1:L, :]
        egl = jnp.exp(gl)
        eg, egr = jnp.exp(g3), jnp.exp(gl - g3)
        qt, kt = q3 * eg, k3 * egr
        s = st_ref[...]
        ds0 = _bdot_tn(dob, _b(qt))
        dsn = ds_scr[...]
        dsns = [None] * nc
        for c in reversed(range(nc)):
            dsns[c] = dsn
            dsn = ds0[c] + dsn * egl[c]
        ds_scr[...] = dsn
        dsp = jnp.stack(dsns, axis=0)
        dspb = _b(dsp)
        dqt = _bdot(dob, _b(s))
        dkt = _bdot(_b(v3), dspb)
        dv_state = _bdot_nt(_b(kt), dspb)
        dgl = egl * jnp.sum(s * dsp, axis=1, keepdims=True) + jnp.sum(dkt * kt, axis=1, keepdims=True)
        _, dq_i, dkk_i, dv_i = _hgrn_intra(q3, k3, v3, g3, do=do3)
        dq = dq_i + dqt * eg
        dkk = dkk_i + dkt * egr
        dv = dv_i + dv_state
        rowi = lax.broadcasted_iota(jnp.int32, (nc, L, HK), 1)
        dg = q3 * dq - k3 * dkk + jnp.where(rowi == L - 1, dgl, 0.0)
        dlogf = _chunk_rev_cumsum(dg.reshape(tB, HK), rin)
        dkk2 = dkk.reshape(tB, HK)
        df = jnp.where(f > MIN_F, dlogf / f, 0.0) - dkk2
        dlb_ref[...] += jnp.sum(df * (1.0 - sg), axis=0, keepdims=True)
        dp_ref[:, 0:HK] = _b(dq.reshape(tB, HK) * dq_dbq)
        dp_ref[:, HK:2 * HK] = _b(df * (1.0 - lb) * sg * (1.0 - sg))
        dp_ref[:, 2 * HK:3 * HK] = _b(dv.reshape(tB, HK))

    rev = lambda n: nb - 1 - n
    return pl.pallas_call(
        body, name="hgrn_bwd", grid=(N_HGRN, nb),
        in_specs=[pl.BlockSpec((tB, REG_BH), lambda h, n: (rev(n), OFF_B // REG_BH + h)),
                  pl.BlockSpec((1, HK), lambda h, n: (0, h)),
                  pl.BlockSpec((None, nc, HK, HK), lambda h, n: (h, rev(n), 0, 0)),
                  pl.BlockSpec((tB, HK), lambda h, n: (rev(n), h)), pl.BlockSpec(memory_space=pl.ANY)],
        out_specs=[pl.BlockSpec((tB, REG_BH), lambda h, n: (rev(n), OFF_B // REG_BH + h)),
                   pl.BlockSpec((1, HK), lambda h, n: (0, h))],
        out_shape=[jax.ShapeDtypeStruct((T, NP), bf16), jax.ShapeDtypeStruct((1, N_HGRN * HK), f32)],
        input_output_aliases={4: 0},
        scratch_shapes=[pltpu.VMEM((HK, HK), f32)],
        compiler_params=_cp(("parallel", "arbitrary")),
    )(p, lbs, st, do, dp)


def _headnorm_fwd(o, z, w, nheads):
    outs, parts = [], []
    for hh in range(nheads):
        sl = slice(hh * HK, (hh + 1) * HK)
        oh = o[:, sl]
        r = lax.rsqrt(jnp.mean(oh * oh, axis=-1, keepdims=True) + NORM_EPS)
        on = oh * r
        sz, dsz = _silu2(z[:, sl])
        outs.append(on * w * sz)
        parts.append((r, on, sz, dsz))
    return jnp.concatenate(outs, axis=1), parts


def _headnorm_bwd(parts, w, dy):
    dos, dzs = [], []
    dw = jnp.zeros((1, HK), f32)
    for hh, (r, on, sz, dsz) in enumerate(parts):
        dyh = dy[:, hh * HK:(hh + 1) * HK]
        dn = dyh * sz * w
        dos.append(r * (dn - on * jnp.mean(dn * on, axis=-1, keepdims=True)))
        dzs.append(dyh * on * w * dsz)
        dw = dw + jnp.sum(dyh * sz * on, axis=0, keepdims=True)
    return jnp.concatenate(dos, axis=1), jnp.concatenate(dzs, axis=1), dw


def _merge_specs(tT, l):
    row = lambda w, cb=0: pl.BlockSpec((tT, w), lambda i, cb=cb: (i, cb))
    full = lambda r, c: pl.BlockSpec((r, c), lambda i: (0, 0))
    layer = lambda r, c: pl.BlockSpec((None, r, c), lambda i: (l, 0, 0))
    return row, full, layer


def merge_fwd(x, p, ya, ob, oc, hw, gw, bg, woa, wob, woc, wo, l):
    T = x.shape[0]
    tT = min(256, T)
    row, full, layer = _merge_specs(tT, l)

    def body(x_ref, pm_ref, ya_ref, ob_ref, oc_ref, hw_ref, gw_ref, bg_ref,
             woa_ref, wob_ref, woc_ref, wo_ref, out_ref):
        yb = _b(_headnorm_fwd(ob_ref[...], pm_ref[:, M_BZ:M_G], hw_ref[...], N_HGRN)[0])
        yc = _b(_headnorm_fwd(oc_ref[...], pm_ref[:, M_CZ:REG_M], gw_ref[...], N_GDN)[0])
        gates = _sigmoid(pm_ref[:, M_G:M_CZ] + bg_ref[...])
        merged = (gates[:, 0:D] * _dot(ya_ref[...], woa_ref[...]) + gates[:, D:2 * D] * _dot(yb, wob_ref[...])
                  + gates[:, 2 * D:3 * D] * _dot(yc, woc_ref[...]))
        out_ref[...] = x_ref[...] + _dot(_b(merged), wo_ref[...])

    return pl.pallas_call(
        body, name="merge_fwd", grid=(T // tT,),
        in_specs=[row(D), row(REG_M, OFF_M // REG_M),
                  row(512), row(512), row(1024), full(1, HK), full(1, HK), full(1, 3 * D),
                  layer(512, D), layer(512, D), layer(D, D), layer(D, D)],
        out_specs=row(D),
        out_shape=jax.ShapeDtypeStruct((T, D), f32),
        compiler_params=_cp(("parallel",)),
    )(x, p, ya, ob, oc, hw, gw, bg, woa, wob, woc, wo)


def merge_bwd(dxo, p, ya, ob, oc, hw, gw, bg, woa, wob, woc, wo, l):
    T = dxo.shape[0]
    tT = min(256, T)
    row, full, layer = _merge_specs(tT, l)

    def body(dx_ref, pm_ref, ya_ref, ob_ref, oc_ref, hw_ref, gw_ref, bg_ref,
             woa_ref, wob_ref, woc_ref, wo_ref,
             dya_ref, dob_ref, doc_ref, dp_ref, mg_ref, dy3_ref, yb_ref, yc_ref,
             dbg_ref, dhw_ref, dgw_ref):
        @pl.when(pl.program_id(0) == 0)
        def _():
            dbg_ref[...] = jnp.zeros_like(dbg_ref)
            dhw_ref[...] = jnp.zeros_like(dhw_ref)
            dgw_ref[...] = jnp.zeros_like(dgw_ref)
        ob, oc, bz, cz = ob_ref[...], oc_ref[...], pm_ref[:, M_BZ:M_G], pm_ref[:, M_CZ:REG_M]
        hw_, gw_ = hw_ref[...], gw_ref[...]
        yb, parts_b = _headnorm_fwd(ob, bz, hw_, N_HGRN)
        yc, parts_c = _headnorm_fwd(oc, cz, gw_, N_GDN)
        yb, yc = _b(yb), _b(yc)
        yb_ref[...] = yb
        yc_ref[...] = yc
        gates = _sigmoid(pm_ref[:, M_G:M_CZ] + bg_ref[...])
        ys = (_dot(ya_ref[...], woa_ref[...]), _dot(yb, wob_ref[...]), _dot(yc, woc_ref[...]))
        dmerged = _dot_nt(_b(dx_ref[...]), wo_ref[...])
        merged = jnp.zeros_like(dmerged)
        dys = []
        for i in range(3):
            gi = gates[:, i * D:(i + 1) * D]
            merged = merged + gi * ys[i]
            dyi = _b(dmerged * gi)
            dys.append(dyi)
            dy3_ref[:, i * D:(i + 1) * D] = dyi
            dgp = dmerged * ys[i] * gi * (1.0 - gi)
            dp_ref[:, M_G + i * D:M_G + (i + 1) * D] = _b(dgp)
            dbg_ref[:, i * D:(i + 1) * D] += jnp.sum(dgp, axis=0, keepdims=True)
        mg_ref[...] = _b(merged)
        dya_ref[...] = _dot_nt(dys[0], woa_ref[...])
        dob, dbz, dhw = _headnorm_bwd(parts_b, hw_, _dot_nt(dys[1], wob_ref[...]))
        doc, dcz, dgw = _headnorm_bwd(parts_c, gw_, _dot_nt(dys[2], woc_ref[...]))
        dob_ref[...] = dob
        doc_ref[...] = doc
        dp_ref[:, M_BZ:M_G] = _b(dbz)
        dp_ref[:, M_CZ:REG_M] = _b(dcz)
        dhw_ref[...] += dhw
        dgw_ref[...] += dgw

    sd = jax.ShapeDtypeStruct
    return pl.pallas_call(
        body, name="merge_bwd", grid=(T // tT,),
        in_specs=[row(D), row(REG_M, OFF_M // REG_M),
                  row(512), row(512), row(1024), full(1, HK), full(1, HK), full(1, 3 * D),
                  layer(512, D), layer(512, D), layer(D, D), layer(D, D)],
        out_specs=[row(512), row(512), row(1024), row(REG_M, OFF_M // REG_M), row(D), row(3 * D), row(512),
                   row(1024), full(1, 3 * D), full(1, HK), full(1, HK)],
        out_shape=[sd((T, 512), f32), sd((T, 512), f32), sd((T, 1024), f32), sd((T, NP), bf16),
                   sd((T, D), bf16), sd((T, 3 * D), bf16), sd((T, 512), bf16),
                   sd((T, 1024), bf16), sd((1, 3 * D), f32), sd((1, HK), f32), sd((1, HK), f32)],
        compiler_params=_cp(("arbitrary",)),
    )(dxo, p, ya, ob, oc, hw, gw, bg, woa, wob, woc, wo)


def layer_fwd(x, w):
    l = w["l"]
    p, h = inproj_fwd(x, w["norm_w"], w["w_in"])
    ya = mixa_fwd(p, w["conv_a"])
    qn, kn, vv, sm = mixc_pre_fwd(p, w["conv_c"], w["alog_l"], w["dtb_l"])
    oc, st_c, ti = gdn_fwd(qn, kn, vv, sm)
    ob, st_b = hgrn_fwd(p, w["lbs"])
    if "late" in w:
        w.update(w.pop("late")((ya, oc, ob)))
    xo = merge_fwd(x, p, ya, ob, oc, w["hgrn_norm_w"], w["gdn_norm_w"], w["b_gate"],
                   w["w_out_a"], w["w_out_b"], w["w_out_c"], w["w_o"], l)
    saved = dict(x=x, p=p, h=h, ya=ya, qn=qn, kn=kn, vv=vv, sm=sm, oc=oc, st_c=st_c, ti=ti, ob=ob, st_b=st_b)
    return xo, saved


OUT_MATS = (("w_out_a", "cols"), ("w_out_b", "cols"), ("w_out_c", "rows"), ("w_o", "rows"))


def layer_bwd(dxo, w, s, chip):
    p, l = s["p"], w["l"]
    (dya, dob, doc, dp, merged, dy3, yb, yc, dbg, dhw, dgw) = merge_bwd(
        dxo, p, s["ya"], s["ob"], s["oc"], w["hgrn_norm_w"], w["gdn_norm_w"], w["b_gate"],
        w["w_out_a"], w["w_out_b"], w["w_out_c"], w["w_o"], l)
    full = {"w_o": matmul_tn(merged, dxo, "dw_o", with_bf16=True),
            "w_out_a": matmul_tn(s["ya"], dy3, "dw_out_a", n=D, b_col0=0, with_bf16=True),
            "w_out_b": matmul_tn(yb, dy3, "dw_out_b", n=D, b_col0=D, with_bf16=True),
            "w_out_c": matmul_tn(yc, dy3, "dw_out_c", n=D, b_col0=2 * D, with_bf16=True)}
    out_kinds = [k for _, k in OUT_MATS]
    sent_out, token = exchange_start([full[n][1] for n, _ in OUT_MATS], out_kinds, f"grads_out_start{l}")
    dp, dlbs = hgrn_bwd(p, w["lbs"] + token[0:1, 0:1], s["st_b"], dob, dp)
    dq8, dk8, dvv, dsm8 = gdn_bwd(s["qn"], s["kn"], s["vv"], s["sm"], s["st_c"], s["ti"], doc)
    dp, dcc, dsmall = mixc_pre_bwd(p, w["conv_c"], w["alog_l"], w["dtb_l"], dq8, dk8, dvv, dsm8, dp)
    dp, dca = mixa_bwd(p, w["conv_a"], dya, dp)
    gf_win, gb_win = win_from_padded(matmul_tn(s["h"], dp, "dw_in"))
    sent_in, token = exchange_start([gb_win], ["slot"], f"grads_in_start{l}")
    dx, dnw = inproj_bwd(dp, w["w_in"], s["x"], w["norm_w"] + token[0:1, 0:1], dxo)
    recv_out = exchange_wait(sent_out, out_kinds, dx, f"grads_out_wait{l}")
    recv_in = exchange_wait(sent_in, ["slot"], dx, f"grads_in_wait{l}")
    half = {"w_in": partial_sum(gf_win, "slot", recv_in[0], chip, "psum_w_in", transposed=True)}
    for (n, kind), r in zip(OUT_MATS, recv_out):
        half[n] = partial_sum(full[n][0], kind, r, chip, "psum_" + n)
    small = dict(norm_w=dnw, b_gate=dbg, hgrn_norm_w=dhw, gdn_norm_w=dgw, lbs=dlbs, conv_a=dca[0:3], conv_c=dcc[0:4],
                 a_log=dsmall[0:1, 8:16], dt_bias=dsmall[1:2, 8:16])
    return dx, small, half


def lbs_fwd(lb):
    def body(lb_ref, o_ref):
        l0, l1 = lb_ref[0:1, :], lb_ref[1:2, :]
        mx = jnp.maximum(l0, l1)
        e0, e1 = jnp.exp(l0 - mx), jnp.exp(l1 - mx)
        o_ref[0:1, :] = jnp.zeros_like(l0)
        o_ref[1:2, :] = e1 / (e0 + e1)
    return pl.pallas_call(body, name="lbs_fwd", out_shape=jax.ShapeDtypeStruct(lb.shape, f32))(lb)


def _adam_math(w, g, m, v):
    mn = ADAM_B1 * m + (1.0 - ADAM_B1) * g
    vn = ADAM_B2 * v + (1.0 - ADAM_B2) * (g * g)
    mh = mn / (1.0 - ADAM_B1 ** ADAM_STEP)
    vh = vn / (1.0 - ADAM_B2 ** ADAM_STEP)
    return -ADAM_LR * (mh / (jnp.sqrt(vh) + ADAM_EPS) + ADAM_WD * w), mn, vn


def adam(w, g, m, v, name):
    R, C = w.shape
    tr = 256 if R % 256 == 0 else R

    def body(w_ref, g_ref, m_ref, v_ref, d_ref, mo_ref, vo_ref):
        d, mn, vn = _adam_math(w_ref[...], g_ref[...], m_ref[...], v_ref[...])
        d_ref[...] = d
        mo_ref[...] = mn
        vo_ref[...] = vn

    spec = pl.BlockSpec((tr, C), lambda i: (i, 0))
    return pl.pallas_call(
        body, name=name, grid=(R // tr,), in_specs=[spec] * 4, out_specs=[spec] * 3,
        out_shape=[jax.ShapeDtypeStruct((R, C), f32)] * 3, compiler_params=_cp(("parallel",)),
    )(w, g, m, v)


def adam_pair(h, hs, w, m, v, name):
    _, R, C = w.shape
    cp = h[0].shape[1]
    tr = 128 if R % 128 == 0 else R
    nt = R // tr

    def body(h0_ref, h1_ref, s0_ref, s1_ref, w_ref, m_ref, v_ref, g_ref, d_ref, mo_ref, vo_ref):
        def update(h_ref, s_ref):
            g = (h_ref[...] + s_ref[...])[:, :C]
            d, mn, vn = _adam_math(w_ref[...], g, m_ref[...], v_ref[...])
            g_ref[...] = g
            d_ref[...] = d
            mo_ref[...] = mn
            vo_ref[...] = vn

        @pl.when(pl.program_id(0) == 0)
        def _():
            update(h0_ref, s0_ref)

        @pl.when(pl.program_id(0) == 1)
        def _():
            update(h1_ref, s1_ref)

    h0spec = pl.BlockSpec((tr, cp), lambda l, i: (jnp.where(l == 0, i, nt - 1), 0))
    h1spec = pl.BlockSpec((tr, cp), lambda l, i: (jnp.where(l == 1, i, 0), 0))
    spec = pl.BlockSpec((None, tr, C), lambda l, i: (l, i, 0))
    return pl.pallas_call(
        body, name=name, grid=(2, nt), in_specs=[h0spec, h1spec, h0spec, h1spec, spec, spec, spec],
        out_specs=[spec] * 4, out_shape=[jax.ShapeDtypeStruct(w.shape, f32)] * 4,
        compiler_params=_cp(("arbitrary", "arbitrary")),
    )(h[0], h[1], hs[0], hs[1], w, m, v)


_SMALL = (("norm_w", 2 * D), ("b_gate", 6 * D), ("lower_bounds", None), ("hgrn_norm_w", 2 * HK),
          ("gdn_norm_w", 2 * HK), ("a_log", 16), ("dt_bias", 16), ("final_norm_w", D), ("loss", None))
_CONV = (("conv_a", 2 * 3 * 512), ("conv_c", 2 * 4 * 2048))


def _small_rows(n):
    return 16 if n is None else -(-n // 1024) * 8


LB_ROW = sum(_small_rows(n) for _, n in _SMALL[:2])
ADAM_ROWS = sum(_small_rows(n) for _, n in _SMALL)
SMALL_ROWS = ADAM_ROWS + sum(_small_rows(n) for _, n in _CONV)


def small_update(parts, wp, mp, vp):
    def body(p_ref, w_ref, m_ref, v_ref, g_ref, d_ref, mo_ref, vo_ref):
        gs = p_ref[0]
        for i in range(1, 8):
            gs = gs + p_ref[i]
        w = w_ref[...]
        l0, l1 = w[LB_ROW:LB_ROW + 8], w[LB_ROW + 8:LB_ROW + 16]
        mx = jnp.maximum(l0, l1)
        e0, e1 = jnp.exp(l0 - mx), jnp.exp(l1 - mx)
        p0, p1 = e0 / (e0 + e1), e1 / (e0 + e1)
        dl1 = gs[LB_ROW + 8:LB_ROW + 16]
        s = p1 * dl1
        g = jnp.concatenate([gs[0:LB_ROW], -p0 * s, p1 * dl1 - p1 * s, gs[LB_ROW + 16:ADAM_ROWS]], axis=0)
        d, mn, vn = _adam_math(w, g, m_ref[...], v_ref[...])
        g_ref[0:ADAM_ROWS, :] = g
        g_ref[ADAM_ROWS:, :] = gs[ADAM_ROWS:]
        d_ref[...] = d
        mo_ref[...] = mn
        vo_ref[...] = vn
    sd = jax.ShapeDtypeStruct
    return pl.pallas_call(body, name="small_update",
                          out_shape=[sd((SMALL_ROWS, 128), f32)] + [sd((ADAM_ROWS, 128), f32)] * 3)(parts, wp, mp, vp)


def partial_sum(own, kind, recv, chip, name, transposed=False):
    _, r, c = recv.shape
    tr = 256 if r % 256 == 0 else r

    def body(chip_ref, o_ref, r_ref, out_ref):
        s = ((o_ref[...] + r_ref[0].astype(f32)) + r_ref[1].astype(f32)) + r_ref[2].astype(f32)
        out_ref[...] = s.T if transposed else s

    own_spec = {"slot": pl.BlockSpec((None, tr, c), lambda i, chip: (chip[0], i, 0)),
                "cols": pl.BlockSpec((tr, c), lambda i, chip: (i, chip[0])),
                "rows": pl.BlockSpec((tr, c), lambda i, chip: (chip[0] * (r // tr) + i, 0))}[kind]
    out_spec = pl.BlockSpec((c, tr), lambda i, chip: (0, i)) if transposed else pl.BlockSpec((tr, c), lambda i, chip: (i, 0))
    return pl.pallas_call(
        body, name=name,
        grid_spec=pltpu.PrefetchScalarGridSpec(
            num_scalar_prefetch=1, grid=(r // tr,),
            in_specs=[own_spec, pl.BlockSpec((3, tr, c), lambda i, chip: (0, i, 0))], out_specs=out_spec),
        out_shape=jax.ShapeDtypeStruct((c, r) if transposed else (r, c), f32), compiler_params=_cp(("arbitrary",)),
    )(chip, own, recv)


def adam_pair_t(h, hs, wt, mt, vt, name):
    C, _, R = wt.shape
    tc = 128

    def body(h0_ref, h1_ref, s0_ref, s1_ref, w_ref, m_ref, v_ref, g_ref, d_ref, mo_ref, vo_ref):
        g = jnp.stack([h0_ref[...] + s0_ref[...], h1_ref[...] + s1_ref[...]], axis=1)
        d, mn, vn = _adam_math(w_ref[...], g, m_ref[...], v_ref[...])
        g_ref[...] = g
        d_ref[...] = d
        mo_ref[...] = mn
        vo_ref[...] = vn

    hspec = pl.BlockSpec((tc, R), lambda i: (i, 0))
    spec = pl.BlockSpec((tc, 2, R), lambda i: (i, 0, 0))
    return pl.pallas_call(
        body, name=name, grid=(pl.cdiv(C, tc),), in_specs=[hspec] * 4 + [spec] * 3, out_specs=[spec] * 4,
        out_shape=[jax.ShapeDtypeStruct(wt.shape, f32)] * 4, compiler_params=_cp(("parallel",)),
    )(h[0], h[1], hs[0], hs[1], wt, mt, vt)


MESH = pl.DeviceIdType.MESH
_HBM = pl.BlockSpec(memory_space=pltpu.HBM)


def _place():
    return lax.axis_index("x"), lax.axis_index("y"), lax.axis_index("c")


def weight_gather(arrs):
    n = len(arrs)

    def body(*refs):
        x_refs, out_refs = refs[:n], refs[n:2 * n]
        send_sems, recv_sems, local_sems = refs[2 * n:]
        x, y, c = _place()
        me, sibling = (x, y, c), (x, y, 1 - c)
        chips = [(1 - x, y), (x, 1 - y), (1 - x, 1 - y)]

        def copy(a, k, block, to, own_src=False):
            px, py, pc = block
            dst = out_refs[a].at[2 * px + py, pc]
            return pltpu.make_async_remote_copy(
                src_ref=x_refs[a].at[c] if own_src else dst, dst_ref=dst,
                send_sem=send_sems.at[7 * a + k], recv_sem=recv_sems.at[7 * a + k], device_id=to, device_id_type=MESH)

        mine = [pltpu.make_async_copy(x_refs[a].at[c], out_refs[a].at[2 * x + y, c], local_sems.at[a])
                for a in range(n)]
        for cp in mine:
            cp.start()
        first = []
        for a in range(n):
            first.append(copy(a, 0, me, sibling, own_src=True))
            first += [copy(a, 1 + j, me, (*chip, c), own_src=True) for j, chip in enumerate(chips)]
        for cp in first:
            cp.start()
        passed = []
        for j, chip in enumerate(chips):
            for a in range(n):
                copy(a, 1 + j, (*chip, c), me).wait_recv()
                fwd = copy(a, 4 + j, (*chip, c), sibling)
                fwd.start()
                passed.append(fwd)
        for a in range(n):
            copy(a, 0, sibling, me).wait_recv()
            for j, chip in enumerate(chips):
                copy(a, 4 + j, (*chip, 1 - c), me).wait_recv()
        for cp in first + passed:
            cp.wait_send()
        for cp in mine:
            cp.wait()

    return pl.pallas_call(
        body, name="weight_gather", in_specs=[_HBM] * n, out_specs=[_HBM] * n,
        out_shape=[jax.ShapeDtypeStruct((N_CHIPS,) + a.shape, a.dtype) for a in arrs],
        scratch_shapes=[pltpu.SemaphoreType.DMA((7 * n,)), pltpu.SemaphoreType.DMA((7 * n,)),
                        pltpu.SemaphoreType.DMA((n,))],
    )(*arrs)


SHARD_W = 256


_SEM = pl.BlockSpec(memory_space=pltpu.SEMAPHORE)
_EFFECT = pltpu.SideEffectType.DATAFLOW_SIDE_EFFECTING


def _landing_shape(a, kind):
    if kind == "all":
        return (N_CHIPS,) + a.shape
    if kind == "slot":
        return (3,) + a.shape[1:]
    return (3,) + ((a.shape[0], SHARD_W) if kind == "cols" else (SHARD_W, a.shape[1]))


def _shard_copies(src_refs, land_refs, kinds, send_sems, recv_sems):
    x, y, c = _place()
    copies = []
    for a, (src, land, kind) in enumerate(zip(src_refs, land_refs, kinds)):
        for j, (px, py) in enumerate(((1 - x, y), (x, 1 - y), (1 - x, 1 - y))):
            q = 2 * px + py
            lo = pl.multiple_of(q * SHARD_W, SHARD_W)
            part = {"slot": lambda: src.at[q], "cols": lambda: src.at[:, pl.ds(lo, SHARD_W)],
                    "rows": lambda: src.at[pl.ds(lo, SHARD_W), :], "all": lambda: src}[kind]()
            k = 3 * a + j
            copies.append(pltpu.make_async_remote_copy(
                src_ref=part, dst_ref=land.at[2 * x + y] if kind == "all" else land.at[j],
                send_sem=send_sems.at[k], recv_sem=recv_sems.at[k], device_id=(px, py, c), device_id_type=MESH))
    return copies


def exchange_start(srcs, kinds, name, after=None):
    n = len(srcs)
    lands = [lax.empty(_landing_shape(a, k), a.dtype) for a, k in zip(srcs, kinds)]
    extra = [] if after is None else [after]

    def body(*refs):
        src_refs, land_refs, token = refs[:n], refs[n:2 * n], refs[-1]
        send_sems, recv_sems = refs[2 * n + len(extra)], refs[2 * n + len(extra) + 1]
        for cp in _shard_copies(src_refs, land_refs, kinds, send_sems, recv_sems):
            cp.start()
        token[...] = jnp.zeros_like(token)

    both = list(srcs) + lands
    out = pl.pallas_call(
        body, name=name,
        out_shape=(pltpu.SemaphoreType.DMA((3 * n,)), pltpu.SemaphoreType.DMA((3 * n,)),
                   *[pltpu.HBM(a.shape, a.dtype) for a in both], jax.ShapeDtypeStruct((8, 128), f32)),
        in_specs=[_HBM] * (2 * n) + [pl.BlockSpec(memory_space=pl.ANY)] * len(extra),
        out_specs=(_SEM, _SEM, *[_HBM] * (2 * n), pl.BlockSpec(memory_space=pltpu.VMEM)),
        input_output_aliases={i: 2 + i for i in range(2 * n)},
        compiler_params=pltpu.CompilerParams(has_side_effects=_EFFECT),
    )(*[pltpu.with_memory_space_constraint(a, pltpu.HBM) for a in both], *extra)
    return (out[0], out[1], out[2:2 + 2 * n]), out[-1]


def exchange_wait(handle, kinds, after, name):
    send_sems, recv_sems, both = handle
    n = len(kinds)
    after = after if isinstance(after, tuple) else (after,)

    def body(*refs):
        src_refs, land_refs, s_sems, r_sems = refs[:n], refs[n:2 * n], refs[2 * n], refs[2 * n + 1]
        for cp in _shard_copies(src_refs, land_refs, kinds, s_sems, r_sems):
            cp.wait_send()
            cp.wait_recv()

    out = pl.pallas_call(
        body, name=name, out_shape=tuple(pltpu.HBM(a.shape, a.dtype) for a in both),
        in_specs=[_HBM] * (2 * n) + [_SEM, _SEM] + [pl.BlockSpec(memory_space=pl.ANY)] * len(after),
        out_specs=tuple([_HBM] * (2 * n)), input_output_aliases={i: i for i in range(2 * n)},
        compiler_params=pltpu.CompilerParams(has_side_effects=_EFFECT),
    )(*both, send_sems, recv_sems, *after)
    return out[n:]


def final_exchange(hs, small):
    n = len(hs)
    S = small.shape[0]

    def body(*refs):
        h_refs, sm_ref, out_refs, smalls_ref = refs[:n], refs[n], refs[n + 1:2 * n + 1], refs[2 * n + 1]
        send_sems, recv_sems, local_sem = refs[2 * n + 2:]
        x, y, c = _place()
        my_slot = smalls_ref.at[4 * x + 2 * y + c]
        mine = pltpu.make_async_copy(sm_ref, my_slot, local_sem)
        mine.start()
        copies = [pltpu.make_async_remote_copy(src_ref=h_refs[a], dst_ref=out_refs[a], send_sem=send_sems.at[a],
                                               recv_sem=recv_sems.at[a], device_id=(x, y, 1 - c), device_id_type=MESH)
                  for a in range(n)]
        for mask in range(1, 8):
            fx, fy, fc = (mask >> 2) & 1, (mask >> 1) & 1, mask & 1
            peer = ((1 - x) if fx else x, (1 - y) if fy else y, (1 - c) if fc else c)
            copies.append(pltpu.make_async_remote_copy(
                src_ref=sm_ref, dst_ref=my_slot, send_sem=send_sems.at[n - 1 + mask], recv_sem=recv_sems.at[n - 1 + mask],
                device_id=peer, device_id_type=MESH))
        for cp in copies:
            cp.start()
        for cp in copies:
            cp.wait_recv()
        for cp in copies:
            cp.wait_send()
        mine.wait()

    sd = jax.ShapeDtypeStruct
    out = pl.pallas_call(
        body, name="final_exchange", in_specs=[_HBM] * (n + 1), out_specs=[_HBM] * (n + 1),
        out_shape=[sd(h.shape, h.dtype) for h in hs] + [sd((8, S, 128), f32)],
        scratch_shapes=[pltpu.SemaphoreType.DMA((n + 7,)), pltpu.SemaphoreType.DMA((n + 7,)),
                        pltpu.SemaphoreType.DMA],
    )(*hs, small)
    return out[:n], out[n]


N_CHIPS = 4
SHARD_COLS = N_ORIG // N_CHIPS


SHARD_PAD = 2688
_COL_SEGMENTS = (
    ((0, 2048, OFF_A),)
    + tuple((2048 + 512 * j + HK * h, 2048 + 512 * j + HK * (h + 1), OFF_B + REG_BH * h + HK * j)
            for j in range(3) for h in range(N_HGRN))
    + ((3584, 4096, OFF_M + M_BZ), (4096, 6144, OFF_C), (6144, 6160, OFF_C + 2048), (6160, 7184, OFF_M + M_CZ),
       (7184, N_ORIG, OFF_M + M_G)))


def _shard_pieces():
    pieces = []
    for lo, hi, dst in _COL_SEGMENTS:
        for p in range(N_CHIPS):
            a, b = max(lo, p * SHARD_COLS), min(hi, (p + 1) * SHARD_COLS)
            if a < b:
                pieces.append((p, a - p * SHARD_COLS, dst + a - lo, b - a))
    return pieces


def win_cast_pad(wt):
    tc = 128

    def body(x_ref, o0_ref, o1_ref):
        col = pl.program_id(0) * tc + lax.broadcasted_iota(jnp.int32, (tc, 1), 0)
        for l, o_ref in enumerate((o0_ref, o1_ref)):
            o_ref[...] = _b(jnp.where(col < SHARD_COLS, x_ref[:, l, :], 0.0).T)

    spec = pl.BlockSpec((D, tc), lambda i: (0, i))
    return pl.pallas_call(
        body, name="win_cast_pad", grid=(SHARD_PAD // tc,),
        in_specs=[pl.BlockSpec((tc, 2, D), lambda i: (i, 0, 0))], out_specs=[spec, spec],
        out_shape=[jax.ShapeDtypeStruct((D, SHARD_PAD), bf16)] * 2, compiler_params=_cp(("parallel",)),
    )(wt)


def win_to_padded(w4, name):
    tr = 256
    pieces = _shard_pieces()

    def body(a_ref, o_ref):
        o_ref[...] = jnp.zeros((tr, NP), bf16)
        for p, j0, c0, n in pieces:
            o_ref[:, c0:c0 + n] = a_ref[p, :, j0:j0 + n]

    return pl.pallas_call(
        body, name=name, grid=(D // tr,),
        in_specs=[pl.BlockSpec((N_CHIPS, tr, SHARD_PAD), lambda i: (0, i, 0))],
        out_specs=pl.BlockSpec((tr, NP), lambda i: (i, 0)),
        out_shape=jax.ShapeDtypeStruct((D, NP), bf16), compiler_params=_cp(("parallel",)),
    )(w4)


def win_from_padded(dw):
    tr = 128
    pieces = _shard_pieces()

    def body(d_ref, of_ref, ob_ref):
        for p in range(N_CHIPS):
            of_ref[p, :, SHARD_COLS:] = jnp.zeros((tr, SHARD_PAD - SHARD_COLS), f32)
            ob_ref[p, :, SHARD_COLS:] = jnp.zeros((tr, SHARD_PAD - SHARD_COLS), bf16)
        for p, j0, c0, n in pieces:
            v = d_ref[:, c0:c0 + n]
            of_ref[p, :, j0:j0 + n] = v
            ob_ref[p, :, j0:j0 + n] = _b(v)

    out_spec = pl.BlockSpec((N_CHIPS, tr, SHARD_PAD), lambda i: (0, i, 0))
    return pl.pallas_call(
        body, name="win_from_padded", grid=(D // tr,),
        in_specs=[pl.BlockSpec((tr, NP), lambda i: (i, 0))], out_specs=[out_spec, out_spec],
        out_shape=[jax.ShapeDtypeStruct((N_CHIPS, D, SHARD_PAD), f32),
                   jax.ShapeDtypeStruct((N_CHIPS, D, SHARD_PAD), bf16)],
        compiler_params=_cp(("parallel",)),
    )(dw)


def _rows128(a):
    flat = a.reshape(-1)
    total = -(-flat.shape[0] // 1024) * 1024
    return jnp.pad(flat, (0, total - flat.shape[0])).reshape(total // 128, 128)


def _lb_rows(lb):
    return jnp.pad(lb.reshape(2, 4, 128), ((0, 0), (0, 4), (0, 0))).reshape(16, 128)


def _pack_small(v, with_conv):
    rows = []
    for name, n in _SMALL + (_CONV if with_conv else ()):
        if name == "lower_bounds":
            rows.append(_lb_rows(v[name]))
        elif name == "loss":
            rows.append(jnp.broadcast_to(v[name], (16, 128)) if name in v else jnp.zeros((16, 128), f32))
        else:
            rows.append(_rows128(v[name]))
    return jnp.concatenate(rows, axis=0)


def _unpack_small(p, shapes, with_conv):
    out, row = {}, 0
    for name, n in _SMALL + (_CONV if with_conv else ()):
        nrows = _small_rows(n)
        blk = p[row:row + nrows]
        if name == "lower_bounds":
            out[name] = blk.reshape(2, 8, 128)[:, :4].reshape(2, 512)
        elif name == "loss":
            out[name] = blk[0, 0]
        else:
            out[name] = blk.reshape(-1)[:n].reshape(shapes[name])
        row += nrows
    return out


def _lane_vec(a8):
    return jnp.pad(a8.reshape(1, 8), ((0, 0), (8, 112)))


WEIGHT_NAMES = ("norm_w", "w_in", "b_gate", "conv_a", "conv_c", "a_log", "dt_bias", "lower_bounds", "hgrn_norm_w",
                "gdn_norm_w", "w_out_a", "w_out_b", "w_out_c", "w_o", "final_norm_w")


def kernel(x, norm_w, w_in, b_gate, conv_a, conv_c, a_log, dt_bias, lower_bounds, hgrn_norm_w, gdn_norm_w, w_out_a, w_out_b, w_out_c, w_o, final_norm_w, loss_target, m_norm_w, m_w_in, m_b_gate, m_conv_a, m_conv_c, m_a_log, m_dt_bias, m_lower_bounds, m_hgrn_norm_w, m_gdn_norm_w, m_w_out_a, m_w_out_b, m_w_out_c, m_w_o, m_final_norm_w, v_norm_w, v_w_in, v_b_gate, v_conv_a, v_conv_c, v_a_log, v_dt_bias, v_lower_bounds, v_hgrn_norm_w, v_gdn_norm_w, v_w_out_a, v_w_out_b, v_w_out_c, v_w_o, v_final_norm_w):
    wts = dict(norm_w=norm_w, w_in=w_in, b_gate=b_gate, conv_a=conv_a, conv_c=conv_c, a_log=a_log, dt_bias=dt_bias,
               lower_bounds=lower_bounds, hgrn_norm_w=hgrn_norm_w, gdn_norm_w=gdn_norm_w, w_out_a=w_out_a,
               w_out_b=w_out_b, w_out_c=w_out_c, w_o=w_o, final_norm_w=final_norm_w)
    mom = dict(norm_w=m_norm_w, w_in=m_w_in, b_gate=m_b_gate, conv_a=m_conv_a, conv_c=m_conv_c, a_log=m_a_log,
               dt_bias=m_dt_bias, lower_bounds=m_lower_bounds, hgrn_norm_w=m_hgrn_norm_w, gdn_norm_w=m_gdn_norm_w,
               w_out_a=m_w_out_a, w_out_b=m_w_out_b, w_out_c=m_w_out_c, w_o=m_w_o, final_norm_w=m_final_norm_w)
    var = dict(norm_w=v_norm_w, w_in=v_w_in, b_gate=v_b_gate, conv_a=v_conv_a, conv_c=v_conv_c, a_log=v_a_log,
               dt_bias=v_dt_bias, lower_bounds=v_lower_bounds, hgrn_norm_w=v_hgrn_norm_w, gdn_norm_w=v_gdn_norm_w,
               w_out_a=v_w_out_a, w_out_b=v_w_out_b, w_out_c=v_w_out_c, w_o=v_w_o, final_norm_w=v_final_norm_w)
    chip = 2 * lax.axis_index("x") + lax.axis_index("y")
    chip1 = chip.reshape(1).astype(jnp.int32)

    win_l0, win_l1 = win_cast_pad(jnp.transpose(w_in, (2, 0, 1)))
    win4_l0, ca4, cc4 = weight_gather([win_l0.reshape(2, D // 2, SHARD_PAD), conv_a, conv_c])
    by_cols = lambda a: a.transpose(1, 2, 0, 3).reshape(a.shape[1], a.shape[2], N_CHIPS * a.shape[3])
    by_rows = lambda a: a.transpose(1, 0, 2, 3).reshape(a.shape[1], N_CHIPS * a.shape[2], a.shape[3])
    conv_a_full, conv_c_full = by_cols(ca4), by_cols(cc4)
    later = [win_l1, _b(w_out_a), _b(w_out_b), _b(w_out_c), _b(w_o)]
    sent_w, token = exchange_start(later, ["all"] * 5, "weights_start", after=win4_l0)

    def late_weights(after):
        lands = exchange_wait(sent_w, ["all"] * 5, after, "weights_wait")
        l1, woa4, wob4, woc4, wo4 = (lax.dynamic_update_index_in_dim(land, own, chip, 0)
                                     for land, own in zip(lands, later))
        outs = dict(w_out_a=by_cols(woa4), w_out_b=by_cols(wob4), w_out_c=by_rows(woc4), w_o=by_rows(wo4))
        layers[1].update(outs, w_in=win_to_padded(l1, "win_to_padded1"))
        return outs

    lbs = lbs_fwd(lower_bounds)
    layers = []
    for l in range(2):
        layers.append(dict(
            l=l, norm_w=norm_w[l:l + 1], b_gate=b_gate[l:l + 1], conv_a=conv_a_full[l], conv_c=conv_c_full[l],
            alog_l=_lane_vec(a_log[l]), dtb_l=_lane_vec(dt_bias[l]), lbs=lbs[l:l + 1],
            hgrn_norm_w=hgrn_norm_w[l:l + 1], gdn_norm_w=gdn_norm_w[l:l + 1]))
    layers[0].update(w_in=win_to_padded(win4_l0.reshape(N_CHIPS, D, SHARD_PAD), "win_to_padded0"), late=late_weights,
                     norm_w=norm_w[0:1] + token[0:1, 0:1])

    xs, saved = x[0], []
    for l in range(2):
        xs, s = layer_fwd(xs, layers[l])
        saved.append(s)
    loss_row, dx, dfw = loss_head(xs, final_norm_w.reshape(1, D), loss_target[0])
    lg, half = [None, None], [None, None]
    for l in (1, 0):
        dx, lg[l], half[l] = layer_bwd(dx, layers[l], saved[l], chip1)
    grad_x = dx[None]

    stack = lambda n: jnp.stack([lg[0][n], lg[1][n]], axis=0)
    gsmall = {n: stack(n) for n in ("norm_w", "b_gate", "hgrn_norm_w", "gdn_norm_w", "a_log", "dt_bias", "conv_a",
                                    "conv_c")}
    gsmall.update(lower_bounds=stack("lbs"), final_norm_w=dfw, loss=loss_row)
    mat_names = ("w_in",) + tuple(n for n, _ in OUT_MATS)
    mine = [half[l][n] for n in mat_names for l in range(2)]
    theirs, smalls = final_exchange(mine, _pack_small(gsmall, True))

    out_g, out_d, out_m, out_v = {}, {}, {}, {}
    for i, n in enumerate(mat_names):
        h, hs = mine[2 * i:2 * i + 2], theirs[2 * i:2 * i + 2]
        if n == "w_in":
            fwd, back = (lambda a: jnp.transpose(a, (2, 0, 1))), (lambda a: jnp.transpose(a, (1, 2, 0)))
            res = adam_pair_t(h, hs, fwd(wts[n]), fwd(mom[n]), fwd(var[n]), "adam_" + n)
            out_g[n], out_d[n], out_m[n], out_v[n] = (back(a) for a in res)
        else:
            out_g[n], out_d[n], out_m[n], out_v[n] = adam_pair(h, hs, wts[n], mom[n], var[n], "adam_" + n)
    small_names = [n for n, _ in _SMALL if n != "loss"]
    pack = lambda v: _pack_small({n: v[n] for n in small_names}, False)
    sg, sd, smn, svn = small_update(smalls, pack(wts), pack(mom), pack(var))
    shapes = {n: wts[n].shape for n in small_names}
    shapes.update(conv_a=(2, 3, 512), conv_c=(2, 4, 2048))
    for dst, src, conv in ((out_g, sg, True), (out_d, sd, False), (out_m, smn, False), (out_v, svn, False)):
        dst.update(_unpack_small(src, shapes, conv))
    loss = out_g.pop("loss")
    for n in ("conv_a", "conv_c"):
        width = wts[n].shape[2]
        g = lax.dynamic_slice_in_dim(out_g[n], chip * width, width, axis=2)
        two_d = lambda a: a.reshape(-1, width)
        d, mn, vn = adam(two_d(wts[n]), two_d(g), two_d(mom[n]), two_d(var[n]), "adam_" + n)
        out_g[n] = g
        out_d[n], out_m[n], out_v[n] = (a.reshape(wts[n].shape) for a in (d, mn, vn))
    return (loss, grad_x, *[out_g[n] for n in WEIGHT_NAMES], *[out_d[n] for n in WEIGHT_NAMES],
            *[out_m[n] for n in WEIGHT_NAMES], *[out_v[n] for n in WEIGHT_NAMES])
```

```python
import jax
import jax.numpy as jnp
from jax import lax
from jax.experimental import pallas as pl
from jax.experimental.pallas import tpu as pltpu

f32 = jnp.float32
bf16 = jnp.bfloat16

D = 1024
L = 64
SUB = 16
NORM_EPS = 1e-6
L2_EPS = 1e-6
MIN_F = 1e-30
HK = 128
QK_SCALE = HK ** -0.5
N_GDN = 8
GDN_BLOCK = 1024
N_HGRN = 4

REG_A = 2304
REG_C = 2304
REG_M = 4608
REG_BH = 384
OFF_A, OFF_C, OFF_M, OFF_B = 0, 2304, 4608, 9216
M_BZ, M_G, M_CZ = 0, 512, 3584
NP = 10752
NP_TILE = 1536
N_ORIG = 10256

ADAM_LR, ADAM_B1, ADAM_B2, ADAM_EPS, ADAM_WD, ADAM_STEP = 0.001, 0.9, 0.999, 1e-08, 0.01, 10

VMEM_LIMIT = 56 * 1024 * 1024


def _cp(sem):
    return pltpu.CompilerParams(dimension_semantics=sem, vmem_limit_bytes=VMEM_LIMIT)


def _sigmoid(x):
    return jax.nn.sigmoid(x)


def _silu(x):
    return x * _sigmoid(x)


def _silu2(x):
    s = _sigmoid(x)
    y = x * s
    return y, s + y * (1.0 - s)


def _softplus(x):
    u = jnp.exp(-jnp.abs(x))
    w = 1.0 + u
    l1p = jnp.where(w == 1.0, u, jnp.log(w) * (u / (w - 1.0)))
    return jnp.maximum(x, 0.0) + l1p


def _dot(a, b):
    return jnp.dot(a, b, preferred_element_type=f32)


def _dot_nt(a, b):
    return lax.dot_general(a, b, (((1,), (1,)), ((), ())), preferred_element_type=f32)


def _dot_tn(a, b):
    return lax.dot_general(a, b, (((0,), (0,)), ((), ())), preferred_element_type=f32)


def _bdot(a, b):
    return lax.dot_general(a, b, (((2,), (1,)), ((0,), (0,))), preferred_element_type=f32)


def _bdot_nt(a, b):
    return lax.dot_general(a, b, (((2,), (2,)), ((0,), (0,))), preferred_element_type=f32)


def _bdot_tn(a, b):
    return lax.dot_general(a, b, (((1,), (1,)), ((0,), (0,))), preferred_element_type=f32)


def _bdot_split(a, b):
    ah, bh = _b(a), _b(b)
    al, bl = _b(a - ah.astype(f32)), _b(b - bh.astype(f32))
    return _bdot(ah, bh) + (_bdot(ah, bl) + _bdot(al, bh))


def _b(x):
    return x.astype(bf16)


def _chunk_cumsum(x, rows_in_chunk):
    n = x.shape[0]
    for s in (1, 2, 4, 8, 16, 32):
        x = x + jnp.where(rows_in_chunk >= s, pltpu.roll(x, s, axis=0), 0.0)
    return x


def _chunk_rev_cumsum(x, rows_in_chunk):
    n = x.shape[0]
    for s in (1, 2, 4, 8, 16, 32):
        x = x + jnp.where(rows_in_chunk + s < L, pltpu.roll(x, n - s, axis=0), 0.0)
    return x


def _shift_down(x, s):
    return pltpu.roll(x, s, axis=0) if s else x


def _shift_up(x, s):
    return pltpu.roll(x, x.shape[0] - s, axis=0) if s else x


def inproj_fwd(x, nw, w):
    T = x.shape[0]
    tT, tn = min(2048, T), NP_TILE // 2

    def body(x_ref, nw_ref, w_ref, p_ref, h_ref, hs):
        @pl.when(pl.program_id(1) == 0)
        def _():
            xv = x_ref[...]
            r = lax.rsqrt(jnp.mean(xv * xv, axis=-1, keepdims=True) + NORM_EPS)
            hv = _b(xv * r * nw_ref[...])
            hs[...] = hv
            h_ref[...] = hv
        p_ref[...] = _dot(hs[...], w_ref[...])

    return pl.pallas_call(
        body, name="inproj_fwd", grid=(T // tT, NP // tn),
        in_specs=[pl.BlockSpec((tT, D), lambda i, j: (i, 0)), pl.BlockSpec((1, D), lambda i, j: (0, 0)),
                  pl.BlockSpec((D, tn), lambda i, j: (0, j))],
        out_specs=[pl.BlockSpec((tT, tn), lambda i, j: (i, j)), pl.BlockSpec((tT, D), lambda i, j: (i, 0))],
        out_shape=[jax.ShapeDtypeStruct((T, NP), f32), jax.ShapeDtypeStruct((T, D), bf16)],
        scratch_shapes=[pltpu.VMEM((tT, D), bf16)],
        compiler_params=_cp(("parallel", "arbitrary")),
    )(x, nw, w)


def matmul_tn(a, b, name, n=None, b_col0=0, with_bf16=False):
    T, K = a.shape
    N = b.shape[1] if n is None else n
    tT = min(2048, T)
    tn = NP_TILE if N % NP_TILE == 0 else min(N, 1024)
    nt = T // tT
    cb0 = b_col0 // tn

    def body(a_ref, b_ref, o_ref, *ob_ref):
        @pl.when(pl.program_id(1) == 0)
        def _():
            o_ref[...] = jnp.zeros_like(o_ref)
        o_ref[...] += _dot_tn(_b(a_ref[...]), _b(b_ref[...]))
        if with_bf16:
            @pl.when(pl.program_id(1) == nt - 1)
            def _():
                ob_ref[0][...] = _b(o_ref[...])

    ospec = pl.BlockSpec((K, tn), lambda j, t: (0, j))
    return pl.pallas_call(
        body, name=name, grid=(N // tn, nt),
        in_specs=[pl.BlockSpec((tT, K), lambda j, t: (t, 0)), pl.BlockSpec((tT, tn), lambda j, t: (t, cb0 + j))],
        out_specs=[ospec, ospec] if with_bf16 else ospec,
        out_shape=([jax.ShapeDtypeStruct((K, N), f32), jax.ShapeDtypeStruct((K, N), bf16)] if with_bf16
                   else jax.ShapeDtypeStruct((K, N), f32)),
        compiler_params=_cp(("parallel", "arbitrary")),
    )(a, b)


def inproj_bwd(dp, w, x, nw, dres):
    T = x.shape[0]
    tT, tk = min(1024, T), NP // 4
    nk = NP // tk

    def body(dp_ref, w_ref, x_ref, nw_ref, dres_ref, dx_ref, dnw_ref, acc):
        i, k = pl.program_id(0), pl.program_id(1)

        @pl.when((i == 0) & (k == 0))
        def _():
            dnw_ref[...] = jnp.zeros_like(dnw_ref)

        @pl.when(k == 0)
        def _():
            acc[...] = jnp.zeros_like(acc)
        acc[...] += _dot_nt(dp_ref[...], w_ref[...])

        @pl.when(k == nk - 1)
        def _():
            xv = x_ref[...]
            r = lax.rsqrt(jnp.mean(xv * xv, axis=-1, keepdims=True) + NORM_EPS)
            xh = xv * r
            dy = acc[...]
            dyw = dy * nw_ref[...]
            dx_ref[...] = r * (dyw - xh * jnp.mean(dyw * xh, axis=-1, keepdims=True)) + dres_ref[...]
            dnw_ref[...] += jnp.sum(dy * xh, axis=0, keepdims=True)

    return pl.pallas_call(
        body, name="inproj_bwd", grid=(T // tT, nk),
        in_specs=[pl.BlockSpec((tT, tk), lambda i, k: (i, k)), pl.BlockSpec((D, tk), lambda i, k: (0, k)),
                  pl.BlockSpec((tT, D), lambda i, k: (i, 0)), pl.BlockSpec((1, D), lambda i, k: (0, 0)),
                  pl.BlockSpec((tT, D), lambda i, k: (i, 0))],
        out_specs=[pl.BlockSpec((tT, D), lambda i, k: (i, 0)), pl.BlockSpec((1, D), lambda i, k: (0, 0))],
        out_shape=[jax.ShapeDtypeStruct((T, D), f32), jax.ShapeDtypeStruct((1, D), f32)],
        scratch_shapes=[pltpu.VMEM((tT, D), f32)],
        compiler_params=_cp(("arbitrary", "arbitrary")),
    )(dp, w, x, nw, dres)


def loss_head(x, fw, tgt):
    T = x.shape[0]
    tT = min(512, T)

    def body(x_ref, fw_ref, t_ref, loss_ref, dx_ref, dfw_ref):
        @pl.when(pl.program_id(0) == 0)
        def _():
            loss_ref[...] = jnp.zeros_like(loss_ref)
            dfw_ref[...] = jnp.zeros_like(dfw_ref)
        xv = x_ref[...]
        r = lax.rsqrt(jnp.mean(xv * xv, axis=-1, keepdims=True) + NORM_EPS)
        xh = xv * r
        err = xh * fw_ref[...] - t_ref[...]
        part = 0.5 * jnp.sum(jnp.mean(err * err, axis=-1, keepdims=True), axis=0, keepdims=True)
        loss_ref[...] += jnp.broadcast_to(part, loss_ref.shape)
        dy = err * (1.0 / D)
        dyw = dy * fw_ref[...]
        dx_ref[...] = r * (dyw - xh * jnp.mean(dyw * xh, axis=-1, keepdims=True))
        dfw_ref[...] += jnp.sum(dy * xh, axis=0, keepdims=True)

    return pl.pallas_call(
        body, name="loss_head", grid=(T // tT,),
        in_specs=[pl.BlockSpec((tT, D), lambda i: (i, 0)), pl.BlockSpec((1, D), lambda i: (0, 0)),
                  pl.BlockSpec((tT, D), lambda i: (i, 0))],
        out_specs=[pl.BlockSpec((1, 128), lambda i: (0, 0)), pl.BlockSpec((tT, D), lambda i: (i, 0)),
                   pl.BlockSpec((1, D), lambda i: (0, 0))],
        out_shape=[jax.ShapeDtypeStruct((1, 128), f32), jax.ShapeDtypeStruct((T, D), f32),
                   jax.ShapeDtypeStruct((1, D), f32)],
        compiler_params=_cp(("arbitrary",)),
    )(x, fw, tgt)


def _halo_specs(tT, T, width, colblk):
    nb8 = T // 8
    per = tT // 8
    prev = pl.BlockSpec((8, width), lambda i: (jnp.maximum(i * per - 1, 0), colblk))
    nxt = pl.BlockSpec((8, width), lambda i: (jnp.minimum((i + 1) * per, nb8 - 1), colblk))
    return prev, nxt


def mixa_fwd(p, cw):
    T = p.shape[0]
    tT = min(512, T)
    prev_spec, _ = _halo_specs(tT, T, REG_A, OFF_A // REG_A)

    def body(p_ref, pp_ref, cw_ref, y_ref):
        pv = p_ref[...]
        u = pv[:, 512:1024] * pv[:, 1024:1536]
        pp = pp_ref[...]
        up = jnp.where(pl.program_id(0) == 0, 0.0, pp[:, 512:1024] * pp[:, 1024:1536])
        ue = jnp.concatenate([up, u], axis=0)
        cv = cw_ref[0:1, :] * _shift_down(ue, 2) + cw_ref[1:2, :] * _shift_down(ue, 1) + cw_ref[2:3, :] * ue
        y_ref[...] = _b(pv[:, 0:512] * cv[8:] * _silu(pv[:, 1536:2048]))

    return pl.pallas_call(
        body, name="mixa_fwd", grid=(T // tT,),
        in_specs=[pl.BlockSpec((tT, REG_A), lambda i: (i, OFF_A // REG_A)), prev_spec,
                  pl.BlockSpec((3, 512), lambda i: (0, 0))],
        out_specs=pl.BlockSpec((tT, 512), lambda i: (i, 0)),
        out_shape=jax.ShapeDtypeStruct((T, 512), bf16),
        compiler_params=_cp(("parallel",)),
    )(p, p, cw)


def mixa_bwd(p, cw, dy, dp):
    T = p.shape[0]
    tT = min(512, T)
    nt = T // tT
    prev_spec, next_spec = _halo_specs(tT, T, REG_A, OFF_A // REG_A)
    _, dnext_spec = _halo_specs(tT, T, 512, 0)

    def body(p_ref, pp_ref, pn_ref, cw_ref, dy_ref, dyn_ref, dp_in, dp_ref, dcw_ref):
        i = pl.program_id(0)

        @pl.when(i == 0)
        def _():
            dcw_ref[...] = jnp.zeros_like(dcw_ref)
        pv, pp, pn = p_ref[:, 0:2048], pp_ref[:, 0:2048], pn_ref[:, 0:2048]
        pe = jnp.concatenate([pp, pv, pn], axis=0)
        rows = lax.broadcasted_iota(jnp.int32, (tT + 16, 1), 0)
        ab, ac, ax, az = pe[:, 0:512], pe[:, 512:1024], pe[:, 1024:1536], pe[:, 1536:2048]
        u = jnp.where((rows < 8) & (i == 0), 0.0, ac * ax)
        u1, u2 = _shift_down(u, 1), _shift_down(u, 2)
        w0, w1, w2 = cw_ref[0:1, :], cw_ref[1:2, :], cw_ref[2:3, :]
        cv = w0 * u2 + w1 * u1 + w2 * u
        dye = jnp.concatenate([jnp.zeros((8, 512), f32), dy_ref[...], dyn_ref[...]], axis=0)
        dye = jnp.where((rows >= tT + 8) & (i == nt - 1), 0.0, dye)
        sz, dsz = _silu2(az)
        dcv = dye * ab * sz
        du = w2 * dcv + w1 * _shift_up(dcv, 1) + w0 * _shift_up(dcv, 2)
        inner = (rows >= 8) & (rows < tT + 8)
        dcv_in = jnp.where(inner, dcv, 0.0)
        dcw_ref[0:1, :] += jnp.sum(dcv_in * u2, axis=0, keepdims=True)
        dcw_ref[1:2, :] += jnp.sum(dcv_in * u1, axis=0, keepdims=True)
        dcw_ref[2:3, :] += jnp.sum(dcv_in * u, axis=0, keepdims=True)
        sl = slice(8, tT + 8)
        dp_ref[:, 0:512] = _b((dye * cv * sz)[sl])
        dp_ref[:, 512:1024] = _b((du * ax)[sl])
        dp_ref[:, 1024:1536] = _b((du * ac)[sl])
        dp_ref[:, 1536:2048] = _b((dye * ab * cv * dsz)[sl])
        dp_ref[:, 2048:] = jnp.zeros((tT, REG_A - 2048), bf16)

    return pl.pallas_call(
        body, name="mixa_bwd", grid=(nt,),
        in_specs=[pl.BlockSpec((tT, REG_A), lambda i: (i, OFF_A // REG_A)), prev_spec, next_spec,
                  pl.BlockSpec((3, 512), lambda i: (0, 0)),
                  pl.BlockSpec((tT, 512), lambda i: (i, 0)), dnext_spec, pl.BlockSpec(memory_space=pl.ANY)],
        out_specs=[pl.BlockSpec((tT, REG_A), lambda i: (i, OFF_A // REG_A)), pl.BlockSpec((8, 512), lambda i: (0, 0))],
        out_shape=[jax.ShapeDtypeStruct((T, NP), bf16), jax.ShapeDtypeStruct((8, 512), f32)],
        input_output_aliases={6: 0},
        compiler_params=_cp(("arbitrary",)),
    )(p, p, p, cw, dy, dy, dp)


def _l2n_fwd(y):
    return y * lax.rsqrt(jnp.sum(y * y, axis=-1, keepdims=True) + L2_EPS)


def mixc_pre_fwd(p, cw, alog_l, dtb_l):
    T = p.shape[0]
    tT = min(512, T)
    prev_spec, _ = _halo_specs(tT, T, REG_C, OFF_C // REG_C)

    def body(p_ref, pp_ref, cw_ref, al_ref, dt_ref, q_ref, k_ref, v_ref, sm_ref):
        pp = jnp.where(pl.program_id(0) == 0, 0.0, pp_ref[:, 0:2048])
        xe = jnp.concatenate([pp, p_ref[:, 0:2048]], axis=0)
        cv = (cw_ref[0:1, :] * _shift_down(xe, 3) + cw_ref[1:2, :] * _shift_down(xe, 2)
              + cw_ref[2:3, :] * _shift_down(xe, 1) + cw_ref[3:4, :] * xe)[8:]
        y = _silu(cv)
        for hh in range(4):
            sl = slice(hh * HK, (hh + 1) * HK)
            q_ref[:, sl] = _l2n_fwd(y[:, sl]) * QK_SCALE
            k_ref[:, sl] = _l2n_fwd(y[:, 512 + hh * HK:512 + (hh + 1) * HK])
        v_ref[...] = y[:, 1024:2048]
        ps = p_ref[:, 2048:2176]
        lane = lax.broadcasted_iota(jnp.int32, ps.shape, 1)
        la = -jnp.exp(al_ref[...]) * _softplus(ps + dt_ref[...])
        rin = lax.broadcasted_iota(jnp.int32, ps.shape, 0) % L
        g = _chunk_cumsum(la, rin)
        sm_ref[...] = jnp.where(lane < 8, _sigmoid(ps), jnp.where(lane < 16, g, 0.0))

    return pl.pallas_call(
        body, name="mixc_pre_fwd", grid=(T // tT,),
        in_specs=[pl.BlockSpec((tT, REG_C), lambda i: (i, OFF_C // REG_C)), prev_spec,
                  pl.BlockSpec((4, 2048), lambda i: (0, 0)),
                  pl.BlockSpec((1, 128), lambda i: (0, 0)), pl.BlockSpec((1, 128), lambda i: (0, 0))],
        out_specs=[pl.BlockSpec((tT, 512), lambda i: (i, 0)), pl.BlockSpec((tT, 512), lambda i: (i, 0)),
                   pl.BlockSpec((tT, 1024), lambda i: (i, 0)), pl.BlockSpec((tT, 128), lambda i: (i, 0))],
        out_shape=[jax.ShapeDtypeStruct((T, 512), f32), jax.ShapeDtypeStruct((T, 512), f32),
                   jax.ShapeDtypeStruct((T, 1024), f32), jax.ShapeDtypeStruct((T, 128), f32)],
        compiler_params=_cp(("parallel",)),
    )(p, p, cw, alog_l, dtb_l)


def mixc_pre_bwd(p, cw, alog_l, dtb_l, dq8, dk8, dv, dsm8, dp):
    T = p.shape[0]
    tT = min(256, T)
    nt = T // tT
    prev_spec, next_spec = _halo_specs(tT, T, REG_C, OFF_C // REG_C)
    _, n1024 = _halo_specs(tT, T, 1024, 0)

    def body(p_ref, pp_ref, pn_ref, cw_ref, al_ref, dt_ref, dq_ref, dqn_ref, dk_ref, dkn_ref,
             dv_ref, dvn_ref, dsm_ref, dp_in, dp_ref, dcw_ref, dsml_ref):
        i = pl.program_id(0)

        @pl.when(i == 0)
        def _():
            dcw_ref[...] = jnp.zeros_like(dcw_ref)
            dsml_ref[...] = jnp.zeros_like(dsml_ref)
        rows = lax.broadcasted_iota(jnp.int32, (tT + 16, 1), 0)
        pp = jnp.where(i == 0, 0.0, pp_ref[:, 0:2048])
        xe = jnp.concatenate([pp, p_ref[:, 0:2048], pn_ref[:, 0:2048]], axis=0)
        xs = [_shift_down(xe, 3), _shift_down(xe, 2), _shift_down(xe, 1), xe]
        cv = cw_ref[0:1, :] * xs[0] + cw_ref[1:2, :] * xs[1] + cw_ref[2:3, :] * xs[2] + cw_ref[3:4, :] * xs[3]
        y, dy_dcv = _silu2(cv)
        last = (rows >= tT + 8) & (i == nt - 1)
        z8q = jnp.zeros((8, 1024), f32)

        def ext(cur_ref, nxt_ref):
            return jnp.where(last, 0.0, jnp.concatenate([z8q, cur_ref[...], nxt_ref[...]], axis=0))
        dq8e, dk8e, dve = ext(dq_ref, dqn_ref), ext(dk_ref, dkn_ref), ext(dv_ref, dvn_ref)
        dys = []
        for (d8, base, scale) in ((dq8e, 0, QK_SCALE), (dk8e, 512, 1.0)):
            for hh in range(4):
                dn = (d8[:, (2 * hh) * HK:(2 * hh + 1) * HK] + d8[:, (2 * hh + 1) * HK:(2 * hh + 2) * HK]) * scale
                yh = y[:, base + hh * HK:base + (hh + 1) * HK]
                r = lax.rsqrt(jnp.sum(yh * yh, axis=-1, keepdims=True) + L2_EPS)
                nh = yh * r
                dys.append(r * (dn - nh * jnp.sum(dn * nh, axis=-1, keepdims=True)))
        dyy = jnp.concatenate(dys + [dve], axis=1)
        dcv = dyy * dy_dcv
        dx = (cw_ref[3:4, :] * dcv + cw_ref[2:3, :] * _shift_up(dcv, 1) + cw_ref[1:2, :] * _shift_up(dcv, 2)
              + cw_ref[0:1, :] * _shift_up(dcv, 3))
        dp_ref[:, 0:2048] = _b(dx[8:tT + 8])
        dp_ref[:, 2176:] = jnp.zeros((tT, REG_C - 2176), bf16)
        inner = (rows >= 8) & (rows < tT + 8)
        dcv_in = jnp.where(inner, dcv, 0.0)
        for j in range(4):
            dcw_ref[j:j + 1, :] += jnp.sum(dcv_in * xs[j], axis=0, keepdims=True)
        ps = p_ref[:, 2048:2176]
        lane = lax.broadcasted_iota(jnp.int32, ps.shape, 1)
        dsm = dsm_ref[:, 0:128]
        for hh in range(1, N_GDN):
            dsm = dsm + dsm_ref[:, hh * 128:(hh + 1) * 128]
        beta = _sigmoid(ps)
        xa = ps + dt_ref[...]
        nea = -jnp.exp(al_ref[...])
        dpa = dsm * nea * _sigmoid(xa)
        dp_ref[:, 2048:2176] = _b(jnp.where(lane < 8, dsm * beta * (1.0 - beta), jnp.where(lane < 16, dpa, 0.0)))
        amask = (lane >= 8) & (lane < 16)
        dsml_ref[0:1, :] += jnp.sum(jnp.where(amask, dsm * nea * _softplus(xa), 0.0), axis=0, keepdims=True)
        dsml_ref[1:2, :] += jnp.sum(jnp.where(amask, dpa, 0.0), axis=0, keepdims=True)

    cur1024 = pl.BlockSpec((tT, 1024), lambda i: (i, 0))
    return pl.pallas_call(
        body, name="mixc_pre_bwd", grid=(nt,),
        in_specs=[pl.BlockSpec((tT, REG_C), lambda i: (i, OFF_C // REG_C)), prev_spec, next_spec,
                  pl.BlockSpec((4, 2048), lambda i: (0, 0)),
                  pl.BlockSpec((1, 128), lambda i: (0, 0)), pl.BlockSpec((1, 128), lambda i: (0, 0)),
                  cur1024, n1024, cur1024, n1024, cur1024, n1024, cur1024, pl.BlockSpec(memory_space=pl.ANY)],
        out_specs=[pl.BlockSpec((tT, REG_C), lambda i: (i, OFF_C // REG_C)),
                   pl.BlockSpec((8, 2048), lambda i: (0, 0)), pl.BlockSpec((8, 128), lambda i: (0, 0))],
        out_shape=[jax.ShapeDtypeStruct((T, NP), bf16),
                   jax.ShapeDtypeStruct((8, 2048), f32), jax.ShapeDtypeStruct((8, 128), f32)],
        input_output_aliases={13: 0},
        compiler_params=_cp(("arbitrary",)),
    )(p, p, p, cw, alog_l, dtb_l, dq8, dq8, dk8, dk8, dv, dv, dsm8, dp)


def _tri_inverse(m):
    r = lax.broadcasted_iota(jnp.int32, (L, L), 0)
    c = lax.broadcasted_iota(jnp.int32, (L, L), 1)
    eye = (r == c).astype(f32)[None]
    same = lambda w: ((r // w) == (c // w))[None]
    md = jnp.where(same(8), m, 0.0)
    m2 = _bdot_split(md, md)
    m4 = _bdot_split(m2, m2)
    t = _bdot_split(_bdot_split(eye - md, eye + m2), eye + m4)
    for w in (16, 32, 64):
        mo = jnp.where(same(w) & jnp.logical_not(same(w // 2)), m, 0.0)
        t = t - _bdot_split(_bdot_split(t, mo), t)
    return t


def _col_to_row(col, eye):
    return jnp.sum(eye * col, axis=1, keepdims=True)


def _row_to_col(row, eye):
    return jnp.sum(eye * row, axis=2, keepdims=True)


def _gdn_chunk_terms(q, k, v, beta, g, t_inv=None):
    r = lax.broadcasted_iota(jnp.int32, (L, L), 0)
    c = lax.broadcasted_iota(jnp.int32, (L, L), 1)
    eye = (r == c).astype(f32)[None]
    causal, strict = (c <= r)[None], (c < r)[None]
    diff = g - _col_to_row(g, eye)
    dec = jnp.exp(jnp.where(causal, diff, 0.0))
    dc = jnp.where(causal, dec, 0.0)
    ds = jnp.where(strict, dec, 0.0)
    eg = jnp.exp(g)
    gl = g[:, L - 1:L, :]
    egl = jnp.exp(gl - g)
    kb = k * beta
    kk = _bdot_nt(_b(k), _b(kb))
    qk = _bdot_nt(_b(q), _b(kb))
    m = kk * ds
    aqk = qk * dc
    if t_inv is None:
        t_inv = _tri_inverse(m)
    tb = _b(t_inv)
    keg = k * eg
    u = _bdot(tb, _b(v))
    w = _bdot(tb, _b(keg))
    ks = kb * egl
    ksw = _bdot_tn(_b(ks), _b(w))
    return dict(eye=eye, causal=causal, strict=strict, dc=dc, ds=ds, eg=eg, gl=gl, egl=egl, kb=kb, kk=kk, qk=qk,
                m=m, aqk=aqk, t=t_inv, u=u, w=w, qi=q * eg, ks=ks, keg=keg, ksw=ksw)


def gdn_fwd(qn, kn, vv, sm):
    T = qn.shape[0]
    tB = min(2 * GDN_BLOCK, T)
    nc = tB // L
    N = T // L

    def body(q_ref, k_ref, v_ref, sm_ref, o_ref, st_ref, ti_ref, s_scr):
        j = pl.program_id(0)

        @pl.when(pl.program_id(1) == 0)
        def _():
            s_scr[...] = jnp.zeros_like(s_scr)
        smv = sm_ref[...]
        lane = lax.broadcasted_iota(jnp.int32, smv.shape, 1)
        q = q_ref[...].reshape(nc, L, HK)
        k = k_ref[...].reshape(nc, L, HK)
        tms, ksus, kswbs, egls = [], [], [], []
        for a in range(2):
            h = 2 * j + a
            beta = jnp.sum(jnp.where(lane == h, smv, 0.0), axis=1, keepdims=True).reshape(nc, L, 1)
            g = jnp.sum(jnp.where(lane == 8 + h, smv, 0.0), axis=1, keepdims=True).reshape(nc, L, 1)
            v = v_ref[:, a * HK:(a + 1) * HK].reshape(nc, L, HK)
            tm = _gdn_chunk_terms(q, k, v, beta, g)
            ti_ref[a] = tm["t"]
            tms.append(tm)
            ksus.append(_bdot_tn(_b(tm["ks"]), _b(tm["u"])))
            kswbs.append(_b(tm["ksw"]))
            egls.append(jnp.exp(tm["gl"]))
        s = [s_scr[0], s_scr[1]]
        states = [[None] * nc, [None] * nc]
        for ci in range(nc):
            for a in range(2):
                states[a][ci] = s[a]
                s[a] = egls[a][ci] * s[a] + (ksus[a][ci] - _dot(kswbs[a][ci], _b(s[a])))
        for a in range(2):
            s_scr[a] = s[a]
            sall = jnp.stack(states[a], axis=0)
            st_ref[a] = sall
            sb = _b(sall)
            tm = tms[a]
            e = tm["u"] - _bdot(_b(tm["w"]), sb)
            o = _bdot(_b(tm["qi"]), sb) + _bdot(_b(tm["aqk"]), _b(e))
            o_ref[:, a * HK:(a + 1) * HK] = o.reshape(tB, HK)

    return pl.pallas_call(
        body, name="gdn_fwd", grid=(N_GDN // 2, T // tB),
        in_specs=[pl.BlockSpec((tB, HK), lambda j, n: (n, j)), pl.BlockSpec((tB, HK), lambda j, n: (n, j)),
                  pl.BlockSpec((tB, 2 * HK), lambda j, n: (n, j)), pl.BlockSpec((tB, 128), lambda j, n: (n, 0))],
        out_specs=[pl.BlockSpec((tB, 2 * HK), lambda j, n: (n, j)),
                   pl.BlockSpec((2, nc, HK, HK), lambda j, n: (j, n, 0, 0)),
                   pl.BlockSpec((2, nc, L, L), lambda j, n: (j, n, 0, 0))],
        out_shape=[jax.ShapeDtypeStruct((T, N_GDN * HK), f32), jax.ShapeDtypeStruct((N_GDN, N, HK, HK), f32),
                   jax.ShapeDtypeStruct((N_GDN, N, L, L), f32)],
        scratch_shapes=[pltpu.VMEM((2, HK, HK), f32)],
        compiler_params=_cp(("parallel", "arbitrary")),
    )(qn, kn, vv, sm)


def gdn_bwd(qn, kn, vv, sm, st, ti, do):
    T = qn.shape[0]
    tB = min(GDN_BLOCK, T)
    nc = tB // L
    nb = T // tB

    def body(q_ref, k_ref, v_ref, sm_ref, st_ref, ti_ref, do_ref, dq_ref, dk_ref, dv_ref, dsm_ref, ds_scr):
        j = pl.program_id(0)

        @pl.when(pl.program_id(1) == 0)
        def _():
            ds_scr[...] = jnp.zeros_like(ds_scr)
        smv = sm_ref[...]
        lane = lax.broadcasted_iota(jnp.int32, smv.shape, 1)
        q = q_ref[...].reshape(nc, L, HK)
        k = k_ref[...].reshape(nc, L, HK)
        kbf, qbf = _b(k), _b(q)
        heads = []
        for a in range(2):
            h = 2 * j + a
            beta = jnp.sum(jnp.where(lane == h, smv, 0.0), axis=1, keepdims=True).reshape(nc, L, 1)
            g = jnp.sum(jnp.where(lane == 8 + h, smv, 0.0), axis=1, keepdims=True).reshape(nc, L, 1)
            v = v_ref[:, a * HK:(a + 1) * HK].reshape(nc, L, HK)
            do = do_ref[:, a * HK:(a + 1) * HK].reshape(nc, L, HK)
            s = st_ref[a]
            tm = _gdn_chunk_terms(q, k, v, beta, g, t_inv=ti_ref[a])
            sb, dob = _b(s), _b(do)
            e = tm["u"] - _bdot(_b(tm["w"]), sb)
            de0 = _bdot_tn(_b(tm["aqk"]), dob)
            ds0 = _bdot_tn(_b(tm["qi"]), dob) - _bdot_tn(_b(tm["w"]), _b(de0))
            heads.append(dict(h=h, beta=beta, tm=tm, s=s, sb=sb, dob=dob, e=e, de0=de0, ds0=ds0, kswb=_b(tm["ksw"]),
                              egl_last=jnp.exp(tm["gl"])))
        dsn = [ds_scr[0], ds_scr[1]]
        dsns = [[None] * nc, [None] * nc]
        for ci in reversed(range(nc)):
            for a, hd in enumerate(heads):
                dsns[a][ci] = dsn[a]
                dsn[a] = hd["ds0"][ci] + (hd["egl_last"][ci] * dsn[a] - _dot_tn(hd["kswb"][ci], _b(dsn[a])))
        rowi = lax.broadcasted_iota(jnp.int32, (nc, L, 1), 1)
        rin = lax.broadcasted_iota(jnp.int32, (tB, 128), 0) % L
        for a, hd in enumerate(heads):
            ds_scr[a] = dsn[a]
            tm, s, sb, dob, e, beta, h = hd["tm"], hd["s"], hd["sb"], hd["dob"], hd["e"], hd["beta"], hd["h"]
            eye, dc, ds_, eg, egl = tm["eye"], tm["dc"], tm["ds"], tm["eg"], tm["egl"]
            kb, u, w, qi, ks = tm["kb"], tm["u"], tm["w"], tm["qi"], tm["ks"]
            eb = _b(e)
            dsp = jnp.stack(dsns[a], axis=0)
            dspb = _b(dsp)
            de = hd["de0"] + _bdot(_b(ks), dspb)
            deb = _b(de)
            dks = _bdot_nt(eb, dspb)
            dqi = _bdot_nt(dob, sb)
            daqk = jnp.where(tm["causal"], _bdot_nt(dob, eb), 0.0)
            dw = -_bdot_nt(deb, sb)
            tb = _b(tm["t"])
            dvv = _bdot_tn(tb, deb)
            dkg = _bdot_tn(tb, _b(dw))
            dm = -jnp.where(tm["strict"], _bdot_nt(_b(dvv), _b(u)) + _bdot_nt(_b(dkg), _b(w)), 0.0)
            x = _b(dm * ds_)
            y = _b(daqk * dc)
            kbb = _b(kb)
            dk = _bdot(x, kbb) + dkg * eg
            dkb = _bdot_tn(x, kbf) + _bdot_tn(y, qbf) + dks * egl
            dq = _bdot(y, kbb) + dqi * eg
            dk = dk + dkb * beta
            dbeta = jnp.sum(dkb * k, axis=-1, keepdims=True)
            z = dm * tm["m"] + daqk * tm["aqk"]
            dg = (jnp.sum(dqi * qi - dks * ks + dkg * tm["keg"], axis=-1, keepdims=True)
                  + jnp.sum(z, axis=-1, keepdims=True) - _row_to_col(jnp.sum(z, axis=1, keepdims=True), eye))
            dgl = (hd["egl_last"] * jnp.sum(jnp.sum(s * dsp, axis=2, keepdims=True), axis=1, keepdims=True)
                   + jnp.sum(jnp.sum(dks * ks, axis=2, keepdims=True), axis=1, keepdims=True))
            dg = dg + jnp.where(rowi == L - 1, dgl, 0.0)
            dla = _chunk_rev_cumsum(jnp.broadcast_to(dg.reshape(tB, 1), (tB, 128)), rin)
            sl = slice(a * HK, (a + 1) * HK)
            dq_ref[:, sl] = dq.reshape(tB, HK)
            dk_ref[:, sl] = dk.reshape(tB, HK)
            dv_ref[:, sl] = dvv.reshape(tB, HK)
            dsm_ref[:, sl] = jnp.where(lane == h, dbeta.reshape(tB, 1), jnp.where(lane == 8 + h, dla, 0.0))

    rev = lambda n: nb - 1 - n
    pair = pl.BlockSpec((tB, 2 * HK), lambda j, n: (rev(n), j))
    return pl.pallas_call(
        body, name="gdn_bwd", grid=(N_GDN // 2, nb),
        in_specs=[pl.BlockSpec((tB, HK), lambda j, n: (rev(n), j)), pl.BlockSpec((tB, HK), lambda j, n: (rev(n), j)),
                  pair, pl.BlockSpec((tB, 128), lambda j, n: (rev(n), 0)),
                  pl.BlockSpec((2, nc, HK, HK), lambda j, n: (j, rev(n), 0, 0)),
                  pl.BlockSpec((2, nc, L, L), lambda j, n: (j, rev(n), 0, 0)), pair],
        out_specs=[pair] * 4,
        out_shape=[jax.ShapeDtypeStruct((T, N_GDN * HK), f32)] * 4,
        scratch_shapes=[pltpu.VMEM((2, HK, HK), f32)],
        compiler_params=_cp(("parallel", "arbitrary")),
    )(qn, kn, vv, sm, st, ti, do)


def _hgrn_prep(bq, bf_, bi, lb):
    tB = bq.shape[0]
    sq, dsq = _silu2(bq)
    sg = _sigmoid(bf_)
    f = lb + (1.0 - lb) * sg
    logf = jnp.log(jnp.maximum(f, MIN_F))
    rin = lax.broadcasted_iota(jnp.int32, (tB, HK), 0) % L
    g = _chunk_cumsum(logf, rin)
    return sq * QK_SCALE, sg, f, 1.0 - f, bi, g, rin, dsq * QK_SCALE


def _hgrn_intra(q, kk, v, g, do=None):
    n = q.shape[0]
    nsub = L // SUB
    bwd = do is not None
    o_rows = [None] * nsub
    if bwd:
        dq_rows = [None] * nsub
        dkk_acc = jnp.zeros_like(kk)
        dv_acc = jnp.zeros_like(v)
    for i in range(1, nsub):
        lo, hi, w = i * SUB, (i + 1) * SUB, i * SUB
        ref = g[:, lo - 1:lo, :]
        eq = jnp.exp(g[:, lo:hi, :] - ref)
        ek = jnp.exp(ref - g[:, :w, :])
        qs = _b(q[:, lo:hi, :] * eq)
        ks = _b(kk[:, :w, :] * ek)
        p = _bdot_nt(qs, ks)
        o_rows[i] = _bdot(_b(p), _b(v[:, :w, :]))
        if bwd:
            dob = _b(do[:, lo:hi, :])
            dp = _b(_bdot_nt(dob, _b(v[:, :w, :])))
            dq_rows[i] = _bdot(dp, ks) * eq
            pad = jnp.zeros((n, L - w, HK), f32)
            dkk_acc = dkk_acc + jnp.concatenate([_bdot_tn(dp, qs) * ek, pad], axis=1)
            dv_acc = dv_acc + jnp.concatenate([_bdot_tn(_b(p), dob), pad], axis=1)
    m = n * nsub
    q4, k4, v4, g4 = (a.reshape(m, SUB, HK) for a in (q, kk, v, g))
    r = lax.broadcasted_iota(jnp.int32, (m, SUB, HK), 1)
    od = jnp.zeros((m, SUB, HK), f32)
    if bwd:
        do4 = do.reshape(m, SUB, HK)
        dqd = jnp.zeros((m, SUB, HK), f32)
        dkd = jnp.zeros((m, SUB, HK), f32)
        dvd = jnp.zeros((m, SUB, HK), f32)
    for j in range(SUB):
        gj, kj, vj = g4[:, j:j + 1, :], k4[:, j:j + 1, :], v4[:, j:j + 1, :]
        ok = r >= j
        e = jnp.where(ok, jnp.exp(g4 - gj), 0.0)
        xq = q4 * e
        pj = jnp.sum(xq * kj, axis=-1, keepdims=True)
        od = od + pj * vj
        if bwd:
            dpj = jnp.sum(do4 * vj, axis=-1, keepdims=True)
            dqd = dqd + dpj * kj * e
            dkd = dkd + jnp.where(r == j, jnp.sum(dpj * xq, axis=1, keepdims=True), 0.0)
            dvd = dvd + jnp.where(r == j, jnp.sum(pj * do4, axis=1, keepdims=True), 0.0)
    od = od.reshape(n, L, HK)
    o = jnp.concatenate([od[:, :SUB, :]] + [od[:, i * SUB:(i + 1) * SUB, :] + o_rows[i] for i in range(1, nsub)], axis=1)
    if not bwd:
        return o
    dqd = dqd.reshape(n, L, HK)
    dq = jnp.concatenate([dqd[:, :SUB, :]] + [dqd[:, i * SUB:(i + 1) * SUB, :] + dq_rows[i] for i in range(1, nsub)], axis=1)
    return o, dq, dkk_acc + dkd.reshape(n, L, HK), dv_acc + dvd.reshape(n, L, HK)


def hgrn_fwd(p, lbs):
    T = p.shape[0]
    tB = min(1024, T)
    nc = tB // L
    N = T // L

    def body(b_ref, lb_ref, o_ref, st_ref, s_scr):
        @pl.when(pl.program_id(1) == 0)
        def _():
            s_scr[...] = jnp.zeros_like(s_scr)
        q, sg, f, kk, v, g, rin, _ = _hgrn_prep(b_ref[:, 0:HK], b_ref[:, HK:2 * HK], b_ref[:, 2 * HK:3 * HK], lb_ref[...])
        q3, k3, v3, g3 = (a.reshape(nc, L, HK) for a in (q, kk, v, g))
        o = _hgrn_intra(q3, k3, v3, g3)
        gl = g3[:, L - 1:L, :]
        qt = _b(q3 * jnp.exp(g3))
        kt = _b(k3 * jnp.exp(gl - g3))
        vb = _b(v3)
        st = s_scr[...]
        for c in range(nc):
            st_ref[c] = st
            o_ref[c * L:(c + 1) * L, :] = o[c] + _dot_nt(qt[c], _b(st))
            st = st * jnp.exp(gl[c]) + _dot_tn(vb[c], kt[c])
        s_scr[...] = st

    return pl.pallas_call(
        body, name="hgrn_fwd", grid=(N_HGRN, T // tB),
        in_specs=[pl.BlockSpec((tB, REG_BH), lambda h, n: (n, OFF_B // REG_BH + h)),
                  pl.BlockSpec((1, HK), lambda h, n: (0, h))],
        out_specs=[pl.BlockSpec((tB, HK), lambda h, n: (n, h)),
                   pl.BlockSpec((None, nc, HK, HK), lambda h, n: (h, n, 0, 0))],
        out_shape=[jax.ShapeDtypeStruct((T, N_HGRN * HK), f32), jax.ShapeDtypeStruct((N_HGRN, N, HK, HK), f32)],
        scratch_shapes=[pltpu.VMEM((HK, HK), f32)],
        compiler_params=_cp(("parallel", "arbitrary")),
    )(p, lbs)


def hgrn_bwd(p, lbs, st, do, dp):
    T = p.shape[0]
    tB = min(256, T)
    nc = tB // L
    nb = T // tB

    def body(b_ref, lb_ref, st_ref, do_ref, dp_in, dp_ref, dlb_ref, ds_scr):
        @pl.when(pl.program_id(1) == 0)
        def _():
            ds_scr[...] = jnp.zeros_like(ds_scr)
            dlb_ref[...] = jnp.zeros_like(dlb_ref)
        lb = lb_ref[...]
        bq = b_ref[:, 0:HK]
        q, sg, f, kk, v, g, rin, dq_dbq = _hgrn_prep(bq, b_ref[:, HK:2 * HK], b_ref[:, 2 * HK:3 * HK], lb)
        q3, k3, v3, g3 = (a.reshape(nc, L, HK) for a in (q, kk, v, g))
        do3 = do_ref[...].reshape(nc, L, HK)
        dob = _b(do3)
        gl = g3[:, L - 1:L, :]
        egl = jnp.exp(gl)
        eg, egr = jnp.exp(g3), jnp.exp(gl - g3)
        qt, kt = q3 * eg, k3 * egr
        s = st_ref[...]
        ds0 = _bdot_tn(dob, _b(qt))
        dsn = ds_scr[...]
        dsns = [None] * nc
        for c in reversed(range(nc)):
            dsns[c] = dsn
            dsn = ds0[c] + dsn * egl[c]
        ds_scr[...] = dsn
        dsp = jnp.stack(dsns, axis=0)
        dspb = _b(dsp)
        dqt = _bdot(dob, _b(s))
        dkt = _bdot(_b(v3), dspb)
        dv_state = _bdot_nt(_b(kt), dspb)
        dgl = egl * jnp.sum(s * dsp, axis=1, keepdims=True) + jnp.sum(dkt * kt, axis=1, keepdims=True)
        _, dq_i, dkk_i, dv_i = _hgrn_intra(q3, k3, v3, g3, do=do3)
        dq = dq_i + dqt * eg
        dkk = dkk_i + dkt * egr
        dv = dv_i + dv_state
        rowi = lax.broadcasted_iota(jnp.int32, (nc, L, HK), 1)
        dg = q3 * dq - k3 * dkk + jnp.where(rowi == L - 1, dgl, 0.0)
        dlogf = _chunk_rev_cumsum(dg.reshape(tB, HK), rin)
        dkk2 = dkk.reshape(tB, HK)
        df = jnp.where(f > MIN_F, dlogf / f, 0.0) - dkk2
        dlb_ref[...] += jnp.sum(df * (1.0 - sg), axis=0, keepdims=True)
        dp_ref[:, 0:HK] = _b(dq.reshape(tB, HK) * dq_dbq)
        dp_ref[:, HK:2 * HK] = _b(df * (1.0 - lb) * sg * (1.0 - sg))
        dp_ref[:, 2 * HK:3 * HK] = _b(dv.reshape(tB, HK))

    rev = lambda n: nb - 1 - n
    return pl.pallas_call(
        body, name="hgrn_bwd", grid=(N_HGRN, nb),
        in_specs=[pl.BlockSpec((tB, REG_BH), lambda h, n: (rev(n), OFF_B // REG_BH + h)),
                  pl.BlockSpec((1, HK), lambda h, n: (0, h)),
                  pl.BlockSpec((None, nc, HK, HK), lambda h, n: (h, rev(n), 0, 0)),
                  pl.BlockSpec((tB, HK), lambda h, n: (rev(n), h)), pl.BlockSpec(memory_space=pl.ANY)],
        out_specs=[pl.BlockSpec((tB, REG_BH), lambda h, n: (rev(n), OFF_B // REG_BH + h)),
                   pl.BlockSpec((1, HK), lambda h, n: (0, h))],
        out_shape=[jax.ShapeDtypeStruct((T, NP), bf16), jax.ShapeDtypeStruct((1, N_HGRN * HK), f32)],
        input_output_aliases={4: 0},
        scratch_shapes=[pltpu.VMEM((HK, HK), f32)],
        compiler_params=_cp(("parallel", "arbitrary")),
    )(p, lbs, st, do, dp)


def _headnorm_fwd(o, z, w, nheads):
    outs, parts = [], []
    for hh in range(nheads):
        sl = slice(hh * HK, (hh + 1) * HK)
        oh = o[:, sl]
        r = lax.rsqrt(jnp.mean(oh * oh, axis=-1, keepdims=True) + NORM_EPS)
        on = oh * r
        sz, dsz = _silu2(z[:, sl])
        outs.append(on * w * sz)
        parts.append((r, on, sz, dsz))
    return jnp.concatenate(outs, axis=1), parts


def _headnorm_bwd(parts, w, dy):
    dos, dzs = [], []
    dw = jnp.zeros((1, HK), f32)
    for hh, (r, on, sz, dsz) in enumerate(parts):
        dyh = dy[:, hh * HK:(hh + 1) * HK]
        dn = dyh * sz * w
        dos.append(r * (dn - on * jnp.mean(dn * on, axis=-1, keepdims=True)))
        dzs.append(dyh * on * w * dsz)
        dw = dw + jnp.sum(dyh * sz * on, axis=0, keepdims=True)
    return jnp.concatenate(dos, axis=1), jnp.concatenate(dzs, axis=1), dw


def _merge_specs(tT, l):
    row = lambda w, cb=0: pl.BlockSpec((tT, w), lambda i, cb=cb: (i, cb))
    full = lambda r, c: pl.BlockSpec((r, c), lambda i: (0, 0))
    layer = lambda r, c: pl.BlockSpec((None, r, c), lambda i: (l, 0, 0))
    return row, full, layer


def merge_fwd(x, p, ya, ob, oc, hw, gw, bg, woa, wob, woc, wo, l):
    T = x.shape[0]
    tT = min(256, T)
    row, full, layer = _merge_specs(tT, l)

    def body(x_ref, pm_ref, ya_ref, ob_ref, oc_ref, hw_ref, gw_ref, bg_ref,
             woa_ref, wob_ref, woc_ref, wo_ref, out_ref):
        yb = _b(_headnorm_fwd(ob_ref[...], pm_ref[:, M_BZ:M_G], hw_ref[...], N_HGRN)[0])
        yc = _b(_headnorm_fwd(oc_ref[...], pm_ref[:, M_CZ:REG_M], gw_ref[...], N_GDN)[0])
        gates = _sigmoid(pm_ref[:, M_G:M_CZ] + bg_ref[...])
        merged = (gates[:, 0:D] * _dot(ya_ref[...], woa_ref[...]) + gates[:, D:2 * D] * _dot(yb, wob_ref[...])
                  + gates[:, 2 * D:3 * D] * _dot(yc, woc_ref[...]))
        out_ref[...] = x_ref[...] + _dot(_b(merged), wo_ref[...])

    return pl.pallas_call(
        body, name="merge_fwd", grid=(T // tT,),
        in_specs=[row(D), row(REG_M, OFF_M // REG_M),
                  row(512), row(512), row(1024), full(1, HK), full(1, HK), full(1, 3 * D),
                  layer(512, D), layer(512, D), layer(D, D), layer(D, D)],
        out_specs=row(D),
        out_shape=jax.ShapeDtypeStruct((T, D), f32),
        compiler_params=_cp(("parallel",)),
    )(x, p, ya, ob, oc, hw, gw, bg, woa, wob, woc, wo)


def merge_bwd(dxo, p, ya, ob, oc, hw, gw, bg, woa, wob, woc, wo, l):
    T = dxo.shape[0]
    tT = min(256, T)
    row, full, layer = _merge_specs(tT, l)

    def body(dx_ref, pm_ref, ya_ref, ob_ref, oc_ref, hw_ref, gw_ref, bg_ref,
             woa_ref, wob_ref, woc_ref, wo_ref,
             dya_ref, dob_ref, doc_ref, dp_ref, mg_ref, dy3_ref, yb_ref, yc_ref,
             dbg_ref, dhw_ref, dgw_ref):
        @pl.when(pl.program_id(0) == 0)
        def _():
            dbg_ref[...] = jnp.zeros_like(dbg_ref)
            dhw_ref[...] = jnp.zeros_like(dhw_ref)
            dgw_ref[...] = jnp.zeros_like(dgw_ref)
        ob, oc, bz, cz = ob_ref[...], oc_ref[...], pm_ref[:, M_BZ:M_G], pm_ref[:, M_CZ:REG_M]
        hw_, gw_ = hw_ref[...], gw_ref[...]
        yb, parts_b = _headnorm_fwd(ob, bz, hw_, N_HGRN)
        yc, parts_c = _headnorm_fwd(oc, cz, gw_, N_GDN)
        yb, yc = _b(yb), _b(yc)
        yb_ref[...] = yb
        yc_ref[...] = yc
        gates = _sigmoid(pm_ref[:, M_G:M_CZ] + bg_ref[...])
        ys = (_dot(ya_ref[...], woa_ref[...]), _dot(yb, wob_ref[...]), _dot(yc, woc_ref[...]))
        dmerged = _dot_nt(_b(dx_ref[...]), wo_ref[...])
        merged = jnp.zeros_like(dmerged)
        dys = []
        for i in range(3):
            gi = gates[:, i * D:(i + 1) * D]
            merged = merged + gi * ys[i]
            dyi = _b(dmerged * gi)
            dys.append(dyi)
            dy3_ref[:, i * D:(i + 1) * D] = dyi
            dgp = dmerged * ys[i] * gi * (1.0 - gi)
            dp_ref[:, M_G + i * D:M_G + (i + 1) * D] = _b(dgp)
            dbg_ref[:, i * D:(i + 1) * D] += jnp.sum(dgp, axis=0, keepdims=True)
        mg_ref[...] = _b(merged)
        dya_ref[...] = _dot_nt(dys[0], woa_ref[...])
        dob, dbz, dhw = _headnorm_bwd(parts_b, hw_, _dot_nt(dys[1], wob_ref[...]))
        doc, dcz, dgw = _headnorm_bwd(parts_c, gw_, _dot_nt(dys[2], woc_ref[...]))
        dob_ref[...] = dob
        doc_ref[...] = doc
        dp_ref[:, M_BZ:M_G] = _b(dbz)
        dp_ref[:, M_CZ:REG_M] = _b(dcz)
        dhw_ref[...] += dhw
        dgw_ref[...] += dgw

    sd = jax.ShapeDtypeStruct
    return pl.pallas_call(
        body, name="merge_bwd", grid=(T // tT,),
        in_specs=[row(D), row(REG_M, OFF_M // REG_M),
                  row(512), row(512), row(1024), full(1, HK), full(1, HK), full(1, 3 * D),
                  layer(512, D), layer(512, D), layer(D, D), layer(D, D)],
        out_specs=[row(512), row(512), row(1024), row(REG_M, OFF_M // REG_M), row(D), row(3 * D), row(512),
                   row(1024), full(1, 3 * D), full(1, HK), full(1, HK)],
        out_shape=[sd((T, 512), f32), sd((T, 512), f32), sd((T, 1024), f32), sd((T, NP), bf16),
                   sd((T, D), bf16), sd((T, 3 * D), bf16), sd((T, 512), bf16),
                   sd((T, 1024), bf16), sd((1, 3 * D), f32), sd((1, HK), f32), sd((1, HK), f32)],
        compiler_params=_cp(("arbitrary",)),
    )(dxo, p, ya, ob, oc, hw, gw, bg, woa, wob, woc, wo)


def layer_fwd(x, w):
    l = w["l"]
    p, h = inproj_fwd(x, w["norm_w"], w["w_in"])
    ya = mixa_fwd(p, w["conv_a"])
    qn, kn, vv, sm = mixc_pre_fwd(p, w["conv_c"], w["alog_l"], w["dtb_l"])
    oc, st_c, ti = gdn_fwd(qn, kn, vv, sm)
    ob, st_b = hgrn_fwd(p, w["lbs"])
    if "late" in w:
        w.update(w.pop("late")((ya, oc, ob)))
    xo = merge_fwd(x, p, ya, ob, oc, w["hgrn_norm_w"], w["gdn_norm_w"], w["b_gate"],
                   w["w_out_a"], w["w_out_b"], w["w_out_c"], w["w_o"], l)
    saved = dict(x=x, p=p, h=h, ya=ya, qn=qn, kn=kn, vv=vv, sm=sm, oc=oc, st_c=st_c, ti=ti, ob=ob, st_b=st_b)
    return xo, saved


OUT_MATS = (("w_out_a", "cols"), ("w_out_b", "cols"), ("w_out_c", "rows"), ("w_o", "rows"))


def layer_bwd(dxo, w, s, chip):
    p, l = s["p"], w["l"]
    (dya, dob, doc, dp, merged, dy3, yb, yc, dbg, dhw, dgw) = merge_bwd(
        dxo, p, s["ya"], s["ob"], s["oc"], w["hgrn_norm_w"], w["gdn_norm_w"], w["b_gate"],
        w["w_out_a"], w["w_out_b"], w["w_out_c"], w["w_o"], l)
    full = {"w_o": matmul_tn(merged, dxo, "dw_o", with_bf16=True),
            "w_out_a": matmul_tn(s["ya"], dy3, "dw_out_a", n=D, b_col0=0, with_bf16=True),
            "w_out_b": matmul_tn(yb, dy3, "dw_out_b", n=D, b_col0=D, with_bf16=True),
            "w_out_c": matmul_tn(yc, dy3, "dw_out_c", n=D, b_col0=2 * D, with_bf16=True)}
    out_kinds = [k for _, k in OUT_MATS]
    sent_out, token = exchange_start([full[n][1] for n, _ in OUT_MATS], out_kinds, f"grads_out_start{l}")
    dp, dlbs = hgrn_bwd(p, w["lbs"] + token[0:1, 0:1], s["st_b"], dob, dp)
    dq8, dk8, dvv, dsm8 = gdn_bwd(s["qn"], s["kn"], s["vv"], s["sm"], s["st_c"], s["ti"], doc)
    dp, dcc, dsmall = mixc_pre_bwd(p, w["conv_c"], w["alog_l"], w["dtb_l"], dq8, dk8, dvv, dsm8, dp)
    dp, dca = mixa_bwd(p, w["conv_a"], dya, dp)
    gf_win, gb_win = win_from_padded(matmul_tn(s["h"], dp, "dw_in"))
    sent_in, token = exchange_start([gb_win], ["slot"], f"grads_in_start{l}")
    dx, dnw = inproj_bwd(dp, w["w_in"], s["x"], w["norm_w"] + token[0:1, 0:1], dxo)
    recv_out = exchange_wait(sent_out, out_kinds, dx, f"grads_out_wait{l}")
    recv_in = exchange_wait(sent_in, ["slot"], dx, f"grads_in_wait{l}")
    half = {"w_in": partial_sum(gf_win, "slot", recv_in[0], chip, "psum_w_in", transposed=True)}
    for (n, kind), r in zip(OUT_MATS, recv_out):
        half[n] = partial_sum(full[n][0], kind, r, chip, "psum_" + n)
    small = dict(norm_w=dnw, b_gate=dbg, hgrn_norm_w=dhw, gdn_norm_w=dgw, lbs=dlbs, conv_a=dca[0:3], conv_c=dcc[0:4],
                 a_log=dsmall[0:1, 8:16], dt_bias=dsmall[1:2, 8:16])
    return dx, small, half


def lbs_fwd(lb):
    def body(lb_ref, o_ref):
        l0, l1 = lb_ref[0:1, :], lb_ref[1:2, :]
        mx = jnp.maximum(l0, l1)
        e0, e1 = jnp.exp(l0 - mx), jnp.exp(l1 - mx)
        o_ref[0:1, :] = jnp.zeros_like(l0)
        o_ref[1:2, :] = e1 / (e0 + e1)
    return pl.pallas_call(body, name="lbs_fwd", out_shape=jax.ShapeDtypeStruct(lb.shape, f32))(lb)


def _adam_math(w, g, m, v):
    mn = ADAM_B1 * m + (1.0 - ADAM_B1) * g
    vn = ADAM_B2 * v + (1.0 - ADAM_B2) * (g * g)
    mh = mn / (1.0 - ADAM_B1 ** ADAM_STEP)
    vh = vn / (1.0 - ADAM_B2 ** ADAM_STEP)
    return -ADAM_LR * (mh / (jnp.sqrt(vh) + ADAM_EPS) + ADAM_WD * w), mn, vn


def adam(w, g, m, v, name):
    R, C = w.shape
    tr = 256 if R % 256 == 0 else R

    def body(w_ref, g_ref, m_ref, v_ref, d_ref, mo_ref, vo_ref):
        d, mn, vn = _adam_math(w_ref[...], g_ref[...], m_ref[...], v_ref[...])
        d_ref[...] = d
        mo_ref[...] = mn
        vo_ref[...] = vn

    spec = pl.BlockSpec((tr, C), lambda i: (i, 0))
    return pl.pallas_call(
        body, name=name, grid=(R // tr,), in_specs=[spec] * 4, out_specs=[spec] * 3,
        out_shape=[jax.ShapeDtypeStruct((R, C), f32)] * 3, compiler_params=_cp(("parallel",)),
    )(w, g, m, v)


def adam_pair(h, hs, w, m, v, name):
    _, R, C = w.shape
    cp = h[0].shape[1]
    tr = 128 if R % 128 == 0 else R
    nt = R // tr

    def body(h0_ref, h1_ref, s0_ref, s1_ref, w_ref, m_ref, v_ref, g_ref, d_ref, mo_ref, vo_ref):
        def update(h_ref, s_ref):
            g = (h_ref[...] + s_ref[...])[:, :C]
            d, mn, vn = _adam_math(w_ref[...], g, m_ref[...], v_ref[...])
            g_ref[...] = g
            d_ref[...] = d
            mo_ref[...] = mn
            vo_ref[...] = vn

        @pl.when(pl.program_id(0) == 0)
        def _():
            update(h0_ref, s0_ref)

        @pl.when(pl.program_id(0) == 1)
        def _():
            update(h1_ref, s1_ref)

    h0spec = pl.BlockSpec((tr, cp), lambda l, i: (jnp.where(l == 0, i, nt - 1), 0))
    h1spec = pl.BlockSpec((tr, cp), lambda l, i: (jnp.where(l == 1, i, 0), 0))
    spec = pl.BlockSpec((None, tr, C), lambda l, i: (l, i, 0))
    return pl.pallas_call(
        body, name=name, grid=(2, nt), in_specs=[h0spec, h1spec, h0spec, h1spec, spec, spec, spec],
        out_specs=[spec] * 4, out_shape=[jax.ShapeDtypeStruct(w.shape, f32)] * 4,
        compiler_params=_cp(("arbitrary", "arbitrary")),
    )(h[0], h[1], hs[0], hs[1], w, m, v)


_SMALL = (("norm_w", 2 * D), ("b_gate", 6 * D), ("lower_bounds", None), ("hgrn_norm_w", 2 * HK),
          ("gdn_norm_w", 2 * HK), ("a_log", 16), ("dt_bias", 16), ("final_norm_w", D), ("loss", None))
_CONV = (("conv_a", 2 * 3 * 512), ("conv_c", 2 * 4 * 2048))


def _small_rows(n):
    return 16 if n is None else -(-n // 1024) * 8


LB_ROW = sum(_small_rows(n) for _, n in _SMALL[:2])
ADAM_ROWS = sum(_small_rows(n) for _, n in _SMALL)
SMALL_ROWS = ADAM_ROWS + sum(_small_rows(n) for _, n in _CONV)


def small_update(parts, wp, mp, vp):
    def body(p_ref, w_ref, m_ref, v_ref, g_ref, d_ref, mo_ref, vo_ref):
        gs = p_ref[0]
        for i in range(1, 8):
            gs = gs + p_ref[i]
        w = w_ref[...]
        l0, l1 = w[LB_ROW:LB_ROW + 8], w[LB_ROW + 8:LB_ROW + 16]
        mx = jnp.maximum(l0, l1)
        e0, e1 = jnp.exp(l0 - mx), jnp.exp(l1 - mx)
        p0, p1 = e0 / (e0 + e1), e1 / (e0 + e1)
        dl1 = gs[LB_ROW + 8:LB_ROW + 16]
        s = p1 * dl1
        g = jnp.concatenate([gs[0:LB_ROW], -p0 * s, p1 * dl1 - p1 * s, gs[LB_ROW + 16:ADAM_ROWS]], axis=0)
        d, mn, vn = _adam_math(w, g, m_ref[...], v_ref[...])
        g_ref[0:ADAM_ROWS, :] = g
        g_ref[ADAM_ROWS:, :] = gs[ADAM_ROWS:]
        d_ref[...] = d
        mo_ref[...] = mn
        vo_ref[...] = vn
    sd = jax.ShapeDtypeStruct
    return pl.pallas_call(body, name="small_update",
                          out_shape=[sd((SMALL_ROWS, 128), f32)] + [sd((ADAM_ROWS, 128), f32)] * 3)(parts, wp, mp, vp)


def partial_sum(own, kind, recv, chip, name, transposed=False):
    _, r, c = recv.shape
    tr = 256 if r % 256 == 0 else r

    def body(chip_ref, o_ref, r_ref, out_ref):
        s = ((o_ref[...] + r_ref[0].astype(f32)) + r_ref[1].astype(f32)) + r_ref[2].astype(f32)
        out_ref[...] = s.T if transposed else s

    own_spec = {"slot": pl.BlockSpec((None, tr, c), lambda i, chip: (chip[0], i, 0)),
                "cols": pl.BlockSpec((tr, c), lambda i, chip: (i, chip[0])),
                "rows": pl.BlockSpec((tr, c), lambda i, chip: (chip[0] * (r // tr) + i, 0))}[kind]
    out_spec = pl.BlockSpec((c, tr), lambda i, chip: (0, i)) if transposed else pl.BlockSpec((tr, c), lambda i, chip: (i, 0))
    return pl.pallas_call(
        body, name=name,
        grid_spec=pltpu.PrefetchScalarGridSpec(
            num_scalar_prefetch=1, grid=(r // tr,),
            in_specs=[own_spec, pl.BlockSpec((3, tr, c), lambda i, chip: (0, i, 0))], out_specs=out_spec),
        out_shape=jax.ShapeDtypeStruct((c, r) if transposed else (r, c), f32), compiler_params=_cp(("arbitrary",)),
    )(chip, own, recv)


def adam_pair_t(h, hs, wt, mt, vt, name):
    C, _, R = wt.shape
    tc = 128

    def body(h0_ref, h1_ref, s0_ref, s1_ref, w_ref, m_ref, v_ref, g_ref, d_ref, mo_ref, vo_ref):
        g = jnp.stack([h0_ref[...] + s0_ref[...], h1_ref[...] + s1_ref[...]], axis=1)
        d, mn, vn = _adam_math(w_ref[...], g, m_ref[...], v_ref[...])
        g_ref[...] = g
        d_ref[...] = d
        mo_ref[...] = mn
        vo_ref[...] = vn

    hspec = pl.BlockSpec((tc, R), lambda i: (i, 0))
    spec = pl.BlockSpec((tc, 2, R), lambda i: (i, 0, 0))
    return pl.pallas_call(
        body, name=name, grid=(pl.cdiv(C, tc),), in_specs=[hspec] * 4 + [spec] * 3, out_specs=[spec] * 4,
        out_shape=[jax.ShapeDtypeStruct(wt.shape, f32)] * 4, compiler_params=_cp(("parallel",)),
    )(h[0], h[1], hs[0], hs[1], wt, mt, vt)


MESH = pl.DeviceIdType.MESH
_HBM = pl.BlockSpec(memory_space=pltpu.HBM)


def _place():
    return lax.axis_index("x"), lax.axis_index("y"), lax.axis_index("c")


def weight_gather(arrs):
    n = len(arrs)

    def body(*refs):
        x_refs, out_refs = refs[:n], refs[n:2 * n]
        send_sems, recv_sems, local_sems = refs[2 * n:]
        x, y, c = _place()
        me, sibling = (x, y, c), (x, y, 1 - c)
        chips = [(1 - x, y), (x, 1 - y), (1 - x, 1 - y)]

        def copy(a, k, block, to, own_src=False):
            px, py, pc = block
            dst = out_refs[a].at[2 * px + py, pc]
            return pltpu.make_async_remote_copy(
                src_ref=x_refs[a].at[c] if own_src else dst, dst_ref=dst,
                send_sem=send_sems.at[7 * a + k], recv_sem=recv_sems.at[7 * a + k], device_id=to, device_id_type=MESH)

        mine = [pltpu.make_async_copy(x_refs[a].at[c], out_refs[a].at[2 * x + y, c], local_sems.at[a])
                for a in range(n)]
        for cp in mine:
            cp.start()
        first = []
        for a in range(n):
            first.append(copy(a, 0, me, sibling, own_src=True))
            first += [copy(a, 1 + j, me, (*chip, c), own_src=True) for j, chip in enumerate(chips)]
        for cp in first:
            cp.start()
        passed = []
        for j, chip in enumerate(chips):
            for a in range(n):
                copy(a, 1 + j, (*chip, c), me).wait_recv()
                fwd = copy(a, 4 + j, (*chip, c), sibling)
                fwd.start()
                passed.append(fwd)
        for a in range(n):
            copy(a, 0, sibling, me).wait_recv()
            for j, chip in enumerate(chips):
                copy(a, 4 + j, (*chip, 1 - c), me).wait_recv()
        for cp in first + passed:
            cp.wait_send()
        for cp in mine:
            cp.wait()

    return pl.pallas_call(
        body, name="weight_gather", in_specs=[_HBM] * n, out_specs=[_HBM] * n,
        out_shape=[jax.ShapeDtypeStruct((N_CHIPS,) + a.shape, a.dtype) for a in arrs],
        scratch_shapes=[pltpu.SemaphoreType.DMA((7 * n,)), pltpu.SemaphoreType.DMA((7 * n,)),
                        pltpu.SemaphoreType.DMA((n,))],
    )(*arrs)


SHARD_W = 256


_SEM = pl.BlockSpec(memory_space=pltpu.SEMAPHORE)
_EFFECT = pltpu.SideEffectType.DATAFLOW_SIDE_EFFECTING


def _landing_shape(a, kind):
    if kind == "all":
        return (N_CHIPS,) + a.shape
    if kind == "slot":
        return (3,) + a.shape[1:]
    return (3,) + ((a.shape[0], SHARD_W) if kind == "cols" else (SHARD_W, a.shape[1]))


def _shard_copies(src_refs, land_refs, kinds, send_sems, recv_sems):
    x, y, c = _place()
    copies = []
    for a, (src, land, kind) in enumerate(zip(src_refs, land_refs, kinds)):
        for j, (px, py) in enumerate(((1 - x, y), (x, 1 - y), (1 - x, 1 - y))):
            q = 2 * px + py
            lo = pl.multiple_of(q * SHARD_W, SHARD_W)
            part = {"slot": lambda: src.at[q], "cols": lambda: src.at[:, pl.ds(lo, SHARD_W)],
                    "rows": lambda: src.at[pl.ds(lo, SHARD_W), :], "all": lambda: src}[kind]()
            k = 3 * a + j
            copies.append(pltpu.make_async_remote_copy(
                src_ref=part, dst_ref=land.at[2 * x + y] if kind == "all" else land.at[j],
                send_sem=send_sems.at[k], recv_sem=recv_sems.at[k], device_id=(px, py, c), device_id_type=MESH))
    return copies


def exchange_start(srcs, kinds, name, after=None):
    n = len(srcs)
    lands = [lax.empty(_landing_shape(a, k), a.dtype) for a, k in zip(srcs, kinds)]
    extra = [] if after is None else [after]

    def body(*refs):
        src_refs, land_refs, token = refs[:n], refs[n:2 * n], refs[-1]
        send_sems, recv_sems = refs[2 * n + len(extra)], refs[2 * n + len(extra) + 1]
        for cp in _shard_copies(src_refs, land_refs, kinds, send_sems, recv_sems):
            cp.start()
        token[...] = jnp.zeros_like(token)

    both = list(srcs) + lands
    out = pl.pallas_call(
        body, name=name,
        out_shape=(pltpu.SemaphoreType.DMA((3 * n,)), pltpu.SemaphoreType.DMA((3 * n,)),
                   *[pltpu.HBM(a.shape, a.dtype) for a in both], jax.ShapeDtypeStruct((8, 128), f32)),
        in_specs=[_HBM] * (2 * n) + [pl.BlockSpec(memory_space=pl.ANY)] * len(extra),
        out_specs=(_SEM, _SEM, *[_HBM] * (2 * n), pl.BlockSpec(memory_space=pltpu.VMEM)),
        input_output_aliases={i: 2 + i for i in range(2 * n)},
        compiler_params=pltpu.CompilerParams(has_side_effects=_EFFECT),
    )(*[pltpu.with_memory_space_constraint(a, pltpu.HBM) for a in both], *extra)
    return (out[0], out[1], out[2:2 + 2 * n]), out[-1]


def exchange_wait(handle, kinds, after, name):
    send_sems, recv_sems, both = handle
    n = len(kinds)
    after = after if isinstance(after, tuple) else (after,)

    def body(*refs):
        src_refs, land_refs, s_sems, r_sems = refs[:n], refs[n:2 * n], refs[2 * n], refs[2 * n + 1]
        for cp in _shard_copies(src_refs, land_refs, kinds, s_sems, r_sems):
            cp.wait_send()
            cp.wait_recv()

    out = pl.pallas_call(
        body, name=name, out_shape=tuple(pltpu.HBM(a.shape, a.dtype) for a in both),
        in_specs=[_HBM] * (2 * n) + [_SEM, _SEM] + [pl.BlockSpec(memory_space=pl.ANY)] * len(after),
        out_specs=tuple([_HBM] * (2 * n)), input_output_aliases={i: i for i in range(2 * n)},
        compiler_params=pltpu.CompilerParams(has_side_effects=_EFFECT),
    )(*both, send_sems, recv_sems, *after)
    return out[n:]


def final_exchange(hs, small):
    n = len(hs)
    S = small.shape[0]

    def body(*refs):
        h_refs, sm_ref, out_refs, smalls_ref = refs[:n], refs[n], refs[n + 1:2 * n + 1], refs[2 * n + 1]
        send_sems, recv_sems, local_sem = refs[2 * n + 2:]
        x, y, c = _place()
        my_slot = smalls_ref.at[4 * x + 2 * y + c]
        mine = pltpu.make_async_copy(sm_ref, my_slot, local_sem)
        mine.start()
        copies = [pltpu.make_async_remote_copy(src_ref=h_refs[a], dst_ref=out_refs[a], send_sem=send_sems.at[a],
                                               recv_sem=recv_sems.at[a], device_id=(x, y, 1 - c), device_id_type=MESH)
                  for a in range(n)]
        for mask in range(1, 8):
            fx, fy, fc = (mask >> 2) & 1, (mask >> 1) & 1, mask & 1
            peer = ((1 - x) if fx else x, (1 - y) if fy else y, (1 - c) if fc else c)
            copies.append(pltpu.make_async_remote_copy(
                src_ref=sm_ref, dst_ref=my_slot, send_sem=send_sems.at[n - 1 + mask], recv_sem=recv_sems.at[n - 1 + mask],
                device_id=peer, device_id_type=MESH))
        for cp in copies:
            cp.start()
        for cp in copies:
            cp.wait_recv()
        for cp in copies:
            cp.wait_send()
        mine.wait()

    sd = jax.ShapeDtypeStruct
    out = pl.pallas_call(
        body, name="final_exchange", in_specs=[_HBM] * (n + 1), out_specs=[_HBM] * (n + 1),
        out_shape=[sd(h.shape, h.dtype) for h in hs] + [sd((8, S, 128), f32)],
        scratch_shapes=[pltpu.SemaphoreType.DMA((n + 7,)), pltpu.SemaphoreType.DMA((n + 7,)),
                        pltpu.SemaphoreType.DMA],
    )(*hs, small)
    return out[:n], out[n]


N_CHIPS = 4
SHARD_COLS = N_ORIG // N_CHIPS


SHARD_PAD = 2688
_COL_SEGMENTS = (
    ((0, 2048, OFF_A),)
    + tuple((2048 + 512 * j + HK * h, 2048 + 512 * j + HK * (h + 1), OFF_B + REG_BH * h + HK * j)
            for j in range(3) for h in range(N_HGRN))
    + ((3584, 4096, OFF_M + M_BZ), (4096, 6144, OFF_C), (6144, 6160, OFF_C + 2048), (6160, 7184, OFF_M + M_CZ),
       (7184, N_ORIG, OFF_M + M_G)))


def _shard_pieces():
    pieces = []
    for lo, hi, dst in _COL_SEGMENTS:
        for p in range(N_CHIPS):
            a, b = max(lo, p * SHARD_COLS), min(hi, (p + 1) * SHARD_COLS)
            if a < b:
                pieces.append((p, a - p * SHARD_COLS, dst + a - lo, b - a))
    return pieces


def win_cast_pad(wt):
    tc = 128

    def body(x_ref, o0_ref, o1_ref):
        col = pl.program_id(0) * tc + lax.broadcasted_iota(jnp.int32, (tc, 1), 0)
        for l, o_ref in enumerate((o0_ref, o1_ref)):
            o_ref[...] = _b(jnp.where(col < SHARD_COLS, x_ref[:, l, :], 0.0).T)

    spec = pl.BlockSpec((D, tc), lambda i: (0, i))
    return pl.pallas_call(
        body, name="win_cast_pad", grid=(SHARD_PAD // tc,),
        in_specs=[pl.BlockSpec((tc, 2, D), lambda i: (i, 0, 0))], out_specs=[spec, spec],
        out_shape=[jax.ShapeDtypeStruct((D, SHARD_PAD), bf16)] * 2, compiler_params=_cp(("parallel",)),
    )(wt)


def win_to_padded(w4, name):
    tr = 256
    pieces = _shard_pieces()

    def body(a_ref, o_ref):
        o_ref[...] = jnp.zeros((tr, NP), bf16)
        for p, j0, c0, n in pieces:
            o_ref[:, c0:c0 + n] = a_ref[p, :, j0:j0 + n]

    return pl.pallas_call(
        body, name=name, grid=(D // tr,),
        in_specs=[pl.BlockSpec((N_CHIPS, tr, SHARD_PAD), lambda i: (0, i, 0))],
        out_specs=pl.BlockSpec((tr, NP), lambda i: (i, 0)),
        out_shape=jax.ShapeDtypeStruct((D, NP), bf16), compiler_params=_cp(("parallel",)),
    )(w4)


def win_from_padded(dw):
    tr = 128
    pieces = _shard_pieces()

    def body(d_ref, of_ref, ob_ref):
        for p in range(N_CHIPS):
            of_ref[p, :, SHARD_COLS:] = jnp.zeros((tr, SHARD_PAD - SHARD_COLS), f32)
            ob_ref[p, :, SHARD_COLS:] = jnp.zeros((tr, SHARD_PAD - SHARD_COLS), bf16)
        for p, j0, c0, n in pieces:
            v = d_ref[:, c0:c0 + n]
            of_ref[p, :, j0:j0 + n] = v
            ob_ref[p, :, j0:j0 + n] = _b(v)

    out_spec = pl.BlockSpec((N_CHIPS, tr, SHARD_PAD), lambda i: (0, i, 0))
    return pl.pallas_call(
        body, name="win_from_padded", grid=(D // tr,),
        in_specs=[pl.BlockSpec((tr, NP), lambda i: (i, 0))], out_specs=[out_spec, out_spec],
        out_shape=[jax.ShapeDtypeStruct((N_CHIPS, D, SHARD_PAD), f32),
                   jax.ShapeDtypeStruct((N_CHIPS, D, SHARD_PAD), bf16)],
        compiler_params=_cp(("parallel",)),
    )(dw)


def _rows128(a):
    flat = a.reshape(-1)
    total = -(-flat.shape[0] // 1024) * 1024
    return jnp.pad(flat, (0, total - flat.shape[0])).reshape(total // 128, 128)


def _lb_rows(lb):
    return jnp.pad(lb.reshape(2, 4, 128), ((0, 0), (0, 4), (0, 0))).reshape(16, 128)


def _pack_small(v, with_conv):
    rows = []
    for name, n in _SMALL + (_CONV if with_conv else ()):
        if name == "lower_bounds":
            rows.append(_lb_rows(v[name]))
        elif name == "loss":
            rows.append(jnp.broadcast_to(v[name], (16, 128)) if name in v else jnp.zeros((16, 128), f32))
        else:
            rows.append(_rows128(v[name]))
    return jnp.concatenate(rows, axis=0)


def _unpack_small(p, shapes, with_conv):
    out, row = {}, 0
    for name, n in _SMALL + (_CONV if with_conv else ()):
        nrows = _small_rows(n)
        blk = p[row:row + nrows]
        if name == "lower_bounds":
            out[name] = blk.reshape(2, 8, 128)[:, :4].reshape(2, 512)
        elif name == "loss":
            out[name] = blk[0, 0]
        else:
            out[name] = blk.reshape(-1)[:n].reshape(shapes[name])
        row += nrows
    return out


def _lane_vec(a8):
    return jnp.pad(a8.reshape(1, 8), ((0, 0), (8, 112)))


WEIGHT_NAMES = ("norm_w", "w_in", "b_gate", "conv_a", "conv_c", "a_log", "dt_bias", "lower_bounds", "hgrn_norm_w",
                "gdn_norm_w", "w_out_a", "w_out_b", "w_out_c", "w_o", "final_norm_w")


def kernel(x, norm_w, w_in, b_gate, conv_a, conv_c, a_log, dt_bias, lower_bounds, hgrn_norm_w, gdn_norm_w, w_out_a, w_out_b, w_out_c, w_o, final_norm_w, loss_target, m_norm_w, m_w_in, m_b_gate, m_conv_a, m_conv_c, m_a_log, m_dt_bias, m_lower_bounds, m_hgrn_norm_w, m_gdn_norm_w, m_w_out_a, m_w_out_b, m_w_out_c, m_w_o, m_final_norm_w, v_norm_w, v_w_in, v_b_gate, v_conv_a, v_conv_c, v_a_log, v_dt_bias, v_lower_bounds, v_hgrn_norm_w, v_gdn_norm_w, v_w_out_a, v_w_out_b, v_w_out_c, v_w_o, v_final_norm_w):
    wts = dict(norm_w=norm_w, w_in=w_in, b_gate=b_gate, conv_a=conv_a, conv_c=conv_c, a_log=a_log, dt_bias=dt_bias,
               lower_bounds=lower_bounds, hgrn_norm_w=hgrn_norm_w, gdn_norm_w=gdn_norm_w, w_out_a=w_out_a,
               w_out_b=w_out_b, w_out_c=w_out_c, w_o=w_o, final_norm_w=final_norm_w)
    mom = dict(norm_w=m_norm_w, w_in=m_w_in, b_gate=m_b_gate, conv_a=m_conv_a, conv_c=m_conv_c, a_log=m_a_log,
               dt_bias=m_dt_bias, lower_bounds=m_lower_bounds, hgrn_norm_w=m_hgrn_norm_w, gdn_norm_w=m_gdn_norm_w,
               w_out_a=m_w_out_a, w_out_b=m_w_out_b, w_out_c=m_w_out_c, w_o=m_w_o, final_norm_w=m_final_norm_w)
    var = dict(norm_w=v_norm_w, w_in=v_w_in, b_gate=v_b_gate, conv_a=v_conv_a, conv_c=v_conv_c, a_log=v_a_log,
               dt_bias=v_dt_bias, lower_bounds=v_lower_bounds, hgrn_norm_w=v_hgrn_norm_w, gdn_norm_w=v_gdn_norm_w,
               w_out_a=v_w_out_a, w_out_b=v_w_out_b, w_out_c=v_w_out_c, w_o=v_w_o, final_norm_w=v_final_norm_w)
    chip = 2 * lax.axis_index("x") + lax.axis_index("y")
    chip1 = chip.reshape(1).astype(jnp.int32)

    win_l0, win_l1 = win_cast_pad(jnp.transpose(w_in, (2, 0, 1)))
    win4_l0, ca4, cc4 = weight_gather([win_l0.reshape(2, D // 2, SHARD_PAD), conv_a, conv_c])
    by_cols = lambda a: a.transpose(1, 2, 0, 3).reshape(a.shape[1], a.shape[2], N_CHIPS * a.shape[3])
    by_rows = lambda a: a.transpose(1, 0, 2, 3).reshape(a.shape[1], N_CHIPS * a.shape[2], a.shape[3])
    conv_a_full, conv_c_full = by_cols(ca4), by_cols(cc4)
    later = [win_l1, _b(w_out_a), _b(w_out_b), _b(w_out_c), _b(w_o)]
    sent_w, token = exchange_start(later, ["all"] * 5, "weights_start", after=win4_l0)

    def late_weights(after):
        lands = exchange_wait(sent_w, ["all"] * 5, after, "weights_wait")
        l1, woa4, wob4, woc4, wo4 = (lax.dynamic_update_index_in_dim(land, own, chip, 0)
                                     for land, own in zip(lands, later))
        outs = dict(w_out_a=by_cols(woa4), w_out_b=by_cols(wob4), w_out_c=by_rows(woc4), w_o=by_rows(wo4))
        layers[1].update(outs, w_in=win_to_padded(l1, "win_to_padded1"))
        return outs

    lbs = lbs_fwd(lower_bounds)
    layers = []
    for l in range(2):
        layers.append(dict(
            l=l, norm_w=norm_w[l:l + 1], b_gate=b_gate[l:l + 1], conv_a=conv_a_full[l], conv_c=conv_c_full[l],
            alog_l=_lane_vec(a_log[l]), dtb_l=_lane_vec(dt_bias[l]), lbs=lbs[l:l + 1],
            hgrn_norm_w=hgrn_norm_w[l:l + 1], gdn_norm_w=gdn_norm_w[l:l + 1]))
    layers[0].update(w_in=win_to_padded(win4_l0.reshape(N_CHIPS, D, SHARD_PAD), "win_to_padded0"), late=late_weights,
                     norm_w=norm_w[0:1] + token[0:1, 0:1])

    xs, saved = x[0], []
    for l in range(2):
        xs, s = layer_fwd(xs, layers[l])
        saved.append(s)
    loss_row, dx, dfw = loss_head(xs, final_norm_w.reshape(1, D), loss_target[0])
    lg, half = [None, None], [None, None]
    for l in (1, 0):
        dx, lg[l], half[l] = layer_bwd(dx, layers[l], saved[l], chip1)
    grad_x = dx[None]

    stack = lambda n: jnp.stack([lg[0][n], lg[1][n]], axis=0)
    gsmall = {n: stack(n) for n in ("norm_w", "b_gate", "hgrn_norm_w", "gdn_norm_w", "a_log", "dt_bias", "conv_a",
                                    "conv_c")}
    gsmall.update(lower_bounds=stack("lbs"), final_norm_w=dfw, loss=loss_row)
    mat_names = ("w_in",) + tuple(n for n, _ in OUT_MATS)
    mine = [half[l][n] for n in mat_names for l in range(2)]
    theirs, smalls = final_exchange(mine, _pack_small(gsmall, True))

    out_g, out_d, out_m, out_v = {}, {}, {}, {}
    for i, n in enumerate(mat_names):
        h, hs = mine[2 * i:2 * i + 2], theirs[2 * i:2 * i + 2]
        if n == "w_in":
            fwd, back = (lambda a: jnp.transpose(a, (2, 0, 1))), (lambda a: jnp.transpose(a, (1, 2, 0)))
            res = adam_pair_t(h, hs, fwd(wts[n]), fwd(mom[n]), fwd(var[n]), "adam_" + n)
            out_g[n], out_d[n], out_m[n], out_v[n] = (back(a) for a in res)
        else:
            out_g[n], out_d[n], out_m[n], out_v[n] = adam_pair(h, hs, wts[n], mom[n], var[n], "adam_" + n)
    small_names = [n for n, _ in _SMALL if n != "loss"]
    pack = lambda v: _pack_small({n: v[n] for n in small_names}, False)
    sg, sd, smn, svn = small_update(smalls, pack(wts), pack(mom), pack(var))
    shapes = {n: wts[n].shape for n in small_names}
    shapes.update(conv_a=(2, 3, 512), conv_c=(2, 4, 2048))
    for dst, src, conv in ((out_g, sg, True), (out_d, sd, False), (out_m, smn, False), (out_v, svn, False)):
        dst.update(_unpack_small(src, shapes, conv))
    loss = out_g.pop("loss")
    for n in ("conv_a", "conv_c"):
        width = wts[n].shape[2]
        g = lax.dynamic_slice_in_dim(out_g[n], chip * width, width, axis=2)
        two_d = lambda a: a.reshape(-1, width)
        d, mn, vn = adam(two_d(wts[n]), two_d(g), two_d(mom[n]), two_d(var[n]), "adam_" + n)
        out_g[n] = g
        out_d[n], out_m[n], out_v[n] = (a.reshape(wts[n].shape) for a in (d, mn, vn))
    return (loss, grad_x, *[out_g[n] for n in WEIGHT_NAMES], *[out_d[n] for n in WEIGHT_NAMES],
            *[out_m[n] for n in WEIGHT_NAMES], *[out_v[n] for n in WEIGHT_NAMES])
```

```python
import jax
import jax.numpy as jnp
from jax import lax
from jax.experimental import pallas as pl
from jax.experimental.pallas import tpu as pltpu

f32 = jnp.float32
bf16 = jnp.bfloat16

D = 1024
L = 64
SUB = 16
NORM_EPS = 1e-6
L2_EPS = 1e-6
MIN_F = 1e-30
HK = 128
QK_SCALE = HK ** -0.5
N_GDN = 8
GDN_BLOCK = 1024
N_HGRN = 4

REG_A = 2304
REG_C = 2304
REG_M = 4608
REG_BH = 384
OFF_A, OFF_C, OFF_M, OFF_B = 0, 2304, 4608, 9216
M_BZ, M_G, M_CZ = 0, 512, 3584
NP = 10752
NP_TILE = 1536
N_ORIG = 10256

ADAM_LR, ADAM_B1, ADAM_B2, ADAM_EPS, ADAM_WD, ADAM_STEP = 0.001, 0.9, 0.999, 1e-08, 0.01, 10

VMEM_LIMIT = 56 * 1024 * 1024


def _cp(sem):
    return pltpu.CompilerParams(dimension_semantics=sem, vmem_limit_bytes=VMEM_LIMIT)


def _sigmoid(x):
    return jax.nn.sigmoid(x)


def _silu(x):
    return x * _sigmoid(x)


def _silu2(x):
    s = _sigmoid(x)
    y = x * s
    return y, s + y * (1.0 - s)


def _softplus(x):
    u = jnp.exp(-jnp.abs(x))
    w = 1.0 + u
    l1p = jnp.where(w == 1.0, u, jnp.log(w) * (u / (w - 1.0)))
    return jnp.maximum(x, 0.0) + l1p


def _dot(a, b):
    return jnp.dot(a, b, preferred_element_type=f32)


def _dot_nt(a, b):
    return lax.dot_general(a, b, (((1,), (1,)), ((), ())), preferred_element_type=f32)


def _dot_tn(a, b):
    return lax.dot_general(a, b, (((0,), (0,)), ((), ())), preferred_element_type=f32)


def _bdot(a, b):
    return lax.dot_general(a, b, (((2,), (1,)), ((0,), (0,))), preferred_element_type=f32)


def _bdot_nt(a, b):
    return lax.dot_general(a, b, (((2,), (2,)), ((0,), (0,))), preferred_element_type=f32)


def _bdot_tn(a, b):
    return lax.dot_general(a, b, (((1,), (1,)), ((0,), (0,))), preferred_element_type=f32)


def _bdot_split(a, b):
    ah, bh = _b(a), _b(b)
    al, bl = _b(a - ah.astype(f32)), _b(b - bh.astype(f32))
    return _bdot(ah, bh) + (_bdot(ah, bl) + _bdot(al, bh))


def _b(x):
    return x.astype(bf16)


def _chunk_cumsum(x, rows_in_chunk):
    n = x.shape[0]
    for s in (1, 2, 4, 8, 16, 32):
        x = x + jnp.where(rows_in_chunk >= s, pltpu.roll(x, s, axis=0), 0.0)
    return x


def _chunk_rev_cumsum(x, rows_in_chunk):
    n = x.shape[0]
    for s in (1, 2, 4, 8, 16, 32):
        x = x + jnp.where(rows_in_chunk + s < L, pltpu.roll(x, n - s, axis=0), 0.0)
    return x


def _shift_down(x, s):
    return pltpu.roll(x, s, axis=0) if s else x


def _shift_up(x, s):
    return pltpu.roll(x, x.shape[0] - s, axis=0) if s else x


def inproj_fwd(x, nw, w):
    T = x.shape[0]
    tT, tn = min(2048, T), NP_TILE

    def body(x_ref, nw_ref, w_ref, p_ref, h_ref, hs):
        @pl.when(pl.program_id(1) == 0)
        def _():
            xv = x_ref[...]
            r = lax.rsqrt(jnp.mean(xv * xv, axis=-1, keepdims=True) + NORM_EPS)
            hv = _b(xv * r * nw_ref[...])
            hs[...] = hv
            h_ref[...] = hv
        p_ref[...] = _b(_dot(hs[...], w_ref[...]))

    return pl.pallas_call(
        body, name="inproj_fwd", grid=(T // tT, NP // tn),
        in_specs=[pl.BlockSpec((tT, D), lambda i, j: (i, 0)), pl.BlockSpec((1, D), lambda i, j: (0, 0)),
                  pl.BlockSpec((D, tn), lambda i, j: (0, j))],
        out_specs=[pl.BlockSpec((tT, tn), lambda i, j: (i, j)), pl.BlockSpec((tT, D), lambda i, j: (i, 0))],
        out_shape=[jax.ShapeDtypeStruct((T, NP), bf16), jax.ShapeDtypeStruct((T, D), bf16)],
        scratch_shapes=[pltpu.VMEM((tT, D), bf16)],
        compiler_params=_cp(("parallel", "arbitrary")),
    )(x, nw, w)


def matmul_tn(a, b, name, n=None, b_col0=0, with_bf16=False):
    T, K = a.shape
    N = b.shape[1] if n is None else n
    tT = min(2048, T)
    tn = NP_TILE if N % NP_TILE == 0 else min(N, 1024)
    nt = T // tT
    cb0 = b_col0 // tn

    def body(a_ref, b_ref, o_ref, *ob_ref):
        @pl.when(pl.program_id(1) == 0)
        def _():
            o_ref[...] = jnp.zeros_like(o_ref)
        o_ref[...] += _dot_tn(_b(a_ref[...]), _b(b_ref[...]))
        if with_bf16:
            @pl.when(pl.program_id(1) == nt - 1)
            def _():
                ob_ref[0][...] = _b(o_ref[...])

    ospec = pl.BlockSpec((K, tn), lambda j, t: (0, j))
    return pl.pallas_call(
        body, name=name, grid=(N // tn, nt),
        in_specs=[pl.BlockSpec((tT, K), lambda j, t: (t, 0)), pl.BlockSpec((tT, tn), lambda j, t: (t, cb0 + j))],
        out_specs=[ospec, ospec] if with_bf16 else ospec,
        out_shape=([jax.ShapeDtypeStruct((K, N), f32), jax.ShapeDtypeStruct((K, N), bf16)] if with_bf16
                   else jax.ShapeDtypeStruct((K, N), f32)),
        compiler_params=_cp(("parallel", "arbitrary")),
    )(a, b)


def inproj_bwd(dp, w, x, nw, dres):
    T = x.shape[0]
    tT, tk = min(1024, T), NP // 4
    nk = NP // tk

    def body(dp_ref, w_ref, x_ref, nw_ref, dres_ref, dx_ref, dnw_ref, acc):
        i, k = pl.program_id(0), pl.program_id(1)

        @pl.when((i == 0) & (k == 0))
        def _():
            dnw_ref[...] = jnp.zeros_like(dnw_ref)

        @pl.when(k == 0)
        def _():
            acc[...] = jnp.zeros_like(acc)
        acc[...] += _dot_nt(dp_ref[...], w_ref[...])

        @pl.when(k == nk - 1)
        def _():
            xv = x_ref[...]
            r = lax.rsqrt(jnp.mean(xv * xv, axis=-1, keepdims=True) + NORM_EPS)
            xh = xv * r
            dy = acc[...]
            dyw = dy * nw_ref[...]
            dx_ref[...] = r * (dyw - xh * jnp.mean(dyw * xh, axis=-1, keepdims=True)) + dres_ref[...]
            dnw_ref[...] += jnp.sum(dy * xh, axis=0, keepdims=True)

    return pl.pallas_call(
        body, name="inproj_bwd", grid=(T // tT, nk),
        in_specs=[pl.BlockSpec((tT, tk), lambda i, k: (i, k)), pl.BlockSpec((D, tk), lambda i, k: (0, k)),
                  pl.BlockSpec((tT, D), lambda i, k: (i, 0)), pl.BlockSpec((1, D), lambda i, k: (0, 0)),
                  pl.BlockSpec((tT, D), lambda i, k: (i, 0))],
        out_specs=[pl.BlockSpec((tT, D), lambda i, k: (i, 0)), pl.BlockSpec((1, D), lambda i, k: (0, 0))],
        out_shape=[jax.ShapeDtypeStruct((T, D), f32), jax.ShapeDtypeStruct((1, D), f32)],
        scratch_shapes=[pltpu.VMEM((tT, D), f32)],
        compiler_params=_cp(("arbitrary", "arbitrary")),
    )(dp, w, x, nw, dres)


def loss_head(x, fw, tgt):
    T = x.shape[0]
    tT = min(512, T)

    def body(x_ref, fw_ref, t_ref, loss_ref, dx_ref, dfw_ref):
        @pl.when(pl.program_id(0) == 0)
        def _():
            loss_ref[...] = jnp.zeros_like(loss_ref)
            dfw_ref[...] = jnp.zeros_like(dfw_ref)
        xv = x_ref[...]
        r = lax.rsqrt(jnp.mean(xv * xv, axis=-1, keepdims=True) + NORM_EPS)
        xh = xv * r
        err = xh * fw_ref[...] - t_ref[...]
        part = 0.5 * jnp.sum(jnp.mean(err * err, axis=-1, keepdims=True), axis=0, keepdims=True)
        loss_ref[...] += jnp.broadcast_to(part, loss_ref.shape)
        dy = err * (1.0 / D)
        dyw = dy * fw_ref[...]
        dx_ref[...] = r * (dyw - xh * jnp.mean(dyw * xh, axis=-1, keepdims=True))
        dfw_ref[...] += jnp.sum(dy * xh, axis=0, keepdims=True)

    return pl.pallas_call(
        body, name="loss_head", grid=(T // tT,),
        in_specs=[pl.BlockSpec((tT, D), lambda i: (i, 0)), pl.BlockSpec((1, D), lambda i: (0, 0)),
                  pl.BlockSpec((tT, D), lambda i: (i, 0))],
        out_specs=[pl.BlockSpec((1, 128), lambda i: (0, 0)), pl.BlockSpec((tT, D), lambda i: (i, 0)),
                   pl.BlockSpec((1, D), lambda i: (0, 0))],
        out_shape=[jax.ShapeDtypeStruct((1, 128), f32), jax.ShapeDtypeStruct((T, D), f32),
                   jax.ShapeDtypeStruct((1, D), f32)],
        compiler_params=_cp(("arbitrary",)),
    )(x, fw, tgt)


def _halo_specs(tT, T, width, colblk, rows=8):
    nb = T // rows
    per = tT // rows
    prev = pl.BlockSpec((rows, width), lambda i: (jnp.maximum(i * per - 1, 0), colblk))
    nxt = pl.BlockSpec((rows, width), lambda i: (jnp.minimum((i + 1) * per, nb - 1), colblk))
    return prev, nxt


def _p_rows(p_ref, width=2048):
    return p_ref[:, 0:width].astype(f32)


def _p_prev(pp_ref, width=2048):
    return pp_ref[:, 0:width].astype(f32)[8:16]


def _p_next(pn_ref, width=2048):
    return pn_ref[:, 0:width].astype(f32)[0:8]


def mixa_fwd(p, cw):
    T = p.shape[0]
    tT = min(512, T)
    prev_spec, _ = _halo_specs(tT, T, REG_A, OFF_A // REG_A, rows=16)

    def body(p_ref, pp_ref, cw_ref, y_ref):
        pv = _p_rows(p_ref)
        u = pv[:, 512:1024] * pv[:, 1024:1536]
        pp = _p_prev(pp_ref)
        up = jnp.where(pl.program_id(0) == 0, 0.0, pp[:, 512:1024] * pp[:, 1024:1536])
        ue = jnp.concatenate([up, u], axis=0)
        cv = cw_ref[0:1, :] * _shift_down(ue, 2) + cw_ref[1:2, :] * _shift_down(ue, 1) + cw_ref[2:3, :] * ue
        y_ref[...] = _b(pv[:, 0:512] * cv[8:] * _silu(pv[:, 1536:2048]))

    return pl.pallas_call(
        body, name="mixa_fwd", grid=(T // tT,),
        in_specs=[pl.BlockSpec((tT, REG_A), lambda i: (i, OFF_A // REG_A)), prev_spec,
                  pl.BlockSpec((3, 512), lambda i: (0, 0))],
        out_specs=pl.BlockSpec((tT, 512), lambda i: (i, 0)),
        out_shape=jax.ShapeDtypeStruct((T, 512), bf16),
        compiler_params=_cp(("parallel",)),
    )(p, p, cw)


def mixa_bwd(p, cw, dy, dp):
    T = p.shape[0]
    tT = min(512, T)
    nt = T // tT
    prev_spec, next_spec = _halo_specs(tT, T, REG_A, OFF_A // REG_A, rows=16)
    _, dnext_spec = _halo_specs(tT, T, 512, 0)

    def body(p_ref, pp_ref, pn_ref, cw_ref, dy_ref, dyn_ref, dp_in, dp_ref, dcw_ref):
        i = pl.program_id(0)

        @pl.when(i == 0)
        def _():
            dcw_ref[...] = jnp.zeros_like(dcw_ref)
        pv, pp, pn = _p_rows(p_ref), _p_prev(pp_ref), _p_next(pn_ref)
        pe = jnp.concatenate([pp, pv, pn], axis=0)
        rows = lax.broadcasted_iota(jnp.int32, (tT + 16, 1), 0)
        ab, ac, ax, az = pe[:, 0:512], pe[:, 512:1024], pe[:, 1024:1536], pe[:, 1536:2048]
        u = jnp.where((rows < 8) & (i == 0), 0.0, ac * ax)
        u1, u2 = _shift_down(u, 1), _shift_down(u, 2)
        w0, w1, w2 = cw_ref[0:1, :], cw_ref[1:2, :], cw_ref[2:3, :]
        cv = w0 * u2 + w1 * u1 + w2 * u
        dye = jnp.concatenate([jnp.zeros((8, 512), f32), dy_ref[...], dyn_ref[...]], axis=0)
        dye = jnp.where((rows >= tT + 8) & (i == nt - 1), 0.0, dye)
        sz, dsz = _silu2(az)
        dcv = dye * ab * sz
        du = w2 * dcv + w1 * _shift_up(dcv, 1) + w0 * _shift_up(dcv, 2)
        inner = (rows >= 8) & (rows < tT + 8)
        dcv_in = jnp.where(inner, dcv, 0.0)
        dcw_ref[0:1, :] += jnp.sum(dcv_in * u2, axis=0, keepdims=True)
        dcw_ref[1:2, :] += jnp.sum(dcv_in * u1, axis=0, keepdims=True)
        dcw_ref[2:3, :] += jnp.sum(dcv_in * u, axis=0, keepdims=True)
        sl = slice(8, tT + 8)
        dp_ref[:, 0:512] = _b((dye * cv * sz)[sl])
        dp_ref[:, 512:1024] = _b((du * ax)[sl])
        dp_ref[:, 1024:1536] = _b((du * ac)[sl])
        dp_ref[:, 1536:2048] = _b((dye * ab * cv * dsz)[sl])
        dp_ref[:, 2048:] = jnp.zeros((tT, REG_A - 2048), bf16)

    return pl.pallas_call(
        body, name="mixa_bwd", grid=(nt,),
        in_specs=[pl.BlockSpec((tT, REG_A), lambda i: (i, OFF_A // REG_A)), prev_spec, next_spec,
                  pl.BlockSpec((3, 512), lambda i: (0, 0)),
                  pl.BlockSpec((tT, 512), lambda i: (i, 0)), dnext_spec, pl.BlockSpec(memory_space=pl.ANY)],
        out_specs=[pl.BlockSpec((tT, REG_A), lambda i: (i, OFF_A // REG_A)), pl.BlockSpec((8, 512), lambda i: (0, 0))],
        out_shape=[jax.ShapeDtypeStruct((T, NP), bf16), jax.ShapeDtypeStruct((8, 512), f32)],
        input_output_aliases={6: 0},
        compiler_params=_cp(("arbitrary",)),
    )(p, p, p, cw, dy, dy, dp)


def _l2n_fwd(y):
    return y * lax.rsqrt(jnp.sum(y * y, axis=-1, keepdims=True) + L2_EPS)


def mixc_pre_fwd(p, cw, alog_l, dtb_l):
    T = p.shape[0]
    tT = min(512, T)
    prev_spec, _ = _halo_specs(tT, T, REG_C, OFF_C // REG_C, rows=16)

    def body(p_ref, pp_ref, cw_ref, al_ref, dt_ref, q_ref, k_ref, v_ref, sm_ref):
        pp = jnp.where(pl.program_id(0) == 0, 0.0, _p_prev(pp_ref))
        xe = jnp.concatenate([pp, _p_rows(p_ref)], axis=0)
        cv = (cw_ref[0:1, :] * _shift_down(xe, 3) + cw_ref[1:2, :] * _shift_down(xe, 2)
              + cw_ref[2:3, :] * _shift_down(xe, 1) + cw_ref[3:4, :] * xe)[8:]
        y = _silu(cv)
        for hh in range(4):
            sl = slice(hh * HK, (hh + 1) * HK)
            q_ref[:, sl] = _l2n_fwd(y[:, sl]) * QK_SCALE
            k_ref[:, sl] = _l2n_fwd(y[:, 512 + hh * HK:512 + (hh + 1) * HK])
        v_ref[...] = y[:, 1024:2048]
        ps = p_ref[:, 2048:2176].astype(f32)
        lane = lax.broadcasted_iota(jnp.int32, ps.shape, 1)
        la = -jnp.exp(al_ref[...]) * _softplus(ps + dt_ref[...])
        rin = lax.broadcasted_iota(jnp.int32, ps.shape, 0) % L
        g = _chunk_cumsum(la, rin)
        sm_ref[...] = jnp.where(lane < 8, _sigmoid(ps), jnp.where(lane < 16, g, 0.0))

    return pl.pallas_call(
        body, name="mixc_pre_fwd", grid=(T // tT,),
        in_specs=[pl.BlockSpec((tT, REG_C), lambda i: (i, OFF_C // REG_C)), prev_spec,
                  pl.BlockSpec((4, 2048), lambda i: (0, 0)),
                  pl.BlockSpec((1, 128), lambda i: (0, 0)), pl.BlockSpec((1, 128), lambda i: (0, 0))],
        out_specs=[pl.BlockSpec((tT, 512), lambda i: (i, 0)), pl.BlockSpec((tT, 512), lambda i: (i, 0)),
                   pl.BlockSpec((tT, 1024), lambda i: (i, 0)), pl.BlockSpec((tT, 128), lambda i: (i, 0))],
        out_shape=[jax.ShapeDtypeStruct((T, 512), f32), jax.ShapeDtypeStruct((T, 512), f32),
                   jax.ShapeDtypeStruct((T, 1024), f32), jax.ShapeDtypeStruct((T, 128), f32)],
        compiler_params=_cp(("parallel",)),
    )(p, p, cw, alog_l, dtb_l)


def mixc_pre_bwd(p, cw, alog_l, dtb_l, dq8, dk8, dv, dsm8, dp):
    T = p.shape[0]
    tT = min(256, T)
    nt = T // tT
    prev_spec, next_spec = _halo_specs(tT, T, REG_C, OFF_C // REG_C, rows=16)
    _, n1024 = _halo_specs(tT, T, 1024, 0)

    def body(p_ref, pp_ref, pn_ref, cw_ref, al_ref, dt_ref, dq_ref, dqn_ref, dk_ref, dkn_ref,
             dv_ref, dvn_ref, dsm_ref, dp_in, dp_ref, dcw_ref, dsml_ref):
        i = pl.program_id(0)

        @pl.when(i == 0)
        def _():
            dcw_ref[...] = jnp.zeros_like(dcw_ref)
            dsml_ref[...] = jnp.zeros_like(dsml_ref)
        rows = lax.broadcasted_iota(jnp.int32, (tT + 16, 1), 0)
        pp = jnp.where(i == 0, 0.0, _p_prev(pp_ref))
        xe = jnp.concatenate([pp, _p_rows(p_ref), _p_next(pn_ref)], axis=0)
        xs = [_shift_down(xe, 3), _shift_down(xe, 2), _shift_down(xe, 1), xe]
        cv = cw_ref[0:1, :] * xs[0] + cw_ref[1:2, :] * xs[1] + cw_ref[2:3, :] * xs[2] + cw_ref[3:4, :] * xs[3]
        y, dy_dcv = _silu2(cv)
        last = (rows >= tT + 8) & (i == nt - 1)
        z8q = jnp.zeros((8, 1024), f32)

        def ext(cur_ref, nxt_ref):
            return jnp.where(last, 0.0, jnp.concatenate([z8q, cur_ref[...], nxt_ref[...]], axis=0))
        dq8e, dk8e, dve = ext(dq_ref, dqn_ref), ext(dk_ref, dkn_ref), ext(dv_ref, dvn_ref)
        dys = []
        for (d8, base, scale) in ((dq8e, 0, QK_SCALE), (dk8e, 512, 1.0)):
            for hh in range(4):
                dn = (d8[:, (2 * hh) * HK:(2 * hh + 1) * HK] + d8[:, (2 * hh + 1) * HK:(2 * hh + 2) * HK]) * scale
                yh = y[:, base + hh * HK:base + (hh + 1) * HK]
                r = lax.rsqrt(jnp.sum(yh * yh, axis=-1, keepdims=True) + L2_EPS)
                nh = yh * r
                dys.append(r * (dn - nh * jnp.sum(dn * nh, axis=-1, keepdims=True)))
        dyy = jnp.concatenate(dys + [dve], axis=1)
        dcv = dyy * dy_dcv
        dx = (cw_ref[3:4, :] * dcv + cw_ref[2:3, :] * _shift_up(dcv, 1) + cw_ref[1:2, :] * _shift_up(dcv, 2)
              + cw_ref[0:1, :] * _shift_up(dcv, 3))
        dp_ref[:, 0:2048] = _b(dx[8:tT + 8])
        dp_ref[:, 2176:] = jnp.zeros((tT, REG_C - 2176), bf16)
        inner = (rows >= 8) & (rows < tT + 8)
        dcv_in = jnp.where(inner, dcv, 0.0)
        for j in range(4):
            dcw_ref[j:j + 1, :] += jnp.sum(dcv_in * xs[j], axis=0, keepdims=True)
        ps = p_ref[:, 2048:2176].astype(f32)
        lane = lax.broadcasted_iota(jnp.int32, ps.shape, 1)
        dsm = dsm_ref[:, 0:128]
        for hh in range(1, N_GDN):
            dsm = dsm + dsm_ref[:, hh * 128:(hh + 1) * 128]
        beta = _sigmoid(ps)
        xa = ps + dt_ref[...]
        nea = -jnp.exp(al_ref[...])
        dpa = dsm * nea * _sigmoid(xa)
        dp_ref[:, 2048:2176] = _b(jnp.where(lane < 8, dsm * beta * (1.0 - beta), jnp.where(lane < 16, dpa, 0.0)))
        amask = (lane >= 8) & (lane < 16)
        dsml_ref[0:1, :] += jnp.sum(jnp.where(amask, dsm * nea * _softplus(xa), 0.0), axis=0, keepdims=True)
        dsml_ref[1:2, :] += jnp.sum(jnp.where(amask, dpa, 0.0), axis=0, keepdims=True)

    cur1024 = pl.BlockSpec((tT, 1024), lambda i: (i, 0))
    return pl.pallas_call(
        body, name="mixc_pre_bwd", grid=(nt,),
        in_specs=[pl.BlockSpec((tT, REG_C), lambda i: (i, OFF_C // REG_C)), prev_spec, next_spec,
                  pl.BlockSpec((4, 2048), lambda i: (0, 0)),
                  pl.BlockSpec((1, 128), lambda i: (0, 0)), pl.BlockSpec((1, 128), lambda i: (0, 0)),
                  cur1024, n1024, cur1024, n1024, cur1024, n1024, cur1024, pl.BlockSpec(memory_space=pl.ANY)],
        out_specs=[pl.BlockSpec((tT, REG_C), lambda i: (i, OFF_C // REG_C)),
                   pl.BlockSpec((8, 2048), lambda i: (0, 0)), pl.BlockSpec((8, 128), lambda i: (0, 0))],
        out_shape=[jax.ShapeDtypeStruct((T, NP), bf16),
                   jax.ShapeDtypeStruct((8, 2048), f32), jax.ShapeDtypeStruct((8, 128), f32)],
        input_output_aliases={13: 0},
        compiler_params=_cp(("arbitrary",)),
    )(p, p, p, cw, alog_l, dtb_l, dq8, dq8, dk8, dk8, dv, dv, dsm8, dp)


def _tri_inverse(m):
    r = lax.broadcasted_iota(jnp.int32, (L, L), 0)
    c = lax.broadcasted_iota(jnp.int32, (L, L), 1)
    eye = (r == c).astype(f32)[None]
    same = lambda w: ((r // w) == (c // w))[None]
    md = jnp.where(same(8), m, 0.0)
    m2 = _bdot_split(md, md)
    m4 = _bdot_split(m2, m2)
    t = _bdot_split(_bdot_split(eye - md, eye + m2), eye + m4)
    for w in (16, 32, 64):
        mo = jnp.where(same(w) & jnp.logical_not(same(w // 2)), m, 0.0)
        t = t - _bdot_split(_bdot_split(t, mo), t)
    return t


def _col_to_row(col, eye):
    return jnp.sum(eye * col, axis=1, keepdims=True)


def _row_to_col(row, eye):
    return jnp.sum(eye * row, axis=2, keepdims=True)


def _gdn_chunk_terms(q, k, v, beta, g, t_inv=None):
    r = lax.broadcasted_iota(jnp.int32, (L, L), 0)
    c = lax.broadcasted_iota(jnp.int32, (L, L), 1)
    eye = (r == c).astype(f32)[None]
    causal, strict = (c <= r)[None], (c < r)[None]
    diff = g - _col_to_row(g, eye)
    dec = jnp.exp(jnp.where(causal, diff, 0.0))
    dc = jnp.where(causal, dec, 0.0)
    ds = jnp.where(strict, dec, 0.0)
    eg = jnp.exp(g)
    gl = g[:, L - 1:L, :]
    egl = jnp.exp(gl - g)
    kb = k * beta
    kk = _bdot_nt(_b(k), _b(kb))
    qk = _bdot_nt(_b(q), _b(kb))
    m = kk * ds
    aqk = qk * dc
    if t_inv is None:
        t_inv = _tri_inverse(m)
    tb = _b(t_inv)
    keg = k * eg
    u = _bdot(tb, _b(v))
    w = _bdot(tb, _b(keg))
    ks = kb * egl
    ksw = _bdot_tn(_b(ks), _b(w))
    return dict(eye=eye, causal=causal, strict=strict, dc=dc, ds=ds, eg=eg, gl=gl, egl=egl, kb=kb, kk=kk, qk=qk,
                m=m, aqk=aqk, t=t_inv, u=u, w=w, qi=q * eg, ks=ks, keg=keg, ksw=ksw)


def gdn_fwd(qn, kn, vv, sm):
    T = qn.shape[0]
    tB = min(2 * GDN_BLOCK, T)
    nc = tB // L
    N = T // L

    def body(q_ref, k_ref, v_ref, sm_ref, o_ref, st_ref, ti_ref, s_scr):
        j = pl.program_id(0)

        @pl.when(pl.program_id(1) == 0)
        def _():
            s_scr[...] = jnp.zeros_like(s_scr)
        smv = sm_ref[...]
        lane = lax.broadcasted_iota(jnp.int32, smv.shape, 1)
        q = q_ref[...].reshape(nc, L, HK)
        k = k_ref[...].reshape(nc, L, HK)
        tms, ksus, kswbs, egls = [], [], [], []
        for a in range(2):
            h = 2 * j + a
            beta = jnp.sum(jnp.where(lane == h, smv, 0.0), axis=1, keepdims=True).reshape(nc, L, 1)
            g = jnp.sum(jnp.where(lane == 8 + h, smv, 0.0), axis=1, keepdims=True).reshape(nc, L, 1)
            v = v_ref[:, a * HK:(a + 1) * HK].reshape(nc, L, HK)
            tm = _gdn_chunk_terms(q, k, v, beta, g)
            ti_ref[a] = tm["t"]
            tms.append(tm)
            ksus.append(_bdot_tn(_b(tm["ks"]), _b(tm["u"])))
            kswbs.append(_b(tm["ksw"]))
            egls.append(jnp.exp(tm["gl"]))
        s = [s_scr[0], s_scr[1]]
        states = [[None] * nc, [None] * nc]
        for ci in range(nc):
            for a in range(2):
                states[a][ci] = s[a]
                s[a] = egls[a][ci] * s[a] + (ksus[a][ci] - _dot(kswbs[a][ci], _b(s[a])))
        for a in range(2):
            s_scr[a] = s[a]
            sall = jnp.stack(states[a], axis=0)
            st_ref[a] = sall
            sb = _b(sall)
            tm = tms[a]
            e = tm["u"] - _bdot(_b(tm["w"]), sb)
            o = _bdot(_b(tm["qi"]), sb) + _bdot(_b(tm["aqk"]), _b(e))
            o_ref[:, a * HK:(a + 1) * HK] = o.reshape(tB, HK)

    return pl.pallas_call(
        body, name="gdn_fwd", grid=(N_GDN // 2, T // tB),
        in_specs=[pl.BlockSpec((tB, HK), lambda j, n: (n, j)), pl.BlockSpec((tB, HK), lambda j, n: (n, j)),
                  pl.BlockSpec((tB, 2 * HK), lambda j, n: (n, j)), pl.BlockSpec((tB, 128), lambda j, n: (n, 0))],
        out_specs=[pl.BlockSpec((tB, 2 * HK), lambda j, n: (n, j)),
                   pl.BlockSpec((2, nc, HK, HK), lambda j, n: (j, n, 0, 0)),
                   pl.BlockSpec((2, nc, L, L), lambda j, n: (j, n, 0, 0))],
        out_shape=[jax.ShapeDtypeStruct((T, N_GDN * HK), f32), jax.ShapeDtypeStruct((N_GDN, N, HK, HK), f32),
                   jax.ShapeDtypeStruct((N_GDN, N, L, L), f32)],
        scratch_shapes=[pltpu.VMEM((2, HK, HK), f32)],
        compiler_params=_cp(("parallel", "arbitrary")),
    )(qn, kn, vv, sm)


def gdn_bwd(qn, kn, vv, sm, st, ti, do):
    T = qn.shape[0]
    tB = min(GDN_BLOCK, T)
    nc = tB // L
    nb = T // tB

    def body(q_ref, k_ref, v_ref, sm_ref, st_ref, ti_ref, do_ref, dq_ref, dk_ref, dv_ref, dsm_ref, ds_scr):
        j = pl.program_id(0)

        @pl.when(pl.program_id(1) == 0)
        def _():
            ds_scr[...] = jnp.zeros_like(ds_scr)
        smv = sm_ref[...]
        lane = lax.broadcasted_iota(jnp.int32, smv.shape, 1)
        q = q_ref[...].reshape(nc, L, HK)
        k = k_ref[...].reshape(nc, L, HK)
        kbf, qbf = _b(k), _b(q)
        heads = []
        for a in range(2):
            h = 2 * j + a
            beta = jnp.sum(jnp.where(lane == h, smv, 0.0), axis=1, keepdims=True).reshape(nc, L, 1)
            g = jnp.sum(jnp.where(lane == 8 + h, smv, 0.0), axis=1, keepdims=True).reshape(nc, L, 1)
            v = v_ref[:, a * HK:(a + 1) * HK].reshape(nc, L, HK)
            do = do_ref[:, a * HK:(a + 1) * HK].reshape(nc, L, HK)
            s = st_ref[a]
            tm = _gdn_chunk_terms(q, k, v, beta, g, t_inv=ti_ref[a])
            sb, dob = _b(s), _b(do)
            e = tm["u"] - _bdot(_b(tm["w"]), sb)
            de0 = _bdot_tn(_b(tm["aqk"]), dob)
            ds0 = _bdot_tn(_b(tm["qi"]), dob) - _bdot_tn(_b(tm["w"]), _b(de0))
            heads.append(dict(h=h, beta=beta, tm=tm, s=s, sb=sb, dob=dob, e=e, de0=de0, ds0=ds0, kswb=_b(tm["ksw"]),
                              egl_last=jnp.exp(tm["gl"])))
        dsn = [ds_scr[0], ds_scr[1]]
        dsns = [[None] * nc, [None] * nc]
        for ci in reversed(range(nc)):
            for a, hd in enumerate(heads):
                dsns[a][ci] = dsn[a]
                dsn[a] = hd["ds0"][ci] + (hd["egl_last"][ci] * dsn[a] - _dot_tn(hd["kswb"][ci], _b(dsn[a])))
        rowi = lax.broadcasted_iota(jnp.int32, (nc, L, 1), 1)
        rin = lax.broadcasted_iota(jnp.int32, (tB, 128), 0) % L
        for a, hd in enumerate(heads):
            ds_scr[a] = dsn[a]
            tm, s, sb, dob, e, beta, h = hd["tm"], hd["s"], hd["sb"], hd["dob"], hd["e"], hd["beta"], hd["h"]
            eye, dc, ds_, eg, egl = tm["eye"], tm["dc"], tm["ds"], tm["eg"], tm["egl"]
            kb, u, w, qi, ks = tm["kb"], tm["u"], tm["w"], tm["qi"], tm["ks"]
            eb = _b(e)
            dsp = jnp.stack(dsns[a], axis=0)
            dspb = _b(dsp)
            de = hd["de0"] + _bdot(_b(ks), dspb)
            deb = _b(de)
            dks = _bdot_nt(eb, dspb)
            dqi = _bdot_nt(dob, sb)
            daqk = jnp.where(tm["causal"], _bdot_nt(dob, eb), 0.0)
            dw = -_bdot_nt(deb, sb)
            tb = _b(tm["t"])
            dvv = _bdot_tn(tb, deb)
            dkg = _bdot_tn(tb, _b(dw))
            dm = -jnp.where(tm["strict"], _bdot_nt(_b(dvv), _b(u)) + _bdot_nt(_b(dkg), _b(w)), 0.0)
            x = _b(dm * ds_)
            y = _b(daqk * dc)
            kbb = _b(kb)
            dk = _bdot(x, kbb) + dkg * eg
            dkb = _bdot_tn(x, kbf) + _bdot_tn(y, qbf) + dks * egl
            dq = _bdot(y, kbb) + dqi * eg
            dk = dk + dkb * beta
            dbeta = jnp.sum(dkb * k, axis=-1, keepdims=True)
            z = dm * tm["m"] + daqk * tm["aqk"]
            dg = (jnp.sum(dqi * qi - dks * ks + dkg * tm["keg"], axis=-1, keepdims=True)
                  + jnp.sum(z, axis=-1, keepdims=True) - _row_to_col(jnp.sum(z, axis=1, keepdims=True), eye))
            dgl = (hd["egl_last"] * jnp.sum(jnp.sum(s * dsp, axis=2, keepdims=True), axis=1, keepdims=True)
                   + jnp.sum(jnp.sum(dks * ks, axis=2, keepdims=True), axis=1, keepdims=True))
            dg = dg + jnp.where(rowi == L - 1, dgl, 0.0)
            dla = _chunk_rev_cumsum(jnp.broadcast_to(dg.reshape(tB, 1), (tB, 128)), rin)
            sl = slice(a * HK, (a + 1) * HK)
            dq_ref[:, sl] = dq.reshape(tB, HK)
            dk_ref[:, sl] = dk.reshape(tB, HK)
            dv_ref[:, sl] = dvv.reshape(tB, HK)
            dsm_ref[:, sl] = jnp.where(lane == h, dbeta.reshape(tB, 1), jnp.where(lane == 8 + h, dla, 0.0))

    rev = lambda n: nb - 1 - n
    pair = pl.BlockSpec((tB, 2 * HK), lambda j, n: (rev(n), j))
    return pl.pallas_call(
        body, name="gdn_bwd", grid=(N_GDN // 2, nb),
        in_specs=[pl.BlockSpec((tB, HK), lambda j, n: (rev(n), j)), pl.BlockSpec((tB, HK), lambda j, n: (rev(n), j)),
                  pair, pl.BlockSpec((tB, 128), lambda j, n: (rev(n), 0)),
                  pl.BlockSpec((2, nc, HK, HK), lambda j, n: (j, rev(n), 0, 0)),
                  pl.BlockSpec((2, nc, L, L), lambda j, n: (j, rev(n), 0, 0)), pair],
        out_specs=[pair] * 4,
        out_shape=[jax.ShapeDtypeStruct((T, N_GDN * HK), f32)] * 4,
        scratch_shapes=[pltpu.VMEM((2, HK, HK), f32)],
        compiler_params=_cp(("parallel", "arbitrary")),
    )(qn, kn, vv, sm, st, ti, do)


def _hgrn_prep(bq, bf_, bi, lb):
    tB = bq.shape[0]
    sq, dsq = _silu2(bq)
    sg = _sigmoid(bf_)
    f = lb + (1.0 - lb) * sg
    logf = jnp.log(jnp.maximum(f, MIN_F))
    rin = lax.broadcasted_iota(jnp.int32, (tB, HK), 0) % L
    g = _chunk_cumsum(logf, rin)
    return sq * QK_SCALE, sg, f, 1.0 - f, bi, g, rin, dsq * QK_SCALE


def _hgrn_intra(q, kk, v, g, do=None):
    n = q.shape[0]
    nsub = L // SUB
    bwd = do is not None
    o_rows = [None] * nsub
    if bwd:
        dq_rows = [None] * nsub
        dkk_acc = jnp.zeros_like(kk)
        dv_acc = jnp.zeros_like(v)
    for i in range(1, nsub):
        lo, hi, w = i * SUB, (i + 1) * SUB, i * SUB
        ref = g[:, lo - 1:lo, :]
        eq = jnp.exp(g[:, lo:hi, :] - ref)
        ek = jnp.exp(ref - g[:, :w, :])
        qs = _b(q[:, lo:hi, :] * eq)
        ks = _b(kk[:, :w, :] * ek)
        p = _bdot_nt(qs, ks)
        o_rows[i] = _bdot(_b(p), _b(v[:, :w, :]))
        if bwd:
            dob = _b(do[:, lo:hi, :])
            dp = _b(_bdot_nt(dob, _b(v[:, :w, :])))
            dq_rows[i] = _bdot(dp, ks) * eq
            pad = jnp.zeros((n, L - w, HK), f32)
            dkk_acc = dkk_acc + jnp.concatenate([_bdot_tn(dp, qs) * ek, pad], axis=1)
            dv_acc = dv_acc + jnp.concatenate([_bdot_tn(_b(p), dob), pad], axis=1)
    m = n * nsub
    q4, k4, v4, g4 = (a.reshape(m, SUB, HK) for a in (q, kk, v, g))
    r = lax.broadcasted_iota(jnp.int32, (m, SUB, HK), 1)
    od = jnp.zeros((m, SUB, HK), f32)
    if bwd:
        do4 = do.reshape(m, SUB, HK)
        dqd = jnp.zeros((m, SUB, HK), f32)
        dkd = jnp.zeros((m, SUB, HK), f32)
        dvd = jnp.zeros((m, SUB, HK), f32)
    for j in range(SUB):
        gj, kj, vj = g4[:, j:j + 1, :], k4[:, j:j + 1, :], v4[:, j:j + 1, :]
        ok = r >= j
        e = jnp.where(ok, jnp.exp(g4 - gj), 0.0)
        xq = q4 * e
        pj = jnp.sum(xq * kj, axis=-1, keepdims=True)
        od = od + pj * vj
        if bwd:
            dpj = jnp.sum(do4 * vj, axis=-1, keepdims=True)
            dqd = dqd + dpj * kj * e
            dkd = dkd + jnp.where(r == j, jnp.sum(dpj * xq, axis=1, keepdims=True), 0.0)
            dvd = dvd + jnp.where(r == j, jnp.sum(pj * do4, axis=1, keepdims=True), 0.0)
    od = od.reshape(n, L, HK)
    o = jnp.concatenate([od[:, :SUB, :]] + [od[:, i * SUB:(i + 1) * SUB, :] + o_rows[i] for i in range(1, nsub)], axis=1)
    if not bwd:
        return o
    dqd = dqd.reshape(n, L, HK)
    dq = jnp.concatenate([dqd[:, :SUB, :]] + [dqd[:, i * SUB:(i + 1) * SUB, :] + dq_rows[i] for i in range(1, nsub)], axis=1)
    return o, dq, dkk_acc + dkd.reshape(n, L, HK), dv_acc + dvd.reshape(n, L, HK)


def hgrn_fwd(p, lbs):
    T = p.shape[0]
    tB = min(1024, T)
    nc = tB // L
    N = T // L

    def body(b_ref, lb_ref, o_ref, st_ref, s_scr):
        @pl.when(pl.program_id(1) == 0)
        def _():
            s_scr[...] = jnp.zeros_like(s_scr)
        bv = b_ref[...].astype(f32)
        q, sg, f, kk, v, g, rin, _ = _hgrn_prep(bv[:, 0:HK], bv[:, HK:2 * HK], bv[:, 2 * HK:3 * HK], lb_ref[...])
        q3, k3, v3, g3 = (a.reshape(nc, L, HK) for a in (q, kk, v, g))
        o = _hgrn_intra(q3, k3, v3, g3)
        gl = g3[:, L - 1:L, :]
        qt = _b(q3 * jnp.exp(g3))
        kt = _b(k3 * jnp.exp(gl - g3))
        vb = _b(v3)
        st = s_scr[...]
        for c in range(nc):
            st_ref[c] = st
            o_ref[c * L:(c + 1) * L, :] = o[c] + _dot_nt(qt[c], _b(st))
            st = st * jnp.exp(gl[c]) + _dot_tn(vb[c], kt[c])
        s_scr[...] = st

    return pl.pallas_call(
        body, name="hgrn_fwd", grid=(N_HGRN, T // tB),
        in_specs=[pl.BlockSpec((tB, REG_BH), lambda h, n: (n, OFF_B // REG_BH + h)),
                  pl.BlockSpec((1, HK), lambda h, n: (0, h))],
        out_specs=[pl.BlockSpec((tB, HK), lambda h, n: (n, h)),
                   pl.BlockSpec((None, nc, HK, HK), lambda h, n: (h, n, 0, 0))],
        out_shape=[jax.ShapeDtypeStruct((T, N_HGRN * HK), f32), jax.ShapeDtypeStruct((N_HGRN, N, HK, HK), f32)],
        scratch_shapes=[pltpu.VMEM((HK, HK), f32)],
        compiler_params=_cp(("parallel", "arbitrary")),
    )(p, lbs)


def hgrn_bwd(p, lbs, st, do, dp):
    T = p.shape[0]
    tB = min(256, T)
    nc = tB // L
    nb = T // tB

    def body(b_ref, lb_ref, st_ref, do_ref, dp_in, dp_ref, dlb_ref, ds_scr):
        @pl.when(pl.program_id(1) == 0)
        def _():
            ds_scr[...] = jnp.zeros_like(ds_scr)
            dlb_ref[...] = jnp.zeros_like(dlb_ref)
        lb = lb_ref[...]
        bv = b_ref[...].astype(f32)
        q, sg, f, kk, v, g, rin, dq_dbq = _hgrn_prep(bv[:, 0:HK], bv[:, HK:2 * HK], bv[:, 2 * HK:3 * HK], lb)
        q3, k3, v3, g3 = (a.reshape(nc, L, HK) for a in (q, kk, v, g))
        do3 = do_ref[...].reshape(nc, L, HK)
        dob = _b(do3)
        gl = g3[:, L - 1:L, :]
        egl = jnp.exp(gl)
        eg, egr = jnp.exp(g3), jnp.exp(gl - g3)
        qt, kt = q3 * eg, k3 * egr
        s = st_ref[...]
        ds0 = _bdot_tn(dob, _b(qt))
        dsn = ds_scr[...]
        dsns = [None] * nc
        for c in reversed(range(nc)):
            dsns[c] = dsn
            dsn = ds0[c] + dsn * egl[c]
        ds_scr[...] = dsn
        dsp = jnp.stack(dsns, axis=0)
        dspb = _b(dsp)
        dqt = _bdot(dob, _b(s))
        dkt = _bdot(_b(v3), dspb)
        dv_state = _bdot_nt(_b(kt), dspb)
        dgl = egl * jnp.sum(s * dsp, axis=1, keepdims=True) + jnp.sum(dkt * kt, axis=1, keepdims=True)
        _, dq_i, dkk_i, dv_i = _hgrn_intra(q3, k3, v3, g3, do=do3)
        dq = dq_i + dqt * eg
        dkk = dkk_i + dkt * egr
        dv = dv_i + dv_state
        rowi = lax.broadcasted_iota(jnp.int32, (nc, L, HK), 1)
        dg = q3 * dq - k3 * dkk + jnp.where(rowi == L - 1, dgl, 0.0)
        dlogf = _chunk_rev_cumsum(dg.reshape(tB, HK), rin)
        dkk2 = dkk.reshape(tB, HK)
        df = jnp.where(f > MIN_F, dlogf / f, 0.0) - dkk2
        dlb_ref[...] += jnp.sum(df * (1.0 - sg), axis=0, keepdims=True)
        dp_ref[:, 0:HK] = _b(dq.reshape(tB, HK) * dq_dbq)
        dp_ref[:, HK:2 * HK] = _b(df * (1.0 - lb) * sg * (1.0 - sg))
        dp_ref[:, 2 * HK:3 * HK] = _b(dv.reshape(tB, HK))

    rev = lambda n: nb - 1 - n
    return pl.pallas_call(
        body, name="hgrn_bwd", grid=(N_HGRN, nb),
        in_specs=[pl.BlockSpec((tB, REG_BH), lambda h, n: (rev(n), OFF_B // REG_BH + h)),
                  pl.BlockSpec((1, HK), lambda h, n: (0, h)),
                  pl.BlockSpec((None, nc, HK, HK), lambda h, n: (h, rev(n), 0, 0)),
                  pl.BlockSpec((tB, HK), lambda h, n: (rev(n), h)), pl.BlockSpec(memory_space=pl.ANY)],
        out_specs=[pl.BlockSpec((tB, REG_BH), lambda h, n: (rev(n), OFF_B // REG_BH + h)),
                   pl.BlockSpec((1, HK), lambda h, n: (0, h))],
        out_shape=[jax.ShapeDtypeStruct((T, NP), bf16), jax.ShapeDtypeStruct((1, N_HGRN * HK), f32)],
        input_output_aliases={4: 0},
        scratch_shapes=[pltpu.VMEM((HK, HK), f32)],
        compiler_params=_cp(("parallel", "arbitrary")),
    )(p, lbs, st, do, dp)


def _headnorm_fwd(o, z, w, nheads):
    outs, parts = [], []
    for hh in range(nheads):
        sl = slice(hh * HK, (hh + 1) * HK)
        oh = o[:, sl]
        r = lax.rsqrt(jnp.mean(oh * oh, axis=-1, keepdims=True) + NORM_EPS)
        on = oh * r
        sz, dsz = _silu2(z[:, sl])
        outs.append(on * w * sz)
        parts.append((r, on, sz, dsz))
    return jnp.concatenate(outs, axis=1), parts


def _headnorm_bwd(parts, w, dy):
    dos, dzs = [], []
    dw = jnp.zeros((1, HK), f32)
    for hh, (r, on, sz, dsz) in enumerate(parts):
        dyh = dy[:, hh * HK:(hh + 1) * HK]
        dn = dyh * sz * w
        dos.append(r * (dn - on * jnp.mean(dn * on, axis=-1, keepdims=True)))
        dzs.append(dyh * on * w * dsz)
        dw = dw + jnp.sum(dyh * sz * on, axis=0, keepdims=True)
    return jnp.concatenate(dos, axis=1), jnp.concatenate(dzs, axis=1), dw


def _merge_specs(tT, l):
    row = lambda w, cb=0: pl.BlockSpec((tT, w), lambda i, cb=cb: (i, cb))
    full = lambda r, c: pl.BlockSpec((r, c), lambda i: (0, 0))
    layer = lambda r, c: pl.BlockSpec((None, r, c), lambda i: (l, 0, 0))
    return row, full, layer


def merge_fwd(x, p, ya, ob, oc, hw, gw, bg, woa, wob, woc, wo, l):
    T = x.shape[0]
    tT = min(256, T)
    row, full, layer = _merge_specs(tT, l)

    def body(x_ref, pm_ref, ya_ref, ob_ref, oc_ref, hw_ref, gw_ref, bg_ref,
             woa_ref, wob_ref, woc_ref, wo_ref, out_ref):
        yb = _b(_headnorm_fwd(ob_ref[...], pm_ref[:, M_BZ:M_G].astype(f32), hw_ref[...], N_HGRN)[0])
        yc = _b(_headnorm_fwd(oc_ref[...], pm_ref[:, M_CZ:REG_M].astype(f32), gw_ref[...], N_GDN)[0])
        gates = _sigmoid(pm_ref[:, M_G:M_CZ].astype(f32) + bg_ref[...])
        merged = (gates[:, 0:D] * _dot(ya_ref[...], woa_ref[...]) + gates[:, D:2 * D] * _dot(yb, wob_ref[...])
                  + gates[:, 2 * D:3 * D] * _dot(yc, woc_ref[...]))
        out_ref[...] = x_ref[...] + _dot(_b(merged), wo_ref[...])

    return pl.pallas_call(
        body, name="merge_fwd", grid=(T // tT,),
        in_specs=[row(D), row(REG_M, OFF_M // REG_M),
                  row(512), row(512), row(1024), full(1, HK), full(1, HK), full(1, 3 * D),
                  layer(512, D), layer(512, D), layer(D, D), layer(D, D)],
        out_specs=row(D),
        out_shape=jax.ShapeDtypeStruct((T, D), f32),
        compiler_params=_cp(("parallel",)),
    )(x, p, ya, ob, oc, hw, gw, bg, woa, wob, woc, wo)


def merge_bwd(dxo, p, ya, ob, oc, hw, gw, bg, woa, wob, woc, wo, l):
    T = dxo.shape[0]
    tT = min(256, T)
    row, full, layer = _merge_specs(tT, l)

    def body(dx_ref, pm_ref, ya_ref, ob_ref, oc_ref, hw_ref, gw_ref, bg_ref,
             woa_ref, wob_ref, woc_ref, wo_ref,
             dya_ref, dob_ref, doc_ref, dp_ref, mg_ref, dy3_ref, yb_ref, yc_ref,
             dbg_ref, dhw_ref, dgw_ref):
        @pl.when(pl.program_id(0) == 0)
        def _():
            dbg_ref[...] = jnp.zeros_like(dbg_ref)
            dhw_ref[...] = jnp.zeros_like(dhw_ref)
            dgw_ref[...] = jnp.zeros_like(dgw_ref)
        ob, oc, bz, cz = ob_ref[...], oc_ref[...], pm_ref[:, M_BZ:M_G].astype(f32), pm_ref[:, M_CZ:REG_M].astype(f32)
        hw_, gw_ = hw_ref[...], gw_ref[...]
        yb, parts_b = _headnorm_fwd(ob, bz, hw_, N_HGRN)
        yc, parts_c = _headnorm_fwd(oc, cz, gw_, N_GDN)
        yb, yc = _b(yb), _b(yc)
        yb_ref[...] = yb
        yc_ref[...] = yc
        gates = _sigmoid(pm_ref[:, M_G:M_CZ].astype(f32) + bg_ref[...])
        ys = (_dot(ya_ref[...], woa_ref[...]), _dot(yb, wob_ref[...]), _dot(yc, woc_ref[...]))
        dmerged = _dot_nt(_b(dx_ref[...]), wo_ref[...])
        merged = jnp.zeros_like(dmerged)
        dys = []
        for i in range(3):
            gi = gates[:, i * D:(i + 1) * D]
            merged = merged + gi * ys[i]
            dyi = _b(dmerged * gi)
            dys.append(dyi)
            dy3_ref[:, i * D:(i + 1) * D] = dyi
            dgp = dmerged * ys[i] * gi * (1.0 - gi)
            dp_ref[:, M_G + i * D:M_G + (i + 1) * D] = _b(dgp)
            dbg_ref[:, i * D:(i + 1) * D] += jnp.sum(dgp, axis=0, keepdims=True)
        mg_ref[...] = _b(merged)
        dya_ref[...] = _dot_nt(dys[0], woa_ref[...])
        dob, dbz, dhw = _headnorm_bwd(parts_b, hw_, _dot_nt(dys[1], wob_ref[...]))
        doc, dcz, dgw = _headnorm_bwd(parts_c, gw_, _dot_nt(dys[2], woc_ref[...]))
        dob_ref[...] = dob
        doc_ref[...] = doc
        dp_ref[:, M_BZ:M_G] = _b(dbz)
        dp_ref[:, M_CZ:REG_M] = _b(dcz)
        dhw_ref[...] += dhw
        dgw_ref[...] += dgw

    sd = jax.ShapeDtypeStruct
    return pl.pallas_call(
        body, name="merge_bwd", grid=(T // tT,),
        in_specs=[row(D), row(REG_M, OFF_M // REG_M),
                  row(512), row(512), row(1024), full(1, HK), full(1, HK), full(1, 3 * D),
                  layer(512, D), layer(512, D), layer(D, D), layer(D, D)],
        out_specs=[row(512), row(512), row(1024), row(REG_M, OFF_M // REG_M), row(D), row(3 * D), row(512),
                   row(1024), full(1, 3 * D), full(1, HK), full(1, HK)],
        out_shape=[sd((T, 512), f32), sd((T, 512), f32), sd((T, 1024), f32), sd((T, NP), bf16),
                   sd((T, D), bf16), sd((T, 3 * D), bf16), sd((T, 512), bf16),
                   sd((T, 1024), bf16), sd((1, 3 * D), f32), sd((1, HK), f32), sd((1, HK), f32)],
        compiler_params=_cp(("arbitrary",)),
    )(dxo, p, ya, ob, oc, hw, gw, bg, woa, wob, woc, wo)


def layer_fwd(x, w):
    l = w["l"]
    p, h = inproj_fwd(x, w["norm_w"], w["w_in"])
    ya = mixa_fwd(p, w["conv_a"])
    qn, kn, vv, sm = mixc_pre_fwd(p, w["conv_c"], w["alog_l"], w["dtb_l"])
    oc, st_c, ti = gdn_fwd(qn, kn, vv, sm)
    ob, st_b = hgrn_fwd(p, w["lbs"])
    if "late" in w:
        w.update(w.pop("late")((ya, oc, ob)))
    xo = merge_fwd(x, p, ya, ob, oc, w["hgrn_norm_w"], w["gdn_norm_w"], w["b_gate"],
                   w["w_out_a"], w["w_out_b"], w["w_out_c"], w["w_o"], l)
    saved = dict(x=x, p=p, h=h, ya=ya, qn=qn, kn=kn, vv=vv, sm=sm, oc=oc, st_c=st_c, ti=ti, ob=ob, st_b=st_b)
    return xo, saved


OUT_MATS = (("w_out_a", "cols"), ("w_out_b", "cols"), ("w_out_c", "rows"), ("w_o", "rows"))


def layer_bwd(dxo, w, s, chip):
    p, l = s["p"], w["l"]
    (dya, dob, doc, dp, merged, dy3, yb, yc, dbg, dhw, dgw) = merge_bwd(
        dxo, p, s["ya"], s["ob"], s["oc"], w["hgrn_norm_w"], w["gdn_norm_w"], w["b_gate"],
        w["w_out_a"], w["w_out_b"], w["w_out_c"], w["w_o"], l)
    full = {"w_o": matmul_tn(merged, dxo, "dw_o", with_bf16=True),
            "w_out_a": matmul_tn(s["ya"], dy3, "dw_out_a", n=D, b_col0=0, with_bf16=True),
            "w_out_b": matmul_tn(yb, dy3, "dw_out_b", n=D, b_col0=D, with_bf16=True),
            "w_out_c": matmul_tn(yc, dy3, "dw_out_c", n=D, b_col0=2 * D, with_bf16=True)}
    out_kinds = [k for _, k in OUT_MATS]
    sent_out, token = exchange_start([full[n][1] for n, _ in OUT_MATS], out_kinds, f"grads_out_start{l}")
    dp, dlbs = hgrn_bwd(p, w["lbs"] + token[0:1, 0:1], s["st_b"], dob, dp)
    dq8, dk8, dvv, dsm8 = gdn_bwd(s["qn"], s["kn"], s["vv"], s["sm"], s["st_c"], s["ti"], doc)
    dp, dcc, dsmall = mixc_pre_bwd(p, w["conv_c"], w["alog_l"], w["dtb_l"], dq8, dk8, dvv, dsm8, dp)
    dp, dca = mixa_bwd(p, w["conv_a"], dya, dp)
    gf_win, gb_win = win_from_padded(matmul_tn(s["h"], dp, "dw_in"))
    sent_in, token = exchange_start([gb_win], ["slot"], f"grads_in_start{l}")
    dx, dnw = inproj_bwd(dp, w["w_in"], s["x"], w["norm_w"] + token[0:1, 0:1], dxo)
    recv_out = exchange_wait(sent_out, out_kinds, dx, f"grads_out_wait{l}")
    recv_in = exchange_wait(sent_in, ["slot"], dx, f"grads_in_wait{l}")
    half = {"w_in": partial_sum(gf_win, "slot", recv_in[0], chip, "psum_w_in", transposed=True)}
    for (n, kind), r in zip(OUT_MATS, recv_out):
        half[n] = partial_sum(full[n][0], kind, r, chip, "psum_" + n)
    small = dict(norm_w=dnw, b_gate=dbg, hgrn_norm_w=dhw, gdn_norm_w=dgw, lbs=dlbs, conv_a=dca[0:3], conv_c=dcc[0:4],
                 a_log=dsmall[0:1, 8:16], dt_bias=dsmall[1:2, 8:16])
    return dx, small, half


def lbs_fwd(lb):
    def body(lb_ref, o_ref):
        l0, l1 = lb_ref[0:1, :], lb_ref[1:2, :]
        mx = jnp.maximum(l0, l1)
        e0, e1 = jnp.exp(l0 - mx), jnp.exp(l1 - mx)
        o_ref[0:1, :] = jnp.zeros_like(l0)
        o_ref[1:2, :] = e1 / (e0 + e1)
    return pl.pallas_call(body, name="lbs_fwd", out_shape=jax.ShapeDtypeStruct(lb.shape, f32))(lb)


def _adam_math(w, g, m, v):
    mn = ADAM_B1 * m + (1.0 - ADAM_B1) * g
    vn = ADAM_B2 * v + (1.0 - ADAM_B2) * (g * g)
    mh = mn / (1.0 - ADAM_B1 ** ADAM_STEP)
    vh = vn / (1.0 - ADAM_B2 ** ADAM_STEP)
    return -ADAM_LR * (mh / (jnp.sqrt(vh) + ADAM_EPS) + ADAM_WD * w), mn, vn


def adam(w, g, m, v, name):
    R, C = w.shape
    tr = 256 if R % 256 == 0 else R

    def body(w_ref, g_ref, m_ref, v_ref, d_ref, mo_ref, vo_ref):
        d, mn, vn = _adam_math(w_ref[...], g_ref[...], m_ref[...], v_ref[...])
        d_ref[...] = d
        mo_ref[...] = mn
        vo_ref[...] = vn

    spec = pl.BlockSpec((tr, C), lambda i: (i, 0))
    return pl.pallas_call(
        body, name=name, grid=(R // tr,), in_specs=[spec] * 4, out_specs=[spec] * 3,
        out_shape=[jax.ShapeDtypeStruct((R, C), f32)] * 3, compiler_params=_cp(("parallel",)),
    )(w, g, m, v)


def adam_pair(h, hs, w, m, v, name):
    _, R, C = w.shape
    cp = h[0].shape[1]
    tr = 128 if R % 128 == 0 else R
    nt = R // tr

    def body(h0_ref, h1_ref, s0_ref, s1_ref, w_ref, m_ref, v_ref, g_ref, d_ref, mo_ref, vo_ref):
        def update(h_ref, s_ref):
            g = (h_ref[...] + s_ref[...])[:, :C]
            d, mn, vn = _adam_math(w_ref[...], g, m_ref[...], v_ref[...])
            g_ref[...] = g
            d_ref[...] = d
            mo_ref[...] = mn
            vo_ref[...] = vn

        @pl.when(pl.program_id(0) == 0)
        def _():
            update(h0_ref, s0_ref)

        @pl.when(pl.program_id(0) == 1)
        def _():
            update(h1_ref, s1_ref)

    h0spec = pl.BlockSpec((tr, cp), lambda l, i: (jnp.where(l == 0, i, nt - 1), 0))
    h1spec = pl.BlockSpec((tr, cp), lambda l, i: (jnp.where(l == 1, i, 0), 0))
    spec = pl.BlockSpec((None, tr, C), lambda l, i: (l, i, 0))
    return pl.pallas_call(
        body, name=name, grid=(2, nt), in_specs=[h0spec, h1spec, h0spec, h1spec, spec, spec, spec],
        out_specs=[spec] * 4, out_shape=[jax.ShapeDtypeStruct(w.shape, f32)] * 4,
        compiler_params=_cp(("arbitrary", "arbitrary")),
    )(h[0], h[1], hs[0], hs[1], w, m, v)


_SMALL = (("norm_w", 2 * D), ("b_gate", 6 * D), ("lower_bounds", None), ("hgrn_norm_w", 2 * HK),
          ("gdn_norm_w", 2 * HK), ("a_log", 16), ("dt_bias", 16), ("final_norm_w", D), ("loss", None))
_CONV = (("conv_a", 2 * 3 * 512), ("conv_c", 2 * 4 * 2048))


def _small_rows(n):
    return 16 if n is None else -(-n // 1024) * 8


LB_ROW = sum(_small_rows(n) for _, n in _SMALL[:2])
ADAM_ROWS = sum(_small_rows(n) for _, n in _SMALL)
SMALL_ROWS = ADAM_ROWS + sum(_small_rows(n) for _, n in _CONV)


def small_update(parts, wp, mp, vp):
    def body(p_ref, w_ref, m_ref, v_ref, g_ref, d_ref, mo_ref, vo_ref):
        gs = p_ref[0]
        for i in range(1, 8):
            gs = gs + p_ref[i]
        w = w_ref[...]
        l0, l1 = w[LB_ROW:LB_ROW + 8], w[LB_ROW + 8:LB_ROW + 16]
        mx = jnp.maximum(l0, l1)
        e0, e1 = jnp.exp(l0 - mx), jnp.exp(l1 - mx)
        p0, p1 = e0 / (e0 + e1), e1 / (e0 + e1)
        dl1 = gs[LB_ROW + 8:LB_ROW + 16]
        s = p1 * dl1
        g = jnp.concatenate([gs[0:LB_ROW], -p0 * s, p1 * dl1 - p1 * s, gs[LB_ROW + 16:ADAM_ROWS]], axis=0)
        d, mn, vn = _adam_math(w, g, m_ref[...], v_ref[...])
        g_ref[0:ADAM_ROWS, :] = g
        g_ref[ADAM_ROWS:, :] = gs[ADAM_ROWS:]
        d_ref[...] = d
        mo_ref[...] = mn
        vo_ref[...] = vn
    sd = jax.ShapeDtypeStruct
    return pl.pallas_call(body, name="small_update",
                          out_shape=[sd((SMALL_ROWS, 128), f32)] + [sd((ADAM_ROWS, 128), f32)] * 3)(parts, wp, mp, vp)


def partial_sum(own, kind, recv, chip, name, transposed=False):
    _, r, c = recv.shape
    tr = 256 if r % 256 == 0 else r

    def body(chip_ref, o_ref, r_ref, out_ref):
        s = ((o_ref[...] + r_ref[0].astype(f32)) + r_ref[1].astype(f32)) + r_ref[2].astype(f32)
        out_ref[...] = s.T if transposed else s

    own_spec = {"slot": pl.BlockSpec((None, tr, c), lambda i, chip: (chip[0], i, 0)),
                "cols": pl.BlockSpec((tr, c), lambda i, chip: (i, chip[0])),
                "rows": pl.BlockSpec((tr, c), lambda i, chip: (chip[0] * (r // tr) + i, 0))}[kind]
    out_spec = pl.BlockSpec((c, tr), lambda i, chip: (0, i)) if transposed else pl.BlockSpec((tr, c), lambda i, chip: (i, 0))
    return pl.pallas_call(
        body, name=name,
        grid_spec=pltpu.PrefetchScalarGridSpec(
            num_scalar_prefetch=1, grid=(r // tr,),
            in_specs=[own_spec, pl.BlockSpec((3, tr, c), lambda i, chip: (0, i, 0))], out_specs=out_spec),
        out_shape=jax.ShapeDtypeStruct((c, r) if transposed else (r, c), f32), compiler_params=_cp(("arbitrary",)),
    )(chip, own, recv)


def adam_pair_t(h, hs, wt, mt, vt, name):
    C, _, R = wt.shape
    tc = 128

    def body(h0_ref, h1_ref, s0_ref, s1_ref, w_ref, m_ref, v_ref, g_ref, d_ref, mo_ref, vo_ref):
        g = jnp.stack([h0_ref[...] + s0_ref[...], h1_ref[...] + s1_ref[...]], axis=1)
        d, mn, vn = _adam_math(w_ref[...], g, m_ref[...], v_ref[...])
        g_ref[...] = g
        d_ref[...] = d
        mo_ref[...] = mn
        vo_ref[...] = vn

    hspec = pl.BlockSpec((tc, R), lambda i: (i, 0))
    spec = pl.BlockSpec((tc, 2, R), lambda i: (i, 0, 0))
    return pl.pallas_call(
        body, name=name, grid=(pl.cdiv(C, tc),), in_specs=[hspec] * 4 + [spec] * 3, out_specs=[spec] * 4,
        out_shape=[jax.ShapeDtypeStruct(wt.shape, f32)] * 4, compiler_params=_cp(("parallel",)),
    )(h[0], h[1], hs[0], hs[1], wt, mt, vt)


MESH = pl.DeviceIdType.MESH
_HBM = pl.BlockSpec(memory_space=pltpu.HBM)


def _place():
    return lax.axis_index("x"), lax.axis_index("y"), lax.axis_index("c")


def weight_gather(arrs):
    n = len(arrs)

    def body(*refs):
        x_refs, out_refs = refs[:n], refs[n:2 * n]
        send_sems, recv_sems, local_sems = refs[2 * n:]
        x, y, c = _place()
        me, sibling = (x, y, c), (x, y, 1 - c)
        chips = [(1 - x, y), (x, 1 - y), (1 - x, 1 - y)]

        def copy(a, k, block, to, own_src=False):
            px, py, pc = block
            dst = out_refs[a].at[2 * px + py, pc]
            return pltpu.make_async_remote_copy(
                src_ref=x_refs[a].at[c] if own_src else dst, dst_ref=dst,
                send_sem=send_sems.at[7 * a + k], recv_sem=recv_sems.at[7 * a + k], device_id=to, device_id_type=MESH)

        mine = [pltpu.make_async_copy(x_refs[a].at[c], out_refs[a].at[2 * x + y, c], local_sems.at[a])
                for a in range(n)]
        for cp in mine:
            cp.start()
        first = []
        for a in range(n):
            first.append(copy(a, 0, me, sibling, own_src=True))
            first += [copy(a, 1 + j, me, (*chip, c), own_src=True) for j, chip in enumerate(chips)]
        for cp in first:
            cp.start()
        passed = []
        for j, chip in enumerate(chips):
            for a in range(n):
                copy(a, 1 + j, (*chip, c), me).wait_recv()
                fwd = copy(a, 4 + j, (*chip, c), sibling)
                fwd.start()
                passed.append(fwd)
        for a in range(n):
            copy(a, 0, sibling, me).wait_recv()
            for j, chip in enumerate(chips):
                copy(a, 4 + j, (*chip, 1 - c), me).wait_recv()
        for cp in first + passed:
            cp.wait_send()
        for cp in mine:
            cp.wait()

    return pl.pallas_call(
        body, name="weight_gather", in_specs=[_HBM] * n, out_specs=[_HBM] * n,
        out_shape=[jax.ShapeDtypeStruct((N_CHIPS,) + a.shape, a.dtype) for a in arrs],
        scratch_shapes=[pltpu.SemaphoreType.DMA((7 * n,)), pltpu.SemaphoreType.DMA((7 * n,)),
                        pltpu.SemaphoreType.DMA((n,))],
    )(*arrs)


SHARD_W = 256


_SEM = pl.BlockSpec(memory_space=pltpu.SEMAPHORE)
_EFFECT = pltpu.SideEffectType.DATAFLOW_SIDE_EFFECTING


def _landing_shape(a, kind):
    if kind == "all":
        return (N_CHIPS,) + a.shape
    if kind == "slot":
        return (3,) + a.shape[1:]
    return (3,) + ((a.shape[0], SHARD_W) if kind == "cols" else (SHARD_W, a.shape[1]))


def _shard_copies(src_refs, land_refs, kinds, send_sems, recv_sems):
    x, y, c = _place()
    copies = []
    for a, (src, land, kind) in enumerate(zip(src_refs, land_refs, kinds)):
        for j, (px, py) in enumerate(((1 - x, y), (x, 1 - y), (1 - x, 1 - y))):
            q = 2 * px + py
            lo = pl.multiple_of(q * SHARD_W, SHARD_W)
            part = {"slot": lambda: src.at[q], "cols": lambda: src.at[:, pl.ds(lo, SHARD_W)],
                    "rows": lambda: src.at[pl.ds(lo, SHARD_W), :], "all": lambda: src}[kind]()
            k = 3 * a + j
            copies.append(pltpu.make_async_remote_copy(
                src_ref=part, dst_ref=land.at[2 * x + y] if kind == "all" else land.at[j],
                send_sem=send_sems.at[k], recv_sem=recv_sems.at[k], device_id=(px, py, c), device_id_type=MESH))
    return copies


def exchange_start(srcs, kinds, name, after=None):
    n = len(srcs)
    lands = [lax.empty(_landing_shape(a, k), a.dtype) for a, k in zip(srcs, kinds)]
    extra = [] if after is None else [after]

    def body(*refs):
        src_refs, land_refs, token = refs[:n], refs[n:2 * n], refs[-1]
        send_sems, recv_sems = refs[2 * n + len(extra)], refs[2 * n + len(extra) + 1]
        for cp in _shard_copies(src_refs, land_refs, kinds, send_sems, recv_sems):
            cp.start()
        token[...] = jnp.zeros_like(token)

    both = list(srcs) + lands
    out = pl.pallas_call(
        body, name=name,
        out_shape=(pltpu.SemaphoreType.DMA((3 * n,)), pltpu.SemaphoreType.DMA((3 * n,)),
                   *[pltpu.HBM(a.shape, a.dtype) for a in both], jax.ShapeDtypeStruct((8, 128), f32)),
        in_specs=[_HBM] * (2 * n) + [pl.BlockSpec(memory_space=pl.ANY)] * len(extra),
        out_specs=(_SEM, _SEM, *[_HBM] * (2 * n), pl.BlockSpec(memory_space=pltpu.VMEM)),
        input_output_aliases={i: 2 + i for i in range(2 * n)},
        compiler_params=pltpu.CompilerParams(has_side_effects=_EFFECT),
    )(*[pltpu.with_memory_space_constraint(a, pltpu.HBM) for a in both], *extra)
    return (out[0], out[1], out[2:2 + 2 * n]), out[-1]


def exchange_wait(handle, kinds, after, name):
    send_sems, recv_sems, both = handle
    n = len(kinds)
    after = after if isinstance(after, tuple) else (after,)

    def body(*refs):
        src_refs, land_refs, s_sems, r_sems = refs[:n], refs[n:2 * n], refs[2 * n], refs[2 * n + 1]
        for cp in _shard_copies(src_refs, land_refs, kinds, s_sems, r_sems):
            cp.wait_send()
            cp.wait_recv()

    out = pl.pallas_call(
        body, name=name, out_shape=tuple(pltpu.HBM(a.shape, a.dtype) for a in both),
        in_specs=[_HBM] * (2 * n) + [_SEM, _SEM] + [pl.BlockSpec(memory_space=pl.ANY)] * len(after),
        out_specs=tuple([_HBM] * (2 * n)), input_output_aliases={i: i for i in range(2 * n)},
        compiler_params=pltpu.CompilerParams(has_side_effects=_EFFECT),
    )(*both, send_sems, recv_sems, *after)
    return out[n:]


def final_exchange(hs, small):
    n = len(hs)
    S = small.shape[0]

    def body(*refs):
        h_refs, sm_ref, out_refs, smalls_ref = refs[:n], refs[n], refs[n + 1:2 * n + 1], refs[2 * n + 1]
        send_sems, recv_sems, local_sem = refs[2 * n + 2:]
        x, y, c = _place()
        my_slot = smalls_ref.at[4 * x + 2 * y + c]
        mine = pltpu.make_async_copy(sm_ref, my_slot, local_sem)
        mine.start()
        copies = [pltpu.make_async_remote_copy(src_ref=h_refs[a], dst_ref=out_refs[a], send_sem=send_sems.at[a],
                                               recv_sem=recv_sems.at[a], device_id=(x, y, 1 - c), device_id_type=MESH)
                  for a in range(n)]
        for mask in range(1, 8):
            fx, fy, fc = (mask >> 2) & 1, (mask >> 1) & 1, mask & 1
            peer = ((1 - x) if fx else x, (1 - y) if fy else y, (1 - c) if fc else c)
            copies.append(pltpu.make_async_remote_copy(
                src_ref=sm_ref, dst_ref=my_slot, send_sem=send_sems.at[n - 1 + mask], recv_sem=recv_sems.at[n - 1 + mask],
                device_id=peer, device_id_type=MESH))
        for cp in copies:
            cp.start()
        for cp in copies:
            cp.wait_recv()
        for cp in copies:
            cp.wait_send()
        mine.wait()

    sd = jax.ShapeDtypeStruct
    out = pl.pallas_call(
        body, name="final_exchange", in_specs=[_HBM] * (n + 1), out_specs=[_HBM] * (n + 1),
        out_shape=[sd(h.shape, h.dtype) for h in hs] + [sd((8, S, 128), f32)],
        scratch_shapes=[pltpu.SemaphoreType.DMA((n + 7,)), pltpu.SemaphoreType.DMA((n + 7,)),
                        pltpu.SemaphoreType.DMA],
    )(*hs, small)
    return out[:n], out[n]


N_CHIPS = 4
SHARD_COLS = N_ORIG // N_CHIPS


SHARD_PAD = 2688
_COL_SEGMENTS = (
    ((0, 2048, OFF_A),)
    + tuple((2048 + 512 * j + HK * h, 2048 + 512 * j + HK * (h + 1), OFF_B + REG_BH * h + HK * j)
            for j in range(3) for h in range(N_HGRN))
    + ((3584, 4096, OFF_M + M_BZ), (4096, 6144, OFF_C), (6144, 6160, OFF_C + 2048), (6160, 7184, OFF_M + M_CZ),
       (7184, N_ORIG, OFF_M + M_G)))


def _shard_pieces():
    pieces = []
    for lo, hi, dst in _COL_SEGMENTS:
        for p in range(N_CHIPS):
            a, b = max(lo, p * SHARD_COLS), min(hi, (p + 1) * SHARD_COLS)
            if a < b:
                pieces.append((p, a - p * SHARD_COLS, dst + a - lo, b - a))
    return pieces


def win_cast_pad(wt):
    tc = 128

    def body(x_ref, o0_ref, o1_ref):
        col = pl.program_id(0) * tc + lax.broadcasted_iota(jnp.int32, (tc, 1), 0)
        for l, o_ref in enumerate((o0_ref, o1_ref)):
            o_ref[...] = _b(jnp.where(col < SHARD_COLS, x_ref[:, l, :], 0.0).T)

    spec = pl.BlockSpec((D, tc), lambda i: (0, i))
    return pl.pallas_call(
        body, name="win_cast_pad", grid=(SHARD_PAD // tc,),
        in_specs=[pl.BlockSpec((tc, 2, D), lambda i: (i, 0, 0))], out_specs=[spec, spec],
        out_shape=[jax.ShapeDtypeStruct((D, SHARD_PAD), bf16)] * 2, compiler_params=_cp(("parallel",)),
    )(wt)


def win_to_padded(w4, name):
    tr = 256
    pieces = _shard_pieces()

    def body(a_ref, o_ref):
        o_ref[...] = jnp.zeros((tr, NP), bf16)
        for p, j0, c0, n in pieces:
            o_ref[:, c0:c0 + n] = a_ref[p, :, j0:j0 + n]

    return pl.pallas_call(
        body, name=name, grid=(D // tr,),
        in_specs=[pl.BlockSpec((N_CHIPS, tr, SHARD_PAD), lambda i: (0, i, 0))],
        out_specs=pl.BlockSpec((tr, NP), lambda i: (i, 0)),
        out_shape=jax.ShapeDtypeStruct((D, NP), bf16), compiler_params=_cp(("parallel",)),
    )(w4)


def win_from_padded(dw):
    tr = 128
    pieces = _shard_pieces()

    def body(d_ref, of_ref, ob_ref):
        for p in range(N_CHIPS):
            of_ref[p, :, SHARD_COLS:] = jnp.zeros((tr, SHARD_PAD - SHARD_COLS), f32)
            ob_ref[p, :, SHARD_COLS:] = jnp.zeros((tr, SHARD_PAD - SHARD_COLS), bf16)
        for p, j0, c0, n in pieces:
            v = d_ref[:, c0:c0 + n]
            of_ref[p, :, j0:j0 + n] = v
            ob_ref[p, :, j0:j0 + n] = _b(v)

    out_spec = pl.BlockSpec((N_CHIPS, tr, SHARD_PAD), lambda i: (0, i, 0))
    return pl.pallas_call(
        body, name="win_from_padded", grid=(D // tr,),
        in_specs=[pl.BlockSpec((tr, NP), lambda i: (i, 0))], out_specs=[out_spec, out_spec],
        out_shape=[jax.ShapeDtypeStruct((N_CHIPS, D, SHARD_PAD), f32),
                   jax.ShapeDtypeStruct((N_CHIPS, D, SHARD_PAD), bf16)],
        compiler_params=_cp(("parallel",)),
    )(dw)


def _rows128(a):
    flat = a.reshape(-1)
    total = -(-flat.shape[0] // 1024) * 1024
    return jnp.pad(flat, (0, total - flat.shape[0])).reshape(total // 128, 128)


def _lb_rows(lb):
    return jnp.pad(lb.reshape(2, 4, 128), ((0, 0), (0, 4), (0, 0))).reshape(16, 128)


def _pack_small(v, with_conv):
    rows = []
    for name, n in _SMALL + (_CONV if with_conv else ()):
        if name == "lower_bounds":
            rows.append(_lb_rows(v[name]))
        elif name == "loss":
            rows.append(jnp.broadcast_to(v[name], (16, 128)) if name in v else jnp.zeros((16, 128), f32))
        else:
            rows.append(_rows128(v[name]))
    return jnp.concatenate(rows, axis=0)


def _unpack_small(p, shapes, with_conv):
    out, row = {}, 0
    for name, n in _SMALL + (_CONV if with_conv else ()):
        nrows = _small_rows(n)
        blk = p[row:row + nrows]
        if name == "lower_bounds":
            out[name] = blk.reshape(2, 8, 128)[:, :4].reshape(2, 512)
        elif name == "loss":
            out[name] = blk[0, 0]
        else:
            out[name] = blk.reshape(-1)[:n].reshape(shapes[name])
        row += nrows
    return out


def _lane_vec(a8):
    return jnp.pad(a8.reshape(1, 8), ((0, 0), (8, 112)))


WEIGHT_NAMES = ("norm_w", "w_in", "b_gate", "conv_a", "conv_c", "a_log", "dt_bias", "lower_bounds", "hgrn_norm_w",
                "gdn_norm_w", "w_out_a", "w_out_b", "w_out_c", "w_o", "final_norm_w")


def kernel(x, norm_w, w_in, b_gate, conv_a, conv_c, a_log, dt_bias, lower_bounds, hgrn_norm_w, gdn_norm_w, w_out_a, w_out_b, w_out_c, w_o, final_norm_w, loss_target, m_norm_w, m_w_in, m_b_gate, m_conv_a, m_conv_c, m_a_log, m_dt_bias, m_lower_bounds, m_hgrn_norm_w, m_gdn_norm_w, m_w_out_a, m_w_out_b, m_w_out_c, m_w_o, m_final_norm_w, v_norm_w, v_w_in, v_b_gate, v_conv_a, v_conv_c, v_a_log, v_dt_bias, v_lower_bounds, v_hgrn_norm_w, v_gdn_norm_w, v_w_out_a, v_w_out_b, v_w_out_c, v_w_o, v_final_norm_w):
    wts = dict(norm_w=norm_w, w_in=w_in, b_gate=b_gate, conv_a=conv_a, conv_c=conv_c, a_log=a_log, dt_bias=dt_bias,
               lower_bounds=lower_bounds, hgrn_norm_w=hgrn_norm_w, gdn_norm_w=gdn_norm_w, w_out_a=w_out_a,
               w_out_b=w_out_b, w_out_c=w_out_c, w_o=w_o, final_norm_w=final_norm_w)
    mom = dict(norm_w=m_norm_w, w_in=m_w_in, b_gate=m_b_gate, conv_a=m_conv_a, conv_c=m_conv_c, a_log=m_a_log,
               dt_bias=m_dt_bias, lower_bounds=m_lower_bounds, hgrn_norm_w=m_hgrn_norm_w, gdn_norm_w=m_gdn_norm_w,
               w_out_a=m_w_out_a, w_out_b=m_w_out_b, w_out_c=m_w_out_c, w_o=m_w_o, final_norm_w=m_final_norm_w)
    var = dict(norm_w=v_norm_w, w_in=v_w_in, b_gate=v_b_gate, conv_a=v_conv_a, conv_c=v_conv_c, a_log=v_a_log,
               dt_bias=v_dt_bias, lower_bounds=v_lower_bounds, hgrn_norm_w=v_hgrn_norm_w, gdn_norm_w=v_gdn_norm_w,
               w_out_a=v_w_out_a, w_out_b=v_w_out_b, w_out_c=v_w_out_c, w_o=v_w_o, final_norm_w=v_final_norm_w)
    chip = 2 * lax.axis_index("x") + lax.axis_index("y")
    chip1 = chip.reshape(1).astype(jnp.int32)

    win_l0, win_l1 = win_cast_pad(jnp.transpose(w_in, (2, 0, 1)))
    win4_l0, ca4, cc4 = weight_gather([win_l0.reshape(2, D // 2, SHARD_PAD), conv_a, conv_c])
    by_cols = lambda a: a.transpose(1, 2, 0, 3).reshape(a.shape[1], a.shape[2], N_CHIPS * a.shape[3])
    by_rows = lambda a: a.transpose(1, 0, 2, 3).reshape(a.shape[1], N_CHIPS * a.shape[2], a.shape[3])
    conv_a_full, conv_c_full = by_cols(ca4), by_cols(cc4)
    later = [win_l1, _b(w_out_a), _b(w_out_b), _b(w_out_c), _b(w_o)]
    sent_w, token = exchange_start(later, ["all"] * 5, "weights_start", after=win4_l0)

    def late_weights(after):
        lands = exchange_wait(sent_w, ["all"] * 5, after, "weights_wait")
        l1, woa4, wob4, woc4, wo4 = (lax.dynamic_update_index_in_dim(land, own, chip, 0)
                                     for land, own in zip(lands, later))
        outs = dict(w_out_a=by_cols(woa4), w_out_b=by_cols(wob4), w_out_c=by_rows(woc4), w_o=by_rows(wo4))
        layers[1].update(outs, w_in=win_to_padded(l1, "win_to_padded1"))
        return outs

    lbs = lbs_fwd(lower_bounds)
    layers = []
    for l in range(2):
        layers.append(dict(
            l=l, norm_w=norm_w[l:l + 1], b_gate=b_gate[l:l + 1], conv_a=conv_a_full[l], conv_c=conv_c_full[l],
            alog_l=_lane_vec(a_log[l]), dtb_l=_lane_vec(dt_bias[l]), lbs=lbs[l:l + 1],
            hgrn_norm_w=hgrn_norm_w[l:l + 1], gdn_norm_w=gdn_norm_w[l:l + 1]))
    layers[0].update(w_in=win_to_padded(win4_l0.reshape(N_CHIPS, D, SHARD_PAD), "win_to_padded0"), late=late_weights,
                     norm_w=norm_w[0:1] + token[0:1, 0:1])

    xs, saved = x[0], []
    for l in range(2):
        xs, s = layer_fwd(xs, layers[l])
        saved.append(s)
    loss_row, dx, dfw = loss_head(xs, final_norm_w.reshape(1, D), loss_target[0])
    lg, half = [None, None], [None, None]
    for l in (1, 0):
        dx, lg[l], half[l] = layer_bwd(dx, layers[l], saved[l], chip1)
    grad_x = dx[None]

    stack = lambda n: jnp.stack([lg[0][n], lg[1][n]], axis=0)
    gsmall = {n: stack(n) for n in ("norm_w", "b_gate", "hgrn_norm_w", "gdn_norm_w", "a_log", "dt_bias", "conv_a",
                                    "conv_c")}
    gsmall.update(lower_bounds=stack("lbs"), final_norm_w=dfw, loss=loss_row)
    mat_names = ("w_in",) + tuple(n for n, _ in OUT_MATS)
    mine = [half[l][n] for n in mat_names for l in range(2)]
    theirs, smalls = final_exchange(mine, _pack_small(gsmall, True))

    out_g, out_d, out_m, out_v = {}, {}, {}, {}
    for i, n in enumerate(mat_names):
        h, hs = mine[2 * i:2 * i + 2], theirs[2 * i:2 * i + 2]
        if n == "w_in":
            fwd, back = (lambda a: jnp.transpose(a, (2, 0, 1))), (lambda a: jnp.transpose(a, (1, 2, 0)))
            res = adam_pair_t(h, hs, fwd(wts[n]), fwd(mom[n]), fwd(var[n]), "adam_" + n)
            out_g[n], out_d[n], out_m[n], out_v[n] = (back(a) for a in res)
        else:
            out_g[n], out_d[n], out_m[n], out_v[n] = adam_pair(h, hs, wts[n], mom[n], var[n], "adam_" + n)
    small_names = [n for n, _ in _SMALL if n != "loss"]
    pack = lambda v: _pack_small({n: v[n] for n in small_names}, False)
    sg, sd, smn, svn = small_update(smalls, pack(wts), pack(mom), pack(var))
    shapes = {n: wts[n].shape for n in small_names}
    shapes.update(conv_a=(2, 3, 512), conv_c=(2, 4, 2048))
    for dst, src, conv in ((out_g, sg, True), (out_d, sd, False), (out_m, smn, False), (out_v, svn, False)):
        dst.update(_unpack_small(src, shapes, conv))
    loss = out_g.pop("loss")
    for n in ("conv_a", "conv_c"):
        width = wts[n].shape[2]
        g = lax.dynamic_slice_in_dim(out_g[n], chip * width, width, axis=2)
        two_d = lambda a: a.reshape(-1, width)
        d, mn, vn = adam(two_d(wts[n]), two_d(g), two_d(mom[n]), two_d(var[n]), "adam_" + n)
        out_g[n] = g
        out_d[n], out_m[n], out_v[n] = (a.reshape(wts[n].shape) for a in (d, mn, vn))
    return (loss, grad_x, *[out_g[n] for n in WEIGHT_NAMES], *[out_d[n] for n in WEIGHT_NAMES],
            *[out_m[n] for n in WEIGHT_NAMES], *[out_v[n] for n in WEIGHT_NAMES])
```

```python
import jax
import jax.numpy as jnp
from jax import lax
from jax.experimental import pallas as pl
from jax.experimental.pallas import tpu as pltpu

f32 = jnp.float32
bf16 = jnp.bfloat16

D = 1024
L = 64
SUB = 16
NORM_EPS = 1e-6
L2_EPS = 1e-6
MIN_F = 1e-30
HK = 128
QK_SCALE = HK ** -0.5
N_GDN = 8
GDN_BLOCK = 1024
N_HGRN = 4

REG_A = 2304
REG_C = 2304
REG_M = 4608
REG_BH = 384
OFF_A, OFF_C, OFF_M, OFF_B = 0, 2304, 4608, 9216
M_BZ, M_G, M_CZ = 0, 512, 3584
NP = 10752
NP_TILE = 1536
N_ORIG = 10256

ADAM_LR, ADAM_B1, ADAM_B2, ADAM_EPS, ADAM_WD, ADAM_STEP = 0.001, 0.9, 0.999, 1e-08, 0.01, 10

VMEM_LIMIT = 56 * 1024 * 1024


def _cp(sem):
    return pltpu.CompilerParams(dimension_semantics=sem, vmem_limit_bytes=VMEM_LIMIT)


def _sigmoid(x):
    return jax.nn.sigmoid(x)


def _silu(x):
    return x * _sigmoid(x)


def _silu2(x):
    s = _sigmoid(x)
    y = x * s
    return y, s + y * (1.0 - s)


def _softplus(x):
    u = jnp.exp(-jnp.abs(x))
    w = 1.0 + u
    l1p = jnp.where(w == 1.0, u, jnp.log(w) * (u / (w - 1.0)))
    return jnp.maximum(x, 0.0) + l1p


def _dot(a, b):
    return jnp.dot(a, b, preferred_element_type=f32)


def _dot_nt(a, b):
    return lax.dot_general(a, b, (((1,), (1,)), ((), ())), preferred_element_type=f32)


def _dot_tn(a, b):
    return lax.dot_general(a, b, (((0,), (0,)), ((), ())), preferred_element_type=f32)


def _bdot(a, b):
    return lax.dot_general(a, b, (((2,), (1,)), ((0,), (0,))), preferred_element_type=f32)


def _bdot_nt(a, b):
    return lax.dot_general(a, b, (((2,), (2,)), ((0,), (0,))), preferred_element_type=f32)


def _bdot_tn(a, b):
    return lax.dot_general(a, b, (((1,), (1,)), ((0,), (0,))), preferred_element_type=f32)


def _bdot_split(a, b):
    ah, bh = _b(a), _b(b)
    al, bl = _b(a - ah.astype(f32)), _b(b - bh.astype(f32))
    return _bdot(ah, bh) + (_bdot(ah, bl) + _bdot(al, bh))


def _b(x):
    return x.astype(bf16)


def _chunk_cumsum(x, rows_in_chunk):
    n = x.shape[0]
    for s in (1, 2, 4, 8, 16, 32):
        x = x + jnp.where(rows_in_chunk >= s, pltpu.roll(x, s, axis=0), 0.0)
    return x


def _chunk_rev_cumsum(x, rows_in_chunk):
    n = x.shape[0]
    for s in (1, 2, 4, 8, 16, 32):
        x = x + jnp.where(rows_in_chunk + s < L, pltpu.roll(x, n - s, axis=0), 0.0)
    return x


def _shift_down(x, s):
    return pltpu.roll(x, s, axis=0) if s else x


def _shift_up(x, s):
    return pltpu.roll(x, x.shape[0] - s, axis=0) if s else x


def inproj_fwd(x, nw, w):
    T = x.shape[0]
    tT, tn = min(2048, T), NP_TILE

    def body(x_ref, nw_ref, w_ref, p_ref, h_ref, hs):
        @pl.when(pl.program_id(1) == 0)
        def _():
            xv = x_ref[...]
            r = lax.rsqrt(jnp.mean(xv * xv, axis=-1, keepdims=True) + NORM_EPS)
            hv = _b(xv * r * nw_ref[...])
            hs[...] = hv
            h_ref[...] = hv
        p_ref[...] = _b(_dot(hs[...], w_ref[...]))

    return pl.pallas_call(
        body, name="inproj_fwd", grid=(T // tT, NP // tn),
        in_specs=[pl.BlockSpec((tT, D), lambda i, j: (i, 0)), pl.BlockSpec((1, D), lambda i, j: (0, 0)),
                  pl.BlockSpec((D, tn), lambda i, j: (0, j))],
        out_specs=[pl.BlockSpec((tT, tn), lambda i, j: (i, j)), pl.BlockSpec((tT, D), lambda i, j: (i, 0))],
        out_shape=[jax.ShapeDtypeStruct((T, NP), bf16), jax.ShapeDtypeStruct((T, D), bf16)],
        scratch_shapes=[pltpu.VMEM((tT, D), bf16)],
        compiler_params=_cp(("parallel", "arbitrary")),
    )(x, nw, w)


def matmul_tn(a, b, name, n=None, b_col0=0, with_bf16=False):
    T, K = a.shape
    N = b.shape[1] if n is None else n
    tT = min(2048, T)
    tn = NP_TILE if N % NP_TILE == 0 else min(N, 1024)
    nt = T // tT
    cb0 = b_col0 // tn

    def body(a_ref, b_ref, o_ref, *ob_ref):
        @pl.when(pl.program_id(1) == 0)
        def _():
            o_ref[...] = jnp.zeros_like(o_ref)
        o_ref[...] += _dot_tn(_b(a_ref[...]), _b(b_ref[...]))
        if with_bf16:
            @pl.when(pl.program_id(1) == nt - 1)
            def _():
                ob_ref[0][...] = _b(o_ref[...])

    ospec = pl.BlockSpec((K, tn), lambda j, t: (0, j))
    return pl.pallas_call(
        body, name=name, grid=(N // tn, nt),
        in_specs=[pl.BlockSpec((tT, K), lambda j, t: (t, 0)), pl.BlockSpec((tT, tn), lambda j, t: (t, cb0 + j))],
        out_specs=[ospec, ospec] if with_bf16 else ospec,
        out_shape=([jax.ShapeDtypeStruct((K, N), f32), jax.ShapeDtypeStruct((K, N), bf16)] if with_bf16
                   else jax.ShapeDtypeStruct((K, N), f32)),
        compiler_params=_cp(("parallel", "arbitrary")),
    )(a, b)


def inproj_bwd(dp, w, x, nw, dres):
    T = x.shape[0]
    tT, tk = min(1024, T), NP // 4
    nk = NP // tk

    def body(dp_ref, w_ref, x_ref, nw_ref, dres_ref, dx_ref, dnw_ref, acc):
        i, k = pl.program_id(0), pl.program_id(1)

        @pl.when((i == 0) & (k == 0))
        def _():
            dnw_ref[...] = jnp.zeros_like(dnw_ref)

        @pl.when(k == 0)
        def _():
            acc[...] = jnp.zeros_like(acc)
        acc[...] += _dot_nt(dp_ref[...], w_ref[...])

        @pl.when(k == nk - 1)
        def _():
            xv = x_ref[...]
            r = lax.rsqrt(jnp.mean(xv * xv, axis=-1, keepdims=True) + NORM_EPS)
            xh = xv * r
            dy = acc[...]
            dyw = dy * nw_ref[...]
            dx_ref[...] = r * (dyw - xh * jnp.mean(dyw * xh, axis=-1, keepdims=True)) + dres_ref[...]
            dnw_ref[...] += jnp.sum(dy * xh, axis=0, keepdims=True)

    return pl.pallas_call(
        body, name="inproj_bwd", grid=(T // tT, nk),
        in_specs=[pl.BlockSpec((tT, tk), lambda i, k: (i, k)), pl.BlockSpec((D, tk), lambda i, k: (0, k)),
                  pl.BlockSpec((tT, D), lambda i, k: (i, 0)), pl.BlockSpec((1, D), lambda i, k: (0, 0)),
                  pl.BlockSpec((tT, D), lambda i, k: (i, 0))],
        out_specs=[pl.BlockSpec((tT, D), lambda i, k: (i, 0)), pl.BlockSpec((1, D), lambda i, k: (0, 0))],
        out_shape=[jax.ShapeDtypeStruct((T, D), f32), jax.ShapeDtypeStruct((1, D), f32)],
        scratch_shapes=[pltpu.VMEM((tT, D), f32)],
        compiler_params=_cp(("arbitrary", "arbitrary")),
    )(dp, w, x, nw, dres)


def loss_head(x, fw, tgt):
    T = x.shape[0]
    tT = min(512, T)

    def body(x_ref, fw_ref, t_ref, loss_ref, dx_ref, dfw_ref):
        @pl.when(pl.program_id(0) == 0)
        def _():
            loss_ref[...] = jnp.zeros_like(loss_ref)
            dfw_ref[...] = jnp.zeros_like(dfw_ref)
        xv = x_ref[...]
        r = lax.rsqrt(jnp.mean(xv * xv, axis=-1, keepdims=True) + NORM_EPS)
        xh = xv * r
        err = xh * fw_ref[...] - t_ref[...]
        part = 0.5 * jnp.sum(jnp.mean(err * err, axis=-1, keepdims=True), axis=0, keepdims=True)
        loss_ref[...] += jnp.broadcast_to(part, loss_ref.shape)
        dy = err * (1.0 / D)
        dyw = dy * fw_ref[...]
        dx_ref[...] = r * (dyw - xh * jnp.mean(dyw * xh, axis=-1, keepdims=True))
        dfw_ref[...] += jnp.sum(dy * xh, axis=0, keepdims=True)

    return pl.pallas_call(
        body, name="loss_head", grid=(T // tT,),
        in_specs=[pl.BlockSpec((tT, D), lambda i: (i, 0)), pl.BlockSpec((1, D), lambda i: (0, 0)),
                  pl.BlockSpec((tT, D), lambda i: (i, 0))],
        out_specs=[pl.BlockSpec((1, 128), lambda i: (0, 0)), pl.BlockSpec((tT, D), lambda i: (i, 0)),
                   pl.BlockSpec((1, D), lambda i: (0, 0))],
        out_shape=[jax.ShapeDtypeStruct((1, 128), f32), jax.ShapeDtypeStruct((T, D), f32),
                   jax.ShapeDtypeStruct((1, D), f32)],
        compiler_params=_cp(("arbitrary",)),
    )(x, fw, tgt)


def _halo_specs(tT, T, width, colblk, rows=8):
    nb = T // rows
    per = tT // rows
    prev = pl.BlockSpec((rows, width), lambda i: (jnp.maximum(i * per - 1, 0), colblk))
    nxt = pl.BlockSpec((rows, width), lambda i: (jnp.minimum((i + 1) * per, nb - 1), colblk))
    return prev, nxt


def _p_rows(p_ref, width=2048):
    return p_ref[:, 0:width].astype(f32)


def _p_prev(pp_ref, width=2048):
    return pp_ref[:, 0:width].astype(f32)[8:16]


def _p_next(pn_ref, width=2048):
    return pn_ref[:, 0:width].astype(f32)[0:8]


def mixa_fwd(p, cw):
    T = p.shape[0]
    tT = min(512, T)
    prev_spec, _ = _halo_specs(tT, T, REG_A, OFF_A // REG_A, rows=16)

    def body(p_ref, pp_ref, cw_ref, y_ref):
        pv = _p_rows(p_ref)
        u = pv[:, 512:1024] * pv[:, 1024:1536]
        pp = _p_prev(pp_ref)
        up = jnp.where(pl.program_id(0) == 0, 0.0, pp[:, 512:1024] * pp[:, 1024:1536])
        ue = jnp.concatenate([up, u], axis=0)
        cv = cw_ref[0:1, :] * _shift_down(ue, 2) + cw_ref[1:2, :] * _shift_down(ue, 1) + cw_ref[2:3, :] * ue
        y_ref[...] = _b(pv[:, 0:512] * cv[8:] * _silu(pv[:, 1536:2048]))

    return pl.pallas_call(
        body, name="mixa_fwd", grid=(T // tT,),
        in_specs=[pl.BlockSpec((tT, REG_A), lambda i: (i, OFF_A // REG_A)), prev_spec,
                  pl.BlockSpec((3, 512), lambda i: (0, 0))],
        out_specs=pl.BlockSpec((tT, 512), lambda i: (i, 0)),
        out_shape=jax.ShapeDtypeStruct((T, 512), bf16),
        compiler_params=_cp(("parallel",)),
    )(p, p, cw)


def mixa_bwd(p, cw, dy, dp):
    T = p.shape[0]
    tT = min(512, T)
    nt = T // tT
    prev_spec, next_spec = _halo_specs(tT, T, REG_A, OFF_A // REG_A, rows=16)
    _, dnext_spec = _halo_specs(tT, T, 512, 0)

    def body(p_ref, pp_ref, pn_ref, cw_ref, dy_ref, dyn_ref, dp_in, dp_ref, dcw_ref):
        i = pl.program_id(0)

        @pl.when(i == 0)
        def _():
            dcw_ref[...] = jnp.zeros_like(dcw_ref)
        pv, pp, pn = _p_rows(p_ref), _p_prev(pp_ref), _p_next(pn_ref)
        pe = jnp.concatenate([pp, pv, pn], axis=0)
        rows = lax.broadcasted_iota(jnp.int32, (tT + 16, 1), 0)
        ab, ac, ax, az = pe[:, 0:512], pe[:, 512:1024], pe[:, 1024:1536], pe[:, 1536:2048]
        u = jnp.where((rows < 8) & (i == 0), 0.0, ac * ax)
        u1, u2 = _shift_down(u, 1), _shift_down(u, 2)
        w0, w1, w2 = cw_ref[0:1, :], cw_ref[1:2, :], cw_ref[2:3, :]
        cv = w0 * u2 + w1 * u1 + w2 * u
        dye = jnp.concatenate([jnp.zeros((8, 512), f32), dy_ref[...], dyn_ref[...]], axis=0)
        dye = jnp.where((rows >= tT + 8) & (i == nt - 1), 0.0, dye)
        sz, dsz = _silu2(az)
        dcv = dye * ab * sz
        du = w2 * dcv + w1 * _shift_up(dcv, 1) + w0 * _shift_up(dcv, 2)
        inner = (rows >= 8) & (rows < tT + 8)
        dcv_in = jnp.where(inner, dcv, 0.0)
        dcw_ref[0:1, :] += jnp.sum(dcv_in * u2, axis=0, keepdims=True)
        dcw_ref[1:2, :] += jnp.sum(dcv_in * u1, axis=0, keepdims=True)
        dcw_ref[2:3, :] += jnp.sum(dcv_in * u, axis=0, keepdims=True)
        sl = slice(8, tT + 8)
        dp_ref[:, 0:512] = _b((dye * cv * sz)[sl])
        dp_ref[:, 512:1024] = _b((du * ax)[sl])
        dp_ref[:, 1024:1536] = _b((du * ac)[sl])
        dp_ref[:, 1536:2048] = _b((dye * ab * cv * dsz)[sl])
        dp_ref[:, 2048:] = jnp.zeros((tT, REG_A - 2048), bf16)

    return pl.pallas_call(
        body, name="mixa_bwd", grid=(nt,),
        in_specs=[pl.BlockSpec((tT, REG_A), lambda i: (i, OFF_A // REG_A)), prev_spec, next_spec,
                  pl.BlockSpec((3, 512), lambda i: (0, 0)),
                  pl.BlockSpec((tT, 512), lambda i: (i, 0)), dnext_spec, pl.BlockSpec(memory_space=pl.ANY)],
        out_specs=[pl.BlockSpec((tT, REG_A), lambda i: (i, OFF_A // REG_A)), pl.BlockSpec((8, 512), lambda i: (0, 0))],
        out_shape=[jax.ShapeDtypeStruct((T, NP), bf16), jax.ShapeDtypeStruct((8, 512), f32)],
        input_output_aliases={6: 0},
        compiler_params=_cp(("arbitrary",)),
    )(p, p, p, cw, dy, dy, dp)


def _l2n_fwd(y):
    return y * lax.rsqrt(jnp.sum(y * y, axis=-1, keepdims=True) + L2_EPS)


def mixc_pre_fwd(p, cw, alog_l, dtb_l):
    T = p.shape[0]
    tT = min(512, T)
    prev_spec, _ = _halo_specs(tT, T, REG_C, OFF_C // REG_C, rows=16)

    def body(p_ref, pp_ref, cw_ref, al_ref, dt_ref, q_ref, k_ref, v_ref, sm_ref):
        pp = jnp.where(pl.program_id(0) == 0, 0.0, _p_prev(pp_ref))
        xe = jnp.concatenate([pp, _p_rows(p_ref)], axis=0)
        cv = (cw_ref[0:1, :] * _shift_down(xe, 3) + cw_ref[1:2, :] * _shift_down(xe, 2)
              + cw_ref[2:3, :] * _shift_down(xe, 1) + cw_ref[3:4, :] * xe)[8:]
        y = _silu(cv)
        for hh in range(4):
            sl = slice(hh * HK, (hh + 1) * HK)
            q_ref[:, sl] = _l2n_fwd(y[:, sl]) * QK_SCALE
            k_ref[:, sl] = _l2n_fwd(y[:, 512 + hh * HK:512 + (hh + 1) * HK])
        v_ref[...] = y[:, 1024:2048]
        ps = p_ref[:, 2048:2176].astype(f32)
        lane = lax.broadcasted_iota(jnp.int32, ps.shape, 1)
        la = -jnp.exp(al_ref[...]) * _softplus(ps + dt_ref[...])
        rin = lax.broadcasted_iota(jnp.int32, ps.shape, 0) % L
        g = _chunk_cumsum(la, rin)
        sm_ref[...] = jnp.where(lane < 8, _sigmoid(ps), jnp.where(lane < 16, g, 0.0))

    return pl.pallas_call(
        body, name="mixc_pre_fwd", grid=(T // tT,),
        in_specs=[pl.BlockSpec((tT, REG_C), lambda i: (i, OFF_C // REG_C)), prev_spec,
                  pl.BlockSpec((4, 2048), lambda i: (0, 0)),
                  pl.BlockSpec((1, 128), lambda i: (0, 0)), pl.BlockSpec((1, 128), lambda i: (0, 0))],
        out_specs=[pl.BlockSpec((tT, 512), lambda i: (i, 0)), pl.BlockSpec((tT, 512), lambda i: (i, 0)),
                   pl.BlockSpec((tT, 1024), lambda i: (i, 0)), pl.BlockSpec((tT, 128), lambda i: (i, 0))],
        out_shape=[jax.ShapeDtypeStruct((T, 512), f32), jax.ShapeDtypeStruct((T, 512), f32),
                   jax.ShapeDtypeStruct((T, 1024), f32), jax.ShapeDtypeStruct((T, 128), f32)],
        compiler_params=_cp(("parallel",)),
    )(p, p, cw, alog_l, dtb_l)


def mixc_pre_bwd(p, cw, alog_l, dtb_l, dq8, dk8, dv, dsm8, dp):
    T = p.shape[0]
    tT = min(256, T)
    nt = T // tT
    prev_spec, next_spec = _halo_specs(tT, T, REG_C, OFF_C // REG_C, rows=16)
    _, n1024 = _halo_specs(tT, T, 1024, 0)

    def body(p_ref, pp_ref, pn_ref, cw_ref, al_ref, dt_ref, dq_ref, dqn_ref, dk_ref, dkn_ref,
             dv_ref, dvn_ref, dsm_ref, dp_in, dp_ref, dcw_ref, dsml_ref):
        i = pl.program_id(0)

        @pl.when(i == 0)
        def _():
            dcw_ref[...] = jnp.zeros_like(dcw_ref)
            dsml_ref[...] = jnp.zeros_like(dsml_ref)
        rows = lax.broadcasted_iota(jnp.int32, (tT + 16, 1), 0)
        pp = jnp.where(i == 0, 0.0, _p_prev(pp_ref))
        xe = jnp.concatenate([pp, _p_rows(p_ref), _p_next(pn_ref)], axis=0)
        xs = [_shift_down(xe, 3), _shift_down(xe, 2), _shift_down(xe, 1), xe]
        cv = cw_ref[0:1, :] * xs[0] + cw_ref[1:2, :] * xs[1] + cw_ref[2:3, :] * xs[2] + cw_ref[3:4, :] * xs[3]
        y, dy_dcv = _silu2(cv)
        last = (rows >= tT + 8) & (i == nt - 1)
        z8q = jnp.zeros((8, 1024), f32)

        def ext(cur_ref, nxt_ref):
            return jnp.where(last, 0.0, jnp.concatenate([z8q, cur_ref[...], nxt_ref[...]], axis=0))
        dq8e, dk8e, dve = ext(dq_ref, dqn_ref), ext(dk_ref, dkn_ref), ext(dv_ref, dvn_ref)
        dys = []
        for (d8, base, scale) in ((dq8e, 0, QK_SCALE), (dk8e, 512, 1.0)):
            for hh in range(4):
                dn = (d8[:, (2 * hh) * HK:(2 * hh + 1) * HK] + d8[:, (2 * hh + 1) * HK:(2 * hh + 2) * HK]) * scale
                yh = y[:, base + hh * HK:base + (hh + 1) * HK]
                r = lax.rsqrt(jnp.sum(yh * yh, axis=-1, keepdims=True) + L2_EPS)
                nh = yh * r
                dys.append(r * (dn - nh * jnp.sum(dn * nh, axis=-1, keepdims=True)))
        dyy = jnp.concatenate(dys + [dve], axis=1)
        dcv = dyy * dy_dcv
        dx = (cw_ref[3:4, :] * dcv + cw_ref[2:3, :] * _shift_up(dcv, 1) + cw_ref[1:2, :] * _shift_up(dcv, 2)
              + cw_ref[0:1, :] * _shift_up(dcv, 3))
        dp_ref[:, 0:2048] = _b(dx[8:tT + 8])
        dp_ref[:, 2176:] = jnp.zeros((tT, REG_C - 2176), bf16)
        inner = (rows >= 8) & (rows < tT + 8)
        dcv_in = jnp.where(inner, dcv, 0.0)
        for j in range(4):
            dcw_ref[j:j + 1, :] += jnp.sum(dcv_in * xs[j], axis=0, keepdims=True)
        ps = p_ref[:, 2048:2176].astype(f32)
        lane = lax.broadcasted_iota(jnp.int32, ps.shape, 1)
        dsm = dsm_ref[:, 0:128]
        for hh in range(1, N_GDN):
            dsm = dsm + dsm_ref[:, hh * 128:(hh + 1) * 128]
        beta = _sigmoid(ps)
        xa = ps + dt_ref[...]
        nea = -jnp.exp(al_ref[...])
        dpa = dsm * nea * _sigmoid(xa)
        dp_ref[:, 2048:2176] = _b(jnp.where(lane < 8, dsm * beta * (1.0 - beta), jnp.where(lane < 16, dpa, 0.0)))
        amask = (lane >= 8) & (lane < 16)
        dsml_ref[0:1, :] += jnp.sum(jnp.where(amask, dsm * nea * _softplus(xa), 0.0), axis=0, keepdims=True)
        dsml_ref[1:2, :] += jnp.sum(jnp.where(amask, dpa, 0.0), axis=0, keepdims=True)

    cur1024 = pl.BlockSpec((tT, 1024), lambda i: (i, 0))
    return pl.pallas_call(
        body, name="mixc_pre_bwd", grid=(nt,),
        in_specs=[pl.BlockSpec((tT, REG_C), lambda i: (i, OFF_C // REG_C)), prev_spec, next_spec,
                  pl.BlockSpec((4, 2048), lambda i: (0, 0)),
                  pl.BlockSpec((1, 128), lambda i: (0, 0)), pl.BlockSpec((1, 128), lambda i: (0, 0)),
                  cur1024, n1024, cur1024, n1024, cur1024, n1024, cur1024, pl.BlockSpec(memory_space=pl.ANY)],
        out_specs=[pl.BlockSpec((tT, REG_C), lambda i: (i, OFF_C // REG_C)),
                   pl.BlockSpec((8, 2048), lambda i: (0, 0)), pl.BlockSpec((8, 128), lambda i: (0, 0))],
        out_shape=[jax.ShapeDtypeStruct((T, NP), bf16),
                   jax.ShapeDtypeStruct((8, 2048), f32), jax.ShapeDtypeStruct((8, 128), f32)],
        input_output_aliases={13: 0},
        compiler_params=_cp(("arbitrary",)),
    )(p, p, p, cw, alog_l, dtb_l, dq8, dq8, dk8, dk8, dv, dv, dsm8, dp)


def _tri_inverse(m):
    r = lax.broadcasted_iota(jnp.int32, (L, L), 0)
    c = lax.broadcasted_iota(jnp.int32, (L, L), 1)
    eye = (r == c).astype(f32)[None]
    same = lambda w: ((r // w) == (c // w))[None]
    md = jnp.where(same(8), m, 0.0)
    m2 = _bdot_split(md, md)
    m4 = _bdot_split(m2, m2)
    t = _bdot_split(_bdot_split(eye - md, eye + m2), eye + m4)
    for w in (16, 32, 64):
        mo = jnp.where(same(w) & jnp.logical_not(same(w // 2)), m, 0.0)
        t = t - _bdot_split(_bdot_split(t, mo), t)
    return t


def _col_to_row(col, eye):
    return jnp.sum(eye * col, axis=1, keepdims=True)


def _row_to_col(row, eye):
    return jnp.sum(eye * row, axis=2, keepdims=True)


def _gdn_chunk_terms(q, k, v, beta, g, t_inv=None):
    r = lax.broadcasted_iota(jnp.int32, (L, L), 0)
    c = lax.broadcasted_iota(jnp.int32, (L, L), 1)
    eye = (r == c).astype(f32)[None]
    causal, strict = (c <= r)[None], (c < r)[None]
    diff = g - _col_to_row(g, eye)
    dec = jnp.exp(jnp.where(causal, diff, 0.0))
    dc = jnp.where(causal, dec, 0.0)
    ds = jnp.where(strict, dec, 0.0)
    eg = jnp.exp(g)
    gl = g[:, L - 1:L, :]
    egl = jnp.exp(gl - g)
    kb = k * beta
    kk = _bdot_nt(_b(k), _b(kb))
    qk = _bdot_nt(_b(q), _b(kb))
    m = kk * ds
    aqk = qk * dc
    if t_inv is None:
        t_inv = _tri_inverse(m)
    tb = _b(t_inv)
    keg = k * eg
    u = _bdot(tb, _b(v))
    w = _bdot(tb, _b(keg))
    ks = kb * egl
    ksw = _bdot_tn(_b(ks), _b(w))
    return dict(eye=eye, causal=causal, strict=strict, dc=dc, ds=ds, eg=eg, gl=gl, egl=egl, kb=kb, kk=kk, qk=qk,
                m=m, aqk=aqk, t=t_inv, u=u, w=w, qi=q * eg, ks=ks, keg=keg, ksw=ksw)


def gdn_fwd(qn, kn, vv, sm):
    T = qn.shape[0]
    tB = min(2 * GDN_BLOCK, T)
    nc = tB // L
    N = T // L

    def body(q_ref, k_ref, v_ref, sm_ref, o_ref, st_ref, ti_ref, s_scr):
        j = pl.program_id(0)

        @pl.when(pl.program_id(1) == 0)
        def _():
            s_scr[...] = jnp.zeros_like(s_scr)
        smv = sm_ref[...]
        lane = lax.broadcasted_iota(jnp.int32, smv.shape, 1)
        q = q_ref[...].reshape(nc, L, HK)
        k = k_ref[...].reshape(nc, L, HK)
        tms, ksus, kswbs, egls = [], [], [], []
        for a in range(2):
            h = 2 * j + a
            beta = jnp.sum(jnp.where(lane == h, smv, 0.0), axis=1, keepdims=True).reshape(nc, L, 1)
            g = jnp.sum(jnp.where(lane == 8 + h, smv, 0.0), axis=1, keepdims=True).reshape(nc, L, 1)
            v = v_ref[:, a * HK:(a + 1) * HK].reshape(nc, L, HK)
            tm = _gdn_chunk_terms(q, k, v, beta, g)
            ti_ref[a] = tm["t"]
            tms.append(tm)
            ksus.append(_bdot_tn(_b(tm["ks"]), _b(tm["u"])))
            kswbs.append(_b(tm["ksw"]))
            egls.append(jnp.exp(tm["gl"]))
        s = [s_scr[0], s_scr[1]]
        states = [[None] * nc, [None] * nc]
        for ci in range(nc):
            for a in range(2):
                states[a][ci] = s[a]
                s[a] = egls[a][ci] * s[a] + (ksus[a][ci] - _dot(kswbs[a][ci], _b(s[a])))
        for a in range(2):
            s_scr[a] = s[a]
            sall = jnp.stack(states[a], axis=0)
            st_ref[a] = sall
            sb = _b(sall)
            tm = tms[a]
            e = tm["u"] - _bdot(_b(tm["w"]), sb)
            o = _bdot(_b(tm["qi"]), sb) + _bdot(_b(tm["aqk"]), _b(e))
            o_ref[:, a * HK:(a + 1) * HK] = o.reshape(tB, HK)

    return pl.pallas_call(
        body, name="gdn_fwd", grid=(N_GDN // 2, T // tB),
        in_specs=[pl.BlockSpec((tB, HK), lambda j, n: (n, j)), pl.BlockSpec((tB, HK), lambda j, n: (n, j)),
                  pl.BlockSpec((tB, 2 * HK), lambda j, n: (n, j)), pl.BlockSpec((tB, 128), lambda j, n: (n, 0))],
        out_specs=[pl.BlockSpec((tB, 2 * HK), lambda j, n: (n, j)),
                   pl.BlockSpec((2, nc, HK, HK), lambda j, n: (j, n, 0, 0)),
                   pl.BlockSpec((2, nc, L, L), lambda j, n: (j, n, 0, 0))],
        out_shape=[jax.ShapeDtypeStruct((T, N_GDN * HK), f32), jax.ShapeDtypeStruct((N_GDN, N, HK, HK), f32),
                   jax.ShapeDtypeStruct((N_GDN, N, L, L), f32)],
        scratch_shapes=[pltpu.VMEM((2, HK, HK), f32)],
        compiler_params=_cp(("parallel", "arbitrary")),
    )(qn, kn, vv, sm)


def gdn_bwd(qn, kn, vv, sm, st, ti, do):
    T = qn.shape[0]
    tB = min(GDN_BLOCK, T)
    nc = tB // L
    nb = T // tB

    def body(q_ref, k_ref, v_ref, sm_ref, st_ref, ti_ref, do_ref, dq_ref, dk_ref, dv_ref, dsm_ref, ds_scr):
        j = pl.program_id(0)

        @pl.when(pl.program_id(1) == 0)
        def _():
            ds_scr[...] = jnp.zeros_like(ds_scr)
        smv = sm_ref[...]
        lane = lax.broadcasted_iota(jnp.int32, smv.shape, 1)
        q = q_ref[...].reshape(nc, L, HK)
        k = k_ref[...].reshape(nc, L, HK)
        kbf, qbf = _b(k), _b(q)
        heads = []
        for a in range(2):
            h = 2 * j + a
            beta = jnp.sum(jnp.where(lane == h, smv, 0.0), axis=1, keepdims=True).reshape(nc, L, 1)
            g = jnp.sum(jnp.where(lane == 8 + h, smv, 0.0), axis=1, keepdims=True).reshape(nc, L, 1)
            v = v_ref[:, a * HK:(a + 1) * HK].reshape(nc, L, HK)
            do = do_ref[:, a * HK:(a + 1) * HK].reshape(nc, L, HK)
            s = st_ref[a]
            tm = _gdn_chunk_terms(q, k, v, beta, g, t_inv=ti_ref[a])
            sb, dob = _b(s), _b(do)
            e = tm["u"] - _bdot(_b(tm["w"]), sb)
            de0 = _bdot_tn(_b(tm["aqk"]), dob)
            ds0 = _bdot_tn(_b(tm["qi"]), dob) - _bdot_tn(_b(tm["w"]), _b(de0))
            heads.append(dict(h=h, beta=beta, tm=tm, s=s, sb=sb, dob=dob, e=e, de0=de0, ds0=ds0, kswb=_b(tm["ksw"]),
                              egl_last=jnp.exp(tm["gl"])))
        dsn = [ds_scr[0], ds_scr[1]]
        dsns = [[None] * nc, [None] * nc]
        for ci in reversed(range(nc)):
            for a, hd in enumerate(heads):
                dsns[a][ci] = dsn[a]
                dsn[a] = hd["ds0"][ci] + (hd["egl_last"][ci] * dsn[a] - _dot_tn(hd["kswb"][ci], _b(dsn[a])))
        rowi = lax.broadcasted_iota(jnp.int32, (nc, L, 1), 1)
        rin = lax.broadcasted_iota(jnp.int32, (tB, 128), 0) % L
        for a, hd in enumerate(heads):
            ds_scr[a] = dsn[a]
            tm, s, sb, dob, e, beta, h = hd["tm"], hd["s"], hd["sb"], hd["dob"], hd["e"], hd["beta"], hd["h"]
            eye, dc, ds_, eg, egl = tm["eye"], tm["dc"], tm["ds"], tm["eg"], tm["egl"]
            kb, u, w, qi, ks = tm["kb"], tm["u"], tm["w"], tm["qi"], tm["ks"]
            eb = _b(e)
            dsp = jnp.stack(dsns[a], axis=0)
            dspb = _b(dsp)
            de = hd["de0"] + _bdot(_b(ks), dspb)
            deb = _b(de)
            dks = _bdot_nt(eb, dspb)
            dqi = _bdot_nt(dob, sb)
            daqk = jnp.where(tm["causal"], _bdot_nt(dob, eb), 0.0)
            dw = -_bdot_nt(deb, sb)
            tb = _b(tm["t"])
            dvv = _bdot_tn(tb, deb)
            dkg = _bdot_tn(tb, _b(dw))
            dm = -jnp.where(tm["strict"], _bdot_nt(_b(dvv), _b(u)) + _bdot_nt(_b(dkg), _b(w)), 0.0)
            x = _b(dm * ds_)
            y = _b(daqk * dc)
            kbb = _b(kb)
            dk = _bdot(x, kbb) + dkg * eg
            dkb = _bdot_tn(x, kbf) + _bdot_tn(y, qbf) + dks * egl
            dq = _bdot(y, kbb) + dqi * eg
            dk = dk + dkb * beta
            dbeta = jnp.sum(dkb * k, axis=-1, keepdims=True)
            z = dm * tm["m"] + daqk * tm["aqk"]
            dg = (jnp.sum(dqi * qi - dks * ks + dkg * tm["keg"], axis=-1, keepdims=True)
                  + jnp.sum(z, axis=-1, keepdims=True) - _row_to_col(jnp.sum(z, axis=1, keepdims=True), eye))
            dgl = (hd["egl_last"] * jnp.sum(jnp.sum(s * dsp, axis=2, keepdims=True), axis=1, keepdims=True)
                   + jnp.sum(jnp.sum(dks * ks, axis=2, keepdims=True), axis=1, keepdims=True))
            dg = dg + jnp.where(rowi == L - 1, dgl, 0.0)
            dla = _chunk_rev_cumsum(jnp.broadcast_to(dg.reshape(tB, 1), (tB, 128)), rin)
            sl = slice(a * HK, (a + 1) * HK)
            dq_ref[:, sl] = dq.reshape(tB, HK)
            dk_ref[:, sl] = dk.reshape(tB, HK)
            dv_ref[:, sl] = dvv.reshape(tB, HK)
            dsm_ref[:, sl] = jnp.where(lane == h, dbeta.reshape(tB, 1), jnp.where(lane == 8 + h, dla, 0.0))

    rev = lambda n: nb - 1 - n
    pair = pl.BlockSpec((tB, 2 * HK), lambda j, n: (rev(n), j))
    return pl.pallas_call(
        body, name="gdn_bwd", grid=(N_GDN // 2, nb),
        in_specs=[pl.BlockSpec((tB, HK), lambda j, n: (rev(n), j)), pl.BlockSpec((tB, HK), lambda j, n: (rev(n), j)),
                  pair, pl.BlockSpec((tB, 128), lambda j, n: (rev(n), 0)),
                  pl.BlockSpec((2, nc, HK, HK), lambda j, n: (j, rev(n), 0, 0)),
                  pl.BlockSpec((2, nc, L, L), lambda j, n: (j, rev(n), 0, 0)), pair],
        out_specs=[pair] * 4,
        out_shape=[jax.ShapeDtypeStruct((T, N_GDN * HK), f32)] * 4,
        scratch_shapes=[pltpu.VMEM((2, HK, HK), f32)],
        compiler_params=_cp(("parallel", "arbitrary")),
    )(qn, kn, vv, sm, st, ti, do)


def _hgrn_prep(bq, bf_, bi, lb):
    tB = bq.shape[0]
    sq, dsq = _silu2(bq)
    sg = _sigmoid(bf_)
    f = lb + (1.0 - lb) * sg
    logf = jnp.log(jnp.maximum(f, MIN_F))
    rin = lax.broadcasted_iota(jnp.int32, (tB, HK), 0) % L
    g = _chunk_cumsum(logf, rin)
    return sq * QK_SCALE, sg, f, 1.0 - f, bi, g, rin, dsq * QK_SCALE


def _hgrn_intra(q, kk, v, g, do=None):
    n = q.shape[0]
    nsub = L // SUB
    bwd = do is not None
    o_rows = [None] * nsub
    if bwd:
        dq_rows = [None] * nsub
        dkk_acc = jnp.zeros_like(kk)
        dv_acc = jnp.zeros_like(v)
    for i in range(1, nsub):
        lo, hi, w = i * SUB, (i + 1) * SUB, i * SUB
        ref = g[:, lo - 1:lo, :]
        eq = jnp.exp(g[:, lo:hi, :] - ref)
        ek = jnp.exp(ref - g[:, :w, :])
        qs = _b(q[:, lo:hi, :] * eq)
        ks = _b(kk[:, :w, :] * ek)
        p = _bdot_nt(qs, ks)
        o_rows[i] = _bdot(_b(p), _b(v[:, :w, :]))
        if bwd:
            dob = _b(do[:, lo:hi, :])
            dp = _b(_bdot_nt(dob, _b(v[:, :w, :])))
            dq_rows[i] = _bdot(dp, ks) * eq
            pad = jnp.zeros((n, L - w, HK), f32)
            dkk_acc = dkk_acc + jnp.concatenate([_bdot_tn(dp, qs) * ek, pad], axis=1)
            dv_acc = dv_acc + jnp.concatenate([_bdot_tn(_b(p), dob), pad], axis=1)
    m = n * nsub
    q4, k4, v4, g4 = (a.reshape(m, SUB, HK) for a in (q, kk, v, g))
    r = lax.broadcasted_iota(jnp.int32, (m, SUB, HK), 1)
    od = jnp.zeros((m, SUB, HK), f32)
    if bwd:
        do4 = do.reshape(m, SUB, HK)
        dqd = jnp.zeros((m, SUB, HK), f32)
        dkd = jnp.zeros((m, SUB, HK), f32)
        dvd = jnp.zeros((m, SUB, HK), f32)
    for j in range(SUB):
        gj, kj, vj = g4[:, j:j + 1, :], k4[:, j:j + 1, :], v4[:, j:j + 1, :]
        ok = r >= j
        e = jnp.where(ok, jnp.exp(g4 - gj), 0.0)
        xq = q4 * e
        pj = jnp.sum(xq * kj, axis=-1, keepdims=True)
        od = od + pj * vj
        if bwd:
            dpj = jnp.sum(do4 * vj, axis=-1, keepdims=True)
            dqd = dqd + dpj * kj * e
            dkd = dkd + jnp.where(r == j, jnp.sum(dpj * xq, axis=1, keepdims=True), 0.0)
            dvd = dvd + jnp.where(r == j, jnp.sum(pj * do4, axis=1, keepdims=True), 0.0)
    od = od.reshape(n, L, HK)
    o = jnp.concatenate([od[:, :SUB, :]] + [od[:, i * SUB:(i + 1) * SUB, :] + o_rows[i] for i in range(1, nsub)], axis=1)
    if not bwd:
        return o
    dqd = dqd.reshape(n, L, HK)
    dq = jnp.concatenate([dqd[:, :SUB, :]] + [dqd[:, i * SUB:(i + 1) * SUB, :] + dq_rows[i] for i in range(1, nsub)], axis=1)
    return o, dq, dkk_acc + dkd.reshape(n, L, HK), dv_acc + dvd.reshape(n, L, HK)


def hgrn_fwd(p, lbs):
    T = p.shape[0]
    tB = min(1024, T)
    nc = tB // L
    N = T // L

    def body(b_ref, lb_ref, o_ref, st_ref, s_scr):
        @pl.when(pl.program_id(1) == 0)
        def _():
            s_scr[...] = jnp.zeros_like(s_scr)
        bv = b_ref[...].astype(f32)
        q, sg, f, kk, v, g, rin, _ = _hgrn_prep(bv[:, 0:HK], bv[:, HK:2 * HK], bv[:, 2 * HK:3 * HK], lb_ref[...])
        q3, k3, v3, g3 = (a.reshape(nc, L, HK) for a in (q, kk, v, g))
        o = _hgrn_intra(q3, k3, v3, g3)
        gl = g3[:, L - 1:L, :]
        qt = _b(q3 * jnp.exp(g3))
        kt = _b(k3 * jnp.exp(gl - g3))
        vb = _b(v3)
        st = s_scr[...]
        for c in range(nc):
            st_ref[c] = st
            o_ref[c * L:(c + 1) * L, :] = o[c] + _dot_nt(qt[c], _b(st))
            st = st * jnp.exp(gl[c]) + _dot_tn(vb[c], kt[c])
        s_scr[...] = st

    return pl.pallas_call(
        body, name="hgrn_fwd", grid=(N_HGRN, T // tB),
        in_specs=[pl.BlockSpec((tB, REG_BH), lambda h, n: (n, OFF_B // REG_BH + h)),
                  pl.BlockSpec((1, HK), lambda h, n: (0, h))],
        out_specs=[pl.BlockSpec((tB, HK), lambda h, n: (n, h)),
                   pl.BlockSpec((None, nc, HK, HK), lambda h, n: (h, n, 0, 0))],
        out_shape=[jax.ShapeDtypeStruct((T, N_HGRN * HK), f32), jax.ShapeDtypeStruct((N_HGRN, N, HK, HK), f32)],
        scratch_shapes=[pltpu.VMEM((HK, HK), f32)],
        compiler_params=_cp(("parallel", "arbitrary")),
    )(p, lbs)


def hgrn_bwd(p, lbs, st, do, dp):
    T = p.shape[0]
    tB = min(256, T)
    nc = tB // L
    nb = T // tB

    def body(b_ref, lb_ref, st_ref, do_ref, dp_in, dp_ref, dlb_ref, ds_scr):
        @pl.when(pl.program_id(1) == 0)
        def _():
            ds_scr[...] = jnp.zeros_like(ds_scr)
            dlb_ref[...] = jnp.zeros_like(dlb_ref)
        lb = lb_ref[...]
        bv = b_ref[...].astype(f32)
        q, sg, f, kk, v, g, rin, dq_dbq = _hgrn_prep(bv[:, 0:HK], bv[:, HK:2 * HK], bv[:, 2 * HK:3 * HK], lb)
        q3, k3, v3, g3 = (a.reshape(nc, L, HK) for a in (q, kk, v, g))
        do3 = do_ref[...].reshape(nc, L, HK)
        dob = _b(do3)
        gl = g3[:, L - 1:L, :]
        egl = jnp.exp(gl)
        eg, egr = jnp.exp(g3), jnp.exp(gl - g3)
        qt, kt = q3 * eg, k3 * egr
        s = st_ref[...]
        ds0 = _bdot_tn(dob, _b(qt))
        dsn = ds_scr[...]
        dsns = [None] * nc
        for c in reversed(range(nc)):
            dsns[c] = dsn
            dsn = ds0[c] + dsn * egl[c]
        ds_scr[...] = dsn
        dsp = jnp.stack(dsns, axis=0)
        dspb = _b(dsp)
        dqt = _bdot(dob, _b(s))
        dkt = _bdot(_b(v3), dspb)
        dv_state = _bdot_nt(_b(kt), dspb)
        dgl = egl * jnp.sum(s * dsp, axis=1, keepdims=True) + jnp.sum(dkt * kt, axis=1, keepdims=True)
        _, dq_i, dkk_i, dv_i = _hgrn_intra(q3, k3, v3, g3, do=do3)
        dq = dq_i + dqt * eg
        dkk = dkk_i + dkt * egr
        dv = dv_i + dv_state
        rowi = lax.broadcasted_iota(jnp.int32, (nc, L, HK), 1)
        dg = q3 * dq - k3 * dkk + jnp.where(rowi == L - 1, dgl, 0.0)
        dlogf = _chunk_rev_cumsum(dg.reshape(tB, HK), rin)
        dkk2 = dkk.reshape(tB, HK)
        df = jnp.where(f > MIN_F, dlogf / f, 0.0) - dkk2
        dlb_ref[...] += jnp.sum(df * (1.0 - sg), axis=0, keepdims=True)
        dp_ref[:, 0:HK] = _b(dq.reshape(tB, HK) * dq_dbq)
        dp_ref[:, HK:2 * HK] = _b(df * (1.0 - lb) * sg * (1.0 - sg))
        dp_ref[:, 2 * HK:3 * HK] = _b(dv.reshape(tB, HK))

    rev = lambda n: nb - 1 - n
    return pl.pallas_call(
        body, name="hgrn_bwd", grid=(N_HGRN, nb),
        in_specs=[pl.BlockSpec((tB, REG_BH), lambda h, n: (rev(n), OFF_B // REG_BH + h)),
                  pl.BlockSpec((1, HK), lambda h, n: (0, h)),
                  pl.BlockSpec((None, nc, HK, HK), lambda h, n: (h, rev(n), 0, 0)),
                  pl.BlockSpec((tB, HK), lambda h, n: (rev(n), h)), pl.BlockSpec(memory_space=pl.ANY)],
        out_specs=[pl.BlockSpec((tB, REG_BH), lambda h, n: (rev(n), OFF_B // REG_BH + h)),
                   pl.BlockSpec((1, HK), lambda h, n: (0, h))],
        out_shape=[jax.ShapeDtypeStruct((T, NP), bf16), jax.ShapeDtypeStruct((1, N_HGRN * HK), f32)],
        input_output_aliases={4: 0},
        scratch_shapes=[pltpu.VMEM((HK, HK), f32)],
        compiler_params=_cp(("parallel", "arbitrary")),
    )(p, lbs, st, do, dp)


def _headnorm_fwd(o, z, w, nheads):
    outs, parts = [], []
    for hh in range(nheads):
        sl = slice(hh * HK, (hh + 1) * HK)
        oh = o[:, sl]
        r = lax.rsqrt(jnp.mean(oh * oh, axis=-1, keepdims=True) + NORM_EPS)
        on = oh * r
        sz, dsz = _silu2(z[:, sl])
        outs.append(on * w * sz)
        parts.append((r, on, sz, dsz))
    return jnp.concatenate(outs, axis=1), parts


def _headnorm_bwd(parts, w, dy):
    dos, dzs = [], []
    dw = jnp.zeros((1, HK), f32)
    for hh, (r, on, sz, dsz) in enumerate(parts):
        dyh = dy[:, hh * HK:(hh + 1) * HK]
        dn = dyh * sz * w
        dos.append(r * (dn - on * jnp.mean(dn * on, axis=-1, keepdims=True)))
        dzs.append(dyh * on * w * dsz)
        dw = dw + jnp.sum(dyh * sz * on, axis=0, keepdims=True)
    return jnp.concatenate(dos, axis=1), jnp.concatenate(dzs, axis=1), dw


def _merge_specs(tT, l):
    row = lambda w, cb=0: pl.BlockSpec((tT, w), lambda i, cb=cb: (i, cb))
    full = lambda r, c: pl.BlockSpec((r, c), lambda i: (0, 0))
    layer = lambda r, c: pl.BlockSpec((None, r, c), lambda i: (l, 0, 0))
    return row, full, layer


def merge_fwd(x, p, ya, ob, oc, hw, gw, bg, woa, wob, woc, wo, l):
    T = x.shape[0]
    tT = min(512, T)
    row, full, layer = _merge_specs(tT, l)

    def body(x_ref, pm_ref, ya_ref, ob_ref, oc_ref, hw_ref, gw_ref, bg_ref,
             woa_ref, wob_ref, woc_ref, wo_ref, out_ref):
        yb = _b(_headnorm_fwd(ob_ref[...], pm_ref[:, M_BZ:M_G].astype(f32), hw_ref[...], N_HGRN)[0])
        yc = _b(_headnorm_fwd(oc_ref[...], pm_ref[:, M_CZ:REG_M].astype(f32), gw_ref[...], N_GDN)[0])
        gates = _sigmoid(pm_ref[:, M_G:M_CZ].astype(f32) + bg_ref[...])
        merged = (gates[:, 0:D] * _dot(ya_ref[...], woa_ref[...]) + gates[:, D:2 * D] * _dot(yb, wob_ref[...])
                  + gates[:, 2 * D:3 * D] * _dot(yc, woc_ref[...]))
        out_ref[...] = x_ref[...] + _dot(_b(merged), wo_ref[...])

    return pl.pallas_call(
        body, name="merge_fwd", grid=(T // tT,),
        in_specs=[row(D), row(REG_M, OFF_M // REG_M),
                  row(512), row(512), row(1024), full(1, HK), full(1, HK), full(1, 3 * D),
                  layer(512, D), layer(512, D), layer(D, D), layer(D, D)],
        out_specs=row(D),
        out_shape=jax.ShapeDtypeStruct((T, D), f32),
        compiler_params=_cp(("parallel",)),
    )(x, p, ya, ob, oc, hw, gw, bg, woa, wob, woc, wo)


def merge_bwd(dxo, p, ya, ob, oc, hw, gw, bg, woa, wob, woc, wo, l):
    T = dxo.shape[0]
    tT = min(256, T)
    row, full, layer = _merge_specs(tT, l)

    def body(dx_ref, pm_ref, ya_ref, ob_ref, oc_ref, hw_ref, gw_ref, bg_ref,
             woa_ref, wob_ref, woc_ref, wo_ref,
             dya_ref, dob_ref, doc_ref, dp_ref, mg_ref, dy3_ref, yb_ref, yc_ref,
             dbg_ref, dhw_ref, dgw_ref):
        @pl.when(pl.program_id(0) == 0)
        def _():
            dbg_ref[...] = jnp.zeros_like(dbg_ref)
            dhw_ref[...] = jnp.zeros_like(dhw_ref)
            dgw_ref[...] = jnp.zeros_like(dgw_ref)
        ob, oc, bz, cz = ob_ref[...], oc_ref[...], pm_ref[:, M_BZ:M_G].astype(f32), pm_ref[:, M_CZ:REG_M].astype(f32)
        hw_, gw_ = hw_ref[...], gw_ref[...]
        yb, parts_b = _headnorm_fwd(ob, bz, hw_, N_HGRN)
        yc, parts_c = _headnorm_fwd(oc, cz, gw_, N_GDN)
        yb, yc = _b(yb), _b(yc)
        yb_ref[...] = yb
        yc_ref[...] = yc
        gates = _sigmoid(pm_ref[:, M_G:M_CZ].astype(f32) + bg_ref[...])
        ys = (_dot(ya_ref[...], woa_ref[...]), _dot(yb, wob_ref[...]), _dot(yc, woc_ref[...]))
        dmerged = _dot_nt(_b(dx_ref[...]), wo_ref[...])
        merged = jnp.zeros_like(dmerged)
        dys = []
        for i in range(3):
            gi = gates[:, i * D:(i + 1) * D]
            merged = merged + gi * ys[i]
            dyi = _b(dmerged * gi)
            dys.append(dyi)
            dy3_ref[:, i * D:(i + 1) * D] = dyi
            dgp = dmerged * ys[i] * gi * (1.0 - gi)
            dp_ref[:, M_G + i * D:M_G + (i + 1) * D] = _b(dgp)
            dbg_ref[:, i * D:(i + 1) * D] += jnp.sum(dgp, axis=0, keepdims=True)
        mg_ref[...] = _b(merged)
        dya_ref[...] = _dot_nt(dys[0], woa_ref[...])
        dob, dbz, dhw = _headnorm_bwd(parts_b, hw_, _dot_nt(dys[1], wob_ref[...]))
        doc, dcz, dgw = _headnorm_bwd(parts_c, gw_, _dot_nt(dys[2], woc_ref[...]))
        dob_ref[...] = dob
        doc_ref[...] = doc
        dp_ref[:, M_BZ:M_G] = _b(dbz)
        dp_ref[:, M_CZ:REG_M] = _b(dcz)
        dhw_ref[...] += dhw
        dgw_ref[...] += dgw

    sd = jax.ShapeDtypeStruct
    return pl.pallas_call(
        body, name="merge_bwd", grid=(T // tT,),
        in_specs=[row(D), row(REG_M, OFF_M // REG_M),
                  row(512), row(512), row(1024), full(1, HK), full(1, HK), full(1, 3 * D),
                  layer(512, D), layer(512, D), layer(D, D), layer(D, D)],
        out_specs=[row(512), row(512), row(1024), row(REG_M, OFF_M // REG_M), row(D), row(3 * D), row(512),
                   row(1024), full(1, 3 * D), full(1, HK), full(1, HK)],
        out_shape=[sd((T, 512), f32), sd((T, 512), f32), sd((T, 1024), f32), sd((T, NP), bf16),
                   sd((T, D), bf16), sd((T, 3 * D), bf16), sd((T, 512), bf16),
                   sd((T, 1024), bf16), sd((1, 3 * D), f32), sd((1, HK), f32), sd((1, HK), f32)],
        compiler_params=_cp(("arbitrary",)),
    )(dxo, p, ya, ob, oc, hw, gw, bg, woa, wob, woc, wo)


def layer_fwd(x, w):
    l = w["l"]
    p, h = inproj_fwd(x, w["norm_w"], w["w_in"])
    ya = mixa_fwd(p, w["conv_a"])
    qn, kn, vv, sm = mixc_pre_fwd(p, w["conv_c"], w["alog_l"], w["dtb_l"])
    oc, st_c, ti = gdn_fwd(qn, kn, vv, sm)
    ob, st_b = hgrn_fwd(p, w["lbs"])
    if "late" in w:
        w.update(w.pop("late")((ya, oc, ob)))
    xo = merge_fwd(x, p, ya, ob, oc, w["hgrn_norm_w"], w["gdn_norm_w"], w["b_gate"],
                   w["w_out_a"], w["w_out_b"], w["w_out_c"], w["w_o"], l)
    saved = dict(x=x, p=p, h=h, ya=ya, qn=qn, kn=kn, vv=vv, sm=sm, oc=oc, st_c=st_c, ti=ti, ob=ob, st_b=st_b)
    return xo, saved


OUT_MATS = (("w_out_a", "cols"), ("w_out_b", "cols"), ("w_out_c", "rows"), ("w_o", "rows"))


def layer_bwd(dxo, w, s, chip):
    p, l = s["p"], w["l"]
    (dya, dob, doc, dp, merged, dy3, yb, yc, dbg, dhw, dgw) = merge_bwd(
        dxo, p, s["ya"], s["ob"], s["oc"], w["hgrn_norm_w"], w["gdn_norm_w"], w["b_gate"],
        w["w_out_a"], w["w_out_b"], w["w_out_c"], w["w_o"], l)
    full = {"w_o": matmul_tn(merged, dxo, "dw_o", with_bf16=True),
            "w_out_a": matmul_tn(s["ya"], dy3, "dw_out_a", n=D, b_col0=0, with_bf16=True),
            "w_out_b": matmul_tn(yb, dy3, "dw_out_b", n=D, b_col0=D, with_bf16=True),
            "w_out_c": matmul_tn(yc, dy3, "dw_out_c", n=D, b_col0=2 * D, with_bf16=True)}
    out_kinds = [k for _, k in OUT_MATS]
    sent_out, token = exchange_start([full[n][1] for n, _ in OUT_MATS], out_kinds, f"grads_out_start{l}")
    dp, dlbs = hgrn_bwd(p, w["lbs"] + token[0:1, 0:1], s["st_b"], dob, dp)
    dq8, dk8, dvv, dsm8 = gdn_bwd(s["qn"], s["kn"], s["vv"], s["sm"], s["st_c"], s["ti"], doc)
    dp, dcc, dsmall = mixc_pre_bwd(p, w["conv_c"], w["alog_l"], w["dtb_l"], dq8, dk8, dvv, dsm8, dp)
    dp, dca = mixa_bwd(p, w["conv_a"], dya, dp)
    gf_win, gb_win = win_from_padded(matmul_tn(s["h"], dp, "dw_in"), chip)
    sent_in, token = exchange_start([gb_win], ["slot"], f"grads_in_start{l}")
    dx, dnw = inproj_bwd(dp, w["w_in"], s["x"], w["norm_w"] + token[0:1, 0:1], dxo)
    recv_out = exchange_wait(sent_out, out_kinds, dx, f"grads_out_wait{l}")
    recv_in = exchange_wait(sent_in, ["slot"], dx, f"grads_in_wait{l}")
    half = {"w_in": partial_sum(gf_win, "own", recv_in[0], chip, "psum_w_in", transposed=True)}
    for (n, kind), r in zip(OUT_MATS, recv_out):
        half[n] = partial_sum(full[n][0], kind, r, chip, "psum_" + n)
    small = dict(norm_w=dnw, b_gate=dbg, hgrn_norm_w=dhw, gdn_norm_w=dgw, lbs=dlbs, conv_a=dca[0:3], conv_c=dcc[0:4],
                 a_log=dsmall[0:1, 8:16], dt_bias=dsmall[1:2, 8:16])
    return dx, small, half


def lbs_fwd(lb):
    def body(lb_ref, o_ref):
        l0, l1 = lb_ref[0:1, :], lb_ref[1:2, :]
        mx = jnp.maximum(l0, l1)
        e0, e1 = jnp.exp(l0 - mx), jnp.exp(l1 - mx)
        o_ref[0:1, :] = jnp.zeros_like(l0)
        o_ref[1:2, :] = e1 / (e0 + e1)
    return pl.pallas_call(body, name="lbs_fwd", out_shape=jax.ShapeDtypeStruct(lb.shape, f32))(lb)


def _adam_math(w, g, m, v):
    mn = ADAM_B1 * m + (1.0 - ADAM_B1) * g
    vn = ADAM_B2 * v + (1.0 - ADAM_B2) * (g * g)
    mh = mn / (1.0 - ADAM_B1 ** ADAM_STEP)
    vh = vn / (1.0 - ADAM_B2 ** ADAM_STEP)
    return -ADAM_LR * (mh / (jnp.sqrt(vh) + ADAM_EPS) + ADAM_WD * w), mn, vn


def adam(w, g, m, v, name):
    R, C = w.shape
    tr = 256 if R % 256 == 0 else R

    def body(w_ref, g_ref, m_ref, v_ref, d_ref, mo_ref, vo_ref):
        d, mn, vn = _adam_math(w_ref[...], g_ref[...], m_ref[...], v_ref[...])
        d_ref[...] = d
        mo_ref[...] = mn
        vo_ref[...] = vn

    spec = pl.BlockSpec((tr, C), lambda i: (i, 0))
    return pl.pallas_call(
        body, name=name, grid=(R // tr,), in_specs=[spec] * 4, out_specs=[spec] * 3,
        out_shape=[jax.ShapeDtypeStruct((R, C), f32)] * 3, compiler_params=_cp(("parallel",)),
    )(w, g, m, v)


def adam_pair(h, hs, w, m, v, name):
    _, R, C = w.shape
    cp = h[0].shape[1]
    tr = 128 if R % 128 == 0 else R
    nt = R // tr

    def body(h0_ref, h1_ref, s0_ref, s1_ref, w_ref, m_ref, v_ref, g_ref, d_ref, mo_ref, vo_ref):
        def update(h_ref, s_ref):
            g = (h_ref[...] + s_ref[...])[:, :C]
            d, mn, vn = _adam_math(w_ref[...], g, m_ref[...], v_ref[...])
            g_ref[...] = g
            d_ref[...] = d
            mo_ref[...] = mn
            vo_ref[...] = vn

        @pl.when(pl.program_id(0) == 0)
        def _():
            update(h0_ref, s0_ref)

        @pl.when(pl.program_id(0) == 1)
        def _():
            update(h1_ref, s1_ref)

    h0spec = pl.BlockSpec((tr, cp), lambda l, i: (jnp.where(l == 0, i, nt - 1), 0))
    h1spec = pl.BlockSpec((tr, cp), lambda l, i: (jnp.where(l == 1, i, 0), 0))
    spec = pl.BlockSpec((None, tr, C), lambda l, i: (l, i, 0))
    return pl.pallas_call(
        body, name=name, grid=(2, nt), in_specs=[h0spec, h1spec, h0spec, h1spec, spec, spec, spec],
        out_specs=[spec] * 4, out_shape=[jax.ShapeDtypeStruct(w.shape, f32)] * 4,
        compiler_params=_cp(("arbitrary", "arbitrary")),
    )(h[0], h[1], hs[0], hs[1], w, m, v)


_SMALL = (("norm_w", 2 * D), ("b_gate", 6 * D), ("lower_bounds", None), ("hgrn_norm_w", 2 * HK),
          ("gdn_norm_w", 2 * HK), ("a_log", 16), ("dt_bias", 16), ("final_norm_w", D), ("loss", None))
_CONV = (("conv_a", 2 * 3 * 512), ("conv_c", 2 * 4 * 2048))


def _small_rows(n):
    return 16 if n is None else -(-n // 1024) * 8


LB_ROW = sum(_small_rows(n) for _, n in _SMALL[:2])
ADAM_ROWS = sum(_small_rows(n) for _, n in _SMALL)
SMALL_ROWS = ADAM_ROWS + sum(_small_rows(n) for _, n in _CONV)


def small_update(parts, wp, mp, vp):
    def body(p_ref, w_ref, m_ref, v_ref, g_ref, d_ref, mo_ref, vo_ref):
        gs = p_ref[0]
        for i in range(1, 8):
            gs = gs + p_ref[i]
        w = w_ref[...]
        l0, l1 = w[LB_ROW:LB_ROW + 8], w[LB_ROW + 8:LB_ROW + 16]
        mx = jnp.maximum(l0, l1)
        e0, e1 = jnp.exp(l0 - mx), jnp.exp(l1 - mx)
        p0, p1 = e0 / (e0 + e1), e1 / (e0 + e1)
        dl1 = gs[LB_ROW + 8:LB_ROW + 16]
        s = p1 * dl1
        g = jnp.concatenate([gs[0:LB_ROW], -p0 * s, p1 * dl1 - p1 * s, gs[LB_ROW + 16:ADAM_ROWS]], axis=0)
        d, mn, vn = _adam_math(w, g, m_ref[...], v_ref[...])
        g_ref[0:ADAM_ROWS, :] = g
        g_ref[ADAM_ROWS:, :] = gs[ADAM_ROWS:]
        d_ref[...] = d
        mo_ref[...] = mn
        vo_ref[...] = vn
    sd = jax.ShapeDtypeStruct
    return pl.pallas_call(body, name="small_update",
                          out_shape=[sd((SMALL_ROWS, 128), f32)] + [sd((ADAM_ROWS, 128), f32)] * 3)(parts, wp, mp, vp)


def partial_sum(own, kind, recv, chip, name, transposed=False):
    _, r, c = recv.shape
    tr = 256 if r % 256 == 0 else r

    def body(chip_ref, o_ref, r_ref, out_ref):
        s = ((o_ref[...] + r_ref[0].astype(f32)) + r_ref[1].astype(f32)) + r_ref[2].astype(f32)
        out_ref[...] = s.T if transposed else s

    own_spec = {"own": pl.BlockSpec((tr, c), lambda i, chip: (i, 0)),
                "slot": pl.BlockSpec((None, tr, c), lambda i, chip: (chip[0], i, 0)),
                "cols": pl.BlockSpec((tr, c), lambda i, chip: (i, chip[0])),
                "rows": pl.BlockSpec((tr, c), lambda i, chip: (chip[0] * (r // tr) + i, 0))}[kind]
    out_spec = pl.BlockSpec((c, tr), lambda i, chip: (0, i)) if transposed else pl.BlockSpec((tr, c), lambda i, chip: (i, 0))
    return pl.pallas_call(
        body, name=name,
        grid_spec=pltpu.PrefetchScalarGridSpec(
            num_scalar_prefetch=1, grid=(r // tr,),
            in_specs=[own_spec, pl.BlockSpec((3, tr, c), lambda i, chip: (0, i, 0))], out_specs=out_spec),
        out_shape=jax.ShapeDtypeStruct((c, r) if transposed else (r, c), f32), compiler_params=_cp(("arbitrary",)),
    )(chip, own, recv)


def adam_pair_t(h, hs, wt, mt, vt, name):
    C, _, R = wt.shape
    tc = 128

    def body(h0_ref, h1_ref, s0_ref, s1_ref, w_ref, m_ref, v_ref, g_ref, d_ref, mo_ref, vo_ref):
        g = jnp.stack([h0_ref[...] + s0_ref[...], h1_ref[...] + s1_ref[...]], axis=1)
        d, mn, vn = _adam_math(w_ref[...], g, m_ref[...], v_ref[...])
        g_ref[...] = g
        d_ref[...] = d
        mo_ref[...] = mn
        vo_ref[...] = vn

    hspec = pl.BlockSpec((tc, R), lambda i: (i, 0))
    spec = pl.BlockSpec((tc, 2, R), lambda i: (i, 0, 0))
    return pl.pallas_call(
        body, name=name, grid=(pl.cdiv(C, tc),), in_specs=[hspec] * 4 + [spec] * 3, out_specs=[spec] * 4,
        out_shape=[jax.ShapeDtypeStruct(wt.shape, f32)] * 4, compiler_params=_cp(("parallel",)),
    )(h[0], h[1], hs[0], hs[1], wt, mt, vt)


MESH = pl.DeviceIdType.MESH
_HBM = pl.BlockSpec(memory_space=pltpu.HBM)


def _place():
    return lax.axis_index("x"), lax.axis_index("y"), lax.axis_index("c")


def weight_gather(arrs):
    n = len(arrs)

    def body(*refs):
        x_refs, out_refs = refs[:n], refs[n:2 * n]
        send_sems, recv_sems, local_sems = refs[2 * n:]
        x, y, c = _place()
        me, sibling = (x, y, c), (x, y, 1 - c)
        chips = [(1 - x, y), (x, 1 - y), (1 - x, 1 - y)]

        def copy(a, k, block, to, own_src=False):
            px, py, pc = block
            dst = out_refs[a].at[2 * px + py, pc]
            return pltpu.make_async_remote_copy(
                src_ref=x_refs[a].at[c] if own_src else dst, dst_ref=dst,
                send_sem=send_sems.at[7 * a + k], recv_sem=recv_sems.at[7 * a + k], device_id=to, device_id_type=MESH)

        mine = [pltpu.make_async_copy(x_refs[a].at[c], out_refs[a].at[2 * x + y, c], local_sems.at[a])
                for a in range(n)]
        for cp in mine:
            cp.start()
        first = []
        for a in range(n):
            first.append(copy(a, 0, me, sibling, own_src=True))
            first += [copy(a, 1 + j, me, (*chip, c), own_src=True) for j, chip in enumerate(chips)]
        for cp in first:
            cp.start()
        passed = []
        for j, chip in enumerate(chips):
            for a in range(n):
                copy(a, 1 + j, (*chip, c), me).wait_recv()
                fwd = copy(a, 4 + j, (*chip, c), sibling)
                fwd.start()
                passed.append(fwd)
        for a in range(n):
            copy(a, 0, sibling, me).wait_recv()
            for j, chip in enumerate(chips):
                copy(a, 4 + j, (*chip, 1 - c), me).wait_recv()
        for cp in first + passed:
            cp.wait_send()
        for cp in mine:
            cp.wait()

    return pl.pallas_call(
        body, name="weight_gather", in_specs=[_HBM] * n, out_specs=[_HBM] * n,
        out_shape=[jax.ShapeDtypeStruct((N_CHIPS,) + a.shape, a.dtype) for a in arrs],
        scratch_shapes=[pltpu.SemaphoreType.DMA((7 * n,)), pltpu.SemaphoreType.DMA((7 * n,)),
                        pltpu.SemaphoreType.DMA((n,))],
    )(*arrs)


SHARD_W = 256


_SEM = pl.BlockSpec(memory_space=pltpu.SEMAPHORE)
_EFFECT = pltpu.SideEffectType.DATAFLOW_SIDE_EFFECTING


def _landing_shape(a, kind):
    if kind == "all":
        return (N_CHIPS,) + a.shape
    if kind == "slot":
        return (3,) + a.shape[1:]
    return (3,) + ((a.shape[0], SHARD_W) if kind == "cols" else (SHARD_W, a.shape[1]))


def _shard_copies(src_refs, land_refs, kinds, send_sems, recv_sems):
    x, y, c = _place()
    copies = []
    for a, (src, land, kind) in enumerate(zip(src_refs, land_refs, kinds)):
        for j, (px, py) in enumerate(((1 - x, y), (x, 1 - y), (1 - x, 1 - y))):
            q = 2 * px + py
            lo = pl.multiple_of(q * SHARD_W, SHARD_W)
            part = {"slot": lambda: src.at[q], "cols": lambda: src.at[:, pl.ds(lo, SHARD_W)],
                    "rows": lambda: src.at[pl.ds(lo, SHARD_W), :], "all": lambda: src}[kind]()
            k = 3 * a + j
            copies.append(pltpu.make_async_remote_copy(
                src_ref=part, dst_ref=land.at[2 * x + y] if kind == "all" else land.at[j],
                send_sem=send_sems.at[k], recv_sem=recv_sems.at[k], device_id=(px, py, c), device_id_type=MESH))
    return copies


def exchange_start(srcs, kinds, name, after=None):
    n = len(srcs)
    lands = [lax.empty(_landing_shape(a, k), a.dtype) for a, k in zip(srcs, kinds)]
    extra = [] if after is None else [after]

    def body(*refs):
        src_refs, land_refs, token = refs[:n], refs[n:2 * n], refs[-1]
        send_sems, recv_sems = refs[2 * n + len(extra)], refs[2 * n + len(extra) + 1]
        for cp in _shard_copies(src_refs, land_refs, kinds, send_sems, recv_sems):
            cp.start()
        token[...] = jnp.zeros_like(token)

    both = list(srcs) + lands
    out = pl.pallas_call(
        body, name=name,
        out_shape=(pltpu.SemaphoreType.DMA((3 * n,)), pltpu.SemaphoreType.DMA((3 * n,)),
                   *[pltpu.HBM(a.shape, a.dtype) for a in both], jax.ShapeDtypeStruct((8, 128), f32)),
        in_specs=[_HBM] * (2 * n) + [pl.BlockSpec(memory_space=pl.ANY)] * len(extra),
        out_specs=(_SEM, _SEM, *[_HBM] * (2 * n), pl.BlockSpec(memory_space=pltpu.VMEM)),
        input_output_aliases={i: 2 + i for i in range(2 * n)},
        compiler_params=pltpu.CompilerParams(has_side_effects=_EFFECT),
    )(*[pltpu.with_memory_space_constraint(a, pltpu.HBM) for a in both], *extra)
    return (out[0], out[1], out[2:2 + 2 * n]), out[-1]


def exchange_wait(handle, kinds, after, name):
    send_sems, recv_sems, both = handle
    n = len(kinds)
    after = after if isinstance(after, tuple) else (after,)

    def body(*refs):
        src_refs, land_refs, s_sems, r_sems = refs[:n], refs[n:2 * n], refs[2 * n], refs[2 * n + 1]
        for cp in _shard_copies(src_refs, land_refs, kinds, s_sems, r_sems):
            cp.wait_send()
            cp.wait_recv()

    out = pl.pallas_call(
        body, name=name, out_shape=tuple(pltpu.HBM(a.shape, a.dtype) for a in both),
        in_specs=[_HBM] * (2 * n) + [_SEM, _SEM] + [pl.BlockSpec(memory_space=pl.ANY)] * len(after),
        out_specs=tuple([_HBM] * (2 * n)), input_output_aliases={i: i for i in range(2 * n)},
        compiler_params=pltpu.CompilerParams(has_side_effects=_EFFECT),
    )(*both, send_sems, recv_sems, *after)
    return out[n:]


def final_exchange(hs, small):
    n = len(hs)
    S = small.shape[0]

    def body(*refs):
        h_refs, sm_ref, out_refs, smalls_ref = refs[:n], refs[n], refs[n + 1:2 * n + 1], refs[2 * n + 1]
        send_sems, recv_sems, local_sem = refs[2 * n + 2:]
        x, y, c = _place()
        my_slot = smalls_ref.at[4 * x + 2 * y + c]
        mine = pltpu.make_async_copy(sm_ref, my_slot, local_sem)
        mine.start()
        copies = [pltpu.make_async_remote_copy(src_ref=h_refs[a], dst_ref=out_refs[a], send_sem=send_sems.at[a],
                                               recv_sem=recv_sems.at[a], device_id=(x, y, 1 - c), device_id_type=MESH)
                  for a in range(n)]
        for mask in range(1, 8):
            fx, fy, fc = (mask >> 2) & 1, (mask >> 1) & 1, mask & 1
            peer = ((1 - x) if fx else x, (1 - y) if fy else y, (1 - c) if fc else c)
            copies.append(pltpu.make_async_remote_copy(
                src_ref=sm_ref, dst_ref=my_slot, send_sem=send_sems.at[n - 1 + mask], recv_sem=recv_sems.at[n - 1 + mask],
                device_id=peer, device_id_type=MESH))
        for cp in copies:
            cp.start()
        for cp in copies:
            cp.wait_recv()
        for cp in copies:
            cp.wait_send()
        mine.wait()

    sd = jax.ShapeDtypeStruct
    out = pl.pallas_call(
        body, name="final_exchange", in_specs=[_HBM] * (n + 1), out_specs=[_HBM] * (n + 1),
        out_shape=[sd(h.shape, h.dtype) for h in hs] + [sd((8, S, 128), f32)],
        scratch_shapes=[pltpu.SemaphoreType.DMA((n + 7,)), pltpu.SemaphoreType.DMA((n + 7,)),
                        pltpu.SemaphoreType.DMA],
    )(*hs, small)
    return out[:n], out[n]


N_CHIPS = 4
SHARD_COLS = N_ORIG // N_CHIPS


SHARD_PAD = 2688
_COL_SEGMENTS = (
    ((0, 2048, OFF_A),)
    + tuple((2048 + 512 * j + HK * h, 2048 + 512 * j + HK * (h + 1), OFF_B + REG_BH * h + HK * j)
            for j in range(3) for h in range(N_HGRN))
    + ((3584, 4096, OFF_M + M_BZ), (4096, 6144, OFF_C), (6144, 6160, OFF_C + 2048), (6160, 7184, OFF_M + M_CZ),
       (7184, N_ORIG, OFF_M + M_G)))


def _shard_pieces():
    pieces = []
    for lo, hi, dst in _COL_SEGMENTS:
        for p in range(N_CHIPS):
            a, b = max(lo, p * SHARD_COLS), min(hi, (p + 1) * SHARD_COLS)
            if a < b:
                pieces.append((p, a - p * SHARD_COLS, dst + a - lo, b - a))
    return pieces


def win_cast_pad(wt):
    tc = 128

    def body(x_ref, o0_ref, o1_ref):
        col = pl.program_id(0) * tc + lax.broadcasted_iota(jnp.int32, (tc, 1), 0)
        for l, o_ref in enumerate((o0_ref, o1_ref)):
            o_ref[...] = _b(jnp.where(col < SHARD_COLS, x_ref[:, l, :], 0.0).T)

    spec = pl.BlockSpec((D, tc), lambda i: (0, i))
    return pl.pallas_call(
        body, name="win_cast_pad", grid=(SHARD_PAD // tc,),
        in_specs=[pl.BlockSpec((tc, 2, D), lambda i: (i, 0, 0))], out_specs=[spec, spec],
        out_shape=[jax.ShapeDtypeStruct((D, SHARD_PAD), bf16)] * 2, compiler_params=_cp(("parallel",)),
    )(wt)


def win_to_padded(w4, name):
    tr = 256
    pieces = _shard_pieces()

    def body(a_ref, o_ref):
        o_ref[...] = jnp.zeros((tr, NP), bf16)
        for p, j0, c0, n in pieces:
            o_ref[:, c0:c0 + n] = a_ref[p, :, j0:j0 + n]

    return pl.pallas_call(
        body, name=name, grid=(D // tr,),
        in_specs=[pl.BlockSpec((N_CHIPS, tr, SHARD_PAD), lambda i: (0, i, 0))],
        out_specs=pl.BlockSpec((tr, NP), lambda i: (i, 0)),
        out_shape=jax.ShapeDtypeStruct((D, NP), bf16), compiler_params=_cp(("parallel",)),
    )(w4)


def win_from_padded(dw, chip):
    tr = 128
    pieces = _shard_pieces()

    def body(chip_ref, d_ref, of_ref, ob_ref):
        of_ref[:, SHARD_COLS:] = jnp.zeros((tr, SHARD_PAD - SHARD_COLS), f32)
        for p in range(N_CHIPS):
            ob_ref[p, :, SHARD_COLS:] = jnp.zeros((tr, SHARD_PAD - SHARD_COLS), bf16)
        for p, j0, c0, n in pieces:
            v = d_ref[:, c0:c0 + n]
            ob_ref[p, :, j0:j0 + n] = _b(v)

            @pl.when(chip_ref[0] == p)
            def _(v=v, j0=j0, n=n):
                of_ref[:, j0:j0 + n] = v

    return pl.pallas_call(
        body, name="win_from_padded",
        grid_spec=pltpu.PrefetchScalarGridSpec(
            num_scalar_prefetch=1, grid=(D // tr,),
            in_specs=[pl.BlockSpec((tr, NP), lambda i, chip: (i, 0))],
            out_specs=[pl.BlockSpec((tr, SHARD_PAD), lambda i, chip: (i, 0)),
                       pl.BlockSpec((N_CHIPS, tr, SHARD_PAD), lambda i, chip: (0, i, 0))]),
        out_shape=[jax.ShapeDtypeStruct((D, SHARD_PAD), f32),
                   jax.ShapeDtypeStruct((N_CHIPS, D, SHARD_PAD), bf16)],
        compiler_params=_cp(("arbitrary",)),
    )(chip, dw)


def _rows128(a):
    flat = a.reshape(-1)
    total = -(-flat.shape[0] // 1024) * 1024
    return jnp.pad(flat, (0, total - flat.shape[0])).reshape(total // 128, 128)


def _lb_rows(lb):
    return jnp.pad(lb.reshape(2, 4, 128), ((0, 0), (0, 4), (0, 0))).reshape(16, 128)


def _pack_small(v, with_conv):
    rows = []
    for name, n in _SMALL + (_CONV if with_conv else ()):
        if name == "lower_bounds":
            rows.append(_lb_rows(v[name]))
        elif name == "loss":
            rows.append(jnp.broadcast_to(v[name], (16, 128)) if name in v else jnp.zeros((16, 128), f32))
        else:
            rows.append(_rows128(v[name]))
    return jnp.concatenate(rows, axis=0)


def _unpack_small(p, shapes, with_conv):
    out, row = {}, 0
    for name, n in _SMALL + (_CONV if with_conv else ()):
        nrows = _small_rows(n)
        blk = p[row:row + nrows]
        if name == "lower_bounds":
            out[name] = blk.reshape(2, 8, 128)[:, :4].reshape(2, 512)
        elif name == "loss":
            out[name] = blk[0, 0]
        else:
            out[name] = blk.reshape(-1)[:n].reshape(shapes[name])
        row += nrows
    return out


def _lane_vec(a8):
    return jnp.pad(a8.reshape(1, 8), ((0, 0), (8, 112)))


WEIGHT_NAMES = ("norm_w", "w_in", "b_gate", "conv_a", "conv_c", "a_log", "dt_bias", "lower_bounds", "hgrn_norm_w",
                "gdn_norm_w", "w_out_a", "w_out_b", "w_out_c", "w_o", "final_norm_w")


def kernel(x, norm_w, w_in, b_gate, conv_a, conv_c, a_log, dt_bias, lower_bounds, hgrn_norm_w, gdn_norm_w, w_out_a, w_out_b, w_out_c, w_o, final_norm_w, loss_target, m_norm_w, m_w_in, m_b_gate, m_conv_a, m_conv_c, m_a_log, m_dt_bias, m_lower_bounds, m_hgrn_norm_w, m_gdn_norm_w, m_w_out_a, m_w_out_b, m_w_out_c, m_w_o, m_final_norm_w, v_norm_w, v_w_in, v_b_gate, v_conv_a, v_conv_c, v_a_log, v_dt_bias, v_lower_bounds, v_hgrn_norm_w, v_gdn_norm_w, v_w_out_a, v_w_out_b, v_w_out_c, v_w_o, v_final_norm_w):
    wts = dict(norm_w=norm_w, w_in=w_in, b_gate=b_gate, conv_a=conv_a, conv_c=conv_c, a_log=a_log, dt_bias=dt_bias,
               lower_bounds=lower_bounds, hgrn_norm_w=hgrn_norm_w, gdn_norm_w=gdn_norm_w, w_out_a=w_out_a,
               w_out_b=w_out_b, w_out_c=w_out_c, w_o=w_o, final_norm_w=final_norm_w)
    mom = dict(norm_w=m_norm_w, w_in=m_w_in, b_gate=m_b_gate, conv_a=m_conv_a, conv_c=m_conv_c, a_log=m_a_log,
               dt_bias=m_dt_bias, lower_bounds=m_lower_bounds, hgrn_norm_w=m_hgrn_norm_w, gdn_norm_w=m_gdn_norm_w,
               w_out_a=m_w_out_a, w_out_b=m_w_out_b, w_out_c=m_w_out_c, w_o=m_w_o, final_norm_w=m_final_norm_w)
    var = dict(norm_w=v_norm_w, w_in=v_w_in, b_gate=v_b_gate, conv_a=v_conv_a, conv_c=v_conv_c, a_log=v_a_log,
               dt_bias=v_dt_bias, lower_bounds=v_lower_bounds, hgrn_norm_w=v_hgrn_norm_w, gdn_norm_w=v_gdn_norm_w,
               w_out_a=v_w_out_a, w_out_b=v_w_out_b, w_out_c=v_w_out_c, w_o=v_w_o, final_norm_w=v_final_norm_w)
    chip = 2 * lax.axis_index("x") + lax.axis_index("y")
    chip1 = chip.reshape(1).astype(jnp.int32)

    win_l0, win_l1 = win_cast_pad(jnp.transpose(w_in, (2, 0, 1)))
    win4_l0, ca4, cc4 = weight_gather([win_l0.reshape(2, D // 2, SHARD_PAD), conv_a, conv_c])
    by_cols = lambda a: a.transpose(1, 2, 0, 3).reshape(a.shape[1], a.shape[2], N_CHIPS * a.shape[3])
    by_rows = lambda a: a.transpose(1, 0, 2, 3).reshape(a.shape[1], N_CHIPS * a.shape[2], a.shape[3])
    conv_a_full, conv_c_full = by_cols(ca4), by_cols(cc4)
    later = [win_l1, _b(w_out_a), _b(w_out_b), _b(w_out_c), _b(w_o)]
    sent_w, token = exchange_start(later, ["all"] * 5, "weights_start", after=win4_l0)

    def late_weights(after):
        lands = exchange_wait(sent_w, ["all"] * 5, after, "weights_wait")
        l1, woa4, wob4, woc4, wo4 = (lax.dynamic_update_index_in_dim(land, own, chip, 0)
                                     for land, own in zip(lands, later))
        outs = dict(w_out_a=by_cols(woa4), w_out_b=by_cols(wob4), w_out_c=by_rows(woc4), w_o=by_rows(wo4))
        layers[1].update(outs, w_in=win_to_padded(l1, "win_to_padded1"))
        return outs

    lbs = lbs_fwd(lower_bounds)
    layers = []
    for l in range(2):
        layers.append(dict(
            l=l, norm_w=norm_w[l:l + 1], b_gate=b_gate[l:l + 1], conv_a=conv_a_full[l], conv_c=conv_c_full[l],
            alog_l=_lane_vec(a_log[l]), dtb_l=_lane_vec(dt_bias[l]), lbs=lbs[l:l + 1],
            hgrn_norm_w=hgrn_norm_w[l:l + 1], gdn_norm_w=gdn_norm_w[l:l + 1]))
    layers[0].update(w_in=win_to_padded(win4_l0.reshape(N_CHIPS, D, SHARD_PAD), "win_to_padded0"), late=late_weights,
                     norm_w=norm_w[0:1] + token[0:1, 0:1])

    xs, saved = x[0], []
    for l in range(2):
        xs, s = layer_fwd(xs, layers[l])
        saved.append(s)
    loss_row, dx, dfw = loss_head(xs, final_norm_w.reshape(1, D), loss_target[0])
    lg, half = [None, None], [None, None]
    for l in (1, 0):
        dx, lg[l], half[l] = layer_bwd(dx, layers[l], saved[l], chip1)
    grad_x = dx[None]

    stack = lambda n: jnp.stack([lg[0][n], lg[1][n]], axis=0)
    gsmall = {n: stack(n) for n in ("norm_w", "b_gate", "hgrn_norm_w", "gdn_norm_w", "a_log", "dt_bias", "conv_a",
                                    "conv_c")}
    gsmall.update(lower_bounds=stack("lbs"), final_norm_w=dfw, loss=loss_row)
    mat_names = ("w_in",) + tuple(n for n, _ in OUT_MATS)
    mine = [half[l][n] for n in mat_names for l in range(2)]
    theirs, smalls = final_exchange(mine, _pack_small(gsmall, True))

    out_g, out_d, out_m, out_v = {}, {}, {}, {}
    for i, n in enumerate(mat_names):
        h, hs = mine[2 * i:2 * i + 2], theirs[2 * i:2 * i + 2]
        if n == "w_in":
            fwd, back = (lambda a: jnp.transpose(a, (2, 0, 1))), (lambda a: jnp.transpose(a, (1, 2, 0)))
            res = adam_pair_t(h, hs, fwd(wts[n]), fwd(mom[n]), fwd(var[n]), "adam_" + n)
            out_g[n], out_d[n], out_m[n], out_v[n] = (back(a) for a in res)
        else:
            out_g[n], out_d[n], out_m[n], out_v[n] = adam_pair(h, hs, wts[n], mom[n], var[n], "adam_" + n)
    small_names = [n for n, _ in _SMALL if n != "loss"]
    pack = lambda v: _pack_small({n: v[n] for n in small_names}, False)
    sg, sd, smn, svn = small_update(smalls, pack(wts), pack(mom), pack(var))
    shapes = {n: wts[n].shape for n in small_names}
    shapes.update(conv_a=(2, 3, 512), conv_c=(2, 4, 2048))
    for dst, src, conv in ((out_g, sg, True), (out_d, sd, False), (out_m, smn, False), (out_v, svn, False)):
        dst.update(_unpack_small(src, shapes, conv))
    loss = out_g.pop("loss")
    for n in ("conv_a", "conv_c"):
        width = wts[n].shape[2]
        g = lax.dynamic_slice_in_dim(out_g[n], chip * width, width, axis=2)
        two_d = lambda a: a.reshape(-1, width)
        d, mn, vn = adam(two_d(wts[n]), two_d(g), two_d(mom[n]), two_d(var[n]), "adam_" + n)
        out_g[n] = g
        out_d[n], out_m[n], out_v[n] = (a.reshape(wts[n].shape) for a in (d, mn, vn))
    return (loss, grad_x, *[out_g[n] for n in WEIGHT_NAMES], *[out_d[n] for n in WEIGHT_NAMES],
            *[out_m[n] for n in WEIGHT_NAMES], *[out_v[n] for n in WEIGHT_NAMES])
```

```python
import jax
import jax.numpy as jnp
from jax import lax
from jax.experimental import pallas as pl
from jax.experimental.pallas import tpu as pltpu

f32 = jnp.float32
bf16 = jnp.bfloat16

D = 1024
L = 64
SUB = 16
NORM_EPS = 1e-6
L2_EPS = 1e-6
MIN_F = 1e-30
HK = 128
QK_SCALE = HK ** -0.5
N_GDN = 8
GDN_BLOCK = 1024
N_HGRN = 4

REG_A = 2304
REG_C = 2304
REG_M = 4608
REG_BH = 384
OFF_A, OFF_C, OFF_M, OFF_B = 0, 2304, 4608, 9216
M_BZ, M_G, M_CZ = 0, 512, 3584
NP = 10752
NP_TILE = 1536
N_ORIG = 10256

ADAM_LR, ADAM_B1, ADAM_B2, ADAM_EPS, ADAM_WD, ADAM_STEP = 0.001, 0.9, 0.999, 1e-08, 0.01, 10

VMEM_LIMIT = 56 * 1024 * 1024


def _cp(sem):
    return pltpu.CompilerParams(dimension_semantics=sem, vmem_limit_bytes=VMEM_LIMIT)


def _sigmoid(x):
    return jax.nn.sigmoid(x)


def _silu(x):
    return x * _sigmoid(x)


def _silu2(x):
    s = _sigmoid(x)
    y = x * s
    return y, s + y * (1.0 - s)


def _softplus(x):
    u = jnp.exp(-jnp.abs(x))
    w = 1.0 + u
    l1p = jnp.where(w == 1.0, u, jnp.log(w) * (u / (w - 1.0)))
    return jnp.maximum(x, 0.0) + l1p


def _dot(a, b):
    return jnp.dot(a, b, preferred_element_type=f32)


def _dot_nt(a, b):
    return lax.dot_general(a, b, (((1,), (1,)), ((), ())), preferred_element_type=f32)


def _dot_tn(a, b):
    return lax.dot_general(a, b, (((0,), (0,)), ((), ())), preferred_element_type=f32)


def _bdot(a, b):
    return lax.dot_general(a, b, (((2,), (1,)), ((0,), (0,))), preferred_element_type=f32)


def _bdot_nt(a, b):
    return lax.dot_general(a, b, (((2,), (2,)), ((0,), (0,))), preferred_element_type=f32)


def _bdot_tn(a, b):
    return lax.dot_general(a, b, (((1,), (1,)), ((0,), (0,))), preferred_element_type=f32)


def _bdot_split(a, b):
    ah, bh = _b(a), _b(b)
    al, bl = _b(a - ah.astype(f32)), _b(b - bh.astype(f32))
    return _bdot(ah, bh) + (_bdot(ah, bl) + _bdot(al, bh))


def _b(x):
    return x.astype(bf16)


def _chunk_cumsum(x, rows_in_chunk):
    n = x.shape[0]
    for s in (1, 2, 4, 8, 16, 32):
        x = x + jnp.where(rows_in_chunk >= s, pltpu.roll(x, s, axis=0), 0.0)
    return x


def _chunk_rev_cumsum(x, rows_in_chunk):
    n = x.shape[0]
    for s in (1, 2, 4, 8, 16, 32):
        x = x + jnp.where(rows_in_chunk + s < L, pltpu.roll(x, n - s, axis=0), 0.0)
    return x


def _shift_down(x, s):
    return pltpu.roll(x, s, axis=0) if s else x


def _shift_up(x, s):
    return pltpu.roll(x, x.shape[0] - s, axis=0) if s else x


def inproj_fwd(x, nw, w):
    T = x.shape[0]
    tT, tn = min(2048, T), NP_TILE

    def body(x_ref, nw_ref, w_ref, p_ref, h_ref, hs):
        @pl.when(pl.program_id(1) == 0)
        def _():
            xv = x_ref[...]
            r = lax.rsqrt(jnp.mean(xv * xv, axis=-1, keepdims=True) + NORM_EPS)
            hv = _b(xv * r * nw_ref[...])
            hs[...] = hv
            h_ref[...] = hv
        p_ref[...] = _b(_dot(hs[...], w_ref[...]))

    return pl.pallas_call(
        body, name="inproj_fwd", grid=(T // tT, NP // tn),
        in_specs=[pl.BlockSpec((tT, D), lambda i, j: (i, 0)), pl.BlockSpec((1, D), lambda i, j: (0, 0)),
                  pl.BlockSpec((D, tn), lambda i, j: (0, j))],
        out_specs=[pl.BlockSpec((tT, tn), lambda i, j: (i, j)), pl.BlockSpec((tT, D), lambda i, j: (i, 0))],
        out_shape=[jax.ShapeDtypeStruct((T, NP), bf16), jax.ShapeDtypeStruct((T, D), bf16)],
        scratch_shapes=[pltpu.VMEM((tT, D), bf16)],
        compiler_params=_cp(("parallel", "arbitrary")),
    )(x, nw, w)


def matmul_tn(a, b, name, n=None, b_col0=0, with_bf16=False):
    T, K = a.shape
    N = b.shape[1] if n is None else n
    tT = min(2048, T)
    tn = NP_TILE if N % NP_TILE == 0 else min(N, 1024)
    nt = T // tT
    cb0 = b_col0 // tn

    def body(a_ref, b_ref, o_ref, *ob_ref):
        @pl.when(pl.program_id(1) == 0)
        def _():
            o_ref[...] = jnp.zeros_like(o_ref)
        o_ref[...] += _dot_tn(_b(a_ref[...]), _b(b_ref[...]))
        if with_bf16:
            @pl.when(pl.program_id(1) == nt - 1)
            def _():
                ob_ref[0][...] = _b(o_ref[...])

    ospec = pl.BlockSpec((K, tn), lambda j, t: (0, j))
    return pl.pallas_call(
        body, name=name, grid=(N // tn, nt),
        in_specs=[pl.BlockSpec((tT, K), lambda j, t: (t, 0)), pl.BlockSpec((tT, tn), lambda j, t: (t, cb0 + j))],
        out_specs=[ospec, ospec] if with_bf16 else ospec,
        out_shape=([jax.ShapeDtypeStruct((K, N), f32), jax.ShapeDtypeStruct((K, N), bf16)] if with_bf16
                   else jax.ShapeDtypeStruct((K, N), f32)),
        compiler_params=_cp(("parallel", "arbitrary")),
    )(a, b)


def inproj_bwd(dp, w, x, nw, dres):
    T = x.shape[0]
    tT, tk = min(1024, T), NP // 4
    nk = NP // tk

    def body(dp_ref, w_ref, x_ref, nw_ref, dres_ref, dx_ref, dnw_ref, acc):
        i, k = pl.program_id(0), pl.program_id(1)

        @pl.when((i == 0) & (k == 0))
        def _():
            dnw_ref[...] = jnp.zeros_like(dnw_ref)

        @pl.when(k == 0)
        def _():
            acc[...] = jnp.zeros_like(acc)
        acc[...] += _dot_nt(dp_ref[...], w_ref[...])

        @pl.when(k == nk - 1)
        def _():
            xv = x_ref[...]
            r = lax.rsqrt(jnp.mean(xv * xv, axis=-1, keepdims=True) + NORM_EPS)
            xh = xv * r
            dy = acc[...]
            dyw = dy * nw_ref[...]
            dx_ref[...] = r * (dyw - xh * jnp.mean(dyw * xh, axis=-1, keepdims=True)) + dres_ref[...]
            dnw_ref[...] += jnp.sum(dy * xh, axis=0, keepdims=True)

    return pl.pallas_call(
        body, name="inproj_bwd", grid=(T // tT, nk),
        in_specs=[pl.BlockSpec((tT, tk), lambda i, k: (i, k)), pl.BlockSpec((D, tk), lambda i, k: (0, k)),
                  pl.BlockSpec((tT, D), lambda i, k: (i, 0)), pl.BlockSpec((1, D), lambda i, k: (0, 0)),
                  pl.BlockSpec((tT, D), lambda i, k: (i, 0))],
        out_specs=[pl.BlockSpec((tT, D), lambda i, k: (i, 0)), pl.BlockSpec((1, D), lambda i, k: (0, 0))],
        out_shape=[jax.ShapeDtypeStruct((T, D), f32), jax.ShapeDtypeStruct((1, D), f32)],
        scratch_shapes=[pltpu.VMEM((tT, D), f32)],
        compiler_params=_cp(("arbitrary", "arbitrary")),
    )(dp, w, x, nw, dres)


def loss_head(x, fw, tgt):
    T = x.shape[0]
    tT = min(1024, T)

    def body(x_ref, fw_ref, t_ref, loss_ref, dx_ref, dfw_ref):
        @pl.when(pl.program_id(0) == 0)
        def _():
            loss_ref[...] = jnp.zeros_like(loss_ref)
            dfw_ref[...] = jnp.zeros_like(dfw_ref)
        xv = x_ref[...]
        r = lax.rsqrt(jnp.mean(xv * xv, axis=-1, keepdims=True) + NORM_EPS)
        xh = xv * r
        err = xh * fw_ref[...] - t_ref[...]
        part = 0.5 * jnp.sum(jnp.mean(err * err, axis=-1, keepdims=True), axis=0, keepdims=True)
        loss_ref[...] += jnp.broadcast_to(part, loss_ref.shape)
        dy = err * (1.0 / D)
        dyw = dy * fw_ref[...]
        dx_ref[...] = r * (dyw - xh * jnp.mean(dyw * xh, axis=-1, keepdims=True))
        dfw_ref[...] += jnp.sum(dy * xh, axis=0, keepdims=True)

    return pl.pallas_call(
        body, name="loss_head", grid=(T // tT,),
        in_specs=[pl.BlockSpec((tT, D), lambda i: (i, 0)), pl.BlockSpec((1, D), lambda i: (0, 0)),
                  pl.BlockSpec((tT, D), lambda i: (i, 0))],
        out_specs=[pl.BlockSpec((1, 128), lambda i: (0, 0)), pl.BlockSpec((tT, D), lambda i: (i, 0)),
                   pl.BlockSpec((1, D), lambda i: (0, 0))],
        out_shape=[jax.ShapeDtypeStruct((1, 128), f32), jax.ShapeDtypeStruct((T, D), f32),
                   jax.ShapeDtypeStruct((1, D), f32)],
        compiler_params=_cp(("arbitrary",)),
    )(x, fw, tgt)


def _halo_specs(tT, T, width, colblk, rows=8):
    nb = T // rows
    per = tT // rows
    prev = pl.BlockSpec((rows, width), lambda i: (jnp.maximum(i * per - 1, 0), colblk))
    nxt = pl.BlockSpec((rows, width), lambda i: (jnp.minimum((i + 1) * per, nb - 1), colblk))
    return prev, nxt


def _p_rows(p_ref, width=2048):
    return p_ref[:, 0:width].astype(f32)


def _p_prev(pp_ref, width=2048):
    return pp_ref[:, 0:width].astype(f32)[8:16]


def _p_next(pn_ref, width=2048):
    return pn_ref[:, 0:width].astype(f32)[0:8]


def mixa_fwd(p, cw):
    T = p.shape[0]
    tT = min(1024, T)
    prev_spec, _ = _halo_specs(tT, T, REG_A, OFF_A // REG_A, rows=16)

    def body(p_ref, pp_ref, cw_ref, y_ref):
        pv = _p_rows(p_ref)
        u = pv[:, 512:1024] * pv[:, 1024:1536]
        pp = _p_prev(pp_ref)
        up = jnp.where(pl.program_id(0) == 0, 0.0, pp[:, 512:1024] * pp[:, 1024:1536])
        ue = jnp.concatenate([up, u], axis=0)
        cv = cw_ref[0:1, :] * _shift_down(ue, 2) + cw_ref[1:2, :] * _shift_down(ue, 1) + cw_ref[2:3, :] * ue
        y_ref[...] = _b(pv[:, 0:512] * cv[8:] * _silu(pv[:, 1536:2048]))

    return pl.pallas_call(
        body, name="mixa_fwd", grid=(T // tT,),
        in_specs=[pl.BlockSpec((tT, REG_A), lambda i: (i, OFF_A // REG_A)), prev_spec,
                  pl.BlockSpec((3, 512), lambda i: (0, 0))],
        out_specs=pl.BlockSpec((tT, 512), lambda i: (i, 0)),
        out_shape=jax.ShapeDtypeStruct((T, 512), bf16),
        compiler_params=_cp(("parallel",)),
    )(p, p, cw)


def mixa_bwd(p, cw, dy, dp):
    T = p.shape[0]
    tT = min(1024, T)
    nt = T // tT
    prev_spec, next_spec = _halo_specs(tT, T, REG_A, OFF_A // REG_A, rows=16)
    _, dnext_spec = _halo_specs(tT, T, 512, 0)

    def body(p_ref, pp_ref, pn_ref, cw_ref, dy_ref, dyn_ref, dp_in, dp_ref, dcw_ref):
        i = pl.program_id(0)

        @pl.when(i == 0)
        def _():
            dcw_ref[...] = jnp.zeros_like(dcw_ref)
        pv, pp, pn = _p_rows(p_ref), _p_prev(pp_ref), _p_next(pn_ref)
        pe = jnp.concatenate([pp, pv, pn], axis=0)
        rows = lax.broadcasted_iota(jnp.int32, (tT + 16, 1), 0)
        ab, ac, ax, az = pe[:, 0:512], pe[:, 512:1024], pe[:, 1024:1536], pe[:, 1536:2048]
        u = jnp.where((rows < 8) & (i == 0), 0.0, ac * ax)
        u1, u2 = _shift_down(u, 1), _shift_down(u, 2)
        w0, w1, w2 = cw_ref[0:1, :], cw_ref[1:2, :], cw_ref[2:3, :]
        cv = w0 * u2 + w1 * u1 + w2 * u
        dye = jnp.concatenate([jnp.zeros((8, 512), f32), dy_ref[...], dyn_ref[...]], axis=0)
        dye = jnp.where((rows >= tT + 8) & (i == nt - 1), 0.0, dye)
        sz, dsz = _silu2(az)
        dcv = dye * ab * sz
        du = w2 * dcv + w1 * _shift_up(dcv, 1) + w0 * _shift_up(dcv, 2)
        inner = (rows >= 8) & (rows < tT + 8)
        dcv_in = jnp.where(inner, dcv, 0.0)
        dcw_ref[0:1, :] += jnp.sum(dcv_in * u2, axis=0, keepdims=True)
        dcw_ref[1:2, :] += jnp.sum(dcv_in * u1, axis=0, keepdims=True)
        dcw_ref[2:3, :] += jnp.sum(dcv_in * u, axis=0, keepdims=True)
        sl = slice(8, tT + 8)
        dp_ref[:, 0:512] = _b((dye * cv * sz)[sl])
        dp_ref[:, 512:1024] = _b((du * ax)[sl])
        dp_ref[:, 1024:1536] = _b((du * ac)[sl])
        dp_ref[:, 1536:2048] = _b((dye * ab * cv * dsz)[sl])
        dp_ref[:, 2048:] = jnp.zeros((tT, REG_A - 2048), bf16)

    return pl.pallas_call(
        body, name="mixa_bwd", grid=(nt,),
        in_specs=[pl.BlockSpec((tT, REG_A), lambda i: (i, OFF_A // REG_A)), prev_spec, next_spec,
                  pl.BlockSpec((3, 512), lambda i: (0, 0)),
                  pl.BlockSpec((tT, 512), lambda i: (i, 0)), dnext_spec, pl.BlockSpec(memory_space=pl.ANY)],
        out_specs=[pl.BlockSpec((tT, REG_A), lambda i: (i, OFF_A // REG_A)), pl.BlockSpec((8, 512), lambda i: (0, 0))],
        out_shape=[jax.ShapeDtypeStruct((T, NP), bf16), jax.ShapeDtypeStruct((8, 512), f32)],
        input_output_aliases={6: 0},
        compiler_params=_cp(("arbitrary",)),
    )(p, p, p, cw, dy, dy, dp)


def _l2n_fwd(y):
    return y * lax.rsqrt(jnp.sum(y * y, axis=-1, keepdims=True) + L2_EPS)


def mixc_pre_fwd(p, cw, alog_l, dtb_l):
    T = p.shape[0]
    tT = min(1024, T)
    prev_spec, _ = _halo_specs(tT, T, REG_C, OFF_C // REG_C, rows=16)

    def body(p_ref, pp_ref, cw_ref, al_ref, dt_ref, q_ref, k_ref, v_ref, sm_ref):
        pp = jnp.where(pl.program_id(0) == 0, 0.0, _p_prev(pp_ref))
        xe = jnp.concatenate([pp, _p_rows(p_ref)], axis=0)
        cv = (cw_ref[0:1, :] * _shift_down(xe, 3) + cw_ref[1:2, :] * _shift_down(xe, 2)
              + cw_ref[2:3, :] * _shift_down(xe, 1) + cw_ref[3:4, :] * xe)[8:]
        y = _silu(cv)
        for hh in range(4):
            sl = slice(hh * HK, (hh + 1) * HK)
            q_ref[:, sl] = _l2n_fwd(y[:, sl]) * QK_SCALE
            k_ref[:, sl] = _l2n_fwd(y[:, 512 + hh * HK:512 + (hh + 1) * HK])
        v_ref[...] = y[:, 1024:2048]
        ps = p_ref[:, 2048:2176].astype(f32)
        lane = lax.broadcasted_iota(jnp.int32, ps.shape, 1)
        la = -jnp.exp(al_ref[...]) * _softplus(ps + dt_ref[...])
        rin = lax.broadcasted_iota(jnp.int32, ps.shape, 0) % L
        g = _chunk_cumsum(la, rin)
        sm_ref[...] = jnp.where(lane < 8, _sigmoid(ps), jnp.where(lane < 16, g, 0.0))

    return pl.pallas_call(
        body, name="mixc_pre_fwd", grid=(T // tT,),
        in_specs=[pl.BlockSpec((tT, REG_C), lambda i: (i, OFF_C // REG_C)), prev_spec,
                  pl.BlockSpec((4, 2048), lambda i: (0, 0)),
                  pl.BlockSpec((1, 128), lambda i: (0, 0)), pl.BlockSpec((1, 128), lambda i: (0, 0))],
        out_specs=[pl.BlockSpec((tT, 512), lambda i: (i, 0)), pl.BlockSpec((tT, 512), lambda i: (i, 0)),
                   pl.BlockSpec((tT, 1024), lambda i: (i, 0)), pl.BlockSpec((tT, 128), lambda i: (i, 0))],
        out_shape=[jax.ShapeDtypeStruct((T, 512), f32), jax.ShapeDtypeStruct((T, 512), f32),
                   jax.ShapeDtypeStruct((T, 1024), f32), jax.ShapeDtypeStruct((T, 128), f32)],
        compiler_params=_cp(("parallel",)),
    )(p, p, cw, alog_l, dtb_l)


def mixc_pre_bwd(p, cw, alog_l, dtb_l, dq8, dk8, dv, dsm8, dp):
    T = p.shape[0]
    tT = min(512, T)
    nt = T // tT
    prev_spec, next_spec = _halo_specs(tT, T, REG_C, OFF_C // REG_C, rows=16)
    _, n1024 = _halo_specs(tT, T, 1024, 0)

    def body(p_ref, pp_ref, pn_ref, cw_ref, al_ref, dt_ref, dq_ref, dqn_ref, dk_ref, dkn_ref,
             dv_ref, dvn_ref, dsm_ref, dp_in, dp_ref, dcw_ref, dsml_ref):
        i = pl.program_id(0)

        @pl.when(i == 0)
        def _():
            dcw_ref[...] = jnp.zeros_like(dcw_ref)
            dsml_ref[...] = jnp.zeros_like(dsml_ref)
        rows = lax.broadcasted_iota(jnp.int32, (tT + 16, 1), 0)
        pp = jnp.where(i == 0, 0.0, _p_prev(pp_ref))
        xe = jnp.concatenate([pp, _p_rows(p_ref), _p_next(pn_ref)], axis=0)
        xs = [_shift_down(xe, 3), _shift_down(xe, 2), _shift_down(xe, 1), xe]
        cv = cw_ref[0:1, :] * xs[0] + cw_ref[1:2, :] * xs[1] + cw_ref[2:3, :] * xs[2] + cw_ref[3:4, :] * xs[3]
        y, dy_dcv = _silu2(cv)
        last = (rows >= tT + 8) & (i == nt - 1)
        z8q = jnp.zeros((8, 1024), f32)

        def ext(cur_ref, nxt_ref):
            return jnp.where(last, 0.0, jnp.concatenate([z8q, cur_ref[...], nxt_ref[...]], axis=0))
        dq8e, dk8e, dve = ext(dq_ref, dqn_ref), ext(dk_ref, dkn_ref), ext(dv_ref, dvn_ref)
        dys = []
        for (d8, base, scale) in ((dq8e, 0, QK_SCALE), (dk8e, 512, 1.0)):
            for hh in range(4):
                dn = (d8[:, (2 * hh) * HK:(2 * hh + 1) * HK] + d8[:, (2 * hh + 1) * HK:(2 * hh + 2) * HK]) * scale
                yh = y[:, base + hh * HK:base + (hh + 1) * HK]
                r = lax.rsqrt(jnp.sum(yh * yh, axis=-1, keepdims=True) + L2_EPS)
                nh = yh * r
                dys.append(r * (dn - nh * jnp.sum(dn * nh, axis=-1, keepdims=True)))
        dyy = jnp.concatenate(dys + [dve], axis=1)
        dcv = dyy * dy_dcv
        dx = (cw_ref[3:4, :] * dcv + cw_ref[2:3, :] * _shift_up(dcv, 1) + cw_ref[1:2, :] * _shift_up(dcv, 2)
              + cw_ref[0:1, :] * _shift_up(dcv, 3))
        dp_ref[:, 0:2048] = _b(dx[8:tT + 8])
        dp_ref[:, 2176:] = jnp.zeros((tT, REG_C - 2176), bf16)
        inner = (rows >= 8) & (rows < tT + 8)
        dcv_in = jnp.where(inner, dcv, 0.0)
        for j in range(4):
            dcw_ref[j:j + 1, :] += jnp.sum(dcv_in * xs[j], axis=0, keepdims=True)
        ps = p_ref[:, 2048:2176].astype(f32)
        lane = lax.broadcasted_iota(jnp.int32, ps.shape, 1)
        dsm = dsm_ref[:, 0:128]
        for hh in range(1, N_GDN):
            dsm = dsm + dsm_ref[:, hh * 128:(hh + 1) * 128]
        beta = _sigmoid(ps)
        xa = ps + dt_ref[...]
        nea = -jnp.exp(al_ref[...])
        dpa = dsm * nea * _sigmoid(xa)
        dp_ref[:, 2048:2176] = _b(jnp.where(lane < 8, dsm * beta * (1.0 - beta), jnp.where(lane < 16, dpa, 0.0)))
        amask = (lane >= 8) & (lane < 16)
        dsml_ref[0:1, :] += jnp.sum(jnp.where(amask, dsm * nea * _softplus(xa), 0.0), axis=0, keepdims=True)
        dsml_ref[1:2, :] += jnp.sum(jnp.where(amask, dpa, 0.0), axis=0, keepdims=True)

    cur1024 = pl.BlockSpec((tT, 1024), lambda i: (i, 0))
    return pl.pallas_call(
        body, name="mixc_pre_bwd", grid=(nt,),
        in_specs=[pl.BlockSpec((tT, REG_C), lambda i: (i, OFF_C // REG_C)), prev_spec, next_spec,
                  pl.BlockSpec((4, 2048), lambda i: (0, 0)),
                  pl.BlockSpec((1, 128), lambda i: (0, 0)), pl.BlockSpec((1, 128), lambda i: (0, 0)),
                  cur1024, n1024, cur1024, n1024, cur1024, n1024, cur1024, pl.BlockSpec(memory_space=pl.ANY)],
        out_specs=[pl.BlockSpec((tT, REG_C), lambda i: (i, OFF_C // REG_C)),
                   pl.BlockSpec((8, 2048), lambda i: (0, 0)), pl.BlockSpec((8, 128), lambda i: (0, 0))],
        out_shape=[jax.ShapeDtypeStruct((T, NP), bf16),
                   jax.ShapeDtypeStruct((8, 2048), f32), jax.ShapeDtypeStruct((8, 128), f32)],
        input_output_aliases={13: 0},
        compiler_params=_cp(("arbitrary",)),
    )(p, p, p, cw, alog_l, dtb_l, dq8, dq8, dk8, dk8, dv, dv, dsm8, dp)


def _tri_inverse(m):
    r = lax.broadcasted_iota(jnp.int32, (L, L), 0)
    c = lax.broadcasted_iota(jnp.int32, (L, L), 1)
    eye = (r == c).astype(f32)[None]
    same = lambda w: ((r // w) == (c // w))[None]
    md = jnp.where(same(8), m, 0.0)
    m2 = _bdot_split(md, md)
    m4 = _bdot_split(m2, m2)
    t = _bdot_split(_bdot_split(eye - md, eye + m2), eye + m4)
    for w in (16, 32, 64):
        mo = jnp.where(same(w) & jnp.logical_not(same(w // 2)), m, 0.0)
        t = t - _bdot_split(_bdot_split(t, mo), t)
    return t


def _col_to_row(col, eye):
    return jnp.sum(eye * col, axis=1, keepdims=True)


def _row_to_col(row, eye):
    return jnp.sum(eye * row, axis=2, keepdims=True)


def _gdn_chunk_terms(q, k, v, beta, g, t_inv=None):
    r = lax.broadcasted_iota(jnp.int32, (L, L), 0)
    c = lax.broadcasted_iota(jnp.int32, (L, L), 1)
    eye = (r == c).astype(f32)[None]
    causal, strict = (c <= r)[None], (c < r)[None]
    diff = g - _col_to_row(g, eye)
    dec = jnp.exp(jnp.where(causal, diff, 0.0))
    dc = jnp.where(causal, dec, 0.0)
    ds = jnp.where(strict, dec, 0.0)
    eg = jnp.exp(g)
    gl = g[:, L - 1:L, :]
    egl = jnp.exp(gl - g)
    kb = k * beta
    kk = _bdot_nt(_b(k), _b(kb))
    qk = _bdot_nt(_b(q), _b(kb))
    m = kk * ds
    aqk = qk * dc
    if t_inv is None:
        t_inv = _tri_inverse(m)
    tb = _b(t_inv)
    keg = k * eg
    u = _bdot(tb, _b(v))
    w = _bdot(tb, _b(keg))
    ks = kb * egl
    ksw = _bdot_tn(_b(ks), _b(w))
    return dict(eye=eye, causal=causal, strict=strict, dc=dc, ds=ds, eg=eg, gl=gl, egl=egl, kb=kb, kk=kk, qk=qk,
                m=m, aqk=aqk, t=t_inv, u=u, w=w, qi=q * eg, ks=ks, keg=keg, ksw=ksw)


def gdn_fwd(qn, kn, vv, sm):
    T = qn.shape[0]
    tB = min(2 * GDN_BLOCK, T)
    nc = tB // L
    N = T // L

    def body(q_ref, k_ref, v_ref, sm_ref, o_ref, st_ref, ti_ref, s_scr):
        j = pl.program_id(0)

        @pl.when(pl.program_id(1) == 0)
        def _():
            s_scr[...] = jnp.zeros_like(s_scr)
        smv = sm_ref[...]
        lane = lax.broadcasted_iota(jnp.int32, smv.shape, 1)
        q = q_ref[...].reshape(nc, L, HK)
        k = k_ref[...].reshape(nc, L, HK)
        tms, ksus, kswbs, egls = [], [], [], []
        for a in range(2):
            h = 2 * j + a
            beta = jnp.sum(jnp.where(lane == h, smv, 0.0), axis=1, keepdims=True).reshape(nc, L, 1)
            g = jnp.sum(jnp.where(lane == 8 + h, smv, 0.0), axis=1, keepdims=True).reshape(nc, L, 1)
            v = v_ref[:, a * HK:(a + 1) * HK].reshape(nc, L, HK)
            tm = _gdn_chunk_terms(q, k, v, beta, g)
            ti_ref[a] = tm["t"]
            tms.append(tm)
            ksus.append(_bdot_tn(_b(tm["ks"]), _b(tm["u"])))
            kswbs.append(_b(tm["ksw"]))
            egls.append(jnp.exp(tm["gl"]))
        s = [s_scr[0], s_scr[1]]
        states = [[None] * nc, [None] * nc]
        for ci in range(nc):
            for a in range(2):
                states[a][ci] = s[a]
                s[a] = egls[a][ci] * s[a] + (ksus[a][ci] - _dot(kswbs[a][ci], _b(s[a])))
        for a in range(2):
            s_scr[a] = s[a]
            sall = jnp.stack(states[a], axis=0)
            st_ref[a] = sall
            sb = _b(sall)
            tm = tms[a]
            e = tm["u"] - _bdot(_b(tm["w"]), sb)
            o = _bdot(_b(tm["qi"]), sb) + _bdot(_b(tm["aqk"]), _b(e))
            o_ref[:, a * HK:(a + 1) * HK] = o.reshape(tB, HK)

    return pl.pallas_call(
        body, name="gdn_fwd", grid=(N_GDN // 2, T // tB),
        in_specs=[pl.BlockSpec((tB, HK), lambda j, n: (n, j)), pl.BlockSpec((tB, HK), lambda j, n: (n, j)),
                  pl.BlockSpec((tB, 2 * HK), lambda j, n: (n, j)), pl.BlockSpec((tB, 128), lambda j, n: (n, 0))],
        out_specs=[pl.BlockSpec((tB, 2 * HK), lambda j, n: (n, j)),
                   pl.BlockSpec((2, nc, HK, HK), lambda j, n: (j, n, 0, 0)),
                   pl.BlockSpec((2, nc, L, L), lambda j, n: (j, n, 0, 0))],
        out_shape=[jax.ShapeDtypeStruct((T, N_GDN * HK), f32), jax.ShapeDtypeStruct((N_GDN, N, HK, HK), f32),
                   jax.ShapeDtypeStruct((N_GDN, N, L, L), f32)],
        scratch_shapes=[pltpu.VMEM((2, HK, HK), f32)],
        compiler_params=_cp(("parallel", "arbitrary")),
    )(qn, kn, vv, sm)


def gdn_bwd(qn, kn, vv, sm, st, ti, do):
    T = qn.shape[0]
    tB = min(GDN_BLOCK, T)
    nc = tB // L
    nb = T // tB

    def body(q_ref, k_ref, v_ref, sm_ref, st_ref, ti_ref, do_ref, dq_ref, dk_ref, dv_ref, dsm_ref, ds_scr):
        j = pl.program_id(0)

        @pl.when(pl.program_id(1) == 0)
        def _():
            ds_scr[...] = jnp.zeros_like(ds_scr)
        smv = sm_ref[...]
        lane = lax.broadcasted_iota(jnp.int32, smv.shape, 1)
        q = q_ref[...].reshape(nc, L, HK)
        k = k_ref[...].reshape(nc, L, HK)
        kbf, qbf = _b(k), _b(q)
        heads = []
        for a in range(2):
            h = 2 * j + a
            beta = jnp.sum(jnp.where(lane == h, smv, 0.0), axis=1, keepdims=True).reshape(nc, L, 1)
            g = jnp.sum(jnp.where(lane == 8 + h, smv, 0.0), axis=1, keepdims=True).reshape(nc, L, 1)
            v = v_ref[:, a * HK:(a + 1) * HK].reshape(nc, L, HK)
            do = do_ref[:, a * HK:(a + 1) * HK].reshape(nc, L, HK)
            s = st_ref[a]
            tm = _gdn_chunk_terms(q, k, v, beta, g, t_inv=ti_ref[a])
            sb, dob = _b(s), _b(do)
            e = tm["u"] - _bdot(_b(tm["w"]), sb)
            de0 = _bdot_tn(_b(tm["aqk"]), dob)
            ds0 = _bdot_tn(_b(tm["qi"]), dob) - _bdot_tn(_b(tm["w"]), _b(de0))
            heads.append(dict(h=h, beta=beta, tm=tm, s=s, sb=sb, dob=dob, e=e, de0=de0, ds0=ds0, kswb=_b(tm["ksw"]),
                              egl_last=jnp.exp(tm["gl"])))
        dsn = [ds_scr[0], ds_scr[1]]
        dsns = [[None] * nc, [None] * nc]
        for ci in reversed(range(nc)):
            for a, hd in enumerate(heads):
                dsns[a][ci] = dsn[a]
                dsn[a] = hd["ds0"][ci] + (hd["egl_last"][ci] * dsn[a] - _dot_tn(hd["kswb"][ci], _b(dsn[a])))
        rowi = lax.broadcasted_iota(jnp.int32, (nc, L, 1), 1)
        rin = lax.broadcasted_iota(jnp.int32, (tB, 128), 0) % L
        for a, hd in enumerate(heads):
            ds_scr[a] = dsn[a]
            tm, s, sb, dob, e, beta, h = hd["tm"], hd["s"], hd["sb"], hd["dob"], hd["e"], hd["beta"], hd["h"]
            eye, dc, ds_, eg, egl = tm["eye"], tm["dc"], tm["ds"], tm["eg"], tm["egl"]
            kb, u, w, qi, ks = tm["kb"], tm["u"], tm["w"], tm["qi"], tm["ks"]
            eb = _b(e)
            dsp = jnp.stack(dsns[a], axis=0)
            dspb = _b(dsp)
            de = hd["de0"] + _bdot(_b(ks), dspb)
            deb = _b(de)
            dks = _bdot_nt(eb, dspb)
            dqi = _bdot_nt(dob, sb)
            daqk = jnp.where(tm["causal"], _bdot_nt(dob, eb), 0.0)
            dw = -_bdot_nt(deb, sb)
            tb = _b(tm["t"])
            dvv = _bdot_tn(tb, deb)
            dkg = _bdot_tn(tb, _b(dw))
            dm = -jnp.where(tm["strict"], _bdot_nt(_b(dvv), _b(u)) + _bdot_nt(_b(dkg), _b(w)), 0.0)
            x = _b(dm * ds_)
            y = _b(daqk * dc)
            kbb = _b(kb)
            dk = _bdot(x, kbb) + dkg * eg
            dkb = _bdot_tn(x, kbf) + _bdot_tn(y, qbf) + dks * egl
            dq = _bdot(y, kbb) + dqi * eg
            dk = dk + dkb * beta
            dbeta = jnp.sum(dkb * k, axis=-1, keepdims=True)
            z = dm * tm["m"] + daqk * tm["aqk"]
            dg = (jnp.sum(dqi * qi - dks * ks + dkg * tm["keg"], axis=-1, keepdims=True)
                  + jnp.sum(z, axis=-1, keepdims=True) - _row_to_col(jnp.sum(z, axis=1, keepdims=True), eye))
            dgl = (hd["egl_last"] * jnp.sum(jnp.sum(s * dsp, axis=2, keepdims=True), axis=1, keepdims=True)
                   + jnp.sum(jnp.sum(dks * ks, axis=2, keepdims=True), axis=1, keepdims=True))
            dg = dg + jnp.where(rowi == L - 1, dgl, 0.0)
            dla = _chunk_rev_cumsum(jnp.broadcast_to(dg.reshape(tB, 1), (tB, 128)), rin)
            sl = slice(a * HK, (a + 1) * HK)
            dq_ref[:, sl] = dq.reshape(tB, HK)
            dk_ref[:, sl] = dk.reshape(tB, HK)
            dv_ref[:, sl] = dvv.reshape(tB, HK)
            dsm_ref[:, sl] = jnp.where(lane == h, dbeta.reshape(tB, 1), jnp.where(lane == 8 + h, dla, 0.0))

    rev = lambda n: nb - 1 - n
    pair = pl.BlockSpec((tB, 2 * HK), lambda j, n: (rev(n), j))
    return pl.pallas_call(
        body, name="gdn_bwd", grid=(N_GDN // 2, nb),
        in_specs=[pl.BlockSpec((tB, HK), lambda j, n: (rev(n), j)), pl.BlockSpec((tB, HK), lambda j, n: (rev(n), j)),
                  pair, pl.BlockSpec((tB, 128), lambda j, n: (rev(n), 0)),
                  pl.BlockSpec((2, nc, HK, HK), lambda j, n: (j, rev(n), 0, 0)),
                  pl.BlockSpec((2, nc, L, L), lambda j, n: (j, rev(n), 0, 0)), pair],
        out_specs=[pair] * 4,
        out_shape=[jax.ShapeDtypeStruct((T, N_GDN * HK), f32)] * 4,
        scratch_shapes=[pltpu.VMEM((2, HK, HK), f32)],
        compiler_params=_cp(("parallel", "arbitrary")),
    )(qn, kn, vv, sm, st, ti, do)


def _hgrn_prep(bq, bf_, bi, lb):
    tB = bq.shape[0]
    sq, dsq = _silu2(bq)
    sg = _sigmoid(bf_)
    f = lb + (1.0 - lb) * sg
    logf = jnp.log(jnp.maximum(f, MIN_F))
    rin = lax.broadcasted_iota(jnp.int32, (tB, HK), 0) % L
    g = _chunk_cumsum(logf, rin)
    return sq * QK_SCALE, sg, f, 1.0 - f, bi, g, rin, dsq * QK_SCALE


def _hgrn_intra(q, kk, v, g, do=None):
    n = q.shape[0]
    nsub = L // SUB
    bwd = do is not None
    o_rows = [None] * nsub
    if bwd:
        dq_rows = [None] * nsub
        dkk_acc = jnp.zeros_like(kk)
        dv_acc = jnp.zeros_like(v)
    for i in range(1, nsub):
        lo, hi, w = i * SUB, (i + 1) * SUB, i * SUB
        ref = g[:, lo - 1:lo, :]
        eq = jnp.exp(g[:, lo:hi, :] - ref)
        ek = jnp.exp(ref - g[:, :w, :])
        qs = _b(q[:, lo:hi, :] * eq)
        ks = _b(kk[:, :w, :] * ek)
        p = _bdot_nt(qs, ks)
        o_rows[i] = _bdot(_b(p), _b(v[:, :w, :]))
        if bwd:
            dob = _b(do[:, lo:hi, :])
            dp = _b(_bdot_nt(dob, _b(v[:, :w, :])))
            dq_rows[i] = _bdot(dp, ks) * eq
            pad = jnp.zeros((n, L - w, HK), f32)
            dkk_acc = dkk_acc + jnp.concatenate([_bdot_tn(dp, qs) * ek, pad], axis=1)
            dv_acc = dv_acc + jnp.concatenate([_bdot_tn(_b(p), dob), pad], axis=1)
    m = n * nsub
    q4, k4, v4, g4 = (a.reshape(m, SUB, HK) for a in (q, kk, v, g))
    r = lax.broadcasted_iota(jnp.int32, (m, SUB, HK), 1)
    od = jnp.zeros((m, SUB, HK), f32)
    if bwd:
        do4 = do.reshape(m, SUB, HK)
        dqd = jnp.zeros((m, SUB, HK), f32)
        dkd = jnp.zeros((m, SUB, HK), f32)
        dvd = jnp.zeros((m, SUB, HK), f32)
    for j in range(SUB):
        gj, kj, vj = g4[:, j:j + 1, :], k4[:, j:j + 1, :], v4[:, j:j + 1, :]
        ok = r >= j
        e = jnp.where(ok, jnp.exp(g4 - gj), 0.0)
        xq = q4 * e
        pj = jnp.sum(xq * kj, axis=-1, keepdims=True)
        od = od + pj * vj
        if bwd:
            dpj = jnp.sum(do4 * vj, axis=-1, keepdims=True)
            dqd = dqd + dpj * kj * e
            dkd = dkd + jnp.where(r == j, jnp.sum(dpj * xq, axis=1, keepdims=True), 0.0)
            dvd = dvd + jnp.where(r == j, jnp.sum(pj * do4, axis=1, keepdims=True), 0.0)
    od = od.reshape(n, L, HK)
    o = jnp.concatenate([od[:, :SUB, :]] + [od[:, i * SUB:(i + 1) * SUB, :] + o_rows[i] for i in range(1, nsub)], axis=1)
    if not bwd:
        return o
    dqd = dqd.reshape(n, L, HK)
    dq = jnp.concatenate([dqd[:, :SUB, :]] + [dqd[:, i * SUB:(i + 1) * SUB, :] + dq_rows[i] for i in range(1, nsub)], axis=1)
    return o, dq, dkk_acc + dkd.reshape(n, L, HK), dv_acc + dvd.reshape(n, L, HK)


def hgrn_fwd(p, lbs):
    T = p.shape[0]
    tB = min(1024, T)
    nc = tB // L
    N = T // L

    def body(b_ref, lb_ref, o_ref, st_ref, s_scr):
        @pl.when(pl.program_id(1) == 0)
        def _():
            s_scr[...] = jnp.zeros_like(s_scr)
        bv = b_ref[...].astype(f32)
        q, sg, f, kk, v, g, rin, _ = _hgrn_prep(bv[:, 0:HK], bv[:, HK:2 * HK], bv[:, 2 * HK:3 * HK], lb_ref[...])
        q3, k3, v3, g3 = (a.reshape(nc, L, HK) for a in (q, kk, v, g))
        o = _hgrn_intra(q3, k3, v3, g3)
        gl = g3[:, L - 1:L, :]
        qt = _b(q3 * jnp.exp(g3))
        kt = _b(k3 * jnp.exp(gl - g3))
        vb = _b(v3)
        st = s_scr[...]
        for c in range(nc):
            st_ref[c] = st
            o_ref[c * L:(c + 1) * L, :] = o[c] + _dot_nt(qt[c], _b(st))
            st = st * jnp.exp(gl[c]) + _dot_tn(vb[c], kt[c])
        s_scr[...] = st

    return pl.pallas_call(
        body, name="hgrn_fwd", grid=(N_HGRN, T // tB),
        in_specs=[pl.BlockSpec((tB, REG_BH), lambda h, n: (n, OFF_B // REG_BH + h)),
                  pl.BlockSpec((1, HK), lambda h, n: (0, h))],
        out_specs=[pl.BlockSpec((tB, HK), lambda h, n: (n, h)),
                   pl.BlockSpec((None, nc, HK, HK), lambda h, n: (h, n, 0, 0))],
        out_shape=[jax.ShapeDtypeStruct((T, N_HGRN * HK), f32), jax.ShapeDtypeStruct((N_HGRN, N, HK, HK), f32)],
        scratch_shapes=[pltpu.VMEM((HK, HK), f32)],
        compiler_params=_cp(("parallel", "arbitrary")),
    )(p, lbs)


def hgrn_bwd(p, lbs, st, do, dp):
    T = p.shape[0]
    tB = min(256, T)
    nc = tB // L
    nb = T // tB

    def body(b_ref, lb_ref, st_ref, do_ref, dp_in, dp_ref, dlb_ref, ds_scr):
        @pl.when(pl.program_id(1) == 0)
        def _():
            ds_scr[...] = jnp.zeros_like(ds_scr)
            dlb_ref[...] = jnp.zeros_like(dlb_ref)
        lb = lb_ref[...]
        bv = b_ref[...].astype(f32)
        q, sg, f, kk, v, g, rin, dq_dbq = _hgrn_prep(bv[:, 0:HK], bv[:, HK:2 * HK], bv[:, 2 * HK:3 * HK], lb)
        q3, k3, v3, g3 = (a.reshape(nc, L, HK) for a in (q, kk, v, g))
        do3 = do_ref[...].reshape(nc, L, HK)
        dob = _b(do3)
        gl = g3[:, L - 1:L, :]
        egl = jnp.exp(gl)
        eg, egr = jnp.exp(g3), jnp.exp(gl - g3)
        qt, kt = q3 * eg, k3 * egr
        s = st_ref[...]
        ds0 = _bdot_tn(dob, _b(qt))
        dsn = ds_scr[...]
        dsns = [None] * nc
        for c in reversed(range(nc)):
            dsns[c] = dsn
            dsn = ds0[c] + dsn * egl[c]
        ds_scr[...] = dsn
        dsp = jnp.stack(dsns, axis=0)
        dspb = _b(dsp)
        dqt = _bdot(dob, _b(s))
        dkt = _bdot(_b(v3), dspb)
        dv_state = _bdot_nt(_b(kt), dspb)
        dgl = egl * jnp.sum(s * dsp, axis=1, keepdims=True) + jnp.sum(dkt * kt, axis=1, keepdims=True)
        _, dq_i, dkk_i, dv_i = _hgrn_intra(q3, k3, v3, g3, do=do3)
        dq = dq_i + dqt * eg
        dkk = dkk_i + dkt * egr
        dv = dv_i + dv_state
        rowi = lax.broadcasted_iota(jnp.int32, (nc, L, HK), 1)
        dg = q3 * dq - k3 * dkk + jnp.where(rowi == L - 1, dgl, 0.0)
        dlogf = _chunk_rev_cumsum(dg.reshape(tB, HK), rin)
        dkk2 = dkk.reshape(tB, HK)
        df = jnp.where(f > MIN_F, dlogf / f, 0.0) - dkk2
        dlb_ref[...] += jnp.sum(df * (1.0 - sg), axis=0, keepdims=True)
        dp_ref[:, 0:HK] = _b(dq.reshape(tB, HK) * dq_dbq)
        dp_ref[:, HK:2 * HK] = _b(df * (1.0 - lb) * sg * (1.0 - sg))
        dp_ref[:, 2 * HK:3 * HK] = _b(dv.reshape(tB, HK))

    rev = lambda n: nb - 1 - n
    return pl.pallas_call(
        body, name="hgrn_bwd", grid=(N_HGRN, nb),
        in_specs=[pl.BlockSpec((tB, REG_BH), lambda h, n: (rev(n), OFF_B // REG_BH + h)),
                  pl.BlockSpec((1, HK), lambda h, n: (0, h)),
                  pl.BlockSpec((None, nc, HK, HK), lambda h, n: (h, rev(n), 0, 0)),
                  pl.BlockSpec((tB, HK), lambda h, n: (rev(n), h)), pl.BlockSpec(memory_space=pl.ANY)],
        out_specs=[pl.BlockSpec((tB, REG_BH), lambda h, n: (rev(n), OFF_B // REG_BH + h)),
                   pl.BlockSpec((1, HK), lambda h, n: (0, h))],
        out_shape=[jax.ShapeDtypeStruct((T, NP), bf16), jax.ShapeDtypeStruct((1, N_HGRN * HK), f32)],
        input_output_aliases={4: 0},
        scratch_shapes=[pltpu.VMEM((HK, HK), f32)],
        compiler_params=_cp(("parallel", "arbitrary")),
    )(p, lbs, st, do, dp)


def _headnorm_fwd(o, z, w, nheads):
    outs, parts = [], []
    for hh in range(nheads):
        sl = slice(hh * HK, (hh + 1) * HK)
        oh = o[:, sl]
        r = lax.rsqrt(jnp.mean(oh * oh, axis=-1, keepdims=True) + NORM_EPS)
        on = oh * r
        sz, dsz = _silu2(z[:, sl])
        outs.append(on * w * sz)
        parts.append((r, on, sz, dsz))
    return jnp.concatenate(outs, axis=1), parts


def _headnorm_bwd(parts, w, dy):
    dos, dzs = [], []
    dw = jnp.zeros((1, HK), f32)
    for hh, (r, on, sz, dsz) in enumerate(parts):
        dyh = dy[:, hh * HK:(hh + 1) * HK]
        dn = dyh * sz * w
        dos.append(r * (dn - on * jnp.mean(dn * on, axis=-1, keepdims=True)))
        dzs.append(dyh * on * w * dsz)
        dw = dw + jnp.sum(dyh * sz * on, axis=0, keepdims=True)
    return jnp.concatenate(dos, axis=1), jnp.concatenate(dzs, axis=1), dw


def _merge_specs(tT, l):
    row = lambda w, cb=0: pl.BlockSpec((tT, w), lambda i, cb=cb: (i, cb))
    full = lambda r, c: pl.BlockSpec((r, c), lambda i: (0, 0))
    layer = lambda r, c: pl.BlockSpec((None, r, c), lambda i: (l, 0, 0))
    return row, full, layer


def merge_fwd(x, p, ya, ob, oc, hw, gw, bg, woa, wob, woc, wo, l):
    T = x.shape[0]
    tT = min(512, T)
    row, full, layer = _merge_specs(tT, l)

    def body(x_ref, pm_ref, ya_ref, ob_ref, oc_ref, hw_ref, gw_ref, bg_ref,
             woa_ref, wob_ref, woc_ref, wo_ref, out_ref):
        yb = _b(_headnorm_fwd(ob_ref[...], pm_ref[:, M_BZ:M_G].astype(f32), hw_ref[...], N_HGRN)[0])
        yc = _b(_headnorm_fwd(oc_ref[...], pm_ref[:, M_CZ:REG_M].astype(f32), gw_ref[...], N_GDN)[0])
        gates = _sigmoid(pm_ref[:, M_G:M_CZ].astype(f32) + bg_ref[...])
        merged = (gates[:, 0:D] * _dot(ya_ref[...], woa_ref[...]) + gates[:, D:2 * D] * _dot(yb, wob_ref[...])
                  + gates[:, 2 * D:3 * D] * _dot(yc, woc_ref[...]))
        out_ref[...] = x_ref[...] + _dot(_b(merged), wo_ref[...])

    return pl.pallas_call(
        body, name="merge_fwd", grid=(T // tT,),
        in_specs=[row(D), row(REG_M, OFF_M // REG_M),
                  row(512), row(512), row(1024), full(1, HK), full(1, HK), full(1, 3 * D),
                  layer(512, D), layer(512, D), layer(D, D), layer(D, D)],
        out_specs=row(D),
        out_shape=jax.ShapeDtypeStruct((T, D), f32),
        compiler_params=_cp(("parallel",)),
    )(x, p, ya, ob, oc, hw, gw, bg, woa, wob, woc, wo)


def merge_bwd(dxo, p, ya, ob, oc, hw, gw, bg, woa, wob, woc, wo, l):
    T = dxo.shape[0]
    tT = min(256, T)
    row, full, layer = _merge_specs(tT, l)

    def body(dx_ref, pm_ref, ya_ref, ob_ref, oc_ref, hw_ref, gw_ref, bg_ref,
             woa_ref, wob_ref, woc_ref, wo_ref,
             dya_ref, dob_ref, doc_ref, dp_ref, mg_ref, dy3_ref, yb_ref, yc_ref,
             dbg_ref, dhw_ref, dgw_ref):
        @pl.when(pl.program_id(0) == 0)
        def _():
            dbg_ref[...] = jnp.zeros_like(dbg_ref)
            dhw_ref[...] = jnp.zeros_like(dhw_ref)
            dgw_ref[...] = jnp.zeros_like(dgw_ref)
        ob, oc, bz, cz = ob_ref[...], oc_ref[...], pm_ref[:, M_BZ:M_G].astype(f32), pm_ref[:, M_CZ:REG_M].astype(f32)
        hw_, gw_ = hw_ref[...], gw_ref[...]
        yb, parts_b = _headnorm_fwd(ob, bz, hw_, N_HGRN)
        yc, parts_c = _headnorm_fwd(oc, cz, gw_, N_GDN)
        yb, yc = _b(yb), _b(yc)
        yb_ref[...] = yb
        yc_ref[...] = yc
        gates = _sigmoid(pm_ref[:, M_G:M_CZ].astype(f32) + bg_ref[...])
        ys = (_dot(ya_ref[...], woa_ref[...]), _dot(yb, wob_ref[...]), _dot(yc, woc_ref[...]))
        dmerged = _dot_nt(_b(dx_ref[...]), wo_ref[...])
        merged = jnp.zeros_like(dmerged)
        dys = []
        for i in range(3):
            gi = gates[:, i * D:(i + 1) * D]
            merged = merged + gi * ys[i]
            dyi = _b(dmerged * gi)
            dys.append(dyi)
            dy3_ref[:, i * D:(i + 1) * D] = dyi
            dgp = dmerged * ys[i] * gi * (1.0 - gi)
            dp_ref[:, M_G + i * D:M_G + (i + 1) * D] = _b(dgp)
            dbg_ref[:, i * D:(i + 1) * D] += jnp.sum(dgp, axis=0, keepdims=True)
        mg_ref[...] = _b(merged)
        dya_ref[...] = _dot_nt(dys[0], woa_ref[...])
        dob, dbz, dhw = _headnorm_bwd(parts_b, hw_, _dot_nt(dys[1], wob_ref[...]))
        doc, dcz, dgw = _headnorm_bwd(parts_c, gw_, _dot_nt(dys[2], woc_ref[...]))
        dob_ref[...] = dob
        doc_ref[...] = doc
        dp_ref[:, M_BZ:M_G] = _b(dbz)
        dp_ref[:, M_CZ:REG_M] = _b(dcz)
        dhw_ref[...] += dhw
        dgw_ref[...] += dgw

    sd = jax.ShapeDtypeStruct
    return pl.pallas_call(
        body, name="merge_bwd", grid=(T // tT,),
        in_specs=[row(D), row(REG_M, OFF_M // REG_M),
                  row(512), row(512), row(1024), full(1, HK), full(1, HK), full(1, 3 * D),
                  layer(512, D), layer(512, D), layer(D, D), layer(D, D)],
        out_specs=[row(512), row(512), row(1024), row(REG_M, OFF_M // REG_M), row(D), row(3 * D), row(512),
                   row(1024), full(1, 3 * D), full(1, HK), full(1, HK)],
        out_shape=[sd((T, 512), f32), sd((T, 512), f32), sd((T, 1024), f32), sd((T, NP), bf16),
                   sd((T, D), bf16), sd((T, 3 * D), bf16), sd((T, 512), bf16),
                   sd((T, 1024), bf16), sd((1, 3 * D), f32), sd((1, HK), f32), sd((1, HK), f32)],
        compiler_params=_cp(("arbitrary",)),
    )(dxo, p, ya, ob, oc, hw, gw, bg, woa, wob, woc, wo)


def layer_fwd(x, w):
    l = w["l"]
    p, h = inproj_fwd(x, w["norm_w"], w["w_in"])
    ya = mixa_fwd(p, w["conv_a"])
    qn, kn, vv, sm = mixc_pre_fwd(p, w["conv_c"], w["alog_l"], w["dtb_l"])
    oc, st_c, ti = gdn_fwd(qn, kn, vv, sm)
    ob, st_b = hgrn_fwd(p, w["lbs"])
    if "late" in w:
        w.update(w.pop("late")((ya, oc, ob)))
    xo = merge_fwd(x, p, ya, ob, oc, w["hgrn_norm_w"], w["gdn_norm_w"], w["b_gate"],
                   w["w_out_a"], w["w_out_b"], w["w_out_c"], w["w_o"], l)
    saved = dict(x=x, p=p, h=h, ya=ya, qn=qn, kn=kn, vv=vv, sm=sm, oc=oc, st_c=st_c, ti=ti, ob=ob, st_b=st_b)
    return xo, saved


OUT_MATS = (("w_out_a", "cols"), ("w_out_b", "cols"), ("w_out_c", "rows"), ("w_o", "rows"))


def layer_bwd(dxo, w, s, chip):
    p, l = s["p"], w["l"]
    (dya, dob, doc, dp, merged, dy3, yb, yc, dbg, dhw, dgw) = merge_bwd(
        dxo, p, s["ya"], s["ob"], s["oc"], w["hgrn_norm_w"], w["gdn_norm_w"], w["b_gate"],
        w["w_out_a"], w["w_out_b"], w["w_out_c"], w["w_o"], l)
    full = {"w_o": matmul_tn(merged, dxo, "dw_o", with_bf16=True),
            "w_out_a": matmul_tn(s["ya"], dy3, "dw_out_a", n=D, b_col0=0, with_bf16=True),
            "w_out_b": matmul_tn(yb, dy3, "dw_out_b", n=D, b_col0=D, with_bf16=True),
            "w_out_c": matmul_tn(yc, dy3, "dw_out_c", n=D, b_col0=2 * D, with_bf16=True)}
    out_kinds = [k for _, k in OUT_MATS]
    sent_out, token = exchange_start([full[n][1] for n, _ in OUT_MATS], out_kinds, f"grads_out_start{l}")
    dp, dlbs = hgrn_bwd(p, w["lbs"] + token[0:1, 0:1], s["st_b"], dob, dp)
    dq8, dk8, dvv, dsm8 = gdn_bwd(s["qn"], s["kn"], s["vv"], s["sm"], s["st_c"], s["ti"], doc)
    dp, dcc, dsmall = mixc_pre_bwd(p, w["conv_c"], w["alog_l"], w["dtb_l"], dq8, dk8, dvv, dsm8, dp)
    dp, dca = mixa_bwd(p, w["conv_a"], dya, dp)
    gf_win, gb_win = win_from_padded(matmul_tn(s["h"], dp, "dw_in"), chip)
    sent_in, token = exchange_start([gb_win], ["slot"], f"grads_in_start{l}")
    dx, dnw = inproj_bwd(dp, w["w_in"], s["x"], w["norm_w"] + token[0:1, 0:1], dxo)
    recv_out = exchange_wait(sent_out, out_kinds, dx, f"grads_out_wait{l}")
    recv_in = exchange_wait(sent_in, ["slot"], dx, f"grads_in_wait{l}")
    half = {"w_in": partial_sum(gf_win, "own", recv_in[0], chip, "psum_w_in", transposed=True)}
    for (n, kind), r in zip(OUT_MATS, recv_out):
        half[n] = partial_sum(full[n][0], kind, r, chip, "psum_" + n)
    small = dict(norm_w=dnw, b_gate=dbg, hgrn_norm_w=dhw, gdn_norm_w=dgw, lbs=dlbs, conv_a=dca[0:3], conv_c=dcc[0:4],
                 a_log=dsmall[0:1, 8:16], dt_bias=dsmall[1:2, 8:16])
    return dx, small, half


def lbs_fwd(lb):
    def body(lb_ref, o_ref):
        l0, l1 = lb_ref[0:1, :], lb_ref[1:2, :]
        mx = jnp.maximum(l0, l1)
        e0, e1 = jnp.exp(l0 - mx), jnp.exp(l1 - mx)
        o_ref[0:1, :] = jnp.zeros_like(l0)
        o_ref[1:2, :] = e1 / (e0 + e1)
    return pl.pallas_call(body, name="lbs_fwd", out_shape=jax.ShapeDtypeStruct(lb.shape, f32))(lb)


def _adam_math(w, g, m, v):
    mn = ADAM_B1 * m + (1.0 - ADAM_B1) * g
    vn = ADAM_B2 * v + (1.0 - ADAM_B2) * (g * g)
    mh = mn / (1.0 - ADAM_B1 ** ADAM_STEP)
    vh = vn / (1.0 - ADAM_B2 ** ADAM_STEP)
    return -ADAM_LR * (mh / (jnp.sqrt(vh) + ADAM_EPS) + ADAM_WD * w), mn, vn


def adam(w, g, m, v, name):
    R, C = w.shape
    tr = 256 if R % 256 == 0 else R

    def body(w_ref, g_ref, m_ref, v_ref, d_ref, mo_ref, vo_ref):
        d, mn, vn = _adam_math(w_ref[...], g_ref[...], m_ref[...], v_ref[...])
        d_ref[...] = d
        mo_ref[...] = mn
        vo_ref[...] = vn

    spec = pl.BlockSpec((tr, C), lambda i: (i, 0))
    return pl.pallas_call(
        body, name=name, grid=(R // tr,), in_specs=[spec] * 4, out_specs=[spec] * 3,
        out_shape=[jax.ShapeDtypeStruct((R, C), f32)] * 3, compiler_params=_cp(("parallel",)),
    )(w, g, m, v)


def adam_pair(h, hs, w, m, v, name):
    _, R, C = w.shape
    cp = h[0].shape[1]
    tr = 128 if R % 128 == 0 else R
    nt = R // tr

    def body(h0_ref, h1_ref, s0_ref, s1_ref, w_ref, m_ref, v_ref, g_ref, d_ref, mo_ref, vo_ref):
        def update(h_ref, s_ref):
            g = (h_ref[...] + s_ref[...])[:, :C]
            d, mn, vn = _adam_math(w_ref[...], g, m_ref[...], v_ref[...])
            g_ref[...] = g
            d_ref[...] = d
            mo_ref[...] = mn
            vo_ref[...] = vn

        @pl.when(pl.program_id(0) == 0)
        def _():
            update(h0_ref, s0_ref)

        @pl.when(pl.program_id(0) == 1)
        def _():
            update(h1_ref, s1_ref)

    h0spec = pl.BlockSpec((tr, cp), lambda l, i: (jnp.where(l == 0, i, nt - 1), 0))
    h1spec = pl.BlockSpec((tr, cp), lambda l, i: (jnp.where(l == 1, i, 0), 0))
    spec = pl.BlockSpec((None, tr, C), lambda l, i: (l, i, 0))
    return pl.pallas_call(
        body, name=name, grid=(2, nt), in_specs=[h0spec, h1spec, h0spec, h1spec, spec, spec, spec],
        out_specs=[spec] * 4, out_shape=[jax.ShapeDtypeStruct(w.shape, f32)] * 4,
        compiler_params=_cp(("arbitrary", "arbitrary")),
    )(h[0], h[1], hs[0], hs[1], w, m, v)


_SMALL = (("norm_w", 2 * D), ("b_gate", 6 * D), ("lower_bounds", None), ("hgrn_norm_w", 2 * HK),
          ("gdn_norm_w", 2 * HK), ("a_log", 16), ("dt_bias", 16), ("final_norm_w", D), ("loss", None))
_CONV = (("conv_a", 2 * 3 * 512), ("conv_c", 2 * 4 * 2048))


def _small_rows(n):
    return 16 if n is None else -(-n // 1024) * 8


LB_ROW = sum(_small_rows(n) for _, n in _SMALL[:2])
ADAM_ROWS = sum(_small_rows(n) for _, n in _SMALL)
SMALL_ROWS = ADAM_ROWS + sum(_small_rows(n) for _, n in _CONV)


def small_update(parts, wp, mp, vp):
    def body(p_ref, w_ref, m_ref, v_ref, g_ref, d_ref, mo_ref, vo_ref):
        gs = p_ref[0]
        for i in range(1, 8):
            gs = gs + p_ref[i]
        w = w_ref[...]
        l0, l1 = w[LB_ROW:LB_ROW + 8], w[LB_ROW + 8:LB_ROW + 16]
        mx = jnp.maximum(l0, l1)
        e0, e1 = jnp.exp(l0 - mx), jnp.exp(l1 - mx)
        p0, p1 = e0 / (e0 + e1), e1 / (e0 + e1)
        dl1 = gs[LB_ROW + 8:LB_ROW + 16]
        s = p1 * dl1
        g = jnp.concatenate([gs[0:LB_ROW], -p0 * s, p1 * dl1 - p1 * s, gs[LB_ROW + 16:ADAM_ROWS]], axis=0)
        d, mn, vn = _adam_math(w, g, m_ref[...], v_ref[...])
        g_ref[0:ADAM_ROWS, :] = g
        g_ref[ADAM_ROWS:, :] = gs[ADAM_ROWS:]
        d_ref[...] = d
        mo_ref[...] = mn
        vo_ref[...] = vn
    sd = jax.ShapeDtypeStruct
    return pl.pallas_call(body, name="small_update",
                          out_shape=[sd((SMALL_ROWS, 128), f32)] + [sd((ADAM_ROWS, 128), f32)] * 3)(parts, wp, mp, vp)


def partial_sum(own, kind, recv, chip, name, transposed=False):
    _, r, c = recv.shape
    tr = 256 if r % 256 == 0 else r

    def body(chip_ref, o_ref, r_ref, out_ref):
        s = ((o_ref[...] + r_ref[0].astype(f32)) + r_ref[1].astype(f32)) + r_ref[2].astype(f32)
        out_ref[...] = s.T if transposed else s

    own_spec = {"own": pl.BlockSpec((tr, c), lambda i, chip: (i, 0)),
                "slot": pl.BlockSpec((None, tr, c), lambda i, chip: (chip[0], i, 0)),
                "cols": pl.BlockSpec((tr, c), lambda i, chip: (i, chip[0])),
                "rows": pl.BlockSpec((tr, c), lambda i, chip: (chip[0] * (r // tr) + i, 0))}[kind]
    out_spec = pl.BlockSpec((c, tr), lambda i, chip: (0, i)) if transposed else pl.BlockSpec((tr, c), lambda i, chip: (i, 0))
    return pl.pallas_call(
        body, name=name,
        grid_spec=pltpu.PrefetchScalarGridSpec(
            num_scalar_prefetch=1, grid=(r // tr,),
            in_specs=[own_spec, pl.BlockSpec((3, tr, c), lambda i, chip: (0, i, 0))], out_specs=out_spec),
        out_shape=jax.ShapeDtypeStruct((c, r) if transposed else (r, c), f32), compiler_params=_cp(("arbitrary",)),
    )(chip, own, recv)


def adam_pair_t(h, hs, wt, mt, vt, name):
    C, _, R = wt.shape
    tc = 128

    def body(h0_ref, h1_ref, s0_ref, s1_ref, w_ref, m_ref, v_ref, g_ref, d_ref, mo_ref, vo_ref):
        g = jnp.stack([h0_ref[...] + s0_ref[...], h1_ref[...] + s1_ref[...]], axis=1)
        d, mn, vn = _adam_math(w_ref[...], g, m_ref[...], v_ref[...])
        g_ref[...] = g
        d_ref[...] = d
        mo_ref[...] = mn
        vo_ref[...] = vn

    hspec = pl.BlockSpec((tc, R), lambda i: (i, 0))
    spec = pl.BlockSpec((tc, 2, R), lambda i: (i, 0, 0))
    return pl.pallas_call(
        body, name=name, grid=(pl.cdiv(C, tc),), in_specs=[hspec] * 4 + [spec] * 3, out_specs=[spec] * 4,
        out_shape=[jax.ShapeDtypeStruct(wt.shape, f32)] * 4, compiler_params=_cp(("parallel",)),
    )(h[0], h[1], hs[0], hs[1], wt, mt, vt)


MESH = pl.DeviceIdType.MESH
_HBM = pl.BlockSpec(memory_space=pltpu.HBM)


def _place():
    return lax.axis_index("x"), lax.axis_index("y"), lax.axis_index("c")


def weight_gather(arrs):
    n = len(arrs)

    def body(*refs):
        x_refs, out_refs = refs[:n], refs[n:2 * n]
        send_sems, recv_sems, local_sems = refs[2 * n:]
        x, y, c = _place()
        me, sibling = (x, y, c), (x, y, 1 - c)
        chips = [(1 - x, y), (x, 1 - y), (1 - x, 1 - y)]

        def copy(a, k, block, to, own_src=False):
            px, py, pc = block
            dst = out_refs[a].at[2 * px + py, pc]
            return pltpu.make_async_remote_copy(
                src_ref=x_refs[a].at[c] if own_src else dst, dst_ref=dst,
                send_sem=send_sems.at[7 * a + k], recv_sem=recv_sems.at[7 * a + k], device_id=to, device_id_type=MESH)

        mine = [pltpu.make_async_copy(x_refs[a].at[c], out_refs[a].at[2 * x + y, c], local_sems.at[a])
                for a in range(n)]
        for cp in mine:
            cp.start()
        first = []
        for a in range(n):
            first.append(copy(a, 0, me, sibling, own_src=True))
            first += [copy(a, 1 + j, me, (*chip, c), own_src=True) for j, chip in enumerate(chips)]
        for cp in first:
            cp.start()
        passed = []
        for j, chip in enumerate(chips):
            for a in range(n):
                copy(a, 1 + j, (*chip, c), me).wait_recv()
                fwd = copy(a, 4 + j, (*chip, c), sibling)
                fwd.start()
                passed.append(fwd)
        for a in range(n):
            copy(a, 0, sibling, me).wait_recv()
            for j, chip in enumerate(chips):
                copy(a, 4 + j, (*chip, 1 - c), me).wait_recv()
        for cp in first + passed:
            cp.wait_send()
        for cp in mine:
            cp.wait()

    return pl.pallas_call(
        body, name="weight_gather", in_specs=[_HBM] * n, out_specs=[_HBM] * n,
        out_shape=[jax.ShapeDtypeStruct((N_CHIPS,) + a.shape, a.dtype) for a in arrs],
        scratch_shapes=[pltpu.SemaphoreType.DMA((7 * n,)), pltpu.SemaphoreType.DMA((7 * n,)),
                        pltpu.SemaphoreType.DMA((n,))],
    )(*arrs)


SHARD_W = 256


_SEM = pl.BlockSpec(memory_space=pltpu.SEMAPHORE)
_EFFECT = pltpu.SideEffectType.DATAFLOW_SIDE_EFFECTING


def _landing_shape(a, kind):
    if kind == "all":
        return (N_CHIPS,) + a.shape
    if kind == "slot":
        return (3,) + a.shape[1:]
    return (3,) + ((a.shape[0], SHARD_W) if kind == "cols" else (SHARD_W, a.shape[1]))


def _shard_copies(src_refs, land_refs, kinds, send_sems, recv_sems):
    x, y, c = _place()
    copies = []
    for a, (src, land, kind) in enumerate(zip(src_refs, land_refs, kinds)):
        for j, (px, py) in enumerate(((1 - x, y), (x, 1 - y), (1 - x, 1 - y))):
            q = 2 * px + py
            lo = pl.multiple_of(q * SHARD_W, SHARD_W)
            part = {"slot": lambda: src.at[q], "cols": lambda: src.at[:, pl.ds(lo, SHARD_W)],
                    "rows": lambda: src.at[pl.ds(lo, SHARD_W), :], "all": lambda: src}[kind]()
            k = 3 * a + j
            copies.append(pltpu.make_async_remote_copy(
                src_ref=part, dst_ref=land.at[2 * x + y] if kind == "all" else land.at[j],
                send_sem=send_sems.at[k], recv_sem=recv_sems.at[k], device_id=(px, py, c), device_id_type=MESH))
    return copies


def exchange_start(srcs, kinds, name, after=None):
    n = len(srcs)
    lands = [lax.empty(_landing_shape(a, k), a.dtype) for a, k in zip(srcs, kinds)]
    extra = [] if after is None else [after]

    def body(*refs):
        src_refs, land_refs, token = refs[:n], refs[n:2 * n], refs[-1]
        send_sems, recv_sems = refs[2 * n + len(extra)], refs[2 * n + len(extra) + 1]
        for cp in _shard_copies(src_refs, land_refs, kinds, send_sems, recv_sems):
            cp.start()
        token[...] = jnp.zeros_like(token)

    both = list(srcs) + lands
    out = pl.pallas_call(
        body, name=name,
        out_shape=(pltpu.SemaphoreType.DMA((3 * n,)), pltpu.SemaphoreType.DMA((3 * n,)),
                   *[pltpu.HBM(a.shape, a.dtype) for a in both], jax.ShapeDtypeStruct((8, 128), f32)),
        in_specs=[_HBM] * (2 * n) + [pl.BlockSpec(memory_space=pl.ANY)] * len(extra),
        out_specs=(_SEM, _SEM, *[_HBM] * (2 * n), pl.BlockSpec(memory_space=pltpu.VMEM)),
        input_output_aliases={i: 2 + i for i in range(2 * n)},
        compiler_params=pltpu.CompilerParams(has_side_effects=_EFFECT),
    )(*[pltpu.with_memory_space_constraint(a, pltpu.HBM) for a in both], *extra)
    return (out[0], out[1], out[2:2 + 2 * n]), out[-1]


def exchange_wait(handle, kinds, after, name):
    send_sems, recv_sems, both = handle
    n = len(kinds)
    after = after if isinstance(after, tuple) else (after,)

    def body(*refs):
        src_refs, land_refs, s_sems, r_sems = refs[:n], refs[n:2 * n], refs[2 * n], refs[2 * n + 1]
        for cp in _shard_copies(src_refs, land_refs, kinds, s_sems, r_sems):
            cp.wait_send()
            cp.wait_recv()

    out = pl.pallas_call(
        body, name=name, out_shape=tuple(pltpu.HBM(a.shape, a.dtype) for a in both),
        in_specs=[_HBM] * (2 * n) + [_SEM, _SEM] + [pl.BlockSpec(memory_space=pl.ANY)] * len(after),
        out_specs=tuple([_HBM] * (2 * n)), input_output_aliases={i: i for i in range(2 * n)},
        compiler_params=pltpu.CompilerParams(has_side_effects=_EFFECT),
    )(*both, send_sems, recv_sems, *after)
    return out[n:]


def final_exchange(hs, small):
    n = len(hs)
    S = small.shape[0]

    def body(*refs):
        h_refs, sm_ref, out_refs, smalls_ref = refs[:n], refs[n], refs[n + 1:2 * n + 1], refs[2 * n + 1]
        send_sems, recv_sems, local_sem = refs[2 * n + 2:]
        x, y, c = _place()
        my_slot = smalls_ref.at[4 * x + 2 * y + c]
        mine = pltpu.make_async_copy(sm_ref, my_slot, local_sem)
        mine.start()
        copies = [pltpu.make_async_remote_copy(src_ref=h_refs[a], dst_ref=out_refs[a], send_sem=send_sems.at[a],
                                               recv_sem=recv_sems.at[a], device_id=(x, y, 1 - c), device_id_type=MESH)
                  for a in range(n)]
        for mask in range(1, 8):
            fx, fy, fc = (mask >> 2) & 1, (mask >> 1) & 1, mask & 1
            peer = ((1 - x) if fx else x, (1 - y) if fy else y, (1 - c) if fc else c)
            copies.append(pltpu.make_async_remote_copy(
                src_ref=sm_ref, dst_ref=my_slot, send_sem=send_sems.at[n - 1 + mask], recv_sem=recv_sems.at[n - 1 + mask],
                device_id=peer, device_id_type=MESH))
        for cp in copies:
            cp.start()
        for cp in copies:
            cp.wait_recv()
        for cp in copies:
            cp.wait_send()
        mine.wait()

    sd = jax.ShapeDtypeStruct
    out = pl.pallas_call(
        body, name="final_exchange", in_specs=[_HBM] * (n + 1), out_specs=[_HBM] * (n + 1),
        out_shape=[sd(h.shape, h.dtype) for h in hs] + [sd((8, S, 128), f32)],
        scratch_shapes=[pltpu.SemaphoreType.DMA((n + 7,)), pltpu.SemaphoreType.DMA((n + 7,)),
                        pltpu.SemaphoreType.DMA],
    )(*hs, small)
    return out[:n], out[n]


N_CHIPS = 4
SHARD_COLS = N_ORIG // N_CHIPS


SHARD_PAD = 2688
_COL_SEGMENTS = (
    ((0, 2048, OFF_A),)
    + tuple((2048 + 512 * j + HK * h, 2048 + 512 * j + HK * (h + 1), OFF_B + REG_BH * h + HK * j)
            for j in range(3) for h in range(N_HGRN))
    + ((3584, 4096, OFF_M + M_BZ), (4096, 6144, OFF_C), (6144, 6160, OFF_C + 2048), (6160, 7184, OFF_M + M_CZ),
       (7184, N_ORIG, OFF_M + M_G)))


def _shard_pieces():
    pieces = []
    for lo, hi, dst in _COL_SEGMENTS:
        for p in range(N_CHIPS):
            a, b = max(lo, p * SHARD_COLS), min(hi, (p + 1) * SHARD_COLS)
            if a < b:
                pieces.append((p, a - p * SHARD_COLS, dst + a - lo, b - a))
    return pieces


def win_cast_pad(wt):
    tc = 128

    def body(x_ref, o0_ref, o1_ref):
        col = pl.program_id(0) * tc + lax.broadcasted_iota(jnp.int32, (tc, 1), 0)
        for l, o_ref in enumerate((o0_ref, o1_ref)):
            o_ref[...] = _b(jnp.where(col < SHARD_COLS, x_ref[:, l, :], 0.0).T)

    spec = pl.BlockSpec((D, tc), lambda i: (0, i))
    return pl.pallas_call(
        body, name="win_cast_pad", grid=(SHARD_PAD // tc,),
        in_specs=[pl.BlockSpec((tc, 2, D), lambda i: (i, 0, 0))], out_specs=[spec, spec],
        out_shape=[jax.ShapeDtypeStruct((D, SHARD_PAD), bf16)] * 2, compiler_params=_cp(("parallel",)),
    )(wt)


def win_to_padded(w4, name):
    tr = 256
    pieces = _shard_pieces()

    def body(a_ref, o_ref):
        o_ref[...] = jnp.zeros((tr, NP), bf16)
        for p, j0, c0, n in pieces:
            o_ref[:, c0:c0 + n] = a_ref[p, :, j0:j0 + n]

    return pl.pallas_call(
        body, name=name, grid=(D // tr,),
        in_specs=[pl.BlockSpec((N_CHIPS, tr, SHARD_PAD), lambda i: (0, i, 0))],
        out_specs=pl.BlockSpec((tr, NP), lambda i: (i, 0)),
        out_shape=jax.ShapeDtypeStruct((D, NP), bf16), compiler_params=_cp(("parallel",)),
    )(w4)


def win_from_padded(dw, chip):
    tr = 128
    pieces = _shard_pieces()

    def body(chip_ref, d_ref, of_ref, ob_ref):
        of_ref[:, SHARD_COLS:] = jnp.zeros((tr, SHARD_PAD - SHARD_COLS), f32)
        for p in range(N_CHIPS):
            ob_ref[p, :, SHARD_COLS:] = jnp.zeros((tr, SHARD_PAD - SHARD_COLS), bf16)
        for p, j0, c0, n in pieces:
            v = d_ref[:, c0:c0 + n]
            ob_ref[p, :, j0:j0 + n] = _b(v)

            @pl.when(chip_ref[0] == p)
            def _(v=v, j0=j0, n=n):
                of_ref[:, j0:j0 + n] = v

    return pl.pallas_call(
        body, name="win_from_padded",
        grid_spec=pltpu.PrefetchScalarGridSpec(
            num_scalar_prefetch=1, grid=(D // tr,),
            in_specs=[pl.BlockSpec((tr, NP), lambda i, chip: (i, 0))],
            out_specs=[pl.BlockSpec((tr, SHARD_PAD), lambda i, chip: (i, 0)),
                       pl.BlockSpec((N_CHIPS, tr, SHARD_PAD), lambda i, chip: (0, i, 0))]),
        out_shape=[jax.ShapeDtypeStruct((D, SHARD_PAD), f32),
                   jax.ShapeDtypeStruct((N_CHIPS, D, SHARD_PAD), bf16)],
        compiler_params=_cp(("arbitrary",)),
    )(chip, dw)


def _rows128(a):
    flat = a.reshape(-1)
    total = -(-flat.shape[0] // 1024) * 1024
    return jnp.pad(flat, (0, total - flat.shape[0])).reshape(total // 128, 128)


def _lb_rows(lb):
    return jnp.pad(lb.reshape(2, 4, 128), ((0, 0), (0, 4), (0, 0))).reshape(16, 128)


def _pack_small(v, with_conv):
    rows = []
    for name, n in _SMALL + (_CONV if with_conv else ()):
        if name == "lower_bounds":
            rows.append(_lb_rows(v[name]))
        elif name == "loss":
            rows.append(jnp.broadcast_to(v[name], (16, 128)) if name in v else jnp.zeros((16, 128), f32))
        else:
            rows.append(_rows128(v[name]))
    return jnp.concatenate(rows, axis=0)


def _unpack_small(p, shapes, with_conv):
    out, row = {}, 0
    for name, n in _SMALL + (_CONV if with_conv else ()):
        nrows = _small_rows(n)
        blk = p[row:row + nrows]
        if name == "lower_bounds":
            out[name] = blk.reshape(2, 8, 128)[:, :4].reshape(2, 512)
        elif name == "loss":
            out[name] = blk[0, 0]
        else:
            out[name] = blk.reshape(-1)[:n].reshape(shapes[name])
        row += nrows
    return out


def _lane_vec(a8):
    return jnp.pad(a8.reshape(1, 8), ((0, 0), (8, 112)))


WEIGHT_NAMES = ("norm_w", "w_in", "b_gate", "conv_a", "conv_c", "a_log", "dt_bias", "lower_bounds", "hgrn_norm_w",
                "gdn_norm_w", "w_out_a", "w_out_b", "w_out_c", "w_o", "final_norm_w")


def kernel(x, norm_w, w_in, b_gate, conv_a, conv_c, a_log, dt_bias, lower_bounds, hgrn_norm_w, gdn_norm_w, w_out_a, w_out_b, w_out_c, w_o, final_norm_w, loss_target, m_norm_w, m_w_in, m_b_gate, m_conv_a, m_conv_c, m_a_log, m_dt_bias, m_lower_bounds, m_hgrn_norm_w, m_gdn_norm_w, m_w_out_a, m_w_out_b, m_w_out_c, m_w_o, m_final_norm_w, v_norm_w, v_w_in, v_b_gate, v_conv_a, v_conv_c, v_a_log, v_dt_bias, v_lower_bounds, v_hgrn_norm_w, v_gdn_norm_w, v_w_out_a, v_w_out_b, v_w_out_c, v_w_o, v_final_norm_w):
    wts = dict(norm_w=norm_w, w_in=w_in, b_gate=b_gate, conv_a=conv_a, conv_c=conv_c, a_log=a_log, dt_bias=dt_bias,
               lower_bounds=lower_bounds, hgrn_norm_w=hgrn_norm_w, gdn_norm_w=gdn_norm_w, w_out_a=w_out_a,
               w_out_b=w_out_b, w_out_c=w_out_c, w_o=w_o, final_norm_w=final_norm_w)
    mom = dict(norm_w=m_norm_w, w_in=m_w_in, b_gate=m_b_gate, conv_a=m_conv_a, conv_c=m_conv_c, a_log=m_a_log,
               dt_bias=m_dt_bias, lower_bounds=m_lower_bounds, hgrn_norm_w=m_hgrn_norm_w, gdn_norm_w=m_gdn_norm_w,
               w_out_a=m_w_out_a, w_out_b=m_w_out_b, w_out_c=m_w_out_c, w_o=m_w_o, final_norm_w=m_final_norm_w)
    var = dict(norm_w=v_norm_w, w_in=v_w_in, b_gate=v_b_gate, conv_a=v_conv_a, conv_c=v_conv_c, a_log=v_a_log,
               dt_bias=v_dt_bias, lower_bounds=v_lower_bounds, hgrn_norm_w=v_hgrn_norm_w, gdn_norm_w=v_gdn_norm_w,
               w_out_a=v_w_out_a, w_out_b=v_w_out_b, w_out_c=v_w_out_c, w_o=v_w_o, final_norm_w=v_final_norm_w)
    chip = 2 * lax.axis_index("x") + lax.axis_index("y")
    chip1 = chip.reshape(1).astype(jnp.int32)

    win_l0, win_l1 = win_cast_pad(jnp.transpose(w_in, (2, 0, 1)))
    win4_l0, ca4, cc4 = weight_gather([win_l0.reshape(2, D // 2, SHARD_PAD), conv_a, conv_c])
    by_cols = lambda a: a.transpose(1, 2, 0, 3).reshape(a.shape[1], a.shape[2], N_CHIPS * a.shape[3])
    by_rows = lambda a: a.transpose(1, 0, 2, 3).reshape(a.shape[1], N_CHIPS * a.shape[2], a.shape[3])
    conv_a_full, conv_c_full = by_cols(ca4), by_cols(cc4)
    later = [win_l1, _b(w_out_a), _b(w_out_b), _b(w_out_c), _b(w_o)]
    sent_w, token = exchange_start(later, ["all"] * 5, "weights_start", after=win4_l0)

    def late_weights(after):
        lands = exchange_wait(sent_w, ["all"] * 5, after, "weights_wait")
        l1, woa4, wob4, woc4, wo4 = (lax.dynamic_update_index_in_dim(land, own, chip, 0)
                                     for land, own in zip(lands, later))
        outs = dict(w_out_a=by_cols(woa4), w_out_b=by_cols(wob4), w_out_c=by_rows(woc4), w_o=by_rows(wo4))
        layers[1].update(outs, w_in=win_to_padded(l1, "win_to_padded1"))
        return outs

    lbs = lbs_fwd(lower_bounds)
    layers = []
    for l in range(2):
        layers.append(dict(
            l=l, norm_w=norm_w[l:l + 1], b_gate=b_gate[l:l + 1], conv_a=conv_a_full[l], conv_c=conv_c_full[l],
            alog_l=_lane_vec(a_log[l]), dtb_l=_lane_vec(dt_bias[l]), lbs=lbs[l:l + 1],
            hgrn_norm_w=hgrn_norm_w[l:l + 1], gdn_norm_w=gdn_norm_w[l:l + 1]))
    layers[0].update(w_in=win_to_padded(win4_l0.reshape(N_CHIPS, D, SHARD_PAD), "win_to_padded0"), late=late_weights,
                     norm_w=norm_w[0:1] + token[0:1, 0:1])

    xs, saved = x[0], []
    for l in range(2):
        xs, s = layer_fwd(xs, layers[l])
        saved.append(s)
    loss_row, dx, dfw = loss_head(xs, final_norm_w.reshape(1, D), loss_target[0])
    lg, half = [None, None], [None, None]
    for l in (1, 0):
        dx, lg[l], half[l] = layer_bwd(dx, layers[l], saved[l], chip1)
    grad_x = dx[None]

    stack = lambda n: jnp.stack([lg[0][n], lg[1][n]], axis=0)
    gsmall = {n: stack(n) for n in ("norm_w", "b_gate", "hgrn_norm_w", "gdn_norm_w", "a_log", "dt_bias", "conv_a",
                                    "conv_c")}
    gsmall.update(lower_bounds=stack("lbs"), final_norm_w=dfw, loss=loss_row)
    mat_names = ("w_in",) + tuple(n for n, _ in OUT_MATS)
    mine = [half[l][n] for n in mat_names for l in range(2)]
    theirs, smalls = final_exchange(mine, _pack_small(gsmall, True))

    out_g, out_d, out_m, out_v = {}, {}, {}, {}
    for i, n in enumerate(mat_names):
        h, hs = mine[2 * i:2 * i + 2], theirs[2 * i:2 * i + 2]
        if n == "w_in":
            fwd, back = (lambda a: jnp.transpose(a, (2, 0, 1))), (lambda a: jnp.transpose(a, (1, 2, 0)))
            res = adam_pair_t(h, hs, fwd(wts[n]), fwd(mom[n]), fwd(var[n]), "adam_" + n)
            out_g[n], out_d[n], out_m[n], out_v[n] = (back(a) for a in res)
        else:
            out_g[n], out_d[n], out_m[n], out_v[n] = adam_pair(h, hs, wts[n], mom[n], var[n], "adam_" + n)
    small_names = [n for n, _ in _SMALL if n != "loss"]
    pack = lambda v: _pack_small({n: v[n] for n in small_names}, False)
    sg, sd, smn, svn = small_update(smalls, pack(wts), pack(mom), pack(var))
    shapes = {n: wts[n].shape for n in small_names}
    shapes.update(conv_a=(2, 3, 512), conv_c=(2, 4, 2048))
    for dst, src, conv in ((out_g, sg, True), (out_d, sd, False), (out_m, smn, False), (out_v, svn, False)):
        dst.update(_unpack_small(src, shapes, conv))
    loss = out_g.pop("loss")
    for n in ("conv_a", "conv_c"):
        width = wts[n].shape[2]
        g = lax.dynamic_slice_in_dim(out_g[n], chip * width, width, axis=2)
        two_d = lambda a: a.reshape(-1, width)
        d, mn, vn = adam(two_d(wts[n]), two_d(g), two_d(mom[n]), two_d(var[n]), "adam_" + n)
        out_g[n] = g
        out_d[n], out_m[n], out_v[n] = (a.reshape(wts[n].shape) for a in (d, mn, vn))
    return (loss, grad_x, *[out_g[n] for n in WEIGHT_NAMES], *[out_d[n] for n in WEIGHT_NAMES],
            *[out_m[n] for n in WEIGHT_NAMES], *[out_v[n] for n in WEIGHT_NAMES])
```

```python
import jax
import jax.numpy as jnp
from jax import lax
from jax.experimental import pallas as pl
from jax.experimental.pallas import tpu as pltpu

f32 = jnp.float32
bf16 = jnp.bfloat16

D = 1024
L = 64
SUB = 16
NORM_EPS = 1e-6
L2_EPS = 1e-6
MIN_F = 1e-30
HK = 128
QK_SCALE = HK ** -0.5
N_GDN = 8
GDN_BLOCK = 1024
N_HGRN = 4

REG_A = 2304
REG_C = 2304
REG_M = 4608
REG_BH = 384
OFF_A, OFF_C, OFF_M, OFF_B = 0, 2304, 4608, 9216
M_BZ, M_G, M_CZ = 0, 512, 3584
NP = 10752
NP_TILE = 1536
N_ORIG = 10256

ADAM_LR, ADAM_B1, ADAM_B2, ADAM_EPS, ADAM_WD, ADAM_STEP = 0.001, 0.9, 0.999, 1e-08, 0.01, 10

VMEM_LIMIT = 56 * 1024 * 1024


def _cp(sem):
    return pltpu.CompilerParams(dimension_semantics=sem, vmem_limit_bytes=VMEM_LIMIT)


def _sigmoid(x):
    return jax.nn.sigmoid(x)


def _silu(x):
    return x * _sigmoid(x)


def _silu2(x):
    s = _sigmoid(x)
    y = x * s
    return y, s + y * (1.0 - s)


def _softplus(x):
    u = jnp.exp(-jnp.abs(x))
    w = 1.0 + u
    l1p = jnp.where(w == 1.0, u, jnp.log(w) * (u / (w - 1.0)))
    return jnp.maximum(x, 0.0) + l1p


def _dot(a, b):
    return jnp.dot(a, b, preferred_element_type=f32)


def _dot_nt(a, b):
    return lax.dot_general(a, b, (((1,), (1,)), ((), ())), preferred_element_type=f32)


def _dot_tn(a, b):
    return lax.dot_general(a, b, (((0,), (0,)), ((), ())), preferred_element_type=f32)


def _bdot(a, b):
    return lax.dot_general(a, b, (((2,), (1,)), ((0,), (0,))), preferred_element_type=f32)


def _bdot_nt(a, b):
    return lax.dot_general(a, b, (((2,), (2,)), ((0,), (0,))), preferred_element_type=f32)


def _bdot_tn(a, b):
    return lax.dot_general(a, b, (((1,), (1,)), ((0,), (0,))), preferred_element_type=f32)


def _bdot_split(a, b):
    ah, bh = _b(a), _b(b)
    al, bl = _b(a - ah.astype(f32)), _b(b - bh.astype(f32))
    return _bdot(ah, bh) + (_bdot(ah, bl) + _bdot(al, bh))


def _b(x):
    return x.astype(bf16)


def _chunk_cumsum(x, rows_in_chunk):
    n = x.shape[0]
    for s in (1, 2, 4, 8, 16, 32):
        x = x + jnp.where(rows_in_chunk >= s, pltpu.roll(x, s, axis=0), 0.0)
    return x


def _chunk_rev_cumsum(x, rows_in_chunk):
    n = x.shape[0]
    for s in (1, 2, 4, 8, 16, 32):
        x = x + jnp.where(rows_in_chunk + s < L, pltpu.roll(x, n - s, axis=0), 0.0)
    return x


def _shift_down(x, s):
    return pltpu.roll(x, s, axis=0) if s else x


def _shift_up(x, s):
    return pltpu.roll(x, x.shape[0] - s, axis=0) if s else x


def inproj_fwd(x, nw, w):
    T = x.shape[0]
    tT, tn = min(2048, T), NP_TILE

    def body(x_ref, nw_ref, w_ref, p_ref, h_ref, hs):
        @pl.when(pl.program_id(1) == 0)
        def _():
            xv = x_ref[...]
            r = lax.rsqrt(jnp.mean(xv * xv, axis=-1, keepdims=True) + NORM_EPS)
            hv = _b(xv * r * nw_ref[...])
            hs[...] = hv
            h_ref[...] = hv
        p_ref[...] = _b(_dot(hs[...], w_ref[...]))

    return pl.pallas_call(
        body, name="inproj_fwd", grid=(T // tT, NP // tn),
        in_specs=[pl.BlockSpec((tT, D), lambda i, j: (i, 0)), pl.BlockSpec((1, D), lambda i, j: (0, 0)),
                  pl.BlockSpec((D, tn), lambda i, j: (0, j))],
        out_specs=[pl.BlockSpec((tT, tn), lambda i, j: (i, j)), pl.BlockSpec((tT, D), lambda i, j: (i, 0))],
        out_shape=[jax.ShapeDtypeStruct((T, NP), bf16), jax.ShapeDtypeStruct((T, D), bf16)],
        scratch_shapes=[pltpu.VMEM((tT, D), bf16)],
        compiler_params=_cp(("parallel", "arbitrary")),
    )(x, nw, w)


def matmul_tn(a, b, name, n=None, b_col0=0, with_bf16=False):
    T, K = a.shape
    N = b.shape[1] if n is None else n
    tT = min(2048, T)
    tn = NP_TILE if N % NP_TILE == 0 else min(N, 1024)
    nt = T // tT
    cb0 = b_col0 // tn

    def body(a_ref, b_ref, o_ref, *ob_ref):
        @pl.when(pl.program_id(1) == 0)
        def _():
            o_ref[...] = jnp.zeros_like(o_ref)
        o_ref[...] += _dot_tn(_b(a_ref[...]), _b(b_ref[...]))
        if with_bf16:
            @pl.when(pl.program_id(1) == nt - 1)
            def _():
                ob_ref[0][...] = _b(o_ref[...])

    ospec = pl.BlockSpec((K, tn), lambda j, t: (0, j))
    return pl.pallas_call(
        body, name=name, grid=(N // tn, nt),
        in_specs=[pl.BlockSpec((tT, K), lambda j, t: (t, 0)), pl.BlockSpec((tT, tn), lambda j, t: (t, cb0 + j))],
        out_specs=[ospec, ospec] if with_bf16 else ospec,
        out_shape=([jax.ShapeDtypeStruct((K, N), f32), jax.ShapeDtypeStruct((K, N), bf16)] if with_bf16
                   else jax.ShapeDtypeStruct((K, N), f32)),
        compiler_params=_cp(("parallel", "arbitrary")),
    )(a, b)


def inproj_bwd(dp, w, x, nw, dres):
    T = x.shape[0]
    tT, tk = min(1024, T), NP // 4
    nk = NP // tk

    def body(dp_ref, w_ref, x_ref, nw_ref, dres_ref, dx_ref, dnw_ref, acc):
        i, k = pl.program_id(0), pl.program_id(1)

        @pl.when((i == 0) & (k == 0))
        def _():
            dnw_ref[...] = jnp.zeros_like(dnw_ref)

        @pl.when(k == 0)
        def _():
            acc[...] = jnp.zeros_like(acc)
        acc[...] += _dot_nt(dp_ref[...], w_ref[...])

        @pl.when(k == nk - 1)
        def _():
            xv = x_ref[...]
            r = lax.rsqrt(jnp.mean(xv * xv, axis=-1, keepdims=True) + NORM_EPS)
            xh = xv * r
            dy = acc[...]
            dyw = dy * nw_ref[...]
            dx_ref[...] = r * (dyw - xh * jnp.mean(dyw * xh, axis=-1, keepdims=True)) + dres_ref[...]
            dnw_ref[...] += jnp.sum(dy * xh, axis=0, keepdims=True)

    return pl.pallas_call(
        body, name="inproj_bwd", grid=(T // tT, nk),
        in_specs=[pl.BlockSpec((tT, tk), lambda i, k: (i, k)), pl.BlockSpec((D, tk), lambda i, k: (0, k)),
                  pl.BlockSpec((tT, D), lambda i, k: (i, 0)), pl.BlockSpec((1, D), lambda i, k: (0, 0)),
                  pl.BlockSpec((tT, D), lambda i, k: (i, 0))],
        out_specs=[pl.BlockSpec((tT, D), lambda i, k: (i, 0)), pl.BlockSpec((1, D), lambda i, k: (0, 0))],
        out_shape=[jax.ShapeDtypeStruct((T, D), f32), jax.ShapeDtypeStruct((1, D), f32)],
        scratch_shapes=[pltpu.VMEM((tT, D), f32)],
        compiler_params=_cp(("arbitrary", "arbitrary")),
    )(dp, w, x, nw, dres)


def loss_head(x, fw, tgt):
    T = x.shape[0]
    tT = min(1024, T)

    def body(x_ref, fw_ref, t_ref, loss_ref, dx_ref, dfw_ref):
        @pl.when(pl.program_id(0) == 0)
        def _():
            loss_ref[...] = jnp.zeros_like(loss_ref)
            dfw_ref[...] = jnp.zeros_like(dfw_ref)
        xv = x_ref[...]
        r = lax.rsqrt(jnp.mean(xv * xv, axis=-1, keepdims=True) + NORM_EPS)
        xh = xv * r
        err = xh * fw_ref[...] - t_ref[...]
        part = 0.5 * jnp.sum(jnp.mean(err * err, axis=-1, keepdims=True), axis=0, keepdims=True)
        loss_ref[...] += jnp.broadcast_to(part, loss_ref.shape)
        dy = err * (1.0 / D)
        dyw = dy * fw_ref[...]
        dx_ref[...] = r * (dyw - xh * jnp.mean(dyw * xh, axis=-1, keepdims=True))
        dfw_ref[...] += jnp.sum(dy * xh, axis=0, keepdims=True)

    return pl.pallas_call(
        body, name="loss_head", grid=(T // tT,),
        in_specs=[pl.BlockSpec((tT, D), lambda i: (i, 0)), pl.BlockSpec((1, D), lambda i: (0, 0)),
                  pl.BlockSpec((tT, D), lambda i: (i, 0))],
        out_specs=[pl.BlockSpec((1, 128), lambda i: (0, 0)), pl.BlockSpec((tT, D), lambda i: (i, 0)),
                   pl.BlockSpec((1, D), lambda i: (0, 0))],
        out_shape=[jax.ShapeDtypeStruct((1, 128), f32), jax.ShapeDtypeStruct((T, D), f32),
                   jax.ShapeDtypeStruct((1, D), f32)],
        compiler_params=_cp(("arbitrary",)),
    )(x, fw, tgt)


def _halo_specs(tT, T, width, colblk, rows=8):
    nb = T // rows
    per = tT // rows
    prev = pl.BlockSpec((rows, width), lambda i: (jnp.maximum(i * per - 1, 0), colblk))
    nxt = pl.BlockSpec((rows, width), lambda i: (jnp.minimum((i + 1) * per, nb - 1), colblk))
    return prev, nxt


def _p_rows(p_ref, width=2048):
    return p_ref[:, 0:width].astype(f32)


def _p_prev(pp_ref, width=2048):
    return pp_ref[:, 0:width].astype(f32)[8:16]


def _p_next(pn_ref, width=2048):
    return pn_ref[:, 0:width].astype(f32)[0:8]


def mixa_fwd(p, cw):
    T = p.shape[0]
    tT = min(1024, T)
    prev_spec, _ = _halo_specs(tT, T, REG_A, OFF_A // REG_A, rows=16)

    def body(p_ref, pp_ref, cw_ref, y_ref):
        pv = _p_rows(p_ref)
        u = pv[:, 512:1024] * pv[:, 1024:1536]
        pp = _p_prev(pp_ref)
        up = jnp.where(pl.program_id(0) == 0, 0.0, pp[:, 512:1024] * pp[:, 1024:1536])
        ue = jnp.concatenate([up, u], axis=0)
        cv = cw_ref[0:1, :] * _shift_down(ue, 2) + cw_ref[1:2, :] * _shift_down(ue, 1) + cw_ref[2:3, :] * ue
        y_ref[...] = _b(pv[:, 0:512] * cv[8:] * _silu(pv[:, 1536:2048]))

    return pl.pallas_call(
        body, name="mixa_fwd", grid=(T // tT,),
        in_specs=[pl.BlockSpec((tT, REG_A), lambda i: (i, OFF_A // REG_A)), prev_spec,
                  pl.BlockSpec((3, 512), lambda i: (0, 0))],
        out_specs=pl.BlockSpec((tT, 512), lambda i: (i, 0)),
        out_shape=jax.ShapeDtypeStruct((T, 512), bf16),
        compiler_params=_cp(("parallel",)),
    )(p, p, cw)


def mixa_bwd(p, cw, dy, dp):
    T = p.shape[0]
    tT = min(1024, T)
    nt = T // tT
    prev_spec, next_spec = _halo_specs(tT, T, REG_A, OFF_A // REG_A, rows=16)
    _, dnext_spec = _halo_specs(tT, T, 512, 0)

    def body(p_ref, pp_ref, pn_ref, cw_ref, dy_ref, dyn_ref, dp_in, dp_ref, dcw_ref):
        i = pl.program_id(0)

        @pl.when(i == 0)
        def _():
            dcw_ref[...] = jnp.zeros_like(dcw_ref)
        pv, pp, pn = _p_rows(p_ref), _p_prev(pp_ref), _p_next(pn_ref)
        pe = jnp.concatenate([pp, pv, pn], axis=0)
        rows = lax.broadcasted_iota(jnp.int32, (tT + 16, 1), 0)
        ab, ac, ax, az = pe[:, 0:512], pe[:, 512:1024], pe[:, 1024:1536], pe[:, 1536:2048]
        u = jnp.where((rows < 8) & (i == 0), 0.0, ac * ax)
        u1, u2 = _shift_down(u, 1), _shift_down(u, 2)
        w0, w1, w2 = cw_ref[0:1, :], cw_ref[1:2, :], cw_ref[2:3, :]
        cv = w0 * u2 + w1 * u1 + w2 * u
        dye = jnp.concatenate([jnp.zeros((8, 512), f32), dy_ref[...], dyn_ref[...]], axis=0)
        dye = jnp.where((rows >= tT + 8) & (i == nt - 1), 0.0, dye)
        sz, dsz = _silu2(az)
        dcv = dye * ab * sz
        du = w2 * dcv + w1 * _shift_up(dcv, 1) + w0 * _shift_up(dcv, 2)
        inner = (rows >= 8) & (rows < tT + 8)
        dcv_in = jnp.where(inner, dcv, 0.0)
        dcw_ref[0:1, :] += jnp.sum(dcv_in * u2, axis=0, keepdims=True)
        dcw_ref[1:2, :] += jnp.sum(dcv_in * u1, axis=0, keepdims=True)
        dcw_ref[2:3, :] += jnp.sum(dcv_in * u, axis=0, keepdims=True)
        sl = slice(8, tT + 8)
        dp_ref[:, 0:512] = _b((dye * cv * sz)[sl])
        dp_ref[:, 512:1024] = _b((du * ax)[sl])
        dp_ref[:, 1024:1536] = _b((du * ac)[sl])
        dp_ref[:, 1536:2048] = _b((dye * ab * cv * dsz)[sl])
        dp_ref[:, 2048:] = jnp.zeros((tT, REG_A - 2048), bf16)

    return pl.pallas_call(
        body, name="mixa_bwd", grid=(nt,),
        in_specs=[pl.BlockSpec((tT, REG_A), lambda i: (i, OFF_A // REG_A)), prev_spec, next_spec,
                  pl.BlockSpec((3, 512), lambda i: (0, 0)),
                  pl.BlockSpec((tT, 512), lambda i: (i, 0)), dnext_spec, pl.BlockSpec(memory_space=pl.ANY)],
        out_specs=[pl.BlockSpec((tT, REG_A), lambda i: (i, OFF_A // REG_A)), pl.BlockSpec((8, 512), lambda i: (0, 0))],
        out_shape=[jax.ShapeDtypeStruct((T, NP), bf16), jax.ShapeDtypeStruct((8, 512), f32)],
        input_output_aliases={6: 0},
        compiler_params=_cp(("arbitrary",)),
    )(p, p, p, cw, dy, dy, dp)


def _l2n_fwd(y):
    return y * lax.rsqrt(jnp.sum(y * y, axis=-1, keepdims=True) + L2_EPS)


def mixc_pre_fwd(p, cw, alog_l, dtb_l):
    T = p.shape[0]
    tT = min(512, T)
    prev_spec, _ = _halo_specs(tT, T, REG_C, OFF_C // REG_C, rows=16)

    def body(p_ref, pp_ref, cw_ref, al_ref, dt_ref, q_ref, k_ref, v_ref, sm_ref):
        pp = jnp.where(pl.program_id(0) == 0, 0.0, _p_prev(pp_ref))
        xe = jnp.concatenate([pp, _p_rows(p_ref)], axis=0)
        cv = (cw_ref[0:1, :] * _shift_down(xe, 3) + cw_ref[1:2, :] * _shift_down(xe, 2)
              + cw_ref[2:3, :] * _shift_down(xe, 1) + cw_ref[3:4, :] * xe)[8:]
        y = _silu(cv)
        for hh in range(4):
            sl = slice(hh * HK, (hh + 1) * HK)
            q_ref[:, sl] = _l2n_fwd(y[:, sl]) * QK_SCALE
            k_ref[:, sl] = _l2n_fwd(y[:, 512 + hh * HK:512 + (hh + 1) * HK])
        v_ref[...] = y[:, 1024:2048]
        ps = p_ref[:, 2048:2176].astype(f32)
        lane = lax.broadcasted_iota(jnp.int32, ps.shape, 1)
        la = -jnp.exp(al_ref[...]) * _softplus(ps + dt_ref[...])
        rin = lax.broadcasted_iota(jnp.int32, ps.shape, 0) % L
        g = _chunk_cumsum(la, rin)
        sm_ref[...] = jnp.where(lane < 8, _sigmoid(ps), jnp.where(lane < 16, g, 0.0))

    return pl.pallas_call(
        body, name="mixc_pre_fwd", grid=(T // tT,),
        in_specs=[pl.BlockSpec((tT, REG_C), lambda i: (i, OFF_C // REG_C)), prev_spec,
                  pl.BlockSpec((4, 2048), lambda i: (0, 0)),
                  pl.BlockSpec((1, 128), lambda i: (0, 0)), pl.BlockSpec((1, 128), lambda i: (0, 0))],
        out_specs=[pl.BlockSpec((tT, 512), lambda i: (i, 0)), pl.BlockSpec((tT, 512), lambda i: (i, 0)),
                   pl.BlockSpec((tT, 1024), lambda i: (i, 0)), pl.BlockSpec((tT, 128), lambda i: (i, 0))],
        out_shape=[jax.ShapeDtypeStruct((T, 512), f32), jax.ShapeDtypeStruct((T, 512), f32),
                   jax.ShapeDtypeStruct((T, 1024), f32), jax.ShapeDtypeStruct((T, 128), f32)],
        compiler_params=_cp(("parallel",)),
    )(p, p, cw, alog_l, dtb_l)


def mixc_pre_bwd(p, cw, alog_l, dtb_l, dq8, dk8, dv, dsm8, dp):
    T = p.shape[0]
    tT = min(256, T)
    nt = T // tT
    prev_spec, next_spec = _halo_specs(tT, T, REG_C, OFF_C // REG_C, rows=16)
    _, n1024 = _halo_specs(tT, T, 1024, 0)

    def body(p_ref, pp_ref, pn_ref, cw_ref, al_ref, dt_ref, dq_ref, dqn_ref, dk_ref, dkn_ref,
             dv_ref, dvn_ref, dsm_ref, dp_in, dp_ref, dcw_ref, dsml_ref):
        i = pl.program_id(0)

        @pl.when(i == 0)
        def _():
            dcw_ref[...] = jnp.zeros_like(dcw_ref)
            dsml_ref[...] = jnp.zeros_like(dsml_ref)
        rows = lax.broadcasted_iota(jnp.int32, (tT + 16, 1), 0)
        pp = jnp.where(i == 0, 0.0, _p_prev(pp_ref))
        xe = jnp.concatenate([pp, _p_rows(p_ref), _p_next(pn_ref)], axis=0)
        xs = [_shift_down(xe, 3), _shift_down(xe, 2), _shift_down(xe, 1), xe]
        cv = cw_ref[0:1, :] * xs[0] + cw_ref[1:2, :] * xs[1] + cw_ref[2:3, :] * xs[2] + cw_ref[3:4, :] * xs[3]
        y, dy_dcv = _silu2(cv)
        last = (rows >= tT + 8) & (i == nt - 1)
        z8q = jnp.zeros((8, 1024), f32)

        def ext(cur_ref, nxt_ref):
            return jnp.where(last, 0.0, jnp.concatenate([z8q, cur_ref[...], nxt_ref[...]], axis=0))
        dq8e, dk8e, dve = ext(dq_ref, dqn_ref), ext(dk_ref, dkn_ref), ext(dv_ref, dvn_ref)
        dys = []
        for (d8, base, scale) in ((dq8e, 0, QK_SCALE), (dk8e, 512, 1.0)):
            for hh in range(4):
                dn = (d8[:, (2 * hh) * HK:(2 * hh + 1) * HK] + d8[:, (2 * hh + 1) * HK:(2 * hh + 2) * HK]) * scale
                yh = y[:, base + hh * HK:base + (hh + 1) * HK]
                r = lax.rsqrt(jnp.sum(yh * yh, axis=-1, keepdims=True) + L2_EPS)
                nh = yh * r
                dys.append(r * (dn - nh * jnp.sum(dn * nh, axis=-1, keepdims=True)))
        dyy = jnp.concatenate(dys + [dve], axis=1)
        dcv = dyy * dy_dcv
        dx = (cw_ref[3:4, :] * dcv + cw_ref[2:3, :] * _shift_up(dcv, 1) + cw_ref[1:2, :] * _shift_up(dcv, 2)
              + cw_ref[0:1, :] * _shift_up(dcv, 3))
        dp_ref[:, 0:2048] = _b(dx[8:tT + 8])
        dp_ref[:, 2176:] = jnp.zeros((tT, REG_C - 2176), bf16)
        inner = (rows >= 8) & (rows < tT + 8)
        dcv_in = jnp.where(inner, dcv, 0.0)
        for j in range(4):
            dcw_ref[j:j + 1, :] += jnp.sum(dcv_in * xs[j], axis=0, keepdims=True)
        ps = p_ref[:, 2048:2176].astype(f32)
        lane = lax.broadcasted_iota(jnp.int32, ps.shape, 1)
        dsm = dsm_ref[:, 0:128]
        for hh in range(1, N_GDN):
            dsm = dsm + dsm_ref[:, hh * 128:(hh + 1) * 128]
        beta = _sigmoid(ps)
        xa = ps + dt_ref[...]
        nea = -jnp.exp(al_ref[...])
        dpa = dsm * nea * _sigmoid(xa)
        dp_ref[:, 2048:2176] = _b(jnp.where(lane < 8, dsm * beta * (1.0 - beta), jnp.where(lane < 16, dpa, 0.0)))
        amask = (lane >= 8) & (lane < 16)
        dsml_ref[0:1, :] += jnp.sum(jnp.where(amask, dsm * nea * _softplus(xa), 0.0), axis=0, keepdims=True)
        dsml_ref[1:2, :] += jnp.sum(jnp.where(amask, dpa, 0.0), axis=0, keepdims=True)

    cur1024 = pl.BlockSpec((tT, 1024), lambda i: (i, 0))
    return pl.pallas_call(
        body, name="mixc_pre_bwd", grid=(nt,),
        in_specs=[pl.BlockSpec((tT, REG_C), lambda i: (i, OFF_C // REG_C)), prev_spec, next_spec,
                  pl.BlockSpec((4, 2048), lambda i: (0, 0)),
                  pl.BlockSpec((1, 128), lambda i: (0, 0)), pl.BlockSpec((1, 128), lambda i: (0, 0)),
                  cur1024, n1024, cur1024, n1024, cur1024, n1024, cur1024, pl.BlockSpec(memory_space=pl.ANY)],
        out_specs=[pl.BlockSpec((tT, REG_C), lambda i: (i, OFF_C // REG_C)),
                   pl.BlockSpec((8, 2048), lambda i: (0, 0)), pl.BlockSpec((8, 128), lambda i: (0, 0))],
        out_shape=[jax.ShapeDtypeStruct((T, NP), bf16),
                   jax.ShapeDtypeStruct((8, 2048), f32), jax.ShapeDtypeStruct((8, 128), f32)],
        input_output_aliases={13: 0},
        compiler_params=_cp(("arbitrary",)),
    )(p, p, p, cw, alog_l, dtb_l, dq8, dq8, dk8, dk8, dv, dv, dsm8, dp)


def _tri_inverse(m):
    r = lax.broadcasted_iota(jnp.int32, (L, L), 0)
    c = lax.broadcasted_iota(jnp.int32, (L, L), 1)
    eye = (r == c).astype(f32)[None]
    same = lambda w: ((r // w) == (c // w))[None]
    md = jnp.where(same(8), m, 0.0)
    m2 = _bdot_split(md, md)
    m4 = _bdot_split(m2, m2)
    t = _bdot_split(_bdot_split(eye - md, eye + m2), eye + m4)
    for w in (16, 32, 64):
        mo = jnp.where(same(w) & jnp.logical_not(same(w // 2)), m, 0.0)
        t = t - _bdot_split(_bdot_split(t, mo), t)
    return t


def _col_to_row(col, eye):
    return jnp.sum(eye * col, axis=1, keepdims=True)


def _row_to_col(row, eye):
    return jnp.sum(eye * row, axis=2, keepdims=True)


def _gdn_chunk_terms(q, k, v, beta, g, t_inv=None):
    r = lax.broadcasted_iota(jnp.int32, (L, L), 0)
    c = lax.broadcasted_iota(jnp.int32, (L, L), 1)
    eye = (r == c).astype(f32)[None]
    causal, strict = (c <= r)[None], (c < r)[None]
    diff = g - _col_to_row(g, eye)
    dec = jnp.exp(jnp.where(causal, diff, 0.0))
    dc = jnp.where(causal, dec, 0.0)
    ds = jnp.where(strict, dec, 0.0)
    eg = jnp.exp(g)
    gl = g[:, L - 1:L, :]
    egl = jnp.exp(gl - g)
    kb = k * beta
    kk = _bdot_nt(_b(k), _b(kb))
    qk = _bdot_nt(_b(q), _b(kb))
    m = kk * ds
    aqk = qk * dc
    if t_inv is None:
        t_inv = _tri_inverse(m)
    tb = _b(t_inv)
    keg = k * eg
    u = _bdot(tb, _b(v))
    w = _bdot(tb, _b(keg))
    ks = kb * egl
    ksw = _bdot_tn(_b(ks), _b(w))
    return dict(eye=eye, causal=causal, strict=strict, dc=dc, ds=ds, eg=eg, gl=gl, egl=egl, kb=kb, kk=kk, qk=qk,
                m=m, aqk=aqk, t=t_inv, u=u, w=w, qi=q * eg, ks=ks, keg=keg, ksw=ksw)


def gdn_fwd(qn, kn, vv, sm):
    T = qn.shape[0]
    tB = min(2 * GDN_BLOCK, T)
    nc = tB // L
    N = T // L

    def body(q_ref, k_ref, v_ref, sm_ref, o_ref, st_ref, ti_ref, s_scr):
        j = pl.program_id(0)

        @pl.when(pl.program_id(1) == 0)
        def _():
            s_scr[...] = jnp.zeros_like(s_scr)
        smv = sm_ref[...]
        lane = lax.broadcasted_iota(jnp.int32, smv.shape, 1)
        q = q_ref[...].reshape(nc, L, HK)
        k = k_ref[...].reshape(nc, L, HK)
        tms, ksus, kswbs, egls = [], [], [], []
        for a in range(2):
            h = 2 * j + a
            beta = jnp.sum(jnp.where(lane == h, smv, 0.0), axis=1, keepdims=True).reshape(nc, L, 1)
            g = jnp.sum(jnp.where(lane == 8 + h, smv, 0.0), axis=1, keepdims=True).reshape(nc, L, 1)
            v = v_ref[:, a * HK:(a + 1) * HK].reshape(nc, L, HK)
            tm = _gdn_chunk_terms(q, k, v, beta, g)
            ti_ref[a] = tm["t"]
            tms.append(tm)
            ksus.append(_bdot_tn(_b(tm["ks"]), _b(tm["u"])))
            kswbs.append(_b(tm["ksw"]))
            egls.append(jnp.exp(tm["gl"]))
        s = [s_scr[0], s_scr[1]]
        states = [[None] * nc, [None] * nc]
        for ci in range(nc):
            for a in range(2):
                states[a][ci] = s[a]
                s[a] = egls[a][ci] * s[a] + (ksus[a][ci] - _dot(kswbs[a][ci], _b(s[a])))
        for a in range(2):
            s_scr[a] = s[a]
            sall = jnp.stack(states[a], axis=0)
            st_ref[a] = sall
            sb = _b(sall)
            tm = tms[a]
            e = tm["u"] - _bdot(_b(tm["w"]), sb)
            o = _bdot(_b(tm["qi"]), sb) + _bdot(_b(tm["aqk"]), _b(e))
            o_ref[:, a * HK:(a + 1) * HK] = o.reshape(tB, HK)

    return pl.pallas_call(
        body, name="gdn_fwd", grid=(N_GDN // 2, T // tB),
        in_specs=[pl.BlockSpec((tB, HK), lambda j, n: (n, j)), pl.BlockSpec((tB, HK), lambda j, n: (n, j)),
                  pl.BlockSpec((tB, 2 * HK), lambda j, n: (n, j)), pl.BlockSpec((tB, 128), lambda j, n: (n, 0))],
        out_specs=[pl.BlockSpec((tB, 2 * HK), lambda j, n: (n, j)),
                   pl.BlockSpec((2, nc, HK, HK), lambda j, n: (j, n, 0, 0)),
                   pl.BlockSpec((2, nc, L, L), lambda j, n: (j, n, 0, 0))],
        out_shape=[jax.ShapeDtypeStruct((T, N_GDN * HK), f32), jax.ShapeDtypeStruct((N_GDN, N, HK, HK), f32),
                   jax.ShapeDtypeStruct((N_GDN, N, L, L), f32)],
        scratch_shapes=[pltpu.VMEM((2, HK, HK), f32)],
        compiler_params=_cp(("parallel", "arbitrary")),
    )(qn, kn, vv, sm)


def gdn_bwd(qn, kn, vv, sm, st, ti, do):
    T = qn.shape[0]
    tB = min(GDN_BLOCK, T)
    nc = tB // L
    nb = T // tB

    def body(q_ref, k_ref, v_ref, sm_ref, st_ref, ti_ref, do_ref, dq_ref, dk_ref, dv_ref, dsm_ref, ds_scr):
        j = pl.program_id(0)

        @pl.when(pl.program_id(1) == 0)
        def _():
            ds_scr[...] = jnp.zeros_like(ds_scr)
        smv = sm_ref[...]
        lane = lax.broadcasted_iota(jnp.int32, smv.shape, 1)
        q = q_ref[...].reshape(nc, L, HK)
        k = k_ref[...].reshape(nc, L, HK)
        kbf, qbf = _b(k), _b(q)
        heads = []
        for a in range(2):
            h = 2 * j + a
            beta = jnp.sum(jnp.where(lane == h, smv, 0.0), axis=1, keepdims=True).reshape(nc, L, 1)
            g = jnp.sum(jnp.where(lane == 8 + h, smv, 0.0), axis=1, keepdims=True).reshape(nc, L, 1)
            v = v_ref[:, a * HK:(a + 1) * HK].reshape(nc, L, HK)
            do = do_ref[:, a * HK:(a + 1) * HK].reshape(nc, L, HK)
            s = st_ref[a]
            tm = _gdn_chunk_terms(q, k, v, beta, g, t_inv=ti_ref[a])
            sb, dob = _b(s), _b(do)
            e = tm["u"] - _bdot(_b(tm["w"]), sb)
            de0 = _bdot_tn(_b(tm["aqk"]), dob)
            ds0 = _bdot_tn(_b(tm["qi"]), dob) - _bdot_tn(_b(tm["w"]), _b(de0))
            heads.append(dict(h=h, beta=beta, tm=tm, s=s, sb=sb, dob=dob, e=e, de0=de0, ds0=ds0, kswb=_b(tm["ksw"]),
                              egl_last=jnp.exp(tm["gl"])))
        dsn = [ds_scr[0], ds_scr[1]]
        dsns = [[None] * nc, [None] * nc]
        for ci in reversed(range(nc)):
            for a, hd in enumerate(heads):
                dsns[a][ci] = dsn[a]
                dsn[a] = hd["ds0"][ci] + (hd["egl_last"][ci] * dsn[a] - _dot_tn(hd["kswb"][ci], _b(dsn[a])))
        rowi = lax.broadcasted_iota(jnp.int32, (nc, L, 1), 1)
        rin = lax.broadcasted_iota(jnp.int32, (tB, 128), 0) % L
        for a, hd in enumerate(heads):
            ds_scr[a] = dsn[a]
            tm, s, sb, dob, e, beta, h = hd["tm"], hd["s"], hd["sb"], hd["dob"], hd["e"], hd["beta"], hd["h"]
            eye, dc, ds_, eg, egl = tm["eye"], tm["dc"], tm["ds"], tm["eg"], tm["egl"]
            kb, u, w, qi, ks = tm["kb"], tm["u"], tm["w"], tm["qi"], tm["ks"]
            eb = _b(e)
            dsp = jnp.stack(dsns[a], axis=0)
            dspb = _b(dsp)
            de = hd["de0"] + _bdot(_b(ks), dspb)
            deb = _b(de)
            dks = _bdot_nt(eb, dspb)
            dqi = _bdot_nt(dob, sb)
            daqk = jnp.where(tm["causal"], _bdot_nt(dob, eb), 0.0)
            dw = -_bdot_nt(deb, sb)
            tb = _b(tm["t"])
            dvv = _bdot_tn(tb, deb)
            dkg = _bdot_tn(tb, _b(dw))
            dm = -jnp.where(tm["strict"], _bdot_nt(_b(dvv), _b(u)) + _bdot_nt(_b(dkg), _b(w)), 0.0)
            x = _b(dm * ds_)
            y = _b(daqk * dc)
            kbb = _b(kb)
            dk = _bdot(x, kbb) + dkg * eg
            dkb = _bdot_tn(x, kbf) + _bdot_tn(y, qbf) + dks * egl
            dq = _bdot(y, kbb) + dqi * eg
            dk = dk + dkb * beta
            dbeta = jnp.sum(dkb * k, axis=-1, keepdims=True)
            z = dm * tm["m"] + daqk * tm["aqk"]
            dg = (jnp.sum(dqi * qi - dks * ks + dkg * tm["keg"], axis=-1, keepdims=True)
                  + jnp.sum(z, axis=-1, keepdims=True) - _row_to_col(jnp.sum(z, axis=1, keepdims=True), eye))
            dgl = (hd["egl_last"] * jnp.sum(jnp.sum(s * dsp, axis=2, keepdims=True), axis=1, keepdims=True)
                   + jnp.sum(jnp.sum(dks * ks, axis=2, keepdims=True), axis=1, keepdims=True))
            dg = dg + jnp.where(rowi == L - 1, dgl, 0.0)
            dla = _chunk_rev_cumsum(jnp.broadcast_to(dg.reshape(tB, 1), (tB, 128)), rin)
            sl = slice(a * HK, (a + 1) * HK)
            dq_ref[:, sl] = dq.reshape(tB, HK)
            dk_ref[:, sl] = dk.reshape(tB, HK)
            dv_ref[:, sl] = dvv.reshape(tB, HK)
            dsm_ref[:, sl] = jnp.where(lane == h, dbeta.reshape(tB, 1), jnp.where(lane == 8 + h, dla, 0.0))

    rev = lambda n: nb - 1 - n
    pair = pl.BlockSpec((tB, 2 * HK), lambda j, n: (rev(n), j))
    return pl.pallas_call(
        body, name="gdn_bwd", grid=(N_GDN // 2, nb),
        in_specs=[pl.BlockSpec((tB, HK), lambda j, n: (rev(n), j)), pl.BlockSpec((tB, HK), lambda j, n: (rev(n), j)),
                  pair, pl.BlockSpec((tB, 128), lambda j, n: (rev(n), 0)),
                  pl.BlockSpec((2, nc, HK, HK), lambda j, n: (j, rev(n), 0, 0)),
                  pl.BlockSpec((2, nc, L, L), lambda j, n: (j, rev(n), 0, 0)), pair],
        out_specs=[pair] * 4,
        out_shape=[jax.ShapeDtypeStruct((T, N_GDN * HK), f32)] * 4,
        scratch_shapes=[pltpu.VMEM((2, HK, HK), f32)],
        compiler_params=_cp(("parallel", "arbitrary")),
    )(qn, kn, vv, sm, st, ti, do)


def _hgrn_prep(bq, bf_, bi, lb):
    tB = bq.shape[0]
    sq, dsq = _silu2(bq)
    sg = _sigmoid(bf_)
    f = lb + (1.0 - lb) * sg
    logf = jnp.log(jnp.maximum(f, MIN_F))
    rin = lax.broadcasted_iota(jnp.int32, (tB, HK), 0) % L
    g = _chunk_cumsum(logf, rin)
    return sq * QK_SCALE, sg, f, 1.0 - f, bi, g, rin, dsq * QK_SCALE


def _hgrn_intra(q, kk, v, g, do=None):
    n = q.shape[0]
    nsub = L // SUB
    bwd = do is not None
    o_rows = [None] * nsub
    if bwd:
        dq_rows = [None] * nsub
        dkk_acc = jnp.zeros_like(kk)
        dv_acc = jnp.zeros_like(v)
    for i in range(1, nsub):
        lo, hi, w = i * SUB, (i + 1) * SUB, i * SUB
        ref = g[:, lo - 1:lo, :]
        eq = jnp.exp(g[:, lo:hi, :] - ref)
        ek = jnp.exp(ref - g[:, :w, :])
        qs = _b(q[:, lo:hi, :] * eq)
        ks = _b(kk[:, :w, :] * ek)
        p = _bdot_nt(qs, ks)
        o_rows[i] = _bdot(_b(p), _b(v[:, :w, :]))
        if bwd:
            dob = _b(do[:, lo:hi, :])
            dp = _b(_bdot_nt(dob, _b(v[:, :w, :])))
            dq_rows[i] = _bdot(dp, ks) * eq
            pad = jnp.zeros((n, L - w, HK), f32)
            dkk_acc = dkk_acc + jnp.concatenate([_bdot_tn(dp, qs) * ek, pad], axis=1)
            dv_acc = dv_acc + jnp.concatenate([_bdot_tn(_b(p), dob), pad], axis=1)
    m = n * nsub
    q4, k4, v4, g4 = (a.reshape(m, SUB, HK) for a in (q, kk, v, g))
    r = lax.broadcasted_iota(jnp.int32, (m, SUB, HK), 1)
    od = jnp.zeros((m, SUB, HK), f32)
    if bwd:
        do4 = do.reshape(m, SUB, HK)
        dqd = jnp.zeros((m, SUB, HK), f32)
        dkd = jnp.zeros((m, SUB, HK), f32)
        dvd = jnp.zeros((m, SUB, HK), f32)
    for j in range(SUB):
        gj, kj, vj = g4[:, j:j + 1, :], k4[:, j:j + 1, :], v4[:, j:j + 1, :]
        ok = r >= j
        e = jnp.where(ok, jnp.exp(g4 - gj), 0.0)
        xq = q4 * e
        pj = jnp.sum(xq * kj, axis=-1, keepdims=True)
        od = od + pj * vj
        if bwd:
            dpj = jnp.sum(do4 * vj, axis=-1, keepdims=True)
            dqd = dqd + dpj * kj * e
            dkd = dkd + jnp.where(r == j, jnp.sum(dpj * xq, axis=1, keepdims=True), 0.0)
            dvd = dvd + jnp.where(r == j, jnp.sum(pj * do4, axis=1, keepdims=True), 0.0)
    od = od.reshape(n, L, HK)
    o = jnp.concatenate([od[:, :SUB, :]] + [od[:, i * SUB:(i + 1) * SUB, :] + o_rows[i] for i in range(1, nsub)], axis=1)
    if not bwd:
        return o
    dqd = dqd.reshape(n, L, HK)
    dq = jnp.concatenate([dqd[:, :SUB, :]] + [dqd[:, i * SUB:(i + 1) * SUB, :] + dq_rows[i] for i in range(1, nsub)], axis=1)
    return o, dq, dkk_acc + dkd.reshape(n, L, HK), dv_acc + dvd.reshape(n, L, HK)


def hgrn_fwd(p, lbs):
    T = p.shape[0]
    tB = min(1024, T)
    nc = tB // L
    N = T // L

    def body(b_ref, lb_ref, o_ref, st_ref, s_scr):
        @pl.when(pl.program_id(1) == 0)
        def _():
            s_scr[...] = jnp.zeros_like(s_scr)
        bv = b_ref[...].astype(f32)
        q, sg, f, kk, v, g, rin, _ = _hgrn_prep(bv[:, 0:HK], bv[:, HK:2 * HK], bv[:, 2 * HK:3 * HK], lb_ref[...])
        q3, k3, v3, g3 = (a.reshape(nc, L, HK) for a in (q, kk, v, g))
        o = _hgrn_intra(q3, k3, v3, g3)
        gl = g3[:, L - 1:L, :]
        qt = _b(q3 * jnp.exp(g3))
        kt = _b(k3 * jnp.exp(gl - g3))
        vb = _b(v3)
        st = s_scr[...]
        for c in range(nc):
            st_ref[c] = st
            o_ref[c * L:(c + 1) * L, :] = o[c] + _dot_nt(qt[c], _b(st))
            st = st * jnp.exp(gl[c]) + _dot_tn(vb[c], kt[c])
        s_scr[...] = st

    return pl.pallas_call(
        body, name="hgrn_fwd", grid=(N_HGRN, T // tB),
        in_specs=[pl.BlockSpec((tB, REG_BH), lambda h, n: (n, OFF_B // REG_BH + h)),
                  pl.BlockSpec((1, HK), lambda h, n: (0, h))],
        out_specs=[pl.BlockSpec((tB, HK), lambda h, n: (n, h)),
                   pl.BlockSpec((None, nc, HK, HK), lambda h, n: (h, n, 0, 0))],
        out_shape=[jax.ShapeDtypeStruct((T, N_HGRN * HK), f32), jax.ShapeDtypeStruct((N_HGRN, N, HK, HK), f32)],
        scratch_shapes=[pltpu.VMEM((HK, HK), f32)],
        compiler_params=_cp(("parallel", "arbitrary")),
    )(p, lbs)


def hgrn_bwd(p, lbs, st, do, dp):
    T = p.shape[0]
    tB = min(256, T)
    nc = tB // L
    nb = T // tB

    def body(b_ref, lb_ref, st_ref, do_ref, dp_in, dp_ref, dlb_ref, ds_scr):
        @pl.when(pl.program_id(1) == 0)
        def _():
            ds_scr[...] = jnp.zeros_like(ds_scr)
            dlb_ref[...] = jnp.zeros_like(dlb_ref)
        lb = lb_ref[...]
        bv = b_ref[...].astype(f32)
        q, sg, f, kk, v, g, rin, dq_dbq = _hgrn_prep(bv[:, 0:HK], bv[:, HK:2 * HK], bv[:, 2 * HK:3 * HK], lb)
        q3, k3, v3, g3 = (a.reshape(nc, L, HK) for a in (q, kk, v, g))
        do3 = do_ref[...].reshape(nc, L, HK)
        dob = _b(do3)
        gl = g3[:, L - 1:L, :]
        egl = jnp.exp(gl)
        eg, egr = jnp.exp(g3), jnp.exp(gl - g3)
        qt, kt = q3 * eg, k3 * egr
        s = st_ref[...]
        ds0 = _bdot_tn(dob, _b(qt))
        dsn = ds_scr[...]
        dsns = [None] * nc
        for c in reversed(range(nc)):
            dsns[c] = dsn
            dsn = ds0[c] + dsn * egl[c]
        ds_scr[...] = dsn
        dsp = jnp.stack(dsns, axis=0)
        dspb = _b(dsp)
        dqt = _bdot(dob, _b(s))
        dkt = _bdot(_b(v3), dspb)
        dv_state = _bdot_nt(_b(kt), dspb)
        dgl = egl * jnp.sum(s * dsp, axis=1, keepdims=True) + jnp.sum(dkt * kt, axis=1, keepdims=True)
        _, dq_i, dkk_i, dv_i = _hgrn_intra(q3, k3, v3, g3, do=do3)
        dq = dq_i + dqt * eg
        dkk = dkk_i + dkt * egr
        dv = dv_i + dv_state
        rowi = lax.broadcasted_iota(jnp.int32, (nc, L, HK), 1)
        dg = q3 * dq - k3 * dkk + jnp.where(rowi == L - 1, dgl, 0.0)
        dlogf = _chunk_rev_cumsum(dg.reshape(tB, HK), rin)
        dkk2 = dkk.reshape(tB, HK)
        df = jnp.where(f > MIN_F, dlogf / f, 0.0) - dkk2
        dlb_ref[...] += jnp.sum(df * (1.0 - sg), axis=0, keepdims=True)
        dp_ref[:, 0:HK] = _b(dq.reshape(tB, HK) * dq_dbq)
        dp_ref[:, HK:2 * HK] = _b(df * (1.0 - lb) * sg * (1.0 - sg))
        dp_ref[:, 2 * HK:3 * HK] = _b(dv.reshape(tB, HK))

    rev = lambda n: nb - 1 - n
    return pl.pallas_call(
        body, name="hgrn_bwd", grid=(N_HGRN, nb),
        in_specs=[pl.BlockSpec((tB, REG_BH), lambda h, n: (rev(n), OFF_B // REG_BH + h)),
                  pl.BlockSpec((1, HK), lambda h, n: (0, h)),
                  pl.BlockSpec((None, nc, HK, HK), lambda h, n: (h, rev(n), 0, 0)),
                  pl.BlockSpec((tB, HK), lambda h, n: (rev(n), h)), pl.BlockSpec(memory_space=pl.ANY)],
        out_specs=[pl.BlockSpec((tB, REG_BH), lambda h, n: (rev(n), OFF_B // REG_BH + h)),
                   pl.BlockSpec((1, HK), lambda h, n: (0, h))],
        out_shape=[jax.ShapeDtypeStruct((T, NP), bf16), jax.ShapeDtypeStruct((1, N_HGRN * HK), f32)],
        input_output_aliases={4: 0},
        scratch_shapes=[pltpu.VMEM((HK, HK), f32)],
        compiler_params=_cp(("parallel", "arbitrary")),
    )(p, lbs, st, do, dp)


def _headnorm_fwd(o, z, w, nheads):
    outs, parts = [], []
    for hh in range(nheads):
        sl = slice(hh * HK, (hh + 1) * HK)
        oh = o[:, sl]
        r = lax.rsqrt(jnp.mean(oh * oh, axis=-1, keepdims=True) + NORM_EPS)
        on = oh * r
        sz, dsz = _silu2(z[:, sl])
        outs.append(on * w * sz)
        parts.append((r, on, sz, dsz))
    return jnp.concatenate(outs, axis=1), parts


def _headnorm_bwd(parts, w, dy):
    dos, dzs = [], []
    dw = jnp.zeros((1, HK), f32)
    for hh, (r, on, sz, dsz) in enumerate(parts):
        dyh = dy[:, hh * HK:(hh + 1) * HK]
        dn = dyh * sz * w
        dos.append(r * (dn - on * jnp.mean(dn * on, axis=-1, keepdims=True)))
        dzs.append(dyh * on * w * dsz)
        dw = dw + jnp.sum(dyh * sz * on, axis=0, keepdims=True)
    return jnp.concatenate(dos, axis=1), jnp.concatenate(dzs, axis=1), dw


def _merge_specs(tT, l):
    row = lambda w, cb=0: pl.BlockSpec((tT, w), lambda i, cb=cb: (i, cb))
    full = lambda r, c: pl.BlockSpec((r, c), lambda i: (0, 0))
    layer = lambda r, c: pl.BlockSpec((None, r, c), lambda i: (l, 0, 0))
    return row, full, layer


def merge_fwd(x, p, ya, ob, oc, hw, gw, bg, woa, wob, woc, wo, l):
    T = x.shape[0]
    tT = min(512, T)
    row, full, layer = _merge_specs(tT, l)

    def body(x_ref, pm_ref, ya_ref, ob_ref, oc_ref, hw_ref, gw_ref, bg_ref,
             woa_ref, wob_ref, woc_ref, wo_ref, out_ref):
        yb = _b(_headnorm_fwd(ob_ref[...], pm_ref[:, M_BZ:M_G].astype(f32), hw_ref[...], N_HGRN)[0])
        yc = _b(_headnorm_fwd(oc_ref[...], pm_ref[:, M_CZ:REG_M].astype(f32), gw_ref[...], N_GDN)[0])
        gates = _sigmoid(pm_ref[:, M_G:M_CZ].astype(f32) + bg_ref[...])
        merged = (gates[:, 0:D] * _dot(ya_ref[...], woa_ref[...]) + gates[:, D:2 * D] * _dot(yb, wob_ref[...])
                  + gates[:, 2 * D:3 * D] * _dot(yc, woc_ref[...]))
        out_ref[...] = x_ref[...] + _dot(_b(merged), wo_ref[...])

    return pl.pallas_call(
        body, name="merge_fwd", grid=(T // tT,),
        in_specs=[row(D), row(REG_M, OFF_M // REG_M),
                  row(512), row(512), row(1024), full(1, HK), full(1, HK), full(1, 3 * D),
                  layer(512, D), layer(512, D), layer(D, D), layer(D, D)],
        out_specs=row(D),
        out_shape=jax.ShapeDtypeStruct((T, D), f32),
        compiler_params=_cp(("parallel",)),
    )(x, p, ya, ob, oc, hw, gw, bg, woa, wob, woc, wo)


def merge_bwd(dxo, p, ya, ob, oc, hw, gw, bg, woa, wob, woc, wo, l):
    T = dxo.shape[0]
    tT = min(256, T)
    row, full, layer = _merge_specs(tT, l)

    def body(dx_ref, pm_ref, ya_ref, ob_ref, oc_ref, hw_ref, gw_ref, bg_ref,
             woa_ref, wob_ref, woc_ref, wo_ref,
             dya_ref, dob_ref, doc_ref, dp_ref, mg_ref, dy3_ref, yb_ref, yc_ref,
             dbg_ref, dhw_ref, dgw_ref):
        @pl.when(pl.program_id(0) == 0)
        def _():
            dbg_ref[...] = jnp.zeros_like(dbg_ref)
            dhw_ref[...] = jnp.zeros_like(dhw_ref)
            dgw_ref[...] = jnp.zeros_like(dgw_ref)
        ob, oc, bz, cz = ob_ref[...], oc_ref[...], pm_ref[:, M_BZ:M_G].astype(f32), pm_ref[:, M_CZ:REG_M].astype(f32)
        hw_, gw_ = hw_ref[...], gw_ref[...]
        yb, parts_b = _headnorm_fwd(ob, bz, hw_, N_HGRN)
        yc, parts_c = _headnorm_fwd(oc, cz, gw_, N_GDN)
        yb, yc = _b(yb), _b(yc)
        yb_ref[...] = yb
        yc_ref[...] = yc
        gates = _sigmoid(pm_ref[:, M_G:M_CZ].astype(f32) + bg_ref[...])
        ys = (_dot(ya_ref[...], woa_ref[...]), _dot(yb, wob_ref[...]), _dot(yc, woc_ref[...]))
        dmerged = _dot_nt(_b(dx_ref[...]), wo_ref[...])
        merged = jnp.zeros_like(dmerged)
        dys = []
        for i in range(3):
            gi = gates[:, i * D:(i + 1) * D]
            merged = merged + gi * ys[i]
            dyi = _b(dmerged * gi)
            dys.append(dyi)
            dy3_ref[:, i * D:(i + 1) * D] = dyi
            dgp = dmerged * ys[i] * gi * (1.0 - gi)
            dp_ref[:, M_G + i * D:M_G + (i + 1) * D] = _b(dgp)
            dbg_ref[:, i * D:(i + 1) * D] += jnp.sum(dgp, axis=0, keepdims=True)
        mg_ref[...] = _b(merged)
        dya_ref[...] = _dot_nt(dys[0], woa_ref[...])
        dob, dbz, dhw = _headnorm_bwd(parts_b, hw_, _dot_nt(dys[1], wob_ref[...]))
        doc, dcz, dgw = _headnorm_bwd(parts_c, gw_, _dot_nt(dys[2], woc_ref[...]))
        dob_ref[...] = dob
        doc_ref[...] = doc
        dp_ref[:, M_BZ:M_G] = _b(dbz)
        dp_ref[:, M_CZ:REG_M] = _b(dcz)
        dhw_ref[...] += dhw
        dgw_ref[...] += dgw

    sd = jax.ShapeDtypeStruct
    return pl.pallas_call(
        body, name="merge_bwd", grid=(T // tT,),
        in_specs=[row(D), row(REG_M, OFF_M // REG_M),
                  row(512), row(512), row(1024), full(1, HK), full(1, HK), full(1, 3 * D),
                  layer(512, D), layer(512, D), layer(D, D), layer(D, D)],
        out_specs=[row(512), row(512), row(1024), row(REG_M, OFF_M // REG_M), row(D), row(3 * D), row(512),
                   row(1024), full(1, 3 * D), full(1, HK), full(1, HK)],
        out_shape=[sd((T, 512), f32), sd((T, 512), f32), sd((T, 1024), f32), sd((T, NP), bf16),
                   sd((T, D), bf16), sd((T, 3 * D), bf16), sd((T, 512), bf16),
                   sd((T, 1024), bf16), sd((1, 3 * D), f32), sd((1, HK), f32), sd((1, HK), f32)],
        compiler_params=_cp(("arbitrary",)),
    )(dxo, p, ya, ob, oc, hw, gw, bg, woa, wob, woc, wo)


def layer_fwd(x, w):
    l = w["l"]
    p, h = inproj_fwd(x, w["norm_w"], w["w_in"])
    ya = mixa_fwd(p, w["conv_a"])
    qn, kn, vv, sm = mixc_pre_fwd(p, w["conv_c"], w["alog_l"], w["dtb_l"])
    oc, st_c, ti = gdn_fwd(qn, kn, vv, sm)
    ob, st_b = hgrn_fwd(p, w["lbs"])
    if "late" in w:
        w.update(w.pop("late")((ya, oc, ob)))
    xo = merge_fwd(x, p, ya, ob, oc, w["hgrn_norm_w"], w["gdn_norm_w"], w["b_gate"],
                   w["w_out_a"], w["w_out_b"], w["w_out_c"], w["w_o"], l)
    saved = dict(x=x, p=p, h=h, ya=ya, qn=qn, kn=kn, vv=vv, sm=sm, oc=oc, st_c=st_c, ti=ti, ob=ob, st_b=st_b)
    return xo, saved


OUT_MATS = (("w_out_a", "cols"), ("w_out_b", "cols"), ("w_out_c", "rows"), ("w_o", "rows"))


def layer_bwd(dxo, w, s, chip):
    p, l = s["p"], w["l"]
    (dya, dob, doc, dp, merged, dy3, yb, yc, dbg, dhw, dgw) = merge_bwd(
        dxo, p, s["ya"], s["ob"], s["oc"], w["hgrn_norm_w"], w["gdn_norm_w"], w["b_gate"],
        w["w_out_a"], w["w_out_b"], w["w_out_c"], w["w_o"], l)
    full = {"w_o": matmul_tn(merged, dxo, "dw_o", with_bf16=True),
            "w_out_a": matmul_tn(s["ya"], dy3, "dw_out_a", n=D, b_col0=0, with_bf16=True),
            "w_out_b": matmul_tn(yb, dy3, "dw_out_b", n=D, b_col0=D, with_bf16=True),
            "w_out_c": matmul_tn(yc, dy3, "dw_out_c", n=D, b_col0=2 * D, with_bf16=True)}
    out_kinds = [k for _, k in OUT_MATS]
    sent_out, token = exchange_start([full[n][1] for n, _ in OUT_MATS], out_kinds, f"grads_out_start{l}")
    dp, dlbs = hgrn_bwd(p, w["lbs"] + token[0:1, 0:1], s["st_b"], dob, dp)
    dq8, dk8, dvv, dsm8 = gdn_bwd(s["qn"], s["kn"], s["vv"], s["sm"], s["st_c"], s["ti"], doc)
    dp, dcc, dsmall = mixc_pre_bwd(p, w["conv_c"], w["alog_l"], w["dtb_l"], dq8, dk8, dvv, dsm8, dp)
    dp, dca = mixa_bwd(p, w["conv_a"], dya, dp)
    gf_win, gb_win = win_from_padded(matmul_tn(s["h"], dp, "dw_in"), chip)
    sent_in, token = exchange_start([gb_win], ["slot"], f"grads_in_start{l}")
    dx, dnw = inproj_bwd(dp, w["w_in"], s["x"], w["norm_w"] + token[0:1, 0:1], dxo)
    recv_out = exchange_wait(sent_out, out_kinds, dx, f"grads_out_wait{l}")
    recv_in = exchange_wait(sent_in, ["slot"], dx, f"grads_in_wait{l}")
    half = {"w_in": partial_sum(gf_win, "own", recv_in[0], chip, "psum_w_in", transposed=True)}
    for (n, kind), r in zip(OUT_MATS, recv_out):
        half[n] = partial_sum(full[n][0], kind, r, chip, "psum_" + n)
    small = dict(norm_w=dnw, b_gate=dbg, hgrn_norm_w=dhw, gdn_norm_w=dgw, lbs=dlbs, conv_a=dca[0:3], conv_c=dcc[0:4],
                 a_log=dsmall[0:1, 8:16], dt_bias=dsmall[1:2, 8:16])
    return dx, small, half


def lbs_fwd(lb):
    def body(lb_ref, o_ref):
        l0, l1 = lb_ref[0:1, :], lb_ref[1:2, :]
        mx = jnp.maximum(l0, l1)
        e0, e1 = jnp.exp(l0 - mx), jnp.exp(l1 - mx)
        o_ref[0:1, :] = jnp.zeros_like(l0)
        o_ref[1:2, :] = e1 / (e0 + e1)
    return pl.pallas_call(body, name="lbs_fwd", out_shape=jax.ShapeDtypeStruct(lb.shape, f32))(lb)


def _adam_math(w, g, m, v):
    mn = ADAM_B1 * m + (1.0 - ADAM_B1) * g
    vn = ADAM_B2 * v + (1.0 - ADAM_B2) * (g * g)
    mh = mn / (1.0 - ADAM_B1 ** ADAM_STEP)
    vh = vn / (1.0 - ADAM_B2 ** ADAM_STEP)
    return -ADAM_LR * (mh / (jnp.sqrt(vh) + ADAM_EPS) + ADAM_WD * w), mn, vn


def adam(w, g, m, v, name):
    R, C = w.shape
    tr = 256 if R % 256 == 0 else R

    def body(w_ref, g_ref, m_ref, v_ref, d_ref, mo_ref, vo_ref):
        d, mn, vn = _adam_math(w_ref[...], g_ref[...], m_ref[...], v_ref[...])
        d_ref[...] = d
        mo_ref[...] = mn
        vo_ref[...] = vn

    spec = pl.BlockSpec((tr, C), lambda i: (i, 0))
    return pl.pallas_call(
        body, name=name, grid=(R // tr,), in_specs=[spec] * 4, out_specs=[spec] * 3,
        out_shape=[jax.ShapeDtypeStruct((R, C), f32)] * 3, compiler_params=_cp(("parallel",)),
    )(w, g, m, v)


def adam_pair(h, hs, w, m, v, name):
    _, R, C = w.shape
    cp = h[0].shape[1]
    tr = 128 if R % 128 == 0 else R
    nt = R // tr

    def body(h0_ref, h1_ref, s0_ref, s1_ref, w_ref, m_ref, v_ref, g_ref, d_ref, mo_ref, vo_ref):
        def update(h_ref, s_ref):
            g = (h_ref[...] + s_ref[...])[:, :C]
            d, mn, vn = _adam_math(w_ref[...], g, m_ref[...], v_ref[...])
            g_ref[...] = g
            d_ref[...] = d
            mo_ref[...] = mn
            vo_ref[...] = vn

        @pl.when(pl.program_id(0) == 0)
        def _():
            update(h0_ref, s0_ref)

        @pl.when(pl.program_id(0) == 1)
        def _():
            update(h1_ref, s1_ref)

    h0spec = pl.BlockSpec((tr, cp), lambda l, i: (jnp.where(l == 0, i, nt - 1), 0))
    h1spec = pl.BlockSpec((tr, cp), lambda l, i: (jnp.where(l == 1, i, 0), 0))
    spec = pl.BlockSpec((None, tr, C), lambda l, i: (l, i, 0))
    return pl.pallas_call(
        body, name=name, grid=(2, nt), in_specs=[h0spec, h1spec, h0spec, h1spec, spec, spec, spec],
        out_specs=[spec] * 4, out_shape=[jax.ShapeDtypeStruct(w.shape, f32)] * 4,
        compiler_params=_cp(("arbitrary", "arbitrary")),
    )(h[0], h[1], hs[0], hs[1], w, m, v)


_SMALL = (("norm_w", 2 * D), ("b_gate", 6 * D), ("lower_bounds", None), ("hgrn_norm_w", 2 * HK),
          ("gdn_norm_w", 2 * HK), ("a_log", 16), ("dt_bias", 16), ("final_norm_w", D), ("loss", None))
_CONV = (("conv_a", 2 * 3 * 512), ("conv_c", 2 * 4 * 2048))


def _small_rows(n):
    return 16 if n is None else -(-n // 1024) * 8


LB_ROW = sum(_small_rows(n) for _, n in _SMALL[:2])
ADAM_ROWS = sum(_small_rows(n) for _, n in _SMALL)
SMALL_ROWS = ADAM_ROWS + sum(_small_rows(n) for _, n in _CONV)


def small_update(parts, wp, mp, vp):
    def body(p_ref, w_ref, m_ref, v_ref, g_ref, d_ref, mo_ref, vo_ref):
        gs = p_ref[0]
        for i in range(1, 8):
            gs = gs + p_ref[i]
        w = w_ref[...]
        l0, l1 = w[LB_ROW:LB_ROW + 8], w[LB_ROW + 8:LB_ROW + 16]
        mx = jnp.maximum(l0, l1)
        e0, e1 = jnp.exp(l0 - mx), jnp.exp(l1 - mx)
        p0, p1 = e0 / (e0 + e1), e1 / (e0 + e1)
        dl1 = gs[LB_ROW + 8:LB_ROW + 16]
        s = p1 * dl1
        g = jnp.concatenate([gs[0:LB_ROW], -p0 * s, p1 * dl1 - p1 * s, gs[LB_ROW + 16:ADAM_ROWS]], axis=0)
        d, mn, vn = _adam_math(w, g, m_ref[...], v_ref[...])
        g_ref[0:ADAM_ROWS, :] = g
        g_ref[ADAM_ROWS:, :] = gs[ADAM_ROWS:]
        d_ref[...] = d
        mo_ref[...] = mn
        vo_ref[...] = vn
    sd = jax.ShapeDtypeStruct
    return pl.pallas_call(body, name="small_update",
                          out_shape=[sd((SMALL_ROWS, 128), f32)] + [sd((ADAM_ROWS, 128), f32)] * 3)(parts, wp, mp, vp)


def partial_sum(own, kind, recv, chip, name, transposed=False):
    _, r, c = recv.shape
    tr = 256 if r % 256 == 0 else r

    def body(chip_ref, o_ref, r_ref, out_ref):
        s = ((o_ref[...] + r_ref[0].astype(f32)) + r_ref[1].astype(f32)) + r_ref[2].astype(f32)
        out_ref[...] = s.T if transposed else s

    own_spec = {"own": pl.BlockSpec((tr, c), lambda i, chip: (i, 0)),
                "slot": pl.BlockSpec((None, tr, c), lambda i, chip: (chip[0], i, 0)),
                "cols": pl.BlockSpec((tr, c), lambda i, chip: (i, chip[0])),
                "rows": pl.BlockSpec((tr, c), lambda i, chip: (chip[0] * (r // tr) + i, 0))}[kind]
    out_spec = pl.BlockSpec((c, tr), lambda i, chip: (0, i)) if transposed else pl.BlockSpec((tr, c), lambda i, chip: (i, 0))
    return pl.pallas_call(
        body, name=name,
        grid_spec=pltpu.PrefetchScalarGridSpec(
            num_scalar_prefetch=1, grid=(r // tr,),
            in_specs=[own_spec, pl.BlockSpec((3, tr, c), lambda i, chip: (0, i, 0))], out_specs=out_spec),
        out_shape=jax.ShapeDtypeStruct((c, r) if transposed else (r, c), f32), compiler_params=_cp(("arbitrary",)),
    )(chip, own, recv)


def adam_pair_t(h, hs, wt, mt, vt, name):
    C, _, R = wt.shape
    tc = 128

    def body(h0_ref, h1_ref, s0_ref, s1_ref, w_ref, m_ref, v_ref, g_ref, d_ref, mo_ref, vo_ref):
        g = jnp.stack([h0_ref[...] + s0_ref[...], h1_ref[...] + s1_ref[...]], axis=1)
        d, mn, vn = _adam_math(w_ref[...], g, m_ref[...], v_ref[...])
        g_ref[...] = g
        d_ref[...] = d
        mo_ref[...] = mn
        vo_ref[...] = vn

    hspec = pl.BlockSpec((tc, R), lambda i: (i, 0))
    spec = pl.BlockSpec((tc, 2, R), lambda i: (i, 0, 0))
    return pl.pallas_call(
        body, name=name, grid=(pl.cdiv(C, tc),), in_specs=[hspec] * 4 + [spec] * 3, out_specs=[spec] * 4,
        out_shape=[jax.ShapeDtypeStruct(wt.shape, f32)] * 4, compiler_params=_cp(("parallel",)),
    )(h[0], h[1], hs[0], hs[1], wt, mt, vt)


MESH = pl.DeviceIdType.MESH
_HBM = pl.BlockSpec(memory_space=pltpu.HBM)


def _place():
    return lax.axis_index("x"), lax.axis_index("y"), lax.axis_index("c")


def weight_gather(arrs):
    n = len(arrs)

    def body(*refs):
        x_refs, out_refs = refs[:n], refs[n:2 * n]
        send_sems, recv_sems, local_sems = refs[2 * n:]
        x, y, c = _place()
        me, sibling = (x, y, c), (x, y, 1 - c)
        chips = [(1 - x, y), (x, 1 - y), (1 - x, 1 - y)]

        def copy(a, k, block, to, own_src=False):
            px, py, pc = block
            dst = out_refs[a].at[2 * px + py, pc]
            return pltpu.make_async_remote_copy(
                src_ref=x_refs[a].at[c] if own_src else dst, dst_ref=dst,
                send_sem=send_sems.at[7 * a + k], recv_sem=recv_sems.at[7 * a + k], device_id=to, device_id_type=MESH)

        mine = [pltpu.make_async_copy(x_refs[a].at[c], out_refs[a].at[2 * x + y, c], local_sems.at[a])
                for a in range(n)]
        for cp in mine:
            cp.start()
        first = []
        for a in range(n):
            first.append(copy(a, 0, me, sibling, own_src=True))
            first += [copy(a, 1 + j, me, (*chip, c), own_src=True) for j, chip in enumerate(chips)]
        for cp in first:
            cp.start()
        passed = []
        for j, chip in enumerate(chips):
            for a in range(n):
                copy(a, 1 + j, (*chip, c), me).wait_recv()
                fwd = copy(a, 4 + j, (*chip, c), sibling)
                fwd.start()
                passed.append(fwd)
        for a in range(n):
            copy(a, 0, sibling, me).wait_recv()
            for j, chip in enumerate(chips):
                copy(a, 4 + j, (*chip, 1 - c), me).wait_recv()
        for cp in first + passed:
            cp.wait_send()
        for cp in mine:
            cp.wait()

    return pl.pallas_call(
        body, name="weight_gather", in_specs=[_HBM] * n, out_specs=[_HBM] * n,
        out_shape=[jax.ShapeDtypeStruct((N_CHIPS,) + a.shape, a.dtype) for a in arrs],
        scratch_shapes=[pltpu.SemaphoreType.DMA((7 * n,)), pltpu.SemaphoreType.DMA((7 * n,)),
                        pltpu.SemaphoreType.DMA((n,))],
    )(*arrs)


SHARD_W = 256


_SEM = pl.BlockSpec(memory_space=pltpu.SEMAPHORE)
_EFFECT = pltpu.SideEffectType.DATAFLOW_SIDE_EFFECTING


def _landing_shape(a, kind):
    if kind == "all":
        return (N_CHIPS,) + a.shape
    if kind == "slot":
        return (3,) + a.shape[1:]
    return (3,) + ((a.shape[0], SHARD_W) if kind == "cols" else (SHARD_W, a.shape[1]))


def _shard_copies(src_refs, land_refs, kinds, send_sems, recv_sems):
    x, y, c = _place()
    copies = []
    for a, (src, land, kind) in enumerate(zip(src_refs, land_refs, kinds)):
        for j, (px, py) in enumerate(((1 - x, y), (x, 1 - y), (1 - x, 1 - y))):
            q = 2 * px + py
            lo = pl.multiple_of(q * SHARD_W, SHARD_W)
            part = {"slot": lambda: src.at[q], "cols": lambda: src.at[:, pl.ds(lo, SHARD_W)],
                    "rows": lambda: src.at[pl.ds(lo, SHARD_W), :], "all": lambda: src}[kind]()
            k = 3 * a + j
            copies.append(pltpu.make_async_remote_copy(
                src_ref=part, dst_ref=land.at[2 * x + y] if kind == "all" else land.at[j],
                send_sem=send_sems.at[k], recv_sem=recv_sems.at[k], device_id=(px, py, c), device_id_type=MESH))
    return copies


def exchange_start(srcs, kinds, name, after=None):
    n = len(srcs)
    lands = [lax.empty(_landing_shape(a, k), a.dtype) for a, k in zip(srcs, kinds)]
    extra = [] if after is None else [after]

    def body(*refs):
        src_refs, land_refs, token = refs[:n], refs[n:2 * n], refs[-1]
        send_sems, recv_sems = refs[2 * n + len(extra)], refs[2 * n + len(extra) + 1]
        for cp in _shard_copies(src_refs, land_refs, kinds, send_sems, recv_sems):
            cp.start()
        token[...] = jnp.zeros_like(token)

    both = list(srcs) + lands
    out = pl.pallas_call(
        body, name=name,
        out_shape=(pltpu.SemaphoreType.DMA((3 * n,)), pltpu.SemaphoreType.DMA((3 * n,)),
                   *[pltpu.HBM(a.shape, a.dtype) for a in both], jax.ShapeDtypeStruct((8, 128), f32)),
        in_specs=[_HBM] * (2 * n) + [pl.BlockSpec(memory_space=pl.ANY)] * len(extra),
        out_specs=(_SEM, _SEM, *[_HBM] * (2 * n), pl.BlockSpec(memory_space=pltpu.VMEM)),
        input_output_aliases={i: 2 + i for i in range(2 * n)},
        compiler_params=pltpu.CompilerParams(has_side_effects=_EFFECT),
    )(*[pltpu.with_memory_space_constraint(a, pltpu.HBM) for a in both], *extra)
    return (out[0], out[1], out[2:2 + 2 * n]), out[-1]


def exchange_wait(handle, kinds, after, name):
    send_sems, recv_sems, both = handle
    n = len(kinds)
    after = after if isinstance(after, tuple) else (after,)

    def body(*refs):
        src_refs, land_refs, s_sems, r_sems = refs[:n], refs[n:2 * n], refs[2 * n], refs[2 * n + 1]
        for cp in _shard_copies(src_refs, land_refs, kinds, s_sems, r_sems):
            cp.wait_send()
            cp.wait_recv()

    out = pl.pallas_call(
        body, name=name, out_shape=tuple(pltpu.HBM(a.shape, a.dtype) for a in both),
        in_specs=[_HBM] * (2 * n) + [_SEM, _SEM] + [pl.BlockSpec(memory_space=pl.ANY)] * len(after),
        out_specs=tuple([_HBM] * (2 * n)), input_output_aliases={i: i for i in range(2 * n)},
        compiler_params=pltpu.CompilerParams(has_side_effects=_EFFECT),
    )(*both, send_sems, recv_sems, *after)
    return out[n:]


def final_exchange(hs, small):
    n = len(hs)
    S = small.shape[0]

    def body(*refs):
        h_refs, sm_ref, out_refs, smalls_ref = refs[:n], refs[n], refs[n + 1:2 * n + 1], refs[2 * n + 1]
        send_sems, recv_sems, local_sem = refs[2 * n + 2:]
        x, y, c = _place()
        my_slot = smalls_ref.at[4 * x + 2 * y + c]
        mine = pltpu.make_async_copy(sm_ref, my_slot, local_sem)
        mine.start()
        copies = [pltpu.make_async_remote_copy(src_ref=h_refs[a], dst_ref=out_refs[a], send_sem=send_sems.at[a],
                                               recv_sem=recv_sems.at[a], device_id=(x, y, 1 - c), device_id_type=MESH)
                  for a in range(n)]
        for mask in range(1, 8):
            fx, fy, fc = (mask >> 2) & 1, (mask >> 1) & 1, mask & 1
            peer = ((1 - x) if fx else x, (1 - y) if fy else y, (1 - c) if fc else c)
            copies.append(pltpu.make_async_remote_copy(
                src_ref=sm_ref, dst_ref=my_slot, send_sem=send_sems.at[n - 1 + mask], recv_sem=recv_sems.at[n - 1 + mask],
                device_id=peer, device_id_type=MESH))
        for cp in copies:
            cp.start()
        for cp in copies:
            cp.wait_recv()
        for cp in copies:
            cp.wait_send()
        mine.wait()

    sd = jax.ShapeDtypeStruct
    out = pl.pallas_call(
        body, name="final_exchange", in_specs=[_HBM] * (n + 1), out_specs=[_HBM] * (n + 1),
        out_shape=[sd(h.shape, h.dtype) for h in hs] + [sd((8, S, 128), f32)],
        scratch_shapes=[pltpu.SemaphoreType.DMA((n + 7,)), pltpu.SemaphoreType.DMA((n + 7,)),
                        pltpu.SemaphoreType.DMA],
    )(*hs, small)
    return out[:n], out[n]


N_CHIPS = 4
SHARD_COLS = N_ORIG // N_CHIPS


SHARD_PAD = 2688
_COL_SEGMENTS = (
    ((0, 2048, OFF_A),)
    + tuple((2048 + 512 * j + HK * h, 2048 + 512 * j + HK * (h + 1), OFF_B + REG_BH * h + HK * j)
            for j in range(3) for h in range(N_HGRN))
    + ((3584, 4096, OFF_M + M_BZ), (4096, 6144, OFF_C), (6144, 6160, OFF_C + 2048), (6160, 7184, OFF_M + M_CZ),
       (7184, N_ORIG, OFF_M + M_G)))


def _shard_pieces():
    pieces = []
    for lo, hi, dst in _COL_SEGMENTS:
        for p in range(N_CHIPS):
            a, b = max(lo, p * SHARD_COLS), min(hi, (p + 1) * SHARD_COLS)
            if a < b:
                pieces.append((p, a - p * SHARD_COLS, dst + a - lo, b - a))
    return pieces


def win_cast_pad(wt):
    tc = 128

    def body(x_ref, o0_ref, o1_ref):
        col = pl.program_id(0) * tc + lax.broadcasted_iota(jnp.int32, (tc, 1), 0)
        for l, o_ref in enumerate((o0_ref, o1_ref)):
            o_ref[...] = _b(jnp.where(col < SHARD_COLS, x_ref[:, l, :], 0.0).T)

    spec = pl.BlockSpec((D, tc), lambda i: (0, i))
    return pl.pallas_call(
        body, name="win_cast_pad", grid=(SHARD_PAD // tc,),
        in_specs=[pl.BlockSpec((tc, 2, D), lambda i: (i, 0, 0))], out_specs=[spec, spec],
        out_shape=[jax.ShapeDtypeStruct((D, SHARD_PAD), bf16)] * 2, compiler_params=_cp(("parallel",)),
    )(wt)


def win_to_padded(w4, name):
    tr = 256
    pieces = _shard_pieces()

    def body(a_ref, o_ref):
        o_ref[...] = jnp.zeros((tr, NP), bf16)
        for p, j0, c0, n in pieces:
            o_ref[:, c0:c0 + n] = a_ref[p, :, j0:j0 + n]

    return pl.pallas_call(
        body, name=name, grid=(D // tr,),
        in_specs=[pl.BlockSpec((N_CHIPS, tr, SHARD_PAD), lambda i: (0, i, 0))],
        out_specs=pl.BlockSpec((tr, NP), lambda i: (i, 0)),
        out_shape=jax.ShapeDtypeStruct((D, NP), bf16), compiler_params=_cp(("parallel",)),
    )(w4)


def win_from_padded(dw, chip):
    tr = 128
    pieces = _shard_pieces()

    def body(chip_ref, d_ref, of_ref, ob_ref):
        of_ref[:, SHARD_COLS:] = jnp.zeros((tr, SHARD_PAD - SHARD_COLS), f32)
        for p in range(N_CHIPS):
            ob_ref[p, :, SHARD_COLS:] = jnp.zeros((tr, SHARD_PAD - SHARD_COLS), bf16)
        for p, j0, c0, n in pieces:
            v = d_ref[:, c0:c0 + n]
            ob_ref[p, :, j0:j0 + n] = _b(v)

            @pl.when(chip_ref[0] == p)
            def _(v=v, j0=j0, n=n):
                of_ref[:, j0:j0 + n] = v

    return pl.pallas_call(
        body, name="win_from_padded",
        grid_spec=pltpu.PrefetchScalarGridSpec(
            num_scalar_prefetch=1, grid=(D // tr,),
            in_specs=[pl.BlockSpec((tr, NP), lambda i, chip: (i, 0))],
            out_specs=[pl.BlockSpec((tr, SHARD_PAD), lambda i, chip: (i, 0)),
                       pl.BlockSpec((N_CHIPS, tr, SHARD_PAD), lambda i, chip: (0, i, 0))]),
        out_shape=[jax.ShapeDtypeStruct((D, SHARD_PAD), f32),
                   jax.ShapeDtypeStruct((N_CHIPS, D, SHARD_PAD), bf16)],
        compiler_params=_cp(("arbitrary",)),
    )(chip, dw)


def _rows128(a):
    flat = a.reshape(-1)
    total = -(-flat.shape[0] // 1024) * 1024
    return jnp.pad(flat, (0, total - flat.shape[0])).reshape(total // 128, 128)


def _lb_rows(lb):
    return jnp.pad(lb.reshape(2, 4, 128), ((0, 0), (0, 4), (0, 0))).reshape(16, 128)


def _pack_small(v, with_conv):
    rows = []
    for name, n in _SMALL + (_CONV if with_conv else ()):
        if name == "lower_bounds":
            rows.append(_lb_rows(v[name]))
        elif name == "loss":
            rows.append(jnp.broadcast_to(v[name], (16, 128)) if name in v else jnp.zeros((16, 128), f32))
        else:
            rows.append(_rows128(v[name]))
    return jnp.concatenate(rows, axis=0)


def _unpack_small(p, shapes, with_conv):
    out, row = {}, 0
    for name, n in _SMALL + (_CONV if with_conv else ()):
        nrows = _small_rows(n)
        blk = p[row:row + nrows]
        if name == "lower_bounds":
            out[name] = blk.reshape(2, 8, 128)[:, :4].reshape(2, 512)
        elif name == "loss":
            out[name] = blk[0, 0]
        else:
            out[name] = blk.reshape(-1)[:n].reshape(shapes[name])
        row += nrows
    return out


def _lane_vec(a8):
    return jnp.pad(a8.reshape(1, 8), ((0, 0), (8, 112)))


WEIGHT_NAMES = ("norm_w", "w_in", "b_gate", "conv_a", "conv_c", "a_log", "dt_bias", "lower_bounds", "hgrn_norm_w",
                "gdn_norm_w", "w_out_a", "w_out_b", "w_out_c", "w_o", "final_norm_w")


def kernel(x, norm_w, w_in, b_gate, conv_a, conv_c, a_log, dt_bias, lower_bounds, hgrn_norm_w, gdn_norm_w, w_out_a, w_out_b, w_out_c, w_o, final_norm_w, loss_target, m_norm_w, m_w_in, m_b_gate, m_conv_a, m_conv_c, m_a_log, m_dt_bias, m_lower_bounds, m_hgrn_norm_w, m_gdn_norm_w, m_w_out_a, m_w_out_b, m_w_out_c, m_w_o, m_final_norm_w, v_norm_w, v_w_in, v_b_gate, v_conv_a, v_conv_c, v_a_log, v_dt_bias, v_lower_bounds, v_hgrn_norm_w, v_gdn_norm_w, v_w_out_a, v_w_out_b, v_w_out_c, v_w_o, v_final_norm_w):
    wts = dict(norm_w=norm_w, w_in=w_in, b_gate=b_gate, conv_a=conv_a, conv_c=conv_c, a_log=a_log, dt_bias=dt_bias,
               lower_bounds=lower_bounds, hgrn_norm_w=hgrn_norm_w, gdn_norm_w=gdn_norm_w, w_out_a=w_out_a,
               w_out_b=w_out_b, w_out_c=w_out_c, w_o=w_o, final_norm_w=final_norm_w)
    mom = dict(norm_w=m_norm_w, w_in=m_w_in, b_gate=m_b_gate, conv_a=m_conv_a, conv_c=m_conv_c, a_log=m_a_log,
               dt_bias=m_dt_bias, lower_bounds=m_lower_bounds, hgrn_norm_w=m_hgrn_norm_w, gdn_norm_w=m_gdn_norm_w,
               w_out_a=m_w_out_a, w_out_b=m_w_out_b, w_out_c=m_w_out_c, w_o=m_w_o, final_norm_w=m_final_norm_w)
    var = dict(norm_w=v_norm_w, w_in=v_w_in, b_gate=v_b_gate, conv_a=v_conv_a, conv_c=v_conv_c, a_log=v_a_log,
               dt_bias=v_dt_bias, lower_bounds=v_lower_bounds, hgrn_norm_w=v_hgrn_norm_w, gdn_norm_w=v_gdn_norm_w,
               w_out_a=v_w_out_a, w_out_b=v_w_out_b, w_out_c=v_w_out_c, w_o=v_w_o, final_norm_w=v_final_norm_w)
    chip = 2 * lax.axis_index("x") + lax.axis_index("y")
    chip1 = chip.reshape(1).astype(jnp.int32)

    win_l0, win_l1 = win_cast_pad(jnp.transpose(w_in, (2, 0, 1)))
    win4_l0, ca4, cc4 = weight_gather([win_l0.reshape(2, D // 2, SHARD_PAD), conv_a, conv_c])
    by_cols = lambda a: a.transpose(1, 2, 0, 3).reshape(a.shape[1], a.shape[2], N_CHIPS * a.shape[3])
    by_rows = lambda a: a.transpose(1, 0, 2, 3).reshape(a.shape[1], N_CHIPS * a.shape[2], a.shape[3])
    conv_a_full, conv_c_full = by_cols(ca4), by_cols(cc4)
    later = [win_l1, _b(w_out_a), _b(w_out_b), _b(w_out_c), _b(w_o)]
    sent_w, token = exchange_start(later, ["all"] * 5, "weights_start", after=win4_l0)

    def late_weights(after):
        lands = exchange_wait(sent_w, ["all"] * 5, after, "weights_wait")
        l1, woa4, wob4, woc4, wo4 = (lax.dynamic_update_index_in_dim(land, own, chip, 0)
                                     for land, own in zip(lands, later))
        outs = dict(w_out_a=by_cols(woa4), w_out_b=by_cols(wob4), w_out_c=by_rows(woc4), w_o=by_rows(wo4))
        layers[1].update(outs, w_in=win_to_padded(l1, "win_to_padded1"))
        return outs

    lbs = lbs_fwd(lower_bounds)
    layers = []
    for l in range(2):
        layers.append(dict(
            l=l, norm_w=norm_w[l:l + 1], b_gate=b_gate[l:l + 1], conv_a=conv_a_full[l], conv_c=conv_c_full[l],
            alog_l=_lane_vec(a_log[l]), dtb_l=_lane_vec(dt_bias[l]), lbs=lbs[l:l + 1],
            hgrn_norm_w=hgrn_norm_w[l:l + 1], gdn_norm_w=gdn_norm_w[l:l + 1]))
    layers[0].update(w_in=win_to_padded(win4_l0.reshape(N_CHIPS, D, SHARD_PAD), "win_to_padded0"), late=late_weights,
                     norm_w=norm_w[0:1] + token[0:1, 0:1])

    xs, saved = x[0], []
    for l in range(2):
        xs, s = layer_fwd(xs, layers[l])
        saved.append(s)
    loss_row, dx, dfw = loss_head(xs, final_norm_w.reshape(1, D), loss_target[0])
    lg, half = [None, None], [None, None]
    for l in (1, 0):
        dx, lg[l], half[l] = layer_bwd(dx, layers[l], saved[l], chip1)
    grad_x = dx[None]

    stack = lambda n: jnp.stack([lg[0][n], lg[1][n]], axis=0)
    gsmall = {n: stack(n) for n in ("norm_w", "b_gate", "hgrn_norm_w", "gdn_norm_w", "a_log", "dt_bias", "conv_a",
                                    "conv_c")}
    gsmall.update(lower_bounds=stack("lbs"), final_norm_w=dfw, loss=loss_row)
    mat_names = ("w_in",) + tuple(n for n, _ in OUT_MATS)
    mine = [half[l][n] for n in mat_names for l in range(2)]
    theirs, smalls = final_exchange(mine, _pack_small(gsmall, True))

    out_g, out_d, out_m, out_v = {}, {}, {}, {}
    for i, n in enumerate(mat_names):
        h, hs = mine[2 * i:2 * i + 2], theirs[2 * i:2 * i + 2]
        if n == "w_in":
            fwd, back = (lambda a: jnp.transpose(a, (2, 0, 1))), (lambda a: jnp.transpose(a, (1, 2, 0)))
            res = adam_pair_t(h, hs, fwd(wts[n]), fwd(mom[n]), fwd(var[n]), "adam_" + n)
            out_g[n], out_d[n], out_m[n], out_v[n] = (back(a) for a in res)
        else:
            out_g[n], out_d[n], out_m[n], out_v[n] = adam_pair(h, hs, wts[n], mom[n], var[n], "adam_" + n)
    small_names = [n for n, _ in _SMALL if n != "loss"]
    pack = lambda v: _pack_small({n: v[n] for n in small_names}, False)
    sg, sd, smn, svn = small_update(smalls, pack(wts), pack(mom), pack(var))
    shapes = {n: wts[n].shape for n in small_names}
    shapes.update(conv_a=(2, 3, 512), conv_c=(2, 4, 2048))
    for dst, src, conv in ((out_g, sg, True), (out_d, sd, False), (out_m, smn, False), (out_v, svn, False)):
        dst.update(_unpack_small(src, shapes, conv))
    loss = out_g.pop("loss")
    for n in ("conv_a", "conv_c"):
        width = wts[n].shape[2]
        g = lax.dynamic_slice_in_dim(out_g[n], chip * width, width, axis=2)
        two_d = lambda a: a.reshape(-1, width)
        d, mn, vn = adam(two_d(wts[n]), two_d(g), two_d(mom[n]), two_d(var[n]), "adam_" + n)
        out_g[n] = g
        out_d[n], out_m[n], out_v[n] = (a.reshape(wts[n].shape) for a in (d, mn, vn))
    return (loss, grad_x, *[out_g[n] for n in WEIGHT_NAMES], *[out_d[n] for n in WEIGHT_NAMES],
            *[out_m[n] for n in WEIGHT_NAMES], *[out_v[n] for n in WEIGHT_NAMES])
```

```python
import jax
import jax.numpy as jnp
from jax import lax
from jax.experimental import pallas as pl
from jax.experimental.pallas import tpu as pltpu

f32 = jnp.float32
bf16 = jnp.bfloat16

D = 1024
L = 64
SUB = 16
NORM_EPS = 1e-6
L2_EPS = 1e-6
MIN_F = 1e-30
HK = 128
QK_SCALE = HK ** -0.5
N_GDN = 8
GDN_BLOCK = 1024
N_HGRN = 4

REG_A = 2304
REG_C = 2304
REG_M = 4608
REG_BH = 384
OFF_A, OFF_C, OFF_M, OFF_B = 0, 2304, 4608, 9216
M_BZ, M_G, M_CZ = 0, 512, 3584
NP = 10752
NP_TILE = 1536
N_ORIG = 10256

ADAM_LR, ADAM_B1, ADAM_B2, ADAM_EPS, ADAM_WD, ADAM_STEP = 0.001, 0.9, 0.999, 1e-08, 0.01, 10

VMEM_LIMIT = 56 * 1024 * 1024


def _cp(sem):
    return pltpu.CompilerParams(dimension_semantics=sem, vmem_limit_bytes=VMEM_LIMIT)


def _sigmoid(x):
    return jax.nn.sigmoid(x)


def _silu(x):
    return x * _sigmoid(x)


def _silu2(x):
    s = _sigmoid(x)
    y = x * s
    return y, s + y * (1.0 - s)


def _softplus(x):
    u = jnp.exp(-jnp.abs(x))
    w = 1.0 + u
    l1p = jnp.where(w == 1.0, u, jnp.log(w) * (u / (w - 1.0)))
    return jnp.maximum(x, 0.0) + l1p


def _dot(a, b):
    return jnp.dot(a, b, preferred_element_type=f32)


def _dot_nt(a, b):
    return lax.dot_general(a, b, (((1,), (1,)), ((), ())), preferred_element_type=f32)


def _dot_tn(a, b):
    return lax.dot_general(a, b, (((0,), (0,)), ((), ())), preferred_element_type=f32)


def _bdot(a, b):
    return lax.dot_general(a, b, (((2,), (1,)), ((0,), (0,))), preferred_element_type=f32)


def _bdot_nt(a, b):
    return lax.dot_general(a, b, (((2,), (2,)), ((0,), (0,))), preferred_element_type=f32)


def _bdot_tn(a, b):
    return lax.dot_general(a, b, (((1,), (1,)), ((0,), (0,))), preferred_element_type=f32)


def _bdot_split(a, b):
    ah, bh = _b(a), _b(b)
    al, bl = _b(a - ah.astype(f32)), _b(b - bh.astype(f32))
    return _bdot(ah, bh) + (_bdot(ah, bl) + _bdot(al, bh))


def _b(x):
    return x.astype(bf16)


def _chunk_cumsum(x, rows_in_chunk):
    n = x.shape[0]
    for s in (1, 2, 4, 8, 16, 32):
        x = x + jnp.where(rows_in_chunk >= s, pltpu.roll(x, s, axis=0), 0.0)
    return x


def _chunk_rev_cumsum(x, rows_in_chunk):
    n = x.shape[0]
    for s in (1, 2, 4, 8, 16, 32):
        x = x + jnp.where(rows_in_chunk + s < L, pltpu.roll(x, n - s, axis=0), 0.0)
    return x


def _shift_down(x, s):
    return pltpu.roll(x, s, axis=0) if s else x


def _shift_up(x, s):
    return pltpu.roll(x, x.shape[0] - s, axis=0) if s else x


def inproj_fwd(x, nw, w):
    T = x.shape[0]
    tT, tn = min(2048, T), NP_TILE

    def body(x_ref, nw_ref, w_ref, p_ref, h_ref, hs):
        @pl.when(pl.program_id(1) == 0)
        def _():
            xv = x_ref[...]
            r = lax.rsqrt(jnp.mean(xv * xv, axis=-1, keepdims=True) + NORM_EPS)
            hv = _b(xv * r * nw_ref[...])
            hs[...] = hv
            h_ref[...] = hv
        p_ref[...] = _b(_dot(hs[...], w_ref[...]))

    return pl.pallas_call(
        body, name="inproj_fwd", grid=(T // tT, NP // tn),
        in_specs=[pl.BlockSpec((tT, D), lambda i, j: (i, 0)), pl.BlockSpec((1, D), lambda i, j: (0, 0)),
                  pl.BlockSpec((D, tn), lambda i, j: (0, j))],
        out_specs=[pl.BlockSpec((tT, tn), lambda i, j: (i, j)), pl.BlockSpec((tT, D), lambda i, j: (i, 0))],
        out_shape=[jax.ShapeDtypeStruct((T, NP), bf16), jax.ShapeDtypeStruct((T, D), bf16)],
        scratch_shapes=[pltpu.VMEM((tT, D), bf16)],
        compiler_params=_cp(("parallel", "arbitrary")),
    )(x, nw, w)


def matmul_tn(a, b, name, n=None, b_col0=0, with_bf16=False):
    T, K = a.shape
    N = b.shape[1] if n is None else n
    tT = min(2048, T)
    tn = NP_TILE if N % NP_TILE == 0 else min(N, 1024)
    nt = T // tT
    cb0 = b_col0 // tn

    def body(a_ref, b_ref, o_ref, *ob_ref):
        @pl.when(pl.program_id(1) == 0)
        def _():
            o_ref[...] = jnp.zeros_like(o_ref)
        o_ref[...] += _dot_tn(_b(a_ref[...]), _b(b_ref[...]))
        if with_bf16:
            @pl.when(pl.program_id(1) == nt - 1)
            def _():
                ob_ref[0][...] = _b(o_ref[...])

    ospec = pl.BlockSpec((K, tn), lambda j, t: (0, j))
    return pl.pallas_call(
        body, name=name, grid=(N // tn, nt),
        in_specs=[pl.BlockSpec((tT, K), lambda j, t: (t, 0)), pl.BlockSpec((tT, tn), lambda j, t: (t, cb0 + j))],
        out_specs=[ospec, ospec] if with_bf16 else ospec,
        out_shape=([jax.ShapeDtypeStruct((K, N), f32), jax.ShapeDtypeStruct((K, N), bf16)] if with_bf16
                   else jax.ShapeDtypeStruct((K, N), f32)),
        compiler_params=_cp(("parallel", "arbitrary")),
    )(a, b)


def inproj_bwd(dp, w, x, nw, dres):
    T = x.shape[0]
    tT, tk = min(1024, T), NP // 4
    nk = NP // tk

    def body(dp_ref, w_ref, x_ref, nw_ref, dres_ref, dx_ref, dnw_ref, acc):
        i, k = pl.program_id(0), pl.program_id(1)

        @pl.when((i == 0) & (k == 0))
        def _():
            dnw_ref[...] = jnp.zeros_like(dnw_ref)

        @pl.when(k == 0)
        def _():
            acc[...] = jnp.zeros_like(acc)
        acc[...] += _dot_nt(dp_ref[...], w_ref[...])

        @pl.when(k == nk - 1)
        def _():
            xv = x_ref[...]
            r = lax.rsqrt(jnp.mean(xv * xv, axis=-1, keepdims=True) + NORM_EPS)
            xh = xv * r
            dy = acc[...]
            dyw = dy * nw_ref[...]
            dx_ref[...] = r * (dyw - xh * jnp.mean(dyw * xh, axis=-1, keepdims=True)) + dres_ref[...]
            dnw_ref[...] += jnp.sum(dy * xh, axis=0, keepdims=True)

    return pl.pallas_call(
        body, name="inproj_bwd", grid=(T // tT, nk),
        in_specs=[pl.BlockSpec((tT, tk), lambda i, k: (i, k)), pl.BlockSpec((D, tk), lambda i, k: (0, k)),
                  pl.BlockSpec((tT, D), lambda i, k: (i, 0)), pl.BlockSpec((1, D), lambda i, k: (0, 0)),
                  pl.BlockSpec((tT, D), lambda i, k: (i, 0))],
        out_specs=[pl.BlockSpec((tT, D), lambda i, k: (i, 0)), pl.BlockSpec((1, D), lambda i, k: (0, 0))],
        out_shape=[jax.ShapeDtypeStruct((T, D), f32), jax.ShapeDtypeStruct((1, D), f32)],
        scratch_shapes=[pltpu.VMEM((tT, D), f32)],
        compiler_params=_cp(("arbitrary", "arbitrary")),
    )(dp, w, x, nw, dres)


def loss_head(x, fw, tgt):
    T = x.shape[0]
    tT = min(1024, T)

    def body(x_ref, fw_ref, t_ref, loss_ref, dx_ref, dfw_ref):
        @pl.when(pl.program_id(0) == 0)
        def _():
            loss_ref[...] = jnp.zeros_like(loss_ref)
            dfw_ref[...] = jnp.zeros_like(dfw_ref)
        xv = x_ref[...]
        r = lax.rsqrt(jnp.mean(xv * xv, axis=-1, keepdims=True) + NORM_EPS)
        xh = xv * r
        err = xh * fw_ref[...] - t_ref[...]
        part = 0.5 * jnp.sum(jnp.mean(err * err, axis=-1, keepdims=True), axis=0, keepdims=True)
        loss_ref[...] += jnp.broadcast_to(part, loss_ref.shape)
        dy = err * (1.0 / D)
        dyw = dy * fw_ref[...]
        dx_ref[...] = r * (dyw - xh * jnp.mean(dyw * xh, axis=-1, keepdims=True))
        dfw_ref[...] += jnp.sum(dy * xh, axis=0, keepdims=True)

    return pl.pallas_call(
        body, name="loss_head", grid=(T // tT,),
        in_specs=[pl.BlockSpec((tT, D), lambda i: (i, 0)), pl.BlockSpec((1, D), lambda i: (0, 0)),
                  pl.BlockSpec((tT, D), lambda i: (i, 0))],
        out_specs=[pl.BlockSpec((1, 128), lambda i: (0, 0)), pl.BlockSpec((tT, D), lambda i: (i, 0)),
                   pl.BlockSpec((1, D), lambda i: (0, 0))],
        out_shape=[jax.ShapeDtypeStruct((1, 128), f32), jax.ShapeDtypeStruct((T, D), f32),
                   jax.ShapeDtypeStruct((1, D), f32)],
        compiler_params=_cp(("arbitrary",)),
    )(x, fw, tgt)


def _halo_specs(tT, T, width, colblk, rows=8):
    nb = T // rows
    per = tT // rows
    prev = pl.BlockSpec((rows, width), lambda i: (jnp.maximum(i * per - 1, 0), colblk))
    nxt = pl.BlockSpec((rows, width), lambda i: (jnp.minimum((i + 1) * per, nb - 1), colblk))
    return prev, nxt


def _p_rows(p_ref, width=2048):
    return p_ref[:, 0:width].astype(f32)


def _p_prev(pp_ref, width=2048):
    return pp_ref[:, 0:width].astype(f32)[8:16]


def _p_next(pn_ref, width=2048):
    return pn_ref[:, 0:width].astype(f32)[0:8]


def mixa_fwd(p, cw):
    T = p.shape[0]
    tT = min(1024, T)
    prev_spec, _ = _halo_specs(tT, T, REG_A, OFF_A // REG_A, rows=16)

    def body(p_ref, pp_ref, cw_ref, y_ref):
        pv = _p_rows(p_ref)
        u = pv[:, 512:1024] * pv[:, 1024:1536]
        pp = _p_prev(pp_ref)
        up = jnp.where(pl.program_id(0) == 0, 0.0, pp[:, 512:1024] * pp[:, 1024:1536])
        ue = jnp.concatenate([up, u], axis=0)
        cv = cw_ref[0:1, :] * _shift_down(ue, 2) + cw_ref[1:2, :] * _shift_down(ue, 1) + cw_ref[2:3, :] * ue
        y_ref[...] = _b(pv[:, 0:512] * cv[8:] * _silu(pv[:, 1536:2048]))

    return pl.pallas_call(
        body, name="mixa_fwd", grid=(T // tT,),
        in_specs=[pl.BlockSpec((tT, REG_A), lambda i: (i, OFF_A // REG_A)), prev_spec,
                  pl.BlockSpec((3, 512), lambda i: (0, 0))],
        out_specs=pl.BlockSpec((tT, 512), lambda i: (i, 0)),
        out_shape=jax.ShapeDtypeStruct((T, 512), bf16),
        compiler_params=_cp(("parallel",)),
    )(p, p, cw)


def mixa_bwd(p, cw, dy, dp):
    T = p.shape[0]
    tT = min(1024, T)
    nt = T // tT
    prev_spec, next_spec = _halo_specs(tT, T, REG_A, OFF_A // REG_A, rows=16)
    _, dnext_spec = _halo_specs(tT, T, 512, 0)

    def body(p_ref, pp_ref, pn_ref, cw_ref, dy_ref, dyn_ref, dp_in, dp_ref, dcw_ref):
        i = pl.program_id(0)

        @pl.when(i == 0)
        def _():
            dcw_ref[...] = jnp.zeros_like(dcw_ref)
        pv, pp, pn = _p_rows(p_ref), _p_prev(pp_ref), _p_next(pn_ref)
        pe = jnp.concatenate([pp, pv, pn], axis=0)
        rows = lax.broadcasted_iota(jnp.int32, (tT + 16, 1), 0)
        ab, ac, ax, az = pe[:, 0:512], pe[:, 512:1024], pe[:, 1024:1536], pe[:, 1536:2048]
        u = jnp.where((rows < 8) & (i == 0), 0.0, ac * ax)
        u1, u2 = _shift_down(u, 1), _shift_down(u, 2)
        w0, w1, w2 = cw_ref[0:1, :], cw_ref[1:2, :], cw_ref[2:3, :]
        cv = w0 * u2 + w1 * u1 + w2 * u
        dye = jnp.concatenate([jnp.zeros((8, 512), f32), dy_ref[...], dyn_ref[...]], axis=0)
        dye = jnp.where((rows >= tT + 8) & (i == nt - 1), 0.0, dye)
        sz, dsz = _silu2(az)
        dcv = dye * ab * sz
        du = w2 * dcv + w1 * _shift_up(dcv, 1) + w0 * _shift_up(dcv, 2)
        inner = (rows >= 8) & (rows < tT + 8)
        dcv_in = jnp.where(inner, dcv, 0.0)
        dcw_ref[0:1, :] += jnp.sum(dcv_in * u2, axis=0, keepdims=True)
        dcw_ref[1:2, :] += jnp.sum(dcv_in * u1, axis=0, keepdims=True)
        dcw_ref[2:3, :] += jnp.sum(dcv_in * u, axis=0, keepdims=True)
        sl = slice(8, tT + 8)
        dp_ref[:, 0:512] = _b((dye * cv * sz)[sl])
        dp_ref[:, 512:1024] = _b((du * ax)[sl])
        dp_ref[:, 1024:1536] = _b((du * ac)[sl])
        dp_ref[:, 1536:2048] = _b((dye * ab * cv * dsz)[sl])
        dp_ref[:, 2048:] = jnp.zeros((tT, REG_A - 2048), bf16)

    return pl.pallas_call(
        body, name="mixa_bwd", grid=(nt,),
        in_specs=[pl.BlockSpec((tT, REG_A), lambda i: (i, OFF_A // REG_A)), prev_spec, next_spec,
                  pl.BlockSpec((3, 512), lambda i: (0, 0)),
                  pl.BlockSpec((tT, 512), lambda i: (i, 0)), dnext_spec, pl.BlockSpec(memory_space=pl.ANY)],
        out_specs=[pl.BlockSpec((tT, REG_A), lambda i: (i, OFF_A // REG_A)), pl.BlockSpec((8, 512), lambda i: (0, 0))],
        out_shape=[jax.ShapeDtypeStruct((T, NP), bf16), jax.ShapeDtypeStruct((8, 512), f32)],
        input_output_aliases={6: 0},
        compiler_params=_cp(("arbitrary",)),
    )(p, p, p, cw, dy, dy, dp)


def _l2n_fwd(y):
    return y * lax.rsqrt(jnp.sum(y * y, axis=-1, keepdims=True) + L2_EPS)


def mixc_pre_fwd(p, cw, alog_l, dtb_l):
    T = p.shape[0]
    tT = min(1024, T)
    prev_spec, _ = _halo_specs(tT, T, REG_C, OFF_C // REG_C, rows=16)

    def body(p_ref, pp_ref, cw_ref, al_ref, dt_ref, q_ref, k_ref, v_ref, sm_ref):
        pp = jnp.where(pl.program_id(0) == 0, 0.0, _p_prev(pp_ref))
        xe = jnp.concatenate([pp, _p_rows(p_ref)], axis=0)
        cv = (cw_ref[0:1, :] * _shift_down(xe, 3) + cw_ref[1:2, :] * _shift_down(xe, 2)
              + cw_ref[2:3, :] * _shift_down(xe, 1) + cw_ref[3:4, :] * xe)[8:]
        y = _silu(cv)
        for hh in range(4):
            sl = slice(hh * HK, (hh + 1) * HK)
            q_ref[:, sl] = _l2n_fwd(y[:, sl]) * QK_SCALE
            k_ref[:, sl] = _l2n_fwd(y[:, 512 + hh * HK:512 + (hh + 1) * HK])
        v_ref[...] = y[:, 1024:2048]
        ps = p_ref[:, 2048:2176].astype(f32)
        lane = lax.broadcasted_iota(jnp.int32, ps.shape, 1)
        la = -jnp.exp(al_ref[...]) * _softplus(ps + dt_ref[...])
        rin = lax.broadcasted_iota(jnp.int32, ps.shape, 0) % L
        g = _chunk_cumsum(la, rin)
        sm_ref[...] = jnp.where(lane < 8, _sigmoid(ps), jnp.where(lane < 16, g, 0.0))

    return pl.pallas_call(
        body, name="mixc_pre_fwd", grid=(T // tT,),
        in_specs=[pl.BlockSpec((tT, REG_C), lambda i: (i, OFF_C // REG_C)), prev_spec,
                  pl.BlockSpec((4, 2048), lambda i: (0, 0)),
                  pl.BlockSpec((1, 128), lambda i: (0, 0)), pl.BlockSpec((1, 128), lambda i: (0, 0))],
        out_specs=[pl.BlockSpec((tT, 512), lambda i: (i, 0)), pl.BlockSpec((tT, 512), lambda i: (i, 0)),
                   pl.BlockSpec((tT, 1024), lambda i: (i, 0)), pl.BlockSpec((tT, 128), lambda i: (i, 0))],
        out_shape=[jax.ShapeDtypeStruct((T, 512), f32), jax.ShapeDtypeStruct((T, 512), f32),
                   jax.ShapeDtypeStruct((T, 1024), f32), jax.ShapeDtypeStruct((T, 128), f32)],
        compiler_params=_cp(("parallel",)),
    )(p, p, cw, alog_l, dtb_l)


def mixc_pre_bwd(p, cw, alog_l, dtb_l, dq8, dk8, dv, dsm8, dp):
    T = p.shape[0]
    tT = min(512, T)
    nt = T // tT
    prev_spec, next_spec = _halo_specs(tT, T, REG_C, OFF_C // REG_C, rows=16)
    _, n1024 = _halo_specs(tT, T, 1024, 0)

    def body(p_ref, pp_ref, pn_ref, cw_ref, al_ref, dt_ref, dq_ref, dqn_ref, dk_ref, dkn_ref,
             dv_ref, dvn_ref, dsm_ref, dp_in, dp_ref, dcw_ref, dsml_ref):
        i = pl.program_id(0)

        @pl.when(i == 0)
        def _():
            dcw_ref[...] = jnp.zeros_like(dcw_ref)
            dsml_ref[...] = jnp.zeros_like(dsml_ref)
        rows = lax.broadcasted_iota(jnp.int32, (tT + 16, 1), 0)
        pp = jnp.where(i == 0, 0.0, _p_prev(pp_ref))
        xe = jnp.concatenate([pp, _p_rows(p_ref), _p_next(pn_ref)], axis=0)
        xs = [_shift_down(xe, 3), _shift_down(xe, 2), _shift_down(xe, 1), xe]
        cv = cw_ref[0:1, :] * xs[0] + cw_ref[1:2, :] * xs[1] + cw_ref[2:3, :] * xs[2] + cw_ref[3:4, :] * xs[3]
        y, dy_dcv = _silu2(cv)
        last = (rows >= tT + 8) & (i == nt - 1)
        z8q = jnp.zeros((8, 1024), f32)

        def ext(cur_ref, nxt_ref):
            return jnp.where(last, 0.0, jnp.concatenate([z8q, cur_ref[...], nxt_ref[...]], axis=0))
        dq8e, dk8e, dve = ext(dq_ref, dqn_ref), ext(dk_ref, dkn_ref), ext(dv_ref, dvn_ref)
        dys = []
        for (d8, base, scale) in ((dq8e, 0, QK_SCALE), (dk8e, 512, 1.0)):
            for hh in range(4):
                dn = (d8[:, (2 * hh) * HK:(2 * hh + 1) * HK] + d8[:, (2 * hh + 1) * HK:(2 * hh + 2) * HK]) * scale
                yh = y[:, base + hh * HK:base + (hh + 1) * HK]
                r = lax.rsqrt(jnp.sum(yh * yh, axis=-1, keepdims=True) + L2_EPS)
                nh = yh * r
                dys.append(r * (dn - nh * jnp.sum(dn * nh, axis=-1, keepdims=True)))
        dyy = jnp.concatenate(dys + [dve], axis=1)
        dcv = dyy * dy_dcv
        dx = (cw_ref[3:4, :] * dcv + cw_ref[2:3, :] * _shift_up(dcv, 1) + cw_ref[1:2, :] * _shift_up(dcv, 2)
              + cw_ref[0:1, :] * _shift_up(dcv, 3))
        dp_ref[:, 0:2048] = _b(dx[8:tT + 8])
        dp_ref[:, 2176:] = jnp.zeros((tT, REG_C - 2176), bf16)
        inner = (rows >= 8) & (rows < tT + 8)
        dcv_in = jnp.where(inner, dcv, 0.0)
        for j in range(4):
            dcw_ref[j:j + 1, :] += jnp.sum(dcv_in * xs[j], axis=0, keepdims=True)
        ps = p_ref[:, 2048:2176].astype(f32)
        lane = lax.broadcasted_iota(jnp.int32, ps.shape, 1)
        dsm = dsm_ref[:, 0:128]
        for hh in range(1, N_GDN):
            dsm = dsm + dsm_ref[:, hh * 128:(hh + 1) * 128]
        beta = _sigmoid(ps)
        xa = ps + dt_ref[...]
        nea = -jnp.exp(al_ref[...])
        dpa = dsm * nea * _sigmoid(xa)
        dp_ref[:, 2048:2176] = _b(jnp.where(lane < 8, dsm * beta * (1.0 - beta), jnp.where(lane < 16, dpa, 0.0)))
        amask = (lane >= 8) & (lane < 16)
        dsml_ref[0:1, :] += jnp.sum(jnp.where(amask, dsm * nea * _softplus(xa), 0.0), axis=0, keepdims=True)
        dsml_ref[1:2, :] += jnp.sum(jnp.where(amask, dpa, 0.0), axis=0, keepdims=True)

    cur1024 = pl.BlockSpec((tT, 1024), lambda i: (i, 0))
    return pl.pallas_call(
        body, name="mixc_pre_bwd", grid=(nt,),
        in_specs=[pl.BlockSpec((tT, REG_C), lambda i: (i, OFF_C // REG_C)), prev_spec, next_spec,
                  pl.BlockSpec((4, 2048), lambda i: (0, 0)),
                  pl.BlockSpec((1, 128), lambda i: (0, 0)), pl.BlockSpec((1, 128), lambda i: (0, 0)),
                  cur1024, n1024, cur1024, n1024, cur1024, n1024, cur1024, pl.BlockSpec(memory_space=pl.ANY)],
        out_specs=[pl.BlockSpec((tT, REG_C), lambda i: (i, OFF_C // REG_C)),
                   pl.BlockSpec((8, 2048), lambda i: (0, 0)), pl.BlockSpec((8, 128), lambda i: (0, 0))],
        out_shape=[jax.ShapeDtypeStruct((T, NP), bf16),
                   jax.ShapeDtypeStruct((8, 2048), f32), jax.ShapeDtypeStruct((8, 128), f32)],
        input_output_aliases={13: 0},
        compiler_params=_cp(("arbitrary",)),
    )(p, p, p, cw, alog_l, dtb_l, dq8, dq8, dk8, dk8, dv, dv, dsm8, dp)


def _tri_inverse(m):
    r = lax.broadcasted_iota(jnp.int32, (L, L), 0)
    c = lax.broadcasted_iota(jnp.int32, (L, L), 1)
    eye = (r == c).astype(f32)[None]
    same = lambda w: ((r // w) == (c // w))[None]
    md = jnp.where(same(8), m, 0.0)
    m2 = _bdot_split(md, md)
    m4 = _bdot_split(m2, m2)
    t = _bdot_split(_bdot_split(eye - md, eye + m2), eye + m4)
    for w in (16, 32, 64):
        mo = jnp.where(same(w) & jnp.logical_not(same(w // 2)), m, 0.0)
        t = t - _bdot_split(_bdot_split(t, mo), t)
    return t


def _col_to_row(col, eye):
    return jnp.sum(eye * col, axis=1, keepdims=True)


def _row_to_col(row, eye):
    return jnp.sum(eye * row, axis=2, keepdims=True)


def _gdn_chunk_terms(q, k, v, beta, g, t_inv=None):
    r = lax.broadcasted_iota(jnp.int32, (L, L), 0)
    c = lax.broadcasted_iota(jnp.int32, (L, L), 1)
    eye = (r == c).astype(f32)[None]
    causal, strict = (c <= r)[None], (c < r)[None]
    diff = g - _col_to_row(g, eye)
    dec = jnp.exp(jnp.where(causal, diff, 0.0))
    dc = jnp.where(causal, dec, 0.0)
    ds = jnp.where(strict, dec, 0.0)
    eg = jnp.exp(g)
    gl = g[:, L - 1:L, :]
    egl = jnp.exp(gl - g)
    kb = k * beta
    kk = _bdot_nt(_b(k), _b(kb))
    qk = _bdot_nt(_b(q), _b(kb))
    m = kk * ds
    aqk = qk * dc
    if t_inv is None:
        t_inv = _tri_inverse(m)
    tb = _b(t_inv)
    keg = k * eg
    u = _bdot(tb, _b(v))
    w = _bdot(tb, _b(keg))
    ks = kb * egl
    ksw = _bdot_tn(_b(ks), _b(w))
    return dict(eye=eye, causal=causal, strict=strict, dc=dc, ds=ds, eg=eg, gl=gl, egl=egl, kb=kb, kk=kk, qk=qk,
                m=m, aqk=aqk, t=t_inv, u=u, w=w, qi=q * eg, ks=ks, keg=keg, ksw=ksw)


def gdn_fwd(qn, kn, vv, sm):
    T = qn.shape[0]
    tB = min(2 * GDN_BLOCK, T)
    nc = tB // L
    N = T // L

    def body(q_ref, k_ref, v_ref, sm_ref, o_ref, st_ref, ti_ref, s_scr):
        j = pl.program_id(0)

        @pl.when(pl.program_id(1) == 0)
        def _():
            s_scr[...] = jnp.zeros_like(s_scr)
        smv = sm_ref[...]
        lane = lax.broadcasted_iota(jnp.int32, smv.shape, 1)
        q = q_ref[...].reshape(nc, L, HK)
        k = k_ref[...].reshape(nc, L, HK)
        tms, ksus, kswbs, egls = [], [], [], []
        for a in range(2):
            h = 2 * j + a
            beta = jnp.sum(jnp.where(lane == h, smv, 0.0), axis=1, keepdims=True).reshape(nc, L, 1)
            g = jnp.sum(jnp.where(lane == 8 + h, smv, 0.0), axis=1, keepdims=True).reshape(nc, L, 1)
            v = v_ref[:, a * HK:(a + 1) * HK].reshape(nc, L, HK)
            tm = _gdn_chunk_terms(q, k, v, beta, g)
            ti_ref[a] = tm["t"]
            tms.append(tm)
            ksus.append(_bdot_tn(_b(tm["ks"]), _b(tm["u"])))
            kswbs.append(_b(tm["ksw"]))
            egls.append(jnp.exp(tm["gl"]))
        s = [s_scr[0], s_scr[1]]
        states = [[None] * nc, [None] * nc]
        for ci in range(nc):
            for a in range(2):
                states[a][ci] = s[a]
                s[a] = egls[a][ci] * s[a] + (ksus[a][ci] - _dot(kswbs[a][ci], _b(s[a])))
        for a in range(2):
            s_scr[a] = s[a]
            sall = jnp.stack(states[a], axis=0)
            st_ref[a] = sall
            sb = _b(sall)
            tm = tms[a]
            e = tm["u"] - _bdot(_b(tm["w"]), sb)
            o = _bdot(_b(tm["qi"]), sb) + _bdot(_b(tm["aqk"]), _b(e))
            o_ref[:, a * HK:(a + 1) * HK] = o.reshape(tB, HK)

    return pl.pallas_call(
        body, name="gdn_fwd", grid=(N_GDN // 2, T // tB),
        in_specs=[pl.BlockSpec((tB, HK), lambda j, n: (n, j)), pl.BlockSpec((tB, HK), lambda j, n: (n, j)),
                  pl.BlockSpec((tB, 2 * HK), lambda j, n: (n, j)), pl.BlockSpec((tB, 128), lambda j, n: (n, 0))],
        out_specs=[pl.BlockSpec((tB, 2 * HK), lambda j, n: (n, j)),
                   pl.BlockSpec((2, nc, HK, HK), lambda j, n: (j, n, 0, 0)),
                   pl.BlockSpec((2, nc, L, L), lambda j, n: (j, n, 0, 0))],
        out_shape=[jax.ShapeDtypeStruct((T, N_GDN * HK), f32), jax.ShapeDtypeStruct((N_GDN, N, HK, HK), f32),
                   jax.ShapeDtypeStruct((N_GDN, N, L, L), f32)],
        scratch_shapes=[pltpu.VMEM((2, HK, HK), f32)],
        compiler_params=_cp(("parallel", "arbitrary")),
    )(qn, kn, vv, sm)


def gdn_bwd(qn, kn, vv, sm, st, ti, do):
    T = qn.shape[0]
    tB = min(GDN_BLOCK, T)
    nc = tB // L
    nb = T // tB

    def body(q_ref, k_ref, v_ref, sm_ref, st_ref, ti_ref, do_ref, dq_ref, dk_ref, dv_ref, dsm_ref, ds_scr):
        j = pl.program_id(0)

        @pl.when(pl.program_id(1) == 0)
        def _():
            ds_scr[...] = jnp.zeros_like(ds_scr)
        smv = sm_ref[...]
        lane = lax.broadcasted_iota(jnp.int32, smv.shape, 1)
        q = q_ref[...].reshape(nc, L, HK)
        k = k_ref[...].reshape(nc, L, HK)
        kbf, qbf = _b(k), _b(q)
        heads = []
        for a in range(2):
            h = 2 * j + a
            beta = jnp.sum(jnp.where(lane == h, smv, 0.0), axis=1, keepdims=True).reshape(nc, L, 1)
            g = jnp.sum(jnp.where(lane == 8 + h, smv, 0.0), axis=1, keepdims=True).reshape(nc, L, 1)
            v = v_ref[:, a * HK:(a + 1) * HK].reshape(nc, L, HK)
            do = do_ref[:, a * HK:(a + 1) * HK].reshape(nc, L, HK)
            s = st_ref[a]
            tm = _gdn_chunk_terms(q, k, v, beta, g, t_inv=ti_ref[a])
            sb, dob = _b(s), _b(do)
            e = tm["u"] - _bdot(_b(tm["w"]), sb)
            de0 = _bdot_tn(_b(tm["aqk"]), dob)
            ds0 = _bdot_tn(_b(tm["qi"]), dob) - _bdot_tn(_b(tm["w"]), _b(de0))
            heads.append(dict(h=h, beta=beta, tm=tm, s=s, sb=sb, dob=dob, e=e, de0=de0, ds0=ds0, kswb=_b(tm["ksw"]),
                              egl_last=jnp.exp(tm["gl"])))
        dsn = [ds_scr[0], ds_scr[1]]
        dsns = [[None] * nc, [None] * nc]
        for ci in reversed(range(nc)):
            for a, hd in enumerate(heads):
                dsns[a][ci] = dsn[a]
                dsn[a] = hd["ds0"][ci] + (hd["egl_last"][ci] * dsn[a] - _dot_tn(hd["kswb"][ci], _b(dsn[a])))
        rowi = lax.broadcasted_iota(jnp.int32, (nc, L, 1), 1)
        rin = lax.broadcasted_iota(jnp.int32, (tB, 128), 0) % L
        for a, hd in enumerate(heads):
            ds_scr[a] = dsn[a]
            tm, s, sb, dob, e, beta, h = hd["tm"], hd["s"], hd["sb"], hd["dob"], hd["e"], hd["beta"], hd["h"]
            eye, dc, ds_, eg, egl = tm["eye"], tm["dc"], tm["ds"], tm["eg"], tm["egl"]
            kb, u, w, qi, ks = tm["kb"], tm["u"], tm["w"], tm["qi"], tm["ks"]
            eb = _b(e)
            dsp = jnp.stack(dsns[a], axis=0)
            dspb = _b(dsp)
            de = hd["de0"] + _bdot(_b(ks), dspb)
            deb = _b(de)
            dks = _bdot_nt(eb, dspb)
            dqi = _bdot_nt(dob, sb)
            daqk = jnp.where(tm["causal"], _bdot_nt(dob, eb), 0.0)
            dw = -_bdot_nt(deb, sb)
            tb = _b(tm["t"])
            dvv = _bdot_tn(tb, deb)
            dkg = _bdot_tn(tb, _b(dw))
            dm = -jnp.where(tm["strict"], _bdot_nt(_b(dvv), _b(u)) + _bdot_nt(_b(dkg), _b(w)), 0.0)
            x = _b(dm * ds_)
            y = _b(daqk * dc)
            kbb = _b(kb)
            dk = _bdot(x, kbb) + dkg * eg
            dkb = _bdot_tn(x, kbf) + _bdot_tn(y, qbf) + dks * egl
            dq = _bdot(y, kbb) + dqi * eg
            dk = dk + dkb * beta
            dbeta = jnp.sum(dkb * k, axis=-1, keepdims=True)
            z = dm * tm["m"] + daqk * tm["aqk"]
            dg = (jnp.sum(dqi * qi - dks * ks + dkg * tm["keg"], axis=-1, keepdims=True)
                  + jnp.sum(z, axis=-1, keepdims=True) - _row_to_col(jnp.sum(z, axis=1, keepdims=True), eye))
            dgl = (hd["egl_last"] * jnp.sum(jnp.sum(s * dsp, axis=2, keepdims=True), axis=1, keepdims=True)
                   + jnp.sum(jnp.sum(dks * ks, axis=2, keepdims=True), axis=1, keepdims=True))
            dg = dg + jnp.where(rowi == L - 1, dgl, 0.0)
            dla = _chunk_rev_cumsum(jnp.broadcast_to(dg.reshape(tB, 1), (tB, 128)), rin)
            sl = slice(a * HK, (a + 1) * HK)
            dq_ref[:, sl] = dq.reshape(tB, HK)
            dk_ref[:, sl] = dk.reshape(tB, HK)
            dv_ref[:, sl] = dvv.reshape(tB, HK)
            dsm_ref[:, sl] = jnp.where(lane == h, dbeta.reshape(tB, 1), jnp.where(lane == 8 + h, dla, 0.0))

    rev = lambda n: nb - 1 - n
    pair = pl.BlockSpec((tB, 2 * HK), lambda j, n: (rev(n), j))
    return pl.pallas_call(
        body, name="gdn_bwd", grid=(N_GDN // 2, nb),
        in_specs=[pl.BlockSpec((tB, HK), lambda j, n: (rev(n), j)), pl.BlockSpec((tB, HK), lambda j, n: (rev(n), j)),
                  pair, pl.BlockSpec((tB, 128), lambda j, n: (rev(n), 0)),
                  pl.BlockSpec((2, nc, HK, HK), lambda j, n: (j, rev(n), 0, 0)),
                  pl.BlockSpec((2, nc, L, L), lambda j, n: (j, rev(n), 0, 0)), pair],
        out_specs=[pair] * 4,
        out_shape=[jax.ShapeDtypeStruct((T, N_GDN * HK), f32)] * 4,
        scratch_shapes=[pltpu.VMEM((2, HK, HK), f32)],
        compiler_params=_cp(("parallel", "arbitrary")),
    )(qn, kn, vv, sm, st, ti, do)


def _hgrn_prep(bq, bf_, bi, lb):
    tB = bq.shape[0]
    sq, dsq = _silu2(bq)
    sg = _sigmoid(bf_)
    f = lb + (1.0 - lb) * sg
    logf = jnp.log(jnp.maximum(f, MIN_F))
    rin = lax.broadcasted_iota(jnp.int32, (tB, HK), 0) % L
    g = _chunk_cumsum(logf, rin)
    return sq * QK_SCALE, sg, f, 1.0 - f, bi, g, rin, dsq * QK_SCALE


def _hgrn_intra(q, kk, v, g, do=None):
    n = q.shape[0]
    nsub = L // SUB
    bwd = do is not None
    o_rows = [None] * nsub
    if bwd:
        dq_rows = [None] * nsub
        dkk_acc = jnp.zeros_like(kk)
        dv_acc = jnp.zeros_like(v)
    for i in range(1, nsub):
        lo, hi, w = i * SUB, (i + 1) * SUB, i * SUB
        ref = g[:, lo - 1:lo, :]
        eq = jnp.exp(g[:, lo:hi, :] - ref)
        ek = jnp.exp(ref - g[:, :w, :])
        qs = _b(q[:, lo:hi, :] * eq)
        ks = _b(kk[:, :w, :] * ek)
        p = _bdot_nt(qs, ks)
        o_rows[i] = _bdot(_b(p), _b(v[:, :w, :]))
        if bwd:
            dob = _b(do[:, lo:hi, :])
            dp = _b(_bdot_nt(dob, _b(v[:, :w, :])))
            dq_rows[i] = _bdot(dp, ks) * eq
            pad = jnp.zeros((n, L - w, HK), f32)
            dkk_acc = dkk_acc + jnp.concatenate([_bdot_tn(dp, qs) * ek, pad], axis=1)
            dv_acc = dv_acc + jnp.concatenate([_bdot_tn(_b(p), dob), pad], axis=1)
    m = n * nsub
    q4, k4, v4, g4 = (a.reshape(m, SUB, HK) for a in (q, kk, v, g))
    r = lax.broadcasted_iota(jnp.int32, (m, SUB, HK), 1)
    od = jnp.zeros((m, SUB, HK), f32)
    if bwd:
        do4 = do.reshape(m, SUB, HK)
        dqd = jnp.zeros((m, SUB, HK), f32)
        dkd = jnp.zeros((m, SUB, HK), f32)
        dvd = jnp.zeros((m, SUB, HK), f32)
    for j in range(SUB):
        gj, kj, vj = g4[:, j:j + 1, :], k4[:, j:j + 1, :], v4[:, j:j + 1, :]
        ok = r >= j
        e = jnp.where(ok, jnp.exp(g4 - gj), 0.0)
        xq = q4 * e
        pj = jnp.sum(xq * kj, axis=-1, keepdims=True)
        od = od + pj * vj
        if bwd:
            dpj = jnp.sum(do4 * vj, axis=-1, keepdims=True)
            dqd = dqd + dpj * kj * e
            dkd = dkd + jnp.where(r == j, jnp.sum(dpj * xq, axis=1, keepdims=True), 0.0)
            dvd = dvd + jnp.where(r == j, jnp.sum(pj * do4, axis=1, keepdims=True), 0.0)
    od = od.reshape(n, L, HK)
    o = jnp.concatenate([od[:, :SUB, :]] + [od[:, i * SUB:(i + 1) * SUB, :] + o_rows[i] for i in range(1, nsub)], axis=1)
    if not bwd:
        return o
    dqd = dqd.reshape(n, L, HK)
    dq = jnp.concatenate([dqd[:, :SUB, :]] + [dqd[:, i * SUB:(i + 1) * SUB, :] + dq_rows[i] for i in range(1, nsub)], axis=1)
    return o, dq, dkk_acc + dkd.reshape(n, L, HK), dv_acc + dvd.reshape(n, L, HK)


def hgrn_fwd(p, lbs):
    T = p.shape[0]
    tB = min(1024, T)
    nc = tB // L
    N = T // L

    def body(b_ref, lb_ref, o_ref, st_ref, s_scr):
        @pl.when(pl.program_id(1) == 0)
        def _():
            s_scr[...] = jnp.zeros_like(s_scr)
        bv = b_ref[...].astype(f32)
        q, sg, f, kk, v, g, rin, _ = _hgrn_prep(bv[:, 0:HK], bv[:, HK:2 * HK], bv[:, 2 * HK:3 * HK], lb_ref[...])
        q3, k3, v3, g3 = (a.reshape(nc, L, HK) for a in (q, kk, v, g))
        o = _hgrn_intra(q3, k3, v3, g3)
        gl = g3[:, L - 1:L, :]
        qt = _b(q3 * jnp.exp(g3))
        kt = _b(k3 * jnp.exp(gl - g3))
        vb = _b(v3)
        st = s_scr[...]
        for c in range(nc):
            st_ref[c] = st
            o_ref[c * L:(c + 1) * L, :] = o[c] + _dot_nt(qt[c], _b(st))
            st = st * jnp.exp(gl[c]) + _dot_tn(vb[c], kt[c])
        s_scr[...] = st

    return pl.pallas_call(
        body, name="hgrn_fwd", grid=(N_HGRN, T // tB),
        in_specs=[pl.BlockSpec((tB, REG_BH), lambda h, n: (n, OFF_B // REG_BH + h)),
                  pl.BlockSpec((1, HK), lambda h, n: (0, h))],
        out_specs=[pl.BlockSpec((tB, HK), lambda h, n: (n, h)),
                   pl.BlockSpec((None, nc, HK, HK), lambda h, n: (h, n, 0, 0))],
        out_shape=[jax.ShapeDtypeStruct((T, N_HGRN * HK), f32), jax.ShapeDtypeStruct((N_HGRN, N, HK, HK), f32)],
        scratch_shapes=[pltpu.VMEM((HK, HK), f32)],
        compiler_params=_cp(("parallel", "arbitrary")),
    )(p, lbs)


def hgrn_bwd(p, lbs, st, do, dp):
    T = p.shape[0]
    tB = min(256, T)
    nc = tB // L
    nb = T // tB

    def body(b_ref, lb_ref, st_ref, do_ref, dp_in, dp_ref, dlb_ref, ds_scr):
        @pl.when(pl.program_id(1) == 0)
        def _():
            ds_scr[...] = jnp.zeros_like(ds_scr)
            dlb_ref[...] = jnp.zeros_like(dlb_ref)
        lb = lb_ref[...]
        bv = b_ref[...].astype(f32)
        q, sg, f, kk, v, g, rin, dq_dbq = _hgrn_prep(bv[:, 0:HK], bv[:, HK:2 * HK], bv[:, 2 * HK:3 * HK], lb)
        q3, k3, v3, g3 = (a.reshape(nc, L, HK) for a in (q, kk, v, g))
        do3 = do_ref[...].reshape(nc, L, HK)
        dob = _b(do3)
        gl = g3[:, L - 1:L, :]
        egl = jnp.exp(gl)
        eg, egr = jnp.exp(g3), jnp.exp(gl - g3)
        qt, kt = q3 * eg, k3 * egr
        s = st_ref[...]
        ds0 = _bdot_tn(dob, _b(qt))
        dsn = ds_scr[...]
        dsns = [None] * nc
        for c in reversed(range(nc)):
            dsns[c] = dsn
            dsn = ds0[c] + dsn * egl[c]
        ds_scr[...] = dsn
        dsp = jnp.stack(dsns, axis=0)
        dspb = _b(dsp)
        dqt = _bdot(dob, _b(s))
        dkt = _bdot(_b(v3), dspb)
        dv_state = _bdot_nt(_b(kt), dspb)
        dgl = egl * jnp.sum(s * dsp, axis=1, keepdims=True) + jnp.sum(dkt * kt, axis=1, keepdims=True)
        _, dq_i, dkk_i, dv_i = _hgrn_intra(q3, k3, v3, g3, do=do3)
        dq = dq_i + dqt * eg
        dkk = dkk_i + dkt * egr
        dv = dv_i + dv_state
        rowi = lax.broadcasted_iota(jnp.int32, (nc, L, HK), 1)
        dg = q3 * dq - k3 * dkk + jnp.where(rowi == L - 1, dgl, 0.0)
        dlogf = _chunk_rev_cumsum(dg.reshape(tB, HK), rin)
        dkk2 = dkk.reshape(tB, HK)
        df = jnp.where(f > MIN_F, dlogf / f, 0.0) - dkk2
        dlb_ref[...] += jnp.sum(df * (1.0 - sg), axis=0, keepdims=True)
        dp_ref[:, 0:HK] = _b(dq.reshape(tB, HK) * dq_dbq)
        dp_ref[:, HK:2 * HK] = _b(df * (1.0 - lb) * sg * (1.0 - sg))
        dp_ref[:, 2 * HK:3 * HK] = _b(dv.reshape(tB, HK))

    rev = lambda n: nb - 1 - n
    return pl.pallas_call(
        body, name="hgrn_bwd", grid=(N_HGRN, nb),
        in_specs=[pl.BlockSpec((tB, REG_BH), lambda h, n: (rev(n), OFF_B // REG_BH + h)),
                  pl.BlockSpec((1, HK), lambda h, n: (0, h)),
                  pl.BlockSpec((None, nc, HK, HK), lambda h, n: (h, rev(n), 0, 0)),
                  pl.BlockSpec((tB, HK), lambda h, n: (rev(n), h)), pl.BlockSpec(memory_space=pl.ANY)],
        out_specs=[pl.BlockSpec((tB, REG_BH), lambda h, n: (rev(n), OFF_B // REG_BH + h)),
                   pl.BlockSpec((1, HK), lambda h, n: (0, h))],
        out_shape=[jax.ShapeDtypeStruct((T, NP), bf16), jax.ShapeDtypeStruct((1, N_HGRN * HK), f32)],
        input_output_aliases={4: 0},
        scratch_shapes=[pltpu.VMEM((HK, HK), f32)],
        compiler_params=_cp(("parallel", "arbitrary")),
    )(p, lbs, st, do, dp)


def _headnorm_fwd(o, z, w, nheads):
    outs, parts = [], []
    for hh in range(nheads):
        sl = slice(hh * HK, (hh + 1) * HK)
        oh = o[:, sl]
        r = lax.rsqrt(jnp.mean(oh * oh, axis=-1, keepdims=True) + NORM_EPS)
        on = oh * r
        sz, dsz = _silu2(z[:, sl])
        outs.append(on * w * sz)
        parts.append((r, on, sz, dsz))
    return jnp.concatenate(outs, axis=1), parts


def _headnorm_bwd(parts, w, dy):
    dos, dzs = [], []
    dw = jnp.zeros((1, HK), f32)
    for hh, (r, on, sz, dsz) in enumerate(parts):
        dyh = dy[:, hh * HK:(hh + 1) * HK]
        dn = dyh * sz * w
        dos.append(r * (dn - on * jnp.mean(dn * on, axis=-1, keepdims=True)))
        dzs.append(dyh * on * w * dsz)
        dw = dw + jnp.sum(dyh * sz * on, axis=0, keepdims=True)
    return jnp.concatenate(dos, axis=1), jnp.concatenate(dzs, axis=1), dw


def _merge_specs(tT, l):
    row = lambda w, cb=0: pl.BlockSpec((tT, w), lambda i, cb=cb: (i, cb))
    full = lambda r, c: pl.BlockSpec((r, c), lambda i: (0, 0))
    layer = lambda r, c: pl.BlockSpec((None, r, c), lambda i: (l, 0, 0))
    return row, full, layer


def merge_fwd(x, p, ya, ob, oc, hw, gw, bg, woa, wob, woc, wo, l):
    T = x.shape[0]
    tT = min(512, T)
    row, full, layer = _merge_specs(tT, l)

    def body(x_ref, pm_ref, ya_ref, ob_ref, oc_ref, hw_ref, gw_ref, bg_ref,
             woa_ref, wob_ref, woc_ref, wo_ref, out_ref):
        yb = _b(_headnorm_fwd(ob_ref[...], pm_ref[:, M_BZ:M_G].astype(f32), hw_ref[...], N_HGRN)[0])
        yc = _b(_headnorm_fwd(oc_ref[...], pm_ref[:, M_CZ:REG_M].astype(f32), gw_ref[...], N_GDN)[0])
        gates = _sigmoid(pm_ref[:, M_G:M_CZ].astype(f32) + bg_ref[...])
        merged = (gates[:, 0:D] * _dot(ya_ref[...], woa_ref[...]) + gates[:, D:2 * D] * _dot(yb, wob_ref[...])
                  + gates[:, 2 * D:3 * D] * _dot(yc, woc_ref[...]))
        out_ref[...] = x_ref[...] + _dot(_b(merged), wo_ref[...])

    return pl.pallas_call(
        body, name="merge_fwd", grid=(T // tT,),
        in_specs=[row(D), row(REG_M, OFF_M // REG_M),
                  row(512), row(512), row(1024), full(1, HK), full(1, HK), full(1, 3 * D),
                  layer(512, D), layer(512, D), layer(D, D), layer(D, D)],
        out_specs=row(D),
        out_shape=jax.ShapeDtypeStruct((T, D), f32),
        compiler_params=_cp(("parallel",)),
    )(x, p, ya, ob, oc, hw, gw, bg, woa, wob, woc, wo)


def merge_bwd(dxo, p, ya, ob, oc, hw, gw, bg, woa, wob, woc, wo, l):
    T = dxo.shape[0]
    tT = min(256, T)
    row, full, layer = _merge_specs(tT, l)

    def body(dx_ref, pm_ref, ya_ref, ob_ref, oc_ref, hw_ref, gw_ref, bg_ref,
             woa_ref, wob_ref, woc_ref, wo_ref,
             dya_ref, dob_ref, doc_ref, dp_ref, mg_ref, dy3_ref, yb_ref, yc_ref,
             dbg_ref, dhw_ref, dgw_ref):
        @pl.when(pl.program_id(0) == 0)
        def _():
            dbg_ref[...] = jnp.zeros_like(dbg_ref)
            dhw_ref[...] = jnp.zeros_like(dhw_ref)
            dgw_ref[...] = jnp.zeros_like(dgw_ref)
        ob, oc, bz, cz = ob_ref[...], oc_ref[...], pm_ref[:, M_BZ:M_G].astype(f32), pm_ref[:, M_CZ:REG_M].astype(f32)
        hw_, gw_ = hw_ref[...], gw_ref[...]
        yb, parts_b = _headnorm_fwd(ob, bz, hw_, N_HGRN)
        yc, parts_c = _headnorm_fwd(oc, cz, gw_, N_GDN)
        yb, yc = _b(yb), _b(yc)
        yb_ref[...] = yb
        yc_ref[...] = yc
        gates = _sigmoid(pm_ref[:, M_G:M_CZ].astype(f32) + bg_ref[...])
        ys = (_dot(ya_ref[...], woa_ref[...]), _dot(yb, wob_ref[...]), _dot(yc, woc_ref[...]))
        dmerged = _dot_nt(_b(dx_ref[...]), wo_ref[...])
        merged = jnp.zeros_like(dmerged)
        dys = []
        for i in range(3):
            gi = gates[:, i * D:(i + 1) * D]
            merged = merged + gi * ys[i]
            dyi = _b(dmerged * gi)
            dys.append(dyi)
            dy3_ref[:, i * D:(i + 1) * D] = dyi
            dgp = dmerged * ys[i] * gi * (1.0 - gi)
            dp_ref[:, M_G + i * D:M_G + (i + 1) * D] = _b(dgp)
            dbg_ref[:, i * D:(i + 1) * D] += jnp.sum(dgp, axis=0, keepdims=True)
        mg_ref[...] = _b(merged)
        dya_ref[...] = _dot_nt(dys[0], woa_ref[...])
        dob, dbz, dhw = _headnorm_bwd(parts_b, hw_, _dot_nt(dys[1], wob_ref[...]))
        doc, dcz, dgw = _headnorm_bwd(parts_c, gw_, _dot_nt(dys[2], woc_ref[...]))
        dob_ref[...] = dob
        doc_ref[...] = doc
        dp_ref[:, M_BZ:M_G] = _b(dbz)
        dp_ref[:, M_CZ:REG_M] = _b(dcz)
        dhw_ref[...] += dhw
        dgw_ref[...] += dgw

    sd = jax.ShapeDtypeStruct
    return pl.pallas_call(
        body, name="merge_bwd", grid=(T // tT,),
        in_specs=[row(D), row(REG_M, OFF_M // REG_M),
                  row(512), row(512), row(1024), full(1, HK), full(1, HK), full(1, 3 * D),
                  layer(512, D), layer(512, D), layer(D, D), layer(D, D)],
        out_specs=[row(512), row(512), row(1024), row(REG_M, OFF_M // REG_M), row(D), row(3 * D), row(512),
                   row(1024), full(1, 3 * D), full(1, HK), full(1, HK)],
        out_shape=[sd((T, 512), f32), sd((T, 512), f32), sd((T, 1024), f32), sd((T, NP), bf16),
                   sd((T, D), bf16), sd((T, 3 * D), bf16), sd((T, 512), bf16),
                   sd((T, 1024), bf16), sd((1, 3 * D), f32), sd((1, HK), f32), sd((1, HK), f32)],
        compiler_params=_cp(("arbitrary",)),
    )(dxo, p, ya, ob, oc, hw, gw, bg, woa, wob, woc, wo)


def layer_fwd(x, w):
    l = w["l"]
    p, h = inproj_fwd(x, w["norm_w"], w["w_in"])
    ya = mixa_fwd(p, w["conv_a"])
    qn, kn, vv, sm = mixc_pre_fwd(p, w["conv_c"], w["alog_l"], w["dtb_l"])
    oc, st_c, ti = gdn_fwd(qn, kn, vv, sm)
    ob, st_b = hgrn_fwd(p, w["lbs"])
    if "late" in w:
        w.update(w.pop("late")((ya, oc, ob)))
    xo = merge_fwd(x, p, ya, ob, oc, w["hgrn_norm_w"], w["gdn_norm_w"], w["b_gate"],
                   w["w_out_a"], w["w_out_b"], w["w_out_c"], w["w_o"], l)
    saved = dict(x=x, p=p, h=h, ya=ya, qn=qn, kn=kn, vv=vv, sm=sm, oc=oc, st_c=st_c, ti=ti, ob=ob, st_b=st_b)
    return xo, saved


OUT_MATS = (("w_out_a", "cols"), ("w_out_b", "cols"), ("w_out_c", "rows"), ("w_o", "rows"))


def layer_bwd(dxo, w, s, chip):
    p, l = s["p"], w["l"]
    (dya, dob, doc, dp, merged, dy3, yb, yc, dbg, dhw, dgw) = merge_bwd(
        dxo, p, s["ya"], s["ob"], s["oc"], w["hgrn_norm_w"], w["gdn_norm_w"], w["b_gate"],
        w["w_out_a"], w["w_out_b"], w["w_out_c"], w["w_o"], l)
    full = {"w_o": matmul_tn(merged, dxo, "dw_o", with_bf16=True),
            "w_out_a": matmul_tn(s["ya"], dy3, "dw_out_a", n=D, b_col0=0, with_bf16=True),
            "w_out_b": matmul_tn(yb, dy3, "dw_out_b", n=D, b_col0=D, with_bf16=True),
            "w_out_c": matmul_tn(yc, dy3, "dw_out_c", n=D, b_col0=2 * D, with_bf16=True)}
    out_kinds = [k for _, k in OUT_MATS]
    sent_out, token = exchange_start([full[n][1] for n, _ in OUT_MATS], out_kinds, f"grads_out_start{l}")
    dp, dlbs = hgrn_bwd(p, w["lbs"] + token[0:1, 0:1], s["st_b"], dob, dp)
    dq8, dk8, dvv, dsm8 = gdn_bwd(s["qn"], s["kn"], s["vv"], s["sm"], s["st_c"], s["ti"], doc)
    dp, dcc, dsmall = mixc_pre_bwd(p, w["conv_c"], w["alog_l"], w["dtb_l"], dq8, dk8, dvv, dsm8, dp)
    dp, dca = mixa_bwd(p, w["conv_a"], dya, dp)
    gf_win, gb_win = win_from_padded(matmul_tn(s["h"], dp, "dw_in"), chip)
    sent_in, token = exchange_start([gb_win], ["slot"], f"grads_in_start{l}")
    dx, dnw = inproj_bwd(dp, w["w_in"], s["x"], w["norm_w"] + token[0:1, 0:1], dxo)
    recv_out = exchange_wait(sent_out, out_kinds, dx, f"grads_out_wait{l}")
    recv_in = exchange_wait(sent_in, ["slot"], dx, f"grads_in_wait{l}")
    half = {"w_in": partial_sum(gf_win, "own", recv_in[0], chip, "psum_w_in", transposed=True)}
    for (n, kind), r in zip(OUT_MATS, recv_out):
        half[n] = partial_sum(full[n][0], kind, r, chip, "psum_" + n)
    small = dict(norm_w=dnw, b_gate=dbg, hgrn_norm_w=dhw, gdn_norm_w=dgw, lbs=dlbs, conv_a=dca[0:3], conv_c=dcc[0:4],
                 a_log=dsmall[0:1, 8:16], dt_bias=dsmall[1:2, 8:16])
    return dx, small, half


def lbs_fwd(lb):
    def body(lb_ref, o_ref):
        l0, l1 = lb_ref[0:1, :], lb_ref[1:2, :]
        mx = jnp.maximum(l0, l1)
        e0, e1 = jnp.exp(l0 - mx), jnp.exp(l1 - mx)
        o_ref[0:1, :] = jnp.zeros_like(l0)
        o_ref[1:2, :] = e1 / (e0 + e1)
    return pl.pallas_call(body, name="lbs_fwd", out_shape=jax.ShapeDtypeStruct(lb.shape, f32))(lb)


def _adam_math(w, g, m, v):
    mn = ADAM_B1 * m + (1.0 - ADAM_B1) * g
    vn = ADAM_B2 * v + (1.0 - ADAM_B2) * (g * g)
    mh = mn / (1.0 - ADAM_B1 ** ADAM_STEP)
    vh = vn / (1.0 - ADAM_B2 ** ADAM_STEP)
    return -ADAM_LR * (mh / (jnp.sqrt(vh) + ADAM_EPS) + ADAM_WD * w), mn, vn


def adam(w, g, m, v, name):
    R, C = w.shape
    tr = 256 if R % 256 == 0 else R

    def body(w_ref, g_ref, m_ref, v_ref, d_ref, mo_ref, vo_ref):
        d, mn, vn = _adam_math(w_ref[...], g_ref[...], m_ref[...], v_ref[...])
        d_ref[...] = d
        mo_ref[...] = mn
        vo_ref[...] = vn

    spec = pl.BlockSpec((tr, C), lambda i: (i, 0))
    return pl.pallas_call(
        body, name=name, grid=(R // tr,), in_specs=[spec] * 4, out_specs=[spec] * 3,
        out_shape=[jax.ShapeDtypeStruct((R, C), f32)] * 3, compiler_params=_cp(("parallel",)),
    )(w, g, m, v)


def adam_pair(h, hs, w, m, v, name):
    _, R, C = w.shape
    cp = h[0].shape[1]
    tr = 128 if R % 128 == 0 else R
    nt = R // tr

    def body(h0_ref, h1_ref, s0_ref, s1_ref, w_ref, m_ref, v_ref, g_ref, d_ref, mo_ref, vo_ref):
        def update(h_ref, s_ref):
            g = (h_ref[...] + s_ref[...])[:, :C]
            d, mn, vn = _adam_math(w_ref[...], g, m_ref[...], v_ref[...])
            g_ref[...] = g
            d_ref[...] = d
            mo_ref[...] = mn
            vo_ref[...] = vn

        @pl.when(pl.program_id(0) == 0)
        def _():
            update(h0_ref, s0_ref)

        @pl.when(pl.program_id(0) == 1)
        def _():
            update(h1_ref, s1_ref)

    h0spec = pl.BlockSpec((tr, cp), lambda l, i: (jnp.where(l == 0, i, nt - 1), 0))
    h1spec = pl.BlockSpec((tr, cp), lambda l, i: (jnp.where(l == 1, i, 0), 0))
    spec = pl.BlockSpec((None, tr, C), lambda l, i: (l, i, 0))
    return pl.pallas_call(
        body, name=name, grid=(2, nt), in_specs=[h0spec, h1spec, h0spec, h1spec, spec, spec, spec],
        out_specs=[spec] * 4, out_shape=[jax.ShapeDtypeStruct(w.shape, f32)] * 4,
        compiler_params=_cp(("arbitrary", "arbitrary")),
    )(h[0], h[1], hs[0], hs[1], w, m, v)


_SMALL = (("norm_w", 2 * D), ("b_gate", 6 * D), ("lower_bounds", None), ("hgrn_norm_w", 2 * HK),
          ("gdn_norm_w", 2 * HK), ("a_log", 16), ("dt_bias", 16), ("final_norm_w", D), ("loss", None))
_CONV = (("conv_a", 2 * 3 * 512), ("conv_c", 2 * 4 * 2048))


def _small_rows(n):
    return 16 if n is None else -(-n // 1024) * 8


LB_ROW = sum(_small_rows(n) for _, n in _SMALL[:2])
ADAM_ROWS = sum(_small_rows(n) for _, n in _SMALL)
SMALL_ROWS = ADAM_ROWS + sum(_small_rows(n) for _, n in _CONV)


def small_update(parts, wp, mp, vp):
    def body(p_ref, w_ref, m_ref, v_ref, g_ref, d_ref, mo_ref, vo_ref):
        gs = p_ref[0]
        for i in range(1, 8):
            gs = gs + p_ref[i]
        w = w_ref[...]
        l0, l1 = w[LB_ROW:LB_ROW + 8], w[LB_ROW + 8:LB_ROW + 16]
        mx = jnp.maximum(l0, l1)
        e0, e1 = jnp.exp(l0 - mx), jnp.exp(l1 - mx)
        p0, p1 = e0 / (e0 + e1), e1 / (e0 + e1)
        dl1 = gs[LB_ROW + 8:LB_ROW + 16]
        s = p1 * dl1
        g = jnp.concatenate([gs[0:LB_ROW], -p0 * s, p1 * dl1 - p1 * s, gs[LB_ROW + 16:ADAM_ROWS]], axis=0)
        d, mn, vn = _adam_math(w, g, m_ref[...], v_ref[...])
        g_ref[0:ADAM_ROWS, :] = g
        g_ref[ADAM_ROWS:, :] = gs[ADAM_ROWS:]
        d_ref[...] = d
        mo_ref[...] = mn
        vo_ref[...] = vn
    sd = jax.ShapeDtypeStruct
    return pl.pallas_call(body, name="small_update",
                          out_shape=[sd((SMALL_ROWS, 128), f32)] + [sd((ADAM_ROWS, 128), f32)] * 3)(parts, wp, mp, vp)


def partial_sum(own, kind, recv, chip, name, transposed=False):
    _, r, c = recv.shape
    tr = 256 if r % 256 == 0 else r

    def body(chip_ref, o_ref, r_ref, out_ref):
        s = ((o_ref[...] + r_ref[0].astype(f32)) + r_ref[1].astype(f32)) + r_ref[2].astype(f32)
        out_ref[...] = s.T if transposed else s

    own_spec = {"own": pl.BlockSpec((tr, c), lambda i, chip: (i, 0)),
                "slot": pl.BlockSpec((None, tr, c), lambda i, chip: (chip[0], i, 0)),
                "cols": pl.BlockSpec((tr, c), lambda i, chip: (i, chip[0])),
                "rows": pl.BlockSpec((tr, c), lambda i, chip: (chip[0] * (r // tr) + i, 0))}[kind]
    out_spec = pl.BlockSpec((c, tr), lambda i, chip: (0, i)) if transposed else pl.BlockSpec((tr, c), lambda i, chip: (i, 0))
    return pl.pallas_call(
        body, name=name,
        grid_spec=pltpu.PrefetchScalarGridSpec(
            num_scalar_prefetch=1, grid=(r // tr,),
            in_specs=[own_spec, pl.BlockSpec((3, tr, c), lambda i, chip: (0, i, 0))], out_specs=out_spec),
        out_shape=jax.ShapeDtypeStruct((c, r) if transposed else (r, c), f32), compiler_params=_cp(("arbitrary",)),
    )(chip, own, recv)


def adam_pair_t(h, hs, wt, mt, vt, name):
    C, _, R = wt.shape
    tc = 128

    def body(h0_ref, h1_ref, s0_ref, s1_ref, w_ref, m_ref, v_ref, g_ref, d_ref, mo_ref, vo_ref):
        g = jnp.stack([h0_ref[...] + s0_ref[...], h1_ref[...] + s1_ref[...]], axis=1)
        d, mn, vn = _adam_math(w_ref[...], g, m_ref[...], v_ref[...])
        g_ref[...] = g
        d_ref[...] = d
        mo_ref[...] = mn
        vo_ref[...] = vn

    hspec = pl.BlockSpec((tc, R), lambda i: (i, 0))
    spec = pl.BlockSpec((tc, 2, R), lambda i: (i, 0, 0))
    return pl.pallas_call(
        body, name=name, grid=(pl.cdiv(C, tc),), in_specs=[hspec] * 4 + [spec] * 3, out_specs=[spec] * 4,
        out_shape=[jax.ShapeDtypeStruct(wt.shape, f32)] * 4, compiler_params=_cp(("parallel",)),
    )(h[0], h[1], hs[0], hs[1], wt, mt, vt)


MESH = pl.DeviceIdType.MESH
_HBM = pl.BlockSpec(memory_space=pltpu.HBM)


def _place():
    return lax.axis_index("x"), lax.axis_index("y"), lax.axis_index("c")


def weight_gather(arrs):
    n = len(arrs)

    def body(*refs):
        x_refs, out_refs = refs[:n], refs[n:2 * n]
        send_sems, recv_sems, local_sems = refs[2 * n:]
        x, y, c = _place()
        me, sibling = (x, y, c), (x, y, 1 - c)
        chips = [(1 - x, y), (x, 1 - y), (1 - x, 1 - y)]

        def copy(a, k, block, to, own_src=False):
            px, py, pc = block
            dst = out_refs[a].at[2 * px + py, pc]
            return pltpu.make_async_remote_copy(
                src_ref=x_refs[a].at[c] if own_src else dst, dst_ref=dst,
                send_sem=send_sems.at[7 * a + k], recv_sem=recv_sems.at[7 * a + k], device_id=to, device_id_type=MESH)

        mine = [pltpu.make_async_copy(x_refs[a].at[c], out_refs[a].at[2 * x + y, c], local_sems.at[a])
                for a in range(n)]
        for cp in mine:
            cp.start()
        first = []
        for a in range(n):
            first.append(copy(a, 0, me, sibling, own_src=True))
            first += [copy(a, 1 + j, me, (*chip, c), own_src=True) for j, chip in enumerate(chips)]
        for cp in first:
            cp.start()
        passed = []
        for j, chip in enumerate(chips):
            for a in range(n):
                copy(a, 1 + j, (*chip, c), me).wait_recv()
                fwd = copy(a, 4 + j, (*chip, c), sibling)
                fwd.start()
                passed.append(fwd)
        for a in range(n):
            copy(a, 0, sibling, me).wait_recv()
            for j, chip in enumerate(chips):
                copy(a, 4 + j, (*chip, 1 - c), me).wait_recv()
        for cp in first + passed:
            cp.wait_send()
        for cp in mine:
            cp.wait()

    return pl.pallas_call(
        body, name="weight_gather", in_specs=[_HBM] * n, out_specs=[_HBM] * n,
        out_shape=[jax.ShapeDtypeStruct((N_CHIPS,) + a.shape, a.dtype) for a in arrs],
        scratch_shapes=[pltpu.SemaphoreType.DMA((7 * n,)), pltpu.SemaphoreType.DMA((7 * n,)),
                        pltpu.SemaphoreType.DMA((n,))],
    )(*arrs)


SHARD_W = 256


_SEM = pl.BlockSpec(memory_space=pltpu.SEMAPHORE)
_EFFECT = pltpu.SideEffectType.DATAFLOW_SIDE_EFFECTING


def _landing_shape(a, kind):
    if kind == "sib":
        return a.shape
    if kind == "all":
        return (N_CHIPS,) + a.shape
    if kind == "slot":
        return (3,) + a.shape[1:]
    return (3,) + ((a.shape[0], SHARD_W) if kind == "cols" else (SHARD_W, a.shape[1]))


def _shard_copies(src_refs, land_refs, kinds, send_sems, recv_sems):
    x, y, c = _place()
    copies = []
    for a, (src, land, kind) in enumerate(zip(src_refs, land_refs, kinds)):
        if kind == "sib":
            copies.append(pltpu.make_async_remote_copy(
                src_ref=src, dst_ref=land, send_sem=send_sems.at[3 * a], recv_sem=recv_sems.at[3 * a],
                device_id=(x, y, 1 - c), device_id_type=MESH))
            continue
        for j, (px, py) in enumerate(((1 - x, y), (x, 1 - y), (1 - x, 1 - y))):
            q = 2 * px + py
            lo = pl.multiple_of(q * SHARD_W, SHARD_W)
            part = {"slot": lambda: src.at[q], "cols": lambda: src.at[:, pl.ds(lo, SHARD_W)],
                    "rows": lambda: src.at[pl.ds(lo, SHARD_W), :], "all": lambda: src}[kind]()
            k = 3 * a + j
            copies.append(pltpu.make_async_remote_copy(
                src_ref=part, dst_ref=land.at[2 * x + y] if kind == "all" else land.at[j],
                send_sem=send_sems.at[k], recv_sem=recv_sems.at[k], device_id=(px, py, c), device_id_type=MESH))
    return copies


def exchange_start(srcs, kinds, name, after=None):
    n = len(srcs)
    lands = [lax.empty(_landing_shape(a, k), a.dtype) for a, k in zip(srcs, kinds)]
    extra = [] if after is None else [after]

    def body(*refs):
        src_refs, land_refs, token = refs[:n], refs[n:2 * n], refs[-1]
        send_sems, recv_sems = refs[2 * n + len(extra)], refs[2 * n + len(extra) + 1]
        for cp in _shard_copies(src_refs, land_refs, kinds, send_sems, recv_sems):
            cp.start()
        token[...] = jnp.zeros_like(token)

    both = list(srcs) + lands
    out = pl.pallas_call(
        body, name=name,
        out_shape=(pltpu.SemaphoreType.DMA((3 * n,)), pltpu.SemaphoreType.DMA((3 * n,)),
                   *[pltpu.HBM(a.shape, a.dtype) for a in both], jax.ShapeDtypeStruct((8, 128), f32)),
        in_specs=[_HBM] * (2 * n) + [pl.BlockSpec(memory_space=pl.ANY)] * len(extra),
        out_specs=(_SEM, _SEM, *[_HBM] * (2 * n), pl.BlockSpec(memory_space=pltpu.VMEM)),
        input_output_aliases={i: 2 + i for i in range(2 * n)},
        compiler_params=pltpu.CompilerParams(has_side_effects=_EFFECT),
    )(*[pltpu.with_memory_space_constraint(a, pltpu.HBM) for a in both], *extra)
    return (out[0], out[1], out[2:2 + 2 * n]), out[-1]


def exchange_wait(handle, kinds, after, name, with_srcs=False):
    send_sems, recv_sems, both = handle
    n = len(kinds)
    after = after if isinstance(after, tuple) else (after,)

    def body(*refs):
        src_refs, land_refs, s_sems, r_sems = refs[:n], refs[n:2 * n], refs[2 * n], refs[2 * n + 1]
        for cp in _shard_copies(src_refs, land_refs, kinds, s_sems, r_sems):
            cp.wait_send()
            cp.wait_recv()

    out = pl.pallas_call(
        body, name=name, out_shape=tuple(pltpu.HBM(a.shape, a.dtype) for a in both),
        in_specs=[_HBM] * (2 * n) + [_SEM, _SEM] + [pl.BlockSpec(memory_space=pl.ANY)] * len(after),
        out_specs=tuple([_HBM] * (2 * n)), input_output_aliases={i: i for i in range(2 * n)},
        compiler_params=pltpu.CompilerParams(has_side_effects=_EFFECT),
    )(*both, send_sems, recv_sems, *after)
    return (out[:n], out[n:]) if with_srcs else out[n:]


def final_exchange(hs, small):
    n = len(hs)
    S = small.shape[0]

    def body(*refs):
        h_refs, sm_ref, out_refs, smalls_ref = refs[:n], refs[n], refs[n + 1:2 * n + 1], refs[2 * n + 1]
        send_sems, recv_sems, local_sem = refs[2 * n + 2:]
        x, y, c = _place()
        my_slot = smalls_ref.at[4 * x + 2 * y + c]
        mine = pltpu.make_async_copy(sm_ref, my_slot, local_sem)
        mine.start()
        copies = [pltpu.make_async_remote_copy(src_ref=h_refs[a], dst_ref=out_refs[a], send_sem=send_sems.at[a],
                                               recv_sem=recv_sems.at[a], device_id=(x, y, 1 - c), device_id_type=MESH)
                  for a in range(n)]
        for mask in range(1, 8):
            fx, fy, fc = (mask >> 2) & 1, (mask >> 1) & 1, mask & 1
            peer = ((1 - x) if fx else x, (1 - y) if fy else y, (1 - c) if fc else c)
            copies.append(pltpu.make_async_remote_copy(
                src_ref=sm_ref, dst_ref=my_slot, send_sem=send_sems.at[n - 1 + mask], recv_sem=recv_sems.at[n - 1 + mask],
                device_id=peer, device_id_type=MESH))
        for cp in copies:
            cp.start()
        for cp in copies:
            cp.wait_recv()
        for cp in copies:
            cp.wait_send()
        mine.wait()

    sd = jax.ShapeDtypeStruct
    out = pl.pallas_call(
        body, name="final_exchange", in_specs=[_HBM] * (n + 1), out_specs=[_HBM] * (n + 1),
        out_shape=[sd(h.shape, h.dtype) for h in hs] + [sd((8, S, 128), f32)],
        scratch_shapes=[pltpu.SemaphoreType.DMA((n + 7,)), pltpu.SemaphoreType.DMA((n + 7,)),
                        pltpu.SemaphoreType.DMA],
    )(*hs, small)
    return out[:n], out[n]


N_CHIPS = 4
SHARD_COLS = N_ORIG // N_CHIPS


SHARD_PAD = 2688
_COL_SEGMENTS = (
    ((0, 2048, OFF_A),)
    + tuple((2048 + 512 * j + HK * h, 2048 + 512 * j + HK * (h + 1), OFF_B + REG_BH * h + HK * j)
            for j in range(3) for h in range(N_HGRN))
    + ((3584, 4096, OFF_M + M_BZ), (4096, 6144, OFF_C), (6144, 6160, OFF_C + 2048), (6160, 7184, OFF_M + M_CZ),
       (7184, N_ORIG, OFF_M + M_G)))


def _shard_pieces():
    pieces = []
    for lo, hi, dst in _COL_SEGMENTS:
        for p in range(N_CHIPS):
            a, b = max(lo, p * SHARD_COLS), min(hi, (p + 1) * SHARD_COLS)
            if a < b:
                pieces.append((p, a - p * SHARD_COLS, dst + a - lo, b - a))
    return pieces


def win_cast_pad(wt):
    tc = 128

    def body(x_ref, o0_ref, o1_ref):
        col = pl.program_id(0) * tc + lax.broadcasted_iota(jnp.int32, (tc, 1), 0)
        for l, o_ref in enumerate((o0_ref, o1_ref)):
            o_ref[...] = _b(jnp.where(col < SHARD_COLS, x_ref[:, l, :], 0.0).T)

    spec = pl.BlockSpec((D, tc), lambda i: (0, i))
    return pl.pallas_call(
        body, name="win_cast_pad", grid=(SHARD_PAD // tc,),
        in_specs=[pl.BlockSpec((tc, 2, D), lambda i: (i, 0, 0))], out_specs=[spec, spec],
        out_shape=[jax.ShapeDtypeStruct((D, SHARD_PAD), bf16)] * 2, compiler_params=_cp(("parallel",)),
    )(wt)


def win_to_padded(w4, name):
    tr = 256
    pieces = _shard_pieces()

    def body(a_ref, o_ref):
        o_ref[...] = jnp.zeros((tr, NP), bf16)
        for p, j0, c0, n in pieces:
            o_ref[:, c0:c0 + n] = a_ref[p, :, j0:j0 + n]

    return pl.pallas_call(
        body, name=name, grid=(D // tr,),
        in_specs=[pl.BlockSpec((N_CHIPS, tr, SHARD_PAD), lambda i: (0, i, 0))],
        out_specs=pl.BlockSpec((tr, NP), lambda i: (i, 0)),
        out_shape=jax.ShapeDtypeStruct((D, NP), bf16), compiler_params=_cp(("parallel",)),
    )(w4)


def win_from_padded(dw, chip):
    tr = 128
    pieces = _shard_pieces()

    def body(chip_ref, d_ref, of_ref, ob_ref):
        of_ref[:, SHARD_COLS:] = jnp.zeros((tr, SHARD_PAD - SHARD_COLS), f32)
        for p in range(N_CHIPS):
            ob_ref[p, :, SHARD_COLS:] = jnp.zeros((tr, SHARD_PAD - SHARD_COLS), bf16)
        for p, j0, c0, n in pieces:
            v = d_ref[:, c0:c0 + n]
            ob_ref[p, :, j0:j0 + n] = _b(v)

            @pl.when(chip_ref[0] == p)
            def _(v=v, j0=j0, n=n):
                of_ref[:, j0:j0 + n] = v

    return pl.pallas_call(
        body, name="win_from_padded",
        grid_spec=pltpu.PrefetchScalarGridSpec(
            num_scalar_prefetch=1, grid=(D // tr,),
            in_specs=[pl.BlockSpec((tr, NP), lambda i, chip: (i, 0))],
            out_specs=[pl.BlockSpec((tr, SHARD_PAD), lambda i, chip: (i, 0)),
                       pl.BlockSpec((N_CHIPS, tr, SHARD_PAD), lambda i, chip: (0, i, 0))]),
        out_shape=[jax.ShapeDtypeStruct((D, SHARD_PAD), f32),
                   jax.ShapeDtypeStruct((N_CHIPS, D, SHARD_PAD), bf16)],
        compiler_params=_cp(("arbitrary",)),
    )(chip, dw)


def _rows128(a):
    flat = a.reshape(-1)
    total = -(-flat.shape[0] // 1024) * 1024
    return jnp.pad(flat, (0, total - flat.shape[0])).reshape(total // 128, 128)


def _lb_rows(lb):
    return jnp.pad(lb.reshape(2, 4, 128), ((0, 0), (0, 4), (0, 0))).reshape(16, 128)


def _pack_small(v, with_conv):
    rows = []
    for name, n in _SMALL + (_CONV if with_conv else ()):
        if name == "lower_bounds":
            rows.append(_lb_rows(v[name]))
        elif name == "loss":
            rows.append(jnp.broadcast_to(v[name], (16, 128)) if name in v else jnp.zeros((16, 128), f32))
        else:
            rows.append(_rows128(v[name]))
    return jnp.concatenate(rows, axis=0)


def _unpack_small(p, shapes, with_conv):
    out, row = {}, 0
    for name, n in _SMALL + (_CONV if with_conv else ()):
        nrows = _small_rows(n)
        blk = p[row:row + nrows]
        if name == "lower_bounds":
            out[name] = blk.reshape(2, 8, 128)[:, :4].reshape(2, 512)
        elif name == "loss":
            out[name] = blk[0, 0]
        else:
            out[name] = blk.reshape(-1)[:n].reshape(shapes[name])
        row += nrows
    return out


def _lane_vec(a8):
    return jnp.pad(a8.reshape(1, 8), ((0, 0), (8, 112)))


WEIGHT_NAMES = ("norm_w", "w_in", "b_gate", "conv_a", "conv_c", "a_log", "dt_bias", "lower_bounds", "hgrn_norm_w",
                "gdn_norm_w", "w_out_a", "w_out_b", "w_out_c", "w_o", "final_norm_w")


def kernel(x, norm_w, w_in, b_gate, conv_a, conv_c, a_log, dt_bias, lower_bounds, hgrn_norm_w, gdn_norm_w, w_out_a, w_out_b, w_out_c, w_o, final_norm_w, loss_target, m_norm_w, m_w_in, m_b_gate, m_conv_a, m_conv_c, m_a_log, m_dt_bias, m_lower_bounds, m_hgrn_norm_w, m_gdn_norm_w, m_w_out_a, m_w_out_b, m_w_out_c, m_w_o, m_final_norm_w, v_norm_w, v_w_in, v_b_gate, v_conv_a, v_conv_c, v_a_log, v_dt_bias, v_lower_bounds, v_hgrn_norm_w, v_gdn_norm_w, v_w_out_a, v_w_out_b, v_w_out_c, v_w_o, v_final_norm_w):
    wts = dict(norm_w=norm_w, w_in=w_in, b_gate=b_gate, conv_a=conv_a, conv_c=conv_c, a_log=a_log, dt_bias=dt_bias,
               lower_bounds=lower_bounds, hgrn_norm_w=hgrn_norm_w, gdn_norm_w=gdn_norm_w, w_out_a=w_out_a,
               w_out_b=w_out_b, w_out_c=w_out_c, w_o=w_o, final_norm_w=final_norm_w)
    mom = dict(norm_w=m_norm_w, w_in=m_w_in, b_gate=m_b_gate, conv_a=m_conv_a, conv_c=m_conv_c, a_log=m_a_log,
               dt_bias=m_dt_bias, lower_bounds=m_lower_bounds, hgrn_norm_w=m_hgrn_norm_w, gdn_norm_w=m_gdn_norm_w,
               w_out_a=m_w_out_a, w_out_b=m_w_out_b, w_out_c=m_w_out_c, w_o=m_w_o, final_norm_w=m_final_norm_w)
    var = dict(norm_w=v_norm_w, w_in=v_w_in, b_gate=v_b_gate, conv_a=v_conv_a, conv_c=v_conv_c, a_log=v_a_log,
               dt_bias=v_dt_bias, lower_bounds=v_lower_bounds, hgrn_norm_w=v_hgrn_norm_w, gdn_norm_w=v_gdn_norm_w,
               w_out_a=v_w_out_a, w_out_b=v_w_out_b, w_out_c=v_w_out_c, w_o=v_w_o, final_norm_w=v_final_norm_w)
    chip = 2 * lax.axis_index("x") + lax.axis_index("y")
    chip1 = chip.reshape(1).astype(jnp.int32)

    win_l0, win_l1 = win_cast_pad(jnp.transpose(w_in, (2, 0, 1)))
    win4_l0, ca4, cc4 = weight_gather([win_l0.reshape(2, D // 2, SHARD_PAD), conv_a, conv_c])
    by_cols = lambda a: a.transpose(1, 2, 0, 3).reshape(a.shape[1], a.shape[2], N_CHIPS * a.shape[3])
    by_rows = lambda a: a.transpose(1, 0, 2, 3).reshape(a.shape[1], N_CHIPS * a.shape[2], a.shape[3])
    conv_a_full, conv_c_full = by_cols(ca4), by_cols(cc4)
    later = [win_l1, _b(w_out_a), _b(w_out_b), _b(w_out_c), _b(w_o)]
    sent_w, token = exchange_start(later, ["all"] * 5, "weights_start", after=win4_l0)

    def late_weights(after):
        lands = exchange_wait(sent_w, ["all"] * 5, after, "weights_wait")
        l1, woa4, wob4, woc4, wo4 = (lax.dynamic_update_index_in_dim(land, own, chip, 0)
                                     for land, own in zip(lands, later))
        outs = dict(w_out_a=by_cols(woa4), w_out_b=by_cols(wob4), w_out_c=by_rows(woc4), w_o=by_rows(wo4))
        layers[1].update(outs, w_in=win_to_padded(l1, "win_to_padded1"))
        return outs

    lbs = lbs_fwd(lower_bounds)
    layers = []
    for l in range(2):
        layers.append(dict(
            l=l, norm_w=norm_w[l:l + 1], b_gate=b_gate[l:l + 1], conv_a=conv_a_full[l], conv_c=conv_c_full[l],
            alog_l=_lane_vec(a_log[l]), dtb_l=_lane_vec(dt_bias[l]), lbs=lbs[l:l + 1],
            hgrn_norm_w=hgrn_norm_w[l:l + 1], gdn_norm_w=gdn_norm_w[l:l + 1]))
    layers[0].update(w_in=win_to_padded(win4_l0.reshape(N_CHIPS, D, SHARD_PAD), "win_to_padded0"), late=late_weights,
                     norm_w=norm_w[0:1] + token[0:1, 0:1])

    xs, saved = x[0], []
    for l in range(2):
        xs, s = layer_fwd(xs, layers[l])
        saved.append(s)
    loss_row, dx, dfw = loss_head(xs, final_norm_w.reshape(1, D), loss_target[0])
    lg, half = [None, None], [None, None]
    mat_names = ("w_in",) + tuple(n for n, _ in OUT_MATS)
    dx, lg[1], half[1] = layer_bwd(dx, layers[1], saved[1], chip1)
    mine1 = [half[1][n] for n in mat_names]
    sent_sib, token = exchange_start(mine1, ["sib"] * len(mine1), "sibling_start")
    layers[0]["hgrn_norm_w"] = layers[0]["hgrn_norm_w"] + token[0:1, 0:1]
    dx, lg[0], half[0] = layer_bwd(dx, layers[0], saved[0], chip1)
    mine1, theirs1 = exchange_wait(sent_sib, ["sib"] * len(mine1), dx, "sibling_wait", with_srcs=True)
    grad_x = dx[None]

    stack = lambda n: jnp.stack([lg[0][n], lg[1][n]], axis=0)
    gsmall = {n: stack(n) for n in ("norm_w", "b_gate", "hgrn_norm_w", "gdn_norm_w", "a_log", "dt_bias", "conv_a",
                                    "conv_c")}
    gsmall.update(lower_bounds=stack("lbs"), final_norm_w=dfw, loss=loss_row)
    mine0 = [half[0][n] for n in mat_names]
    theirs0, smalls = final_exchange(mine0, _pack_small(gsmall, True))

    out_g, out_d, out_m, out_v = {}, {}, {}, {}
    for i, n in enumerate(mat_names):
        h, hs = [mine0[i], mine1[i]], [theirs0[i], theirs1[i]]
        if n == "w_in":
            fwd, back = (lambda a: jnp.transpose(a, (2, 0, 1))), (lambda a: jnp.transpose(a, (1, 2, 0)))
            res = adam_pair_t(h, hs, fwd(wts[n]), fwd(mom[n]), fwd(var[n]), "adam_" + n)
            out_g[n], out_d[n], out_m[n], out_v[n] = (back(a) for a in res)
        else:
            out_g[n], out_d[n], out_m[n], out_v[n] = adam_pair(h, hs, wts[n], mom[n], var[n], "adam_" + n)
    small_names = [n for n, _ in _SMALL if n != "loss"]
    pack = lambda v: _pack_small({n: v[n] for n in small_names}, False)
    sg, sd, smn, svn = small_update(smalls, pack(wts), pack(mom), pack(var))
    shapes = {n: wts[n].shape for n in small_names}
    shapes.update(conv_a=(2, 3, 512), conv_c=(2, 4, 2048))
    for dst, src, conv in ((out_g, sg, True), (out_d, sd, False), (out_m, smn, False), (out_v, svn, False)):
        dst.update(_unpack_small(src, shapes, conv))
    loss = out_g.pop("loss")
    for n in ("conv_a", "conv_c"):
        width = wts[n].shape[2]
        g = lax.dynamic_slice_in_dim(out_g[n], chip * width, width, axis=2)
        two_d = lambda a: a.reshape(-1, width)
        d, mn, vn = adam(two_d(wts[n]), two_d(g), two_d(mom[n]), two_d(var[n]), "adam_" + n)
        out_g[n] = g
        out_d[n], out_m[n], out_v[n] = (a.reshape(wts[n].shape) for a in (d, mn, vn))
    return (loss, grad_x, *[out_g[n] for n in WEIGHT_NAMES], *[out_d[n] for n in WEIGHT_NAMES],
            *[out_m[n] for n in WEIGHT_NAMES], *[out_v[n] for n in WEIGHT_NAMES])
```

```python
import jax
import jax.numpy as jnp
from jax import lax
from jax.experimental import pallas as pl
from jax.experimental.pallas import tpu as pltpu

f32 = jnp.float32
bf16 = jnp.bfloat16

D = 1024
L = 64
SUB = 16
NORM_EPS = 1e-6
L2_EPS = 1e-6
MIN_F = 1e-30
HK = 128
QK_SCALE = HK ** -0.5
N_GDN = 8
GDN_BLOCK = 1024
N_HGRN = 4

REG_A = 2304
REG_C = 2304
REG_M = 4608
REG_BH = 384
OFF_A, OFF_C, OFF_M, OFF_B = 0, 2304, 4608, 9216
M_BZ, M_G, M_CZ = 0, 512, 3584
NP = 10752
NP_TILE = 1536
N_ORIG = 10256

ADAM_LR, ADAM_B1, ADAM_B2, ADAM_EPS, ADAM_WD, ADAM_STEP = 0.001, 0.9, 0.999, 1e-08, 0.01, 10

VMEM_LIMIT = 56 * 1024 * 1024


def _cp(sem):
    return pltpu.CompilerParams(dimension_semantics=sem, vmem_limit_bytes=VMEM_LIMIT)


def _sigmoid(x):
    return jax.nn.sigmoid(x)


def _silu(x):
    return x * _sigmoid(x)


def _silu2(x):
    s = _sigmoid(x)
    y = x * s
    return y, s + y * (1.0 - s)


def _softplus(x):
    u = jnp.exp(-jnp.abs(x))
    w = 1.0 + u
    l1p = jnp.where(w == 1.0, u, jnp.log(w) * (u / (w - 1.0)))
    return jnp.maximum(x, 0.0) + l1p


def _dot(a, b):
    return jnp.dot(a, b, preferred_element_type=f32)


def _dot_nt(a, b):
    return lax.dot_general(a, b, (((1,), (1,)), ((), ())), preferred_element_type=f32)


def _dot_tn(a, b):
    return lax.dot_general(a, b, (((0,), (0,)), ((), ())), preferred_element_type=f32)


def _bdot(a, b):
    return lax.dot_general(a, b, (((2,), (1,)), ((0,), (0,))), preferred_element_type=f32)


def _bdot_nt(a, b):
    return lax.dot_general(a, b, (((2,), (2,)), ((0,), (0,))), preferred_element_type=f32)


def _bdot_tn(a, b):
    return lax.dot_general(a, b, (((1,), (1,)), ((0,), (0,))), preferred_element_type=f32)


def _bdot_split(a, b):
    ah, bh = _b(a), _b(b)
    al, bl = _b(a - ah.astype(f32)), _b(b - bh.astype(f32))
    return _bdot(ah, bh) + (_bdot(ah, bl) + _bdot(al, bh))


def _b(x):
    return x.astype(bf16)


def _chunk_cumsum(x, rows_in_chunk):
    n = x.shape[0]
    for s in (1, 2, 4, 8, 16, 32):
        x = x + jnp.where(rows_in_chunk >= s, pltpu.roll(x, s, axis=0), 0.0)
    return x


def _chunk_rev_cumsum(x, rows_in_chunk):
    n = x.shape[0]
    for s in (1, 2, 4, 8, 16, 32):
        x = x + jnp.where(rows_in_chunk + s < L, pltpu.roll(x, n - s, axis=0), 0.0)
    return x


def _shift_down(x, s):
    return pltpu.roll(x, s, axis=0) if s else x


def _shift_up(x, s):
    return pltpu.roll(x, x.shape[0] - s, axis=0) if s else x


def inproj_fwd(x, nw, w):
    T = x.shape[0]
    tT, tn = min(2048, T), NP_TILE

    def body(x_ref, nw_ref, w_ref, p_ref, h_ref, hs):
        @pl.when(pl.program_id(1) == 0)
        def _():
            xv = x_ref[...]
            r = lax.rsqrt(jnp.mean(xv * xv, axis=-1, keepdims=True) + NORM_EPS)
            hv = _b(xv * r * nw_ref[...])
            hs[...] = hv
            h_ref[...] = hv
        p_ref[...] = _b(_dot(hs[...], w_ref[...]))

    return pl.pallas_call(
        body, name="inproj_fwd", grid=(T // tT, NP // tn),
        in_specs=[pl.BlockSpec((tT, D), lambda i, j: (i, 0)), pl.BlockSpec((1, D), lambda i, j: (0, 0)),
                  pl.BlockSpec((D, tn), lambda i, j: (0, j))],
        out_specs=[pl.BlockSpec((tT, tn), lambda i, j: (i, j)), pl.BlockSpec((tT, D), lambda i, j: (i, 0))],
        out_shape=[jax.ShapeDtypeStruct((T, NP), bf16), jax.ShapeDtypeStruct((T, D), bf16)],
        scratch_shapes=[pltpu.VMEM((tT, D), bf16)],
        compiler_params=_cp(("parallel", "arbitrary")),
    )(x, nw, w)


def matmul_tn(a, b, name, n=None, b_col0=0, with_bf16=False):
    T, K = a.shape
    N = b.shape[1] if n is None else n
    tT = min(2048, T)
    tn = NP_TILE if N % NP_TILE == 0 else min(N, 1024)
    nt = T // tT
    cb0 = b_col0 // tn

    def body(a_ref, b_ref, o_ref, *ob_ref):
        @pl.when(pl.program_id(1) == 0)
        def _():
            o_ref[...] = jnp.zeros_like(o_ref)
        o_ref[...] += _dot_tn(_b(a_ref[...]), _b(b_ref[...]))
        if with_bf16:
            @pl.when(pl.program_id(1) == nt - 1)
            def _():
                ob_ref[0][...] = _b(o_ref[...])

    ospec = pl.BlockSpec((K, tn), lambda j, t: (0, j))
    return pl.pallas_call(
        body, name=name, grid=(N // tn, nt),
        in_specs=[pl.BlockSpec((tT, K), lambda j, t: (t, 0)), pl.BlockSpec((tT, tn), lambda j, t: (t, cb0 + j))],
        out_specs=[ospec, ospec] if with_bf16 else ospec,
        out_shape=([jax.ShapeDtypeStruct((K, N), f32), jax.ShapeDtypeStruct((K, N), bf16)] if with_bf16
                   else jax.ShapeDtypeStruct((K, N), f32)),
        compiler_params=_cp(("parallel", "arbitrary")),
    )(a, b)


def inproj_bwd(dp, w, x, nw, dres):
    T = x.shape[0]
    tT, tk = min(1024, T), NP // 4
    nk = NP // tk

    def body(dp_ref, w_ref, x_ref, nw_ref, dres_ref, dx_ref, dnw_ref, acc):
        i, k = pl.program_id(0), pl.program_id(1)

        @pl.when((i == 0) & (k == 0))
        def _():
            dnw_ref[...] = jnp.zeros_like(dnw_ref)

        @pl.when(k == 0)
        def _():
            acc[...] = jnp.zeros_like(acc)
        acc[...] += _dot_nt(dp_ref[...], w_ref[...])

        @pl.when(k == nk - 1)
        def _():
            xv = x_ref[...]
            r = lax.rsqrt(jnp.mean(xv * xv, axis=-1, keepdims=True) + NORM_EPS)
            xh = xv * r
            dy = acc[...]
            dyw = dy * nw_ref[...]
            dx_ref[...] = r * (dyw - xh * jnp.mean(dyw * xh, axis=-1, keepdims=True)) + dres_ref[...]
            dnw_ref[...] += jnp.sum(dy * xh, axis=0, keepdims=True)

    return pl.pallas_call(
        body, name="inproj_bwd", grid=(T // tT, nk),
        in_specs=[pl.BlockSpec((tT, tk), lambda i, k: (i, k)), pl.BlockSpec((D, tk), lambda i, k: (0, k)),
                  pl.BlockSpec((tT, D), lambda i, k: (i, 0)), pl.BlockSpec((1, D), lambda i, k: (0, 0)),
                  pl.BlockSpec((tT, D), lambda i, k: (i, 0))],
        out_specs=[pl.BlockSpec((tT, D), lambda i, k: (i, 0)), pl.BlockSpec((1, D), lambda i, k: (0, 0))],
        out_shape=[jax.ShapeDtypeStruct((T, D), f32), jax.ShapeDtypeStruct((1, D), f32)],
        scratch_shapes=[pltpu.VMEM((tT, D), f32)],
        compiler_params=_cp(("arbitrary", "arbitrary")),
    )(dp, w, x, nw, dres)


def loss_head(x, fw, tgt):
    T = x.shape[0]
    tT = min(1024, T)

    def body(x_ref, fw_ref, t_ref, loss_ref, dx_ref, dfw_ref):
        @pl.when(pl.program_id(0) == 0)
        def _():
            loss_ref[...] = jnp.zeros_like(loss_ref)
            dfw_ref[...] = jnp.zeros_like(dfw_ref)
        xv = x_ref[...]
        r = lax.rsqrt(jnp.mean(xv * xv, axis=-1, keepdims=True) + NORM_EPS)
        xh = xv * r
        err = xh * fw_ref[...] - t_ref[...]
        part = 0.5 * jnp.sum(jnp.mean(err * err, axis=-1, keepdims=True), axis=0, keepdims=True)
        loss_ref[...] += jnp.broadcast_to(part, loss_ref.shape)
        dy = err * (1.0 / D)
        dyw = dy * fw_ref[...]
        dx_ref[...] = r * (dyw - xh * jnp.mean(dyw * xh, axis=-1, keepdims=True))
        dfw_ref[...] += jnp.sum(dy * xh, axis=0, keepdims=True)

    return pl.pallas_call(
        body, name="loss_head", grid=(T // tT,),
        in_specs=[pl.BlockSpec((tT, D), lambda i: (i, 0)), pl.BlockSpec((1, D), lambda i: (0, 0)),
                  pl.BlockSpec((tT, D), lambda i: (i, 0))],
        out_specs=[pl.BlockSpec((1, 128), lambda i: (0, 0)), pl.BlockSpec((tT, D), lambda i: (i, 0)),
                   pl.BlockSpec((1, D), lambda i: (0, 0))],
        out_shape=[jax.ShapeDtypeStruct((1, 128), f32), jax.ShapeDtypeStruct((T, D), f32),
                   jax.ShapeDtypeStruct((1, D), f32)],
        compiler_params=_cp(("arbitrary",)),
    )(x, fw, tgt)


def _halo_specs(tT, T, width, colblk, rows=8):
    nb = T // rows
    per = tT // rows
    prev = pl.BlockSpec((rows, width), lambda i: (jnp.maximum(i * per - 1, 0), colblk))
    nxt = pl.BlockSpec((rows, width), lambda i: (jnp.minimum((i + 1) * per, nb - 1), colblk))
    return prev, nxt


def _p_rows(p_ref, width=2048):
    return p_ref[:, 0:width].astype(f32)


def _p_prev(pp_ref, width=2048):
    return pp_ref[:, 0:width].astype(f32)[8:16]


def _p_next(pn_ref, width=2048):
    return pn_ref[:, 0:width].astype(f32)[0:8]


def mixa_fwd(p, cw):
    T = p.shape[0]
    tT = min(1024, T)
    prev_spec, _ = _halo_specs(tT, T, REG_A, OFF_A // REG_A, rows=16)

    def body(p_ref, pp_ref, cw_ref, y_ref):
        pv = _p_rows(p_ref)
        u = pv[:, 512:1024] * pv[:, 1024:1536]
        pp = _p_prev(pp_ref)
        up = jnp.where(pl.program_id(0) == 0, 0.0, pp[:, 512:1024] * pp[:, 1024:1536])
        ue = jnp.concatenate([up, u], axis=0)
        cv = cw_ref[0:1, :] * _shift_down(ue, 2) + cw_ref[1:2, :] * _shift_down(ue, 1) + cw_ref[2:3, :] * ue
        y_ref[...] = _b(pv[:, 0:512] * cv[8:] * _silu(pv[:, 1536:2048]))

    return pl.pallas_call(
        body, name="mixa_fwd", grid=(T // tT,),
        in_specs=[pl.BlockSpec((tT, REG_A), lambda i: (i, OFF_A // REG_A)), prev_spec,
                  pl.BlockSpec((3, 512), lambda i: (0, 0))],
        out_specs=pl.BlockSpec((tT, 512), lambda i: (i, 0)),
        out_shape=jax.ShapeDtypeStruct((T, 512), bf16),
        compiler_params=_cp(("parallel",)),
    )(p, p, cw)


def mixa_bwd(p, cw, dy, dp):
    T = p.shape[0]
    tT = min(1024, T)
    nt = T // tT
    prev_spec, next_spec = _halo_specs(tT, T, REG_A, OFF_A // REG_A, rows=16)
    _, dnext_spec = _halo_specs(tT, T, 512, 0)

    def body(p_ref, pp_ref, pn_ref, cw_ref, dy_ref, dyn_ref, dp_in, dp_ref, dcw_ref):
        i = pl.program_id(0)

        @pl.when(i == 0)
        def _():
            dcw_ref[...] = jnp.zeros_like(dcw_ref)
        pv, pp, pn = _p_rows(p_ref), _p_prev(pp_ref), _p_next(pn_ref)
        pe = jnp.concatenate([pp, pv, pn], axis=0)
        rows = lax.broadcasted_iota(jnp.int32, (tT + 16, 1), 0)
        ab, ac, ax, az = pe[:, 0:512], pe[:, 512:1024], pe[:, 1024:1536], pe[:, 1536:2048]
        u = jnp.where((rows < 8) & (i == 0), 0.0, ac * ax)
        u1, u2 = _shift_down(u, 1), _shift_down(u, 2)
        w0, w1, w2 = cw_ref[0:1, :], cw_ref[1:2, :], cw_ref[2:3, :]
        cv = w0 * u2 + w1 * u1 + w2 * u
        dye = jnp.concatenate([jnp.zeros((8, 512), f32), dy_ref[...], dyn_ref[...]], axis=0)
        dye = jnp.where((rows >= tT + 8) & (i == nt - 1), 0.0, dye)
        sz, dsz = _silu2(az)
        dcv = dye * ab * sz
        du = w2 * dcv + w1 * _shift_up(dcv, 1) + w0 * _shift_up(dcv, 2)
        inner = (rows >= 8) & (rows < tT + 8)
        dcv_in = jnp.where(inner, dcv, 0.0)
        dcw_ref[0:1, :] += jnp.sum(dcv_in * u2, axis=0, keepdims=True)
        dcw_ref[1:2, :] += jnp.sum(dcv_in * u1, axis=0, keepdims=True)
        dcw_ref[2:3, :] += jnp.sum(dcv_in * u, axis=0, keepdims=True)
        sl = slice(8, tT + 8)
        dp_ref[:, 0:512] = _b((dye * cv * sz)[sl])
        dp_ref[:, 512:1024] = _b((du * ax)[sl])
        dp_ref[:, 1024:1536] = _b((du * ac)[sl])
        dp_ref[:, 1536:2048] = _b((dye * ab * cv * dsz)[sl])
        dp_ref[:, 2048:] = jnp.zeros((tT, REG_A - 2048), bf16)

    return pl.pallas_call(
        body, name="mixa_bwd", grid=(nt,),
        in_specs=[pl.BlockSpec((tT, REG_A), lambda i: (i, OFF_A // REG_A)), prev_spec, next_spec,
                  pl.BlockSpec((3, 512), lambda i: (0, 0)),
                  pl.BlockSpec((tT, 512), lambda i: (i, 0)), dnext_spec, pl.BlockSpec(memory_space=pl.ANY)],
        out_specs=[pl.BlockSpec((tT, REG_A), lambda i: (i, OFF_A // REG_A)), pl.BlockSpec((8, 512), lambda i: (0, 0))],
        out_shape=[jax.ShapeDtypeStruct((T, NP), bf16), jax.ShapeDtypeStruct((8, 512), f32)],
        input_output_aliases={6: 0},
        compiler_params=_cp(("arbitrary",)),
    )(p, p, p, cw, dy, dy, dp)


def _l2n_fwd(y):
    return y * lax.rsqrt(jnp.sum(y * y, axis=-1, keepdims=True) + L2_EPS)


def mixc_pre_fwd(p, cw, alog_l, dtb_l):
    T = p.shape[0]
    tT = min(1024, T)
    prev_spec, _ = _halo_specs(tT, T, REG_C, OFF_C // REG_C, rows=16)

    def body(p_ref, pp_ref, cw_ref, al_ref, dt_ref, q_ref, k_ref, v_ref, sm_ref):
        pp = jnp.where(pl.program_id(0) == 0, 0.0, _p_prev(pp_ref))
        xe = jnp.concatenate([pp, _p_rows(p_ref)], axis=0)
        cv = (cw_ref[0:1, :] * _shift_down(xe, 3) + cw_ref[1:2, :] * _shift_down(xe, 2)
              + cw_ref[2:3, :] * _shift_down(xe, 1) + cw_ref[3:4, :] * xe)[8:]
        y = _silu(cv)
        for hh in range(4):
            sl = slice(hh * HK, (hh + 1) * HK)
            q_ref[:, sl] = _l2n_fwd(y[:, sl]) * QK_SCALE
            k_ref[:, sl] = _l2n_fwd(y[:, 512 + hh * HK:512 + (hh + 1) * HK])
        v_ref[...] = y[:, 1024:2048]
        ps = p_ref[:, 2048:2176].astype(f32)
        lane = lax.broadcasted_iota(jnp.int32, ps.shape, 1)
        la = -jnp.exp(al_ref[...]) * _softplus(ps + dt_ref[...])
        rin = lax.broadcasted_iota(jnp.int32, ps.shape, 0) % L
        g = _chunk_cumsum(la, rin)
        sm_ref[...] = jnp.where(lane < 8, _sigmoid(ps), jnp.where(lane < 16, g, 0.0))

    return pl.pallas_call(
        body, name="mixc_pre_fwd", grid=(T // tT,),
        in_specs=[pl.BlockSpec((tT, REG_C), lambda i: (i, OFF_C // REG_C)), prev_spec,
                  pl.BlockSpec((4, 2048), lambda i: (0, 0)),
                  pl.BlockSpec((1, 128), lambda i: (0, 0)), pl.BlockSpec((1, 128), lambda i: (0, 0))],
        out_specs=[pl.BlockSpec((tT, 512), lambda i: (i, 0)), pl.BlockSpec((tT, 512), lambda i: (i, 0)),
                   pl.BlockSpec((tT, 1024), lambda i: (i, 0)), pl.BlockSpec((tT, 128), lambda i: (i, 0))],
        out_shape=[jax.ShapeDtypeStruct((T, 512), f32), jax.ShapeDtypeStruct((T, 512), f32),
                   jax.ShapeDtypeStruct((T, 1024), f32), jax.ShapeDtypeStruct((T, 128), f32)],
        compiler_params=_cp(("parallel",)),
    )(p, p, cw, alog_l, dtb_l)


def mixc_pre_bwd(p, cw, alog_l, dtb_l, dq8, dk8, dv, dsm8, dp):
    T = p.shape[0]
    tT = min(512, T)
    nt = T // tT
    prev_spec, next_spec = _halo_specs(tT, T, REG_C, OFF_C // REG_C, rows=16)
    _, n1024 = _halo_specs(tT, T, 1024, 0)

    def body(p_ref, pp_ref, pn_ref, cw_ref, al_ref, dt_ref, dq_ref, dqn_ref, dk_ref, dkn_ref,
             dv_ref, dvn_ref, dsm_ref, dp_in, dp_ref, dcw_ref, dsml_ref):
        i = pl.program_id(0)

        @pl.when(i == 0)
        def _():
            dcw_ref[...] = jnp.zeros_like(dcw_ref)
            dsml_ref[...] = jnp.zeros_like(dsml_ref)
        rows = lax.broadcasted_iota(jnp.int32, (tT + 16, 1), 0)
        pp = jnp.where(i == 0, 0.0, _p_prev(pp_ref))
        xe = jnp.concatenate([pp, _p_rows(p_ref), _p_next(pn_ref)], axis=0)
        xs = [_shift_down(xe, 3), _shift_down(xe, 2), _shift_down(xe, 1), xe]
        cv = cw_ref[0:1, :] * xs[0] + cw_ref[1:2, :] * xs[1] + cw_ref[2:3, :] * xs[2] + cw_ref[3:4, :] * xs[3]
        y, dy_dcv = _silu2(cv)
        last = (rows >= tT + 8) & (i == nt - 1)
        z8q = jnp.zeros((8, 1024), f32)

        def ext(cur_ref, nxt_ref):
            return jnp.where(last, 0.0, jnp.concatenate([z8q, cur_ref[...], nxt_ref[...]], axis=0))
        dq8e, dk8e, dve = ext(dq_ref, dqn_ref), ext(dk_ref, dkn_ref), ext(dv_ref, dvn_ref)
        dys = []
        for (d8, base, scale) in ((dq8e, 0, QK_SCALE), (dk8e, 512, 1.0)):
            for hh in range(4):
                dn = (d8[:, (2 * hh) * HK:(2 * hh + 1) * HK] + d8[:, (2 * hh + 1) * HK:(2 * hh + 2) * HK]) * scale
                yh = y[:, base + hh * HK:base + (hh + 1) * HK]
                r = lax.rsqrt(jnp.sum(yh * yh, axis=-1, keepdims=True) + L2_EPS)
                nh = yh * r
                dys.append(r * (dn - nh * jnp.sum(dn * nh, axis=-1, keepdims=True)))
        dyy = jnp.concatenate(dys + [dve], axis=1)
        dcv = dyy * dy_dcv
        dx = (cw_ref[3:4, :] * dcv + cw_ref[2:3, :] * _shift_up(dcv, 1) + cw_ref[1:2, :] * _shift_up(dcv, 2)
              + cw_ref[0:1, :] * _shift_up(dcv, 3))
        dp_ref[:, 0:2048] = _b(dx[8:tT + 8])
        dp_ref[:, 2176:] = jnp.zeros((tT, REG_C - 2176), bf16)
        inner = (rows >= 8) & (rows < tT + 8)
        dcv_in = jnp.where(inner, dcv, 0.0)
        for j in range(4):
            dcw_ref[j:j + 1, :] += jnp.sum(dcv_in * xs[j], axis=0, keepdims=True)
        ps = p_ref[:, 2048:2176].astype(f32)
        lane = lax.broadcasted_iota(jnp.int32, ps.shape, 1)
        dsm = dsm_ref[:, 0:128]
        for hh in range(1, N_GDN):
            dsm = dsm + dsm_ref[:, hh * 128:(hh + 1) * 128]
        beta = _sigmoid(ps)
        xa = ps + dt_ref[...]
        nea = -jnp.exp(al_ref[...])
        dpa = dsm * nea * _sigmoid(xa)
        dp_ref[:, 2048:2176] = _b(jnp.where(lane < 8, dsm * beta * (1.0 - beta), jnp.where(lane < 16, dpa, 0.0)))
        amask = (lane >= 8) & (lane < 16)
        dsml_ref[0:1, :] += jnp.sum(jnp.where(amask, dsm * nea * _softplus(xa), 0.0), axis=0, keepdims=True)
        dsml_ref[1:2, :] += jnp.sum(jnp.where(amask, dpa, 0.0), axis=0, keepdims=True)

    cur1024 = pl.BlockSpec((tT, 1024), lambda i: (i, 0))
    return pl.pallas_call(
        body, name="mixc_pre_bwd", grid=(nt,),
        in_specs=[pl.BlockSpec((tT, REG_C), lambda i: (i, OFF_C // REG_C)), prev_spec, next_spec,
                  pl.BlockSpec((4, 2048), lambda i: (0, 0)),
                  pl.BlockSpec((1, 128), lambda i: (0, 0)), pl.BlockSpec((1, 128), lambda i: (0, 0)),
                  cur1024, n1024, cur1024, n1024, cur1024, n1024, cur1024, pl.BlockSpec(memory_space=pl.ANY)],
        out_specs=[pl.BlockSpec((tT, REG_C), lambda i: (i, OFF_C // REG_C)),
                   pl.BlockSpec((8, 2048), lambda i: (0, 0)), pl.BlockSpec((8, 128), lambda i: (0, 0))],
        out_shape=[jax.ShapeDtypeStruct((T, NP), bf16),
                   jax.ShapeDtypeStruct((8, 2048), f32), jax.ShapeDtypeStruct((8, 128), f32)],
        input_output_aliases={13: 0},
        compiler_params=_cp(("arbitrary",)),
    )(p, p, p, cw, alog_l, dtb_l, dq8, dq8, dk8, dk8, dv, dv, dsm8, dp)


def _tri_inverse(m):
    r = lax.broadcasted_iota(jnp.int32, (L, L), 0)
    c = lax.broadcasted_iota(jnp.int32, (L, L), 1)
    eye = (r == c).astype(f32)[None]
    same = lambda w: ((r // w) == (c // w))[None]
    md = jnp.where(same(8), m, 0.0)
    m2 = _bdot_split(md, md)
    m4 = _bdot_split(m2, m2)
    t = _bdot_split(_bdot_split(eye - md, eye + m2), eye + m4)
    for w in (16, 32, 64):
        mo = jnp.where(same(w) & jnp.logical_not(same(w // 2)), m, 0.0)
        t = t - _bdot_split(_bdot_split(t, mo), t)
    return t


def _col_to_row(col, eye):
    return jnp.sum(eye * col, axis=1, keepdims=True)


def _row_to_col(row, eye):
    return jnp.sum(eye * row, axis=2, keepdims=True)


def _gdn_chunk_terms(q, k, v, beta, g, t_inv=None):
    r = lax.broadcasted_iota(jnp.int32, (L, L), 0)
    c = lax.broadcasted_iota(jnp.int32, (L, L), 1)
    eye = (r == c).astype(f32)[None]
    causal, strict = (c <= r)[None], (c < r)[None]
    diff = g - _col_to_row(g, eye)
    dec = jnp.exp(jnp.where(causal, diff, 0.0))
    dc = jnp.where(causal, dec, 0.0)
    ds = jnp.where(strict, dec, 0.0)
    eg = jnp.exp(g)
    gl = g[:, L - 1:L, :]
    egl = jnp.exp(gl - g)
    kb = k * beta
    kk = _bdot_nt(_b(k), _b(kb))
    qk = _bdot_nt(_b(q), _b(kb))
    m = kk * ds
    aqk = qk * dc
    if t_inv is None:
        t_inv = _tri_inverse(m)
    tb = _b(t_inv)
    keg = k * eg
    u = _bdot(tb, _b(v))
    w = _bdot(tb, _b(keg))
    ks = kb * egl
    ksw = _bdot_tn(_b(ks), _b(w))
    return dict(eye=eye, causal=causal, strict=strict, dc=dc, ds=ds, eg=eg, gl=gl, egl=egl, kb=kb, kk=kk, qk=qk,
                m=m, aqk=aqk, t=t_inv, u=u, w=w, qi=q * eg, ks=ks, keg=keg, ksw=ksw)


def gdn_fwd(qn, kn, vv, sm):
    T = qn.shape[0]
    tB = min(2 * GDN_BLOCK, T)
    nc = tB // L
    N = T // L

    def body(q_ref, k_ref, v_ref, sm_ref, o_ref, st_ref, ti_ref, s_scr):
        j = pl.program_id(0)

        @pl.when(pl.program_id(1) == 0)
        def _():
            s_scr[...] = jnp.zeros_like(s_scr)
        smv = sm_ref[...]
        lane = lax.broadcasted_iota(jnp.int32, smv.shape, 1)
        q = q_ref[...].reshape(nc, L, HK)
        k = k_ref[...].reshape(nc, L, HK)
        tms, ksus, kswbs, egls = [], [], [], []
        for a in range(2):
            h = 2 * j + a
            beta = jnp.sum(jnp.where(lane == h, smv, 0.0), axis=1, keepdims=True).reshape(nc, L, 1)
            g = jnp.sum(jnp.where(lane == 8 + h, smv, 0.0), axis=1, keepdims=True).reshape(nc, L, 1)
            v = v_ref[:, a * HK:(a + 1) * HK].reshape(nc, L, HK)
            tm = _gdn_chunk_terms(q, k, v, beta, g)
            ti_ref[a] = tm["t"]
            tms.append(tm)
            ksus.append(_bdot_tn(_b(tm["ks"]), _b(tm["u"])))
            kswbs.append(_b(tm["ksw"]))
            egls.append(jnp.exp(tm["gl"]))
        s = [s_scr[0], s_scr[1]]
        states = [[None] * nc, [None] * nc]
        for ci in range(nc):
            for a in range(2):
                states[a][ci] = s[a]
                s[a] = egls[a][ci] * s[a] + (ksus[a][ci] - _dot(kswbs[a][ci], _b(s[a])))
        for a in range(2):
            s_scr[a] = s[a]
            sall = jnp.stack(states[a], axis=0)
            st_ref[a] = sall
            sb = _b(sall)
            tm = tms[a]
            e = tm["u"] - _bdot(_b(tm["w"]), sb)
            o = _bdot(_b(tm["qi"]), sb) + _bdot(_b(tm["aqk"]), _b(e))
            o_ref[:, a * HK:(a + 1) * HK] = o.reshape(tB, HK)

    return pl.pallas_call(
        body, name="gdn_fwd", grid=(N_GDN // 2, T // tB),
        in_specs=[pl.BlockSpec((tB, HK), lambda j, n: (n, j)), pl.BlockSpec((tB, HK), lambda j, n: (n, j)),
                  pl.BlockSpec((tB, 2 * HK), lambda j, n: (n, j)), pl.BlockSpec((tB, 128), lambda j, n: (n, 0))],
        out_specs=[pl.BlockSpec((tB, 2 * HK), lambda j, n: (n, j)),
                   pl.BlockSpec((2, nc, HK, HK), lambda j, n: (j, n, 0, 0)),
                   pl.BlockSpec((2, nc, L, L), lambda j, n: (j, n, 0, 0))],
        out_shape=[jax.ShapeDtypeStruct((T, N_GDN * HK), f32), jax.ShapeDtypeStruct((N_GDN, N, HK, HK), f32),
                   jax.ShapeDtypeStruct((N_GDN, N, L, L), f32)],
        scratch_shapes=[pltpu.VMEM((2, HK, HK), f32)],
        compiler_params=_cp(("parallel", "arbitrary")),
    )(qn, kn, vv, sm)


def gdn_bwd(qn, kn, vv, sm, st, ti, do):
    T = qn.shape[0]
    tB = min(GDN_BLOCK, T)
    nc = tB // L
    nb = T // tB

    def body(q_ref, k_ref, v_ref, sm_ref, st_ref, ti_ref, do_ref, dq_ref, dk_ref, dv_ref, dsm_ref, ds_scr):
        j = pl.program_id(0)

        @pl.when(pl.program_id(1) == 0)
        def _():
            ds_scr[...] = jnp.zeros_like(ds_scr)
        smv = sm_ref[...]
        lane = lax.broadcasted_iota(jnp.int32, smv.shape, 1)
        q = q_ref[...].reshape(nc, L, HK)
        k = k_ref[...].reshape(nc, L, HK)
        kbf, qbf = _b(k), _b(q)
        heads = []
        for a in range(2):
            h = 2 * j + a
            beta = jnp.sum(jnp.where(lane == h, smv, 0.0), axis=1, keepdims=True).reshape(nc, L, 1)
            g = jnp.sum(jnp.where(lane == 8 + h, smv, 0.0), axis=1, keepdims=True).reshape(nc, L, 1)
            v = v_ref[:, a * HK:(a + 1) * HK].reshape(nc, L, HK)
            do = do_ref[:, a * HK:(a + 1) * HK].reshape(nc, L, HK)
            s = st_ref[a]
            tm = _gdn_chunk_terms(q, k, v, beta, g, t_inv=ti_ref[a])
            sb, dob = _b(s), _b(do)
            e = tm["u"] - _bdot(_b(tm["w"]), sb)
            de0 = _bdot_tn(_b(tm["aqk"]), dob)
            ds0 = _bdot_tn(_b(tm["qi"]), dob) - _bdot_tn(_b(tm["w"]), _b(de0))
            heads.append(dict(h=h, beta=beta, tm=tm, s=s, sb=sb, dob=dob, e=e, de0=de0, ds0=ds0, kswb=_b(tm["ksw"]),
                              egl_last=jnp.exp(tm["gl"])))
        dsn = [ds_scr[0], ds_scr[1]]
        dsns = [[None] * nc, [None] * nc]
        for ci in reversed(range(nc)):
            for a, hd in enumerate(heads):
                dsns[a][ci] = dsn[a]
                dsn[a] = hd["ds0"][ci] + (hd["egl_last"][ci] * dsn[a] - _dot_tn(hd["kswb"][ci], _b(dsn[a])))
        rowi = lax.broadcasted_iota(jnp.int32, (nc, L, 1), 1)
        rin = lax.broadcasted_iota(jnp.int32, (tB, 128), 0) % L
        for a, hd in enumerate(heads):
            ds_scr[a] = dsn[a]
            tm, s, sb, dob, e, beta, h = hd["tm"], hd["s"], hd["sb"], hd["dob"], hd["e"], hd["beta"], hd["h"]
            eye, dc, ds_, eg, egl = tm["eye"], tm["dc"], tm["ds"], tm["eg"], tm["egl"]
            kb, u, w, qi, ks = tm["kb"], tm["u"], tm["w"], tm["qi"], tm["ks"]
            eb = _b(e)
            dsp = jnp.stack(dsns[a], axis=0)
            dspb = _b(dsp)
            de = hd["de0"] + _bdot(_b(ks), dspb)
            deb = _b(de)
            dks = _bdot_nt(eb, dspb)
            dqi = _bdot_nt(dob, sb)
            daqk = jnp.where(tm["causal"], _bdot_nt(dob, eb), 0.0)
            dw = -_bdot_nt(deb, sb)
            tb = _b(tm["t"])
            dvv = _bdot_tn(tb, deb)
            dkg = _bdot_tn(tb, _b(dw))
            dm = -jnp.where(tm["strict"], _bdot_nt(_b(dvv), _b(u)) + _bdot_nt(_b(dkg), _b(w)), 0.0)
            x = _b(dm * ds_)
            y = _b(daqk * dc)
            kbb = _b(kb)
            dk = _bdot(x, kbb) + dkg * eg
            dkb = _bdot_tn(x, kbf) + _bdot_tn(y, qbf) + dks * egl
            dq = _bdot(y, kbb) + dqi * eg
            dk = dk + dkb * beta
            dbeta = jnp.sum(dkb * k, axis=-1, keepdims=True)
            z = dm * tm["m"] + daqk * tm["aqk"]
            dg = (jnp.sum(dqi * qi - dks * ks + dkg * tm["keg"], axis=-1, keepdims=True)
                  + jnp.sum(z, axis=-1, keepdims=True) - _row_to_col(jnp.sum(z, axis=1, keepdims=True), eye))
            dgl = (hd["egl_last"] * jnp.sum(jnp.sum(s * dsp, axis=2, keepdims=True), axis=1, keepdims=True)
                   + jnp.sum(jnp.sum(dks * ks, axis=2, keepdims=True), axis=1, keepdims=True))
            dg = dg + jnp.where(rowi == L - 1, dgl, 0.0)
            dla = _chunk_rev_cumsum(jnp.broadcast_to(dg.reshape(tB, 1), (tB, 128)), rin)
            sl = slice(a * HK, (a + 1) * HK)
            dq_ref[:, sl] = dq.reshape(tB, HK)
            dk_ref[:, sl] = dk.reshape(tB, HK)
            dv_ref[:, sl] = dvv.reshape(tB, HK)
            dsm_ref[:, sl] = jnp.where(lane == h, dbeta.reshape(tB, 1), jnp.where(lane == 8 + h, dla, 0.0))

    rev = lambda n: nb - 1 - n
    pair = pl.BlockSpec((tB, 2 * HK), lambda j, n: (rev(n), j))
    return pl.pallas_call(
        body, name="gdn_bwd", grid=(N_GDN // 2, nb),
        in_specs=[pl.BlockSpec((tB, HK), lambda j, n: (rev(n), j)), pl.BlockSpec((tB, HK), lambda j, n: (rev(n), j)),
                  pair, pl.BlockSpec((tB, 128), lambda j, n: (rev(n), 0)),
                  pl.BlockSpec((2, nc, HK, HK), lambda j, n: (j, rev(n), 0, 0)),
                  pl.BlockSpec((2, nc, L, L), lambda j, n: (j, rev(n), 0, 0)), pair],
        out_specs=[pair] * 4,
        out_shape=[jax.ShapeDtypeStruct((T, N_GDN * HK), f32)] * 4,
        scratch_shapes=[pltpu.VMEM((2, HK, HK), f32)],
        compiler_params=_cp(("parallel", "arbitrary")),
    )(qn, kn, vv, sm, st, ti, do)


def _hgrn_prep(bq, bf_, bi, lb):
    tB = bq.shape[0]
    sq, dsq = _silu2(bq)
    sg = _sigmoid(bf_)
    f = lb + (1.0 - lb) * sg
    logf = jnp.log(jnp.maximum(f, MIN_F))
    rin = lax.broadcasted_iota(jnp.int32, (tB, HK), 0) % L
    g = _chunk_cumsum(logf, rin)
    return sq * QK_SCALE, sg, f, 1.0 - f, bi, g, rin, dsq * QK_SCALE


def _hgrn_intra(q, kk, v, g, do=None):
    n = q.shape[0]
    nsub = L // SUB
    bwd = do is not None
    o_rows = [None] * nsub
    if bwd:
        dq_rows = [None] * nsub
        dkk_acc = jnp.zeros_like(kk)
        dv_acc = jnp.zeros_like(v)
    for i in range(1, nsub):
        lo, hi, w = i * SUB, (i + 1) * SUB, i * SUB
        ref = g[:, lo - 1:lo, :]
        eq = jnp.exp(g[:, lo:hi, :] - ref)
        ek = jnp.exp(ref - g[:, :w, :])
        qs = _b(q[:, lo:hi, :] * eq)
        ks = _b(kk[:, :w, :] * ek)
        p = _bdot_nt(qs, ks)
        o_rows[i] = _bdot(_b(p), _b(v[:, :w, :]))
        if bwd:
            dob = _b(do[:, lo:hi, :])
            dp = _b(_bdot_nt(dob, _b(v[:, :w, :])))
            dq_rows[i] = _bdot(dp, ks) * eq
            pad = jnp.zeros((n, L - w, HK), f32)
            dkk_acc = dkk_acc + jnp.concatenate([_bdot_tn(dp, qs) * ek, pad], axis=1)
            dv_acc = dv_acc + jnp.concatenate([_bdot_tn(_b(p), dob), pad], axis=1)
    m = n * nsub
    q4, k4, v4, g4 = (a.reshape(m, SUB, HK) for a in (q, kk, v, g))
    r = lax.broadcasted_iota(jnp.int32, (m, SUB, HK), 1)
    od = jnp.zeros((m, SUB, HK), f32)
    if bwd:
        do4 = do.reshape(m, SUB, HK)
        dqd = jnp.zeros((m, SUB, HK), f32)
        dkd = jnp.zeros((m, SUB, HK), f32)
        dvd = jnp.zeros((m, SUB, HK), f32)
    for j in range(SUB):
        gj, kj, vj = g4[:, j:j + 1, :], k4[:, j:j + 1, :], v4[:, j:j + 1, :]
        ok = r >= j
        e = jnp.where(ok, jnp.exp(g4 - gj), 0.0)
        xq = q4 * e
        pj = jnp.sum(xq * kj, axis=-1, keepdims=True)
        od = od + pj * vj
        if bwd:
            dpj = jnp.sum(do4 * vj, axis=-1, keepdims=True)
            dqd = dqd + dpj * kj * e
            dkd = dkd + jnp.where(r == j, jnp.sum(dpj * xq, axis=1, keepdims=True), 0.0)
            dvd = dvd + jnp.where(r == j, jnp.sum(pj * do4, axis=1, keepdims=True), 0.0)
    od = od.reshape(n, L, HK)
    o = jnp.concatenate([od[:, :SUB, :]] + [od[:, i * SUB:(i + 1) * SUB, :] + o_rows[i] for i in range(1, nsub)], axis=1)
    if not bwd:
        return o
    dqd = dqd.reshape(n, L, HK)
    dq = jnp.concatenate([dqd[:, :SUB, :]] + [dqd[:, i * SUB:(i + 1) * SUB, :] + dq_rows[i] for i in range(1, nsub)], axis=1)
    return o, dq, dkk_acc + dkd.reshape(n, L, HK), dv_acc + dvd.reshape(n, L, HK)


def hgrn_fwd(p, lbs):
    T = p.shape[0]
    tB = min(1024, T)
    nc = tB // L
    N = T // L

    def body(b_ref, lb_ref, o_ref, st_ref, s_scr):
        @pl.when(pl.program_id(1) == 0)
        def _():
            s_scr[...] = jnp.zeros_like(s_scr)
        bv = b_ref[...].astype(f32)
        q, sg, f, kk, v, g, rin, _ = _hgrn_prep(bv[:, 0:HK], bv[:, HK:2 * HK], bv[:, 2 * HK:3 * HK], lb_ref[...])
        q3, k3, v3, g3 = (a.reshape(nc, L, HK) for a in (q, kk, v, g))
        o = _hgrn_intra(q3, k3, v3, g3)
        gl = g3[:, L - 1:L, :]
        qt = _b(q3 * jnp.exp(g3))
        kt = _b(k3 * jnp.exp(gl - g3))
        vb = _b(v3)
        st = s_scr[...]
        for c in range(nc):
            st_ref[c] = st
            o_ref[c * L:(c + 1) * L, :] = o[c] + _dot_nt(qt[c], _b(st))
            st = st * jnp.exp(gl[c]) + _dot_tn(vb[c], kt[c])
        s_scr[...] = st

    return pl.pallas_call(
        body, name="hgrn_fwd", grid=(N_HGRN, T // tB),
        in_specs=[pl.BlockSpec((tB, REG_BH), lambda h, n: (n, OFF_B // REG_BH + h)),
                  pl.BlockSpec((1, HK), lambda h, n: (0, h))],
        out_specs=[pl.BlockSpec((tB, HK), lambda h, n: (n, h)),
                   pl.BlockSpec((None, nc, HK, HK), lambda h, n: (h, n, 0, 0))],
        out_shape=[jax.ShapeDtypeStruct((T, N_HGRN * HK), f32), jax.ShapeDtypeStruct((N_HGRN, N, HK, HK), f32)],
        scratch_shapes=[pltpu.VMEM((HK, HK), f32)],
        compiler_params=_cp(("parallel", "arbitrary")),
    )(p, lbs)


def hgrn_bwd(p, lbs, st, do, dp):
    T = p.shape[0]
    tB = min(256, T)
    nc = tB // L
    nb = T // tB

    def body(b_ref, lb_ref, st_ref, do_ref, dp_in, dp_ref, dlb_ref, ds_scr):
        @pl.when(pl.program_id(1) == 0)
        def _():
            ds_scr[...] = jnp.zeros_like(ds_scr)
            dlb_ref[...] = jnp.zeros_like(dlb_ref)
        lb = lb_ref[...]
        bv = b_ref[...].astype(f32)
        q, sg, f, kk, v, g, rin, dq_dbq = _hgrn_prep(bv[:, 0:HK], bv[:, HK:2 * HK], bv[:, 2 * HK:3 * HK], lb)
        q3, k3, v3, g3 = (a.reshape(nc, L, HK) for a in (q, kk, v, g))
        do3 = do_ref[...].reshape(nc, L, HK)
        dob = _b(do3)
        gl = g3[:, L - 1:L, :]
        egl = jnp.exp(gl)
        eg, egr = jnp.exp(g3), jnp.exp(gl - g3)
        qt, kt = q3 * eg, k3 * egr
        s = st_ref[...]
        ds0 = _bdot_tn(dob, _b(qt))
        dsn = ds_scr[...]
        dsns = [None] * nc
        for c in reversed(range(nc)):
            dsns[c] = dsn
            dsn = ds0[c] + dsn * egl[c]
        ds_scr[...] = dsn
        dsp = jnp.stack(dsns, axis=0)
        dspb = _b(dsp)
        dqt = _bdot(dob, _b(s))
        dkt = _bdot(_b(v3), dspb)
        dv_state = _bdot_nt(_b(kt), dspb)
        dgl = egl * jnp.sum(s * dsp, axis=1, keepdims=True) + jnp.sum(dkt * kt, axis=1, keepdims=True)
        _, dq_i, dkk_i, dv_i = _hgrn_intra(q3, k3, v3, g3, do=do3)
        dq = dq_i + dqt * eg
        dkk = dkk_i + dkt * egr
        dv = dv_i + dv_state
        rowi = lax.broadcasted_iota(jnp.int32, (nc, L, HK), 1)
        dg = q3 * dq - k3 * dkk + jnp.where(rowi == L - 1, dgl, 0.0)
        dlogf = _chunk_rev_cumsum(dg.reshape(tB, HK), rin)
        dkk2 = dkk.reshape(tB, HK)
        df = jnp.where(f > MIN_F, dlogf / f, 0.0) - dkk2
        dlb_ref[...] += jnp.sum(df * (1.0 - sg), axis=0, keepdims=True)
        dp_ref[:, 0:HK] = _b(dq.reshape(tB, HK) * dq_dbq)
        dp_ref[:, HK:2 * HK] = _b(df * (1.0 - lb) * sg * (1.0 - sg))
        dp_ref[:, 2 * HK:3 * HK] = _b(dv.reshape(tB, HK))

    rev = lambda n: nb - 1 - n
    return pl.pallas_call(
        body, name="hgrn_bwd", grid=(N_HGRN, nb),
        in_specs=[pl.BlockSpec((tB, REG_BH), lambda h, n: (rev(n), OFF_B // REG_BH + h)),
                  pl.BlockSpec((1, HK), lambda h, n: (0, h)),
                  pl.BlockSpec((None, nc, HK, HK), lambda h, n: (h, rev(n), 0, 0)),
                  pl.BlockSpec((tB, HK), lambda h, n: (rev(n), h)), pl.BlockSpec(memory_space=pl.ANY)],
        out_specs=[pl.BlockSpec((tB, REG_BH), lambda h, n: (rev(n), OFF_B // REG_BH + h)),
                   pl.BlockSpec((1, HK), lambda h, n: (0, h))],
        out_shape=[jax.ShapeDtypeStruct((T, NP), bf16), jax.ShapeDtypeStruct((1, N_HGRN * HK), f32)],
        input_output_aliases={4: 0},
        scratch_shapes=[pltpu.VMEM((HK, HK), f32)],
        compiler_params=_cp(("parallel", "arbitrary")),
    )(p, lbs, st, do, dp)


def _headnorm_fwd(o, z, w, nheads):
    outs, parts = [], []
    for hh in range(nheads):
        sl = slice(hh * HK, (hh + 1) * HK)
        oh = o[:, sl]
        r = lax.rsqrt(jnp.mean(oh * oh, axis=-1, keepdims=True) + NORM_EPS)
        on = oh * r
        sz, dsz = _silu2(z[:, sl])
        outs.append(on * w * sz)
        parts.append((r, on, sz, dsz))
    return jnp.concatenate(outs, axis=1), parts


def _headnorm_bwd(parts, w, dy):
    dos, dzs = [], []
    dw = jnp.zeros((1, HK), f32)
    for hh, (r, on, sz, dsz) in enumerate(parts):
        dyh = dy[:, hh * HK:(hh + 1) * HK]
        dn = dyh * sz * w
        dos.append(r * (dn - on * jnp.mean(dn * on, axis=-1, keepdims=True)))
        dzs.append(dyh * on * w * dsz)
        dw = dw + jnp.sum(dyh * sz * on, axis=0, keepdims=True)
    return jnp.concatenate(dos, axis=1), jnp.concatenate(dzs, axis=1), dw


def _merge_specs(tT, l):
    row = lambda w, cb=0: pl.BlockSpec((tT, w), lambda i, cb=cb: (i, cb))
    full = lambda r, c: pl.BlockSpec((r, c), lambda i: (0, 0))
    layer = lambda r, c: pl.BlockSpec((None, r, c), lambda i: (l, 0, 0))
    return row, full, layer


def merge_fwd(x, p, ya, ob, oc, hw, gw, bg, woa, wob, woc, wo, l):
    T = x.shape[0]
    tT = min(512, T)
    row, full, layer = _merge_specs(tT, l)

    def body(x_ref, pm_ref, ya_ref, ob_ref, oc_ref, hw_ref, gw_ref, bg_ref,
             woa_ref, wob_ref, woc_ref, wo_ref, out_ref):
        yb = _b(_headnorm_fwd(ob_ref[...], pm_ref[:, M_BZ:M_G].astype(f32), hw_ref[...], N_HGRN)[0])
        yc = _b(_headnorm_fwd(oc_ref[...], pm_ref[:, M_CZ:REG_M].astype(f32), gw_ref[...], N_GDN)[0])
        gates = _sigmoid(pm_ref[:, M_G:M_CZ].astype(f32) + bg_ref[...])
        merged = (gates[:, 0:D] * _dot(ya_ref[...], woa_ref[...]) + gates[:, D:2 * D] * _dot(yb, wob_ref[...])
                  + gates[:, 2 * D:3 * D] * _dot(yc, woc_ref[...]))
        out_ref[...] = x_ref[...] + _dot(_b(merged), wo_ref[...])

    return pl.pallas_call(
        body, name="merge_fwd", grid=(T // tT,),
        in_specs=[row(D), row(REG_M, OFF_M // REG_M),
                  row(512), row(512), row(1024), full(1, HK), full(1, HK), full(1, 3 * D),
                  layer(512, D), layer(512, D), layer(D, D), layer(D, D)],
        out_specs=row(D),
        out_shape=jax.ShapeDtypeStruct((T, D), f32),
        compiler_params=_cp(("parallel",)),
    )(x, p, ya, ob, oc, hw, gw, bg, woa, wob, woc, wo)


def merge_bwd(dxo, p, ya, ob, oc, hw, gw, bg, woa, wob, woc, wo, l):
    T = dxo.shape[0]
    tT = min(256, T)
    row, full, layer = _merge_specs(tT, l)

    def body(dx_ref, pm_ref, ya_ref, ob_ref, oc_ref, hw_ref, gw_ref, bg_ref,
             woa_ref, wob_ref, woc_ref, wo_ref,
             dya_ref, dob_ref, doc_ref, dp_ref, mg_ref, dy3_ref, yb_ref, yc_ref,
             dbg_ref, dhw_ref, dgw_ref):
        @pl.when(pl.program_id(0) == 0)
        def _():
            dbg_ref[...] = jnp.zeros_like(dbg_ref)
            dhw_ref[...] = jnp.zeros_like(dhw_ref)
            dgw_ref[...] = jnp.zeros_like(dgw_ref)
        ob, oc, bz, cz = ob_ref[...], oc_ref[...], pm_ref[:, M_BZ:M_G].astype(f32), pm_ref[:, M_CZ:REG_M].astype(f32)
        hw_, gw_ = hw_ref[...], gw_ref[...]
        yb, parts_b = _headnorm_fwd(ob, bz, hw_, N_HGRN)
        yc, parts_c = _headnorm_fwd(oc, cz, gw_, N_GDN)
        yb, yc = _b(yb), _b(yc)
        yb_ref[...] = yb
        yc_ref[...] = yc
        gates = _sigmoid(pm_ref[:, M_G:M_CZ].astype(f32) + bg_ref[...])
        ys = (_dot(ya_ref[...], woa_ref[...]), _dot(yb, wob_ref[...]), _dot(yc, woc_ref[...]))
        dmerged = _dot_nt(_b(dx_ref[...]), wo_ref[...])
        merged = jnp.zeros_like(dmerged)
        dys = []
        for i in range(3):
            gi = gates[:, i * D:(i + 1) * D]
            merged = merged + gi * ys[i]
            dyi = _b(dmerged * gi)
            dys.append(dyi)
            dy3_ref[:, i * D:(i + 1) * D] = dyi
            dgp = dmerged * ys[i] * gi * (1.0 - gi)
            dp_ref[:, M_G + i * D:M_G + (i + 1) * D] = _b(dgp)
            dbg_ref[:, i * D:(i + 1) * D] += jnp.sum(dgp, axis=0, keepdims=True)
        mg_ref[...] = _b(merged)
        dya_ref[...] = _dot_nt(dys[0], woa_ref[...])
        dob, dbz, dhw = _headnorm_bwd(parts_b, hw_, _dot_nt(dys[1], wob_ref[...]))
        doc, dcz, dgw = _headnorm_bwd(parts_c, gw_, _dot_nt(dys[2], woc_ref[...]))
        dob_ref[...] = dob
        doc_ref[...] = doc
        dp_ref[:, M_BZ:M_G] = _b(dbz)
        dp_ref[:, M_CZ:REG_M] = _b(dcz)
        dhw_ref[...] += dhw
        dgw_ref[...] += dgw

    sd = jax.ShapeDtypeStruct
    return pl.pallas_call(
        body, name="merge_bwd", grid=(T // tT,),
        in_specs=[row(D), row(REG_M, OFF_M // REG_M),
                  row(512), row(512), row(1024), full(1, HK), full(1, HK), full(1, 3 * D),
                  layer(512, D), layer(512, D), layer(D, D), layer(D, D)],
        out_specs=[row(512), row(512), row(1024), row(REG_M, OFF_M // REG_M), row(D), row(3 * D), row(512),
                   row(1024), full(1, 3 * D), full(1, HK), full(1, HK)],
        out_shape=[sd((T, 512), f32), sd((T, 512), f32), sd((T, 1024), f32), sd((T, NP), bf16),
                   sd((T, D), bf16), sd((T, 3 * D), bf16), sd((T, 512), bf16),
                   sd((T, 1024), bf16), sd((1, 3 * D), f32), sd((1, HK), f32), sd((1, HK), f32)],
        compiler_params=_cp(("arbitrary",)),
    )(dxo, p, ya, ob, oc, hw, gw, bg, woa, wob, woc, wo)


def layer_fwd(x, w):
    l = w["l"]
    p, h = inproj_fwd(x, w["norm_w"], w["w_in"])
    ya = mixa_fwd(p, w["conv_a"])
    qn, kn, vv, sm = mixc_pre_fwd(p, w["conv_c"], w["alog_l"], w["dtb_l"])
    oc, st_c, ti = gdn_fwd(qn, kn, vv, sm)
    ob, st_b = hgrn_fwd(p, w["lbs"])
    if "late" in w:
        w.update(w.pop("late")((ya, oc, ob)))
    xo = merge_fwd(x, p, ya, ob, oc, w["hgrn_norm_w"], w["gdn_norm_w"], w["b_gate"],
                   w["w_out_a"], w["w_out_b"], w["w_out_c"], w["w_o"], l)
    saved = dict(x=x, p=p, h=h, ya=ya, qn=qn, kn=kn, vv=vv, sm=sm, oc=oc, st_c=st_c, ti=ti, ob=ob, st_b=st_b)
    return xo, saved


OUT_MATS = (("w_out_a", "cols"), ("w_out_b", "cols"), ("w_out_c", "rows"), ("w_o", "rows"))


def layer_bwd(dxo, w, s, chip):
    p, l = s["p"], w["l"]
    (dya, dob, doc, dp, merged, dy3, yb, yc, dbg, dhw, dgw) = merge_bwd(
        dxo, p, s["ya"], s["ob"], s["oc"], w["hgrn_norm_w"], w["gdn_norm_w"], w["b_gate"],
        w["w_out_a"], w["w_out_b"], w["w_out_c"], w["w_o"], l)
    full = {"w_o": matmul_tn(merged, dxo, "dw_o", with_bf16=True),
            "w_out_a": matmul_tn(s["ya"], dy3, "dw_out_a", n=D, b_col0=0, with_bf16=True),
            "w_out_b": matmul_tn(yb, dy3, "dw_out_b", n=D, b_col0=D, with_bf16=True),
            "w_out_c": matmul_tn(yc, dy3, "dw_out_c", n=D, b_col0=2 * D, with_bf16=True)}
    out_kinds = [k for _, k in OUT_MATS]
    sent_out, token = exchange_start([full[n][1] for n, _ in OUT_MATS], out_kinds, f"grads_out_start{l}")
    dp, dlbs = hgrn_bwd(p, w["lbs"] + token[0:1, 0:1], s["st_b"], dob, dp)
    dq8, dk8, dvv, dsm8 = gdn_bwd(s["qn"], s["kn"], s["vv"], s["sm"], s["st_c"], s["ti"], doc)
    dp, dcc, dsmall = mixc_pre_bwd(p, w["conv_c"], w["alog_l"], w["dtb_l"], dq8, dk8, dvv, dsm8, dp)
    dp, dca = mixa_bwd(p, w["conv_a"], dya, dp)
    gf_win, gb_win = win_from_padded(matmul_tn(s["h"], dp, "dw_in"), chip)
    sent_in, token = exchange_start([gb_win], ["slot"], f"grads_in_start{l}")
    dx, dnw = inproj_bwd(dp, w["w_in"], s["x"], w["norm_w"] + token[0:1, 0:1], dxo)
    recv_out = exchange_wait(sent_out, out_kinds, dx, f"grads_out_wait{l}")
    recv_in = exchange_wait(sent_in, ["slot"], dx, f"grads_in_wait{l}")
    half = {"w_in": partial_sum(gf_win, "own", recv_in[0], chip, "psum_w_in", transposed=True)}
    for (n, kind), r in zip(OUT_MATS, recv_out):
        half[n] = partial_sum(full[n][0], kind, r, chip, "psum_" + n)
    small = dict(norm_w=dnw, b_gate=dbg, hgrn_norm_w=dhw, gdn_norm_w=dgw, lbs=dlbs, conv_a=dca[0:3], conv_c=dcc[0:4],
                 a_log=dsmall[0:1, 8:16], dt_bias=dsmall[1:2, 8:16])
    return dx, small, half


def lbs_fwd(lb):
    def body(lb_ref, o_ref):
        l0, l1 = lb_ref[0:1, :], lb_ref[1:2, :]
        mx = jnp.maximum(l0, l1)
        e0, e1 = jnp.exp(l0 - mx), jnp.exp(l1 - mx)
        o_ref[0:1, :] = jnp.zeros_like(l0)
        o_ref[1:2, :] = e1 / (e0 + e1)
    return pl.pallas_call(body, name="lbs_fwd", out_shape=jax.ShapeDtypeStruct(lb.shape, f32))(lb)


def _adam_math(w, g, m, v):
    mn = ADAM_B1 * m + (1.0 - ADAM_B1) * g
    vn = ADAM_B2 * v + (1.0 - ADAM_B2) * (g * g)
    mh = mn / (1.0 - ADAM_B1 ** ADAM_STEP)
    vh = vn / (1.0 - ADAM_B2 ** ADAM_STEP)
    return -ADAM_LR * (mh / (jnp.sqrt(vh) + ADAM_EPS) + ADAM_WD * w), mn, vn


def adam(w, g, m, v, name):
    R, C = w.shape
    tr = 256 if R % 256 == 0 else R

    def body(w_ref, g_ref, m_ref, v_ref, d_ref, mo_ref, vo_ref):
        d, mn, vn = _adam_math(w_ref[...], g_ref[...], m_ref[...], v_ref[...])
        d_ref[...] = d
        mo_ref[...] = mn
        vo_ref[...] = vn

    spec = pl.BlockSpec((tr, C), lambda i: (i, 0))
    return pl.pallas_call(
        body, name=name, grid=(R // tr,), in_specs=[spec] * 4, out_specs=[spec] * 3,
        out_shape=[jax.ShapeDtypeStruct((R, C), f32)] * 3, compiler_params=_cp(("parallel",)),
    )(w, g, m, v)


def adam_pair(h, hs, w, m, v, name):
    _, R, C = w.shape
    cp = h[0].shape[1]
    tr = 128 if R % 128 == 0 else R
    nt = R // tr

    def body(h0_ref, h1_ref, s0_ref, s1_ref, w_ref, m_ref, v_ref, g_ref, d_ref, mo_ref, vo_ref):
        def update(h_ref, s_ref):
            g = (h_ref[...] + s_ref[...])[:, :C]
            d, mn, vn = _adam_math(w_ref[...], g, m_ref[...], v_ref[...])
            g_ref[...] = g
            d_ref[...] = d
            mo_ref[...] = mn
            vo_ref[...] = vn

        @pl.when(pl.program_id(0) == 0)
        def _():
            update(h0_ref, s0_ref)

        @pl.when(pl.program_id(0) == 1)
        def _():
            update(h1_ref, s1_ref)

    h0spec = pl.BlockSpec((tr, cp), lambda l, i: (jnp.where(l == 0, i, nt - 1), 0))
    h1spec = pl.BlockSpec((tr, cp), lambda l, i: (jnp.where(l == 1, i, 0), 0))
    spec = pl.BlockSpec((None, tr, C), lambda l, i: (l, i, 0))
    return pl.pallas_call(
        body, name=name, grid=(2, nt), in_specs=[h0spec, h1spec, h0spec, h1spec, spec, spec, spec],
        out_specs=[spec] * 4, out_shape=[jax.ShapeDtypeStruct(w.shape, f32)] * 4,
        compiler_params=_cp(("arbitrary", "arbitrary")),
    )(h[0], h[1], hs[0], hs[1], w, m, v)


_SMALL = (("norm_w", 2 * D), ("b_gate", 6 * D), ("lower_bounds", None), ("hgrn_norm_w", 2 * HK),
          ("gdn_norm_w", 2 * HK), ("a_log", 16), ("dt_bias", 16), ("final_norm_w", D), ("loss", None))
_CONV = (("conv_a", 2 * 3 * 512), ("conv_c", 2 * 4 * 2048))


def _small_rows(n):
    return 16 if n is None else -(-n // 1024) * 8


LB_ROW = sum(_small_rows(n) for _, n in _SMALL[:2])
ADAM_ROWS = sum(_small_rows(n) for _, n in _SMALL)
SMALL_ROWS = ADAM_ROWS + sum(_small_rows(n) for _, n in _CONV)


def small_update(parts, wp, mp, vp):
    def body(p_ref, w_ref, m_ref, v_ref, g_ref, d_ref, mo_ref, vo_ref):
        gs = p_ref[0]
        for i in range(1, 8):
            gs = gs + p_ref[i]
        w = w_ref[...]
        l0, l1 = w[LB_ROW:LB_ROW + 8], w[LB_ROW + 8:LB_ROW + 16]
        mx = jnp.maximum(l0, l1)
        e0, e1 = jnp.exp(l0 - mx), jnp.exp(l1 - mx)
        p0, p1 = e0 / (e0 + e1), e1 / (e0 + e1)
        dl1 = gs[LB_ROW + 8:LB_ROW + 16]
        s = p1 * dl1
        g = jnp.concatenate([gs[0:LB_ROW], -p0 * s, p1 * dl1 - p1 * s, gs[LB_ROW + 16:ADAM_ROWS]], axis=0)
        d, mn, vn = _adam_math(w, g, m_ref[...], v_ref[...])
        g_ref[0:ADAM_ROWS, :] = g
        g_ref[ADAM_ROWS:, :] = gs[ADAM_ROWS:]
        d_ref[...] = d
        mo_ref[...] = mn
        vo_ref[...] = vn
    sd = jax.ShapeDtypeStruct
    return pl.pallas_call(body, name="small_update",
                          out_shape=[sd((SMALL_ROWS, 128), f32)] + [sd((ADAM_ROWS, 128), f32)] * 3)(parts, wp, mp, vp)


def partial_sum(own, kind, recv, chip, name, transposed=False):
    _, r, c = recv.shape
    tr = 256 if r % 256 == 0 else r

    def body(chip_ref, o_ref, r_ref, out_ref):
        s = ((o_ref[...] + r_ref[0].astype(f32)) + r_ref[1].astype(f32)) + r_ref[2].astype(f32)
        out_ref[...] = s.T if transposed else s

    own_spec = {"own": pl.BlockSpec((tr, c), lambda i, chip: (i, 0)),
                "slot": pl.BlockSpec((None, tr, c), lambda i, chip: (chip[0], i, 0)),
                "cols": pl.BlockSpec((tr, c), lambda i, chip: (i, chip[0])),
                "rows": pl.BlockSpec((tr, c), lambda i, chip: (chip[0] * (r // tr) + i, 0))}[kind]
    out_spec = pl.BlockSpec((c, tr), lambda i, chip: (0, i)) if transposed else pl.BlockSpec((tr, c), lambda i, chip: (i, 0))
    return pl.pallas_call(
        body, name=name,
        grid_spec=pltpu.PrefetchScalarGridSpec(
            num_scalar_prefetch=1, grid=(r // tr,),
            in_specs=[own_spec, pl.BlockSpec((3, tr, c), lambda i, chip: (0, i, 0))], out_specs=out_spec),
        out_shape=jax.ShapeDtypeStruct((c, r) if transposed else (r, c), f32), compiler_params=_cp(("arbitrary",)),
    )(chip, own, recv)


def adam_pair_t(h, hs, wt, mt, vt, name):
    C, _, R = wt.shape
    tc = 128

    def body(h0_ref, h1_ref, s0_ref, s1_ref, w_ref, m_ref, v_ref, g_ref, d_ref, mo_ref, vo_ref):
        g = jnp.stack([h0_ref[...] + s0_ref[...], h1_ref[...] + s1_ref[...]], axis=1)
        d, mn, vn = _adam_math(w_ref[...], g, m_ref[...], v_ref[...])
        g_ref[...] = g
        d_ref[...] = d
        mo_ref[...] = mn
        vo_ref[...] = vn

    hspec = pl.BlockSpec((tc, R), lambda i: (i, 0))
    spec = pl.BlockSpec((tc, 2, R), lambda i: (i, 0, 0))
    return pl.pallas_call(
        body, name=name, grid=(pl.cdiv(C, tc),), in_specs=[hspec] * 4 + [spec] * 3, out_specs=[spec] * 4,
        out_shape=[jax.ShapeDtypeStruct(wt.shape, f32)] * 4, compiler_params=_cp(("parallel",)),
    )(h[0], h[1], hs[0], hs[1], wt, mt, vt)


MESH = pl.DeviceIdType.MESH
_HBM = pl.BlockSpec(memory_space=pltpu.HBM)


def _place():
    return lax.axis_index("x"), lax.axis_index("y"), lax.axis_index("c")


def weight_gather(arrs):
    n = len(arrs)

    def body(*refs):
        x_refs, out_refs = refs[:n], refs[n:2 * n]
        send_sems, recv_sems, local_sems = refs[2 * n:]
        x, y, c = _place()
        me, sibling = (x, y, c), (x, y, 1 - c)
        chips = [(1 - x, y), (x, 1 - y), (1 - x, 1 - y)]

        def copy(a, k, block, to, own_src=False):
            px, py, pc = block
            dst = out_refs[a].at[2 * px + py, pc]
            return pltpu.make_async_remote_copy(
                src_ref=x_refs[a].at[c] if own_src else dst, dst_ref=dst,
                send_sem=send_sems.at[7 * a + k], recv_sem=recv_sems.at[7 * a + k], device_id=to, device_id_type=MESH)

        mine = [pltpu.make_async_copy(x_refs[a].at[c], out_refs[a].at[2 * x + y, c], local_sems.at[a])
                for a in range(n)]
        for cp in mine:
            cp.start()
        first = []
        for a in range(n):
            first.append(copy(a, 0, me, sibling, own_src=True))
            first += [copy(a, 1 + j, me, (*chip, c), own_src=True) for j, chip in enumerate(chips)]
        for cp in first:
            cp.start()
        passed = []
        for j, chip in enumerate(chips):
            for a in range(n):
                copy(a, 1 + j, (*chip, c), me).wait_recv()
                fwd = copy(a, 4 + j, (*chip, c), sibling)
                fwd.start()
                passed.append(fwd)
        for a in range(n):
            copy(a, 0, sibling, me).wait_recv()
            for j, chip in enumerate(chips):
                copy(a, 4 + j, (*chip, 1 - c), me).wait_recv()
        for cp in first + passed:
            cp.wait_send()
        for cp in mine:
            cp.wait()

    return pl.pallas_call(
        body, name="weight_gather", in_specs=[_HBM] * n, out_specs=[_HBM] * n,
        out_shape=[jax.ShapeDtypeStruct((N_CHIPS,) + a.shape, a.dtype) for a in arrs],
        scratch_shapes=[pltpu.SemaphoreType.DMA((7 * n,)), pltpu.SemaphoreType.DMA((7 * n,)),
                        pltpu.SemaphoreType.DMA((n,))],
    )(*arrs)


SHARD_W = 256


_SEM = pl.BlockSpec(memory_space=pltpu.SEMAPHORE)
_EFFECT = pltpu.SideEffectType.DATAFLOW_SIDE_EFFECTING


def _landing_shape(a, kind):
    if kind == "sib":
        return a.shape
    if kind == "all":
        return (N_CHIPS,) + a.shape
    if kind == "slot":
        return (3,) + a.shape[1:]
    return (3,) + ((a.shape[0], SHARD_W) if kind == "cols" else (SHARD_W, a.shape[1]))


def _shard_copies(src_refs, land_refs, kinds, send_sems, recv_sems):
    x, y, c = _place()
    copies = []
    for a, (src, land, kind) in enumerate(zip(src_refs, land_refs, kinds)):
        if kind == "sib":
            copies.append(pltpu.make_async_remote_copy(
                src_ref=src, dst_ref=land, send_sem=send_sems.at[3 * a], recv_sem=recv_sems.at[3 * a],
                device_id=(x, y, 1 - c), device_id_type=MESH))
            continue
        for j, (px, py) in enumerate(((1 - x, y), (x, 1 - y), (1 - x, 1 - y))):
            q = 2 * px + py
            lo = pl.multiple_of(q * SHARD_W, SHARD_W)
            part = {"slot": lambda: src.at[q], "cols": lambda: src.at[:, pl.ds(lo, SHARD_W)],
                    "rows": lambda: src.at[pl.ds(lo, SHARD_W), :], "all": lambda: src}[kind]()
            k = 3 * a + j
            copies.append(pltpu.make_async_remote_copy(
                src_ref=part, dst_ref=land.at[2 * x + y] if kind == "all" else land.at[j],
                send_sem=send_sems.at[k], recv_sem=recv_sems.at[k], device_id=(px, py, c), device_id_type=MESH))
    return copies


def exchange_start(srcs, kinds, name, after=None):
    n = len(srcs)
    lands = [lax.empty(_landing_shape(a, k), a.dtype) for a, k in zip(srcs, kinds)]
    extra = [] if after is None else [after]

    def body(*refs):
        src_refs, land_refs, token = refs[:n], refs[n:2 * n], refs[-1]
        send_sems, recv_sems = refs[2 * n + len(extra)], refs[2 * n + len(extra) + 1]
        for cp in _shard_copies(src_refs, land_refs, kinds, send_sems, recv_sems):
            cp.start()
        token[...] = jnp.zeros_like(token)

    both = list(srcs) + lands
    out = pl.pallas_call(
        body, name=name,
        out_shape=(pltpu.SemaphoreType.DMA((3 * n,)), pltpu.SemaphoreType.DMA((3 * n,)),
                   *[pltpu.HBM(a.shape, a.dtype) for a in both], jax.ShapeDtypeStruct((8, 128), f32)),
        in_specs=[_HBM] * (2 * n) + [pl.BlockSpec(memory_space=pl.ANY)] * len(extra),
        out_specs=(_SEM, _SEM, *[_HBM] * (2 * n), pl.BlockSpec(memory_space=pltpu.VMEM)),
        input_output_aliases={i: 2 + i for i in range(2 * n)},
        compiler_params=pltpu.CompilerParams(has_side_effects=_EFFECT),
    )(*[pltpu.with_memory_space_constraint(a, pltpu.HBM) for a in both], *extra)
    return (out[0], out[1], out[2:2 + 2 * n]), out[-1]


def exchange_wait(handle, kinds, after, name, with_srcs=False):
    send_sems, recv_sems, both = handle
    n = len(kinds)
    after = after if isinstance(after, tuple) else (after,)

    def body(*refs):
        src_refs, land_refs, s_sems, r_sems = refs[:n], refs[n:2 * n], refs[2 * n], refs[2 * n + 1]
        for cp in _shard_copies(src_refs, land_refs, kinds, s_sems, r_sems):
            cp.wait_send()
            cp.wait_recv()

    out = pl.pallas_call(
        body, name=name, out_shape=tuple(pltpu.HBM(a.shape, a.dtype) for a in both),
        in_specs=[_HBM] * (2 * n) + [_SEM, _SEM] + [pl.BlockSpec(memory_space=pl.ANY)] * len(after),
        out_specs=tuple([_HBM] * (2 * n)), input_output_aliases={i: i for i in range(2 * n)},
        compiler_params=pltpu.CompilerParams(has_side_effects=_EFFECT),
    )(*both, send_sems, recv_sems, *after)
    return (out[:n], out[n:]) if with_srcs else out[n:]


def final_exchange(hs, small):
    n = len(hs)
    S = small.shape[0]

    def body(*refs):
        h_refs, sm_ref, out_refs, smalls_ref = refs[:n], refs[n], refs[n + 1:2 * n + 1], refs[2 * n + 1]
        send_sems, recv_sems, local_sem = refs[2 * n + 2:]
        x, y, c = _place()
        my_slot = smalls_ref.at[4 * x + 2 * y + c]
        mine = pltpu.make_async_copy(sm_ref, my_slot, local_sem)
        mine.start()
        copies = [pltpu.make_async_remote_copy(src_ref=h_refs[a], dst_ref=out_refs[a], send_sem=send_sems.at[a],
                                               recv_sem=recv_sems.at[a], device_id=(x, y, 1 - c), device_id_type=MESH)
                  for a in range(n)]
        for mask in range(1, 8):
            fx, fy, fc = (mask >> 2) & 1, (mask >> 1) & 1, mask & 1
            peer = ((1 - x) if fx else x, (1 - y) if fy else y, (1 - c) if fc else c)
            copies.append(pltpu.make_async_remote_copy(
                src_ref=sm_ref, dst_ref=my_slot, send_sem=send_sems.at[n - 1 + mask], recv_sem=recv_sems.at[n - 1 + mask],
                device_id=peer, device_id_type=MESH))
        for cp in copies:
            cp.start()
        for cp in copies:
            cp.wait_recv()
        for cp in copies:
            cp.wait_send()
        mine.wait()

    sd = jax.ShapeDtypeStruct
    out = pl.pallas_call(
        body, name="final_exchange", in_specs=[_HBM] * (n + 1), out_specs=[_HBM] * (n + 1),
        out_shape=[sd(h.shape, h.dtype) for h in hs] + [sd((8, S, 128), f32)],
        scratch_shapes=[pltpu.SemaphoreType.DMA((n + 7,)), pltpu.SemaphoreType.DMA((n + 7,)),
                        pltpu.SemaphoreType.DMA],
    )(*hs, small)
    return out[:n], out[n]


N_CHIPS = 4
SHARD_COLS = N_ORIG // N_CHIPS


SHARD_PAD = 2688
_COL_SEGMENTS = (
    ((0, 2048, OFF_A),)
    + tuple((2048 + 512 * j + HK * h, 2048 + 512 * j + HK * (h + 1), OFF_B + REG_BH * h + HK * j)
            for j in range(3) for h in range(N_HGRN))
    + ((3584, 4096, OFF_M + M_BZ), (4096, 6144, OFF_C), (6144, 6160, OFF_C + 2048), (6160, 7184, OFF_M + M_CZ),
       (7184, N_ORIG, OFF_M + M_G)))


def _shard_pieces():
    pieces = []
    for lo, hi, dst in _COL_SEGMENTS:
        for p in range(N_CHIPS):
            a, b = max(lo, p * SHARD_COLS), min(hi, (p + 1) * SHARD_COLS)
            if a < b:
                pieces.append((p, a - p * SHARD_COLS, dst + a - lo, b - a))
    return pieces


def win_cast_pad(wt):
    tc = 128

    def body(x_ref, o0_ref, o1_ref):
        col = pl.program_id(0) * tc + lax.broadcasted_iota(jnp.int32, (tc, 1), 0)
        for l, o_ref in enumerate((o0_ref, o1_ref)):
            o_ref[...] = _b(jnp.where(col < SHARD_COLS, x_ref[:, l, :], 0.0).T)

    spec = pl.BlockSpec((D, tc), lambda i: (0, i))
    return pl.pallas_call(
        body, name="win_cast_pad", grid=(SHARD_PAD // tc,),
        in_specs=[pl.BlockSpec((tc, 2, D), lambda i: (i, 0, 0))], out_specs=[spec, spec],
        out_shape=[jax.ShapeDtypeStruct((D, SHARD_PAD), bf16)] * 2, compiler_params=_cp(("parallel",)),
    )(wt)


def win_to_padded(w4, name):
    tr = 256
    pieces = _shard_pieces()

    def body(a_ref, o_ref):
        o_ref[...] = jnp.zeros((tr, NP), bf16)
        for p, j0, c0, n in pieces:
            o_ref[:, c0:c0 + n] = a_ref[p, :, j0:j0 + n]

    return pl.pallas_call(
        body, name=name, grid=(D // tr,),
        in_specs=[pl.BlockSpec((N_CHIPS, tr, SHARD_PAD), lambda i: (0, i, 0))],
        out_specs=pl.BlockSpec((tr, NP), lambda i: (i, 0)),
        out_shape=jax.ShapeDtypeStruct((D, NP), bf16), compiler_params=_cp(("parallel",)),
    )(w4)


def win_from_padded(dw, chip):
    tr = 128
    pieces = _shard_pieces()

    def body(chip_ref, d_ref, of_ref, ob_ref):
        of_ref[:, SHARD_COLS:] = jnp.zeros((tr, SHARD_PAD - SHARD_COLS), f32)
        for p in range(N_CHIPS):
            ob_ref[p, :, SHARD_COLS:] = jnp.zeros((tr, SHARD_PAD - SHARD_COLS), bf16)
        for p, j0, c0, n in pieces:
            v = d_ref[:, c0:c0 + n]
            ob_ref[p, :, j0:j0 + n] = _b(v)

            @pl.when(chip_ref[0] == p)
            def _(v=v, j0=j0, n=n):
                of_ref[:, j0:j0 + n] = v

    return pl.pallas_call(
        body, name="win_from_padded",
        grid_spec=pltpu.PrefetchScalarGridSpec(
            num_scalar_prefetch=1, grid=(D // tr,),
            in_specs=[pl.BlockSpec((tr, NP), lambda i, chip: (i, 0))],
            out_specs=[pl.BlockSpec((tr, SHARD_PAD), lambda i, chip: (i, 0)),
                       pl.BlockSpec((N_CHIPS, tr, SHARD_PAD), lambda i, chip: (0, i, 0))]),
        out_shape=[jax.ShapeDtypeStruct((D, SHARD_PAD), f32),
                   jax.ShapeDtypeStruct((N_CHIPS, D, SHARD_PAD), bf16)],
        compiler_params=_cp(("arbitrary",)),
    )(chip, dw)


def _rows128(a):
    flat = a.reshape(-1)
    total = -(-flat.shape[0] // 1024) * 1024
    return jnp.pad(flat, (0, total - flat.shape[0])).reshape(total // 128, 128)


def _lb_rows(lb):
    return jnp.pad(lb.reshape(2, 4, 128), ((0, 0), (0, 4), (0, 0))).reshape(16, 128)


def _pack_small(v, with_conv):
    rows = []
    for name, n in _SMALL + (_CONV if with_conv else ()):
        if name == "lower_bounds":
            rows.append(_lb_rows(v[name]))
        elif name == "loss":
            rows.append(jnp.broadcast_to(v[name], (16, 128)) if name in v else jnp.zeros((16, 128), f32))
        else:
            rows.append(_rows128(v[name]))
    return jnp.concatenate(rows, axis=0)


def _unpack_small(p, shapes, with_conv):
    out, row = {}, 0
    for name, n in _SMALL + (_CONV if with_conv else ()):
        nrows = _small_rows(n)
        blk = p[row:row + nrows]
        if name == "lower_bounds":
            out[name] = blk.reshape(2, 8, 128)[:, :4].reshape(2, 512)
        elif name == "loss":
            out[name] = blk[0, 0]
        else:
            out[name] = blk.reshape(-1)[:n].reshape(shapes[name])
        row += nrows
    return out


def _lane_vec(a8):
    return jnp.pad(a8.reshape(1, 8), ((0, 0), (8, 112)))


WEIGHT_NAMES = ("norm_w", "w_in", "b_gate", "conv_a", "conv_c", "a_log", "dt_bias", "lower_bounds", "hgrn_norm_w",
                "gdn_norm_w", "w_out_a", "w_out_b", "w_out_c", "w_o", "final_norm_w")


def kernel(x, norm_w, w_in, b_gate, conv_a, conv_c, a_log, dt_bias, lower_bounds, hgrn_norm_w, gdn_norm_w, w_out_a, w_out_b, w_out_c, w_o, final_norm_w, loss_target, m_norm_w, m_w_in, m_b_gate, m_conv_a, m_conv_c, m_a_log, m_dt_bias, m_lower_bounds, m_hgrn_norm_w, m_gdn_norm_w, m_w_out_a, m_w_out_b, m_w_out_c, m_w_o, m_final_norm_w, v_norm_w, v_w_in, v_b_gate, v_conv_a, v_conv_c, v_a_log, v_dt_bias, v_lower_bounds, v_hgrn_norm_w, v_gdn_norm_w, v_w_out_a, v_w_out_b, v_w_out_c, v_w_o, v_final_norm_w):
    wts = dict(norm_w=norm_w, w_in=w_in, b_gate=b_gate, conv_a=conv_a, conv_c=conv_c, a_log=a_log, dt_bias=dt_bias,
               lower_bounds=lower_bounds, hgrn_norm_w=hgrn_norm_w, gdn_norm_w=gdn_norm_w, w_out_a=w_out_a,
               w_out_b=w_out_b, w_out_c=w_out_c, w_o=w_o, final_norm_w=final_norm_w)
    mom = dict(norm_w=m_norm_w, w_in=m_w_in, b_gate=m_b_gate, conv_a=m_conv_a, conv_c=m_conv_c, a_log=m_a_log,
               dt_bias=m_dt_bias, lower_bounds=m_lower_bounds, hgrn_norm_w=m_hgrn_norm_w, gdn_norm_w=m_gdn_norm_w,
               w_out_a=m_w_out_a, w_out_b=m_w_out_b, w_out_c=m_w_out_c, w_o=m_w_o, final_norm_w=m_final_norm_w)
    var = dict(norm_w=v_norm_w, w_in=v_w_in, b_gate=v_b_gate, conv_a=v_conv_a, conv_c=v_conv_c, a_log=v_a_log,
               dt_bias=v_dt_bias, lower_bounds=v_lower_bounds, hgrn_norm_w=v_hgrn_norm_w, gdn_norm_w=v_gdn_norm_w,
               w_out_a=v_w_out_a, w_out_b=v_w_out_b, w_out_c=v_w_out_c, w_o=v_w_o, final_norm_w=v_final_norm_w)
    chip = 2 * lax.axis_index("x") + lax.axis_index("y")
    chip1 = chip.reshape(1).astype(jnp.int32)

    win_l0, win_l1 = win_cast_pad(jnp.transpose(w_in, (2, 0, 1)))
    win4_l0, ca4, cc4 = weight_gather([win_l0.reshape(2, D // 2, SHARD_PAD), conv_a, conv_c])
    by_cols = lambda a: a.transpose(1, 2, 0, 3).reshape(a.shape[1], a.shape[2], N_CHIPS * a.shape[3])
    by_rows = lambda a: a.transpose(1, 0, 2, 3).reshape(a.shape[1], N_CHIPS * a.shape[2], a.shape[3])
    conv_a_full, conv_c_full = by_cols(ca4), by_cols(cc4)
    later = [win_l1, _b(w_out_a), _b(w_out_b), _b(w_out_c), _b(w_o)]
    sent_w, token = exchange_start(later, ["all"] * 5, "weights_start", after=win4_l0)

    def late_weights(after):
        lands = exchange_wait(sent_w, ["all"] * 5, after, "weights_wait")
        l1, woa4, wob4, woc4, wo4 = (lax.dynamic_update_index_in_dim(land, own, chip, 0)
                                     for land, own in zip(lands, later))
        outs = dict(w_out_a=by_cols(woa4), w_out_b=by_cols(wob4), w_out_c=by_rows(woc4), w_o=by_rows(wo4))
        layers[1].update(outs, w_in=win_to_padded(l1, "win_to_padded1"))
        return outs

    lbs = lbs_fwd(lower_bounds)
    layers = []
    for l in range(2):
        layers.append(dict(
            l=l, norm_w=norm_w[l:l + 1], b_gate=b_gate[l:l + 1], conv_a=conv_a_full[l], conv_c=conv_c_full[l],
            alog_l=_lane_vec(a_log[l]), dtb_l=_lane_vec(dt_bias[l]), lbs=lbs[l:l + 1],
            hgrn_norm_w=hgrn_norm_w[l:l + 1], gdn_norm_w=gdn_norm_w[l:l + 1]))
    layers[0].update(w_in=win_to_padded(win4_l0.reshape(N_CHIPS, D, SHARD_PAD), "win_to_padded0"), late=late_weights,
                     norm_w=norm_w[0:1] + token[0:1, 0:1])

    xs, saved = x[0], []
    for l in range(2):
        xs, s = layer_fwd(xs, layers[l])
        saved.append(s)
    loss_row, dx, dfw = loss_head(xs, final_norm_w.reshape(1, D), loss_target[0])
    lg, half = [None, None], [None, None]
    mat_names = ("w_in",) + tuple(n for n, _ in OUT_MATS)
    dx, lg[1], half[1] = layer_bwd(dx, layers[1], saved[1], chip1)
    mine1 = [half[1][n] for n in mat_names]
    sent_sib, token = exchange_start(mine1, ["sib"] * len(mine1), "sibling_start")
    layers[0]["hgrn_norm_w"] = layers[0]["hgrn_norm_w"] + token[0:1, 0:1]
    dx, lg[0], half[0] = layer_bwd(dx, layers[0], saved[0], chip1)
    mine1, theirs1 = exchange_wait(sent_sib, ["sib"] * len(mine1), dx, "sibling_wait", with_srcs=True)
    grad_x = dx[None]

    stack = lambda n: jnp.stack([lg[0][n], lg[1][n]], axis=0)
    gsmall = {n: stack(n) for n in ("norm_w", "b_gate", "hgrn_norm_w", "gdn_norm_w", "a_log", "dt_bias", "conv_a",
                                    "conv_c")}
    gsmall.update(lower_bounds=stack("lbs"), final_norm_w=dfw, loss=loss_row)
    mine0 = [half[0][n] for n in mat_names]
    sent_sib0, token0 = exchange_start(mine0, ["sib"] * len(mine0), "sibling_start0")
    _, smalls = final_exchange([], _pack_small(gsmall, True) + token0[0:1, 0:1])
    small_names = [n for n, _ in _SMALL if n != "loss"]
    pack = lambda v: _pack_small({n: v[n] for n in small_names}, False)
    sg, sd, smn, svn = small_update(smalls, pack(wts), pack(mom), pack(var))
    mine0, theirs0 = exchange_wait(sent_sib0, ["sib"] * len(mine0), sg, "sibling_wait0", with_srcs=True)

    out_g, out_d, out_m, out_v = {}, {}, {}, {}
    for i, n in enumerate(mat_names):
        h, hs = [mine0[i], mine1[i]], [theirs0[i], theirs1[i]]
        if n == "w_in":
            fwd, back = (lambda a: jnp.transpose(a, (2, 0, 1))), (lambda a: jnp.transpose(a, (1, 2, 0)))
            res = adam_pair_t(h, hs, fwd(wts[n]), fwd(mom[n]), fwd(var[n]), "adam_" + n)
            out_g[n], out_d[n], out_m[n], out_v[n] = (back(a) for a in res)
        else:
            out_g[n], out_d[n], out_m[n], out_v[n] = adam_pair(h, hs, wts[n], mom[n], var[n], "adam_" + n)
    shapes = {n: wts[n].shape for n in small_names}
    shapes.update(conv_a=(2, 3, 512), conv_c=(2, 4, 2048))
    for dst, src, conv in ((out_g, sg, True), (out_d, sd, False), (out_m, smn, False), (out_v, svn, False)):
        dst.update(_unpack_small(src, shapes, conv))
    loss = out_g.pop("loss")
    for n in ("conv_a", "conv_c"):
        width = wts[n].shape[2]
        g = lax.dynamic_slice_in_dim(out_g[n], chip * width, width, axis=2)
        two_d = lambda a: a.reshape(-1, width)
        d, mn, vn = adam(two_d(wts[n]), two_d(g), two_d(mom[n]), two_d(var[n]), "adam_" + n)
        out_g[n] = g
        out_d[n], out_m[n], out_v[n] = (a.reshape(wts[n].shape) for a in (d, mn, vn))
    return (loss, grad_x, *[out_g[n] for n in WEIGHT_NAMES], *[out_d[n] for n in WEIGHT_NAMES],
            *[out_m[n] for n in WEIGHT_NAMES], *[out_v[n] for n in WEIGHT_NAMES])
```
